```python
import math
import jax, jax.numpy as jnp
from jax import lax
import numpy as np


D_MODEL = 1024
BATCH = 8
SEQ = 8192
DEPTH = 1

MEM_LEN = 256
EPS = 1e-6
ROPE_BASE = 10000.0

RET_HEADS = 8
RET_QK_WIDTH = D_MODEL // 2
RET_V_WIDTH = D_MODEL
RET_DK = RET_QK_WIDTH // RET_HEADS
RET_DV = RET_V_WIDTH // RET_HEADS
CHUNK = 128

S5_WIDTH = D_MODEL
S5_GROUP = 16
S5_GROUPS = S5_WIDTH // S5_GROUP
S5_STATE = 64

D_MIX = RET_V_WIDTH + S5_WIDTH
IN_COLS = 2 * RET_QK_WIDTH + 2 * RET_V_WIDTH + 2 * S5_WIDTH
SPLITS = (RET_QK_WIDTH, 2 * RET_QK_WIDTH, 2 * RET_QK_WIDTH + RET_V_WIDTH,
          2 * RET_QK_WIDTH + 2 * RET_V_WIDTH, 2 * RET_QK_WIDTH + 2 * RET_V_WIDTH + S5_WIDTH)

XA_HEADS = 4
XA_DH = D_MODEL // XA_HEADS

kernel_name = 'hybrid_retention_s5_block'


def rms_norm(x, g):
    xf = x.astype(jnp.float32)
    y = xf * lax.rsqrt(jnp.mean(xf * xf, axis=-1, keepdims=True) + EPS)
    return (y * g.astype(jnp.float32)).astype(x.dtype)


def rotary(x, positions):
    half = x.shape[-1] // 2
    inv = ROPE_BASE ** (-jnp.arange(half, dtype=jnp.float32) / half)
    ang = positions.astype(jnp.float32)[:, :, None, None] * inv
    cos, sin = jnp.cos(ang), jnp.sin(ang)
    x1, x2 = x[..., :half], x[..., half:]
    return jnp.concatenate([x1 * cos - x2 * sin, x1 * sin + x2 * cos], axis=-1)


def retention(q, k, v, gn_g):
    B, L, H, DK = q.shape
    DV = v.shape[-1]
    nC = L // CHUNK
    q = q.astype(jnp.float32)
    k = k.astype(jnp.float32) * DK ** -0.5
    v = v.astype(jnp.float32)
    log_g = jnp.log1p(-jnp.exp2(-5.0 - jnp.arange(H, dtype=jnp.float32)))
    qc = q.reshape(B, nC, CHUNK, H, DK)
    kc = k.reshape(B, nC, CHUNK, H, DK)
    vc = v.reshape(B, nC, CHUNK, H, DV)
    j = jnp.arange(CHUNK, dtype=jnp.float32)
    diff = j[:, None] - j[None, :]
    decay = jnp.where(diff[None] >= 0.0,
                      jnp.exp(log_g[:, None, None] * jnp.maximum(diff, 0.0)[None]), 0.0)
    scores = jnp.einsum('bnihd,bnjhd->bnhij', qc, kc) * decay
    inner = jnp.einsum('bnhij,bnjhe->bnihe', scores, vc)
    k_w = jnp.exp(log_g[None, :] * (CHUNK - 1.0 - j)[:, None])
    kv = jnp.einsum('bnjhd,jh,bnjhe->bnhde', kc, k_w, vc)
    chunk_decay = jnp.exp(log_g * CHUNK)[None, :, None, None]

    def step(R, kv_n):
        return chunk_decay * R + kv_n, R

    _, R_prev = lax.scan(step, jnp.zeros((B, H, DK, DV), jnp.float32), jnp.moveaxis(kv, 1, 0))
    R_prev = jnp.moveaxis(R_prev, 0, 1)
    q_w = jnp.exp(log_g[None, :] * (j + 1.0)[:, None])
    cross = jnp.einsum('bnihd,ih,bnhde->bnihe', qc, q_w, R_prev)
    o = (inner + cross).reshape(B, L, H, DV)
    mu = jnp.mean(o, axis=-1, keepdims=True)
    var = jnp.mean(jnp.square(o - mu), axis=-1, keepdims=True)
    o = ((o - mu) * lax.rsqrt(var + EPS)).reshape(B, L, H * DV)
    return o * gn_g.astype(jnp.float32)


def s5_branch(u, a_re, a_im, log_dt, b_re, b_im, c_re, c_im, d, glu_w, glu_b):
    Bsz, L, W = u.shape
    f32 = jnp.float32
    uf = u.astype(f32)
    ug = uf.reshape(Bsz, L, S5_GROUPS, S5_GROUP)
    dt = jnp.exp(log_dt.astype(f32))[:, None]
    ar, ai = a_re.astype(f32), a_im.astype(f32)
    mag = jnp.exp(ar * dt)
    abar_re = mag * jnp.cos(ai * dt)
    abar_im = mag * jnp.sin(ai * dt)
    den = ar * ar + ai * ai
    nr, ni = abar_re - 1.0, abar_im
    f_re = (nr * ar + ni * ai) / den
    f_im = (ni * ar - nr * ai) / den
    br, bi = b_re.astype(f32), b_im.astype(f32)
    bb_re = f_re[..., None] * br - f_im[..., None] * bi
    bb_im = f_re[..., None] * bi + f_im[..., None] * br
    bu_re = jnp.einsum('blgp,gnp->blgn', ug, bb_re)
    bu_im = jnp.einsum('blgp,gnp->blgn', ug, bb_im)
    a_seq_re = jnp.broadcast_to(abar_re[None, None], (1, L, S5_GROUPS, S5_STATE))
    a_seq_im = jnp.broadcast_to(abar_im[None, None], (1, L, S5_GROUPS, S5_STATE))

    def combine(e1, e2):
        a1r, a1i, b1r, b1i = e1
        a2r, a2i, b2r, b2i = e2
        return (a2r * a1r - a2i * a1i,
                a2r * a1i + a2i * a1r,
                a2r * b1r - a2i * b1i + b2r,
                a2r * b1i + a2i * b1r + b2i)

    _, _, xr, xi = lax.associative_scan(combine, (a_seq_re, a_seq_im, bu_re, bu_im), axis=1)
    y = (jnp.einsum('blgn,gpn->blgp', xr, c_re.astype(f32))
         - jnp.einsum('blgn,gpn->blgp', xi, c_im.astype(f32)))
    y = y.reshape(Bsz, L, W) + d.astype(f32) * uf
    y = jax.nn.gelu(y)
    y = y * jax.nn.sigmoid(y @ glu_w.astype(f32) + glu_b.astype(f32))
    return y


def _fwd_setup_inputs(seed: int = 0) -> dict:
    key = jax.random.key(seed)
    ks = jax.random.split(key, 24)
    f32 = jnp.float32

    def nrm(k, shape, scale):
        return jax.random.normal(k, shape, f32) * scale

    Ld = DEPTH
    G, N, P = S5_GROUPS, S5_STATE, S5_GROUP
    x = nrm(ks[0], (BATCH, SEQ, D_MODEL), 1.0)
    mem = nrm(ks[1], (BATCH, MEM_LEN, D_MODEL), 1.0)
    positions = jnp.broadcast_to(jnp.arange(SEQ, dtype=jnp.int32)[None, :], (BATCH, SEQ))
    norm1_g = 1.0 + nrm(ks[2], (Ld, D_MODEL), 0.02)
    w_in = nrm(ks[3], (Ld, D_MODEL, IN_COLS), D_MODEL ** -0.5)
    ret_gn_g = 1.0 + nrm(ks[4], (Ld, RET_V_WIDTH), 0.02)
    n_idx = jnp.arange(N, dtype=f32)
    s5_a_re = -0.5 + nrm(ks[5], (Ld, G, N), 0.01)
    s5_a_im = math.pi * n_idx[None, None, :] + nrm(ks[6], (Ld, G, N), 0.01)
    s5_log_dt = jax.random.uniform(ks[7], (Ld, G), f32, math.log(1e-3), math.log(1e-1))
    s5_b_re = nrm(ks[8], (Ld, G, N, P), (2.0 * P) ** -0.5)
    s5_b_im = nrm(ks[9], (Ld, G, N, P), (2.0 * P) ** -0.5)
    s5_c_re = nrm(ks[10], (Ld, G, P, N), N ** -0.5)
    s5_c_im = nrm(ks[11], (Ld, G, P, N), N ** -0.5)
    s5_d = nrm(ks[12], (Ld, S5_WIDTH), 0.5)
    s5_glu_w = nrm(ks[13], (Ld, S5_WIDTH, S5_WIDTH), S5_WIDTH ** -0.5)
    s5_glu_b = nrm(ks[14], (Ld, S5_WIDTH), 0.01)
    w_out = nrm(ks[15], (Ld, D_MIX, D_MODEL), D_MIX ** -0.5)
    norm2_g = 1.0 + nrm(ks[16], (Ld, D_MODEL), 0.02)
    norm_mem_g = 1.0 + nrm(ks[17], (Ld, D_MODEL), 0.02)
    xa_wq = nrm(ks[18], (Ld, D_MODEL, D_MODEL), D_MODEL ** -0.5)
    xa_wk = nrm(ks[19], (Ld, D_MODEL, D_MODEL), D_MODEL ** -0.5)
    xa_wv = nrm(ks[20], (Ld, D_MODEL, D_MODEL), D_MODEL ** -0.5)
    xa_wo = nrm(ks[21], (Ld, D_MODEL, D_MODEL), D_MODEL ** -0.5)
    norm_f_g = 1.0 + nrm(ks[22], (D_MODEL,), 0.02)
    return {'x': x, 'mem': mem, 'positions': positions, 'norm1_g': norm1_g, 'w_in': w_in,
            'ret_gn_g': ret_gn_g, 's5_a_re': s5_a_re, 's5_a_im': s5_a_im, 's5_log_dt': s5_log_dt,
            's5_b_re': s5_b_re, 's5_b_im': s5_b_im, 's5_c_re': s5_c_re, 's5_c_im': s5_c_im,
            's5_d': s5_d, 's5_glu_w': s5_glu_w, 's5_glu_b': s5_glu_b, 'w_out': w_out,
            'norm2_g': norm2_g, 'norm_mem_g': norm_mem_g, 'xa_wq': xa_wq, 'xa_wk': xa_wk,
            'xa_wv': xa_wv, 'xa_wo': xa_wo, 'norm_f_g': norm_f_g}


def _fwd_reference(x, mem, positions, norm1_g, w_in, ret_gn_g, s5_a_re, s5_a_im, s5_log_dt,
              s5_b_re, s5_b_im, s5_c_re, s5_c_im, s5_d, s5_glu_w, s5_glu_b, w_out,
              norm2_g, norm_mem_g, xa_wq, xa_wk, xa_wv, xa_wo, norm_f_g):
    B, L, _ = x.shape
    M = mem.shape[1]
    for l in range(DEPTH):
        h = rms_norm(x, norm1_g[l])
        proj = h @ w_in[l]
        q, k, v, g_ret, u, g_s5 = jnp.split(proj, SPLITS, axis=-1)
        q = rotary(q.reshape(B, L, RET_HEADS, RET_DK), positions)
        k = rotary(k.reshape(B, L, RET_HEADS, RET_DK), positions)
        v = v.reshape(B, L, RET_HEADS, RET_DV)
        ret = retention(q, k, v, ret_gn_g[l]).astype(x.dtype) * jax.nn.silu(g_ret)
        ssm = s5_branch(u, s5_a_re[l], s5_a_im[l], s5_log_dt[l], s5_b_re[l], s5_b_im[l],
                        s5_c_re[l], s5_c_im[l], s5_d[l], s5_glu_w[l],
                        s5_glu_b[l]).astype(x.dtype) * jax.nn.silu(g_s5)
        x = x + jnp.concatenate([ret, ssm], axis=-1) @ w_out[l]
        h2 = rms_norm(x, norm2_g[l])
        m = rms_norm(mem, norm_mem_g[l])
        qa = (h2 @ xa_wq[l]).reshape(B, L, XA_HEADS, XA_DH)
        ka = (m @ xa_wk[l]).reshape(B, M, XA_HEADS, XA_DH)
        va = (m @ xa_wv[l]).reshape(B, M, XA_HEADS, XA_DH)
        s = jnp.einsum('blhd,bmhd->bhlm', qa, ka).astype(jnp.float32) * XA_DH ** -0.5
        p = jax.nn.softmax(s, axis=-1).astype(va.dtype)
        o = jnp.einsum('bhlm,bmhd->blhd', p, va).reshape(B, L, XA_HEADS * XA_DH)
        x = x + o @ xa_wo[l]
    return rms_norm(x, norm_f_g)


import jax as _jax
import jax.numpy as _jnp

TWIN_FORMAT = 'train_step'
FWD_PARAMS = ['x', 'mem', 'positions', 'norm1_g', 'w_in', 'ret_gn_g', 's5_a_re', 's5_a_im', 's5_log_dt', 's5_b_re', 's5_b_im', 's5_c_re', 's5_c_im', 's5_d', 's5_glu_w', 's5_glu_b', 'w_out', 'norm2_g', 'norm_mem_g', 'xa_wq', 'xa_wk', 'xa_wv', 'xa_wo', 'norm_f_g']
TWIN_WEIGHTS = ['norm1_g', 'w_in', 'ret_gn_g', 's5_a_re', 's5_a_im', 's5_log_dt', 's5_b_re', 's5_b_im', 's5_c_re', 's5_c_im', 's5_d', 's5_glu_w', 's5_glu_b', 'w_out', 'norm2_g', 'norm_mem_g', 'xa_wq', 'xa_wk', 'xa_wv', 'xa_wo', 'norm_f_g']
TWIN_DIFF_INPUT = 'x'
TWIN_INPUTS = ['x', 'mem', 'positions', 'norm1_g', 'w_in', 'ret_gn_g', 's5_a_re', 's5_a_im', 's5_log_dt', 's5_b_re', 's5_b_im', 's5_c_re', 's5_c_im', 's5_d', 's5_glu_w', 's5_glu_b', 'w_out', 'norm2_g', 'norm_mem_g', 'xa_wq', 'xa_wk', 'xa_wv', 'xa_wo', 'norm_f_g', 'loss_target', 'm_norm1_g', 'm_w_in', 'm_ret_gn_g', 'm_s5_a_re', 'm_s5_a_im', 'm_s5_log_dt', 'm_s5_b_re', 'm_s5_b_im', 'm_s5_c_re', 'm_s5_c_im', 'm_s5_d', 'm_s5_glu_w', 'm_s5_glu_b', 'm_w_out', 'm_norm2_g', 'm_norm_mem_g', 'm_xa_wq', 'm_xa_wk', 'm_xa_wv', 'm_xa_wo', 'm_norm_f_g', 'v_norm1_g', 'v_w_in', 'v_ret_gn_g', 'v_s5_a_re', 'v_s5_a_im', 'v_s5_log_dt', 'v_s5_b_re', 'v_s5_b_im', 'v_s5_c_re', 'v_s5_c_im', 'v_s5_d', 'v_s5_glu_w', 'v_s5_glu_b', 'v_w_out', 'v_norm2_g', 'v_norm_mem_g', 'v_xa_wq', 'v_xa_wk', 'v_xa_wv', 'v_xa_wo', 'v_norm_f_g']
TWIN_OUTPUTS = ['loss', 'grad_x', 'grad_norm1_g', 'grad_w_in', 'grad_ret_gn_g', 'grad_s5_a_re', 'grad_s5_a_im', 'grad_s5_log_dt', 'grad_s5_b_re', 'grad_s5_b_im', 'grad_s5_c_re', 'grad_s5_c_im', 'grad_s5_d', 'grad_s5_glu_w', 'grad_s5_glu_b', 'grad_w_out', 'grad_norm2_g', 'grad_norm_mem_g', 'grad_xa_wq', 'grad_xa_wk', 'grad_xa_wv', 'grad_xa_wo', 'grad_norm_f_g', 'delta_norm1_g', 'delta_w_in', 'delta_ret_gn_g', 'delta_s5_a_re', 'delta_s5_a_im', 'delta_s5_log_dt', 'delta_s5_b_re', 'delta_s5_b_im', 'delta_s5_c_re', 'delta_s5_c_im', 'delta_s5_d', 'delta_s5_glu_w', 'delta_s5_glu_b', 'delta_w_out', 'delta_norm2_g', 'delta_norm_mem_g', 'delta_xa_wq', 'delta_xa_wk', 'delta_xa_wv', 'delta_xa_wo', 'delta_norm_f_g', 'new_m_norm1_g', 'new_m_w_in', 'new_m_ret_gn_g', 'new_m_s5_a_re', 'new_m_s5_a_im', 'new_m_s5_log_dt', 'new_m_s5_b_re', 'new_m_s5_b_im', 'new_m_s5_c_re', 'new_m_s5_c_im', 'new_m_s5_d', 'new_m_s5_glu_w', 'new_m_s5_glu_b', 'new_m_w_out', 'new_m_norm2_g', 'new_m_norm_mem_g', 'new_m_xa_wq', 'new_m_xa_wk', 'new_m_xa_wv', 'new_m_xa_wo', 'new_m_norm_f_g', 'new_v_norm1_g', 'new_v_w_in', 'new_v_ret_gn_g', 'new_v_s5_a_re', 'new_v_s5_a_im', 'new_v_s5_log_dt', 'new_v_s5_b_re', 'new_v_s5_b_im', 'new_v_s5_c_re', 'new_v_s5_c_im', 'new_v_s5_d', 'new_v_s5_glu_w', 'new_v_s5_glu_b', 'new_v_w_out', 'new_v_norm2_g', 'new_v_norm_mem_g', 'new_v_xa_wq', 'new_v_xa_wk', 'new_v_xa_wv', 'new_v_xa_wo', 'new_v_norm_f_g']
TWIN_LEAF_KINDS = {'loss': 'loss', 'grad_x': 'grad_x', 'grad_norm1_g': 'grad_w', 'grad_w_in': 'grad_w', 'grad_ret_gn_g': 'grad_w', 'grad_s5_a_re': 'grad_w', 'grad_s5_a_im': 'grad_w', 'grad_s5_log_dt': 'grad_w', 'grad_s5_b_re': 'grad_w', 'grad_s5_b_im': 'grad_w', 'grad_s5_c_re': 'grad_w', 'grad_s5_c_im': 'grad_w', 'grad_s5_d': 'grad_w', 'grad_s5_glu_w': 'grad_w', 'grad_s5_glu_b': 'grad_w', 'grad_w_out': 'grad_w', 'grad_norm2_g': 'grad_w', 'grad_norm_mem_g': 'grad_w', 'grad_xa_wq': 'grad_w', 'grad_xa_wk': 'grad_w', 'grad_xa_wv': 'grad_w', 'grad_xa_wo': 'grad_w', 'grad_norm_f_g': 'grad_w', 'delta_norm1_g': 'delta_w', 'delta_w_in': 'delta_w', 'delta_ret_gn_g': 'delta_w', 'delta_s5_a_re': 'delta_w', 'delta_s5_a_im': 'delta_w', 'delta_s5_log_dt': 'delta_w', 'delta_s5_b_re': 'delta_w', 'delta_s5_b_im': 'delta_w', 'delta_s5_c_re': 'delta_w', 'delta_s5_c_im': 'delta_w', 'delta_s5_d': 'delta_w', 'delta_s5_glu_w': 'delta_w', 'delta_s5_glu_b': 'delta_w', 'delta_w_out': 'delta_w', 'delta_norm2_g': 'delta_w', 'delta_norm_mem_g': 'delta_w', 'delta_xa_wq': 'delta_w', 'delta_xa_wk': 'delta_w', 'delta_xa_wv': 'delta_w', 'delta_xa_wo': 'delta_w', 'delta_norm_f_g': 'delta_w', 'new_m_norm1_g': 'new_m', 'new_m_w_in': 'new_m', 'new_m_ret_gn_g': 'new_m', 'new_m_s5_a_re': 'new_m', 'new_m_s5_a_im': 'new_m', 'new_m_s5_log_dt': 'new_m', 'new_m_s5_b_re': 'new_m', 'new_m_s5_b_im': 'new_m', 'new_m_s5_c_re': 'new_m', 'new_m_s5_c_im': 'new_m', 'new_m_s5_d': 'new_m', 'new_m_s5_glu_w': 'new_m', 'new_m_s5_glu_b': 'new_m', 'new_m_w_out': 'new_m', 'new_m_norm2_g': 'new_m', 'new_m_norm_mem_g': 'new_m', 'new_m_xa_wq': 'new_m', 'new_m_xa_wk': 'new_m', 'new_m_xa_wv': 'new_m', 'new_m_xa_wo': 'new_m', 'new_m_norm_f_g': 'new_m', 'new_v_norm1_g': 'new_v', 'new_v_w_in': 'new_v', 'new_v_ret_gn_g': 'new_v', 'new_v_s5_a_re': 'new_v', 'new_v_s5_a_im': 'new_v', 'new_v_s5_log_dt': 'new_v', 'new_v_s5_b_re': 'new_v', 'new_v_s5_b_im': 'new_v', 'new_v_s5_c_re': 'new_v', 'new_v_s5_c_im': 'new_v', 'new_v_s5_d': 'new_v', 'new_v_s5_glu_w': 'new_v', 'new_v_s5_glu_b': 'new_v', 'new_v_w_out': 'new_v', 'new_v_norm2_g': 'new_v', 'new_v_norm_mem_g': 'new_v', 'new_v_xa_wq': 'new_v', 'new_v_xa_wk': 'new_v', 'new_v_xa_wv': 'new_v', 'new_v_xa_wo': 'new_v', 'new_v_norm_f_g': 'new_v'}


def _forward(args):
    return _fwd_reference(*[args[k] for k in FWD_PARAMS])


def _output_shape():
    def fwd():
        inp = _fwd_setup_inputs(0)
        return _fwd_reference(*[inp[k] for k in FWD_PARAMS])
    out = _jax.eval_shape(fwd)
    return out.shape, out.dtype

N_MICROBATCH = 1
ADAM_LR = 0.001
ADAM_B1 = 0.9
ADAM_B2 = 0.999
ADAM_EPS = 1e-08
ADAM_WD = 0.01
ADAM_STEP = 10
PER_EXAMPLE_BATCH_AXIS = {'x': 0, 'mem': 0, 'positions': 0, 'loss_target': 0}
SHARED_INPUTS = []
_WEIGHT_DTYPES = {'norm1_g': _jnp.float32, 'w_in': _jnp.float32, 'ret_gn_g': _jnp.float32, 's5_a_re': _jnp.float32, 's5_a_im': _jnp.float32, 's5_log_dt': _jnp.float32, 's5_b_re': _jnp.float32, 's5_b_im': _jnp.float32, 's5_c_re': _jnp.float32, 's5_c_im': _jnp.float32, 's5_d': _jnp.float32, 's5_glu_w': _jnp.float32, 's5_glu_b': _jnp.float32, 'w_out': _jnp.float32, 'norm2_g': _jnp.float32, 'norm_mem_g': _jnp.float32, 'xa_wq': _jnp.float32, 'xa_wk': _jnp.float32, 'xa_wv': _jnp.float32, 'xa_wo': _jnp.float32, 'norm_f_g': _jnp.float32}
MOMENT_SCALE = {'norm1_g': 2.121846e-01, 'w_in': 8.917513e-02, 'ret_gn_g': 1.017538e-01, 's5_a_re': 2.291363e-03, 's5_a_im': 2.238094e-03, 's5_log_dt': 1.311099e+00, 's5_b_re': 1.464403e-03, 's5_b_im': 1.476383e-03, 's5_c_re': 2.131400e-03, 's5_c_im': 2.112055e-03, 's5_d': 3.159737e-02, 's5_glu_w': 2.261509e-03, 's5_glu_b': 7.436544e-03, 'w_out': 9.847237e-02, 'norm2_g': 2.499867e-02, 'norm_mem_g': 3.558504e-02, 'xa_wq': 2.424749e-02, 'xa_wk': 2.412857e-02, 'xa_wv': 2.460441e-02, 'xa_wo': 2.434586e-02, 'norm_f_g': 6.395826e+01}


def _to_microbatches(a, axis):
    t = _jnp.moveaxis(a, axis, 0)
    t = t.reshape((N_MICROBATCH, t.shape[0] // N_MICROBATCH) + t.shape[1:])
    return _jnp.moveaxis(t, 1, axis + 1)


def setup_inputs(seed: int = 0) -> dict:
    inp = _fwd_setup_inputs(seed)
    key = _jax.random.fold_in(_jax.random.key(seed), 7919)
    shape, _ = _output_shape()
    out = dict(inp)
    out["loss_target"] = _jax.random.normal(_jax.random.fold_in(key, 0), shape, _jnp.float32)
    for i, name in enumerate(TWIN_WEIGHTS):
        w = inp[name].astype(_jnp.float32)
        if MOMENT_SCALE is None:
            s = _jnp.sqrt(_jnp.mean(_jnp.square(w)) + 1e-30)
        else:
            s = MOMENT_SCALE[name]
        km, kv = _jax.random.split(_jax.random.fold_in(key, i + 1))
        out[name] = w
        out["m_" + name] = s * _jax.random.normal(km, w.shape, _jnp.float32)
        out["v_" + name] = (s * s) * _jax.random.uniform(kv, w.shape, _jnp.float32, 0.5, 1.5)
    if N_MICROBATCH > 1:
        for name, axis in PER_EXAMPLE_BATCH_AXIS.items():
            out[name] = _to_microbatches(out[name], axis)
    return {'x': out['x'], 'mem': out['mem'], 'positions': out['positions'], 'norm1_g': out['norm1_g'], 'w_in': out['w_in'], 'ret_gn_g': out['ret_gn_g'], 's5_a_re': out['s5_a_re'], 's5_a_im': out['s5_a_im'], 's5_log_dt': out['s5_log_dt'], 's5_b_re': out['s5_b_re'], 's5_b_im': out['s5_b_im'], 's5_c_re': out['s5_c_re'], 's5_c_im': out['s5_c_im'], 's5_d': out['s5_d'], 's5_glu_w': out['s5_glu_w'], 's5_glu_b': out['s5_glu_b'], 'w_out': out['w_out'], 'norm2_g': out['norm2_g'], 'norm_mem_g': out['norm_mem_g'], 'xa_wq': out['xa_wq'], 'xa_wk': out['xa_wk'], 'xa_wv': out['xa_wv'], 'xa_wo': out['xa_wo'], 'norm_f_g': out['norm_f_g'], 'loss_target': out['loss_target'], 'm_norm1_g': out['m_norm1_g'], 'm_w_in': out['m_w_in'], 'm_ret_gn_g': out['m_ret_gn_g'], 'm_s5_a_re': out['m_s5_a_re'], 'm_s5_a_im': out['m_s5_a_im'], 'm_s5_log_dt': out['m_s5_log_dt'], 'm_s5_b_re': out['m_s5_b_re'], 'm_s5_b_im': out['m_s5_b_im'], 'm_s5_c_re': out['m_s5_c_re'], 'm_s5_c_im': out['m_s5_c_im'], 'm_s5_d': out['m_s5_d'], 'm_s5_glu_w': out['m_s5_glu_w'], 'm_s5_glu_b': out['m_s5_glu_b'], 'm_w_out': out['m_w_out'], 'm_norm2_g': out['m_norm2_g'], 'm_norm_mem_g': out['m_norm_mem_g'], 'm_xa_wq': out['m_xa_wq'], 'm_xa_wk': out['m_xa_wk'], 'm_xa_wv': out['m_xa_wv'], 'm_xa_wo': out['m_xa_wo'], 'm_norm_f_g': out['m_norm_f_g'], 'v_norm1_g': out['v_norm1_g'], 'v_w_in': out['v_w_in'], 'v_ret_gn_g': out['v_ret_gn_g'], 'v_s5_a_re': out['v_s5_a_re'], 'v_s5_a_im': out['v_s5_a_im'], 'v_s5_log_dt': out['v_s5_log_dt'], 'v_s5_b_re': out['v_s5_b_re'], 'v_s5_b_im': out['v_s5_b_im'], 'v_s5_c_re': out['v_s5_c_re'], 'v_s5_c_im': out['v_s5_c_im'], 'v_s5_d': out['v_s5_d'], 'v_s5_glu_w': out['v_s5_glu_w'], 'v_s5_glu_b': out['v_s5_glu_b'], 'v_w_out': out['v_w_out'], 'v_norm2_g': out['v_norm2_g'], 'v_norm_mem_g': out['v_norm_mem_g'], 'v_xa_wq': out['v_xa_wq'], 'v_xa_wk': out['v_xa_wk'], 'v_xa_wv': out['v_xa_wv'], 'v_xa_wo': out['v_xa_wo'], 'v_norm_f_g': out['v_norm_f_g']}


def _loss(weights, diff, rest, loss_target):
    with _jax.named_scope("forward"):
        args = {**rest, TWIN_DIFF_INPUT: diff, **{k: w.astype(_WEIGHT_DTYPES[k]) for k, w in weights.items()}}
        y = _forward(args)
    with _jax.named_scope("loss_head"):
        err = _jnp.square(y.astype(_jnp.float32) - loss_target)
        return 0.5 * _jnp.sum(_jnp.mean(err, axis=-1)) if err.ndim else 0.5 * err


def _adamw(w, g, m, v):
    m = ADAM_B1 * m + (1.0 - ADAM_B1) * g
    v = ADAM_B2 * v + (1.0 - ADAM_B2) * _jnp.square(g)
    m_hat = m / (1.0 - ADAM_B1 ** ADAM_STEP)
    v_hat = v / (1.0 - ADAM_B2 ** ADAM_STEP)
    delta = -ADAM_LR * (m_hat / (_jnp.sqrt(v_hat) + ADAM_EPS) + ADAM_WD * w)
    return delta, m, v


def reference(x, mem, positions, norm1_g, w_in, ret_gn_g, s5_a_re, s5_a_im, s5_log_dt, s5_b_re, s5_b_im, s5_c_re, s5_c_im, s5_d, s5_glu_w, s5_glu_b, w_out, norm2_g, norm_mem_g, xa_wq, xa_wk, xa_wv, xa_wo, norm_f_g, loss_target, m_norm1_g, m_w_in, m_ret_gn_g, m_s5_a_re, m_s5_a_im, m_s5_log_dt, m_s5_b_re, m_s5_b_im, m_s5_c_re, m_s5_c_im, m_s5_d, m_s5_glu_w, m_s5_glu_b, m_w_out, m_norm2_g, m_norm_mem_g, m_xa_wq, m_xa_wk, m_xa_wv, m_xa_wo, m_norm_f_g, v_norm1_g, v_w_in, v_ret_gn_g, v_s5_a_re, v_s5_a_im, v_s5_log_dt, v_s5_b_re, v_s5_b_im, v_s5_c_re, v_s5_c_im, v_s5_d, v_s5_glu_w, v_s5_glu_b, v_w_out, v_norm2_g, v_norm_mem_g, v_xa_wq, v_xa_wk, v_xa_wv, v_xa_wo, v_norm_f_g):
    given = dict(x=x, mem=mem, positions=positions, norm1_g=norm1_g, w_in=w_in, ret_gn_g=ret_gn_g, s5_a_re=s5_a_re, s5_a_im=s5_a_im, s5_log_dt=s5_log_dt, s5_b_re=s5_b_re, s5_b_im=s5_b_im, s5_c_re=s5_c_re, s5_c_im=s5_c_im, s5_d=s5_d, s5_glu_w=s5_glu_w, s5_glu_b=s5_glu_b, w_out=w_out, norm2_g=norm2_g, norm_mem_g=norm_mem_g, xa_wq=xa_wq, xa_wk=xa_wk, xa_wv=xa_wv, xa_wo=xa_wo, norm_f_g=norm_f_g, loss_target=loss_target, m_norm1_g=m_norm1_g, m_w_in=m_w_in, m_ret_gn_g=m_ret_gn_g, m_s5_a_re=m_s5_a_re, m_s5_a_im=m_s5_a_im, m_s5_log_dt=m_s5_log_dt, m_s5_b_re=m_s5_b_re, m_s5_b_im=m_s5_b_im, m_s5_c_re=m_s5_c_re, m_s5_c_im=m_s5_c_im, m_s5_d=m_s5_d, m_s5_glu_w=m_s5_glu_w, m_s5_glu_b=m_s5_glu_b, m_w_out=m_w_out, m_norm2_g=m_norm2_g, m_norm_mem_g=m_norm_mem_g, m_xa_wq=m_xa_wq, m_xa_wk=m_xa_wk, m_xa_wv=m_xa_wv, m_xa_wo=m_xa_wo, m_norm_f_g=m_norm_f_g, v_norm1_g=v_norm1_g, v_w_in=v_w_in, v_ret_gn_g=v_ret_gn_g, v_s5_a_re=v_s5_a_re, v_s5_a_im=v_s5_a_im, v_s5_log_dt=v_s5_log_dt, v_s5_b_re=v_s5_b_re, v_s5_b_im=v_s5_b_im, v_s5_c_re=v_s5_c_re, v_s5_c_im=v_s5_c_im, v_s5_d=v_s5_d, v_s5_glu_w=v_s5_glu_w, v_s5_glu_b=v_s5_glu_b, v_w_out=v_w_out, v_norm2_g=v_norm2_g, v_norm_mem_g=v_norm_mem_g, v_xa_wq=v_xa_wq, v_xa_wk=v_xa_wk, v_xa_wv=v_xa_wv, v_xa_wo=v_xa_wo, v_norm_f_g=v_norm_f_g)
    weights = {n: given[n] for n in TWIN_WEIGHTS}
    shared = {n: given[n] for n in SHARED_INPUTS}
    per_example = {n: given[n] for n in ['x', 'mem', 'positions']}
    grad_fn = _jax.value_and_grad(_loss, argnums=(0, 1))

    def one_microbatch(ex, loss_target):
        ex = dict(ex)
        diff = ex.pop(TWIN_DIFF_INPUT)
        return grad_fn(weights, diff, {**shared, **ex}, loss_target)

    if N_MICROBATCH == 1:
        loss, (grad_w, grad_x) = one_microbatch(per_example, given["loss_target"])
    else:
        def body(carry, xs):
            loss_sum, grad_sum = carry
            l_k, (gw_k, gx_k) = one_microbatch(xs[0], xs[1])
            with _jax.named_scope("update"):
                return (loss_sum + l_k, _jax.tree.map(_jnp.add, grad_sum, gw_k)), gx_k

        init = (_jnp.zeros((), _jnp.float32), _jax.tree.map(_jnp.zeros_like, weights))
        (loss, grad_w), grad_x = _jax.lax.scan(body, init, (per_example, given["loss_target"]))
    with _jax.named_scope("update"):
        delta_w, new_m, new_v = {}, {}, {}
        for n in TWIN_WEIGHTS:
            delta_w[n], new_m[n], new_v[n] = _adamw(weights[n], grad_w[n], given["m_" + n], given["v_" + n])
    return (loss, grad_x, *[grad_w[n] for n in TWIN_WEIGHTS], *[delta_w[n] for n in TWIN_WEIGHTS],
            *[new_m[n] for n in TWIN_WEIGHTS], *[new_v[n] for n in TWIN_WEIGHTS])
```

```python
import functools

import jax
import jax.numpy as jnp
from jax import lax
from jax.experimental import pallas as pl
from jax.experimental.pallas import tpu as pltpu

F32 = jnp.float32
BF16 = jnp.bfloat16
MESH = pl.DeviceIdType.MESH

D_MODEL = 1024
RET_HEADS, RET_DK, RET_DV = 8, 64, 128
RET_QK = RET_HEADS * RET_DK
S5_G, S5_N, S5_P = 64, 64, 16
S5_NB = 8
S5_GB = S5_G // S5_NB
S5_BS = S5_GB * S5_N
S5_COLS = 2 * S5_G * S5_N
XA_HEADS, XA_DH = 4, 256
EPS = 1e-6
ROPE_BASE = 10000.0
N_DEV = 8
W_IN_SHARD = 640
ROW_SHARDS = (128, 256, 128, 128, 128, 128)
ROWPACK = sum(ROW_SHARDS)
SMALL_ROWS = 280
ADAM_LR, ADAM_B1, ADAM_B2, ADAM_EPS, ADAM_WD, ADAM_STEP = 0.001, 0.9, 0.999, 1e-08, 0.01, 10

VMEM_LIMIT = 56 * 1024 * 1024


def _cp(*sem):
    return pltpu.CompilerParams(dimension_semantics=tuple(sem), vmem_limit_bytes=VMEM_LIMIT)


def _dot(a, b):
    return jnp.dot(a, b, preferred_element_type=F32)


def _dot_nt(a, b):
    return lax.dot_general(a, b, (((1,), (1,)), ((), ())), preferred_element_type=F32)


def _dot_tn(a, b):
    return lax.dot_general(a, b, (((0,), (0,)), ((), ())), preferred_element_type=F32)


def _sigmoid(x):
    return 1.0 / (1.0 + jnp.exp(-x))


def _silu(x):
    return x * _sigmoid(x)


def _dsilu(x):
    s = _sigmoid(x)
    return s * (1.0 + x * (1.0 - s))


_GELU_C = 0.7978845608028654


def _gelu(x):
    return 0.5 * x * (1.0 + jnp.tanh(_GELU_C * (x + 0.044715 * (x * x * x))))


def _dgelu(x):
    t = jnp.tanh(_GELU_C * (x + 0.044715 * (x * x * x)))
    return 0.5 * (1.0 + t) + 0.5 * x * (1.0 - t * t) * (_GELU_C * (1.0 + 3.0 * 0.044715 * (x * x)))


def _pick(n, cands):
    for c in cands:
        if n % c == 0:
            return c
    return n


def _mm_core(name, operands, in_specs, out_spec, out_shape, grid, nk, dims, acc_shape, has_res):
    def body(*refs):
        if has_res:
            a_ref, b_ref, r_ref, o_ref, acc = refs
        else:
            a_ref, b_ref, o_ref, acc = refs
        k = pl.program_id(2)

        @pl.when(k == 0)
        def _():
            acc[...] = jnp.zeros_like(acc)

        acc[...] += lax.dot_general(a_ref[...].astype(BF16), b_ref[...].astype(BF16), (dims, ((), ())),
                                    preferred_element_type=F32)

        @pl.when(k == nk - 1)
        def _():
            r = acc[...]
            if has_res:
                r = r + r_ref[...]
            o_ref[...] = r.astype(o_ref.dtype)

    return pl.pallas_call(
        body, name=name, grid=grid, in_specs=in_specs, out_specs=out_spec, out_shape=out_shape,
        scratch_shapes=[pltpu.VMEM(acc_shape, F32)],
        compiler_params=_cp("parallel", "parallel", "arbitrary"),
    )(*operands)


def _mm_nn(name, a, b, out_dtype, residual=None):
    m, kk = a.shape
    n = b.shape[1]
    tm, tn, tk = _pick(m, (1024, 512, 256)), _pick(n, (1024, 512)), _pick(kk, (1024, 512))
    ops = [a, b]
    specs = [pl.BlockSpec((tm, tk), lambda i, j, k: (i, k)), pl.BlockSpec((tk, tn), lambda i, j, k: (k, j))]
    if residual is not None:
        ops.append(residual)
        specs.append(pl.BlockSpec((tm, tn), lambda i, j, k: (i, j)))
    return _mm_core(name, ops, specs, pl.BlockSpec((tm, tn), lambda i, j, k: (i, j)),
                    jax.ShapeDtypeStruct((m, n), out_dtype), (m // tm, n // tn, kk // tk), kk // tk,
                    ((1,), (0,)), (tm, tn), residual is not None)


def _mm_nt(name, a, b, out_dtype, residual=None):
    m, kk = a.shape
    n = b.shape[0]
    tm, tn, tk = _pick(m, (1024, 512, 256)), _pick(n, (1024, 512)), _pick(kk, (1024, 512))
    ops = [a, b]
    specs = [pl.BlockSpec((tm, tk), lambda i, j, k: (i, k)), pl.BlockSpec((tn, tk), lambda i, j, k: (j, k))]
    if residual is not None:
        ops.append(residual)
        specs.append(pl.BlockSpec((tm, tn), lambda i, j, k: (i, j)))
    return _mm_core(name, ops, specs, pl.BlockSpec((tm, tn), lambda i, j, k: (i, j)),
                    jax.ShapeDtypeStruct((m, n), out_dtype), (m // tm, n // tn, kk // tk), kk // tk,
                    ((1,), (1,)), (tm, tn), residual is not None)


def _mm_tn(name, a, b, out_dtype):
    kk, m = a.shape
    n = b.shape[1]
    tm, tn, tk = _pick(m, (1024, 512)), _pick(n, (1024, 512)), _pick(kk, (1024, 512, 256))
    specs = [pl.BlockSpec((tk, tm), lambda i, j, k: (k, i)), pl.BlockSpec((tk, tn), lambda i, j, k: (k, j))]
    return _mm_core(name, [a, b], specs, pl.BlockSpec((tm, tn), lambda i, j, k: (i, j)),
                    jax.ShapeDtypeStruct((m, n), out_dtype), (m // tm, n // tn, kk // tk), kk // tk,
                    ((0,), (0,)), (tm, tn), False)


def _mm_nn_slots(name, a, b_slots, out_dtype):
    m, kk = a.shape
    s, _, ns = b_slots.shape
    tm, tk = _pick(m, (1024, 512, 256)), _pick(kk, (1024, 512))
    specs = [pl.BlockSpec((tm, tk), lambda i, j, k: (i, k)), pl.BlockSpec((None, tk, ns), lambda i, j, k: (j, k, 0))]
    return _mm_core(name, [a, b_slots], specs, pl.BlockSpec((tm, ns), lambda i, j, k: (i, j)),
                    jax.ShapeDtypeStruct((m, s * ns), out_dtype), (m // tm, s, kk // tk), kk // tk,
                    ((1,), (0,)), (tm, ns), False)


def _mm_nt_slots(name, a, b_slots, out_dtype):
    m = a.shape[0]
    s, n, ns = b_slots.shape
    tm, tn = _pick(m, (1024, 512, 256)), _pick(n, (1024, 512))
    specs = [pl.BlockSpec((tm, ns), lambda i, j, k: (i, k)), pl.BlockSpec((None, tn, ns), lambda i, j, k: (k, j, 0))]
    return _mm_core(name, [a, b_slots], specs, pl.BlockSpec((tm, tn), lambda i, j, k: (i, j)),
                    jax.ShapeDtypeStruct((m, n), out_dtype), (m // tm, n // tn, s), s,
                    ((1,), (1,)), (tm, tn), False)


def _mm_tn_slots(name, a, b, s, out_dtype):
    kk, m = a.shape
    ns = b.shape[1] // s
    tm, tk = _pick(m, (1024, 512)), _pick(kk, (1024, 512, 256))
    specs = [pl.BlockSpec((tk, tm), lambda i, j, k: (k, i)), pl.BlockSpec((tk, ns), lambda i, j, k: (k, j))]
    return _mm_core(name, [a, b], specs, pl.BlockSpec((None, tm, ns), lambda i, j, k: (j, i, 0)),
                    jax.ShapeDtypeStruct((s, m, ns), out_dtype), (m // tm, s, kk // tk), kk // tk,
                    ((0,), (0,)), (tm, ns), False)


def _rms_fwd(name, x, g):
    r, d = x.shape
    tr = _pick(r, (1024, 512, 256))

    def body(x_ref, g_ref, o_ref):
        xv = x_ref[...]
        rs = lax.rsqrt(jnp.mean(xv * xv, axis=-1, keepdims=True) + EPS)
        o_ref[...] = (xv * rs * g_ref[...]).astype(o_ref.dtype)

    return pl.pallas_call(
        body, name=name, grid=(r // tr,),
        in_specs=[pl.BlockSpec((tr, d), lambda i: (i, 0)), pl.BlockSpec((1, d), lambda i: (0, 0))],
        out_specs=pl.BlockSpec((tr, d), lambda i: (i, 0)),
        out_shape=jax.ShapeDtypeStruct((r, d), BF16), compiler_params=_cp("parallel"),
    )(x, g)


def _rms_bwd(name, x, g, dh, dres):
    r, d = x.shape
    tr = _pick(r, (512, 256))
    has_res = dres is not None

    def body(*refs):
        if has_res:
            x_ref, g_ref, dh_ref, dr_ref, dx_ref, dg_ref = refs
        else:
            x_ref, g_ref, dh_ref, dx_ref, dg_ref = refs
        i = pl.program_id(0)

        @pl.when(i == 0)
        def _():
            dg_ref[...] = jnp.zeros_like(dg_ref)

        xv = x_ref[...]
        dhv = dh_ref[...].astype(F32)
        rs = lax.rsqrt(jnp.mean(xv * xv, axis=-1, keepdims=True) + EPS)
        xn = xv * rs
        dg_ref[...] += jnp.sum(dhv * xn, axis=0, keepdims=True)
        dn = dhv * g_ref[...]
        dx = rs * (dn - xn * jnp.mean(dn * xn, axis=-1, keepdims=True))
        if has_res:
            dx = dx + dr_ref[...]
        dx_ref[...] = dx

    row = pl.BlockSpec((tr, d), lambda i: (i, 0))
    vec = pl.BlockSpec((1, d), lambda i: (0, 0))
    ops = [x, g, dh] + ([dres] if has_res else [])
    return pl.pallas_call(
        body, name=name, grid=(r // tr,),
        in_specs=[row, vec, row] + ([row] if has_res else []),
        out_specs=(row, vec),
        out_shape=(jax.ShapeDtypeStruct((r, d), F32), jax.ShapeDtypeStruct((1, d), F32)),
        compiler_params=_cp("arbitrary"),
    )(*ops)


def _loss_head(x2, gf, target):
    r, d = x2.shape
    tr = _pick(r, (512, 256))

    def body(x_ref, g_ref, t_ref, dx_ref, dg_ref, ls_ref):
        i = pl.program_id(0)

        @pl.when(i == 0)
        def _():
            dg_ref[...] = jnp.zeros_like(dg_ref)
            ls_ref[...] = jnp.zeros_like(ls_ref)

        xv = x_ref[...]
        rs = lax.rsqrt(jnp.mean(xv * xv, axis=-1, keepdims=True) + EPS)
        xn = xv * rs
        e = xn * g_ref[...] - t_ref[...]
        ls_ref[...] += jnp.sum(e * e, axis=0, keepdims=True)
        dy = e * (1.0 / d)
        dg_ref[...] += jnp.sum(dy * xn, axis=0, keepdims=True)
        dn = dy * g_ref[...]
        dx_ref[...] = rs * (dn - xn * jnp.mean(dn * xn, axis=-1, keepdims=True))

    row = pl.BlockSpec((tr, d), lambda i: (i, 0))
    vec = pl.BlockSpec((1, d), lambda i: (0, 0))
    return pl.pallas_call(
        body, name="loss_head", grid=(r // tr,), in_specs=[row, vec, row], out_specs=(row, vec, vec),
        out_shape=(jax.ShapeDtypeStruct((r, d), F32), jax.ShapeDtypeStruct((1, d), F32),
                   jax.ShapeDtypeStruct((1, d), F32)),
        compiler_params=_cp("arbitrary"),
    )(x2, gf, target)


def _rope_tables(pos_col, inv_row):
    l = pos_col.shape[0]
    tl = _pick(l, (1024, 512, 256))

    def body(p_ref, inv_ref, cos_ref, sin_ref):
        ang = p_ref[...].astype(F32) * inv_ref[...]
        lane = lax.broadcasted_iota(jnp.int32, ang.shape, 1)
        c = jnp.cos(ang)
        s = jnp.where((lane % RET_DK) < RET_DK // 2, -jnp.sin(ang), jnp.sin(ang))
        cos_ref[...] = jnp.tile(c, (1, RET_QK // 128))
        sin_ref[...] = jnp.tile(s, (1, RET_QK // 128))

    return pl.pallas_call(
        body, name="rope_tables", grid=(l // tl,),
        in_specs=[pl.BlockSpec((tl, 1), lambda i: (i, 0)), pl.BlockSpec((1, 128), lambda i: (0, 0))],
        out_specs=(pl.BlockSpec((tl, RET_QK), lambda i: (i, 0)), pl.BlockSpec((tl, RET_QK), lambda i: (i, 0))),
        out_shape=(jax.ShapeDtypeStruct((l, RET_QK), F32), jax.ShapeDtypeStruct((l, RET_QK), F32)),
        compiler_params=_cp("parallel"),
    )(pos_col, inv_row)


def _rot(x, cos_t, sin_t):
    n = x.shape[-1]
    lane = lax.broadcasted_iota(jnp.int32, x.shape, 1)
    partner = jnp.where((lane % RET_DK) < RET_DK // 2, pltpu.roll(x, n - RET_DK // 2, 1), pltpu.roll(x, RET_DK // 2, 1))
    return x * cos_t + partner * sin_t


def _ret_constants(c):
    log_g = jnp.log1p(-jnp.exp2(-5.0 - jnp.arange(RET_HEADS, dtype=F32)))
    j = jnp.arange(c, dtype=F32)
    diff = j[:, None] - j[None, :]
    decay = jnp.where(diff[None] >= 0.0, jnp.exp(log_g[:, None, None] * jnp.maximum(diff, 0.0)[None]), 0.0)
    q_w = jnp.exp(log_g[None, :] * (j + 1.0)[:, None])
    k_w = jnp.exp(log_g[None, :] * (c - 1.0 - j)[:, None])
    cd = jnp.exp(log_g * c)
    rep = lambda t: jnp.repeat(t, RET_DK, axis=1)
    cd_row = jnp.repeat(cd, RET_DV)[None, :]
    return decay, rep(q_w), rep(k_w), cd_row


def _ret_fwd(proj, cos_t, sin_t, consts, gn_g, c):
    l = proj.shape[0]
    nc = l // c
    decay, qw, kw, cd_row = consts

    def body(q_ref, k_ref, v_ref, g_ref, cos_ref, sin_ref, dec_ref, qw_ref, kw_ref, cd_ref, gn_ref,
             ret_ref, o_ref, rp_ref, state):
        @pl.when(pl.program_id(0) == 0)
        def _():
            state[...] = jnp.zeros_like(state)

        cs, sn = cos_ref[...], sin_ref[...]
        qr = _rot(q_ref[...], cs, sn)
        kr = _rot(k_ref[...], cs, sn) * (RET_DK ** -0.5)
        qb, kb = qr.astype(BF16), kr.astype(BF16)
        qwb = (qr * qw_ref[...]).astype(BF16)
        kwb = (kr * kw_ref[...]).astype(BF16)
        vb = v_ref[...].astype(BF16)
        for h in range(RET_HEADS):
            qs = slice(h * RET_DK, (h + 1) * RET_DK)
            vs = slice(h * RET_DV, (h + 1) * RET_DV)
            s = _dot_nt(qb[:, qs], kb[:, qs]) * dec_ref[h]
            r_prev = state[h]
            rp_ref[0, h] = r_prev
            o = _dot(s.astype(BF16), vb[:, vs]) + _dot(qwb[:, qs], r_prev.astype(BF16))
            state[h] = cd_ref[:, vs] * r_prev + _dot_tn(kwb[:, qs], vb[:, vs])
            o_ref[:, vs] = o
            mu = jnp.mean(o, axis=-1, keepdims=True)
            var = jnp.mean(jnp.square(o - mu), axis=-1, keepdims=True)
            on = (o - mu) * lax.rsqrt(var + EPS)
            ret_ref[:, vs] = (on * gn_ref[:, vs] * _silu(g_ref[:, vs])).astype(ret_ref.dtype)

    const2 = lambda shape: pl.BlockSpec(shape, lambda i: (0,) * len(shape))
    return pl.pallas_call(
        body, name="retention_fwd", grid=(nc,),
        in_specs=[pl.BlockSpec((c, RET_QK), lambda i: (i, 0)), pl.BlockSpec((c, RET_QK), lambda i: (i, 1)),
                  pl.BlockSpec((c, D_MODEL), lambda i: (i, 1)), pl.BlockSpec((c, D_MODEL), lambda i: (i, 2)),
                  pl.BlockSpec((c, RET_QK), lambda i: (i, 0)), pl.BlockSpec((c, RET_QK), lambda i: (i, 0)),
                  const2((RET_HEADS, c, c)), const2((c, RET_QK)), const2((c, RET_QK)), const2((1, D_MODEL)),
                  const2((1, D_MODEL))],
        out_specs=(pl.BlockSpec((c, D_MODEL), lambda i: (i, 0)), pl.BlockSpec((c, D_MODEL), lambda i: (i, 0)),
                   pl.BlockSpec((1, RET_HEADS, RET_DK, RET_DV), lambda i: (i, 0, 0, 0))),
        out_shape=(jax.ShapeDtypeStruct((l, D_MODEL), BF16), jax.ShapeDtypeStruct((l, D_MODEL), F32),
                   jax.ShapeDtypeStruct((nc, RET_HEADS, RET_DK, RET_DV), F32)),
        scratch_shapes=[pltpu.VMEM((RET_HEADS, RET_DK, RET_DV), F32)],
        compiler_params=_cp("arbitrary"),
    )(proj, proj, proj, proj, cos_t, sin_t, decay, qw, kw, cd_row, gn_g)


def _ret_bwd(proj, cos_t, sin_t, consts, gn_g, o_saved, r_prev_saved, dmix, c):
    l = proj.shape[0]
    nc = l // c
    decay, qw, kw, cd_row = consts

    def body(q_ref, k_ref, v_ref, g_ref, cos_ref, sin_ref, dec_ref, qw_ref, kw_ref, cd_ref, gn_ref, o_ref, rp_ref,
             dr_ref, out_ref, dgn_ref, state, dq_s, dk_s):
        @pl.when(pl.program_id(0) == 0)
        def _():
            state[...] = jnp.zeros_like(state)
            dgn_ref[...] = jnp.zeros_like(dgn_ref)

        cs, sn = cos_ref[...], sin_ref[...]
        qr = _rot(q_ref[...], cs, sn)
        kr = _rot(k_ref[...], cs, sn) * (RET_DK ** -0.5)
        qb, kb = qr.astype(BF16), kr.astype(BF16)
        qwb = (qr * qw_ref[...]).astype(BF16)
        kwv = kw_ref[...]
        kwb = (kr * kwv).astype(BF16)
        qwv = qw_ref[...]
        vb = v_ref[...].astype(BF16)
        for h in range(RET_HEADS):
            qs = slice(h * RET_DK, (h + 1) * RET_DK)
            vs = slice(h * RET_DV, (h + 1) * RET_DV)
            dec = dec_ref[h]
            o = o_ref[:, vs]
            mu = jnp.mean(o, axis=-1, keepdims=True)
            var = jnp.mean(jnp.square(o - mu), axis=-1, keepdims=True)
            rstd = lax.rsqrt(var + EPS)
            on = (o - mu) * rstd
            gate = g_ref[:, vs]
            sg = _silu(gate)
            dret = dr_ref[:, vs]
            gn = gn_ref[:, vs]
            dgn_ref[:, vs] += jnp.sum(dret * on * sg, axis=0, keepdims=True)
            out_ref[:, 2 * RET_QK + D_MODEL + h * RET_DV:2 * RET_QK + D_MODEL + (h + 1) * RET_DV] = (
                dret * on * gn * _dsilu(gate)).astype(out_ref.dtype)
            don = dret * gn * sg
            do = rstd * (don - jnp.mean(don, axis=-1, keepdims=True)
                         - on * jnp.mean(don * on, axis=-1, keepdims=True))
            dob = do.astype(BF16)
            sn_h = state[h]
            snb = sn_h.astype(BF16)
            s = _dot_nt(qb[:, qs], kb[:, qs]) * dec
            dv = _dot_tn(s.astype(BF16), dob) + _dot(kwb[:, qs], snb)
            out_ref[:, 2 * RET_QK + h * RET_DV:2 * RET_QK + (h + 1) * RET_DV] = dv.astype(out_ref.dtype)
            ds = (_dot_nt(dob, vb[:, vs]) * dec).astype(BF16)
            dq_s[:, qs] = _dot(ds, kb[:, qs]) + qwv[:, qs] * _dot_nt(dob, rp_ref[0, h].astype(BF16))
            dk_s[:, qs] = _dot_tn(ds, qb[:, qs]) + kwv[:, qs] * _dot_nt(vb[:, vs], snb)
            state[h] = cd_ref[:, vs] * sn_h + _dot_tn(qwb[:, qs], dob)
        out_ref[:, 0:RET_QK] = _rot(dq_s[...], cs, -sn).astype(out_ref.dtype)
        out_ref[:, RET_QK:2 * RET_QK] = (_rot(dk_s[...], cs, -sn) * (RET_DK ** -0.5)).astype(out_ref.dtype)

    rev = lambda i: nc - 1 - i
    const2 = lambda shape: pl.BlockSpec(shape, lambda i: (0,) * len(shape))
    return pl.pallas_call(
        body, name="retention_bwd", grid=(nc,),
        in_specs=[pl.BlockSpec((c, RET_QK), lambda i: (rev(i), 0)), pl.BlockSpec((c, RET_QK), lambda i: (rev(i), 1)),
                  pl.BlockSpec((c, D_MODEL), lambda i: (rev(i), 1)), pl.BlockSpec((c, D_MODEL), lambda i: (rev(i), 2)),
                  pl.BlockSpec((c, RET_QK), lambda i: (rev(i), 0)), pl.BlockSpec((c, RET_QK), lambda i: (rev(i), 0)),
                  const2((RET_HEADS, c, c)), const2((c, RET_QK)), const2((c, RET_QK)), const2((1, D_MODEL)),
                  const2((1, D_MODEL)),
                  pl.BlockSpec((c, D_MODEL), lambda i: (rev(i), 0)),
                  pl.BlockSpec((1, RET_HEADS, RET_DK, RET_DV), lambda i: (rev(i), 0, 0, 0)),
                  pl.BlockSpec((c, D_MODEL), lambda i: (rev(i), 0))],
        out_specs=(pl.BlockSpec((c, 2 * RET_QK + 2 * D_MODEL), lambda i: (rev(i), 0)), const2((1, D_MODEL))),
        out_shape=(jax.ShapeDtypeStruct((l, 2 * RET_QK + 2 * D_MODEL), BF16), jax.ShapeDtypeStruct((1, D_MODEL), F32)),
        scratch_shapes=[pltpu.VMEM((RET_HEADS, RET_DK, RET_DV), F32), pltpu.VMEM((c, RET_QK), F32),
                        pltpu.VMEM((c, RET_QK), F32)],
        compiler_params=_cp("arbitrary"),
    )(proj, proj, proj, proj, cos_t, sin_t, decay, qw, kw, cd_row, gn_g, o_saved, r_prev_saved, dmix)


def _s5_discretize(a_re, a_im, log_dt, b_re, b_im):
    dt = jnp.exp(log_dt)[:, None]
    mag = jnp.exp(a_re * dt)
    abar_re = mag * jnp.cos(a_im * dt)
    abar_im = mag * jnp.sin(a_im * dt)
    den = a_re * a_re + a_im * a_im
    nr, ni = abar_re - 1.0, abar_im
    f_re = (nr * a_re + ni * a_im) / den
    f_im = (ni * a_re - nr * a_im) / den
    bb_re = f_re[..., None] * b_re - f_im[..., None] * b_im
    bb_im = f_re[..., None] * b_im + f_im[..., None] * b_re
    return abar_re, abar_im, bb_re, bb_im


def _s5_cols(re, im):
    return jnp.stack([re.reshape(S5_NB, S5_BS), im.reshape(S5_NB, S5_BS)], axis=1).reshape(S5_COLS)


def _s5_uncols(v):
    t = v.reshape(S5_NB, 2, S5_BS)
    return t[:, 0].reshape(S5_G, S5_N), t[:, 1].reshape(S5_G, S5_N)


def _s5_scan_tables(abar_re, abar_im):
    def cmul(x, y):
        return x[0] * y[0] - x[1] * y[1], x[0] * y[1] + x[1] * y[0]

    row = jnp.arange(8)[:, None]
    out = []
    for conj in (False, True):
        a1 = (abar_re, -abar_im if conj else abar_im)
        a2 = cmul(a1, a1)
        a4 = cmul(a2, a2)
        pw = [a1]
        for _ in range(7):
            pw.append(cmul(pw[-1], a1))
        tabs = []
        for k, ap in zip((1, 2, 4), (a1, a2, a4)):
            cols = _s5_cols(*ap)[None, :]
            mask = (row <= 7 - k) if conj else (row >= k)
            tabs.append(jnp.where(mask, cols, 0.0))
        order = [pw[7 - i] for i in range(8)] if conj else pw
        tabs.append(jnp.stack([_s5_cols(*p) for p in order], axis=0))
        out.append(jnp.stack(tabs, axis=0))
    return jnp.stack(out, axis=0)


def _s5_block_mats(bb_re, bb_im, c_re, c_im):
    eye = jnp.eye(S5_GB, dtype=F32)
    bb = jnp.stack([bb_re, bb_im], axis=0).reshape(2, S5_NB, S5_GB, S5_N, S5_P)
    bbm = jnp.einsum("rbgnp,gh->bgprhn", bb, eye).reshape(S5_NB, S5_GB * S5_P, 2 * S5_BS)
    cc = jnp.stack([c_re, -c_im], axis=0).reshape(2, S5_NB, S5_GB, S5_P, S5_N)
    ccm = jnp.einsum("rbgpn,gh->brhngp", cc, eye).reshape(S5_NB, 2 * S5_BS, S5_GB * S5_P)
    return bbm.astype(BF16), ccm.astype(BF16)


def _s5_block_diag_bb(m):
    t = m.reshape(S5_NB, S5_GB, S5_P, 2, S5_GB, S5_N)
    d = jnp.einsum("bgprgn->rbgnp", t).reshape(2, S5_G, S5_N, S5_P)
    return d[0], d[1]


def _s5_block_diag_cc(m):
    t = m.reshape(S5_NB, 2, S5_GB, S5_N, S5_GB, S5_P)
    d = jnp.einsum("brgngp->rbgpn", t).reshape(2, S5_G, S5_P, S5_N)
    return d[0], -d[1]


def _scan8(buf, row_off, tab_ref, carry_ref, t, reverse):
    nb = t // 8
    for blk in range(S5_NB):
        cr = slice(blk * 2 * S5_BS, blk * 2 * S5_BS + S5_BS)
        ci = slice(blk * 2 * S5_BS + S5_BS, (blk + 1) * 2 * S5_BS)

        def step(bi, carry, cr=cr, ci=ci):
            b = (nb - 1 - bi) if reverse else bi
            r0 = pl.multiple_of(b * 8, 8) + row_off
            xr = buf[pl.ds(r0, 8), cr]
            xi = buf[pl.ds(r0, 8), ci]
            for k, sh in enumerate((1, 2, 4)):
                shift = (8 - sh) if reverse else sh
                sr, si = pltpu.roll(xr, shift, 0), pltpu.roll(xi, shift, 0)
                tr, ti = tab_ref[k, :, cr], tab_ref[k, :, ci]
                xr, xi = xr + tr * sr - ti * si, xi + tr * si + ti * sr
            pr, pi_ = tab_ref[3, :, cr], tab_ref[3, :, ci]
            c_r, c_i = carry
            xr, xi = xr + pr * c_r - pi_ * c_i, xi + pr * c_i + pi_ * c_r
            buf[pl.ds(r0, 8), cr] = xr
            buf[pl.ds(r0, 8), ci] = xi
            edge = 0 if reverse else 7
            return (jnp.broadcast_to(xr[edge:edge + 1, :], xr.shape), jnp.broadcast_to(xi[edge:edge + 1, :], xi.shape))

        c_r, c_i = lax.fori_loop(0, nb, step, (carry_ref[:, cr], carry_ref[:, ci]))
        carry_ref[:, cr] = c_r
        carry_ref[:, ci] = c_i


def _s5_fwd(proj, bbm, ccm, d_row, glu_w, glu_b, tabs, t):
    l = proj.shape[0]
    nt = l // t

    def body(u_ref, gs_ref, bb_ref, cc_ref, d_ref, gw_ref, gb_ref, tab_ref, ssm_ref, xst_ref, xs, carry):
        @pl.when(pl.program_id(0) == 0)
        def _():
            carry[...] = jnp.zeros_like(carry)

        xst_ref[0] = carry[0:1, :]
        u = u_ref[...]
        ub = u.astype(BF16)
        for blk in range(S5_NB):
            xs[:, blk * 2 * S5_BS:(blk + 1) * 2 * S5_BS] = _dot(ub[:, blk * 128:(blk + 1) * 128], bb_ref[blk])
        _scan8(xs, 0, tab_ref, carry, t, False)
        ys = jnp.concatenate(
            [_dot(xs[:, blk * 2 * S5_BS:(blk + 1) * 2 * S5_BS].astype(BF16), cc_ref[blk]) for blk in range(S5_NB)], axis=1)
        y2 = _gelu(ys + d_ref[...] * u)
        z = _dot(y2.astype(BF16), gw_ref[...]) + gb_ref[...]
        ssm_ref[...] = (y2 * _sigmoid(z) * _silu(gs_ref[...])).astype(ssm_ref.dtype)

    const2 = lambda shape: pl.BlockSpec(shape, lambda i: (0,) * len(shape))
    return pl.pallas_call(
        body, name="s5_fwd", grid=(nt,),
        in_specs=[pl.BlockSpec((t, D_MODEL), lambda i: (i, 3)), pl.BlockSpec((t, D_MODEL), lambda i: (i, 4)),
                  const2(bbm.shape), const2(ccm.shape), const2((1, D_MODEL)), const2((D_MODEL, D_MODEL)),
                  const2((1, D_MODEL)), const2((4, 8, S5_COLS))],
        out_specs=(pl.BlockSpec((t, D_MODEL), lambda i: (i, 0)), pl.BlockSpec((1, 1, S5_COLS), lambda i: (i, 0, 0))),
        out_shape=(jax.ShapeDtypeStruct((l, D_MODEL), BF16), jax.ShapeDtypeStruct((nt, 1, S5_COLS), F32)),
        scratch_shapes=[pltpu.VMEM((t, S5_COLS), F32), pltpu.VMEM((8, S5_COLS), F32)],
        compiler_params=_cp("arbitrary"),
    )(proj, proj, bbm, ccm, d_row, glu_w, glu_b, tabs)


def _s5_bwd(proj, dmix, xstart, bbm, ccm, d_row, glu_w, glu_b, tabs_f, tabs_r, t):
    l = proj.shape[0]
    nt = l // t

    def body(u_ref, gs_ref, dm_ref, xst_ref, bb_ref, cc_ref, d_ref, gw_ref, gb_ref, tf_ref, tr_ref,
             dug_ref, y2_ref, dz_ref, dbb_ref, dcc_ref, da_ref, dd_ref, dgb_ref, xs, lam, carry, lcarry):
        @pl.when(pl.program_id(0) == 0)
        def _():
            lcarry[...] = jnp.zeros_like(lcarry)
            dbb_ref[...] = jnp.zeros_like(dbb_ref)
            dcc_ref[...] = jnp.zeros_like(dcc_ref)
            da_ref[...] = jnp.zeros_like(da_ref)
            dd_ref[...] = jnp.zeros_like(dd_ref)
            dgb_ref[...] = jnp.zeros_like(dgb_ref)

        carry[...] = jnp.broadcast_to(xst_ref[0], carry.shape)
        xs[0:8, :] = carry[...]
        u = u_ref[...]
        ub = u.astype(BF16)
        for blk in range(S5_NB):
            xs[8:, blk * 2 * S5_BS:(blk + 1) * 2 * S5_BS] = _dot(ub[:, blk * 128:(blk + 1) * 128], bb_ref[blk])
        _scan8(xs, 8, tf_ref, carry, t, False)
        ys = jnp.concatenate(
            [_dot(xs[8:, blk * 2 * S5_BS:(blk + 1) * 2 * S5_BS].astype(BF16), cc_ref[blk]) for blk in range(S5_NB)],
            axis=1)
        dv = d_ref[...]
        y1 = ys + dv * u
        y2 = _gelu(y1)
        y2b = y2.astype(BF16)
        sg = _sigmoid(_dot(y2b, gw_ref[...]) + gb_ref[...])
        gs = gs_ref[...]
        dssm = dm_ref[...]
        dug_ref[:, D_MODEL:] = (dssm * (y2 * sg) * _dsilu(gs)).astype(dug_ref.dtype)
        dy3 = dssm * _silu(gs)
        dz = dy3 * y2 * sg * (1.0 - sg)
        dzb = dz.astype(BF16)
        y2_ref[...] = y2b
        dz_ref[...] = dzb
        dgb_ref[...] += jnp.sum(dz, axis=0, keepdims=True)
        dy1 = (dy3 * sg + _dot_nt(dzb, gw_ref[...])) * _dgelu(y1)
        dd_ref[...] += jnp.sum(dy1 * u, axis=0, keepdims=True)
        dyb = dy1.astype(BF16)
        for blk in range(S5_NB):
            cols = slice(blk * 2 * S5_BS, (blk + 1) * 2 * S5_BS)
            ch = slice(blk * 128, (blk + 1) * 128)
            lam[:, cols] = _dot_nt(dyb[:, ch], cc_ref[blk])
            dcc_ref[blk] += _dot_tn(xs[8:, cols].astype(BF16), dyb[:, ch])
        _scan8(lam, 0, tr_ref, lcarry, t, True)
        nb = t // 8
        for blk in range(S5_NB):
            cr = slice(blk * 2 * S5_BS, blk * 2 * S5_BS + S5_BS)
            ci = slice(blk * 2 * S5_BS + S5_BS, (blk + 1) * 2 * S5_BS)

            def acc_step(b, acc, cr=cr, ci=ci):
                r0 = pl.multiple_of(b * 8, 8)
                row = lax.broadcasted_iota(jnp.int32, (8, S5_BS), 0)

                def prev(cols):
                    cur = xs[pl.ds(r0 + 8, 8), cols]
                    before = xs[pl.ds(r0, 8), cols]
                    return jnp.where(row == 0, jnp.broadcast_to(before[7:8, :], cur.shape), pltpu.roll(cur, 1, 0))

                pr, pi_ = prev(cr), prev(ci)
                lr, li = lam[pl.ds(r0, 8), cr], lam[pl.ds(r0, 8), ci]
                return (acc[0] + lr * pr + li * pi_, acc[1] + li * pr - lr * pi_)

            a_r, a_i = lax.fori_loop(0, nb, acc_step, (da_ref[:, cr], da_ref[:, ci]))
            da_ref[:, cr] = a_r
            da_ref[:, ci] = a_i
        du = []
        for blk in range(S5_NB):
            cols = slice(blk * 2 * S5_BS, (blk + 1) * 2 * S5_BS)
            lb = lam[:, cols].astype(BF16)
            du.append(_dot_nt(lb, bb_ref[blk]))
            dbb_ref[blk] += _dot_tn(ub[:, blk * 128:(blk + 1) * 128], lb)
        dug_ref[:, :D_MODEL] = (jnp.concatenate(du, axis=1) + dy1 * dv).astype(dug_ref.dtype)

    rev = lambda i: nt - 1 - i
    const2 = lambda shape: pl.BlockSpec(shape, lambda i: (0,) * len(shape))
    row_out = lambda w: pl.BlockSpec((t, w), lambda i: (rev(i), 0))
    return pl.pallas_call(
        body, name="s5_bwd", grid=(nt,),
        in_specs=[pl.BlockSpec((t, D_MODEL), lambda i: (rev(i), 3)), pl.BlockSpec((t, D_MODEL), lambda i: (rev(i), 4)),
                  pl.BlockSpec((t, D_MODEL), lambda i: (rev(i), 1)),
                  pl.BlockSpec((1, 1, S5_COLS), lambda i: (rev(i), 0, 0)),
                  const2(bbm.shape), const2(ccm.shape), const2((1, D_MODEL)), const2((D_MODEL, D_MODEL)),
                  const2((1, D_MODEL)), const2((4, 8, S5_COLS)), const2((4, 8, S5_COLS))],
        out_specs=(row_out(2 * D_MODEL), row_out(D_MODEL), row_out(D_MODEL), const2(bbm.shape), const2(ccm.shape),
                   const2((8, S5_COLS)), const2((1, D_MODEL)), const2((1, D_MODEL))),
        out_shape=(jax.ShapeDtypeStruct((l, 2 * D_MODEL), BF16), jax.ShapeDtypeStruct((l, D_MODEL), BF16),
                   jax.ShapeDtypeStruct((l, D_MODEL), BF16), jax.ShapeDtypeStruct(bbm.shape, F32),
                   jax.ShapeDtypeStruct(ccm.shape, F32), jax.ShapeDtypeStruct((8, S5_COLS), F32),
                   jax.ShapeDtypeStruct((1, D_MODEL), F32), jax.ShapeDtypeStruct((1, D_MODEL), F32)),
        scratch_shapes=[pltpu.VMEM((t + 8, S5_COLS), F32), pltpu.VMEM((t, S5_COLS), F32),
                        pltpu.VMEM((8, S5_COLS), F32), pltpu.VMEM((8, S5_COLS), F32)],
        compiler_params=_cp("arbitrary"),
    )(proj, proj, dmix, xstart, bbm, ccm, d_row, glu_w, glu_b, tabs_f, tabs_r)


def _attn_probs(qh, kh):
    s = _dot_nt(qh, kh) * (XA_DH ** -0.5)
    e = jnp.exp(s - jnp.max(s, axis=-1, keepdims=True))
    return e / jnp.sum(e, axis=-1, keepdims=True)


def _attn_fwd(qa, ka, va):
    l = qa.shape[0]
    m = ka.shape[0]
    tl = _pick(l, (512, 256))

    def body(q_ref, k_ref, v_ref, o_ref):
        for h in range(XA_HEADS):
            hs = slice(h * XA_DH, (h + 1) * XA_DH)
            p = _attn_probs(q_ref[:, hs], k_ref[:, hs])
            o_ref[:, hs] = _dot(p.astype(BF16), v_ref[:, hs]).astype(o_ref.dtype)

    return pl.pallas_call(
        body, name="xattn_fwd", grid=(l // tl,),
        in_specs=[pl.BlockSpec((tl, D_MODEL), lambda i: (i, 0)), pl.BlockSpec((m, D_MODEL), lambda i: (0, 0)),
                  pl.BlockSpec((m, D_MODEL), lambda i: (0, 0))],
        out_specs=pl.BlockSpec((tl, D_MODEL), lambda i: (i, 0)),
        out_shape=jax.ShapeDtypeStruct((l, D_MODEL), BF16), compiler_params=_cp("parallel"),
    )(qa, ka, va)


def _attn_bwd(qa, ka, va, doa):
    l = qa.shape[0]
    m = ka.shape[0]
    tl = _pick(l, (512, 256))

    def body(q_ref, k_ref, v_ref, do_ref, dq_ref, dk_ref, dv_ref):
        @pl.when(pl.program_id(0) == 0)
        def _():
            dk_ref[...] = jnp.zeros_like(dk_ref)
            dv_ref[...] = jnp.zeros_like(dv_ref)

        for h in range(XA_HEADS):
            hs = slice(h * XA_DH, (h + 1) * XA_DH)
            qh, kh, vh, doh = q_ref[:, hs], k_ref[:, hs], v_ref[:, hs], do_ref[:, hs]
            p = _attn_probs(qh, kh)
            dv_ref[:, hs] += _dot_tn(p.astype(BF16), doh)
            dp = _dot_nt(doh, vh)
            ds = (p * (dp - jnp.sum(dp * p, axis=-1, keepdims=True)) * (XA_DH ** -0.5)).astype(BF16)
            dq_ref[:, hs] = _dot(ds, kh).astype(dq_ref.dtype)
            dk_ref[:, hs] += _dot_tn(ds, qh)

    row = pl.BlockSpec((tl, D_MODEL), lambda i: (i, 0))
    mem = pl.BlockSpec((m, D_MODEL), lambda i: (0, 0))
    return pl.pallas_call(
        body, name="xattn_bwd", grid=(l // tl,), in_specs=[row, mem, mem, row], out_specs=(row, mem, mem),
        out_shape=(jax.ShapeDtypeStruct((l, D_MODEL), BF16), jax.ShapeDtypeStruct((m, D_MODEL), F32),
                   jax.ShapeDtypeStruct((m, D_MODEL), F32)),
        compiler_params=_cp("arbitrary"),
    )(qa, ka, va, doa)


def _me_and_peers():
    x, y, c = lax.axis_index("x"), lax.axis_index("y"), lax.axis_index("c")
    flip = lambda v, bit: (1 - v) if bit else v
    peers = []
    for k in range(1, N_DEV):
        px, py, pc = flip(x, (k >> 2) & 1), flip(y, (k >> 1) & 1), flip(c, k & 1)
        peers.append(((px, py, pc), 4 * px + 2 * py + pc))
    return 4 * x + 2 * y + c, peers


def _exchange(local_pairs, n_arrays, send_sems, recv_sems, loc_sems, src_of, dst_of):
    me, peers = _me_and_peers()
    local = [pltpu.make_async_copy(src_of(a, me), dst_of(a, me), loc_sems.at[a]) for a in range(n_arrays)]
    for cp in local:
        cp.start()
    sends, recvs = [], []
    for k, (peer, peer_idx) in enumerate(peers):
        for a in range(n_arrays):
            s = n_arrays * k + a
            sends.append(pltpu.make_async_remote_copy(
                src_ref=src_of(a, peer_idx), dst_ref=dst_of(a, me), send_sem=send_sems.at[s], recv_sem=recv_sems.at[s],
                device_id=peer, device_id_type=MESH))
            recvs.append(pltpu.make_async_remote_copy(
                src_ref=src_of(a, me), dst_ref=dst_of(a, peer_idx), send_sem=send_sems.at[s], recv_sem=recv_sems.at[s],
                device_id=peer, device_id_type=MESH))
    for cp in sends:
        cp.start()
    for cp in recvs:
        cp.wait_recv()
    for cp in sends:
        cp.wait_send()
    for cp in local:
        cp.wait()


def _allgather_weights(w_in_shard, row_shards):
    n_row = len(row_shards)

    def body(*refs):
        win_ref = refs[0]
        row_refs = refs[1:1 + n_row]
        out_win, out_rows, win_b, rows_b, send_sems, recv_sems, loc_sems = refs[1 + n_row:]
        win_b[...] = win_ref[...].astype(BF16)
        off = 0
        for r in row_refs:
            rows_b[off:off + r.shape[0], :] = r[...].astype(BF16)
            off += r.shape[0]
        srcs = (win_b, rows_b)
        outs = (out_win, out_rows)
        me, _ = _me_and_peers()
        _exchange(None, 2, send_sems, recv_sems, loc_sems,
                  src_of=lambda a, idx: srcs[a], dst_of=lambda a, idx: outs[a].at[idx])

    vm = pl.BlockSpec(memory_space=pltpu.VMEM)
    hbm = pl.BlockSpec(memory_space=pl.ANY)
    return pl.pallas_call(
        body, name="allgather_weights", in_specs=[vm] * (1 + n_row), out_specs=(hbm, hbm),
        out_shape=(jax.ShapeDtypeStruct((N_DEV, D_MODEL, W_IN_SHARD), BF16),
                   jax.ShapeDtypeStruct((N_DEV, ROWPACK, D_MODEL), BF16)),
        scratch_shapes=[pltpu.VMEM((D_MODEL, W_IN_SHARD), BF16), pltpu.VMEM((ROWPACK, D_MODEL), BF16),
                        pltpu.SemaphoreType.DMA((2 * (N_DEV - 1),)), pltpu.SemaphoreType.DMA((2 * (N_DEV - 1),)),
                        pltpu.SemaphoreType.DMA((2,))],
        compiler_params=pltpu.CompilerParams(vmem_limit_bytes=VMEM_LIMIT),
    )(w_in_shard, *row_shards)


def _exchange_grads(dwin_slots, drows_slots, small):
    def body(dwin_ref, drows_ref, small_ref, got_win, got_rows, got_small, send_sems, recv_sems, loc_sems):
        srcs = (dwin_ref, drows_ref)
        outs = (got_win, got_rows, got_small)
        _exchange(None, 3, send_sems, recv_sems, loc_sems,
                  src_of=lambda a, idx: small_ref if a == 2 else srcs[a].at[idx],
                  dst_of=lambda a, idx: outs[a].at[idx])

    hbm = pl.BlockSpec(memory_space=pl.ANY)
    return pl.pallas_call(
        body, name="exchange_grads", in_specs=[hbm, hbm, hbm], out_specs=(hbm, hbm, hbm),
        out_shape=(jax.ShapeDtypeStruct(dwin_slots.shape, BF16), jax.ShapeDtypeStruct(drows_slots.shape, BF16),
                   jax.ShapeDtypeStruct((N_DEV,) + small.shape, F32)),
        scratch_shapes=[pltpu.SemaphoreType.DMA((3 * (N_DEV - 1),)), pltpu.SemaphoreType.DMA((3 * (N_DEV - 1),)),
                        pltpu.SemaphoreType.DMA((3,))],
    )(dwin_slots, drows_slots, small)


def _adamw(name, got, w, m, v):
    r, c = w.shape
    tr = _pick(r, (256, 128, 56))

    def body(got_ref, w_ref, m_ref, v_ref, g_ref, d_ref, nm_ref, nv_ref):
        g = got_ref[0].astype(F32)
        for j in range(1, N_DEV):
            g = g + got_ref[j].astype(F32)
        nm = ADAM_B1 * m_ref[...] + (1.0 - ADAM_B1) * g
        nv = ADAM_B2 * v_ref[...] + (1.0 - ADAM_B2) * jnp.square(g)
        m_hat = nm / (1.0 - ADAM_B1 ** ADAM_STEP)
        v_hat = nv / (1.0 - ADAM_B2 ** ADAM_STEP)
        g_ref[...] = g
        d_ref[...] = -ADAM_LR * (m_hat / (jnp.sqrt(v_hat) + ADAM_EPS) + ADAM_WD * w_ref[...])
        nm_ref[...] = nm
        nv_ref[...] = nv

    blk = pl.BlockSpec((tr, c), lambda i: (i, 0))
    out = jax.ShapeDtypeStruct((r, c), F32)
    return pl.pallas_call(
        body, name=name, grid=(r // tr,),
        in_specs=[pl.BlockSpec((N_DEV, tr, c), lambda i: (0, i, 0)), blk, blk, blk],
        out_specs=(blk, blk, blk, blk), out_shape=(out, out, out, out), compiler_params=_cp("parallel"),
    )(got, w, m, v)


_SMALL_VECS = ("norm1_g", "ret_gn_g", "s5_d", "s5_glu_b", "norm2_g", "norm_mem_g", "norm_f_g")
_SMALL_ORDER = _SMALL_VECS + ("s5_a_re", "s5_a_im", "s5_log_dt", "s5_b_re", "s5_b_im", "s5_c_re", "s5_c_im")


def _pack_small(t, extra_row=None):
    rows = [t[n].reshape(1, D_MODEL) for n in _SMALL_VECS]
    rows.append(jnp.zeros((1, D_MODEL), F32) if extra_row is None else extra_row)
    rows += [t["s5_a_re"].reshape(4, D_MODEL), t["s5_a_im"].reshape(4, D_MODEL)]
    rows.append(jnp.pad(t["s5_log_dt"].reshape(1, S5_G), ((0, 7), (0, D_MODEL - S5_G))))
    rows += [t[n].reshape(64, D_MODEL) for n in ("s5_b_re", "s5_b_im", "s5_c_re", "s5_c_im")]
    return jnp.concatenate(rows, axis=0)


def _unpack_small(p, shapes):
    out = {n: p[i].reshape(shapes[n]) for i, n in enumerate(_SMALL_VECS)}
    out["s5_a_re"] = p[8:12].reshape(shapes["s5_a_re"])
    out["s5_a_im"] = p[12:16].reshape(shapes["s5_a_im"])
    out["s5_log_dt"] = p[16, :S5_G].reshape(shapes["s5_log_dt"])
    for i, n in enumerate(("s5_b_re", "s5_b_im", "s5_c_re", "s5_c_im")):
        out[n] = p[24 + 64 * i:24 + 64 * (i + 1)].reshape(shapes[n])
    return out


_W_NAMES = ("norm1_g", "w_in", "ret_gn_g", "s5_a_re", "s5_a_im", "s5_log_dt", "s5_b_re", "s5_b_im", "s5_c_re", "s5_c_im",
            "s5_d", "s5_glu_w", "s5_glu_b", "w_out", "norm2_g", "norm_mem_g", "xa_wq", "xa_wk", "xa_wv", "xa_wo",
            "norm_f_g")
_ROW_NAMES = ("s5_glu_w", "w_out", "xa_wq", "xa_wk", "xa_wv", "xa_wo")


def kernel(x, mem, positions, norm1_g, w_in, ret_gn_g, s5_a_re, s5_a_im, s5_log_dt, s5_b_re, s5_b_im, s5_c_re, s5_c_im, s5_d, s5_glu_w, s5_glu_b, w_out, norm2_g, norm_mem_g, xa_wq, xa_wk, xa_wv, xa_wo, norm_f_g, loss_target, m_norm1_g, m_w_in, m_ret_gn_g, m_s5_a_re, m_s5_a_im, m_s5_log_dt, m_s5_b_re, m_s5_b_im, m_s5_c_re, m_s5_c_im, m_s5_d, m_s5_glu_w, m_s5_glu_b, m_w_out, m_norm2_g, m_norm_mem_g, m_xa_wq, m_xa_wk, m_xa_wv, m_xa_wo, m_norm_f_g, v_norm1_g, v_w_in, v_ret_gn_g, v_s5_a_re, v_s5_a_im, v_s5_log_dt, v_s5_b_re, v_s5_b_im, v_s5_c_re, v_s5_c_im, v_s5_d, v_s5_glu_w, v_s5_glu_b, v_w_out, v_norm2_g, v_norm_mem_g, v_xa_wq, v_xa_wk, v_xa_wv, v_xa_wo, v_norm_f_g):
    w = dict(norm1_g=norm1_g, w_in=w_in, ret_gn_g=ret_gn_g, s5_a_re=s5_a_re, s5_a_im=s5_a_im, s5_log_dt=s5_log_dt,
             s5_b_re=s5_b_re, s5_b_im=s5_b_im, s5_c_re=s5_c_re, s5_c_im=s5_c_im, s5_d=s5_d, s5_glu_w=s5_glu_w,
             s5_glu_b=s5_glu_b, w_out=w_out, norm2_g=norm2_g, norm_mem_g=norm_mem_g, xa_wq=xa_wq, xa_wk=xa_wk,
             xa_wv=xa_wv, xa_wo=xa_wo, norm_f_g=norm_f_g)
    mom = dict(norm1_g=m_norm1_g, w_in=m_w_in, ret_gn_g=m_ret_gn_g, s5_a_re=m_s5_a_re, s5_a_im=m_s5_a_im,
               s5_log_dt=m_s5_log_dt, s5_b_re=m_s5_b_re, s5_b_im=m_s5_b_im, s5_c_re=m_s5_c_re, s5_c_im=m_s5_c_im,
               s5_d=m_s5_d, s5_glu_w=m_s5_glu_w, s5_glu_b=m_s5_glu_b, w_out=m_w_out, norm2_g=m_norm2_g,
               norm_mem_g=m_norm_mem_g, xa_wq=m_xa_wq, xa_wk=m_xa_wk, xa_wv=m_xa_wv, xa_wo=m_xa_wo,
               norm_f_g=m_norm_f_g)
    var = dict(norm1_g=v_norm1_g, w_in=v_w_in, ret_gn_g=v_ret_gn_g, s5_a_re=v_s5_a_re, s5_a_im=v_s5_a_im,
               s5_log_dt=v_s5_log_dt, s5_b_re=v_s5_b_re, s5_b_im=v_s5_b_im, s5_c_re=v_s5_c_re, s5_c_im=v_s5_c_im,
               s5_d=v_s5_d, s5_glu_w=v_s5_glu_w, s5_glu_b=v_s5_glu_b, w_out=v_w_out, norm2_g=v_norm2_g,
               norm_mem_g=v_norm_mem_g, xa_wq=v_xa_wq, xa_wk=v_xa_wk, xa_wv=v_xa_wv, xa_wo=v_xa_wo,
               norm_f_g=v_norm_f_g)
    shapes = {n: w[n].shape for n in _W_NAMES}

    x2d, mem2d, tgt = x[0], mem[0], loss_target[0]
    l = x2d.shape[0]
    ret_c = _pick(l, (128,))
    s5_t = _pick(l, (128,))
    g1, g2, gm, gf = norm1_g, norm2_g, norm_mem_g, norm_f_g.reshape(1, D_MODEL)

    win_s, rows_all = _allgather_weights(w_in[0], [w[n][0] for n in _ROW_NAMES])
    full = {}
    off = 0
    for n, r in zip(_ROW_NAMES, ROW_SHARDS):
        full[n] = rows_all[:, off:off + r, :].reshape(N_DEV * r, D_MODEL)
        off += r

    disc_args = (s5_a_re[0], s5_a_im[0], s5_log_dt[0], s5_b_re[0], s5_b_im[0])
    (abar_re, abar_im, bb_re, bb_im), disc_vjp = jax.vjp(_s5_discretize, *disc_args)
    bbm, ccm = _s5_block_mats(bb_re, bb_im, s5_c_re[0], s5_c_im[0])
    tabs = _s5_scan_tables(abar_re, abar_im)

    h1 = _rms_fwd("norm1_fwd", x2d, g1)
    proj = _mm_nn_slots("in_proj", h1, win_s, F32)
    half = RET_DK // 2
    inv = ROPE_BASE ** (-jnp.arange(half, dtype=F32) / half)
    cos_t, sin_t = _rope_tables(positions[0].reshape(l, 1), jnp.tile(inv, 128 // half)[None, :])
    rconsts = _ret_constants(ret_c)
    ret, o_saved, r_prev = _ret_fwd(proj, cos_t, sin_t, rconsts, ret_gn_g, ret_c)
    ssm, xstart = _s5_fwd(proj, bbm, ccm, s5_d, full["s5_glu_w"], s5_glu_b, tabs[0], s5_t)
    mix = jnp.concatenate([ret, ssm], axis=1)
    x1 = _mm_nn("out_proj", mix, full["w_out"], F32, residual=x2d)
    h2 = _rms_fwd("norm2_fwd", x1, g2)
    mn = _rms_fwd("norm_mem_fwd", mem2d, gm)
    qa = _mm_nn("xa_q", h2, full["xa_wq"], BF16)
    ka = _mm_nn("xa_k", mn, full["xa_wk"], BF16)
    va = _mm_nn("xa_v", mn, full["xa_wv"], BF16)
    oa = _attn_fwd(qa, ka, va)
    x2 = _mm_nn("xa_o", oa, full["xa_wo"], F32, residual=x1)
    dx2, dgf, loss_lanes = _loss_head(x2, gf, tgt)

    doa = _mm_nt("xa_o_dx", dx2, full["xa_wo"], BF16)
    dwo = _mm_tn("xa_o_dw", oa, dx2, BF16)
    dqa, dka, dva = _attn_bwd(qa, ka, va, doa)
    dh2 = _mm_nt("xa_q_dx", dqa, full["xa_wq"], F32)
    dwq = _mm_tn("xa_q_dw", h2, dqa, BF16)
    dx1, dg2 = _rms_bwd("norm2_bwd", x1, g2, dh2, dx2)
    dwk = _mm_tn("xa_k_dw", mn, dka, BF16)
    dwv = _mm_tn("xa_v_dw", mn, dva, BF16)
    dmn = _mm_nt("xa_v_dx", dva, full["xa_wv"], F32, residual=_mm_nt("xa_k_dx", dka, full["xa_wk"], F32))
    _, dgm = _rms_bwd("norm_mem_bwd", mem2d, gm, dmn, None)
    dmix = _mm_nt("out_proj_dx", dx1, full["w_out"], F32)
    dwout = _mm_tn("out_proj_dw", mix, dx1, BF16)
    dret, dgn = _ret_bwd(proj, cos_t, sin_t, rconsts, ret_gn_g, o_saved, r_prev, dmix, ret_c)
    dug, y2, dz, dbbm, dccm, dabar, dd, dgb = _s5_bwd(proj, dmix, xstart, bbm, ccm, s5_d, full["s5_glu_w"], s5_glu_b,
                                                      tabs[0], tabs[1], s5_t)
    dglu = _mm_tn("s5_glu_dw", y2, dz, BF16)
    dproj = jnp.concatenate([dret, dug], axis=1)
    dh1 = _mm_nt_slots("in_proj_dx", dproj, win_s, F32)
    dwin_s = _mm_tn_slots("in_proj_dw", h1, dproj, N_DEV, BF16)
    grad_x, dg1 = _rms_bwd("norm1_bwd", x2d, g1, dh1, dx1)

    dab_re, dab_im = _s5_uncols(jnp.sum(dabar, axis=0))
    dbb_re, dbb_im = _s5_block_diag_bb(dbbm)
    dc_re, dc_im = _s5_block_diag_cc(dccm)
    da_re, da_im, dlog_dt, db_re, db_im = disc_vjp((dab_re, dab_im, dbb_re, dbb_im))
    small_g = dict(norm1_g=dg1, ret_gn_g=dgn, s5_d=dd, s5_glu_b=dgb, norm2_g=dg2, norm_mem_g=dgm, norm_f_g=dgf,
                   s5_a_re=da_re, s5_a_im=da_im, s5_log_dt=dlog_dt, s5_b_re=db_re, s5_b_im=db_im, s5_c_re=dc_re,
                   s5_c_im=dc_im)
    small_pack = _pack_small(small_g, extra_row=loss_lanes)

    drows = jnp.concatenate(
        [g.reshape(N_DEV, r, D_MODEL) for g, r in zip((dglu, dwout, dwq, dwk, dwv, dwo), ROW_SHARDS)], axis=1)
    got_win, got_rows, got_small = _exchange_grads(dwin_s, drows, small_pack)

    res = {}
    res["w_in"] = _adamw("adamw_w_in", got_win, w_in[0], m_w_in[0], v_w_in[0])
    cat = lambda t: jnp.concatenate([t[n][0] for n in _ROW_NAMES], axis=0)
    rows_out = _adamw("adamw_rows", got_rows, cat(w), cat(mom), cat(var))
    off = 0
    for n, r in zip(_ROW_NAMES, ROW_SHARDS):
        res[n] = tuple(a[off:off + r] for a in rows_out)
        off += r
    small_out = _adamw("adamw_small", got_small, _pack_small(w), _pack_small(mom), _pack_small(var))
    unpacked = [_unpack_small(a, shapes) for a in small_out]
    loss = (0.5 / D_MODEL) * jnp.sum(small_out[0][7])
    for n in _SMALL_ORDER:
        res[n] = tuple(u[n] for u in unpacked)

    outs = [loss, grad_x[None]]
    for part in range(4):
        for n in _W_NAMES:
            outs.append(res[n][part].reshape(shapes[n]))
    return tuple(outs)
```

```python
import functools

import jax
import jax.numpy as jnp
from jax import lax
from jax.experimental import pallas as pl
from jax.experimental.pallas import tpu as pltpu

F32 = jnp.float32
BF16 = jnp.bfloat16
MESH = pl.DeviceIdType.MESH

D_MODEL = 1024
RET_HEADS, RET_DK, RET_DV = 8, 64, 128
RET_QK = RET_HEADS * RET_DK
S5_G, S5_N, S5_P = 64, 64, 16
S5_NB = 8
S5_GB = S5_G // S5_NB
S5_BS = S5_GB * S5_N
S5_COLS = 2 * S5_G * S5_N
XA_HEADS, XA_DH = 4, 256
EPS = 1e-6
ROPE_BASE = 10000.0
N_DEV = 8
W_IN_SHARD = 640
ROW_SHARDS = (128, 256, 128, 128, 128, 128)
ROWPACK = sum(ROW_SHARDS)
SMALL_ROWS = 320
ADAM_LR, ADAM_B1, ADAM_B2, ADAM_EPS, ADAM_WD, ADAM_STEP = 0.001, 0.9, 0.999, 1e-08, 0.01, 10

VMEM_LIMIT = 56 * 1024 * 1024


def _cp(*sem):
    return pltpu.CompilerParams(dimension_semantics=tuple(sem), vmem_limit_bytes=VMEM_LIMIT)


def _dot(a, b):
    return jnp.dot(a, b, preferred_element_type=F32)


def _dot_nt(a, b):
    return lax.dot_general(a, b, (((1,), (1,)), ((), ())), preferred_element_type=F32)


def _dot_tn(a, b):
    return lax.dot_general(a, b, (((0,), (0,)), ((), ())), preferred_element_type=F32)


def _sigmoid(x):
    return 1.0 / (1.0 + jnp.exp(-x))


def _silu(x):
    return x * _sigmoid(x)


def _dsilu(x):
    s = _sigmoid(x)
    return s * (1.0 + x * (1.0 - s))


_GELU_C = 0.7978845608028654


def _gelu(x):
    return 0.5 * x * (1.0 + jnp.tanh(_GELU_C * (x + 0.044715 * (x * x * x))))


def _dgelu(x):
    t = jnp.tanh(_GELU_C * (x + 0.044715 * (x * x * x)))
    return 0.5 * (1.0 + t) + 0.5 * x * (1.0 - t * t) * (_GELU_C * (1.0 + 3.0 * 0.044715 * (x * x)))


def _pick(n, cands):
    for c in cands:
        if n % c == 0:
            return c
    return n


def _mm_core(name, operands, in_specs, out_spec, out_shape, grid, nk, dims, acc_shape, has_res, side=None):
    n_in = 3 if has_res else 2
    n_side_in = len(side.srcs) if side else 0
    n_side_out = side.n if side else 0

    def body(*refs):
        a_ref, b_ref = refs[0], refs[1]
        r_ref = refs[2] if has_res else None
        side_in = refs[n_in:n_in + n_side_in]
        o_ref = refs[n_in + n_side_in]
        side_out = refs[n_in + n_side_in + 1:n_in + n_side_in + 1 + n_side_out]
        acc = refs[n_in + n_side_in + 1 + n_side_out]
        sems = refs[n_in + n_side_in + 2 + n_side_out:]
        i, j, k = pl.program_id(0), pl.program_id(1), pl.program_id(2)
        if side:
            @pl.when((i == 0) & (j == 0) & (k == 0))
            def _():
                side.start(side_in, side_out, sems)

        @pl.when(k == 0)
        def _():
            acc[...] = jnp.zeros_like(acc)

        acc[...] += lax.dot_general(a_ref[...].astype(BF16), b_ref[...].astype(BF16), (dims, ((), ())),
                                    preferred_element_type=F32)

        @pl.when(k == nk - 1)
        def _():
            r = acc[...]
            if has_res:
                r = r + r_ref[...]
            o_ref[...] = r.astype(o_ref.dtype)

        if side:
            @pl.when((i == grid[0] - 1) & (j == grid[1] - 1) & (k == grid[2] - 1))
            def _():
                side.wait(side_in, side_out, sems)

    if side:
        return pl.pallas_call(
            body, name=name, grid=grid, in_specs=list(in_specs) + side.in_specs,
            out_specs=(out_spec, *side.out_specs), out_shape=(out_shape, *side.landing),
            scratch_shapes=[pltpu.VMEM(acc_shape, F32)] + side.scratch,
            compiler_params=_cp("arbitrary", "arbitrary", "arbitrary"),
        )(*operands, *side.srcs)
    return pl.pallas_call(
        body, name=name, grid=grid, in_specs=in_specs, out_specs=out_spec, out_shape=out_shape,
        scratch_shapes=[pltpu.VMEM(acc_shape, F32)],
        compiler_params=_cp("parallel", "parallel", "arbitrary"),
    )(*operands)


def _mm_nn(name, a, b, out_dtype, residual=None):
    m, kk = a.shape
    n = b.shape[1]
    tm, tn, tk = _pick(m, (1024, 512, 256)), _pick(n, (1024, 512)), _pick(kk, (1024, 512))
    ops = [a, b]
    specs = [pl.BlockSpec((tm, tk), lambda i, j, k: (i, k)), pl.BlockSpec((tk, tn), lambda i, j, k: (k, j))]
    if residual is not None:
        ops.append(residual)
        specs.append(pl.BlockSpec((tm, tn), lambda i, j, k: (i, j)))
    return _mm_core(name, ops, specs, pl.BlockSpec((tm, tn), lambda i, j, k: (i, j)),
                    jax.ShapeDtypeStruct((m, n), out_dtype), (m // tm, n // tn, kk // tk), kk // tk,
                    ((1,), (0,)), (tm, tn), residual is not None)


def _mm_nt(name, a, b, out_dtype, residual=None):
    m, kk = a.shape
    n = b.shape[0]
    tm, tn, tk = _pick(m, (1024, 512, 256)), _pick(n, (1024, 512)), _pick(kk, (1024, 512))
    ops = [a, b]
    specs = [pl.BlockSpec((tm, tk), lambda i, j, k: (i, k)), pl.BlockSpec((tn, tk), lambda i, j, k: (j, k))]
    if residual is not None:
        ops.append(residual)
        specs.append(pl.BlockSpec((tm, tn), lambda i, j, k: (i, j)))
    return _mm_core(name, ops, specs, pl.BlockSpec((tm, tn), lambda i, j, k: (i, j)),
                    jax.ShapeDtypeStruct((m, n), out_dtype), (m // tm, n // tn, kk // tk), kk // tk,
                    ((1,), (1,)), (tm, tn), residual is not None)


def _mm_tn(name, a, b, out_dtype):
    kk, m = a.shape
    n = b.shape[1]
    tm, tn, tk = _pick(m, (1024, 512)), _pick(n, (1024, 512)), _pick(kk, (1024, 512, 256))
    specs = [pl.BlockSpec((tk, tm), lambda i, j, k: (k, i)), pl.BlockSpec((tk, tn), lambda i, j, k: (k, j))]
    return _mm_core(name, [a, b], specs, pl.BlockSpec((tm, tn), lambda i, j, k: (i, j)),
                    jax.ShapeDtypeStruct((m, n), out_dtype), (m // tm, n // tn, kk // tk), kk // tk,
                    ((0,), (0,)), (tm, tn), False)


def _mm_nn_slots(name, a, b_slots, out_dtype, side=None):
    m, kk = a.shape
    s, _, ns = b_slots.shape
    tm, tk = _pick(m, (1024, 512, 256)), _pick(kk, (1024, 512))
    specs = [pl.BlockSpec((tm, tk), lambda i, j, k: (i, k)), pl.BlockSpec((None, tk, ns), lambda i, j, k: (j, k, 0))]
    return _mm_core(name, [a, b_slots], specs, pl.BlockSpec((tm, ns), lambda i, j, k: (i, j)),
                    jax.ShapeDtypeStruct((m, s * ns), out_dtype), (m // tm, s, kk // tk), kk // tk,
                    ((1,), (0,)), (tm, ns), False, side)


def _mm_nt_slots(name, a, b_slots, out_dtype, side=None):
    m = a.shape[0]
    s, n, ns = b_slots.shape
    tm, tn = _pick(m, (1024, 512, 256)), _pick(n, (1024, 512))
    specs = [pl.BlockSpec((tm, ns), lambda i, j, k: (i, k)), pl.BlockSpec((None, tn, ns), lambda i, j, k: (k, j, 0))]
    return _mm_core(name, [a, b_slots], specs, pl.BlockSpec((tm, tn), lambda i, j, k: (i, j)),
                    jax.ShapeDtypeStruct((m, n), out_dtype), (m // tm, n // tn, s), s,
                    ((1,), (1,)), (tm, tn), False, side)


def _mm_tn_slots(name, a, b, s, out_dtype):
    kk, m = a.shape
    ns = b.shape[1] // s
    tm, tk = _pick(m, (1024, 512)), _pick(kk, (1024, 512, 256))
    specs = [pl.BlockSpec((tk, tm), lambda i, j, k: (k, i)), pl.BlockSpec((tk, ns), lambda i, j, k: (k, j))]
    return _mm_core(name, [a, b], specs, pl.BlockSpec((None, tm, ns), lambda i, j, k: (j, i, 0)),
                    jax.ShapeDtypeStruct((s, m, ns), out_dtype), (m // tm, s, kk // tk), kk // tk,
                    ((0,), (0,)), (tm, ns), False)


def _rms_fwd(name, x, g):
    r, d = x.shape
    tr = _pick(r, (1024, 512, 256))

    def body(x_ref, g_ref, o_ref):
        xv = x_ref[...]
        rs = lax.rsqrt(jnp.mean(xv * xv, axis=-1, keepdims=True) + EPS)
        o_ref[...] = (xv * rs * g_ref[...]).astype(o_ref.dtype)

    return pl.pallas_call(
        body, name=name, grid=(r // tr,),
        in_specs=[pl.BlockSpec((tr, d), lambda i: (i, 0)), pl.BlockSpec((1, d), lambda i: (0, 0))],
        out_specs=pl.BlockSpec((tr, d), lambda i: (i, 0)),
        out_shape=jax.ShapeDtypeStruct((r, d), BF16), compiler_params=_cp("parallel"),
    )(x, g)


def _rms_bwd(name, x, g, dh, dres):
    r, d = x.shape
    tr = _pick(r, (512, 256))
    has_res = dres is not None

    def body(*refs):
        if has_res:
            x_ref, g_ref, dh_ref, dr_ref, dx_ref, dg_ref = refs
        else:
            x_ref, g_ref, dh_ref, dx_ref, dg_ref = refs
        i = pl.program_id(0)

        @pl.when(i == 0)
        def _():
            dg_ref[...] = jnp.zeros_like(dg_ref)

        xv = x_ref[...]
        dhv = dh_ref[...].astype(F32)
        rs = lax.rsqrt(jnp.mean(xv * xv, axis=-1, keepdims=True) + EPS)
        xn = xv * rs
        dg_ref[...] += jnp.sum(dhv * xn, axis=0, keepdims=True)
        dn = dhv * g_ref[...]
        dx = rs * (dn - xn * jnp.mean(dn * xn, axis=-1, keepdims=True))
        if has_res:
            dx = dx + dr_ref[...]
        dx_ref[...] = dx

    row = pl.BlockSpec((tr, d), lambda i: (i, 0))
    vec = pl.BlockSpec((1, d), lambda i: (0, 0))
    ops = [x, g, dh] + ([dres] if has_res else [])
    return pl.pallas_call(
        body, name=name, grid=(r // tr,),
        in_specs=[row, vec, row] + ([row] if has_res else []),
        out_specs=(row, vec),
        out_shape=(jax.ShapeDtypeStruct((r, d), F32), jax.ShapeDtypeStruct((1, d), F32)),
        compiler_params=_cp("arbitrary"),
    )(*ops)


def _loss_head(x2, gf, target):
    r, d = x2.shape
    tr = _pick(r, (512, 256))

    def body(x_ref, g_ref, t_ref, dx_ref, dg_ref, ls_ref):
        i = pl.program_id(0)

        @pl.when(i == 0)
        def _():
            dg_ref[...] = jnp.zeros_like(dg_ref)
            ls_ref[...] = jnp.zeros_like(ls_ref)

        xv = x_ref[...]
        rs = lax.rsqrt(jnp.mean(xv * xv, axis=-1, keepdims=True) + EPS)
        xn = xv * rs
        e = xn * g_ref[...] - t_ref[...]
        ls_ref[...] += jnp.sum(e * e, axis=0, keepdims=True)
        dy = e * (1.0 / d)
        dg_ref[...] += jnp.sum(dy * xn, axis=0, keepdims=True)
        dn = dy * g_ref[...]
        dx_ref[...] = rs * (dn - xn * jnp.mean(dn * xn, axis=-1, keepdims=True))

    row = pl.BlockSpec((tr, d), lambda i: (i, 0))
    vec = pl.BlockSpec((1, d), lambda i: (0, 0))
    return pl.pallas_call(
        body, name="loss_head", grid=(r // tr,), in_specs=[row, vec, row], out_specs=(row, vec, vec),
        out_shape=(jax.ShapeDtypeStruct((r, d), F32), jax.ShapeDtypeStruct((1, d), F32),
                   jax.ShapeDtypeStruct((1, d), F32)),
        compiler_params=_cp("arbitrary"),
    )(x2, gf, target)


def _rope_tables(pos_col, inv_row):
    l = pos_col.shape[0]
    tl = _pick(l, (1024, 512, 256))

    def body(p_ref, inv_ref, cos_ref, sin_ref):
        ang = p_ref[...].astype(F32) * inv_ref[...]
        lane = lax.broadcasted_iota(jnp.int32, ang.shape, 1)
        c = jnp.cos(ang)
        s = jnp.where((lane % RET_DK) < RET_DK // 2, -jnp.sin(ang), jnp.sin(ang))
        cos_ref[...] = jnp.tile(c, (1, RET_QK // 128))
        sin_ref[...] = jnp.tile(s, (1, RET_QK // 128))

    return pl.pallas_call(
        body, name="rope_tables", grid=(l // tl,),
        in_specs=[pl.BlockSpec((tl, 1), lambda i: (i, 0)), pl.BlockSpec((1, 128), lambda i: (0, 0))],
        out_specs=(pl.BlockSpec((tl, RET_QK), lambda i: (i, 0)), pl.BlockSpec((tl, RET_QK), lambda i: (i, 0))),
        out_shape=(jax.ShapeDtypeStruct((l, RET_QK), F32), jax.ShapeDtypeStruct((l, RET_QK), F32)),
        compiler_params=_cp("parallel"),
    )(pos_col, inv_row)


def _rot(x, cos_t, sin_t):
    n = x.shape[-1]
    lane = lax.broadcasted_iota(jnp.int32, x.shape, 1)
    partner = jnp.where((lane % RET_DK) < RET_DK // 2, pltpu.roll(x, n - RET_DK // 2, 1), pltpu.roll(x, RET_DK // 2, 1))
    return x * cos_t + partner * sin_t


def _ret_constants(c):
    log_g = jnp.log1p(-jnp.exp2(-5.0 - jnp.arange(RET_HEADS, dtype=F32)))
    j = jnp.arange(c, dtype=F32)
    diff = j[:, None] - j[None, :]
    decay = jnp.where(diff[None] >= 0.0, jnp.exp(log_g[:, None, None] * jnp.maximum(diff, 0.0)[None]), 0.0)
    q_w = jnp.exp(log_g[None, :] * (j + 1.0)[:, None])
    k_w = jnp.exp(log_g[None, :] * (c - 1.0 - j)[:, None])
    cd = jnp.exp(log_g * c)
    rep = lambda t: jnp.repeat(t, RET_DK, axis=1)
    cd_row = jnp.repeat(cd, RET_DV)[None, :]
    return decay, rep(q_w), rep(k_w), cd_row


def _ret_fwd(proj, cos_t, sin_t, consts, gn_g, c):
    l = proj.shape[0]
    nc = l // c
    decay, qw, kw, cd_row = consts

    def body(q_ref, k_ref, v_ref, g_ref, cos_ref, sin_ref, dec_ref, qw_ref, kw_ref, cd_ref, gn_ref,
             ret_ref, o_ref, rp_ref, state):
        @pl.when(pl.program_id(0) == 0)
        def _():
            state[...] = jnp.zeros_like(state)

        cs, sn = cos_ref[...], sin_ref[...]
        qr = _rot(q_ref[...], cs, sn)
        kr = _rot(k_ref[...], cs, sn) * (RET_DK ** -0.5)
        qb, kb = qr.astype(BF16), kr.astype(BF16)
        qwb = (qr * qw_ref[...]).astype(BF16)
        kwb = (kr * kw_ref[...]).astype(BF16)
        vb = v_ref[...].astype(BF16)
        for h in range(RET_HEADS):
            qs = slice(h * RET_DK, (h + 1) * RET_DK)
            vs = slice(h * RET_DV, (h + 1) * RET_DV)
            s = _dot_nt(qb[:, qs], kb[:, qs]) * dec_ref[h]
            r_prev = state[h]
            rp_ref[0, h] = r_prev
            o = _dot(s.astype(BF16), vb[:, vs]) + _dot(qwb[:, qs], r_prev.astype(BF16))
            state[h] = cd_ref[:, vs] * r_prev + _dot_tn(kwb[:, qs], vb[:, vs])
            o_ref[:, vs] = o
            mu = jnp.mean(o, axis=-1, keepdims=True)
            var = jnp.mean(jnp.square(o - mu), axis=-1, keepdims=True)
            on = (o - mu) * lax.rsqrt(var + EPS)
            ret_ref[:, vs] = (on * gn_ref[:, vs] * _silu(g_ref[:, vs])).astype(ret_ref.dtype)

    const2 = lambda shape: pl.BlockSpec(shape, lambda i: (0,) * len(shape))
    return pl.pallas_call(
        body, name="retention_fwd", grid=(nc,),
        in_specs=[pl.BlockSpec((c, RET_QK), lambda i: (i, 0)), pl.BlockSpec((c, RET_QK), lambda i: (i, 1)),
                  pl.BlockSpec((c, D_MODEL), lambda i: (i, 1)), pl.BlockSpec((c, D_MODEL), lambda i: (i, 2)),
                  pl.BlockSpec((c, RET_QK), lambda i: (i, 0)), pl.BlockSpec((c, RET_QK), lambda i: (i, 0)),
                  const2((RET_HEADS, c, c)), const2((c, RET_QK)), const2((c, RET_QK)), const2((1, D_MODEL)),
                  const2((1, D_MODEL))],
        out_specs=(pl.BlockSpec((c, D_MODEL), lambda i: (i, 0)), pl.BlockSpec((c, D_MODEL), lambda i: (i, 0)),
                   pl.BlockSpec((1, RET_HEADS, RET_DK, RET_DV), lambda i: (i, 0, 0, 0))),
        out_shape=(jax.ShapeDtypeStruct((l, D_MODEL), BF16), jax.ShapeDtypeStruct((l, D_MODEL), F32),
                   jax.ShapeDtypeStruct((nc, RET_HEADS, RET_DK, RET_DV), F32)),
        scratch_shapes=[pltpu.VMEM((RET_HEADS, RET_DK, RET_DV), F32)],
        compiler_params=_cp("arbitrary"),
    )(proj, proj, proj, proj, cos_t, sin_t, decay, qw, kw, cd_row, gn_g)


def _ret_bwd(proj, cos_t, sin_t, consts, gn_g, o_saved, r_prev_saved, dmix, c, side):
    l = proj.shape[0]
    nc = l // c
    decay, qw, kw, cd_row = consts
    n_in = 14

    def body(*refs):
        (q_ref, k_ref, v_ref, g_ref, cos_ref, sin_ref, dec_ref, qw_ref, kw_ref, cd_ref, gn_ref, o_ref, rp_ref,
         dr_ref) = refs[:n_in]
        side_in = refs[n_in:n_in + len(side.srcs)]
        out_ref, dgn_ref = refs[n_in + len(side.srcs):n_in + len(side.srcs) + 2]
        side_out = refs[n_in + len(side.srcs) + 2:n_in + len(side.srcs) + 2 + side.n]
        state, dq_s, dk_s = refs[n_in + len(side.srcs) + 2 + side.n:n_in + len(side.srcs) + 5 + side.n]
        sems = refs[n_in + len(side.srcs) + 5 + side.n:]

        @pl.when(pl.program_id(0) == 0)
        def _():
            side.start(side_in, side_out, sems)
            state[...] = jnp.zeros_like(state)
            dgn_ref[...] = jnp.zeros_like(dgn_ref)

        cs, sn = cos_ref[...], sin_ref[...]
        qr = _rot(q_ref[...], cs, sn)
        kr = _rot(k_ref[...], cs, sn) * (RET_DK ** -0.5)
        qb, kb = qr.astype(BF16), kr.astype(BF16)
        qwb = (qr * qw_ref[...]).astype(BF16)
        kwv = kw_ref[...]
        kwb = (kr * kwv).astype(BF16)
        qwv = qw_ref[...]
        vb = v_ref[...].astype(BF16)
        for h in range(RET_HEADS):
            qs = slice(h * RET_DK, (h + 1) * RET_DK)
            vs = slice(h * RET_DV, (h + 1) * RET_DV)
            dec = dec_ref[h]
            o = o_ref[:, vs]
            mu = jnp.mean(o, axis=-1, keepdims=True)
            var = jnp.mean(jnp.square(o - mu), axis=-1, keepdims=True)
            rstd = lax.rsqrt(var + EPS)
            on = (o - mu) * rstd
            gate = g_ref[:, vs]
            sg = _silu(gate)
            dret = dr_ref[:, vs]
            gn = gn_ref[:, vs]
            dgn_ref[:, vs] += jnp.sum(dret * on * sg, axis=0, keepdims=True)
            out_ref[:, 2 * RET_QK + D_MODEL + h * RET_DV:2 * RET_QK + D_MODEL + (h + 1) * RET_DV] = (
                dret * on * gn * _dsilu(gate)).astype(out_ref.dtype)
            don = dret * gn * sg
            do = rstd * (don - jnp.mean(don, axis=-1, keepdims=True)
                         - on * jnp.mean(don * on, axis=-1, keepdims=True))
            dob = do.astype(BF16)
            sn_h = state[h]
            snb = sn_h.astype(BF16)
            s = _dot_nt(qb[:, qs], kb[:, qs]) * dec
            dv = _dot_tn(s.astype(BF16), dob) + _dot(kwb[:, qs], snb)
            out_ref[:, 2 * RET_QK + h * RET_DV:2 * RET_QK + (h + 1) * RET_DV] = dv.astype(out_ref.dtype)
            ds = (_dot_nt(dob, vb[:, vs]) * dec).astype(BF16)
            dq_s[:, qs] = _dot(ds, kb[:, qs]) + qwv[:, qs] * _dot_nt(dob, rp_ref[0, h].astype(BF16))
            dk_s[:, qs] = _dot_tn(ds, qb[:, qs]) + kwv[:, qs] * _dot_nt(vb[:, vs], snb)
            state[h] = cd_ref[:, vs] * sn_h + _dot_tn(qwb[:, qs], dob)
        out_ref[:, 0:RET_QK] = _rot(dq_s[...], cs, -sn).astype(out_ref.dtype)
        out_ref[:, RET_QK:2 * RET_QK] = (_rot(dk_s[...], cs, -sn) * (RET_DK ** -0.5)).astype(out_ref.dtype)

        @pl.when(pl.program_id(0) == nc - 1)
        def _():
            side.wait(side_in, side_out, sems)

    rev = lambda i: nc - 1 - i
    const2 = lambda shape: pl.BlockSpec(shape, lambda i: (0,) * len(shape))
    return pl.pallas_call(
        body, name="retention_bwd", grid=(nc,),
        in_specs=[pl.BlockSpec((c, RET_QK), lambda i: (rev(i), 0)), pl.BlockSpec((c, RET_QK), lambda i: (rev(i), 1)),
                  pl.BlockSpec((c, D_MODEL), lambda i: (rev(i), 1)), pl.BlockSpec((c, D_MODEL), lambda i: (rev(i), 2)),
                  pl.BlockSpec((c, RET_QK), lambda i: (rev(i), 0)), pl.BlockSpec((c, RET_QK), lambda i: (rev(i), 0)),
                  const2((RET_HEADS, c, c)), const2((c, RET_QK)), const2((c, RET_QK)), const2((1, D_MODEL)),
                  const2((1, D_MODEL)),
                  pl.BlockSpec((c, D_MODEL), lambda i: (rev(i), 0)),
                  pl.BlockSpec((1, RET_HEADS, RET_DK, RET_DV), lambda i: (rev(i), 0, 0, 0)),
                  pl.BlockSpec((c, D_MODEL), lambda i: (rev(i), 0))] + side.in_specs,
        out_specs=(pl.BlockSpec((c, 2 * RET_QK + 2 * D_MODEL), lambda i: (rev(i), 0)), const2((1, D_MODEL)),
                   *side.out_specs),
        out_shape=(jax.ShapeDtypeStruct((l, 2 * RET_QK + 2 * D_MODEL), BF16), jax.ShapeDtypeStruct((1, D_MODEL), F32),
                   *side.landing),
        scratch_shapes=[pltpu.VMEM((RET_HEADS, RET_DK, RET_DV), F32), pltpu.VMEM((c, RET_QK), F32),
                        pltpu.VMEM((c, RET_QK), F32)] + side.scratch,
        compiler_params=_cp("arbitrary"),
    )(proj, proj, proj, proj, cos_t, sin_t, decay, qw, kw, cd_row, gn_g, o_saved, r_prev_saved, dmix, *side.srcs)


def _s5_discretize(a_re, a_im, log_dt, b_re, b_im):
    dt = jnp.exp(log_dt)[:, None]
    mag = jnp.exp(a_re * dt)
    abar_re = mag * jnp.cos(a_im * dt)
    abar_im = mag * jnp.sin(a_im * dt)
    den = a_re * a_re + a_im * a_im
    nr, ni = abar_re - 1.0, abar_im
    f_re = (nr * a_re + ni * a_im) / den
    f_im = (ni * a_re - nr * a_im) / den
    bb_re = f_re[..., None] * b_re - f_im[..., None] * b_im
    bb_im = f_re[..., None] * b_im + f_im[..., None] * b_re
    return abar_re, abar_im, bb_re, bb_im


def _s5_cols(re, im):
    return jnp.stack([re.reshape(S5_NB, S5_BS), im.reshape(S5_NB, S5_BS)], axis=1).reshape(S5_COLS)


def _s5_uncols(v):
    t = v.reshape(S5_NB, 2, S5_BS)
    return t[:, 0].reshape(S5_G, S5_N), t[:, 1].reshape(S5_G, S5_N)


def _s5_scan_tables(abar_re, abar_im):
    def cmul(x, y):
        return x[0] * y[0] - x[1] * y[1], x[0] * y[1] + x[1] * y[0]

    row = jnp.arange(8)[:, None]
    out = []
    for conj in (False, True):
        a1 = (abar_re, -abar_im if conj else abar_im)
        a2 = cmul(a1, a1)
        a4 = cmul(a2, a2)
        pw = [a1]
        for _ in range(7):
            pw.append(cmul(pw[-1], a1))
        tabs = []
        for k, ap in zip((1, 2, 4), (a1, a2, a4)):
            cols = _s5_cols(*ap)[None, :]
            mask = (row <= 7 - k) if conj else (row >= k)
            tabs.append(jnp.where(mask, cols, 0.0))
        order = [pw[7 - i] for i in range(8)] if conj else pw
        tabs.append(jnp.stack([_s5_cols(*p) for p in order], axis=0))
        out.append(jnp.stack(tabs, axis=0))
    return jnp.stack(out, axis=0)


def _s5_block_mats(bb_re, bb_im, c_re, c_im):
    eye = jnp.eye(S5_GB, dtype=F32)
    bb = jnp.stack([bb_re, bb_im], axis=0).reshape(2, S5_NB, S5_GB, S5_N, S5_P)
    bbm = jnp.einsum("rbgnp,gh->bgprhn", bb, eye).reshape(S5_NB, S5_GB * S5_P, 2 * S5_BS)
    cc = jnp.stack([c_re, -c_im], axis=0).reshape(2, S5_NB, S5_GB, S5_P, S5_N)
    ccm = jnp.einsum("rbgpn,gh->brhngp", cc, eye).reshape(S5_NB, 2 * S5_BS, S5_GB * S5_P)
    return bbm.astype(BF16), ccm.astype(BF16)


def _s5_block_diag_bb(m):
    t = m.reshape(S5_NB, S5_GB, S5_P, 2, S5_GB, S5_N)
    d = jnp.einsum("bgprgn->rbgnp", t).reshape(2, S5_G, S5_N, S5_P)
    return d[0], d[1]


def _s5_block_diag_cc(m):
    t = m.reshape(S5_NB, 2, S5_GB, S5_N, S5_GB, S5_P)
    d = jnp.einsum("brgngp->rbgpn", t).reshape(2, S5_G, S5_P, S5_N)
    return d[0], -d[1]


def _scan8(buf, row_off, tab_ref, carry_ref, t, reverse):
    nb = t // 8
    for blk in range(S5_NB):
        cr = slice(blk * 2 * S5_BS, blk * 2 * S5_BS + S5_BS)
        ci = slice(blk * 2 * S5_BS + S5_BS, (blk + 1) * 2 * S5_BS)

        def step(bi, carry, cr=cr, ci=ci):
            b = (nb - 1 - bi) if reverse else bi
            r0 = pl.multiple_of(b * 8, 8) + row_off
            xr = buf[pl.ds(r0, 8), cr]
            xi = buf[pl.ds(r0, 8), ci]
            for k, sh in enumerate((1, 2, 4)):
                shift = (8 - sh) if reverse else sh
                sr, si = pltpu.roll(xr, shift, 0), pltpu.roll(xi, shift, 0)
                tr, ti = tab_ref[k, :, cr], tab_ref[k, :, ci]
                xr, xi = xr + tr * sr - ti * si, xi + tr * si + ti * sr
            pr, pi_ = tab_ref[3, :, cr], tab_ref[3, :, ci]
            c_r, c_i = carry
            xr, xi = xr + pr * c_r - pi_ * c_i, xi + pr * c_i + pi_ * c_r
            buf[pl.ds(r0, 8), cr] = xr
            buf[pl.ds(r0, 8), ci] = xi
            edge = 0 if reverse else 7
            return (jnp.broadcast_to(xr[edge:edge + 1, :], xr.shape), jnp.broadcast_to(xi[edge:edge + 1, :], xi.shape))

        c_r, c_i = lax.fori_loop(0, nb, step, (carry_ref[:, cr], carry_ref[:, ci]))
        carry_ref[:, cr] = c_r
        carry_ref[:, ci] = c_i


def _s5_fwd(proj, bbm, ccm, d_row, glu_w, glu_b, tabs, t):
    l = proj.shape[0]
    nt = l // t

    def body(u_ref, gs_ref, bb_ref, cc_ref, d_ref, gw_ref, gb_ref, tab_ref, ssm_ref, xst_ref, xs, carry):
        @pl.when(pl.program_id(0) == 0)
        def _():
            carry[...] = jnp.zeros_like(carry)

        xst_ref[0] = carry[0:1, :]
        u = u_ref[...]
        ub = u.astype(BF16)
        for blk in range(S5_NB):
            xs[:, blk * 2 * S5_BS:(blk + 1) * 2 * S5_BS] = _dot(ub[:, blk * 128:(blk + 1) * 128], bb_ref[blk])
        _scan8(xs, 0, tab_ref, carry, t, False)
        ys = jnp.concatenate(
            [_dot(xs[:, blk * 2 * S5_BS:(blk + 1) * 2 * S5_BS].astype(BF16), cc_ref[blk]) for blk in range(S5_NB)], axis=1)
        y2 = _gelu(ys + d_ref[...] * u)
        z = _dot(y2.astype(BF16), gw_ref[...]) + gb_ref[...]
        ssm_ref[...] = (y2 * _sigmoid(z) * _silu(gs_ref[...])).astype(ssm_ref.dtype)

    const2 = lambda shape: pl.BlockSpec(shape, lambda i: (0,) * len(shape))
    return pl.pallas_call(
        body, name="s5_fwd", grid=(nt,),
        in_specs=[pl.BlockSpec((t, D_MODEL), lambda i: (i, 3)), pl.BlockSpec((t, D_MODEL), lambda i: (i, 4)),
                  const2(bbm.shape), const2(ccm.shape), const2((1, D_MODEL)), const2((D_MODEL, D_MODEL)),
                  const2((1, D_MODEL)), const2((4, 8, S5_COLS))],
        out_specs=(pl.BlockSpec((t, D_MODEL), lambda i: (i, 0)), pl.BlockSpec((1, 1, S5_COLS), lambda i: (i, 0, 0))),
        out_shape=(jax.ShapeDtypeStruct((l, D_MODEL), BF16), jax.ShapeDtypeStruct((nt, 1, S5_COLS), F32)),
        scratch_shapes=[pltpu.VMEM((t, S5_COLS), F32), pltpu.VMEM((8, S5_COLS), F32)],
        compiler_params=_cp("arbitrary"),
    )(proj, proj, bbm, ccm, d_row, glu_w, glu_b, tabs)


def _s5_bwd(proj, dmix, xstart, bbm, ccm, d_row, glu_w, glu_b, tabs_f, tabs_r, t):
    l = proj.shape[0]
    nt = l // t

    def body(u_ref, gs_ref, dm_ref, xst_ref, bb_ref, cc_ref, d_ref, gw_ref, gb_ref, tf_ref, tr_ref,
             dug_ref, y2_ref, dz_ref, dbb_ref, dcc_ref, da_ref, dd_ref, dgb_ref, xs, lam, carry, lcarry):
        @pl.when(pl.program_id(0) == 0)
        def _():
            lcarry[...] = jnp.zeros_like(lcarry)
            dbb_ref[...] = jnp.zeros_like(dbb_ref)
            dcc_ref[...] = jnp.zeros_like(dcc_ref)
            da_ref[...] = jnp.zeros_like(da_ref)
            dd_ref[...] = jnp.zeros_like(dd_ref)
            dgb_ref[...] = jnp.zeros_like(dgb_ref)

        carry[...] = jnp.broadcast_to(xst_ref[0], carry.shape)
        xs[0:8, :] = carry[...]
        u = u_ref[...]
        ub = u.astype(BF16)
        for blk in range(S5_NB):
            xs[8:, blk * 2 * S5_BS:(blk + 1) * 2 * S5_BS] = _dot(ub[:, blk * 128:(blk + 1) * 128], bb_ref[blk])
        _scan8(xs, 8, tf_ref, carry, t, False)
        ys = jnp.concatenate(
            [_dot(xs[8:, blk * 2 * S5_BS:(blk + 1) * 2 * S5_BS].astype(BF16), cc_ref[blk]) for blk in range(S5_NB)],
            axis=1)
        dv = d_ref[...]
        y1 = ys + dv * u
        y2 = _gelu(y1)
        y2b = y2.astype(BF16)
        sg = _sigmoid(_dot(y2b, gw_ref[...]) + gb_ref[...])
        gs = gs_ref[...]
        dssm = dm_ref[...]
        dug_ref[:, D_MODEL:] = (dssm * (y2 * sg) * _dsilu(gs)).astype(dug_ref.dtype)
        dy3 = dssm * _silu(gs)
        dz = dy3 * y2 * sg * (1.0 - sg)
        dzb = dz.astype(BF16)
        y2_ref[...] = y2b
        dz_ref[...] = dzb
        dgb_ref[...] += jnp.sum(dz, axis=0, keepdims=True)
        dy1 = (dy3 * sg + _dot_nt(dzb, gw_ref[...])) * _dgelu(y1)
        dd_ref[...] += jnp.sum(dy1 * u, axis=0, keepdims=True)
        dyb = dy1.astype(BF16)
        for blk in range(S5_NB):
            cols = slice(blk * 2 * S5_BS, (blk + 1) * 2 * S5_BS)
            ch = slice(blk * 128, (blk + 1) * 128)
            lam[:, cols] = _dot_nt(dyb[:, ch], cc_ref[blk])
            dcc_ref[blk] += _dot_tn(xs[8:, cols].astype(BF16), dyb[:, ch])
        _scan8(lam, 0, tr_ref, lcarry, t, True)
        nb = t // 8
        for blk in range(S5_NB):
            cr = slice(blk * 2 * S5_BS, blk * 2 * S5_BS + S5_BS)
            ci = slice(blk * 2 * S5_BS + S5_BS, (blk + 1) * 2 * S5_BS)

            def acc_step(b, acc, cr=cr, ci=ci):
                r0 = pl.multiple_of(b * 8, 8)
                row = lax.broadcasted_iota(jnp.int32, (8, S5_BS), 0)

                def prev(cols):
                    cur = xs[pl.ds(r0 + 8, 8), cols]
                    before = xs[pl.ds(r0, 8), cols]
                    return jnp.where(row == 0, jnp.broadcast_to(before[7:8, :], cur.shape), pltpu.roll(cur, 1, 0))

                pr, pi_ = prev(cr), prev(ci)
                lr, li = lam[pl.ds(r0, 8), cr], lam[pl.ds(r0, 8), ci]
                return (acc[0] + lr * pr + li * pi_, acc[1] + li * pr - lr * pi_)

            a_r, a_i = lax.fori_loop(0, nb, acc_step, (da_ref[:, cr], da_ref[:, ci]))
            da_ref[:, cr] = a_r
            da_ref[:, ci] = a_i
        du = []
        for blk in range(S5_NB):
            cols = slice(blk * 2 * S5_BS, (blk + 1) * 2 * S5_BS)
            lb = lam[:, cols].astype(BF16)
            du.append(_dot_nt(lb, bb_ref[blk]))
            dbb_ref[blk] += _dot_tn(ub[:, blk * 128:(blk + 1) * 128], lb)
        dug_ref[:, :D_MODEL] = (jnp.concatenate(du, axis=1) + dy1 * dv).astype(dug_ref.dtype)

    rev = lambda i: nt - 1 - i
    const2 = lambda shape: pl.BlockSpec(shape, lambda i: (0,) * len(shape))
    row_out = lambda w: pl.BlockSpec((t, w), lambda i: (rev(i), 0))
    return pl.pallas_call(
        body, name="s5_bwd", grid=(nt,),
        in_specs=[pl.BlockSpec((t, D_MODEL), lambda i: (rev(i), 3)), pl.BlockSpec((t, D_MODEL), lambda i: (rev(i), 4)),
                  pl.BlockSpec((t, D_MODEL), lambda i: (rev(i), 1)),
                  pl.BlockSpec((1, 1, S5_COLS), lambda i: (rev(i), 0, 0)),
                  const2(bbm.shape), const2(ccm.shape), const2((1, D_MODEL)), const2((D_MODEL, D_MODEL)),
                  const2((1, D_MODEL)), const2((4, 8, S5_COLS)), const2((4, 8, S5_COLS))],
        out_specs=(row_out(2 * D_MODEL), row_out(D_MODEL), row_out(D_MODEL), const2(bbm.shape), const2(ccm.shape),
                   const2((8, S5_COLS)), const2((1, D_MODEL)), const2((1, D_MODEL))),
        out_shape=(jax.ShapeDtypeStruct((l, 2 * D_MODEL), BF16), jax.ShapeDtypeStruct((l, D_MODEL), BF16),
                   jax.ShapeDtypeStruct((l, D_MODEL), BF16), jax.ShapeDtypeStruct(bbm.shape, F32),
                   jax.ShapeDtypeStruct(ccm.shape, F32), jax.ShapeDtypeStruct((8, S5_COLS), F32),
                   jax.ShapeDtypeStruct((1, D_MODEL), F32), jax.ShapeDtypeStruct((1, D_MODEL), F32)),
        scratch_shapes=[pltpu.VMEM((t + 8, S5_COLS), F32), pltpu.VMEM((t, S5_COLS), F32),
                        pltpu.VMEM((8, S5_COLS), F32), pltpu.VMEM((8, S5_COLS), F32)],
        compiler_params=_cp("arbitrary"),
    )(proj, proj, dmix, xstart, bbm, ccm, d_row, glu_w, glu_b, tabs_f, tabs_r)


def _attn_probs(qh, kh):
    s = _dot_nt(qh, kh) * (XA_DH ** -0.5)
    e = jnp.exp(s - jnp.max(s, axis=-1, keepdims=True))
    return e / jnp.sum(e, axis=-1, keepdims=True)


def _attn_fwd(qa, ka, va):
    l = qa.shape[0]
    m = ka.shape[0]
    tl = _pick(l, (512, 256))

    def body(q_ref, k_ref, v_ref, o_ref):
        for h in range(XA_HEADS):
            hs = slice(h * XA_DH, (h + 1) * XA_DH)
            p = _attn_probs(q_ref[:, hs], k_ref[:, hs])
            o_ref[:, hs] = _dot(p.astype(BF16), v_ref[:, hs]).astype(o_ref.dtype)

    return pl.pallas_call(
        body, name="xattn_fwd", grid=(l // tl,),
        in_specs=[pl.BlockSpec((tl, D_MODEL), lambda i: (i, 0)), pl.BlockSpec((m, D_MODEL), lambda i: (0, 0)),
                  pl.BlockSpec((m, D_MODEL), lambda i: (0, 0))],
        out_specs=pl.BlockSpec((tl, D_MODEL), lambda i: (i, 0)),
        out_shape=jax.ShapeDtypeStruct((l, D_MODEL), BF16), compiler_params=_cp("parallel"),
    )(qa, ka, va)


def _attn_bwd(qa, ka, va, doa):
    l = qa.shape[0]
    m = ka.shape[0]
    tl = _pick(l, (512, 256))

    def body(q_ref, k_ref, v_ref, do_ref, dq_ref, dk_ref, dv_ref):
        @pl.when(pl.program_id(0) == 0)
        def _():
            dk_ref[...] = jnp.zeros_like(dk_ref)
            dv_ref[...] = jnp.zeros_like(dv_ref)

        for h in range(XA_HEADS):
            hs = slice(h * XA_DH, (h + 1) * XA_DH)
            qh, kh, vh, doh = q_ref[:, hs], k_ref[:, hs], v_ref[:, hs], do_ref[:, hs]
            p = _attn_probs(qh, kh)
            dv_ref[:, hs] += _dot_tn(p.astype(BF16), doh)
            dp = _dot_nt(doh, vh)
            ds = (p * (dp - jnp.sum(dp * p, axis=-1, keepdims=True)) * (XA_DH ** -0.5)).astype(BF16)
            dq_ref[:, hs] = _dot(ds, kh).astype(dq_ref.dtype)
            dk_ref[:, hs] += _dot_tn(ds, qh)

    row = pl.BlockSpec((tl, D_MODEL), lambda i: (i, 0))
    mem = pl.BlockSpec((m, D_MODEL), lambda i: (0, 0))
    return pl.pallas_call(
        body, name="xattn_bwd", grid=(l // tl,), in_specs=[row, mem, mem, row], out_specs=(row, mem, mem),
        out_shape=(jax.ShapeDtypeStruct((l, D_MODEL), BF16), jax.ShapeDtypeStruct((m, D_MODEL), F32),
                   jax.ShapeDtypeStruct((m, D_MODEL), F32)),
        compiler_params=_cp("arbitrary"),
    )(qa, ka, va, doa)


def _me_and_peers():
    x, y, c = lax.axis_index("x"), lax.axis_index("y"), lax.axis_index("c")
    flip = lambda v, bit: (1 - v) if bit else v
    peers = []
    for k in range(1, N_DEV):
        px, py, pc = flip(x, (k >> 2) & 1), flip(y, (k >> 1) & 1), flip(c, k & 1)
        peers.append(((px, py, pc), 4 * px + 2 * py + pc))
    return 4 * x + 2 * y + c, peers


class _SideJob:
    def __init__(self, srcs, landing, src_of, dst_of):
        self.srcs = list(srcs)
        self.landing = list(landing)
        self.n = len(self.landing)
        self.src_of, self.dst_of = src_of, dst_of
        hbm = pl.BlockSpec(memory_space=pl.ANY)
        self.in_specs = [hbm] * len(self.srcs)
        self.out_specs = [hbm] * self.n
        self.scratch = [pltpu.SemaphoreType.DMA((self.n * (N_DEV - 1),)), pltpu.SemaphoreType.DMA((self.n * (N_DEV - 1),)),
                        pltpu.SemaphoreType.DMA((self.n,))]

    def _copies(self, src_refs, out_refs, sems):
        send_sems, recv_sems, loc_sems = sems
        me, peers = _me_and_peers()
        local = [pltpu.make_async_copy(self.src_of(a, me, src_refs), self.dst_of(a, me, out_refs), loc_sems.at[a])
                 for a in range(self.n)]
        sends, recvs = [], []
        for k, (peer, peer_idx) in enumerate(peers):
            for a in range(self.n):
                s = self.n * k + a
                sends.append(pltpu.make_async_remote_copy(
                    src_ref=self.src_of(a, peer_idx, src_refs), dst_ref=self.dst_of(a, me, out_refs),
                    send_sem=send_sems.at[s], recv_sem=recv_sems.at[s], device_id=peer, device_id_type=MESH))
                recvs.append(pltpu.make_async_remote_copy(
                    src_ref=self.src_of(a, me, src_refs), dst_ref=self.dst_of(a, peer_idx, out_refs),
                    send_sem=send_sems.at[s], recv_sem=recv_sems.at[s], device_id=peer, device_id_type=MESH))
        return local, sends, recvs

    def start(self, src_refs, out_refs, sems):
        local, sends, _ = self._copies(src_refs, out_refs, sems)
        for cp in local + sends:
            cp.start()

    def wait(self, src_refs, out_refs, sems):
        local, sends, recvs = self._copies(src_refs, out_refs, sems)
        for cp in recvs:
            cp.wait_recv()
        for cp in sends:
            cp.wait_send()
        for cp in local:
            cp.wait()


def _gather_job(shards):
    return _SideJob(shards, [jax.ShapeDtypeStruct((N_DEV,) + s.shape, s.dtype) for s in shards],
                    src_of=lambda a, j, srcs: srcs[a], dst_of=lambda a, j, outs: outs[a].at[j])


def _scatter_job(grads):
    landing, parts = [], []
    for g in grads:
        if g.ndim == 3:
            landing.append(jax.ShapeDtypeStruct(g.shape, g.dtype))
            parts.append(None)
        else:
            r = g.shape[0] // N_DEV
            landing.append(jax.ShapeDtypeStruct((N_DEV, r, g.shape[1]), g.dtype))
            parts.append(r)

    def src_of(a, j, srcs):
        if parts[a] is None:
            return srcs[a].at[j]
        return srcs[a].at[pl.ds(pl.multiple_of(j * parts[a], 8), parts[a]), :]

    return _SideJob(grads, landing, src_of=src_of, dst_of=lambda a, j, outs: outs[a].at[j])


def _allgather_w_in(w_in_shard, row_shards):
    n_row = len(row_shards)
    job = _gather_job([jax.ShapeDtypeStruct(w_in_shard.shape, BF16)])

    def body(*refs):
        win_ref = refs[0]
        row_refs = refs[1:1 + n_row]
        out_win = refs[1 + n_row]
        row_outs = refs[2 + n_row:2 + 2 * n_row]
        win_b = refs[2 + 2 * n_row]
        sems = refs[3 + 2 * n_row:]
        win_b[...] = win_ref[...].astype(BF16)
        job.start([win_b], [out_win], sems)
        for r, o in zip(row_refs, row_outs):
            o[...] = r[...].astype(BF16)
        job.wait([win_b], [out_win], sems)

    vm = pl.BlockSpec(memory_space=pltpu.VMEM)
    return pl.pallas_call(
        body, name="allgather_w_in", in_specs=[vm] * (1 + n_row), out_specs=(job.out_specs[0], *([vm] * n_row)),
        out_shape=(job.landing[0], *[jax.ShapeDtypeStruct(r.shape, BF16) for r in row_shards]),
        scratch_shapes=[pltpu.VMEM(w_in_shard.shape, BF16)] + job.scratch,
        compiler_params=pltpu.CompilerParams(vmem_limit_bytes=VMEM_LIMIT),
    )(w_in_shard, *row_shards)


def _allreduce_small(small):
    rows = SMALL_ROWS // N_DEV

    def body(x_ref, out_ref, land, send1, recv1, send2, recv2):
        me, peers = _me_and_peers()
        block = lambda j: pl.ds(pl.multiple_of(j * rows, 8), rows)

        def phase(src_of, dst_of, send_sems, recv_sems):
            sends = [pltpu.make_async_remote_copy(src_ref=src_of(pidx), dst_ref=dst_of(me), send_sem=send_sems.at[k],
                                                  recv_sem=recv_sems.at[k], device_id=peer, device_id_type=MESH)
                     for k, (peer, pidx) in enumerate(peers)]
            recvs = [pltpu.make_async_remote_copy(src_ref=src_of(me), dst_ref=dst_of(pidx), send_sem=send_sems.at[k],
                                                  recv_sem=recv_sems.at[k], device_id=peer, device_id_type=MESH)
                     for k, (peer, pidx) in enumerate(peers)]
            for cp in sends:
                cp.start()
            for cp in recvs:
                cp.wait_recv()
            for cp in sends:
                cp.wait_send()

        land[me] = x_ref[block(me), :]
        phase(lambda j: x_ref.at[block(j), :], lambda j: land.at[j], send1, recv1)
        total = land[0]
        for j in range(1, N_DEV):
            total = total + land[j]
        out_ref[block(me), :] = total
        phase(lambda j: out_ref.at[block(me), :], lambda j: out_ref.at[block(j), :], send2, recv2)

    vm = pl.BlockSpec(memory_space=pltpu.VMEM)
    return pl.pallas_call(
        body, name="allreduce_small", in_specs=[vm], out_specs=vm, out_shape=jax.ShapeDtypeStruct(small.shape, F32),
        scratch_shapes=[pltpu.VMEM((N_DEV, rows, D_MODEL), F32)] + [pltpu.SemaphoreType.DMA((N_DEV - 1,))] * 4,
    )(small)


def _adamw(name, got, w, m, v):
    r, c = w.shape
    n_slots = got.shape[0]
    tr = _pick(r, (256, 128, 64))

    def body(got_ref, w_ref, m_ref, v_ref, g_ref, d_ref, nm_ref, nv_ref):
        g = got_ref[0].astype(F32)
        for j in range(1, n_slots):
            g = g + got_ref[j].astype(F32)
        nm = ADAM_B1 * m_ref[...] + (1.0 - ADAM_B1) * g
        nv = ADAM_B2 * v_ref[...] + (1.0 - ADAM_B2) * jnp.square(g)
        m_hat = nm / (1.0 - ADAM_B1 ** ADAM_STEP)
        v_hat = nv / (1.0 - ADAM_B2 ** ADAM_STEP)
        g_ref[...] = g
        d_ref[...] = -ADAM_LR * (m_hat / (jnp.sqrt(v_hat) + ADAM_EPS) + ADAM_WD * w_ref[...])
        nm_ref[...] = nm
        nv_ref[...] = nv

    blk = pl.BlockSpec((tr, c), lambda i: (i, 0))
    out = jax.ShapeDtypeStruct((r, c), F32)
    return pl.pallas_call(
        body, name=name, grid=(r // tr,),
        in_specs=[pl.BlockSpec((n_slots, tr, c), lambda i: (0, i, 0)), blk, blk, blk],
        out_specs=(blk, blk, blk, blk), out_shape=(out, out, out, out), compiler_params=_cp("parallel"),
    )(got, w, m, v)


_SMALL_VECS = ("norm1_g", "ret_gn_g", "s5_d", "s5_glu_b", "norm2_g", "norm_mem_g", "norm_f_g")
_SMALL_ORDER = _SMALL_VECS + ("s5_a_re", "s5_a_im", "s5_log_dt", "s5_b_re", "s5_b_im", "s5_c_re", "s5_c_im")


def _pack_small(t, extra_row=None):
    rows = [t[n].reshape(1, D_MODEL) for n in _SMALL_VECS]
    rows.append(jnp.zeros((1, D_MODEL), F32) if extra_row is None else extra_row)
    rows += [t["s5_a_re"].reshape(4, D_MODEL), t["s5_a_im"].reshape(4, D_MODEL)]
    rows.append(jnp.pad(t["s5_log_dt"].reshape(1, S5_G), ((0, 7), (0, D_MODEL - S5_G))))
    rows += [t[n].reshape(64, D_MODEL) for n in ("s5_b_re", "s5_b_im", "s5_c_re", "s5_c_im")]
    rows.append(jnp.zeros((SMALL_ROWS - 280, D_MODEL), F32))
    return jnp.concatenate(rows, axis=0)


def _unpack_small(p, shapes):
    out = {n: p[i].reshape(shapes[n]) for i, n in enumerate(_SMALL_VECS)}
    out["s5_a_re"] = p[8:12].reshape(shapes["s5_a_re"])
    out["s5_a_im"] = p[12:16].reshape(shapes["s5_a_im"])
    out["s5_log_dt"] = p[16, :S5_G].reshape(shapes["s5_log_dt"])
    for i, n in enumerate(("s5_b_re", "s5_b_im", "s5_c_re", "s5_c_im")):
        out[n] = p[24 + 64 * i:24 + 64 * (i + 1)].reshape(shapes[n])
    return out


_W_NAMES = ("norm1_g", "w_in", "ret_gn_g", "s5_a_re", "s5_a_im", "s5_log_dt", "s5_b_re", "s5_b_im", "s5_c_re", "s5_c_im",
            "s5_d", "s5_glu_w", "s5_glu_b", "w_out", "norm2_g", "norm_mem_g", "xa_wq", "xa_wk", "xa_wv", "xa_wo",
            "norm_f_g")
_ROW_NAMES = ("s5_glu_w", "w_out", "xa_wq", "xa_wk", "xa_wv", "xa_wo")


def kernel(x, mem, positions, norm1_g, w_in, ret_gn_g, s5_a_re, s5_a_im, s5_log_dt, s5_b_re, s5_b_im, s5_c_re, s5_c_im, s5_d, s5_glu_w, s5_glu_b, w_out, norm2_g, norm_mem_g, xa_wq, xa_wk, xa_wv, xa_wo, norm_f_g, loss_target, m_norm1_g, m_w_in, m_ret_gn_g, m_s5_a_re, m_s5_a_im, m_s5_log_dt, m_s5_b_re, m_s5_b_im, m_s5_c_re, m_s5_c_im, m_s5_d, m_s5_glu_w, m_s5_glu_b, m_w_out, m_norm2_g, m_norm_mem_g, m_xa_wq, m_xa_wk, m_xa_wv, m_xa_wo, m_norm_f_g, v_norm1_g, v_w_in, v_ret_gn_g, v_s5_a_re, v_s5_a_im, v_s5_log_dt, v_s5_b_re, v_s5_b_im, v_s5_c_re, v_s5_c_im, v_s5_d, v_s5_glu_w, v_s5_glu_b, v_w_out, v_norm2_g, v_norm_mem_g, v_xa_wq, v_xa_wk, v_xa_wv, v_xa_wo, v_norm_f_g):
    w = dict(norm1_g=norm1_g, w_in=w_in, ret_gn_g=ret_gn_g, s5_a_re=s5_a_re, s5_a_im=s5_a_im, s5_log_dt=s5_log_dt,
             s5_b_re=s5_b_re, s5_b_im=s5_b_im, s5_c_re=s5_c_re, s5_c_im=s5_c_im, s5_d=s5_d, s5_glu_w=s5_glu_w,
             s5_glu_b=s5_glu_b, w_out=w_out, norm2_g=norm2_g, norm_mem_g=norm_mem_g, xa_wq=xa_wq, xa_wk=xa_wk,
             xa_wv=xa_wv, xa_wo=xa_wo, norm_f_g=norm_f_g)
    mom = dict(norm1_g=m_norm1_g, w_in=m_w_in, ret_gn_g=m_ret_gn_g, s5_a_re=m_s5_a_re, s5_a_im=m_s5_a_im,
               s5_log_dt=m_s5_log_dt, s5_b_re=m_s5_b_re, s5_b_im=m_s5_b_im, s5_c_re=m_s5_c_re, s5_c_im=m_s5_c_im,
               s5_d=m_s5_d, s5_glu_w=m_s5_glu_w, s5_glu_b=m_s5_glu_b, w_out=m_w_out, norm2_g=m_norm2_g,
               norm_mem_g=m_norm_mem_g, xa_wq=m_xa_wq, xa_wk=m_xa_wk, xa_wv=m_xa_wv, xa_wo=m_xa_wo,
               norm_f_g=m_norm_f_g)
    var = dict(norm1_g=v_norm1_g, w_in=v_w_in, ret_gn_g=v_ret_gn_g, s5_a_re=v_s5_a_re, s5_a_im=v_s5_a_im,
               s5_log_dt=v_s5_log_dt, s5_b_re=v_s5_b_re, s5_b_im=v_s5_b_im, s5_c_re=v_s5_c_re, s5_c_im=v_s5_c_im,
               s5_d=v_s5_d, s5_glu_w=v_s5_glu_w, s5_glu_b=v_s5_glu_b, w_out=v_w_out, norm2_g=v_norm2_g,
               norm_mem_g=v_norm_mem_g, xa_wq=v_xa_wq, xa_wk=v_xa_wk, xa_wv=v_xa_wv, xa_wo=v_xa_wo,
               norm_f_g=v_norm_f_g)
    shapes = {n: w[n].shape for n in _W_NAMES}

    x2d, mem2d, tgt = x[0], mem[0], loss_target[0]
    l = x2d.shape[0]
    ret_c = _pick(l, (128,))
    s5_t = _pick(l, (128,))
    g1, g2, gm, gf = norm1_g, norm2_g, norm_mem_g, norm_f_g.reshape(1, D_MODEL)

    win_s, *row_shards_b = _allgather_w_in(w_in[0], [w[n][0] for n in _ROW_NAMES])

    disc_args = (s5_a_re[0], s5_a_im[0], s5_log_dt[0], s5_b_re[0], s5_b_im[0])
    (abar_re, abar_im, bb_re, bb_im), disc_vjp = jax.vjp(_s5_discretize, *disc_args)
    bbm, ccm = _s5_block_mats(bb_re, bb_im, s5_c_re[0], s5_c_im[0])
    tabs = _s5_scan_tables(abar_re, abar_im)

    h1 = _rms_fwd("norm1_fwd", x2d, g1)
    proj, *rows_all = _mm_nn_slots("in_proj", h1, win_s, F32, side=_gather_job(row_shards_b))
    full = {n: g.reshape(N_DEV * r, D_MODEL) for n, g, r in zip(_ROW_NAMES, rows_all, ROW_SHARDS)}
    half = RET_DK // 2
    inv = ROPE_BASE ** (-jnp.arange(half, dtype=F32) / half)
    cos_t, sin_t = _rope_tables(positions[0].reshape(l, 1), jnp.tile(inv, 128 // half)[None, :])
    rconsts = _ret_constants(ret_c)
    ret, o_saved, r_prev = _ret_fwd(proj, cos_t, sin_t, rconsts, ret_gn_g, ret_c)
    ssm, xstart = _s5_fwd(proj, bbm, ccm, s5_d, full["s5_glu_w"], s5_glu_b, tabs[0], s5_t)
    mix = jnp.concatenate([ret, ssm], axis=1)
    x1 = _mm_nn("out_proj", mix, full["w_out"], F32, residual=x2d)
    h2 = _rms_fwd("norm2_fwd", x1, g2)
    mn = _rms_fwd("norm_mem_fwd", mem2d, gm)
    qa = _mm_nn("xa_q", h2, full["xa_wq"], BF16)
    ka = _mm_nn("xa_k", mn, full["xa_wk"], BF16)
    va = _mm_nn("xa_v", mn, full["xa_wv"], BF16)
    oa = _attn_fwd(qa, ka, va)
    x2 = _mm_nn("xa_o", oa, full["xa_wo"], F32, residual=x1)
    dx2, dgf, loss_lanes = _loss_head(x2, gf, tgt)

    doa = _mm_nt("xa_o_dx", dx2, full["xa_wo"], BF16)
    dwo = _mm_tn("xa_o_dw", oa, dx2, BF16)
    dqa, dka, dva = _attn_bwd(qa, ka, va, doa)
    dh2 = _mm_nt("xa_q_dx", dqa, full["xa_wq"], F32)
    dwq = _mm_tn("xa_q_dw", h2, dqa, BF16)
    dx1, dg2 = _rms_bwd("norm2_bwd", x1, g2, dh2, dx2)
    dwk = _mm_tn("xa_k_dw", mn, dka, BF16)
    dwv = _mm_tn("xa_v_dw", mn, dva, BF16)
    dmn = _mm_nt("xa_v_dx", dva, full["xa_wv"], F32, residual=_mm_nt("xa_k_dx", dka, full["xa_wk"], F32))
    _, dgm = _rms_bwd("norm_mem_bwd", mem2d, gm, dmn, None)
    dmix = _mm_nt("out_proj_dx", dx1, full["w_out"], F32)
    dwout = _mm_tn("out_proj_dw", mix, dx1, BF16)
    dret, dgn, *got_a = _ret_bwd(proj, cos_t, sin_t, rconsts, ret_gn_g, o_saved, r_prev, dmix, ret_c,
                                 side=_scatter_job([dwout, dwq, dwk, dwv, dwo]))
    dug, y2, dz, dbbm, dccm, dabar, dd, dgb = _s5_bwd(proj, dmix, xstart, bbm, ccm, s5_d, full["s5_glu_w"], s5_glu_b,
                                                      tabs[0], tabs[1], s5_t)
    dglu = _mm_tn("s5_glu_dw", y2, dz, BF16)
    dproj = jnp.concatenate([dret, dug], axis=1)
    dwin_s = _mm_tn_slots("in_proj_dw", h1, dproj, N_DEV, BF16)
    dh1, got_win, got_glu = _mm_nt_slots("in_proj_dx", dproj, win_s, F32, side=_scatter_job([dwin_s, dglu]))
    grad_x, dg1 = _rms_bwd("norm1_bwd", x2d, g1, dh1, dx1)

    dab_re, dab_im = _s5_uncols(jnp.sum(dabar, axis=0))
    dbb_re, dbb_im = _s5_block_diag_bb(dbbm)
    dc_re, dc_im = _s5_block_diag_cc(dccm)
    da_re, da_im, dlog_dt, db_re, db_im = disc_vjp((dab_re, dab_im, dbb_re, dbb_im))
    small_g = dict(norm1_g=dg1, ret_gn_g=dgn, s5_d=dd, s5_glu_b=dgb, norm2_g=dg2, norm_mem_g=dgm, norm_f_g=dgf,
                   s5_a_re=da_re, s5_a_im=da_im, s5_log_dt=dlog_dt, s5_b_re=db_re, s5_b_im=db_im, s5_c_re=dc_re,
                   s5_c_im=dc_im)
    small_pack = _pack_small(small_g, extra_row=loss_lanes)

    res = {}
    got = dict(zip(("w_out", "xa_wq", "xa_wk", "xa_wv", "xa_wo"), got_a), w_in=got_win, s5_glu_w=got_glu)
    for n in ("w_in",) + _ROW_NAMES:
        res[n] = _adamw("adamw_" + n, got[n], w[n][0], mom[n][0], var[n][0])
    small_sum = _allreduce_small(small_pack)
    small_out = _adamw("adamw_small", small_sum[None], _pack_small(w), _pack_small(mom), _pack_small(var))
    unpacked = [_unpack_small(a, shapes) for a in small_out]
    loss = (0.5 / D_MODEL) * jnp.sum(small_out[0][7])
    for n in _SMALL_ORDER:
        res[n] = tuple(u[n] for u in unpacked)

    outs = [loss, grad_x[None]]
    for part in range(4):
        for n in _W_NAMES:
            outs.append(res[n][part].reshape(shapes[n]))
    return tuple(outs)
```

```python
import functools

import jax
import jax.numpy as jnp
from jax import lax
from jax.experimental import pallas as pl
from jax.experimental.pallas import tpu as pltpu

F32 = jnp.float32
BF16 = jnp.bfloat16
MESH = pl.DeviceIdType.MESH

D_MODEL = 1024
RET_HEADS, RET_DK, RET_DV = 8, 64, 128
RET_QK = RET_HEADS * RET_DK
S5_G, S5_N, S5_P = 64, 64, 16
S5_NB = 8
S5_GB = S5_G // S5_NB
S5_BS = S5_GB * S5_N
S5_COLS = 2 * S5_G * S5_N
XA_HEADS, XA_DH = 4, 256
EPS = 1e-6
ROPE_BASE = 10000.0
N_DEV = 8
W_IN_SHARD = 640
ROW_SHARDS = (128, 256, 128, 128, 128, 128)
ROWPACK = sum(ROW_SHARDS)
SMALL_ROWS = 320
ADAM_LR, ADAM_B1, ADAM_B2, ADAM_EPS, ADAM_WD, ADAM_STEP = 0.001, 0.9, 0.999, 1e-08, 0.01, 10

VMEM_LIMIT = 56 * 1024 * 1024


def _cp(*sem):
    return pltpu.CompilerParams(dimension_semantics=tuple(sem), vmem_limit_bytes=VMEM_LIMIT)


def _dot(a, b):
    return jnp.dot(a, b, preferred_element_type=F32)


def _dot_nt(a, b):
    return lax.dot_general(a, b, (((1,), (1,)), ((), ())), preferred_element_type=F32)


def _dot_tn(a, b):
    return lax.dot_general(a, b, (((0,), (0,)), ((), ())), preferred_element_type=F32)


def _sigmoid(x):
    return 1.0 / (1.0 + jnp.exp(-x))


def _silu(x):
    return x * _sigmoid(x)


def _dsilu(x):
    s = _sigmoid(x)
    return s * (1.0 + x * (1.0 - s))


_GELU_C = 0.7978845608028654


def _gelu(x):
    return 0.5 * x * (1.0 + jnp.tanh(_GELU_C * (x + 0.044715 * (x * x * x))))


def _dgelu(x):
    t = jnp.tanh(_GELU_C * (x + 0.044715 * (x * x * x)))
    return 0.5 * (1.0 + t) + 0.5 * x * (1.0 - t * t) * (_GELU_C * (1.0 + 3.0 * 0.044715 * (x * x)))


def _pick(n, cands):
    for c in cands:
        if n % c == 0:
            return c
    return n


def _mm_core(name, operands, in_specs, out_spec, out_shape, grid, nk, dims, acc_shape, has_res, side=None):
    n_in = 3 if has_res else 2
    n_side_in = len(side.srcs) if side else 0
    n_side_out = side.n if side else 0

    def body(*refs):
        a_ref, b_ref = refs[0], refs[1]
        r_ref = refs[2] if has_res else None
        side_in = refs[n_in:n_in + n_side_in]
        o_ref = refs[n_in + n_side_in]
        side_out = refs[n_in + n_side_in + 1:n_in + n_side_in + 1 + n_side_out]
        acc = refs[n_in + n_side_in + 1 + n_side_out]
        sems = refs[n_in + n_side_in + 2 + n_side_out:]
        i, j, k = pl.program_id(0), pl.program_id(1), pl.program_id(2)
        if side:
            @pl.when((i == 0) & (j == 0) & (k == 0))
            def _():
                side.start(side_in, side_out, sems)

        @pl.when(k == 0)
        def _():
            acc[...] = jnp.zeros_like(acc)

        acc[...] += lax.dot_general(a_ref[...].astype(BF16), b_ref[...].astype(BF16), (dims, ((), ())),
                                    preferred_element_type=F32)

        @pl.when(k == nk - 1)
        def _():
            r = acc[...]
            if has_res:
                r = r + r_ref[...]
            o_ref[...] = r.astype(o_ref.dtype)

        if side:
            @pl.when((i == grid[0] - 1) & (j == grid[1] - 1) & (k == grid[2] - 1))
            def _():
                side.wait(side_in, side_out, sems)

    if side:
        return pl.pallas_call(
            body, name=name, grid=grid, in_specs=list(in_specs) + side.in_specs,
            out_specs=(out_spec, *side.out_specs), out_shape=(out_shape, *side.landing),
            scratch_shapes=[pltpu.VMEM(acc_shape, F32)] + side.scratch,
            compiler_params=_cp("arbitrary", "arbitrary", "arbitrary"),
        )(*operands, *side.srcs)
    return pl.pallas_call(
        body, name=name, grid=grid, in_specs=in_specs, out_specs=out_spec, out_shape=out_shape,
        scratch_shapes=[pltpu.VMEM(acc_shape, F32)],
        compiler_params=_cp("parallel", "parallel", "arbitrary"),
    )(*operands)


def _mm_nn(name, a, b, out_dtype, residual=None):
    m, kk = a.shape
    n = b.shape[1]
    tm, tn, tk = _pick(m, (1024, 512, 256)), _pick(n, (1024, 512)), _pick(kk, (1024, 512))
    ops = [a, b]
    specs = [pl.BlockSpec((tm, tk), lambda i, j, k: (i, k)), pl.BlockSpec((tk, tn), lambda i, j, k: (k, j))]
    if residual is not None:
        ops.append(residual)
        specs.append(pl.BlockSpec((tm, tn), lambda i, j, k: (i, j)))
    return _mm_core(name, ops, specs, pl.BlockSpec((tm, tn), lambda i, j, k: (i, j)),
                    jax.ShapeDtypeStruct((m, n), out_dtype), (m // tm, n // tn, kk // tk), kk // tk,
                    ((1,), (0,)), (tm, tn), residual is not None)


def _mm_nt(name, a, b, out_dtype, residual=None):
    m, kk = a.shape
    n = b.shape[0]
    tm, tn, tk = _pick(m, (1024, 512, 256)), _pick(n, (1024, 512)), _pick(kk, (1024, 512))
    ops = [a, b]
    specs = [pl.BlockSpec((tm, tk), lambda i, j, k: (i, k)), pl.BlockSpec((tn, tk), lambda i, j, k: (j, k))]
    if residual is not None:
        ops.append(residual)
        specs.append(pl.BlockSpec((tm, tn), lambda i, j, k: (i, j)))
    return _mm_core(name, ops, specs, pl.BlockSpec((tm, tn), lambda i, j, k: (i, j)),
                    jax.ShapeDtypeStruct((m, n), out_dtype), (m // tm, n // tn, kk // tk), kk // tk,
                    ((1,), (1,)), (tm, tn), residual is not None)


def _mm_tn(name, a, b, out_dtype):
    kk, m = a.shape
    n = b.shape[1]
    tm, tn, tk = _pick(m, (1024, 512)), _pick(n, (1024, 512)), _pick(kk, (1024, 512, 256))
    specs = [pl.BlockSpec((tk, tm), lambda i, j, k: (k, i)), pl.BlockSpec((tk, tn), lambda i, j, k: (k, j))]
    return _mm_core(name, [a, b], specs, pl.BlockSpec((tm, tn), lambda i, j, k: (i, j)),
                    jax.ShapeDtypeStruct((m, n), out_dtype), (m // tm, n // tn, kk // tk), kk // tk,
                    ((0,), (0,)), (tm, tn), False)


def _mm_nn_slots(name, a, b_slots, out_dtype, side=None):
    m, kk = a.shape
    s, _, ns = b_slots.shape
    tm, tk = _pick(m, (1024, 512, 256)), _pick(kk, (1024, 512))
    specs = [pl.BlockSpec((tm, tk), lambda i, j, k: (i, k)), pl.BlockSpec((None, tk, ns), lambda i, j, k: (j, k, 0))]
    return _mm_core(name, [a, b_slots], specs, pl.BlockSpec((tm, ns), lambda i, j, k: (i, j)),
                    jax.ShapeDtypeStruct((m, s * ns), out_dtype), (m // tm, s, kk // tk), kk // tk,
                    ((1,), (0,)), (tm, ns), False, side)


def _mm_nt_slots(name, a, b_slots, out_dtype, side=None):
    m = a.shape[0]
    s, n, ns = b_slots.shape
    tm, tn = _pick(m, (1024, 512, 256)), _pick(n, (1024, 512))
    specs = [pl.BlockSpec((tm, ns), lambda i, j, k: (i, k)), pl.BlockSpec((None, tn, ns), lambda i, j, k: (k, j, 0))]
    return _mm_core(name, [a, b_slots], specs, pl.BlockSpec((tm, tn), lambda i, j, k: (i, j)),
                    jax.ShapeDtypeStruct((m, n), out_dtype), (m // tm, n // tn, s), s,
                    ((1,), (1,)), (tm, tn), False, side)


def _mm_tn_slots(name, a, b, s, out_dtype):
    kk, m = a.shape
    ns = b.shape[1] // s
    tm, tk = _pick(m, (1024, 512)), _pick(kk, (1024, 512, 256))
    specs = [pl.BlockSpec((tk, tm), lambda i, j, k: (k, i)), pl.BlockSpec((tk, ns), lambda i, j, k: (k, j))]
    return _mm_core(name, [a, b], specs, pl.BlockSpec((None, tm, ns), lambda i, j, k: (j, i, 0)),
                    jax.ShapeDtypeStruct((s, m, ns), out_dtype), (m // tm, s, kk // tk), kk // tk,
                    ((0,), (0,)), (tm, ns), False)


def _rms_fwd(name, x, g):
    r, d = x.shape
    tr = _pick(r, (1024, 512, 256))

    def body(x_ref, g_ref, o_ref):
        xv = x_ref[...]
        rs = lax.rsqrt(jnp.mean(xv * xv, axis=-1, keepdims=True) + EPS)
        o_ref[...] = (xv * rs * g_ref[...]).astype(o_ref.dtype)

    return pl.pallas_call(
        body, name=name, grid=(r // tr,),
        in_specs=[pl.BlockSpec((tr, d), lambda i: (i, 0)), pl.BlockSpec((1, d), lambda i: (0, 0))],
        out_specs=pl.BlockSpec((tr, d), lambda i: (i, 0)),
        out_shape=jax.ShapeDtypeStruct((r, d), BF16), compiler_params=_cp("parallel"),
    )(x, g)


def _rms_bwd(name, x, g, dh, dres):
    r, d = x.shape
    tr = _pick(r, (512, 256))
    has_res = dres is not None

    def body(*refs):
        if has_res:
            x_ref, g_ref, dh_ref, dr_ref, dx_ref, dg_ref = refs
        else:
            x_ref, g_ref, dh_ref, dx_ref, dg_ref = refs
        i = pl.program_id(0)

        @pl.when(i == 0)
        def _():
            dg_ref[...] = jnp.zeros_like(dg_ref)

        xv = x_ref[...]
        dhv = dh_ref[...].astype(F32)
        rs = lax.rsqrt(jnp.mean(xv * xv, axis=-1, keepdims=True) + EPS)
        xn = xv * rs
        dg_ref[...] += jnp.sum(dhv * xn, axis=0, keepdims=True)
        dn = dhv * g_ref[...]
        dx = rs * (dn - xn * jnp.mean(dn * xn, axis=-1, keepdims=True))
        if has_res:
            dx = dx + dr_ref[...]
        dx_ref[...] = dx

    row = pl.BlockSpec((tr, d), lambda i: (i, 0))
    vec = pl.BlockSpec((1, d), lambda i: (0, 0))
    ops = [x, g, dh] + ([dres] if has_res else [])
    return pl.pallas_call(
        body, name=name, grid=(r // tr,),
        in_specs=[row, vec, row] + ([row] if has_res else []),
        out_specs=(row, vec),
        out_shape=(jax.ShapeDtypeStruct((r, d), F32), jax.ShapeDtypeStruct((1, d), F32)),
        compiler_params=_cp("arbitrary"),
    )(*ops)


def _loss_head(x2, gf, target):
    r, d = x2.shape
    tr = _pick(r, (512, 256))

    def body(x_ref, g_ref, t_ref, dx_ref, dg_ref, ls_ref):
        i = pl.program_id(0)

        @pl.when(i == 0)
        def _():
            dg_ref[...] = jnp.zeros_like(dg_ref)
            ls_ref[...] = jnp.zeros_like(ls_ref)

        xv = x_ref[...]
        rs = lax.rsqrt(jnp.mean(xv * xv, axis=-1, keepdims=True) + EPS)
        xn = xv * rs
        e = xn * g_ref[...] - t_ref[...]
        ls_ref[...] += jnp.sum(e * e, axis=0, keepdims=True)
        dy = e * (1.0 / d)
        dg_ref[...] += jnp.sum(dy * xn, axis=0, keepdims=True)
        dn = dy * g_ref[...]
        dx_ref[...] = rs * (dn - xn * jnp.mean(dn * xn, axis=-1, keepdims=True))

    row = pl.BlockSpec((tr, d), lambda i: (i, 0))
    vec = pl.BlockSpec((1, d), lambda i: (0, 0))
    return pl.pallas_call(
        body, name="loss_head", grid=(r // tr,), in_specs=[row, vec, row], out_specs=(row, vec, vec),
        out_shape=(jax.ShapeDtypeStruct((r, d), F32), jax.ShapeDtypeStruct((1, d), F32),
                   jax.ShapeDtypeStruct((1, d), F32)),
        compiler_params=_cp("arbitrary"),
    )(x2, gf, target)


def _rope_tables(pos_col, inv_row):
    l = pos_col.shape[0]
    tl = _pick(l, (1024, 512, 256))

    def body(p_ref, inv_ref, cos_ref, sin_ref):
        ang = p_ref[...].astype(F32) * inv_ref[...]
        lane = lax.broadcasted_iota(jnp.int32, ang.shape, 1)
        c = jnp.cos(ang)
        s = jnp.where((lane % RET_DK) < RET_DK // 2, -jnp.sin(ang), jnp.sin(ang))
        cos_ref[...] = jnp.tile(c, (1, RET_QK // 128))
        sin_ref[...] = jnp.tile(s, (1, RET_QK // 128))

    return pl.pallas_call(
        body, name="rope_tables", grid=(l // tl,),
        in_specs=[pl.BlockSpec((tl, 1), lambda i: (i, 0)), pl.BlockSpec((1, 128), lambda i: (0, 0))],
        out_specs=(pl.BlockSpec((tl, RET_QK), lambda i: (i, 0)), pl.BlockSpec((tl, RET_QK), lambda i: (i, 0))),
        out_shape=(jax.ShapeDtypeStruct((l, RET_QK), F32), jax.ShapeDtypeStruct((l, RET_QK), F32)),
        compiler_params=_cp("parallel"),
    )(pos_col, inv_row)


def _rot(x, cos_t, sin_t):
    n = x.shape[-1]
    lane = lax.broadcasted_iota(jnp.int32, x.shape, 1)
    partner = jnp.where((lane % RET_DK) < RET_DK // 2, pltpu.roll(x, n - RET_DK // 2, 1), pltpu.roll(x, RET_DK // 2, 1))
    return x * cos_t + partner * sin_t


def _ret_constants(c):
    log_g = jnp.log1p(-jnp.exp2(-5.0 - jnp.arange(RET_HEADS, dtype=F32)))
    j = jnp.arange(c, dtype=F32)
    diff = j[:, None] - j[None, :]
    decay = jnp.where(diff[None] >= 0.0, jnp.exp(log_g[:, None, None] * jnp.maximum(diff, 0.0)[None]), 0.0)
    q_w = jnp.exp(log_g[None, :] * (j + 1.0)[:, None])
    k_w = jnp.exp(log_g[None, :] * (c - 1.0 - j)[:, None])
    cd = jnp.exp(log_g * c)
    rep = lambda t: jnp.repeat(t, RET_DK, axis=1)
    cd_row = jnp.repeat(cd, RET_DV)[None, :]
    return decay, rep(q_w), rep(k_w), cd_row


def _ret_fwd(proj, cos_t, sin_t, consts, gn_g, c):
    l = proj.shape[0]
    nc = l // c
    decay, qw, kw, cd_row = consts

    def body(q_ref, k_ref, v_ref, g_ref, cos_ref, sin_ref, dec_ref, qw_ref, kw_ref, cd_ref, gn_ref,
             ret_ref, o_ref, rp_ref, state):
        @pl.when(pl.program_id(0) == 0)
        def _():
            state[...] = jnp.zeros_like(state)

        cs, sn = cos_ref[...], sin_ref[...]
        qr = _rot(q_ref[...], cs, sn)
        kr = _rot(k_ref[...], cs, sn) * (RET_DK ** -0.5)
        qb, kb = qr.astype(BF16), kr.astype(BF16)
        qwb = (qr * qw_ref[...]).astype(BF16)
        kwb = (kr * kw_ref[...]).astype(BF16)
        vb = v_ref[...].astype(BF16)
        for h in range(RET_HEADS):
            qs = slice(h * RET_DK, (h + 1) * RET_DK)
            vs = slice(h * RET_DV, (h + 1) * RET_DV)
            s = _dot_nt(qb[:, qs], kb[:, qs]) * dec_ref[h]
            r_prev = state[h]
            rp_ref[0, h] = r_prev
            o = _dot(s.astype(BF16), vb[:, vs]) + _dot(qwb[:, qs], r_prev.astype(BF16))
            state[h] = cd_ref[:, vs] * r_prev + _dot_tn(kwb[:, qs], vb[:, vs])
            o_ref[:, vs] = o
            mu = jnp.mean(o, axis=-1, keepdims=True)
            var = jnp.mean(jnp.square(o - mu), axis=-1, keepdims=True)
            on = (o - mu) * lax.rsqrt(var + EPS)
            ret_ref[:, vs] = (on * gn_ref[:, vs] * _silu(g_ref[:, vs])).astype(ret_ref.dtype)

    const2 = lambda shape: pl.BlockSpec(shape, lambda i: (0,) * len(shape))
    return pl.pallas_call(
        body, name="retention_fwd", grid=(nc,),
        in_specs=[pl.BlockSpec((c, RET_QK), lambda i: (i, 0)), pl.BlockSpec((c, RET_QK), lambda i: (i, 1)),
                  pl.BlockSpec((c, D_MODEL), lambda i: (i, 1)), pl.BlockSpec((c, D_MODEL), lambda i: (i, 2)),
                  pl.BlockSpec((c, RET_QK), lambda i: (i, 0)), pl.BlockSpec((c, RET_QK), lambda i: (i, 0)),
                  const2((RET_HEADS, c, c)), const2((c, RET_QK)), const2((c, RET_QK)), const2((1, D_MODEL)),
                  const2((1, D_MODEL))],
        out_specs=(pl.BlockSpec((c, D_MODEL), lambda i: (i, 0)), pl.BlockSpec((c, D_MODEL), lambda i: (i, 0)),
                   pl.BlockSpec((1, RET_HEADS, RET_DK, RET_DV), lambda i: (i, 0, 0, 0))),
        out_shape=(jax.ShapeDtypeStruct((l, D_MODEL), BF16), jax.ShapeDtypeStruct((l, D_MODEL), F32),
                   jax.ShapeDtypeStruct((nc, RET_HEADS, RET_DK, RET_DV), F32)),
        scratch_shapes=[pltpu.VMEM((RET_HEADS, RET_DK, RET_DV), F32)],
        compiler_params=_cp("arbitrary"),
    )(proj, proj, proj, proj, cos_t, sin_t, decay, qw, kw, cd_row, gn_g)


def _ret_bwd(proj, cos_t, sin_t, consts, gn_g, o_saved, r_prev_saved, dmix, c, side):
    l = proj.shape[0]
    nc = l // c
    decay, qw, kw, cd_row = consts
    n_in = 14

    def body(*refs):
        (q_ref, k_ref, v_ref, g_ref, cos_ref, sin_ref, dec_ref, qw_ref, kw_ref, cd_ref, gn_ref, o_ref, rp_ref,
         dr_ref) = refs[:n_in]
        side_in = refs[n_in:n_in + len(side.srcs)]
        out_ref, dgn_ref = refs[n_in + len(side.srcs):n_in + len(side.srcs) + 2]
        side_out = refs[n_in + len(side.srcs) + 2:n_in + len(side.srcs) + 2 + side.n]
        state, dq_s, dk_s = refs[n_in + len(side.srcs) + 2 + side.n:n_in + len(side.srcs) + 5 + side.n]
        sems = refs[n_in + len(side.srcs) + 5 + side.n:]

        @pl.when(pl.program_id(0) == 0)
        def _():
            side.start(side_in, side_out, sems)
            state[...] = jnp.zeros_like(state)
            dgn_ref[...] = jnp.zeros_like(dgn_ref)

        cs, sn = cos_ref[...], sin_ref[...]
        qr = _rot(q_ref[...], cs, sn)
        kr = _rot(k_ref[...], cs, sn) * (RET_DK ** -0.5)
        qb, kb = qr.astype(BF16), kr.astype(BF16)
        qwb = (qr * qw_ref[...]).astype(BF16)
        kwv = kw_ref[...]
        kwb = (kr * kwv).astype(BF16)
        qwv = qw_ref[...]
        vb = v_ref[...].astype(BF16)
        for h in range(RET_HEADS):
            qs = slice(h * RET_DK, (h + 1) * RET_DK)
            vs = slice(h * RET_DV, (h + 1) * RET_DV)
            dec = dec_ref[h]
            o = o_ref[:, vs]
            mu = jnp.mean(o, axis=-1, keepdims=True)
            var = jnp.mean(jnp.square(o - mu), axis=-1, keepdims=True)
            rstd = lax.rsqrt(var + EPS)
            on = (o - mu) * rstd
            gate = g_ref[:, vs]
            sg = _silu(gate)
            dret = dr_ref[:, vs]
            gn = gn_ref[:, vs]
            dgn_ref[:, vs] += jnp.sum(dret * on * sg, axis=0, keepdims=True)
            out_ref[:, 2 * RET_QK + D_MODEL + h * RET_DV:2 * RET_QK + D_MODEL + (h + 1) * RET_DV] = (
                dret * on * gn * _dsilu(gate)).astype(out_ref.dtype)
            don = dret * gn * sg
            do = rstd * (don - jnp.mean(don, axis=-1, keepdims=True)
                         - on * jnp.mean(don * on, axis=-1, keepdims=True))
            dob = do.astype(BF16)
            sn_h = state[h]
            snb = sn_h.astype(BF16)
            s = _dot_nt(qb[:, qs], kb[:, qs]) * dec
            dv = _dot_tn(s.astype(BF16), dob) + _dot(kwb[:, qs], snb)
            out_ref[:, 2 * RET_QK + h * RET_DV:2 * RET_QK + (h + 1) * RET_DV] = dv.astype(out_ref.dtype)
            ds = (_dot_nt(dob, vb[:, vs]) * dec).astype(BF16)
            dq_s[:, qs] = _dot(ds, kb[:, qs]) + qwv[:, qs] * _dot_nt(dob, rp_ref[0, h].astype(BF16))
            dk_s[:, qs] = _dot_tn(ds, qb[:, qs]) + kwv[:, qs] * _dot_nt(vb[:, vs], snb)
            state[h] = cd_ref[:, vs] * sn_h + _dot_tn(qwb[:, qs], dob)
        out_ref[:, 0:RET_QK] = _rot(dq_s[...], cs, -sn).astype(out_ref.dtype)
        out_ref[:, RET_QK:2 * RET_QK] = (_rot(dk_s[...], cs, -sn) * (RET_DK ** -0.5)).astype(out_ref.dtype)

        @pl.when(pl.program_id(0) == nc - 1)
        def _():
            side.wait(side_in, side_out, sems)

    rev = lambda i: nc - 1 - i
    const2 = lambda shape: pl.BlockSpec(shape, lambda i: (0,) * len(shape))
    return pl.pallas_call(
        body, name="retention_bwd", grid=(nc,),
        in_specs=[pl.BlockSpec((c, RET_QK), lambda i: (rev(i), 0)), pl.BlockSpec((c, RET_QK), lambda i: (rev(i), 1)),
                  pl.BlockSpec((c, D_MODEL), lambda i: (rev(i), 1)), pl.BlockSpec((c, D_MODEL), lambda i: (rev(i), 2)),
                  pl.BlockSpec((c, RET_QK), lambda i: (rev(i), 0)), pl.BlockSpec((c, RET_QK), lambda i: (rev(i), 0)),
                  const2((RET_HEADS, c, c)), const2((c, RET_QK)), const2((c, RET_QK)), const2((1, D_MODEL)),
                  const2((1, D_MODEL)),
                  pl.BlockSpec((c, D_MODEL), lambda i: (rev(i), 0)),
                  pl.BlockSpec((1, RET_HEADS, RET_DK, RET_DV), lambda i: (rev(i), 0, 0, 0)),
                  pl.BlockSpec((c, D_MODEL), lambda i: (rev(i), 0))] + side.in_specs,
        out_specs=(pl.BlockSpec((c, 2 * RET_QK + 2 * D_MODEL), lambda i: (rev(i), 0)), const2((1, D_MODEL)),
                   *side.out_specs),
        out_shape=(jax.ShapeDtypeStruct((l, 2 * RET_QK + 2 * D_MODEL), BF16), jax.ShapeDtypeStruct((1, D_MODEL), F32),
                   *side.landing),
        scratch_shapes=[pltpu.VMEM((RET_HEADS, RET_DK, RET_DV), F32), pltpu.VMEM((c, RET_QK), F32),
                        pltpu.VMEM((c, RET_QK), F32)] + side.scratch,
        compiler_params=_cp("arbitrary"),
    )(proj, proj, proj, proj, cos_t, sin_t, decay, qw, kw, cd_row, gn_g, o_saved, r_prev_saved, dmix, *side.srcs)


def _s5_discretize(a_re, a_im, log_dt, b_re, b_im):
    dt = jnp.exp(log_dt)[:, None]
    mag = jnp.exp(a_re * dt)
    abar_re = mag * jnp.cos(a_im * dt)
    abar_im = mag * jnp.sin(a_im * dt)
    den = a_re * a_re + a_im * a_im
    nr, ni = abar_re - 1.0, abar_im
    f_re = (nr * a_re + ni * a_im) / den
    f_im = (ni * a_re - nr * a_im) / den
    bb_re = f_re[..., None] * b_re - f_im[..., None] * b_im
    bb_im = f_re[..., None] * b_im + f_im[..., None] * b_re
    return abar_re, abar_im, bb_re, bb_im


S5_ZQ = S5_NB // 2


def _s5_z(re, im):
    return jnp.concatenate([re.reshape(S5_ZQ, 8, 128), im.reshape(S5_ZQ, 8, 128)], axis=0)


def _s5_unz(z):
    return z[:S5_ZQ].reshape(S5_G, S5_N), z[S5_ZQ:].reshape(S5_G, S5_N)


def _s5_block_mats(bb_re, bb_im, c_re, c_im):
    eye = jnp.eye(S5_GB, dtype=F32)
    bb = jnp.stack([bb_re, bb_im], axis=0).reshape(2, S5_NB, S5_GB, S5_N, S5_P)
    bbm = jnp.einsum("rbgnp,gh->bgprhn", bb, eye).reshape(S5_NB, S5_GB * S5_P, 2 * S5_BS)
    cc = jnp.stack([c_re, -c_im], axis=0).reshape(2, S5_NB, S5_GB, S5_P, S5_N)
    ccm = jnp.einsum("rbgpn,gh->brhngp", cc, eye).reshape(S5_NB, 2 * S5_BS, S5_GB * S5_P)
    return bbm.astype(BF16), ccm.astype(BF16)


def _s5_block_diag_bb(m):
    t = m.reshape(S5_NB, S5_GB, S5_P, 2, S5_GB, S5_N)
    d = jnp.einsum("bgprgn->rbgnp", t).reshape(2, S5_G, S5_N, S5_P)
    return d[0], d[1]


def _s5_block_diag_cc(m):
    t = m.reshape(S5_NB, 2, S5_GB, S5_N, S5_GB, S5_P)
    d = jnp.einsum("brgngp->rbgpn", t).reshape(2, S5_G, S5_P, S5_N)
    return d[0], -d[1]


SCAN_UNROLL = 8


def _z_store(zr, zi, blk, res, t, off):
    q, h = blk // 2, blk % 2
    for lt in range(4):
        zr[q, pl.ds(off + 4 * h + lt, t, stride=8), :] = res[:, lt * 128:(lt + 1) * 128]
        zi[q, pl.ds(off + 4 * h + lt, t, stride=8), :] = res[:, S5_BS + lt * 128:S5_BS + (lt + 1) * 128]


def _z_load(zr, zi, blk, t, off):
    q, h = blk // 2, blk % 2
    return jnp.concatenate([zr[q, pl.ds(off + 4 * h + lt, t, stride=8), :] for lt in range(4)]
                           + [zi[q, pl.ds(off + 4 * h + lt, t, stride=8), :] for lt in range(4)], axis=1)


def _z_scan_fwd(zr, zi, a_ref, carry_ref, t, off):
    ar = [a_ref[q] for q in range(S5_ZQ)]
    ai = [a_ref[S5_ZQ + q] for q in range(S5_ZQ)]

    def step(it, carry):
        carry = list(carry)
        base = pl.multiple_of(it * (8 * SCAN_UNROLL), 8 * SCAN_UNROLL) + off
        for tt in range(SCAN_UNROLL):
            rows = pl.ds(base + 8 * tt, 8)
            for q in range(S5_ZQ):
                c_r, c_i = carry[q], carry[S5_ZQ + q]
                n_r = ar[q] * c_r - ai[q] * c_i + zr[q, rows, :]
                n_i = ar[q] * c_i + ai[q] * c_r + zi[q, rows, :]
                zr[q, rows, :] = n_r
                zi[q, rows, :] = n_i
                carry[q], carry[S5_ZQ + q] = n_r, n_i
        return tuple(carry)

    out = lax.fori_loop(0, t // SCAN_UNROLL, step, tuple(carry_ref[k] for k in range(2 * S5_ZQ)))
    for k in range(2 * S5_ZQ):
        carry_ref[k] = out[k]


def _z_scan_bwd(lr, li, xr, xi, a_ref, carry_ref, acc_ref, t):
    ar = [a_ref[q] for q in range(S5_ZQ)]
    ai = [a_ref[S5_ZQ + q] for q in range(S5_ZQ)]
    n_it = t // SCAN_UNROLL

    def step(it, state):
        carry, acc = list(state[0]), list(state[1])
        base = pl.multiple_of((n_it - 1 - it) * (8 * SCAN_UNROLL), 8 * SCAN_UNROLL)
        for tt in reversed(range(SCAN_UNROLL)):
            rows = pl.ds(base + 8 * tt, 8)
            for q in range(S5_ZQ):
                c_r, c_i = carry[q], carry[S5_ZQ + q]
                n_r = ar[q] * c_r + ai[q] * c_i + lr[q, rows, :]
                n_i = ar[q] * c_i - ai[q] * c_r + li[q, rows, :]
                lr[q, rows, :] = n_r
                li[q, rows, :] = n_i
                p_r, p_i = xr[q, rows, :], xi[q, rows, :]
                acc[q] = acc[q] + n_r * p_r + n_i * p_i
                acc[S5_ZQ + q] = acc[S5_ZQ + q] + n_i * p_r - n_r * p_i
                carry[q], carry[S5_ZQ + q] = n_r, n_i
        return tuple(carry), tuple(acc)

    k8 = range(2 * S5_ZQ)
    carry, acc = lax.fori_loop(0, n_it, step, (tuple(carry_ref[k] for k in k8), tuple(acc_ref[k] for k in k8)))
    for k in k8:
        carry_ref[k] = carry[k]
        acc_ref[k] = acc[k]


def _s5_fwd(proj, bbm, ccm, d_row, glu_w, glu_b, tabs, t):
    l = proj.shape[0]
    nt = l // t

    def body(u_ref, gs_ref, bb_ref, cc_ref, d_ref, gw_ref, gb_ref, a_ref, ssm_ref, xst_ref, zr, zi, carry):
        @pl.when(pl.program_id(0) == 0)
        def _():
            carry[...] = jnp.zeros_like(carry)

        xst_ref[0] = carry[...]
        u = u_ref[...]
        ub = u.astype(BF16)
        for blk in range(S5_NB):
            _z_store(zr, zi, blk, _dot(ub[:, blk * 128:(blk + 1) * 128], bb_ref[blk]), t, 0)
        _z_scan_fwd(zr, zi, a_ref, carry, t, 0)
        ys = jnp.concatenate(
            [_dot(_z_load(zr, zi, blk, t, 0).astype(BF16), cc_ref[blk]) for blk in range(S5_NB)], axis=1)
        y2 = _gelu(ys + d_ref[...] * u)
        z = _dot(y2.astype(BF16), gw_ref[...]) + gb_ref[...]
        ssm_ref[...] = (y2 * _sigmoid(z) * _silu(gs_ref[...])).astype(ssm_ref.dtype)

    const2 = lambda shape: pl.BlockSpec(shape, lambda i: (0,) * len(shape))
    zshape = (2 * S5_ZQ, 8, 128)
    return pl.pallas_call(
        body, name="s5_fwd", grid=(nt,),
        in_specs=[pl.BlockSpec((t, D_MODEL), lambda i: (i, 3)), pl.BlockSpec((t, D_MODEL), lambda i: (i, 4)),
                  const2(bbm.shape), const2(ccm.shape), const2((1, D_MODEL)), const2((D_MODEL, D_MODEL)),
                  const2((1, D_MODEL)), const2(zshape)],
        out_specs=(pl.BlockSpec((t, D_MODEL), lambda i: (i, 0)), pl.BlockSpec((1,) + zshape, lambda i: (i, 0, 0, 0))),
        out_shape=(jax.ShapeDtypeStruct((l, D_MODEL), BF16), jax.ShapeDtypeStruct((nt,) + zshape, F32)),
        scratch_shapes=[pltpu.VMEM((S5_ZQ, 8 * t, 128), F32), pltpu.VMEM((S5_ZQ, 8 * t, 128), F32),
                        pltpu.VMEM(zshape, F32)],
        compiler_params=_cp("arbitrary"),
    )(proj, proj, bbm, ccm, d_row, glu_w, glu_b, tabs)


def _s5_bwd(proj, dmix, xstart, bbm, ccm, d_row, glu_w, glu_b, tabs, t):
    l = proj.shape[0]
    nt = l // t

    def body(u_ref, gs_ref, dm_ref, xst_ref, bb_ref, cc_ref, d_ref, gw_ref, gb_ref, a_ref,
             dug_ref, y2_ref, dz_ref, dbb_ref, dcc_ref, da_ref, dd_ref, dgb_ref, xr, xi, lr, li, carry, lcarry):
        @pl.when(pl.program_id(0) == 0)
        def _():
            lcarry[...] = jnp.zeros_like(lcarry)
            dbb_ref[...] = jnp.zeros_like(dbb_ref)
            dcc_ref[...] = jnp.zeros_like(dcc_ref)
            da_ref[...] = jnp.zeros_like(da_ref)
            dd_ref[...] = jnp.zeros_like(dd_ref)
            dgb_ref[...] = jnp.zeros_like(dgb_ref)

        carry[...] = xst_ref[0]
        for q in range(S5_ZQ):
            xr[q, 0:8, :] = carry[q]
            xi[q, 0:8, :] = carry[S5_ZQ + q]
        u = u_ref[...]
        ub = u.astype(BF16)
        for blk in range(S5_NB):
            _z_store(xr, xi, blk, _dot(ub[:, blk * 128:(blk + 1) * 128], bb_ref[blk]), t, 8)
        _z_scan_fwd(xr, xi, a_ref, carry, t, 8)
        ys = jnp.concatenate(
            [_dot(_z_load(xr, xi, blk, t, 8).astype(BF16), cc_ref[blk]) for blk in range(S5_NB)], axis=1)
        dv = d_ref[...]
        y1 = ys + dv * u
        y2 = _gelu(y1)
        y2b = y2.astype(BF16)
        sg = _sigmoid(_dot(y2b, gw_ref[...]) + gb_ref[...])
        gs = gs_ref[...]
        dssm = dm_ref[...]
        dug_ref[:, D_MODEL:] = (dssm * (y2 * sg) * _dsilu(gs)).astype(dug_ref.dtype)
        dy3 = dssm * _silu(gs)
        dz = dy3 * y2 * sg * (1.0 - sg)
        dzb = dz.astype(BF16)
        y2_ref[...] = y2b
        dz_ref[...] = dzb
        dgb_ref[...] += jnp.sum(dz, axis=0, keepdims=True)
        dy1 = (dy3 * sg + _dot_nt(dzb, gw_ref[...])) * _dgelu(y1)
        dd_ref[...] += jnp.sum(dy1 * u, axis=0, keepdims=True)
        dyb = dy1.astype(BF16)
        for blk in range(S5_NB):
            ch = slice(blk * 128, (blk + 1) * 128)
            _z_store(lr, li, blk, _dot_nt(dyb[:, ch], cc_ref[blk]), t, 0)
            dcc_ref[blk] += _dot_tn(_z_load(xr, xi, blk, t, 8).astype(BF16), dyb[:, ch])
        _z_scan_bwd(lr, li, xr, xi, a_ref, lcarry, da_ref, t)
        du = []
        for blk in range(S5_NB):
            lb = _z_load(lr, li, blk, t, 0).astype(BF16)
            du.append(_dot_nt(lb, bb_ref[blk]))
            dbb_ref[blk] += _dot_tn(ub[:, blk * 128:(blk + 1) * 128], lb)
        dug_ref[:, :D_MODEL] = (jnp.concatenate(du, axis=1) + dy1 * dv).astype(dug_ref.dtype)

    rev = lambda i: nt - 1 - i
    const2 = lambda shape: pl.BlockSpec(shape, lambda i: (0,) * len(shape))
    row_out = lambda w: pl.BlockSpec((t, w), lambda i: (rev(i), 0))
    zshape = (2 * S5_ZQ, 8, 128)
    return pl.pallas_call(
        body, name="s5_bwd", grid=(nt,),
        in_specs=[pl.BlockSpec((t, D_MODEL), lambda i: (rev(i), 3)), pl.BlockSpec((t, D_MODEL), lambda i: (rev(i), 4)),
                  pl.BlockSpec((t, D_MODEL), lambda i: (rev(i), 1)),
                  pl.BlockSpec((1,) + zshape, lambda i: (rev(i), 0, 0, 0)),
                  const2(bbm.shape), const2(ccm.shape), const2((1, D_MODEL)), const2((D_MODEL, D_MODEL)),
                  const2((1, D_MODEL)), const2(zshape)],
        out_specs=(row_out(2 * D_MODEL), row_out(D_MODEL), row_out(D_MODEL), const2(bbm.shape), const2(ccm.shape),
                   const2(zshape), const2((1, D_MODEL)), const2((1, D_MODEL))),
        out_shape=(jax.ShapeDtypeStruct((l, 2 * D_MODEL), BF16), jax.ShapeDtypeStruct((l, D_MODEL), BF16),
                   jax.ShapeDtypeStruct((l, D_MODEL), BF16), jax.ShapeDtypeStruct(bbm.shape, F32),
                   jax.ShapeDtypeStruct(ccm.shape, F32), jax.ShapeDtypeStruct(zshape, F32),
                   jax.ShapeDtypeStruct((1, D_MODEL), F32), jax.ShapeDtypeStruct((1, D_MODEL), F32)),
        scratch_shapes=[pltpu.VMEM((S5_ZQ, 8 * t + 8, 128), F32), pltpu.VMEM((S5_ZQ, 8 * t + 8, 128), F32),
                        pltpu.VMEM((S5_ZQ, 8 * t, 128), F32), pltpu.VMEM((S5_ZQ, 8 * t, 128), F32),
                        pltpu.VMEM(zshape, F32), pltpu.VMEM(zshape, F32)],
        compiler_params=_cp("arbitrary"),
    )(proj, proj, dmix, xstart, bbm, ccm, d_row, glu_w, glu_b, tabs)


def _attn_probs(qh, kh):
    s = _dot_nt(qh, kh) * (XA_DH ** -0.5)
    e = jnp.exp(s - jnp.max(s, axis=-1, keepdims=True))
    return e / jnp.sum(e, axis=-1, keepdims=True)


def _attn_fwd(qa, ka, va):
    l = qa.shape[0]
    m = ka.shape[0]
    tl = _pick(l, (512, 256))

    def body(q_ref, k_ref, v_ref, o_ref):
        for h in range(XA_HEADS):
            hs = slice(h * XA_DH, (h + 1) * XA_DH)
            p = _attn_probs(q_ref[:, hs], k_ref[:, hs])
            o_ref[:, hs] = _dot(p.astype(BF16), v_ref[:, hs]).astype(o_ref.dtype)

    return pl.pallas_call(
        body, name="xattn_fwd", grid=(l // tl,),
        in_specs=[pl.BlockSpec((tl, D_MODEL), lambda i: (i, 0)), pl.BlockSpec((m, D_MODEL), lambda i: (0, 0)),
                  pl.BlockSpec((m, D_MODEL), lambda i: (0, 0))],
        out_specs=pl.BlockSpec((tl, D_MODEL), lambda i: (i, 0)),
        out_shape=jax.ShapeDtypeStruct((l, D_MODEL), BF16), compiler_params=_cp("parallel"),
    )(qa, ka, va)


def _attn_bwd(qa, ka, va, doa):
    l = qa.shape[0]
    m = ka.shape[0]
    tl = _pick(l, (512, 256))

    def body(q_ref, k_ref, v_ref, do_ref, dq_ref, dk_ref, dv_ref):
        @pl.when(pl.program_id(0) == 0)
        def _():
            dk_ref[...] = jnp.zeros_like(dk_ref)
            dv_ref[...] = jnp.zeros_like(dv_ref)

        for h in range(XA_HEADS):
            hs = slice(h * XA_DH, (h + 1) * XA_DH)
            qh, kh, vh, doh = q_ref[:, hs], k_ref[:, hs], v_ref[:, hs], do_ref[:, hs]
            p = _attn_probs(qh, kh)
            dv_ref[:, hs] += _dot_tn(p.astype(BF16), doh)
            dp = _dot_nt(doh, vh)
            ds = (p * (dp - jnp.sum(dp * p, axis=-1, keepdims=True)) * (XA_DH ** -0.5)).astype(BF16)
            dq_ref[:, hs] = _dot(ds, kh).astype(dq_ref.dtype)
            dk_ref[:, hs] += _dot_tn(ds, qh)

    row = pl.BlockSpec((tl, D_MODEL), lambda i: (i, 0))
    mem = pl.BlockSpec((m, D_MODEL), lambda i: (0, 0))
    return pl.pallas_call(
        body, name="xattn_bwd", grid=(l // tl,), in_specs=[row, mem, mem, row], out_specs=(row, mem, mem),
        out_shape=(jax.ShapeDtypeStruct((l, D_MODEL), BF16), jax.ShapeDtypeStruct((m, D_MODEL), F32),
                   jax.ShapeDtypeStruct((m, D_MODEL), F32)),
        compiler_params=_cp("arbitrary"),
    )(qa, ka, va, doa)


def _me_and_peers():
    x, y, c = lax.axis_index("x"), lax.axis_index("y"), lax.axis_index("c")
    flip = lambda v, bit: (1 - v) if bit else v
    peers = []
    for k in range(1, N_DEV):
        px, py, pc = flip(x, (k >> 2) & 1), flip(y, (k >> 1) & 1), flip(c, k & 1)
        peers.append(((px, py, pc), 4 * px + 2 * py + pc))
    return 4 * x + 2 * y + c, peers


class _SideJob:
    def __init__(self, srcs, landing, src_of, dst_of):
        self.srcs = list(srcs)
        self.landing = list(landing)
        self.n = len(self.landing)
        self.src_of, self.dst_of = src_of, dst_of
        hbm = pl.BlockSpec(memory_space=pl.ANY)
        self.in_specs = [hbm] * len(self.srcs)
        self.out_specs = [hbm] * self.n
        self.scratch = [pltpu.SemaphoreType.DMA((self.n * (N_DEV - 1),)), pltpu.SemaphoreType.DMA((self.n * (N_DEV - 1),)),
                        pltpu.SemaphoreType.DMA((self.n,))]

    def _copies(self, src_refs, out_refs, sems):
        send_sems, recv_sems, loc_sems = sems
        me, peers = _me_and_peers()
        local = [pltpu.make_async_copy(self.src_of(a, me, src_refs), self.dst_of(a, me, out_refs), loc_sems.at[a])
                 for a in range(self.n)]
        sends, recvs = [], []
        for k, (peer, peer_idx) in enumerate(peers):
            for a in range(self.n):
                s = self.n * k + a
                sends.append(pltpu.make_async_remote_copy(
                    src_ref=self.src_of(a, peer_idx, src_refs), dst_ref=self.dst_of(a, me, out_refs),
                    send_sem=send_sems.at[s], recv_sem=recv_sems.at[s], device_id=peer, device_id_type=MESH))
                recvs.append(pltpu.make_async_remote_copy(
                    src_ref=self.src_of(a, me, src_refs), dst_ref=self.dst_of(a, peer_idx, out_refs),
                    send_sem=send_sems.at[s], recv_sem=recv_sems.at[s], device_id=peer, device_id_type=MESH))
        return local, sends, recvs

    def start(self, src_refs, out_refs, sems):
        local, sends, _ = self._copies(src_refs, out_refs, sems)
        for cp in local + sends:
            cp.start()

    def wait(self, src_refs, out_refs, sems):
        local, sends, recvs = self._copies(src_refs, out_refs, sems)
        for cp in recvs:
            cp.wait_recv()
        for cp in sends:
            cp.wait_send()
        for cp in local:
            cp.wait()


def _gather_job(shards):
    return _SideJob(shards, [jax.ShapeDtypeStruct((N_DEV,) + s.shape, s.dtype) for s in shards],
                    src_of=lambda a, j, srcs: srcs[a], dst_of=lambda a, j, outs: outs[a].at[j])


def _scatter_job(grads):
    landing, parts = [], []
    for g in grads:
        if g.ndim == 3:
            landing.append(jax.ShapeDtypeStruct(g.shape, g.dtype))
            parts.append(None)
        else:
            r = g.shape[0] // N_DEV
            landing.append(jax.ShapeDtypeStruct((N_DEV, r, g.shape[1]), g.dtype))
            parts.append(r)

    def src_of(a, j, srcs):
        if parts[a] is None:
            return srcs[a].at[j]
        return srcs[a].at[pl.ds(pl.multiple_of(j * parts[a], 8), parts[a]), :]

    return _SideJob(grads, landing, src_of=src_of, dst_of=lambda a, j, outs: outs[a].at[j])


def _allgather_w_in(w_in_shard, row_shards):
    n_row = len(row_shards)
    job = _gather_job([jax.ShapeDtypeStruct(w_in_shard.shape, BF16)])

    def body(*refs):
        win_ref = refs[0]
        row_refs = refs[1:1 + n_row]
        out_win = refs[1 + n_row]
        row_outs = refs[2 + n_row:2 + 2 * n_row]
        win_b = refs[2 + 2 * n_row]
        sems = refs[3 + 2 * n_row:]
        win_b[...] = win_ref[...].astype(BF16)
        job.start([win_b], [out_win], sems)
        for r, o in zip(row_refs, row_outs):
            o[...] = r[...].astype(BF16)
        job.wait([win_b], [out_win], sems)

    vm = pl.BlockSpec(memory_space=pltpu.VMEM)
    return pl.pallas_call(
        body, name="allgather_w_in", in_specs=[vm] * (1 + n_row), out_specs=(job.out_specs[0], *([vm] * n_row)),
        out_shape=(job.landing[0], *[jax.ShapeDtypeStruct(r.shape, BF16) for r in row_shards]),
        scratch_shapes=[pltpu.VMEM(w_in_shard.shape, BF16)] + job.scratch,
        compiler_params=pltpu.CompilerParams(vmem_limit_bytes=VMEM_LIMIT),
    )(w_in_shard, *row_shards)


def _allreduce_small(small):
    rows = SMALL_ROWS // N_DEV

    def body(x_ref, out_ref, land, send1, recv1, send2, recv2):
        me, peers = _me_and_peers()
        block = lambda j: pl.ds(pl.multiple_of(j * rows, 8), rows)

        def phase(src_of, dst_of, send_sems, recv_sems):
            sends = [pltpu.make_async_remote_copy(src_ref=src_of(pidx), dst_ref=dst_of(me), send_sem=send_sems.at[k],
                                                  recv_sem=recv_sems.at[k], device_id=peer, device_id_type=MESH)
                     for k, (peer, pidx) in enumerate(peers)]
            recvs = [pltpu.make_async_remote_copy(src_ref=src_of(me), dst_ref=dst_of(pidx), send_sem=send_sems.at[k],
                                                  recv_sem=recv_sems.at[k], device_id=peer, device_id_type=MESH)
                     for k, (peer, pidx) in enumerate(peers)]
            for cp in sends:
                cp.start()
            for cp in recvs:
                cp.wait_recv()
            for cp in sends:
                cp.wait_send()

        land[me] = x_ref[block(me), :]
        phase(lambda j: x_ref.at[block(j), :], lambda j: land.at[j], send1, recv1)
        total = land[0]
        for j in range(1, N_DEV):
            total = total + land[j]
        out_ref[block(me), :] = total
        phase(lambda j: out_ref.at[block(me), :], lambda j: out_ref.at[block(j), :], send2, recv2)

    vm = pl.BlockSpec(memory_space=pltpu.VMEM)
    return pl.pallas_call(
        body, name="allreduce_small", in_specs=[vm], out_specs=vm, out_shape=jax.ShapeDtypeStruct(small.shape, F32),
        scratch_shapes=[pltpu.VMEM((N_DEV, rows, D_MODEL), F32)] + [pltpu.SemaphoreType.DMA((N_DEV - 1,))] * 4,
    )(small)


def _adamw(name, got, w, m, v):
    r, c = w.shape
    n_slots = got.shape[0]
    tr = _pick(r, (256, 128, 64))

    def body(got_ref, w_ref, m_ref, v_ref, g_ref, d_ref, nm_ref, nv_ref):
        g = got_ref[0].astype(F32)
        for j in range(1, n_slots):
            g = g + got_ref[j].astype(F32)
        nm = ADAM_B1 * m_ref[...] + (1.0 - ADAM_B1) * g
        nv = ADAM_B2 * v_ref[...] + (1.0 - ADAM_B2) * jnp.square(g)
        m_hat = nm / (1.0 - ADAM_B1 ** ADAM_STEP)
        v_hat = nv / (1.0 - ADAM_B2 ** ADAM_STEP)
        g_ref[...] = g
        d_ref[...] = -ADAM_LR * (m_hat / (jnp.sqrt(v_hat) + ADAM_EPS) + ADAM_WD * w_ref[...])
        nm_ref[...] = nm
        nv_ref[...] = nv

    blk = pl.BlockSpec((tr, c), lambda i: (i, 0))
    out = jax.ShapeDtypeStruct((r, c), F32)
    return pl.pallas_call(
        body, name=name, grid=(r // tr,),
        in_specs=[pl.BlockSpec((n_slots, tr, c), lambda i: (0, i, 0)), blk, blk, blk],
        out_specs=(blk, blk, blk, blk), out_shape=(out, out, out, out), compiler_params=_cp("parallel"),
    )(got, w, m, v)


_SMALL_VECS = ("norm1_g", "ret_gn_g", "s5_d", "s5_glu_b", "norm2_g", "norm_mem_g", "norm_f_g")
_SMALL_ORDER = _SMALL_VECS + ("s5_a_re", "s5_a_im", "s5_log_dt", "s5_b_re", "s5_b_im", "s5_c_re", "s5_c_im")


def _pack_small(t, extra_row=None):
    rows = [t[n].reshape(1, D_MODEL) for n in _SMALL_VECS]
    rows.append(jnp.zeros((1, D_MODEL), F32) if extra_row is None else extra_row)
    rows += [t["s5_a_re"].reshape(4, D_MODEL), t["s5_a_im"].reshape(4, D_MODEL)]
    rows.append(jnp.pad(t["s5_log_dt"].reshape(1, S5_G), ((0, 7), (0, D_MODEL - S5_G))))
    rows += [t[n].reshape(64, D_MODEL) for n in ("s5_b_re", "s5_b_im", "s5_c_re", "s5_c_im")]
    rows.append(jnp.zeros((SMALL_ROWS - 280, D_MODEL), F32))
    return jnp.concatenate(rows, axis=0)


def _unpack_small(p, shapes):
    out = {n: p[i].reshape(shapes[n]) for i, n in enumerate(_SMALL_VECS)}
    out["s5_a_re"] = p[8:12].reshape(shapes["s5_a_re"])
    out["s5_a_im"] = p[12:16].reshape(shapes["s5_a_im"])
    out["s5_log_dt"] = p[16, :S5_G].reshape(shapes["s5_log_dt"])
    for i, n in enumerate(("s5_b_re", "s5_b_im", "s5_c_re", "s5_c_im")):
        out[n] = p[24 + 64 * i:24 + 64 * (i + 1)].reshape(shapes[n])
    return out


_W_NAMES = ("norm1_g", "w_in", "ret_gn_g", "s5_a_re", "s5_a_im", "s5_log_dt", "s5_b_re", "s5_b_im", "s5_c_re", "s5_c_im",
            "s5_d", "s5_glu_w", "s5_glu_b", "w_out", "norm2_g", "norm_mem_g", "xa_wq", "xa_wk", "xa_wv", "xa_wo",
            "norm_f_g")
_ROW_NAMES = ("s5_glu_w", "w_out", "xa_wq", "xa_wk", "xa_wv", "xa_wo")


def kernel(x, mem, positions, norm1_g, w_in, ret_gn_g, s5_a_re, s5_a_im, s5_log_dt, s5_b_re, s5_b_im, s5_c_re, s5_c_im, s5_d, s5_glu_w, s5_glu_b, w_out, norm2_g, norm_mem_g, xa_wq, xa_wk, xa_wv, xa_wo, norm_f_g, loss_target, m_norm1_g, m_w_in, m_ret_gn_g, m_s5_a_re, m_s5_a_im, m_s5_log_dt, m_s5_b_re, m_s5_b_im, m_s5_c_re, m_s5_c_im, m_s5_d, m_s5_glu_w, m_s5_glu_b, m_w_out, m_norm2_g, m_norm_mem_g, m_xa_wq, m_xa_wk, m_xa_wv, m_xa_wo, m_norm_f_g, v_norm1_g, v_w_in, v_ret_gn_g, v_s5_a_re, v_s5_a_im, v_s5_log_dt, v_s5_b_re, v_s5_b_im, v_s5_c_re, v_s5_c_im, v_s5_d, v_s5_glu_w, v_s5_glu_b, v_w_out, v_norm2_g, v_norm_mem_g, v_xa_wq, v_xa_wk, v_xa_wv, v_xa_wo, v_norm_f_g):
    w = dict(norm1_g=norm1_g, w_in=w_in, ret_gn_g=ret_gn_g, s5_a_re=s5_a_re, s5_a_im=s5_a_im, s5_log_dt=s5_log_dt,
             s5_b_re=s5_b_re, s5_b_im=s5_b_im, s5_c_re=s5_c_re, s5_c_im=s5_c_im, s5_d=s5_d, s5_glu_w=s5_glu_w,
             s5_glu_b=s5_glu_b, w_out=w_out, norm2_g=norm2_g, norm_mem_g=norm_mem_g, xa_wq=xa_wq, xa_wk=xa_wk,
             xa_wv=xa_wv, xa_wo=xa_wo, norm_f_g=norm_f_g)
    mom = dict(norm1_g=m_norm1_g, w_in=m_w_in, ret_gn_g=m_ret_gn_g, s5_a_re=m_s5_a_re, s5_a_im=m_s5_a_im,
               s5_log_dt=m_s5_log_dt, s5_b_re=m_s5_b_re, s5_b_im=m_s5_b_im, s5_c_re=m_s5_c_re, s5_c_im=m_s5_c_im,
               s5_d=m_s5_d, s5_glu_w=m_s5_glu_w, s5_glu_b=m_s5_glu_b, w_out=m_w_out, norm2_g=m_norm2_g,
               norm_mem_g=m_norm_mem_g, xa_wq=m_xa_wq, xa_wk=m_xa_wk, xa_wv=m_xa_wv, xa_wo=m_xa_wo,
               norm_f_g=m_norm_f_g)
    var = dict(norm1_g=v_norm1_g, w_in=v_w_in, ret_gn_g=v_ret_gn_g, s5_a_re=v_s5_a_re, s5_a_im=v_s5_a_im,
               s5_log_dt=v_s5_log_dt, s5_b_re=v_s5_b_re, s5_b_im=v_s5_b_im, s5_c_re=v_s5_c_re, s5_c_im=v_s5_c_im,
               s5_d=v_s5_d, s5_glu_w=v_s5_glu_w, s5_glu_b=v_s5_glu_b, w_out=v_w_out, norm2_g=v_norm2_g,
               norm_mem_g=v_norm_mem_g, xa_wq=v_xa_wq, xa_wk=v_xa_wk, xa_wv=v_xa_wv, xa_wo=v_xa_wo,
               norm_f_g=v_norm_f_g)
    shapes = {n: w[n].shape for n in _W_NAMES}

    x2d, mem2d, tgt = x[0], mem[0], loss_target[0]
    l = x2d.shape[0]
    ret_c = _pick(l, (128,))
    s5_t = _pick(l, (128,))
    g1, g2, gm, gf = norm1_g, norm2_g, norm_mem_g, norm_f_g.reshape(1, D_MODEL)

    win_s, *row_shards_b = _allgather_w_in(w_in[0], [w[n][0] for n in _ROW_NAMES])

    disc_args = (s5_a_re[0], s5_a_im[0], s5_log_dt[0], s5_b_re[0], s5_b_im[0])
    (abar_re, abar_im, bb_re, bb_im), disc_vjp = jax.vjp(_s5_discretize, *disc_args)
    bbm, ccm = _s5_block_mats(bb_re, bb_im, s5_c_re[0], s5_c_im[0])
    a_z = _s5_z(abar_re, abar_im)

    h1 = _rms_fwd("norm1_fwd", x2d, g1)
    proj, *rows_all = _mm_nn_slots("in_proj", h1, win_s, F32, side=_gather_job(row_shards_b))
    full = {n: g.reshape(N_DEV * r, D_MODEL) for n, g, r in zip(_ROW_NAMES, rows_all, ROW_SHARDS)}
    half = RET_DK // 2
    inv = ROPE_BASE ** (-jnp.arange(half, dtype=F32) / half)
    cos_t, sin_t = _rope_tables(positions[0].reshape(l, 1), jnp.tile(inv, 128 // half)[None, :])
    rconsts = _ret_constants(ret_c)
    ret, o_saved, r_prev = _ret_fwd(proj, cos_t, sin_t, rconsts, ret_gn_g, ret_c)
    ssm, xstart = _s5_fwd(proj, bbm, ccm, s5_d, full["s5_glu_w"], s5_glu_b, a_z, s5_t)
    mix = jnp.concatenate([ret, ssm], axis=1)
    x1 = _mm_nn("out_proj", mix, full["w_out"], F32, residual=x2d)
    h2 = _rms_fwd("norm2_fwd", x1, g2)
    mn = _rms_fwd("norm_mem_fwd", mem2d, gm)
    qa = _mm_nn("xa_q", h2, full["xa_wq"], BF16)
    ka = _mm_nn("xa_k", mn, full["xa_wk"], BF16)
    va = _mm_nn("xa_v", mn, full["xa_wv"], BF16)
    oa = _attn_fwd(qa, ka, va)
    x2 = _mm_nn("xa_o", oa, full["xa_wo"], F32, residual=x1)
    dx2, dgf, loss_lanes = _loss_head(x2, gf, tgt)

    doa = _mm_nt("xa_o_dx", dx2, full["xa_wo"], BF16)
    dwo = _mm_tn("xa_o_dw", oa, dx2, BF16)
    dqa, dka, dva = _attn_bwd(qa, ka, va, doa)
    dh2 = _mm_nt("xa_q_dx", dqa, full["xa_wq"], F32)
    dwq = _mm_tn("xa_q_dw", h2, dqa, BF16)
    dx1, dg2 = _rms_bwd("norm2_bwd", x1, g2, dh2, dx2)
    dwk = _mm_tn("xa_k_dw", mn, dka, BF16)
    dwv = _mm_tn("xa_v_dw", mn, dva, BF16)
    dmn = _mm_nt("xa_v_dx", dva, full["xa_wv"], F32, residual=_mm_nt("xa_k_dx", dka, full["xa_wk"], F32))
    _, dgm = _rms_bwd("norm_mem_bwd", mem2d, gm, dmn, None)
    dmix = _mm_nt("out_proj_dx", dx1, full["w_out"], F32)
    dwout = _mm_tn("out_proj_dw", mix, dx1, BF16)
    dret, dgn, *got_a = _ret_bwd(proj, cos_t, sin_t, rconsts, ret_gn_g, o_saved, r_prev, dmix, ret_c,
                                 side=_scatter_job([dwout, dwq, dwk, dwv, dwo]))
    dug, y2, dz, dbbm, dccm, dabar, dd, dgb = _s5_bwd(proj, dmix, xstart, bbm, ccm, s5_d, full["s5_glu_w"], s5_glu_b,
                                                      a_z, s5_t)
    dglu = _mm_tn("s5_glu_dw", y2, dz, BF16)
    dproj = jnp.concatenate([dret, dug], axis=1)
    dwin_s = _mm_tn_slots("in_proj_dw", h1, dproj, N_DEV, BF16)
    dh1, got_win, got_glu = _mm_nt_slots("in_proj_dx", dproj, win_s, F32, side=_scatter_job([dwin_s, dglu]))
    grad_x, dg1 = _rms_bwd("norm1_bwd", x2d, g1, dh1, dx1)

    dab_re, dab_im = _s5_unz(dabar)
    dbb_re, dbb_im = _s5_block_diag_bb(dbbm)
    dc_re, dc_im = _s5_block_diag_cc(dccm)
    da_re, da_im, dlog_dt, db_re, db_im = disc_vjp((dab_re, dab_im, dbb_re, dbb_im))
    small_g = dict(norm1_g=dg1, ret_gn_g=dgn, s5_d=dd, s5_glu_b=dgb, norm2_g=dg2, norm_mem_g=dgm, norm_f_g=dgf,
                   s5_a_re=da_re, s5_a_im=da_im, s5_log_dt=dlog_dt, s5_b_re=db_re, s5_b_im=db_im, s5_c_re=dc_re,
                   s5_c_im=dc_im)
    small_pack = _pack_small(small_g, extra_row=loss_lanes)

    res = {}
    got = dict(zip(("w_out", "xa_wq", "xa_wk", "xa_wv", "xa_wo"), got_a), w_in=got_win, s5_glu_w=got_glu)
    for n in ("w_in",) + _ROW_NAMES:
        res[n] = _adamw("adamw_" + n, got[n], w[n][0], mom[n][0], var[n][0])
    small_sum = _allreduce_small(small_pack)
    small_out = _adamw("adamw_small", small_sum[None], _pack_small(w), _pack_small(mom), _pack_small(var))
    unpacked = [_unpack_small(a, shapes) for a in small_out]
    loss = (0.5 / D_MODEL) * jnp.sum(small_out[0][7])
    for n in _SMALL_ORDER:
        res[n] = tuple(u[n] for u in unpacked)

    outs = [loss, grad_x[None]]
    for part in range(4):
        for n in _W_NAMES:
            outs.append(res[n][part].reshape(shapes[n]))
    return tuple(outs)
```

```python
import functools

import jax
import jax.numpy as jnp
from jax import lax
from jax.experimental import pallas as pl
from jax.experimental.pallas import tpu as pltpu

F32 = jnp.float32
BF16 = jnp.bfloat16
MESH = pl.DeviceIdType.MESH

D_MODEL = 1024
RET_HEADS, RET_DK, RET_DV = 8, 64, 128
RET_QK = RET_HEADS * RET_DK
S5_G, S5_N, S5_P = 64, 64, 16
S5_NB = 8
S5_GB = S5_G // S5_NB
S5_BS = S5_GB * S5_N
S5_COLS = 2 * S5_G * S5_N
XA_HEADS, XA_DH = 4, 256
EPS = 1e-6
ROPE_BASE = 10000.0
N_DEV = 8
W_IN_SHARD = 640
ROW_SHARDS = (128, 256, 128, 128, 128, 128)
ROWPACK = sum(ROW_SHARDS)
SMALL_ROWS = 320
ADAM_LR, ADAM_B1, ADAM_B2, ADAM_EPS, ADAM_WD, ADAM_STEP = 0.001, 0.9, 0.999, 1e-08, 0.01, 10

VMEM_LIMIT = 56 * 1024 * 1024


def _cp(*sem):
    return pltpu.CompilerParams(dimension_semantics=tuple(sem), vmem_limit_bytes=VMEM_LIMIT)


def _dot(a, b):
    return jnp.dot(a, b, preferred_element_type=F32)


def _dot_nt(a, b):
    return lax.dot_general(a, b, (((1,), (1,)), ((), ())), preferred_element_type=F32)


def _dot_tn(a, b):
    return lax.dot_general(a, b, (((0,), (0,)), ((), ())), preferred_element_type=F32)


def _sigmoid(x):
    return 1.0 / (1.0 + jnp.exp(-x))


def _silu(x):
    return x * _sigmoid(x)


def _dsilu(x):
    s = _sigmoid(x)
    return s * (1.0 + x * (1.0 - s))


_GELU_C = 0.7978845608028654


def _gelu(x):
    return 0.5 * x * (1.0 + jnp.tanh(_GELU_C * (x + 0.044715 * (x * x * x))))


def _dgelu(x):
    t = jnp.tanh(_GELU_C * (x + 0.044715 * (x * x * x)))
    return 0.5 * (1.0 + t) + 0.5 * x * (1.0 - t * t) * (_GELU_C * (1.0 + 3.0 * 0.044715 * (x * x)))


def _pick(n, cands):
    for c in cands:
        if n % c == 0:
            return c
    return n


def _mm_core(name, operands, in_specs, out_spec, out_shape, grid, nk, dims, acc_shape, has_res, side=None):
    n_in = 3 if has_res else 2
    n_side_in = len(side.srcs) if side else 0
    n_side_out = side.n if side else 0

    def body(*refs):
        a_ref, b_ref = refs[0], refs[1]
        r_ref = refs[2] if has_res else None
        side_in = refs[n_in:n_in + n_side_in]
        o_ref = refs[n_in + n_side_in]
        side_out = refs[n_in + n_side_in + 1:n_in + n_side_in + 1 + n_side_out]
        rest = refs[n_in + n_side_in + 1 + n_side_out:]
        acc, sems = (rest[0], rest[1:]) if nk > 1 else (None, rest)
        i, j, k = pl.program_id(0), pl.program_id(1), pl.program_id(2)
        if side:
            @pl.when((i == 0) & (j == 0) & (k == 0))
            def _():
                side.start(side_in, side_out, sems)

        part = lax.dot_general(a_ref[...].astype(BF16), b_ref[...].astype(BF16), (dims, ((), ())),
                               preferred_element_type=F32)
        if nk == 1:
            o_ref[...] = (part + r_ref[...] if has_res else part).astype(o_ref.dtype)
        else:
            @pl.when(k == 0)
            def _():
                acc[...] = part

            @pl.when(k > 0)
            def _():
                acc[...] += part

            @pl.when(k == nk - 1)
            def _():
                r = acc[...]
                if has_res:
                    r = r + r_ref[...]
                o_ref[...] = r.astype(o_ref.dtype)

        if side:
            @pl.when((i == grid[0] - 1) & (j == grid[1] - 1) & (k == grid[2] - 1))
            def _():
                side.wait(side_in, side_out, sems)

    acc_scratch = [pltpu.VMEM(acc_shape, F32)] if nk > 1 else []
    if side:
        return pl.pallas_call(
            body, name=name, grid=grid, in_specs=list(in_specs) + side.in_specs,
            out_specs=(out_spec, *side.out_specs), out_shape=(out_shape, *side.landing),
            scratch_shapes=acc_scratch + side.scratch,
            compiler_params=_cp("arbitrary", "arbitrary", "arbitrary"),
        )(*operands, *side.srcs)
    return pl.pallas_call(
        body, name=name, grid=grid, in_specs=in_specs, out_specs=out_spec, out_shape=out_shape,
        scratch_shapes=acc_scratch,
        compiler_params=_cp("parallel", "parallel", "arbitrary"),
    )(*operands)


def _mm_nn(name, a, b, out_dtype, residual=None):
    m, kk = a.shape
    n = b.shape[1]
    tm, tn, tk = _pick(m, (1024, 512, 256)), _pick(n, (1024, 512)), _pick(kk, (1024, 512))
    ops = [a, b]
    specs = [pl.BlockSpec((tm, tk), lambda i, j, k: (i, k)), pl.BlockSpec((tk, tn), lambda i, j, k: (k, j))]
    if residual is not None:
        ops.append(residual)
        specs.append(pl.BlockSpec((tm, tn), lambda i, j, k: (i, j)))
    return _mm_core(name, ops, specs, pl.BlockSpec((tm, tn), lambda i, j, k: (i, j)),
                    jax.ShapeDtypeStruct((m, n), out_dtype), (m // tm, n // tn, kk // tk), kk // tk,
                    ((1,), (0,)), (tm, tn), residual is not None)


def _mm_nt(name, a, b, out_dtype, residual=None):
    m, kk = a.shape
    n = b.shape[0]
    tm, tn, tk = _pick(m, (1024, 512, 256)), _pick(n, (1024, 512)), _pick(kk, (1024, 512))
    ops = [a, b]
    specs = [pl.BlockSpec((tm, tk), lambda i, j, k: (i, k)), pl.BlockSpec((tn, tk), lambda i, j, k: (j, k))]
    if residual is not None:
        ops.append(residual)
        specs.append(pl.BlockSpec((tm, tn), lambda i, j, k: (i, j)))
    return _mm_core(name, ops, specs, pl.BlockSpec((tm, tn), lambda i, j, k: (i, j)),
                    jax.ShapeDtypeStruct((m, n), out_dtype), (m // tm, n // tn, kk // tk), kk // tk,
                    ((1,), (1,)), (tm, tn), residual is not None)


def _mm_tn(name, a, b, out_dtype):
    kk, m = a.shape
    n = b.shape[1]
    tm, tn, tk = _pick(m, (1024, 512)), _pick(n, (1024, 512)), _pick(kk, (1024, 512, 256))
    specs = [pl.BlockSpec((tk, tm), lambda i, j, k: (k, i)), pl.BlockSpec((tk, tn), lambda i, j, k: (k, j))]
    return _mm_core(name, [a, b], specs, pl.BlockSpec((tm, tn), lambda i, j, k: (i, j)),
                    jax.ShapeDtypeStruct((m, n), out_dtype), (m // tm, n // tn, kk // tk), kk // tk,
                    ((0,), (0,)), (tm, tn), False)


def _mm_nn_slots(name, a, b_slots, out_dtype, side=None):
    m, kk = a.shape
    s, _, ns = b_slots.shape
    tm, tk = _pick(m, (1024, 512, 256)), _pick(kk, (1024, 512))
    specs = [pl.BlockSpec((tm, tk), lambda i, j, k: (i, k)), pl.BlockSpec((None, tk, ns), lambda i, j, k: (j, k, 0))]
    return _mm_core(name, [a, b_slots], specs, pl.BlockSpec((tm, ns), lambda i, j, k: (i, j)),
                    jax.ShapeDtypeStruct((m, s * ns), out_dtype), (m // tm, s, kk // tk), kk // tk,
                    ((1,), (0,)), (tm, ns), False, side)


def _mm_nt_slots(name, a, b_slots, out_dtype, side=None):
    m = a.shape[0]
    s, n, ns = b_slots.shape
    tm, tn = _pick(m, (1024, 512, 256)), _pick(n, (1024, 512))
    specs = [pl.BlockSpec((tm, ns), lambda i, j, k: (i, k)), pl.BlockSpec((None, tn, ns), lambda i, j, k: (k, j, 0))]
    return _mm_core(name, [a, b_slots], specs, pl.BlockSpec((tm, tn), lambda i, j, k: (i, j)),
                    jax.ShapeDtypeStruct((m, n), out_dtype), (m // tm, n // tn, s), s,
                    ((1,), (1,)), (tm, tn), False, side)


def _mm_tn_slots(name, a, b, s, out_dtype):
    kk, m = a.shape
    ns = b.shape[1] // s
    tm, tk = _pick(m, (1024, 512)), _pick(kk, (1024, 512, 256))
    specs = [pl.BlockSpec((tk, tm), lambda i, j, k: (k, i)), pl.BlockSpec((tk, ns), lambda i, j, k: (k, j))]
    return _mm_core(name, [a, b], specs, pl.BlockSpec((None, tm, ns), lambda i, j, k: (j, i, 0)),
                    jax.ShapeDtypeStruct((s, m, ns), out_dtype), (m // tm, s, kk // tk), kk // tk,
                    ((0,), (0,)), (tm, ns), False)


def _rms_fwd(name, x, g):
    r, d = x.shape
    tr = _pick(r, (1024, 512, 256))

    def body(x_ref, g_ref, o_ref):
        xv = x_ref[...]
        rs = lax.rsqrt(jnp.mean(xv * xv, axis=-1, keepdims=True) + EPS)
        o_ref[...] = (xv * rs * g_ref[...]).astype(o_ref.dtype)

    return pl.pallas_call(
        body, name=name, grid=(r // tr,),
        in_specs=[pl.BlockSpec((tr, d), lambda i: (i, 0)), pl.BlockSpec((1, d), lambda i: (0, 0))],
        out_specs=pl.BlockSpec((tr, d), lambda i: (i, 0)),
        out_shape=jax.ShapeDtypeStruct((r, d), BF16), compiler_params=_cp("parallel"),
    )(x, g)


def _rms_bwd(name, x, g, dh, dres):
    r, d = x.shape
    tr = _pick(r, (512, 256))
    has_res = dres is not None

    def body(*refs):
        if has_res:
            x_ref, g_ref, dh_ref, dr_ref, dx_ref, dg_ref = refs
        else:
            x_ref, g_ref, dh_ref, dx_ref, dg_ref = refs
        i = pl.program_id(0)

        @pl.when(i == 0)
        def _():
            dg_ref[...] = jnp.zeros_like(dg_ref)

        xv = x_ref[...]
        dhv = dh_ref[...].astype(F32)
        rs = lax.rsqrt(jnp.mean(xv * xv, axis=-1, keepdims=True) + EPS)
        xn = xv * rs
        dg_ref[...] += jnp.sum(dhv * xn, axis=0, keepdims=True)
        dn = dhv * g_ref[...]
        dx = rs * (dn - xn * jnp.mean(dn * xn, axis=-1, keepdims=True))
        if has_res:
            dx = dx + dr_ref[...]
        dx_ref[...] = dx

    row = pl.BlockSpec((tr, d), lambda i: (i, 0))
    vec = pl.BlockSpec((1, d), lambda i: (0, 0))
    ops = [x, g, dh] + ([dres] if has_res else [])
    return pl.pallas_call(
        body, name=name, grid=(r // tr,),
        in_specs=[row, vec, row] + ([row] if has_res else []),
        out_specs=(row, vec),
        out_shape=(jax.ShapeDtypeStruct((r, d), F32), jax.ShapeDtypeStruct((1, d), F32)),
        compiler_params=_cp("arbitrary"),
    )(*ops)


def _loss_head(x2, gf, target):
    r, d = x2.shape
    tr = _pick(r, (512, 256))

    def body(x_ref, g_ref, t_ref, dx_ref, dg_ref, ls_ref):
        i = pl.program_id(0)

        @pl.when(i == 0)
        def _():
            dg_ref[...] = jnp.zeros_like(dg_ref)
            ls_ref[...] = jnp.zeros_like(ls_ref)

        xv = x_ref[...]
        rs = lax.rsqrt(jnp.mean(xv * xv, axis=-1, keepdims=True) + EPS)
        xn = xv * rs
        e = xn * g_ref[...] - t_ref[...]
        ls_ref[...] += jnp.sum(e * e, axis=0, keepdims=True)
        dy = e * (1.0 / d)
        dg_ref[...] += jnp.sum(dy * xn, axis=0, keepdims=True)
        dn = dy * g_ref[...]
        dx_ref[...] = rs * (dn - xn * jnp.mean(dn * xn, axis=-1, keepdims=True))

    row = pl.BlockSpec((tr, d), lambda i: (i, 0))
    vec = pl.BlockSpec((1, d), lambda i: (0, 0))
    return pl.pallas_call(
        body, name="loss_head", grid=(r // tr,), in_specs=[row, vec, row], out_specs=(row, vec, vec),
        out_shape=(jax.ShapeDtypeStruct((r, d), F32), jax.ShapeDtypeStruct((1, d), F32),
                   jax.ShapeDtypeStruct((1, d), F32)),
        compiler_params=_cp("arbitrary"),
    )(x2, gf, target)


def _rope_tables(pos_col, inv_row):
    l = pos_col.shape[0]
    tl = _pick(l, (1024, 512, 256))

    def body(p_ref, inv_ref, cos_ref, sin_ref):
        ang = p_ref[...].astype(F32) * inv_ref[...]
        lane = lax.broadcasted_iota(jnp.int32, ang.shape, 1)
        c = jnp.cos(ang)
        s = jnp.where((lane % RET_DK) < RET_DK // 2, -jnp.sin(ang), jnp.sin(ang))
        cos_ref[...] = jnp.tile(c, (1, RET_QK // 128))
        sin_ref[...] = jnp.tile(s, (1, RET_QK // 128))

    return pl.pallas_call(
        body, name="rope_tables", grid=(l // tl,),
        in_specs=[pl.BlockSpec((tl, 1), lambda i: (i, 0)), pl.BlockSpec((1, 128), lambda i: (0, 0))],
        out_specs=(pl.BlockSpec((tl, RET_QK), lambda i: (i, 0)), pl.BlockSpec((tl, RET_QK), lambda i: (i, 0))),
        out_shape=(jax.ShapeDtypeStruct((l, RET_QK), F32), jax.ShapeDtypeStruct((l, RET_QK), F32)),
        compiler_params=_cp("parallel"),
    )(pos_col, inv_row)


def _rot(x, cos_t, sin_t):
    n = x.shape[-1]
    lane = lax.broadcasted_iota(jnp.int32, x.shape, 1)
    partner = jnp.where((lane % RET_DK) < RET_DK // 2, pltpu.roll(x, n - RET_DK // 2, 1), pltpu.roll(x, RET_DK // 2, 1))
    return x * cos_t + partner * sin_t


def _ret_constants(c):
    log_g = jnp.log1p(-jnp.exp2(-5.0 - jnp.arange(RET_HEADS, dtype=F32)))
    j = jnp.arange(c, dtype=F32)
    diff = j[:, None] - j[None, :]
    decay = jnp.where(diff[None] >= 0.0, jnp.exp(log_g[:, None, None] * jnp.maximum(diff, 0.0)[None]), 0.0)
    q_w = jnp.exp(log_g[None, :] * (j + 1.0)[:, None])
    k_w = jnp.exp(log_g[None, :] * (c - 1.0 - j)[:, None])
    cd = jnp.exp(log_g * c)
    rep = lambda t: jnp.repeat(t, RET_DK, axis=1)
    cd_row = jnp.repeat(cd, RET_DV)[None, :]
    return decay, rep(q_w), rep(k_w), cd_row


def _ret_fwd(proj, cos_t, sin_t, consts, gn_g, c):
    l = proj.shape[0]
    nc = l // c
    decay, qw, kw, cd_row = consts

    def body(q_ref, k_ref, v_ref, g_ref, cos_ref, sin_ref, dec_ref, qw_ref, kw_ref, cd_ref, gn_ref,
             ret_ref, o_ref, rp_ref, state):
        @pl.when(pl.program_id(0) == 0)
        def _():
            state[...] = jnp.zeros_like(state)

        cs, sn = cos_ref[...], sin_ref[...]
        qr = _rot(q_ref[...].astype(F32), cs, sn)
        kr = _rot(k_ref[...].astype(F32), cs, sn) * (RET_DK ** -0.5)
        qb, kb = qr.astype(BF16), kr.astype(BF16)
        qwb = (qr * qw_ref[...]).astype(BF16)
        kwb = (kr * kw_ref[...]).astype(BF16)
        vb = v_ref[...].astype(BF16)
        for h in range(RET_HEADS):
            qs = slice(h * RET_DK, (h + 1) * RET_DK)
            vs = slice(h * RET_DV, (h + 1) * RET_DV)
            s = _dot_nt(qb[:, qs], kb[:, qs]) * dec_ref[h]
            r_prev = state[h]
            rp_ref[0, h] = r_prev
            o = _dot(s.astype(BF16), vb[:, vs]) + _dot(qwb[:, qs], r_prev.astype(BF16))
            state[h] = cd_ref[:, vs] * r_prev + _dot_tn(kwb[:, qs], vb[:, vs])
            o_ref[:, vs] = o
            mu = jnp.mean(o, axis=-1, keepdims=True)
            var = jnp.mean(jnp.square(o - mu), axis=-1, keepdims=True)
            on = (o - mu) * lax.rsqrt(var + EPS)
            ret_ref[:, vs] = (on * gn_ref[:, vs] * _silu(g_ref[:, vs].astype(F32))).astype(ret_ref.dtype)

    const2 = lambda shape: pl.BlockSpec(shape, lambda i: (0,) * len(shape))
    return pl.pallas_call(
        body, name="retention_fwd", grid=(nc,),
        in_specs=[pl.BlockSpec((c, RET_QK), lambda i: (i, 0)), pl.BlockSpec((c, RET_QK), lambda i: (i, 1)),
                  pl.BlockSpec((c, D_MODEL), lambda i: (i, 1)), pl.BlockSpec((c, D_MODEL), lambda i: (i, 2)),
                  pl.BlockSpec((c, RET_QK), lambda i: (i, 0)), pl.BlockSpec((c, RET_QK), lambda i: (i, 0)),
                  const2((RET_HEADS, c, c)), const2((c, RET_QK)), const2((c, RET_QK)), const2((1, D_MODEL)),
                  const2((1, D_MODEL))],
        out_specs=(pl.BlockSpec((c, D_MODEL), lambda i: (i, 0)), pl.BlockSpec((c, D_MODEL), lambda i: (i, 0)),
                   pl.BlockSpec((1, RET_HEADS, RET_DK, RET_DV), lambda i: (i, 0, 0, 0))),
        out_shape=(jax.ShapeDtypeStruct((l, D_MODEL), BF16), jax.ShapeDtypeStruct((l, D_MODEL), F32),
                   jax.ShapeDtypeStruct((nc, RET_HEADS, RET_DK, RET_DV), F32)),
        scratch_shapes=[pltpu.VMEM((RET_HEADS, RET_DK, RET_DV), F32)],
        compiler_params=_cp("arbitrary"),
    )(proj, proj, proj, proj, cos_t, sin_t, decay, qw, kw, cd_row, gn_g)


def _ret_bwd(proj, cos_t, sin_t, consts, gn_g, o_saved, r_prev_saved, dmix, c, side):
    l = proj.shape[0]
    nc = l // c
    decay, qw, kw, cd_row = consts
    n_in = 14

    def body(*refs):
        (q_ref, k_ref, v_ref, g_ref, cos_ref, sin_ref, dec_ref, qw_ref, kw_ref, cd_ref, gn_ref, o_ref, rp_ref,
         dr_ref) = refs[:n_in]
        side_in = refs[n_in:n_in + len(side.srcs)]
        out_ref, dgn_ref = refs[n_in + len(side.srcs):n_in + len(side.srcs) + 2]
        side_out = refs[n_in + len(side.srcs) + 2:n_in + len(side.srcs) + 2 + side.n]
        state, dq_s, dk_s = refs[n_in + len(side.srcs) + 2 + side.n:n_in + len(side.srcs) + 5 + side.n]
        sems = refs[n_in + len(side.srcs) + 5 + side.n:]

        @pl.when(pl.program_id(0) == 0)
        def _():
            side.start(side_in, side_out, sems)
            state[...] = jnp.zeros_like(state)
            dgn_ref[...] = jnp.zeros_like(dgn_ref)

        cs, sn = cos_ref[...], sin_ref[...]
        qr = _rot(q_ref[...].astype(F32), cs, sn)
        kr = _rot(k_ref[...].astype(F32), cs, sn) * (RET_DK ** -0.5)
        qb, kb = qr.astype(BF16), kr.astype(BF16)
        qwb = (qr * qw_ref[...]).astype(BF16)
        kwv = kw_ref[...]
        kwb = (kr * kwv).astype(BF16)
        qwv = qw_ref[...]
        vb = v_ref[...].astype(BF16)
        for h in range(RET_HEADS):
            qs = slice(h * RET_DK, (h + 1) * RET_DK)
            vs = slice(h * RET_DV, (h + 1) * RET_DV)
            dec = dec_ref[h]
            o = o_ref[:, vs]
            mu = jnp.mean(o, axis=-1, keepdims=True)
            var = jnp.mean(jnp.square(o - mu), axis=-1, keepdims=True)
            rstd = lax.rsqrt(var + EPS)
            on = (o - mu) * rstd
            gate = g_ref[:, vs].astype(F32)
            sg = _silu(gate)
            dret = dr_ref[:, vs].astype(F32)
            gn = gn_ref[:, vs]
            dgn_ref[:, vs] += jnp.sum(dret * on * sg, axis=0, keepdims=True)
            out_ref[:, 2 * RET_QK + D_MODEL + h * RET_DV:2 * RET_QK + D_MODEL + (h + 1) * RET_DV] = (
                dret * on * gn * _dsilu(gate)).astype(out_ref.dtype)
            don = dret * gn * sg
            do = rstd * (don - jnp.mean(don, axis=-1, keepdims=True)
                         - on * jnp.mean(don * on, axis=-1, keepdims=True))
            dob = do.astype(BF16)
            sn_h = state[h]
            snb = sn_h.astype(BF16)
            s = _dot_nt(qb[:, qs], kb[:, qs]) * dec
            dv = _dot_tn(s.astype(BF16), dob) + _dot(kwb[:, qs], snb)
            out_ref[:, 2 * RET_QK + h * RET_DV:2 * RET_QK + (h + 1) * RET_DV] = dv.astype(out_ref.dtype)
            ds = (_dot_nt(dob, vb[:, vs]) * dec).astype(BF16)
            dq_s[:, qs] = _dot(ds, kb[:, qs]) + qwv[:, qs] * _dot_nt(dob, rp_ref[0, h].astype(BF16))
            dk_s[:, qs] = _dot_tn(ds, qb[:, qs]) + kwv[:, qs] * _dot_nt(vb[:, vs], snb)
            state[h] = cd_ref[:, vs] * sn_h + _dot_tn(qwb[:, qs], dob)
        out_ref[:, 0:RET_QK] = _rot(dq_s[...], cs, -sn).astype(out_ref.dtype)
        out_ref[:, RET_QK:2 * RET_QK] = (_rot(dk_s[...], cs, -sn) * (RET_DK ** -0.5)).astype(out_ref.dtype)

        @pl.when(pl.program_id(0) == nc - 1)
        def _():
            side.wait(side_in, side_out, sems)

    rev = lambda i: nc - 1 - i
    const2 = lambda shape: pl.BlockSpec(shape, lambda i: (0,) * len(shape))
    return pl.pallas_call(
        body, name="retention_bwd", grid=(nc,),
        in_specs=[pl.BlockSpec((c, RET_QK), lambda i: (rev(i), 0)), pl.BlockSpec((c, RET_QK), lambda i: (rev(i), 1)),
                  pl.BlockSpec((c, D_MODEL), lambda i: (rev(i), 1)), pl.BlockSpec((c, D_MODEL), lambda i: (rev(i), 2)),
                  pl.BlockSpec((c, RET_QK), lambda i: (rev(i), 0)), pl.BlockSpec((c, RET_QK), lambda i: (rev(i), 0)),
                  const2((RET_HEADS, c, c)), const2((c, RET_QK)), const2((c, RET_QK)), const2((1, D_MODEL)),
                  const2((1, D_MODEL)),
                  pl.BlockSpec((c, D_MODEL), lambda i: (rev(i), 0)),
                  pl.BlockSpec((1, RET_HEADS, RET_DK, RET_DV), lambda i: (rev(i), 0, 0, 0)),
                  pl.BlockSpec((c, D_MODEL), lambda i: (rev(i), 0))] + side.in_specs,
        out_specs=(pl.BlockSpec((c, 2 * RET_QK + 2 * D_MODEL), lambda i: (rev(i), 0)), const2((1, D_MODEL)),
                   *side.out_specs),
        out_shape=(jax.ShapeDtypeStruct((l, 2 * RET_QK + 2 * D_MODEL), BF16), jax.ShapeDtypeStruct((1, D_MODEL), F32),
                   *side.landing),
        scratch_shapes=[pltpu.VMEM((RET_HEADS, RET_DK, RET_DV), F32), pltpu.VMEM((c, RET_QK), F32),
                        pltpu.VMEM((c, RET_QK), F32)] + side.scratch,
        compiler_params=_cp("arbitrary"),
    )(proj, proj, proj, proj, cos_t, sin_t, decay, qw, kw, cd_row, gn_g, o_saved, r_prev_saved, dmix, *side.srcs)


def _s5_discretize(a_re, a_im, log_dt, b_re, b_im):
    dt = jnp.exp(log_dt)[:, None]
    mag = jnp.exp(a_re * dt)
    abar_re = mag * jnp.cos(a_im * dt)
    abar_im = mag * jnp.sin(a_im * dt)
    den = a_re * a_re + a_im * a_im
    nr, ni = abar_re - 1.0, abar_im
    f_re = (nr * a_re + ni * a_im) / den
    f_im = (ni * a_re - nr * a_im) / den
    bb_re = f_re[..., None] * b_re - f_im[..., None] * b_im
    bb_im = f_re[..., None] * b_im + f_im[..., None] * b_re
    return abar_re, abar_im, bb_re, bb_im


S5_ZQ = S5_NB // 2


def _s5_z(re, im):
    return jnp.concatenate([re.reshape(S5_ZQ, 8, 128), im.reshape(S5_ZQ, 8, 128)], axis=0)


def _s5_unz(z):
    return z[:S5_ZQ].reshape(S5_G, S5_N), z[S5_ZQ:].reshape(S5_G, S5_N)


def _s5_block_mats(bb_re, bb_im, c_re, c_im):
    eye = jnp.eye(S5_GB, dtype=F32)
    bb = jnp.stack([bb_re, bb_im], axis=0).reshape(2, S5_NB, S5_GB, S5_N, S5_P)
    bbm = jnp.einsum("rbgnp,gh->bgprhn", bb, eye).reshape(S5_NB, S5_GB * S5_P, 2 * S5_BS)
    cc = jnp.stack([c_re, -c_im], axis=0).reshape(2, S5_NB, S5_GB, S5_P, S5_N)
    ccm = jnp.einsum("rbgpn,gh->brhngp", cc, eye).reshape(S5_NB, 2 * S5_BS, S5_GB * S5_P)
    return bbm.astype(BF16), ccm.astype(BF16)


def _s5_block_diag_bb(m):
    t = m.reshape(S5_NB, S5_GB, S5_P, 2, S5_GB, S5_N)
    d = jnp.einsum("bgprgn->rbgnp", t).reshape(2, S5_G, S5_N, S5_P)
    return d[0], d[1]


def _s5_block_diag_cc(m):
    t = m.reshape(S5_NB, 2, S5_GB, S5_N, S5_GB, S5_P)
    d = jnp.einsum("brgngp->rbgpn", t).reshape(2, S5_G, S5_P, S5_N)
    return d[0], -d[1]


SCAN_UNROLL = 8


def _z_store(zr, zi, blk, res, t, off):
    q, h = blk // 2, blk % 2
    for lt in range(4):
        zr[q, pl.ds(off + 4 * h + lt, t, stride=8), :] = res[:, lt * 128:(lt + 1) * 128]
        zi[q, pl.ds(off + 4 * h + lt, t, stride=8), :] = res[:, S5_BS + lt * 128:S5_BS + (lt + 1) * 128]


def _z_load(zr, zi, blk, t, off):
    q, h = blk // 2, blk % 2
    return jnp.concatenate([zr[q, pl.ds(off + 4 * h + lt, t, stride=8), :] for lt in range(4)]
                           + [zi[q, pl.ds(off + 4 * h + lt, t, stride=8), :] for lt in range(4)], axis=1)


def _z_scan_fwd(zr, zi, a_ref, carry_ref, t, off):
    ar = [a_ref[q] for q in range(S5_ZQ)]
    ai = [a_ref[S5_ZQ + q] for q in range(S5_ZQ)]

    def step(it, carry):
        carry = list(carry)
        base = pl.multiple_of(it * (8 * SCAN_UNROLL), 8 * SCAN_UNROLL) + off
        for tt in range(SCAN_UNROLL):
            rows = pl.ds(base + 8 * tt, 8)
            for q in range(S5_ZQ):
                c_r, c_i = carry[q], carry[S5_ZQ + q]
                n_r = ar[q] * c_r - ai[q] * c_i + zr[q, rows, :]
                n_i = ar[q] * c_i + ai[q] * c_r + zi[q, rows, :]
                zr[q, rows, :] = n_r
                zi[q, rows, :] = n_i
                carry[q], carry[S5_ZQ + q] = n_r, n_i
        return tuple(carry)

    out = lax.fori_loop(0, t // SCAN_UNROLL, step, tuple(carry_ref[k] for k in range(2 * S5_ZQ)))
    for k in range(2 * S5_ZQ):
        carry_ref[k] = out[k]


def _z_scan_bwd(lr, li, xr, xi, a_ref, carry_ref, acc_ref, t):
    ar = [a_ref[q] for q in range(S5_ZQ)]
    ai = [a_ref[S5_ZQ + q] for q in range(S5_ZQ)]
    n_it = t // SCAN_UNROLL

    def step(it, state):
        carry, acc = list(state[0]), list(state[1])
        base = pl.multiple_of((n_it - 1 - it) * (8 * SCAN_UNROLL), 8 * SCAN_UNROLL)
        for tt in reversed(range(SCAN_UNROLL)):
            rows = pl.ds(base + 8 * tt, 8)
            for q in range(S5_ZQ):
                c_r, c_i = carry[q], carry[S5_ZQ + q]
                n_r = ar[q] * c_r + ai[q] * c_i + lr[q, rows, :]
                n_i = ar[q] * c_i - ai[q] * c_r + li[q, rows, :]
                lr[q, rows, :] = n_r
                li[q, rows, :] = n_i
                p_r, p_i = xr[q, rows, :], xi[q, rows, :]
                acc[q] = acc[q] + n_r * p_r + n_i * p_i
                acc[S5_ZQ + q] = acc[S5_ZQ + q] + n_i * p_r - n_r * p_i
                carry[q], carry[S5_ZQ + q] = n_r, n_i
        return tuple(carry), tuple(acc)

    k8 = range(2 * S5_ZQ)
    carry, acc = lax.fori_loop(0, n_it, step, (tuple(carry_ref[k] for k in k8), tuple(acc_ref[k] for k in k8)))
    for k in k8:
        carry_ref[k] = carry[k]
        acc_ref[k] = acc[k]


def _s5_fwd(proj, bbm, ccm, d_row, glu_w, glu_b, tabs, t):
    l = proj.shape[0]
    nt = l // t

    def body(u_ref, gs_ref, bb_ref, cc_ref, d_ref, gw_ref, gb_ref, a_ref, ssm_ref, xst_ref, zr, zi, carry):
        @pl.when(pl.program_id(0) == 0)
        def _():
            carry[...] = jnp.zeros_like(carry)

        xst_ref[0] = carry[...]
        ub = u_ref[...]
        u = ub.astype(F32)
        for blk in range(S5_NB):
            _z_store(zr, zi, blk, _dot(ub[:, blk * 128:(blk + 1) * 128], bb_ref[blk]), t, 0)
        _z_scan_fwd(zr, zi, a_ref, carry, t, 0)
        ys = jnp.concatenate(
            [_dot(_z_load(zr, zi, blk, t, 0).astype(BF16), cc_ref[blk]) for blk in range(S5_NB)], axis=1)
        y2 = _gelu(ys + d_ref[...] * u)
        z = _dot(y2.astype(BF16), gw_ref[...]) + gb_ref[...]
        ssm_ref[...] = (y2 * _sigmoid(z) * _silu(gs_ref[...].astype(F32))).astype(ssm_ref.dtype)

    const2 = lambda shape: pl.BlockSpec(shape, lambda i: (0,) * len(shape))
    zshape = (2 * S5_ZQ, 8, 128)
    return pl.pallas_call(
        body, name="s5_fwd", grid=(nt,),
        in_specs=[pl.BlockSpec((t, D_MODEL), lambda i: (i, 3)), pl.BlockSpec((t, D_MODEL), lambda i: (i, 4)),
                  const2(bbm.shape), const2(ccm.shape), const2((1, D_MODEL)), const2((D_MODEL, D_MODEL)),
                  const2((1, D_MODEL)), const2(zshape)],
        out_specs=(pl.BlockSpec((t, D_MODEL), lambda i: (i, 0)), pl.BlockSpec((1,) + zshape, lambda i: (i, 0, 0, 0))),
        out_shape=(jax.ShapeDtypeStruct((l, D_MODEL), BF16), jax.ShapeDtypeStruct((nt,) + zshape, F32)),
        scratch_shapes=[pltpu.VMEM((S5_ZQ, 8 * t, 128), F32), pltpu.VMEM((S5_ZQ, 8 * t, 128), F32),
                        pltpu.VMEM(zshape, F32)],
        compiler_params=_cp("arbitrary"),
    )(proj, proj, bbm, ccm, d_row, glu_w, glu_b, tabs)


def _s5_bwd(proj, dmix, xstart, bbm, ccm, d_row, glu_w, glu_b, tabs, t):
    l = proj.shape[0]
    nt = l // t

    def body(u_ref, gs_ref, dm_ref, xst_ref, bb_ref, cc_ref, d_ref, gw_ref, gb_ref, a_ref,
             dug_ref, y2_ref, dz_ref, dbb_ref, dcc_ref, da_ref, dd_ref, dgb_ref, xr, xi, lr, li, carry, lcarry):
        @pl.when(pl.program_id(0) == 0)
        def _():
            lcarry[...] = jnp.zeros_like(lcarry)
            dbb_ref[...] = jnp.zeros_like(dbb_ref)
            dcc_ref[...] = jnp.zeros_like(dcc_ref)
            da_ref[...] = jnp.zeros_like(da_ref)
            dd_ref[...] = jnp.zeros_like(dd_ref)
            dgb_ref[...] = jnp.zeros_like(dgb_ref)

        carry[...] = xst_ref[0]
        for q in range(S5_ZQ):
            xr[q, 0:8, :] = carry[q]
            xi[q, 0:8, :] = carry[S5_ZQ + q]
        ub = u_ref[...]
        u = ub.astype(F32)
        for blk in range(S5_NB):
            _z_store(xr, xi, blk, _dot(ub[:, blk * 128:(blk + 1) * 128], bb_ref[blk]), t, 8)
        _z_scan_fwd(xr, xi, a_ref, carry, t, 8)
        ys = jnp.concatenate(
            [_dot(_z_load(xr, xi, blk, t, 8).astype(BF16), cc_ref[blk]) for blk in range(S5_NB)], axis=1)
        dv = d_ref[...]
        y1 = ys + dv * u
        y2 = _gelu(y1)
        y2b = y2.astype(BF16)
        sg = _sigmoid(_dot(y2b, gw_ref[...]) + gb_ref[...])
        gs = gs_ref[...].astype(F32)
        dssm = dm_ref[...].astype(F32)
        dug_ref[:, D_MODEL:] = (dssm * (y2 * sg) * _dsilu(gs)).astype(dug_ref.dtype)
        dy3 = dssm * _silu(gs)
        dz = dy3 * y2 * sg * (1.0 - sg)
        dzb = dz.astype(BF16)
        y2_ref[...] = y2b
        dz_ref[...] = dzb
        dgb_ref[...] += jnp.sum(dz, axis=0, keepdims=True)
        dy1 = (dy3 * sg + _dot_nt(dzb, gw_ref[...])) * _dgelu(y1)
        dd_ref[...] += jnp.sum(dy1 * u, axis=0, keepdims=True)
        dyb = dy1.astype(BF16)
        for blk in range(S5_NB):
            ch = slice(blk * 128, (blk + 1) * 128)
            _z_store(lr, li, blk, _dot_nt(dyb[:, ch], cc_ref[blk]), t, 0)
            dcc_ref[blk] += _dot_tn(_z_load(xr, xi, blk, t, 8).astype(BF16), dyb[:, ch])
        _z_scan_bwd(lr, li, xr, xi, a_ref, lcarry, da_ref, t)
        du = []
        for blk in range(S5_NB):
            lb = _z_load(lr, li, blk, t, 0).astype(BF16)
            du.append(_dot_nt(lb, bb_ref[blk]))
            dbb_ref[blk] += _dot_tn(ub[:, blk * 128:(blk + 1) * 128], lb)
        dug_ref[:, :D_MODEL] = (jnp.concatenate(du, axis=1) + dy1 * dv).astype(dug_ref.dtype)

    rev = lambda i: nt - 1 - i
    const2 = lambda shape: pl.BlockSpec(shape, lambda i: (0,) * len(shape))
    row_out = lambda w: pl.BlockSpec((t, w), lambda i: (rev(i), 0))
    zshape = (2 * S5_ZQ, 8, 128)
    return pl.pallas_call(
        body, name="s5_bwd", grid=(nt,),
        in_specs=[pl.BlockSpec((t, D_MODEL), lambda i: (rev(i), 3)), pl.BlockSpec((t, D_MODEL), lambda i: (rev(i), 4)),
                  pl.BlockSpec((t, D_MODEL), lambda i: (rev(i), 1)),
                  pl.BlockSpec((1,) + zshape, lambda i: (rev(i), 0, 0, 0)),
                  const2(bbm.shape), const2(ccm.shape), const2((1, D_MODEL)), const2((D_MODEL, D_MODEL)),
                  const2((1, D_MODEL)), const2(zshape)],
        out_specs=(row_out(2 * D_MODEL), row_out(D_MODEL), row_out(D_MODEL), const2(bbm.shape), const2(ccm.shape),
                   const2(zshape), const2((1, D_MODEL)), const2((1, D_MODEL))),
        out_shape=(jax.ShapeDtypeStruct((l, 2 * D_MODEL), BF16), jax.ShapeDtypeStruct((l, D_MODEL), BF16),
                   jax.ShapeDtypeStruct((l, D_MODEL), BF16), jax.ShapeDtypeStruct(bbm.shape, F32),
                   jax.ShapeDtypeStruct(ccm.shape, F32), jax.ShapeDtypeStruct(zshape, F32),
                   jax.ShapeDtypeStruct((1, D_MODEL), F32), jax.ShapeDtypeStruct((1, D_MODEL), F32)),
        scratch_shapes=[pltpu.VMEM((S5_ZQ, 8 * t + 8, 128), F32), pltpu.VMEM((S5_ZQ, 8 * t + 8, 128), F32),
                        pltpu.VMEM((S5_ZQ, 8 * t, 128), F32), pltpu.VMEM((S5_ZQ, 8 * t, 128), F32),
                        pltpu.VMEM(zshape, F32), pltpu.VMEM(zshape, F32)],
        compiler_params=_cp("arbitrary"),
    )(proj, proj, dmix, xstart, bbm, ccm, d_row, glu_w, glu_b, tabs)


def _attn_probs(qh, kh):
    s = _dot_nt(qh, kh) * (XA_DH ** -0.5)
    e = jnp.exp(s - jnp.max(s, axis=-1, keepdims=True))
    return e / jnp.sum(e, axis=-1, keepdims=True)


def _attn_fwd(qa, ka, va):
    l = qa.shape[0]
    m = ka.shape[0]
    tl = _pick(l, (512, 256))

    def body(q_ref, k_ref, v_ref, o_ref):
        for h in range(XA_HEADS):
            hs = slice(h * XA_DH, (h + 1) * XA_DH)
            p = _attn_probs(q_ref[:, hs], k_ref[:, hs])
            o_ref[:, hs] = _dot(p.astype(BF16), v_ref[:, hs]).astype(o_ref.dtype)

    return pl.pallas_call(
        body, name="xattn_fwd", grid=(l // tl,),
        in_specs=[pl.BlockSpec((tl, D_MODEL), lambda i: (i, 0)), pl.BlockSpec((m, D_MODEL), lambda i: (0, 0)),
                  pl.BlockSpec((m, D_MODEL), lambda i: (0, 0))],
        out_specs=pl.BlockSpec((tl, D_MODEL), lambda i: (i, 0)),
        out_shape=jax.ShapeDtypeStruct((l, D_MODEL), BF16), compiler_params=_cp("parallel"),
    )(qa, ka, va)


def _attn_bwd(qa, ka, va, doa):
    l = qa.shape[0]
    m = ka.shape[0]
    tl = _pick(l, (512, 256))

    def body(q_ref, k_ref, v_ref, do_ref, dq_ref, dk_ref, dv_ref):
        @pl.when(pl.program_id(0) == 0)
        def _():
            dk_ref[...] = jnp.zeros_like(dk_ref)
            dv_ref[...] = jnp.zeros_like(dv_ref)

        for h in range(XA_HEADS):
            hs = slice(h * XA_DH, (h + 1) * XA_DH)
            qh, kh, vh, doh = q_ref[:, hs], k_ref[:, hs], v_ref[:, hs], do_ref[:, hs]
            p = _attn_probs(qh, kh)
            dv_ref[:, hs] += _dot_tn(p.astype(BF16), doh)
            dp = _dot_nt(doh, vh)
            ds = (p * (dp - jnp.sum(dp * p, axis=-1, keepdims=True)) * (XA_DH ** -0.5)).astype(BF16)
            dq_ref[:, hs] = _dot(ds, kh).astype(dq_ref.dtype)
            dk_ref[:, hs] += _dot_tn(ds, qh)

    row = pl.BlockSpec((tl, D_MODEL), lambda i: (i, 0))
    mem = pl.BlockSpec((m, D_MODEL), lambda i: (0, 0))
    return pl.pallas_call(
        body, name="xattn_bwd", grid=(l // tl,), in_specs=[row, mem, mem, row], out_specs=(row, mem, mem),
        out_shape=(jax.ShapeDtypeStruct((l, D_MODEL), BF16), jax.ShapeDtypeStruct((m, D_MODEL), F32),
                   jax.ShapeDtypeStruct((m, D_MODEL), F32)),
        compiler_params=_cp("arbitrary"),
    )(qa, ka, va, doa)


def _me_and_peers():
    x, y, c = lax.axis_index("x"), lax.axis_index("y"), lax.axis_index("c")
    flip = lambda v, bit: (1 - v) if bit else v
    peers = []
    for k in range(1, N_DEV):
        px, py, pc = flip(x, (k >> 2) & 1), flip(y, (k >> 1) & 1), flip(c, k & 1)
        peers.append(((px, py, pc), 4 * px + 2 * py + pc))
    return 4 * x + 2 * y + c, peers


class _SideJob:
    def __init__(self, srcs, landing, src_of, dst_of):
        self.srcs = list(srcs)
        self.landing = list(landing)
        self.n = len(self.landing)
        self.src_of, self.dst_of = src_of, dst_of
        hbm = pl.BlockSpec(memory_space=pl.ANY)
        self.in_specs = [hbm] * len(self.srcs)
        self.out_specs = [hbm] * self.n
        self.scratch = [pltpu.SemaphoreType.DMA((self.n * (N_DEV - 1),)), pltpu.SemaphoreType.DMA((self.n * (N_DEV - 1),)),
                        pltpu.SemaphoreType.DMA((self.n,))]

    def _copies(self, src_refs, out_refs, sems):
        send_sems, recv_sems, loc_sems = sems
        me, peers = _me_and_peers()
        local = [pltpu.make_async_copy(self.src_of(a, me, src_refs), self.dst_of(a, me, out_refs), loc_sems.at[a])
                 for a in range(self.n)]
        sends, recvs = [], []
        for k, (peer, peer_idx) in enumerate(peers):
            for a in range(self.n):
                s = self.n * k + a
                sends.append(pltpu.make_async_remote_copy(
                    src_ref=self.src_of(a, peer_idx, src_refs), dst_ref=self.dst_of(a, me, out_refs),
                    send_sem=send_sems.at[s], recv_sem=recv_sems.at[s], device_id=peer, device_id_type=MESH))
                recvs.append(pltpu.make_async_remote_copy(
                    src_ref=self.src_of(a, me, src_refs), dst_ref=self.dst_of(a, peer_idx, out_refs),
                    send_sem=send_sems.at[s], recv_sem=recv_sems.at[s], device_id=peer, device_id_type=MESH))
        return local, sends, recvs

    def start(self, src_refs, out_refs, sems):
        local, sends, _ = self._copies(src_refs, out_refs, sems)
        for cp in local + sends:
            cp.start()

    def wait(self, src_refs, out_refs, sems):
        local, sends, recvs = self._copies(src_refs, out_refs, sems)
        for cp in recvs:
            cp.wait_recv()
        for cp in sends:
            cp.wait_send()
        for cp in local:
            cp.wait()


def _gather_job(shards):
    return _SideJob(shards, [jax.ShapeDtypeStruct((N_DEV,) + s.shape, s.dtype) for s in shards],
                    src_of=lambda a, j, srcs: srcs[a], dst_of=lambda a, j, outs: outs[a].at[j])


def _scatter_job(grads):
    landing, parts = [], []
    for g in grads:
        if g.ndim == 3:
            landing.append(jax.ShapeDtypeStruct(g.shape, g.dtype))
            parts.append(None)
        else:
            r = g.shape[0] // N_DEV
            landing.append(jax.ShapeDtypeStruct((N_DEV, r, g.shape[1]), g.dtype))
            parts.append(r)

    def src_of(a, j, srcs):
        if parts[a] is None:
            return srcs[a].at[j]
        return srcs[a].at[pl.ds(pl.multiple_of(j * parts[a], 8), parts[a]), :]

    return _SideJob(grads, landing, src_of=src_of, dst_of=lambda a, j, outs: outs[a].at[j])


def _allgather_w_in(w_in_shard, row_shards):
    n_row = len(row_shards)
    job = _gather_job([jax.ShapeDtypeStruct(w_in_shard.shape, BF16)])

    def body(*refs):
        win_ref = refs[0]
        row_refs = refs[1:1 + n_row]
        out_win = refs[1 + n_row]
        row_outs = refs[2 + n_row:2 + 2 * n_row]
        win_b = refs[2 + 2 * n_row]
        sems = refs[3 + 2 * n_row:]
        win_b[...] = win_ref[...].astype(BF16)
        job.start([win_b], [out_win], sems)
        for r, o in zip(row_refs, row_outs):
            o[...] = r[...].astype(BF16)
        job.wait([win_b], [out_win], sems)

    vm = pl.BlockSpec(memory_space=pltpu.VMEM)
    return pl.pallas_call(
        body, name="allgather_w_in", in_specs=[vm] * (1 + n_row), out_specs=(job.out_specs[0], *([vm] * n_row)),
        out_shape=(job.landing[0], *[jax.ShapeDtypeStruct(r.shape, BF16) for r in row_shards]),
        scratch_shapes=[pltpu.VMEM(w_in_shard.shape, BF16)] + job.scratch,
        compiler_params=pltpu.CompilerParams(vmem_limit_bytes=VMEM_LIMIT),
    )(w_in_shard, *row_shards)


def _allreduce_small(small):
    rows = SMALL_ROWS // N_DEV

    def body(x_ref, out_ref, land, send1, recv1, send2, recv2):
        me, peers = _me_and_peers()
        block = lambda j: pl.ds(pl.multiple_of(j * rows, 8), rows)

        def phase(src_of, dst_of, send_sems, recv_sems):
            sends = [pltpu.make_async_remote_copy(src_ref=src_of(pidx), dst_ref=dst_of(me), send_sem=send_sems.at[k],
                                                  recv_sem=recv_sems.at[k], device_id=peer, device_id_type=MESH)
                     for k, (peer, pidx) in enumerate(peers)]
            recvs = [pltpu.make_async_remote_copy(src_ref=src_of(me), dst_ref=dst_of(pidx), send_sem=send_sems.at[k],
                                                  recv_sem=recv_sems.at[k], device_id=peer, device_id_type=MESH)
                     for k, (peer, pidx) in enumerate(peers)]
            for cp in sends:
                cp.start()
            for cp in recvs:
                cp.wait_recv()
            for cp in sends:
                cp.wait_send()

        land[me] = x_ref[block(me), :]
        phase(lambda j: x_ref.at[block(j), :], lambda j: land.at[j], send1, recv1)
        total = land[0]
        for j in range(1, N_DEV):
            total = total + land[j]
        out_ref[block(me), :] = total
        phase(lambda j: out_ref.at[block(me), :], lambda j: out_ref.at[block(j), :], send2, recv2)

    vm = pl.BlockSpec(memory_space=pltpu.VMEM)
    return pl.pallas_call(
        body, name="allreduce_small", in_specs=[vm], out_specs=vm, out_shape=jax.ShapeDtypeStruct(small.shape, F32),
        scratch_shapes=[pltpu.VMEM((N_DEV, rows, D_MODEL), F32)] + [pltpu.SemaphoreType.DMA((N_DEV - 1,))] * 4,
    )(small)


def _adamw(name, got, w, m, v):
    r, c = w.shape
    n_slots = got.shape[0]
    tr = _pick(r, (256, 128, 64))

    def body(got_ref, w_ref, m_ref, v_ref, g_ref, d_ref, nm_ref, nv_ref):
        g = got_ref[0].astype(F32)
        for j in range(1, n_slots):
            g = g + got_ref[j].astype(F32)
        nm = ADAM_B1 * m_ref[...] + (1.0 - ADAM_B1) * g
        nv = ADAM_B2 * v_ref[...] + (1.0 - ADAM_B2) * jnp.square(g)
        m_hat = nm / (1.0 - ADAM_B1 ** ADAM_STEP)
        v_hat = nv / (1.0 - ADAM_B2 ** ADAM_STEP)
        g_ref[...] = g
        d_ref[...] = -ADAM_LR * (m_hat / (jnp.sqrt(v_hat) + ADAM_EPS) + ADAM_WD * w_ref[...])
        nm_ref[...] = nm
        nv_ref[...] = nv

    blk = pl.BlockSpec((tr, c), lambda i: (i, 0))
    out = jax.ShapeDtypeStruct((r, c), F32)
    return pl.pallas_call(
        body, name=name, grid=(r // tr,),
        in_specs=[pl.BlockSpec((n_slots, tr, c), lambda i: (0, i, 0)), blk, blk, blk],
        out_specs=(blk, blk, blk, blk), out_shape=(out, out, out, out), compiler_params=_cp("parallel"),
    )(got, w, m, v)


_SMALL_VECS = ("norm1_g", "ret_gn_g", "s5_d", "s5_glu_b", "norm2_g", "norm_mem_g", "norm_f_g")
_SMALL_ORDER = _SMALL_VECS + ("s5_a_re", "s5_a_im", "s5_log_dt", "s5_b_re", "s5_b_im", "s5_c_re", "s5_c_im")


def _pack_small(t, extra_row=None):
    rows = [t[n].reshape(1, D_MODEL) for n in _SMALL_VECS]
    rows.append(jnp.zeros((1, D_MODEL), F32) if extra_row is None else extra_row)
    rows += [t["s5_a_re"].reshape(4, D_MODEL), t["s5_a_im"].reshape(4, D_MODEL)]
    rows.append(jnp.pad(t["s5_log_dt"].reshape(1, S5_G), ((0, 7), (0, D_MODEL - S5_G))))
    rows += [t[n].reshape(64, D_MODEL) for n in ("s5_b_re", "s5_b_im", "s5_c_re", "s5_c_im")]
    rows.append(jnp.zeros((SMALL_ROWS - 280, D_MODEL), F32))
    return jnp.concatenate(rows, axis=0)


def _unpack_small(p, shapes):
    out = {n: p[i].reshape(shapes[n]) for i, n in enumerate(_SMALL_VECS)}
    out["s5_a_re"] = p[8:12].reshape(shapes["s5_a_re"])
    out["s5_a_im"] = p[12:16].reshape(shapes["s5_a_im"])
    out["s5_log_dt"] = p[16, :S5_G].reshape(shapes["s5_log_dt"])
    for i, n in enumerate(("s5_b_re", "s5_b_im", "s5_c_re", "s5_c_im")):
        out[n] = p[24 + 64 * i:24 + 64 * (i + 1)].reshape(shapes[n])
    return out


_W_NAMES = ("norm1_g", "w_in", "ret_gn_g", "s5_a_re", "s5_a_im", "s5_log_dt", "s5_b_re", "s5_b_im", "s5_c_re", "s5_c_im",
            "s5_d", "s5_glu_w", "s5_glu_b", "w_out", "norm2_g", "norm_mem_g", "xa_wq", "xa_wk", "xa_wv", "xa_wo",
            "norm_f_g")
_ROW_NAMES = ("s5_glu_w", "w_out", "xa_wq", "xa_wk", "xa_wv", "xa_wo")


def kernel(x, mem, positions, norm1_g, w_in, ret_gn_g, s5_a_re, s5_a_im, s5_log_dt, s5_b_re, s5_b_im, s5_c_re, s5_c_im, s5_d, s5_glu_w, s5_glu_b, w_out, norm2_g, norm_mem_g, xa_wq, xa_wk, xa_wv, xa_wo, norm_f_g, loss_target, m_norm1_g, m_w_in, m_ret_gn_g, m_s5_a_re, m_s5_a_im, m_s5_log_dt, m_s5_b_re, m_s5_b_im, m_s5_c_re, m_s5_c_im, m_s5_d, m_s5_glu_w, m_s5_glu_b, m_w_out, m_norm2_g, m_norm_mem_g, m_xa_wq, m_xa_wk, m_xa_wv, m_xa_wo, m_norm_f_g, v_norm1_g, v_w_in, v_ret_gn_g, v_s5_a_re, v_s5_a_im, v_s5_log_dt, v_s5_b_re, v_s5_b_im, v_s5_c_re, v_s5_c_im, v_s5_d, v_s5_glu_w, v_s5_glu_b, v_w_out, v_norm2_g, v_norm_mem_g, v_xa_wq, v_xa_wk, v_xa_wv, v_xa_wo, v_norm_f_g):
    w = dict(norm1_g=norm1_g, w_in=w_in, ret_gn_g=ret_gn_g, s5_a_re=s5_a_re, s5_a_im=s5_a_im, s5_log_dt=s5_log_dt,
             s5_b_re=s5_b_re, s5_b_im=s5_b_im, s5_c_re=s5_c_re, s5_c_im=s5_c_im, s5_d=s5_d, s5_glu_w=s5_glu_w,
             s5_glu_b=s5_glu_b, w_out=w_out, norm2_g=norm2_g, norm_mem_g=norm_mem_g, xa_wq=xa_wq, xa_wk=xa_wk,
             xa_wv=xa_wv, xa_wo=xa_wo, norm_f_g=norm_f_g)
    mom = dict(norm1_g=m_norm1_g, w_in=m_w_in, ret_gn_g=m_ret_gn_g, s5_a_re=m_s5_a_re, s5_a_im=m_s5_a_im,
               s5_log_dt=m_s5_log_dt, s5_b_re=m_s5_b_re, s5_b_im=m_s5_b_im, s5_c_re=m_s5_c_re, s5_c_im=m_s5_c_im,
               s5_d=m_s5_d, s5_glu_w=m_s5_glu_w, s5_glu_b=m_s5_glu_b, w_out=m_w_out, norm2_g=m_norm2_g,
               norm_mem_g=m_norm_mem_g, xa_wq=m_xa_wq, xa_wk=m_xa_wk, xa_wv=m_xa_wv, xa_wo=m_xa_wo,
               norm_f_g=m_norm_f_g)
    var = dict(norm1_g=v_norm1_g, w_in=v_w_in, ret_gn_g=v_ret_gn_g, s5_a_re=v_s5_a_re, s5_a_im=v_s5_a_im,
               s5_log_dt=v_s5_log_dt, s5_b_re=v_s5_b_re, s5_b_im=v_s5_b_im, s5_c_re=v_s5_c_re, s5_c_im=v_s5_c_im,
               s5_d=v_s5_d, s5_glu_w=v_s5_glu_w, s5_glu_b=v_s5_glu_b, w_out=v_w_out, norm2_g=v_norm2_g,
               norm_mem_g=v_norm_mem_g, xa_wq=v_xa_wq, xa_wk=v_xa_wk, xa_wv=v_xa_wv, xa_wo=v_xa_wo,
               norm_f_g=v_norm_f_g)
    shapes = {n: w[n].shape for n in _W_NAMES}

    x2d, mem2d, tgt = x[0], mem[0], loss_target[0]
    l = x2d.shape[0]
    ret_c = _pick(l, (256, 128))
    s5_t = _pick(l, (256, 128))
    g1, g2, gm, gf = norm1_g, norm2_g, norm_mem_g, norm_f_g.reshape(1, D_MODEL)

    win_s, *row_shards_b = _allgather_w_in(w_in[0], [w[n][0] for n in _ROW_NAMES])

    disc_args = (s5_a_re[0], s5_a_im[0], s5_log_dt[0], s5_b_re[0], s5_b_im[0])
    (abar_re, abar_im, bb_re, bb_im), disc_vjp = jax.vjp(_s5_discretize, *disc_args)
    bbm, ccm = _s5_block_mats(bb_re, bb_im, s5_c_re[0], s5_c_im[0])
    a_z = _s5_z(abar_re, abar_im)

    h1 = _rms_fwd("norm1_fwd", x2d, g1)
    proj, *rows_all = _mm_nn_slots("in_proj", h1, win_s, BF16, side=_gather_job(row_shards_b))
    full = {n: g.reshape(N_DEV * r, D_MODEL) for n, g, r in zip(_ROW_NAMES, rows_all, ROW_SHARDS)}
    half = RET_DK // 2
    inv = ROPE_BASE ** (-jnp.arange(half, dtype=F32) / half)
    cos_t, sin_t = _rope_tables(positions[0].reshape(l, 1), jnp.tile(inv, 128 // half)[None, :])
    rconsts = _ret_constants(ret_c)
    ret, o_saved, r_prev = _ret_fwd(proj, cos_t, sin_t, rconsts, ret_gn_g, ret_c)
    ssm, xstart = _s5_fwd(proj, bbm, ccm, s5_d, full["s5_glu_w"], s5_glu_b, a_z, s5_t)
    mix = jnp.concatenate([ret, ssm], axis=1)
    x1 = _mm_nn("out_proj", mix, full["w_out"], F32, residual=x2d)
    h2 = _rms_fwd("norm2_fwd", x1, g2)
    mn = _rms_fwd("norm_mem_fwd", mem2d, gm)
    qa = _mm_nn("xa_q", h2, full["xa_wq"], BF16)
    ka = _mm_nn("xa_k", mn, full["xa_wk"], BF16)
    va = _mm_nn("xa_v", mn, full["xa_wv"], BF16)
    oa = _attn_fwd(qa, ka, va)
    x2 = _mm_nn("xa_o", oa, full["xa_wo"], F32, residual=x1)
    dx2, dgf, loss_lanes = _loss_head(x2, gf, tgt)

    doa = _mm_nt("xa_o_dx", dx2, full["xa_wo"], BF16)
    dwo = _mm_tn("xa_o_dw", oa, dx2, BF16)
    dqa, dka, dva = _attn_bwd(qa, ka, va, doa)
    dh2 = _mm_nt("xa_q_dx", dqa, full["xa_wq"], F32)
    dwq = _mm_tn("xa_q_dw", h2, dqa, BF16)
    dx1, dg2 = _rms_bwd("norm2_bwd", x1, g2, dh2, dx2)
    dwk = _mm_tn("xa_k_dw", mn, dka, BF16)
    dwv = _mm_tn("xa_v_dw", mn, dva, BF16)
    dmn = _mm_nt("xa_v_dx", dva, full["xa_wv"], F32, residual=_mm_nt("xa_k_dx", dka, full["xa_wk"], F32))
    _, dgm = _rms_bwd("norm_mem_bwd", mem2d, gm, dmn, None)
    dmix = _mm_nt("out_proj_dx", dx1, full["w_out"], BF16)
    dwout = _mm_tn("out_proj_dw", mix, dx1, BF16)
    dret, dgn, *got_a = _ret_bwd(proj, cos_t, sin_t, rconsts, ret_gn_g, o_saved, r_prev, dmix, ret_c,
                                 side=_scatter_job([dwout, dwq, dwk, dwv, dwo]))
    dug, y2, dz, dbbm, dccm, dabar, dd, dgb = _s5_bwd(proj, dmix, xstart, bbm, ccm, s5_d, full["s5_glu_w"], s5_glu_b,
                                                      a_z, s5_t)
    dglu = _mm_tn("s5_glu_dw", y2, dz, BF16)
    dproj = jnp.concatenate([dret, dug], axis=1)
    dwin_s = _mm_tn_slots("in_proj_dw", h1, dproj, N_DEV, BF16)
    dh1, got_win, got_glu = _mm_nt_slots("in_proj_dx", dproj, win_s, F32, side=_scatter_job([dwin_s, dglu]))
    grad_x, dg1 = _rms_bwd("norm1_bwd", x2d, g1, dh1, dx1)

    dab_re, dab_im = _s5_unz(dabar)
    dbb_re, dbb_im = _s5_block_diag_bb(dbbm)
    dc_re, dc_im = _s5_block_diag_cc(dccm)
    da_re, da_im, dlog_dt, db_re, db_im = disc_vjp((dab_re, dab_im, dbb_re, dbb_im))
    small_g = dict(norm1_g=dg1, ret_gn_g=dgn, s5_d=dd, s5_glu_b=dgb, norm2_g=dg2, norm_mem_g=dgm, norm_f_g=dgf,
                   s5_a_re=da_re, s5_a_im=da_im, s5_log_dt=dlog_dt, s5_b_re=db_re, s5_b_im=db_im, s5_c_re=dc_re,
                   s5_c_im=dc_im)
    small_pack = _pack_small(small_g, extra_row=loss_lanes)

    res = {}
    got = dict(zip(("w_out", "xa_wq", "xa_wk", "xa_wv", "xa_wo"), got_a), w_in=got_win, s5_glu_w=got_glu)
    for n in ("w_in",) + _ROW_NAMES:
        res[n] = _adamw("adamw_" + n, got[n], w[n][0], mom[n][0], var[n][0])
    small_sum = _allreduce_small(small_pack)
    small_out = _adamw("adamw_small", small_sum[None], _pack_small(w), _pack_small(mom), _pack_small(var))
    unpacked = [_unpack_small(a, shapes) for a in small_out]
    loss = (0.5 / D_MODEL) * jnp.sum(small_out[0][7])
    for n in _SMALL_ORDER:
        res[n] = tuple(u[n] for u in unpacked)

    outs = [loss, grad_x[None]]
    for part in range(4):
        for n in _W_NAMES:
            outs.append(res[n][part].reshape(shapes[n]))
    return tuple(outs)
```

```python
import functools

import jax
import jax.numpy as jnp
from jax import lax
from jax.experimental import pallas as pl
from jax.experimental.pallas import tpu as pltpu

F32 = jnp.float32
BF16 = jnp.bfloat16
MESH = pl.DeviceIdType.MESH

D_MODEL = 1024
RET_HEADS, RET_DK, RET_DV = 8, 64, 128
RET_QK = RET_HEADS * RET_DK
S5_G, S5_N, S5_P = 64, 64, 16
S5_NB = 8
S5_GB = S5_G // S5_NB
S5_BS = S5_GB * S5_N
S5_COLS = 2 * S5_G * S5_N
XA_HEADS, XA_DH = 4, 256
EPS = 1e-6
ROPE_BASE = 10000.0
N_DEV = 8
W_IN_SHARD = 640
ROW_SHARDS = (128, 256, 128, 128, 128, 128)
ROWPACK = sum(ROW_SHARDS)
SMALL_ROWS = 320
ADAM_LR, ADAM_B1, ADAM_B2, ADAM_EPS, ADAM_WD, ADAM_STEP = 0.001, 0.9, 0.999, 1e-08, 0.01, 10

VMEM_LIMIT = 56 * 1024 * 1024


def _cp(*sem):
    return pltpu.CompilerParams(dimension_semantics=tuple(sem), vmem_limit_bytes=VMEM_LIMIT)


def _dot(a, b):
    return jnp.dot(a, b, preferred_element_type=F32)


def _dot_nt(a, b):
    return lax.dot_general(a, b, (((1,), (1,)), ((), ())), preferred_element_type=F32)


def _dot_tn(a, b):
    return lax.dot_general(a, b, (((0,), (0,)), ((), ())), preferred_element_type=F32)


def _sigmoid(x):
    return 1.0 / (1.0 + jnp.exp(-x))


def _silu(x):
    return x * _sigmoid(x)


def _dsilu(x):
    s = _sigmoid(x)
    return s * (1.0 + x * (1.0 - s))


_GELU_C = 0.7978845608028654


def _gelu(x):
    return 0.5 * x * (1.0 + jnp.tanh(_GELU_C * (x + 0.044715 * (x * x * x))))


def _dgelu(x):
    t = jnp.tanh(_GELU_C * (x + 0.044715 * (x * x * x)))
    return 0.5 * (1.0 + t) + 0.5 * x * (1.0 - t * t) * (_GELU_C * (1.0 + 3.0 * 0.044715 * (x * x)))


def _pick(n, cands):
    for c in cands:
        if n % c == 0:
            return c
    return n


def _mm_core(name, operands, in_specs, out_spec, out_shape, grid, nk, dims, acc_shape, has_res, side=None):
    n_in = 3 if has_res else 2
    n_side_in = len(side.srcs) if side else 0
    n_side_out = side.n if side else 0

    def body(*refs):
        a_ref, b_ref = refs[0], refs[1]
        r_ref = refs[2] if has_res else None
        side_in = refs[n_in:n_in + n_side_in]
        o_ref = refs[n_in + n_side_in]
        side_out = refs[n_in + n_side_in + 1:n_in + n_side_in + 1 + n_side_out]
        rest = refs[n_in + n_side_in + 1 + n_side_out:]
        acc, sems = (rest[0], rest[1:]) if nk > 1 else (None, rest)
        i, j, k = pl.program_id(0), pl.program_id(1), pl.program_id(2)
        if side:
            @pl.when((i == 0) & (j == 0) & (k == 0))
            def _():
                side.start(side_in, side_out, sems)

        def product():
            return lax.dot_general(a_ref[...].astype(BF16), b_ref[...].astype(BF16), (dims, ((), ())),
                                   preferred_element_type=F32)

        if nk == 1:
            o_ref[...] = (product() + r_ref[...] if has_res else product()).astype(o_ref.dtype)
        else:
            @pl.when(k == 0)
            def _():
                acc[...] = jnp.zeros_like(acc)

            acc[...] += product()

            @pl.when(k == nk - 1)
            def _():
                r = acc[...]
                if has_res:
                    r = r + r_ref[...]
                o_ref[...] = r.astype(o_ref.dtype)

        if side:
            @pl.when((i == grid[0] - 1) & (j == grid[1] - 1) & (k == grid[2] - 1))
            def _():
                side.wait(side_in, side_out, sems)

    acc_scratch = [pltpu.VMEM(acc_shape, F32)] if nk > 1 else []
    if side:
        return pl.pallas_call(
            body, name=name, grid=grid, in_specs=list(in_specs) + side.in_specs,
            out_specs=(out_spec, *side.out_specs), out_shape=(out_shape, *side.landing),
            scratch_shapes=acc_scratch + side.scratch,
            compiler_params=_cp("arbitrary", "arbitrary", "arbitrary"),
        )(*operands, *side.srcs)
    return pl.pallas_call(
        body, name=name, grid=grid, in_specs=in_specs, out_specs=out_spec, out_shape=out_shape,
        scratch_shapes=acc_scratch,
        compiler_params=_cp("parallel", "parallel", "arbitrary"),
    )(*operands)


def _mm_nn(name, a, b, out_dtype, residual=None):
    m, kk = a.shape
    n = b.shape[1]
    tm, tn, tk = _pick(m, (1024, 512, 256)), _pick(n, (1024, 512)), _pick(kk, (1024, 512))
    ops = [a, b]
    specs = [pl.BlockSpec((tm, tk), lambda i, j, k: (i, k)), pl.BlockSpec((tk, tn), lambda i, j, k: (k, j))]
    if residual is not None:
        ops.append(residual)
        specs.append(pl.BlockSpec((tm, tn), lambda i, j, k: (i, j)))
    return _mm_core(name, ops, specs, pl.BlockSpec((tm, tn), lambda i, j, k: (i, j)),
                    jax.ShapeDtypeStruct((m, n), out_dtype), (m // tm, n // tn, kk // tk), kk // tk,
                    ((1,), (0,)), (tm, tn), residual is not None)


def _mm_nt(name, a, b, out_dtype, residual=None):
    m, kk = a.shape
    n = b.shape[0]
    tm, tn, tk = _pick(m, (1024, 512, 256)), _pick(n, (1024, 512)), _pick(kk, (1024, 512))
    ops = [a, b]
    specs = [pl.BlockSpec((tm, tk), lambda i, j, k: (i, k)), pl.BlockSpec((tn, tk), lambda i, j, k: (j, k))]
    if residual is not None:
        ops.append(residual)
        specs.append(pl.BlockSpec((tm, tn), lambda i, j, k: (i, j)))
    return _mm_core(name, ops, specs, pl.BlockSpec((tm, tn), lambda i, j, k: (i, j)),
                    jax.ShapeDtypeStruct((m, n), out_dtype), (m // tm, n // tn, kk // tk), kk // tk,
                    ((1,), (1,)), (tm, tn), residual is not None)


def _mm_tn(name, a, b, out_dtype):
    kk, m = a.shape
    n = b.shape[1]
    tm, tn, tk = _pick(m, (1024, 512)), _pick(n, (1024, 512)), _pick(kk, (1024, 512, 256))
    specs = [pl.BlockSpec((tk, tm), lambda i, j, k: (k, i)), pl.BlockSpec((tk, tn), lambda i, j, k: (k, j))]
    return _mm_core(name, [a, b], specs, pl.BlockSpec((tm, tn), lambda i, j, k: (i, j)),
                    jax.ShapeDtypeStruct((m, n), out_dtype), (m // tm, n // tn, kk // tk), kk // tk,
                    ((0,), (0,)), (tm, tn), False)


def _mm_nn_slots(name, a, b_slots, out_dtype, side=None):
    m, kk = a.shape
    s, _, ns = b_slots.shape
    tm, tk = _pick(m, (1024, 512, 256)), _pick(kk, (1024, 512))
    specs = [pl.BlockSpec((tm, tk), lambda i, j, k: (i, k)), pl.BlockSpec((None, tk, ns), lambda i, j, k: (j, k, 0))]
    return _mm_core(name, [a, b_slots], specs, pl.BlockSpec((tm, ns), lambda i, j, k: (i, j)),
                    jax.ShapeDtypeStruct((m, s * ns), out_dtype), (m // tm, s, kk // tk), kk // tk,
                    ((1,), (0,)), (tm, ns), False, side)


def _mm_nt_slots(name, a, b_slots, out_dtype, side=None):
    m = a.shape[0]
    s, n, ns = b_slots.shape
    tm, tn = _pick(m, (1024, 512, 256)), _pick(n, (1024, 512))
    specs = [pl.BlockSpec((tm, ns), lambda i, j, k: (i, k)), pl.BlockSpec((None, tn, ns), lambda i, j, k: (k, j, 0))]
    return _mm_core(name, [a, b_slots], specs, pl.BlockSpec((tm, tn), lambda i, j, k: (i, j)),
                    jax.ShapeDtypeStruct((m, n), out_dtype), (m // tm, n // tn, s), s,
                    ((1,), (1,)), (tm, tn), False, side)


def _mm_tn_slots(name, a, b, s, out_dtype, side=None):
    kk, m = a.shape
    ns = b.shape[1] // s
    tm, tk = _pick(m, (1024, 512)), _pick(kk, (1024, 512, 256))
    specs = [pl.BlockSpec((tk, tm), lambda i, j, k: (k, i)), pl.BlockSpec((tk, ns), lambda i, j, k: (k, j))]
    return _mm_core(name, [a, b], specs, pl.BlockSpec((None, tm, ns), lambda i, j, k: (j, i, 0)),
                    jax.ShapeDtypeStruct((s, m, ns), out_dtype), (m // tm, s, kk // tk), kk // tk,
                    ((0,), (0,)), (tm, ns), False, side)


def _rms_fwd(name, x, g):
    r, d = x.shape
    tr = _pick(r, (1024, 512, 256))

    def body(x_ref, g_ref, o_ref):
        xv = x_ref[...]
        rs = lax.rsqrt(jnp.mean(xv * xv, axis=-1, keepdims=True) + EPS)
        o_ref[...] = (xv * rs * g_ref[...]).astype(o_ref.dtype)

    return pl.pallas_call(
        body, name=name, grid=(r // tr,),
        in_specs=[pl.BlockSpec((tr, d), lambda i: (i, 0)), pl.BlockSpec((1, d), lambda i: (0, 0))],
        out_specs=pl.BlockSpec((tr, d), lambda i: (i, 0)),
        out_shape=jax.ShapeDtypeStruct((r, d), BF16), compiler_params=_cp("parallel"),
    )(x, g)


def _rms_bwd(name, x, g, dh, dres):
    r, d = x.shape
    tr = _pick(r, (512, 256))
    has_res = dres is not None

    def body(*refs):
        if has_res:
            x_ref, g_ref, dh_ref, dr_ref, dx_ref, dg_ref = refs
        else:
            x_ref, g_ref, dh_ref, dx_ref, dg_ref = refs
        i = pl.program_id(0)

        @pl.when(i == 0)
        def _():
            dg_ref[...] = jnp.zeros_like(dg_ref)

        xv = x_ref[...]
        dhv = dh_ref[...].astype(F32)
        rs = lax.rsqrt(jnp.mean(xv * xv, axis=-1, keepdims=True) + EPS)
        xn = xv * rs
        dg_ref[...] += jnp.sum(dhv * xn, axis=0, keepdims=True)
        dn = dhv * g_ref[...]
        dx = rs * (dn - xn * jnp.mean(dn * xn, axis=-1, keepdims=True))
        if has_res:
            dx = dx + dr_ref[...]
        dx_ref[...] = dx

    row = pl.BlockSpec((tr, d), lambda i: (i, 0))
    vec = pl.BlockSpec((1, d), lambda i: (0, 0))
    ops = [x, g, dh] + ([dres] if has_res else [])
    return pl.pallas_call(
        body, name=name, grid=(r // tr,),
        in_specs=[row, vec, row] + ([row] if has_res else []),
        out_specs=(row, vec),
        out_shape=(jax.ShapeDtypeStruct((r, d), F32), jax.ShapeDtypeStruct((1, d), F32)),
        compiler_params=_cp("arbitrary"),
    )(*ops)


def _loss_head(x2, gf, target):
    r, d = x2.shape
    tr = _pick(r, (512, 256))

    def body(x_ref, g_ref, t_ref, dx_ref, dg_ref, ls_ref):
        i = pl.program_id(0)

        @pl.when(i == 0)
        def _():
            dg_ref[...] = jnp.zeros_like(dg_ref)
            ls_ref[...] = jnp.zeros_like(ls_ref)

        xv = x_ref[...]
        rs = lax.rsqrt(jnp.mean(xv * xv, axis=-1, keepdims=True) + EPS)
        xn = xv * rs
        e = xn * g_ref[...] - t_ref[...]
        ls_ref[...] += jnp.sum(e * e, axis=0, keepdims=True)
        dy = e * (1.0 / d)
        dg_ref[...] += jnp.sum(dy * xn, axis=0, keepdims=True)
        dn = dy * g_ref[...]
        dx_ref[...] = rs * (dn - xn * jnp.mean(dn * xn, axis=-1, keepdims=True))

    row = pl.BlockSpec((tr, d), lambda i: (i, 0))
    vec = pl.BlockSpec((1, d), lambda i: (0, 0))
    return pl.pallas_call(
        body, name="loss_head", grid=(r // tr,), in_specs=[row, vec, row], out_specs=(row, vec, vec),
        out_shape=(jax.ShapeDtypeStruct((r, d), F32), jax.ShapeDtypeStruct((1, d), F32),
                   jax.ShapeDtypeStruct((1, d), F32)),
        compiler_params=_cp("arbitrary"),
    )(x2, gf, target)


def _rope_tables(pos_col, inv_row):
    l = pos_col.shape[0]
    tl = _pick(l, (1024, 512, 256))

    def body(p_ref, inv_ref, cos_ref, sin_ref):
        ang = p_ref[...].astype(F32) * inv_ref[...]
        lane = lax.broadcasted_iota(jnp.int32, ang.shape, 1)
        c = jnp.cos(ang)
        s = jnp.where((lane % RET_DK) < RET_DK // 2, -jnp.sin(ang), jnp.sin(ang))
        cos_ref[...] = jnp.tile(c, (1, RET_QK // 128))
        sin_ref[...] = jnp.tile(s, (1, RET_QK // 128))

    return pl.pallas_call(
        body, name="rope_tables", grid=(l // tl,),
        in_specs=[pl.BlockSpec((tl, 1), lambda i: (i, 0)), pl.BlockSpec((1, 128), lambda i: (0, 0))],
        out_specs=(pl.BlockSpec((tl, RET_QK), lambda i: (i, 0)), pl.BlockSpec((tl, RET_QK), lambda i: (i, 0))),
        out_shape=(jax.ShapeDtypeStruct((l, RET_QK), F32), jax.ShapeDtypeStruct((l, RET_QK), F32)),
        compiler_params=_cp("parallel"),
    )(pos_col, inv_row)


def _rot(x, cos_t, sin_t):
    n = x.shape[-1]
    lane = lax.broadcasted_iota(jnp.int32, x.shape, 1)
    partner = jnp.where((lane % RET_DK) < RET_DK // 2, pltpu.roll(x, n - RET_DK // 2, 1), pltpu.roll(x, RET_DK // 2, 1))
    return x * cos_t + partner * sin_t


def _ret_constants(c):
    log_g = jnp.log1p(-jnp.exp2(-5.0 - jnp.arange(RET_HEADS, dtype=F32)))
    j = jnp.arange(c, dtype=F32)
    diff = j[:, None] - j[None, :]
    decay = jnp.where(diff[None] >= 0.0, jnp.exp(log_g[:, None, None] * jnp.maximum(diff, 0.0)[None]), 0.0)
    q_w = jnp.exp(log_g[None, :] * (j + 1.0)[:, None])
    k_w = jnp.exp(log_g[None, :] * (c - 1.0 - j)[:, None])
    cd = jnp.exp(log_g * c)
    rep = lambda t: jnp.repeat(t, RET_DK, axis=1)
    cd_row = jnp.repeat(cd, RET_DV)[None, :]
    return decay, rep(q_w), rep(k_w), cd_row


def _ret_fwd(proj, cos_t, sin_t, consts, gn_g, c):
    l = proj.shape[0]
    nc = l // c
    decay, qw, kw, cd_row = consts

    def body(q_ref, k_ref, v_ref, g_ref, cos_ref, sin_ref, dec_ref, qw_ref, kw_ref, cd_ref, gn_ref,
             ret_ref, o_ref, rp_ref, state):
        @pl.when(pl.program_id(0) == 0)
        def _():
            state[...] = jnp.zeros_like(state)

        cs, sn = cos_ref[...], sin_ref[...]
        qr = _rot(q_ref[...].astype(F32), cs, sn)
        kr = _rot(k_ref[...].astype(F32), cs, sn) * (RET_DK ** -0.5)
        qb, kb = qr.astype(BF16), kr.astype(BF16)
        qwb = (qr * qw_ref[...]).astype(BF16)
        kwb = (kr * kw_ref[...]).astype(BF16)
        vb = v_ref[...].astype(BF16)
        for h in range(RET_HEADS):
            qs = slice(h * RET_DK, (h + 1) * RET_DK)
            vs = slice(h * RET_DV, (h + 1) * RET_DV)
            s = _dot_nt(qb[:, qs], kb[:, qs]) * dec_ref[h]
            r_prev = state[h]
            rp_ref[0, h] = r_prev
            o = _dot(s.astype(BF16), vb[:, vs]) + _dot(qwb[:, qs], r_prev.astype(BF16))
            state[h] = cd_ref[:, vs] * r_prev + _dot_tn(kwb[:, qs], vb[:, vs])
            o_ref[:, vs] = o
            mu = jnp.mean(o, axis=-1, keepdims=True)
            var = jnp.mean(jnp.square(o - mu), axis=-1, keepdims=True)
            on = (o - mu) * lax.rsqrt(var + EPS)
            ret_ref[:, vs] = (on * gn_ref[:, vs] * _silu(g_ref[:, vs].astype(F32))).astype(ret_ref.dtype)

    const2 = lambda shape: pl.BlockSpec(shape, lambda i: (0,) * len(shape))
    return pl.pallas_call(
        body, name="retention_fwd", grid=(nc,),
        in_specs=[pl.BlockSpec((c, RET_QK), lambda i: (i, 0)), pl.BlockSpec((c, RET_QK), lambda i: (i, 1)),
                  pl.BlockSpec((c, D_MODEL), lambda i: (i, 1)), pl.BlockSpec((c, D_MODEL), lambda i: (i, 2)),
                  pl.BlockSpec((c, RET_QK), lambda i: (i, 0)), pl.BlockSpec((c, RET_QK), lambda i: (i, 0)),
                  const2((RET_HEADS, c, c)), const2((c, RET_QK)), const2((c, RET_QK)), const2((1, D_MODEL)),
                  const2((1, D_MODEL))],
        out_specs=(pl.BlockSpec((c, D_MODEL), lambda i: (i, 0)), pl.BlockSpec((c, D_MODEL), lambda i: (i, 0)),
                   pl.BlockSpec((1, RET_HEADS, RET_DK, RET_DV), lambda i: (i, 0, 0, 0))),
        out_shape=(jax.ShapeDtypeStruct((l, D_MODEL), BF16), jax.ShapeDtypeStruct((l, D_MODEL), F32),
                   jax.ShapeDtypeStruct((nc, RET_HEADS, RET_DK, RET_DV), F32)),
        scratch_shapes=[pltpu.VMEM((RET_HEADS, RET_DK, RET_DV), F32)],
        compiler_params=_cp("arbitrary"),
    )(proj, proj, proj, proj, cos_t, sin_t, decay, qw, kw, cd_row, gn_g)


def _ret_bwd(proj, cos_t, sin_t, consts, gn_g, o_saved, r_prev_saved, dmix, c, side):
    l = proj.shape[0]
    nc = l // c
    decay, qw, kw, cd_row = consts
    n_in = 14

    def body(*refs):
        (q_ref, k_ref, v_ref, g_ref, cos_ref, sin_ref, dec_ref, qw_ref, kw_ref, cd_ref, gn_ref, o_ref, rp_ref,
         dr_ref) = refs[:n_in]
        side_in = refs[n_in:n_in + len(side.srcs)]
        out_ref, dgn_ref = refs[n_in + len(side.srcs):n_in + len(side.srcs) + 2]
        side_out = refs[n_in + len(side.srcs) + 2:n_in + len(side.srcs) + 2 + side.n]
        state, dq_s, dk_s = refs[n_in + len(side.srcs) + 2 + side.n:n_in + len(side.srcs) + 5 + side.n]
        sems = refs[n_in + len(side.srcs) + 5 + side.n:]

        @pl.when(pl.program_id(0) == 0)
        def _():
            side.start(side_in, side_out, sems)
            state[...] = jnp.zeros_like(state)
            dgn_ref[...] = jnp.zeros_like(dgn_ref)

        cs, sn = cos_ref[...], sin_ref[...]
        qr = _rot(q_ref[...].astype(F32), cs, sn)
        kr = _rot(k_ref[...].astype(F32), cs, sn) * (RET_DK ** -0.5)
        qb, kb = qr.astype(BF16), kr.astype(BF16)
        qwb = (qr * qw_ref[...]).astype(BF16)
        kwv = kw_ref[...]
        kwb = (kr * kwv).astype(BF16)
        qwv = qw_ref[...]
        vb = v_ref[...].astype(BF16)
        for h in range(RET_HEADS):
            qs = slice(h * RET_DK, (h + 1) * RET_DK)
            vs = slice(h * RET_DV, (h + 1) * RET_DV)
            dec = dec_ref[h]
            o = o_ref[:, vs]
            mu = jnp.mean(o, axis=-1, keepdims=True)
            var = jnp.mean(jnp.square(o - mu), axis=-1, keepdims=True)
            rstd = lax.rsqrt(var + EPS)
            on = (o - mu) * rstd
            gate = g_ref[:, vs].astype(F32)
            sg = _silu(gate)
            dret = dr_ref[:, vs].astype(F32)
            gn = gn_ref[:, vs]
            dgn_ref[:, vs] += jnp.sum(dret * on * sg, axis=0, keepdims=True)
            out_ref[:, 2 * RET_QK + D_MODEL + h * RET_DV:2 * RET_QK + D_MODEL + (h + 1) * RET_DV] = (
                dret * on * gn * _dsilu(gate)).astype(out_ref.dtype)
            don = dret * gn * sg
            do = rstd * (don - jnp.mean(don, axis=-1, keepdims=True)
                         - on * jnp.mean(don * on, axis=-1, keepdims=True))
            dob = do.astype(BF16)
            sn_h = state[h]
            snb = sn_h.astype(BF16)
            s = _dot_nt(qb[:, qs], kb[:, qs]) * dec
            dv = _dot_tn(s.astype(BF16), dob) + _dot(kwb[:, qs], snb)
            out_ref[:, 2 * RET_QK + h * RET_DV:2 * RET_QK + (h + 1) * RET_DV] = dv.astype(out_ref.dtype)
            ds = (_dot_nt(dob, vb[:, vs]) * dec).astype(BF16)
            dq_s[:, qs] = _dot(ds, kb[:, qs]) + qwv[:, qs] * _dot_nt(dob, rp_ref[0, h].astype(BF16))
            dk_s[:, qs] = _dot_tn(ds, qb[:, qs]) + kwv[:, qs] * _dot_nt(vb[:, vs], snb)
            state[h] = cd_ref[:, vs] * sn_h + _dot_tn(qwb[:, qs], dob)
        out_ref[:, 0:RET_QK] = _rot(dq_s[...], cs, -sn).astype(out_ref.dtype)
        out_ref[:, RET_QK:2 * RET_QK] = (_rot(dk_s[...], cs, -sn) * (RET_DK ** -0.5)).astype(out_ref.dtype)

        @pl.when(pl.program_id(0) == nc - 1)
        def _():
            side.wait(side_in, side_out, sems)

    rev = lambda i: nc - 1 - i
    const2 = lambda shape: pl.BlockSpec(shape, lambda i: (0,) * len(shape))
    return pl.pallas_call(
        body, name="retention_bwd", grid=(nc,),
        in_specs=[pl.BlockSpec((c, RET_QK), lambda i: (rev(i), 0)), pl.BlockSpec((c, RET_QK), lambda i: (rev(i), 1)),
                  pl.BlockSpec((c, D_MODEL), lambda i: (rev(i), 1)), pl.BlockSpec((c, D_MODEL), lambda i: (rev(i), 2)),
                  pl.BlockSpec((c, RET_QK), lambda i: (rev(i), 0)), pl.BlockSpec((c, RET_QK), lambda i: (rev(i), 0)),
                  const2((RET_HEADS, c, c)), const2((c, RET_QK)), const2((c, RET_QK)), const2((1, D_MODEL)),
                  const2((1, D_MODEL)),
                  pl.BlockSpec((c, D_MODEL), lambda i: (rev(i), 0)),
                  pl.BlockSpec((1, RET_HEADS, RET_DK, RET_DV), lambda i: (rev(i), 0, 0, 0)),
                  pl.BlockSpec((c, D_MODEL), lambda i: (rev(i), 0))] + side.in_specs,
        out_specs=(pl.BlockSpec((c, 2 * RET_QK + 2 * D_MODEL), lambda i: (rev(i), 0)), const2((1, D_MODEL)),
                   *side.out_specs),
        out_shape=(jax.ShapeDtypeStruct((l, 2 * RET_QK + 2 * D_MODEL), BF16), jax.ShapeDtypeStruct((1, D_MODEL), F32),
                   *side.landing),
        scratch_shapes=[pltpu.VMEM((RET_HEADS, RET_DK, RET_DV), F32), pltpu.VMEM((c, RET_QK), F32),
                        pltpu.VMEM((c, RET_QK), F32)] + side.scratch,
        compiler_params=_cp("arbitrary"),
    )(proj, proj, proj, proj, cos_t, sin_t, decay, qw, kw, cd_row, gn_g, o_saved, r_prev_saved, dmix, *side.srcs)


def _zoh(a_re, a_im, log_dt):
    dt = jnp.exp(log_dt)
    mag = jnp.exp(a_re * dt)
    abar_re = mag * jnp.cos(a_im * dt)
    abar_im = mag * jnp.sin(a_im * dt)
    den = a_re * a_re + a_im * a_im
    nr, ni = abar_re - 1.0, abar_im
    f_re = (nr * a_re + ni * a_im) / den
    f_im = (ni * a_re - nr * a_im) / den
    return dt, abar_re, abar_im, f_re, f_im, den


def _lanes_p(f):
    return jnp.tile(f, (1, S5_P))


def _s5_discretize(a_re, a_im, log_dt, b_re_t, b_im_t):
    def body(ar_ref, ai_ref, ld_ref, br_ref, bi_ref, abr_ref, abi_ref, bbr_ref, bbi_ref):
        _, abar_re, abar_im, f_re, f_im, _ = _zoh(ar_ref[...], ai_ref[...], ld_ref[...])
        abr_ref[...] = abar_re
        abi_ref[...] = abar_im
        fr, fi = _lanes_p(f_re), _lanes_p(f_im)
        bbr_ref[...] = fr * br_ref[...] - fi * bi_ref[...]
        bbi_ref[...] = fr * bi_ref[...] + fi * br_ref[...]

    gn = jax.ShapeDtypeStruct((S5_G, S5_N), F32)
    gpn = jax.ShapeDtypeStruct((S5_G, S5_P * S5_N), F32)
    return pl.pallas_call(body, name="s5_discretize", out_shape=(gn, gn, gpn, gpn))(a_re, a_im, log_dt, b_re_t, b_im_t)


def _s5_discretize_bwd(a_re, a_im, log_dt, b_re_t, b_im_t, dab_re, dab_im, dbb_re_t, dbb_im_t):
    def body(ar_ref, ai_ref, ld_ref, br_ref, bi_ref, gar_ref, gai_ref, gbr_ref, gbi_ref,
             dar_ref, dai_ref, dld_ref, dbr_ref, dbi_ref):
        a_r, a_i = ar_ref[...], ai_ref[...]
        dt, abar_re, abar_im, f_re, f_im, den = _zoh(a_r, a_i, ld_ref[...])
        b_r, b_i, g_br, g_bi = br_ref[...], bi_ref[...], gbr_ref[...], gbi_ref[...]
        fr, fi = _lanes_p(f_re), _lanes_p(f_im)
        dbr_ref[...] = fr * g_br + fi * g_bi
        dbi_ref[...] = fr * g_bi - fi * g_br
        t_r = b_r * g_br + b_i * g_bi
        t_i = b_r * g_bi - b_i * g_br
        gf_r = sum(t_r[:, p * S5_N:(p + 1) * S5_N] for p in range(S5_P))
        gf_i = sum(t_i[:, p * S5_N:(p + 1) * S5_N] for p in range(S5_P))
        inv_r, inv_i = a_r / den, a_i / den
        ga_r = gar_ref[...] + gf_r * inv_r - gf_i * inv_i
        ga_i = gai_ref[...] + gf_r * inv_i + gf_i * inv_r
        q_r = -(f_re * a_r + f_im * a_i) / den
        q_i = -(f_im * a_r - f_re * a_i) / den
        gl_r = q_r * gf_r + q_i * gf_i
        gl_i = q_r * gf_i - q_i * gf_r
        dar_ref[...] = gl_r + dt * (abar_re * ga_r + abar_im * ga_i)
        dai_ref[...] = gl_i + dt * (abar_re * ga_i - abar_im * ga_r)
        la_r = a_r * abar_re - a_i * abar_im
        la_i = a_r * abar_im + a_i * abar_re
        dld_ref[...] = dt * jnp.sum(ga_r * la_r + ga_i * la_i, axis=-1, keepdims=True)

    gn = jax.ShapeDtypeStruct((S5_G, S5_N), F32)
    gpn = jax.ShapeDtypeStruct((S5_G, S5_P * S5_N), F32)
    return pl.pallas_call(
        body, name="s5_discretize_bwd", out_shape=(gn, gn, jax.ShapeDtypeStruct((S5_G, 1), F32), gpn, gpn),
    )(a_re, a_im, log_dt, b_re_t, b_im_t, dab_re, dab_im, dbb_re_t, dbb_im_t)


S5_ZQ = S5_NB // 2


def _s5_z(re, im):
    return jnp.concatenate([re.reshape(S5_ZQ, 8, 128), im.reshape(S5_ZQ, 8, 128)], axis=0)


def _s5_unz(z):
    return z[:S5_ZQ].reshape(S5_G, S5_N), z[S5_ZQ:].reshape(S5_G, S5_N)


def _s5_block_mats(bb_re, bb_im, c_re, c_im):
    eye = jnp.eye(S5_GB, dtype=F32)
    bb = jnp.stack([bb_re, bb_im], axis=0).reshape(2, S5_NB, S5_GB, S5_N, S5_P)
    bbm = jnp.einsum("rbgnp,gh->bgprhn", bb, eye).reshape(S5_NB, S5_GB * S5_P, 2 * S5_BS)
    cc = jnp.stack([c_re, -c_im], axis=0).reshape(2, S5_NB, S5_GB, S5_P, S5_N)
    ccm = jnp.einsum("rbgpn,gh->brhngp", cc, eye).reshape(S5_NB, 2 * S5_BS, S5_GB * S5_P)
    return bbm.astype(BF16), ccm.astype(BF16)


def _s5_block_diag_bb(m):
    t = m.reshape(S5_NB, S5_GB, S5_P, 2, S5_GB, S5_N)
    d = jnp.einsum("bgprgn->rbgnp", t).reshape(2, S5_G, S5_N, S5_P)
    return d[0], d[1]


def _s5_block_diag_cc(m):
    t = m.reshape(S5_NB, 2, S5_GB, S5_N, S5_GB, S5_P)
    d = jnp.einsum("brgngp->rbgpn", t).reshape(2, S5_G, S5_P, S5_N)
    return d[0], -d[1]


SCAN_UNROLL = 8


def _z_store(zr, zi, blk, res, t, off):
    q, h = blk // 2, blk % 2
    for lt in range(4):
        zr[q, pl.ds(off + 4 * h + lt, t, stride=8), :] = res[:, lt * 128:(lt + 1) * 128]
        zi[q, pl.ds(off + 4 * h + lt, t, stride=8), :] = res[:, S5_BS + lt * 128:S5_BS + (lt + 1) * 128]


def _z_load(zr, zi, blk, t, off):
    q, h = blk // 2, blk % 2
    return jnp.concatenate([zr[q, pl.ds(off + 4 * h + lt, t, stride=8), :] for lt in range(4)]
                           + [zi[q, pl.ds(off + 4 * h + lt, t, stride=8), :] for lt in range(4)], axis=1)


def _z_scan_fwd(zr, zi, a_ref, carry_ref, t, off):
    ar = [a_ref[q] for q in range(S5_ZQ)]
    ai = [a_ref[S5_ZQ + q] for q in range(S5_ZQ)]

    def step(it, carry):
        carry = list(carry)
        base = pl.multiple_of(it * (8 * SCAN_UNROLL), 8 * SCAN_UNROLL) + off
        for tt in range(SCAN_UNROLL):
            rows = pl.ds(base + 8 * tt, 8)
            for q in range(S5_ZQ):
                c_r, c_i = carry[q], carry[S5_ZQ + q]
                n_r = ar[q] * c_r - ai[q] * c_i + zr[q, rows, :]
                n_i = ar[q] * c_i + ai[q] * c_r + zi[q, rows, :]
                zr[q, rows, :] = n_r
                zi[q, rows, :] = n_i
                carry[q], carry[S5_ZQ + q] = n_r, n_i
        return tuple(carry)

    out = lax.fori_loop(0, t // SCAN_UNROLL, step, tuple(carry_ref[k] for k in range(2 * S5_ZQ)))
    for k in range(2 * S5_ZQ):
        carry_ref[k] = out[k]


def _z_scan_bwd(lr, li, xr, xi, a_ref, carry_ref, acc_ref, t):
    ar = [a_ref[q] for q in range(S5_ZQ)]
    ai = [a_ref[S5_ZQ + q] for q in range(S5_ZQ)]
    n_it = t // SCAN_UNROLL

    def step(it, state):
        carry, acc = list(state[0]), list(state[1])
        base = pl.multiple_of((n_it - 1 - it) * (8 * SCAN_UNROLL), 8 * SCAN_UNROLL)
        for tt in reversed(range(SCAN_UNROLL)):
            rows = pl.ds(base + 8 * tt, 8)
            for q in range(S5_ZQ):
                c_r, c_i = carry[q], carry[S5_ZQ + q]
                n_r = ar[q] * c_r + ai[q] * c_i + lr[q, rows, :]
                n_i = ar[q] * c_i - ai[q] * c_r + li[q, rows, :]
                lr[q, rows, :] = n_r
                li[q, rows, :] = n_i
                p_r, p_i = xr[q, rows, :], xi[q, rows, :]
                acc[q] = acc[q] + n_r * p_r + n_i * p_i
                acc[S5_ZQ + q] = acc[S5_ZQ + q] + n_i * p_r - n_r * p_i
                carry[q], carry[S5_ZQ + q] = n_r, n_i
        return tuple(carry), tuple(acc)

    k8 = range(2 * S5_ZQ)
    carry, acc = lax.fori_loop(0, n_it, step, (tuple(carry_ref[k] for k in k8), tuple(acc_ref[k] for k in k8)))
    for k in k8:
        carry_ref[k] = carry[k]
        acc_ref[k] = acc[k]


def _s5_fwd(proj, bbm, ccm, d_row, glu_w, glu_b, tabs, t, side):
    l = proj.shape[0]
    nt = l // t
    n_in = 8

    def body(*refs):
        u_ref, gs_ref, bb_ref, cc_ref, d_ref, gw_ref, gb_ref, a_ref = refs[:n_in]
        side_in = refs[n_in:n_in + len(side.srcs)]
        ssm_ref, xst_ref = refs[n_in + len(side.srcs):n_in + len(side.srcs) + 2]
        side_out = refs[n_in + len(side.srcs) + 2:n_in + len(side.srcs) + 2 + side.n]
        zr, zi, carry = refs[n_in + len(side.srcs) + 2 + side.n:n_in + len(side.srcs) + 5 + side.n]
        sems = refs[n_in + len(side.srcs) + 5 + side.n:]

        @pl.when(pl.program_id(0) == 0)
        def _():
            side.start(side_in, side_out, sems)
            carry[...] = jnp.zeros_like(carry)

        xst_ref[0] = carry[...]
        ub = u_ref[...]
        u = ub.astype(F32)
        for blk in range(S5_NB):
            _z_store(zr, zi, blk, _dot(ub[:, blk * 128:(blk + 1) * 128], bb_ref[blk]), t, 0)
        _z_scan_fwd(zr, zi, a_ref, carry, t, 0)
        ys = jnp.concatenate(
            [_dot(_z_load(zr, zi, blk, t, 0).astype(BF16), cc_ref[blk]) for blk in range(S5_NB)], axis=1)
        y2 = _gelu(ys + d_ref[...] * u)
        z = _dot(y2.astype(BF16), gw_ref[...]) + gb_ref[...]
        ssm_ref[...] = (y2 * _sigmoid(z) * _silu(gs_ref[...].astype(F32))).astype(ssm_ref.dtype)

        @pl.when(pl.program_id(0) == nt - 1)
        def _():
            side.wait(side_in, side_out, sems)

    const2 = lambda shape: pl.BlockSpec(shape, lambda i: (0,) * len(shape))
    zshape = (2 * S5_ZQ, 8, 128)
    return pl.pallas_call(
        body, name="s5_fwd", grid=(nt,),
        in_specs=[pl.BlockSpec((t, D_MODEL), lambda i: (i, 3)), pl.BlockSpec((t, D_MODEL), lambda i: (i, 4)),
                  const2(bbm.shape), const2(ccm.shape), const2((1, D_MODEL)), const2((D_MODEL, D_MODEL)),
                  const2((1, D_MODEL)), const2(zshape)] + side.in_specs,
        out_specs=(pl.BlockSpec((t, D_MODEL), lambda i: (i, 0)), pl.BlockSpec((1,) + zshape, lambda i: (i, 0, 0, 0)),
                   *side.out_specs),
        out_shape=(jax.ShapeDtypeStruct((l, D_MODEL), BF16), jax.ShapeDtypeStruct((nt,) + zshape, F32),
                   *side.landing),
        scratch_shapes=[pltpu.VMEM((S5_ZQ, 8 * t, 128), F32), pltpu.VMEM((S5_ZQ, 8 * t, 128), F32),
                        pltpu.VMEM(zshape, F32)] + side.scratch,
        compiler_params=_cp("arbitrary"),
    )(proj, proj, bbm, ccm, d_row, glu_w, glu_b, tabs, *side.srcs)


def _s5_bwd(proj, dmix, xstart, bbm, ccm, d_row, glu_w, glu_b, tabs, t):
    l = proj.shape[0]
    nt = l // t

    def body(u_ref, gs_ref, dm_ref, xst_ref, bb_ref, cc_ref, d_ref, gw_ref, gb_ref, a_ref,
             dug_ref, y2_ref, dz_ref, dbb_ref, dcc_ref, da_ref, dd_ref, dgb_ref, xr, xi, lr, li, carry, lcarry):
        @pl.when(pl.program_id(0) == 0)
        def _():
            lcarry[...] = jnp.zeros_like(lcarry)
            dbb_ref[...] = jnp.zeros_like(dbb_ref)
            dcc_ref[...] = jnp.zeros_like(dcc_ref)
            da_ref[...] = jnp.zeros_like(da_ref)
            dd_ref[...] = jnp.zeros_like(dd_ref)
            dgb_ref[...] = jnp.zeros_like(dgb_ref)

        carry[...] = xst_ref[0]
        for q in range(S5_ZQ):
            xr[q, 0:8, :] = carry[q]
            xi[q, 0:8, :] = carry[S5_ZQ + q]
        ub = u_ref[...]
        u = ub.astype(F32)
        for blk in range(S5_NB):
            _z_store(xr, xi, blk, _dot(ub[:, blk * 128:(blk + 1) * 128], bb_ref[blk]), t, 8)
        _z_scan_fwd(xr, xi, a_ref, carry, t, 8)
        ys = jnp.concatenate(
            [_dot(_z_load(xr, xi, blk, t, 8).astype(BF16), cc_ref[blk]) for blk in range(S5_NB)], axis=1)
        dv = d_ref[...]
        y1 = ys + dv * u
        y2 = _gelu(y1)
        y2b = y2.astype(BF16)
        sg = _sigmoid(_dot(y2b, gw_ref[...]) + gb_ref[...])
        gs = gs_ref[...].astype(F32)
        dssm = dm_ref[...].astype(F32)
        dug_ref[:, D_MODEL:] = (dssm * (y2 * sg) * _dsilu(gs)).astype(dug_ref.dtype)
        dy3 = dssm * _silu(gs)
        dz = dy3 * y2 * sg * (1.0 - sg)
        dzb = dz.astype(BF16)
        y2_ref[...] = y2b
        dz_ref[...] = dzb
        dgb_ref[...] += jnp.sum(dz, axis=0, keepdims=True)
        dy1 = (dy3 * sg + _dot_nt(dzb, gw_ref[...])) * _dgelu(y1)
        dd_ref[...] += jnp.sum(dy1 * u, axis=0, keepdims=True)
        dyb = dy1.astype(BF16)
        for blk in range(S5_NB):
            ch = slice(blk * 128, (blk + 1) * 128)
            _z_store(lr, li, blk, _dot_nt(dyb[:, ch], cc_ref[blk]), t, 0)
            dcc_ref[blk] += _dot_tn(_z_load(xr, xi, blk, t, 8).astype(BF16), dyb[:, ch])
        _z_scan_bwd(lr, li, xr, xi, a_ref, lcarry, da_ref, t)
        du = []
        for blk in range(S5_NB):
            lb = _z_load(lr, li, blk, t, 0).astype(BF16)
            du.append(_dot_nt(lb, bb_ref[blk]))
            dbb_ref[blk] += _dot_tn(ub[:, blk * 128:(blk + 1) * 128], lb)
        dug_ref[:, :D_MODEL] = (jnp.concatenate(du, axis=1) + dy1 * dv).astype(dug_ref.dtype)

    rev = lambda i: nt - 1 - i
    const2 = lambda shape: pl.BlockSpec(shape, lambda i: (0,) * len(shape))
    row_out = lambda w: pl.BlockSpec((t, w), lambda i: (rev(i), 0))
    zshape = (2 * S5_ZQ, 8, 128)
    return pl.pallas_call(
        body, name="s5_bwd", grid=(nt,),
        in_specs=[pl.BlockSpec((t, D_MODEL), lambda i: (rev(i), 3)), pl.BlockSpec((t, D_MODEL), lambda i: (rev(i), 4)),
                  pl.BlockSpec((t, D_MODEL), lambda i: (rev(i), 1)),
                  pl.BlockSpec((1,) + zshape, lambda i: (rev(i), 0, 0, 0)),
                  const2(bbm.shape), const2(ccm.shape), const2((1, D_MODEL)), const2((D_MODEL, D_MODEL)),
                  const2((1, D_MODEL)), const2(zshape)],
        out_specs=(row_out(2 * D_MODEL), row_out(D_MODEL), row_out(D_MODEL), const2(bbm.shape), const2(ccm.shape),
                   const2(zshape), const2((1, D_MODEL)), const2((1, D_MODEL))),
        out_shape=(jax.ShapeDtypeStruct((l, 2 * D_MODEL), BF16), jax.ShapeDtypeStruct((l, D_MODEL), BF16),
                   jax.ShapeDtypeStruct((l, D_MODEL), BF16), jax.ShapeDtypeStruct(bbm.shape, F32),
                   jax.ShapeDtypeStruct(ccm.shape, F32), jax.ShapeDtypeStruct(zshape, F32),
                   jax.ShapeDtypeStruct((1, D_MODEL), F32), jax.ShapeDtypeStruct((1, D_MODEL), F32)),
        scratch_shapes=[pltpu.VMEM((S5_ZQ, 8 * t + 8, 128), F32), pltpu.VMEM((S5_ZQ, 8 * t + 8, 128), F32),
                        pltpu.VMEM((S5_ZQ, 8 * t, 128), F32), pltpu.VMEM((S5_ZQ, 8 * t, 128), F32),
                        pltpu.VMEM(zshape, F32), pltpu.VMEM(zshape, F32)],
        compiler_params=_cp("arbitrary"),
    )(proj, proj, dmix, xstart, bbm, ccm, d_row, glu_w, glu_b, tabs)


def _attn_probs(qh, kh):
    s = _dot_nt(qh, kh) * (XA_DH ** -0.5)
    e = jnp.exp(s - jnp.max(s, axis=-1, keepdims=True))
    return e / jnp.sum(e, axis=-1, keepdims=True)


def _attn_fwd(qa, ka, va):
    l = qa.shape[0]
    m = ka.shape[0]
    tl = _pick(l, (512, 256))

    def body(q_ref, k_ref, v_ref, o_ref):
        for h in range(XA_HEADS):
            hs = slice(h * XA_DH, (h + 1) * XA_DH)
            p = _attn_probs(q_ref[:, hs], k_ref[:, hs])
            o_ref[:, hs] = _dot(p.astype(BF16), v_ref[:, hs]).astype(o_ref.dtype)

    return pl.pallas_call(
        body, name="xattn_fwd", grid=(l // tl,),
        in_specs=[pl.BlockSpec((tl, D_MODEL), lambda i: (i, 0)), pl.BlockSpec((m, D_MODEL), lambda i: (0, 0)),
                  pl.BlockSpec((m, D_MODEL), lambda i: (0, 0))],
        out_specs=pl.BlockSpec((tl, D_MODEL), lambda i: (i, 0)),
        out_shape=jax.ShapeDtypeStruct((l, D_MODEL), BF16), compiler_params=_cp("parallel"),
    )(qa, ka, va)


def _attn_bwd(qa, ka, va, doa):
    l = qa.shape[0]
    m = ka.shape[0]
    tl = _pick(l, (512, 256))

    def body(q_ref, k_ref, v_ref, do_ref, dq_ref, dk_ref, dv_ref):
        @pl.when(pl.program_id(0) == 0)
        def _():
            dk_ref[...] = jnp.zeros_like(dk_ref)
            dv_ref[...] = jnp.zeros_like(dv_ref)

        for h in range(XA_HEADS):
            hs = slice(h * XA_DH, (h + 1) * XA_DH)
            qh, kh, vh, doh = q_ref[:, hs], k_ref[:, hs], v_ref[:, hs], do_ref[:, hs]
            p = _attn_probs(qh, kh)
            dv_ref[:, hs] += _dot_tn(p.astype(BF16), doh)
            dp = _dot_nt(doh, vh)
            ds = (p * (dp - jnp.sum(dp * p, axis=-1, keepdims=True)) * (XA_DH ** -0.5)).astype(BF16)
            dq_ref[:, hs] = _dot(ds, kh).astype(dq_ref.dtype)
            dk_ref[:, hs] += _dot_tn(ds, qh)

    row = pl.BlockSpec((tl, D_MODEL), lambda i: (i, 0))
    mem = pl.BlockSpec((m, D_MODEL), lambda i: (0, 0))
    return pl.pallas_call(
        body, name="xattn_bwd", grid=(l // tl,), in_specs=[row, mem, mem, row], out_specs=(row, mem, mem),
        out_shape=(jax.ShapeDtypeStruct((l, D_MODEL), BF16), jax.ShapeDtypeStruct((m, D_MODEL), F32),
                   jax.ShapeDtypeStruct((m, D_MODEL), F32)),
        compiler_params=_cp("arbitrary"),
    )(qa, ka, va, doa)


def _me_and_peers():
    x, y, c = lax.axis_index("x"), lax.axis_index("y"), lax.axis_index("c")
    flip = lambda v, bit: (1 - v) if bit else v
    peers = []
    for k in range(1, N_DEV):
        px, py, pc = flip(x, (k >> 2) & 1), flip(y, (k >> 1) & 1), flip(c, k & 1)
        peers.append(((px, py, pc), 4 * px + 2 * py + pc))
    return 4 * x + 2 * y + c, peers


class _SideJob:
    def __init__(self, srcs, landing, src_of, dst_of):
        self.srcs = list(srcs)
        self.landing = list(landing)
        self.n = len(self.landing)
        self.src_of, self.dst_of = src_of, dst_of
        hbm = pl.BlockSpec(memory_space=pl.ANY)
        self.in_specs = [hbm] * len(self.srcs)
        self.out_specs = [hbm] * self.n
        self.scratch = [pltpu.SemaphoreType.DMA((self.n * (N_DEV - 1),)), pltpu.SemaphoreType.DMA((self.n * (N_DEV - 1),)),
                        pltpu.SemaphoreType.DMA((self.n,))]

    def _copies(self, src_refs, out_refs, sems):
        send_sems, recv_sems, loc_sems = sems
        me, peers = _me_and_peers()
        local = [pltpu.make_async_copy(self.src_of(a, me, src_refs), self.dst_of(a, me, out_refs), loc_sems.at[a])
                 for a in range(self.n)]
        sends, recvs = [], []
        for k, (peer, peer_idx) in enumerate(peers):
            for a in range(self.n):
                s = self.n * k + a
                sends.append(pltpu.make_async_remote_copy(
                    src_ref=self.src_of(a, peer_idx, src_refs), dst_ref=self.dst_of(a, me, out_refs),
                    send_sem=send_sems.at[s], recv_sem=recv_sems.at[s], device_id=peer, device_id_type=MESH))
                recvs.append(pltpu.make_async_remote_copy(
                    src_ref=self.src_of(a, me, src_refs), dst_ref=self.dst_of(a, peer_idx, out_refs),
                    send_sem=send_sems.at[s], recv_sem=recv_sems.at[s], device_id=peer, device_id_type=MESH))
        return local, sends, recvs

    def start(self, src_refs, out_refs, sems):
        local, sends, _ = self._copies(src_refs, out_refs, sems)
        for cp in local + sends:
            cp.start()

    def wait(self, src_refs, out_refs, sems):
        local, sends, recvs = self._copies(src_refs, out_refs, sems)
        for cp in recvs:
            cp.wait_recv()
        for cp in sends:
            cp.wait_send()
        for cp in local:
            cp.wait()


def _gather_job(shards):
    return _SideJob(shards, [jax.ShapeDtypeStruct((N_DEV,) + s.shape, s.dtype) for s in shards],
                    src_of=lambda a, j, srcs: srcs[a], dst_of=lambda a, j, outs: outs[a].at[j])


def _scatter_job(grads):
    landing, parts = [], []
    for g in grads:
        if g.ndim == 3:
            landing.append(jax.ShapeDtypeStruct(g.shape, g.dtype))
            parts.append(None)
        else:
            r = g.shape[0] // N_DEV
            landing.append(jax.ShapeDtypeStruct((N_DEV, r, g.shape[1]), g.dtype))
            parts.append(r)

    def src_of(a, j, srcs):
        if parts[a] is None:
            return srcs[a].at[j]
        return srcs[a].at[pl.ds(pl.multiple_of(j * parts[a], 8), parts[a]), :]

    return _SideJob(grads, landing, src_of=src_of, dst_of=lambda a, j, outs: outs[a].at[j])


def _allgather_w_in(w_in_shard, row_shards):
    n_row = len(row_shards)

    def body(*refs):
        win_ref = refs[0]
        row_refs = refs[1:1 + n_row]
        out_win = refs[1 + n_row]
        row_outs = refs[2 + n_row:2 + 2 * n_row]
        win_b, send_sems, recv_sems, local_sem = refs[2 + 2 * n_row:]
        win_b[...] = win_ref[...].astype(BF16)
        x, y, c = lax.axis_index("x"), lax.axis_index("y"), lax.axis_index("c")
        me, sibling = (x, y, c), (x, y, 1 - c)
        chips = [(1 - x, y), (x, 1 - y), (1 - x, 1 - y)]
        slot = lambda p: out_win.at[4 * p[0] + 2 * p[1] + p[2]]

        def copy(k, block, to, src=None):
            return pltpu.make_async_remote_copy(
                src_ref=slot(block) if src is None else src, dst_ref=slot(block), send_sem=send_sems.at[k],
                recv_sem=recv_sems.at[k], device_id=to, device_id_type=MESH)

        mine = pltpu.make_async_copy(win_b, slot(me), local_sem)
        mine.start()
        first = [copy(0, me, sibling, src=win_b)]
        first += [copy(1 + j, me, (*chip, c), src=win_b) for j, chip in enumerate(chips)]
        for cp in first:
            cp.start()
        for r, o in zip(row_refs, row_outs):
            o[...] = r[...].astype(BF16)
        passed = [copy(4 + j, (*chip, c), sibling) for j, chip in enumerate(chips)]
        for j, chip in enumerate(chips):
            copy(1 + j, (*chip, c), me).wait_recv()
            passed[j].start()
        copy(0, sibling, me).wait_recv()
        for j, chip in enumerate(chips):
            copy(4 + j, (*chip, 1 - c), me).wait_recv()
        for cp in first + passed:
            cp.wait_send()
        mine.wait()

    vm = pl.BlockSpec(memory_space=pltpu.VMEM)
    return pl.pallas_call(
        body, name="allgather_w_in", in_specs=[vm] * (1 + n_row),
        out_specs=(pl.BlockSpec(memory_space=pl.ANY), *([vm] * n_row)),
        out_shape=(jax.ShapeDtypeStruct((N_DEV,) + w_in_shard.shape, BF16),
                   *[jax.ShapeDtypeStruct(r.shape, BF16) for r in row_shards]),
        scratch_shapes=[pltpu.VMEM(w_in_shard.shape, BF16), pltpu.SemaphoreType.DMA((N_DEV - 1,)),
                        pltpu.SemaphoreType.DMA((N_DEV - 1,)), pltpu.SemaphoreType.DMA],
        compiler_params=pltpu.CompilerParams(vmem_limit_bytes=VMEM_LIMIT),
    )(w_in_shard, *row_shards)


def _allreduce_small(small):
    rows = SMALL_ROWS // N_DEV

    def body(x_ref, out_ref, land, send1, recv1, send2, recv2):
        me, peers = _me_and_peers()
        block = lambda j: pl.ds(pl.multiple_of(j * rows, 8), rows)

        def phase(src_of, dst_of, send_sems, recv_sems):
            sends = [pltpu.make_async_remote_copy(src_ref=src_of(pidx), dst_ref=dst_of(me), send_sem=send_sems.at[k],
                                                  recv_sem=recv_sems.at[k], device_id=peer, device_id_type=MESH)
                     for k, (peer, pidx) in enumerate(peers)]
            recvs = [pltpu.make_async_remote_copy(src_ref=src_of(me), dst_ref=dst_of(pidx), send_sem=send_sems.at[k],
                                                  recv_sem=recv_sems.at[k], device_id=peer, device_id_type=MESH)
                     for k, (peer, pidx) in enumerate(peers)]
            for cp in sends:
                cp.start()
            for cp in recvs:
                cp.wait_recv()
            for cp in sends:
                cp.wait_send()

        land[me] = x_ref[block(me), :]
        phase(lambda j: x_ref.at[block(j), :], lambda j: land.at[j], send1, recv1)
        total = land[0]
        for j in range(1, N_DEV):
            total = total + land[j]
        out_ref[block(me), :] = total
        phase(lambda j: out_ref.at[block(me), :], lambda j: out_ref.at[block(j), :], send2, recv2)

    vm = pl.BlockSpec(memory_space=pltpu.VMEM)
    return pl.pallas_call(
        body, name="allreduce_small", in_specs=[vm], out_specs=vm, out_shape=jax.ShapeDtypeStruct(small.shape, F32),
        scratch_shapes=[pltpu.VMEM((N_DEV, rows, D_MODEL), F32)] + [pltpu.SemaphoreType.DMA((N_DEV - 1,))] * 4,
    )(small)


def _adamw(name, got, w, m, v):
    r, c = w.shape
    n_slots = got.shape[0]
    tr = _pick(r, (256, 128, 64))

    def body(got_ref, w_ref, m_ref, v_ref, g_ref, d_ref, nm_ref, nv_ref):
        g = got_ref[0].astype(F32)
        for j in range(1, n_slots):
            g = g + got_ref[j].astype(F32)
        nm = ADAM_B1 * m_ref[...] + (1.0 - ADAM_B1) * g
        nv = ADAM_B2 * v_ref[...] + (1.0 - ADAM_B2) * jnp.square(g)
        m_hat = nm / (1.0 - ADAM_B1 ** ADAM_STEP)
        v_hat = nv / (1.0 - ADAM_B2 ** ADAM_STEP)
        g_ref[...] = g
        d_ref[...] = -ADAM_LR * (m_hat / (jnp.sqrt(v_hat) + ADAM_EPS) + ADAM_WD * w_ref[...])
        nm_ref[...] = nm
        nv_ref[...] = nv

    blk = pl.BlockSpec((tr, c), lambda i: (i, 0))
    out = jax.ShapeDtypeStruct((r, c), F32)
    return pl.pallas_call(
        body, name=name, grid=(r // tr,),
        in_specs=[pl.BlockSpec((n_slots, tr, c), lambda i: (0, i, 0)), blk, blk, blk],
        out_specs=(blk, blk, blk, blk), out_shape=(out, out, out, out), compiler_params=_cp("parallel"),
    )(got, w, m, v)


_SMALL_VECS = ("norm1_g", "ret_gn_g", "s5_d", "s5_glu_b", "norm2_g", "norm_mem_g", "norm_f_g")
_SMALL_ORDER = _SMALL_VECS + ("s5_a_re", "s5_a_im", "s5_log_dt", "s5_b_re", "s5_b_im", "s5_c_re", "s5_c_im")


def _pack_small(t, extra_row=None):
    rows = [t[n].reshape(1, D_MODEL) for n in _SMALL_VECS]
    rows.append(jnp.zeros((1, D_MODEL), F32) if extra_row is None else extra_row)
    rows += [t["s5_a_re"].reshape(4, D_MODEL), t["s5_a_im"].reshape(4, D_MODEL)]
    rows.append(jnp.pad(t["s5_log_dt"].reshape(1, S5_G), ((0, 7), (0, D_MODEL - S5_G))))
    rows += [t[n].reshape(64, D_MODEL) for n in ("s5_b_re", "s5_b_im", "s5_c_re", "s5_c_im")]
    rows.append(jnp.zeros((SMALL_ROWS - 280, D_MODEL), F32))
    return jnp.concatenate(rows, axis=0)


def _unpack_small(p, shapes):
    out = {n: p[i].reshape(shapes[n]) for i, n in enumerate(_SMALL_VECS)}
    out["s5_a_re"] = p[8:12].reshape(shapes["s5_a_re"])
    out["s5_a_im"] = p[12:16].reshape(shapes["s5_a_im"])
    out["s5_log_dt"] = p[16, :S5_G].reshape(shapes["s5_log_dt"])
    for i, n in enumerate(("s5_b_re", "s5_b_im", "s5_c_re", "s5_c_im")):
        out[n] = p[24 + 64 * i:24 + 64 * (i + 1)].reshape(shapes[n])
    return out


_W_NAMES = ("norm1_g", "w_in", "ret_gn_g", "s5_a_re", "s5_a_im", "s5_log_dt", "s5_b_re", "s5_b_im", "s5_c_re", "s5_c_im",
            "s5_d", "s5_glu_w", "s5_glu_b", "w_out", "norm2_g", "norm_mem_g", "xa_wq", "xa_wk", "xa_wv", "xa_wo",
            "norm_f_g")
_ROW_NAMES = ("s5_glu_w", "w_out", "xa_wq", "xa_wk", "xa_wv", "xa_wo")


def kernel(x, mem, positions, norm1_g, w_in, ret_gn_g, s5_a_re, s5_a_im, s5_log_dt, s5_b_re, s5_b_im, s5_c_re, s5_c_im, s5_d, s5_glu_w, s5_glu_b, w_out, norm2_g, norm_mem_g, xa_wq, xa_wk, xa_wv, xa_wo, norm_f_g, loss_target, m_norm1_g, m_w_in, m_ret_gn_g, m_s5_a_re, m_s5_a_im, m_s5_log_dt, m_s5_b_re, m_s5_b_im, m_s5_c_re, m_s5_c_im, m_s5_d, m_s5_glu_w, m_s5_glu_b, m_w_out, m_norm2_g, m_norm_mem_g, m_xa_wq, m_xa_wk, m_xa_wv, m_xa_wo, m_norm_f_g, v_norm1_g, v_w_in, v_ret_gn_g, v_s5_a_re, v_s5_a_im, v_s5_log_dt, v_s5_b_re, v_s5_b_im, v_s5_c_re, v_s5_c_im, v_s5_d, v_s5_glu_w, v_s5_glu_b, v_w_out, v_norm2_g, v_norm_mem_g, v_xa_wq, v_xa_wk, v_xa_wv, v_xa_wo, v_norm_f_g):
    w = dict(norm1_g=norm1_g, w_in=w_in, ret_gn_g=ret_gn_g, s5_a_re=s5_a_re, s5_a_im=s5_a_im, s5_log_dt=s5_log_dt,
             s5_b_re=s5_b_re, s5_b_im=s5_b_im, s5_c_re=s5_c_re, s5_c_im=s5_c_im, s5_d=s5_d, s5_glu_w=s5_glu_w,
             s5_glu_b=s5_glu_b, w_out=w_out, norm2_g=norm2_g, norm_mem_g=norm_mem_g, xa_wq=xa_wq, xa_wk=xa_wk,
             xa_wv=xa_wv, xa_wo=xa_wo, norm_f_g=norm_f_g)
    mom = dict(norm1_g=m_norm1_g, w_in=m_w_in, ret_gn_g=m_ret_gn_g, s5_a_re=m_s5_a_re, s5_a_im=m_s5_a_im,
               s5_log_dt=m_s5_log_dt, s5_b_re=m_s5_b_re, s5_b_im=m_s5_b_im, s5_c_re=m_s5_c_re, s5_c_im=m_s5_c_im,
               s5_d=m_s5_d, s5_glu_w=m_s5_glu_w, s5_glu_b=m_s5_glu_b, w_out=m_w_out, norm2_g=m_norm2_g,
               norm_mem_g=m_norm_mem_g, xa_wq=m_xa_wq, xa_wk=m_xa_wk, xa_wv=m_xa_wv, xa_wo=m_xa_wo,
               norm_f_g=m_norm_f_g)
    var = dict(norm1_g=v_norm1_g, w_in=v_w_in, ret_gn_g=v_ret_gn_g, s5_a_re=v_s5_a_re, s5_a_im=v_s5_a_im,
               s5_log_dt=v_s5_log_dt, s5_b_re=v_s5_b_re, s5_b_im=v_s5_b_im, s5_c_re=v_s5_c_re, s5_c_im=v_s5_c_im,
               s5_d=v_s5_d, s5_glu_w=v_s5_glu_w, s5_glu_b=v_s5_glu_b, w_out=v_w_out, norm2_g=v_norm2_g,
               norm_mem_g=v_norm_mem_g, xa_wq=v_xa_wq, xa_wk=v_xa_wk, xa_wv=v_xa_wv, xa_wo=v_xa_wo,
               norm_f_g=v_norm_f_g)
    shapes = {n: w[n].shape for n in _W_NAMES}

    x2d, mem2d, tgt = x[0], mem[0], loss_target[0]
    l = x2d.shape[0]
    ret_c = _pick(l, (256, 128))
    s5_t = _pick(l, (256, 128))
    g1, g2, gm, gf = norm1_g, norm2_g, norm_mem_g, norm_f_g.reshape(1, D_MODEL)

    win_s, *row_shards_b = _allgather_w_in(w_in[0], [w[n][0] for n in _ROW_NAMES])

    to_gpn = lambda b: jnp.transpose(b, (0, 2, 1)).reshape(S5_G, S5_P * S5_N)
    from_gpn = lambda b: jnp.transpose(b.reshape(S5_G, S5_P, S5_N), (0, 2, 1))
    disc_args = (s5_a_re[0], s5_a_im[0], s5_log_dt[0].reshape(S5_G, 1), to_gpn(s5_b_re[0]), to_gpn(s5_b_im[0]))
    abar_re, abar_im, bb_re_t, bb_im_t = _s5_discretize(*disc_args)
    bbm, ccm = _s5_block_mats(from_gpn(bb_re_t), from_gpn(bb_im_t), s5_c_re[0], s5_c_im[0])
    a_z = _s5_z(abar_re, abar_im)

    h1 = _rms_fwd("norm1_fwd", x2d, g1)
    proj, *rows_01 = _mm_nn_slots("in_proj", h1, win_s, BF16, side=_gather_job(row_shards_b[:2]))
    full = {n: g.reshape(N_DEV * r, D_MODEL) for n, g, r in zip(_ROW_NAMES[:2], rows_01, ROW_SHARDS[:2])}
    half = RET_DK // 2
    inv = ROPE_BASE ** (-jnp.arange(half, dtype=F32) / half)
    cos_t, sin_t = _rope_tables(positions[0].reshape(l, 1), jnp.tile(inv, 128 // half)[None, :])
    rconsts = _ret_constants(ret_c)
    ret, o_saved, r_prev = _ret_fwd(proj, cos_t, sin_t, rconsts, ret_gn_g, ret_c)
    ssm, xstart, *rows_xa = _s5_fwd(proj, bbm, ccm, s5_d, full["s5_glu_w"], s5_glu_b, a_z, s5_t,
                                    side=_gather_job(row_shards_b[2:]))
    full.update({n: g.reshape(N_DEV * r, D_MODEL) for n, g, r in zip(_ROW_NAMES[2:], rows_xa, ROW_SHARDS[2:])})
    mix = jnp.concatenate([ret, ssm], axis=1)
    x1 = _mm_nn("out_proj", mix, full["w_out"], F32, residual=x2d)
    h2 = _rms_fwd("norm2_fwd", x1, g2)
    mn = _rms_fwd("norm_mem_fwd", mem2d, gm)
    qa = _mm_nn("xa_q", h2, full["xa_wq"], BF16)
    ka = _mm_nn("xa_k", mn, full["xa_wk"], BF16)
    va = _mm_nn("xa_v", mn, full["xa_wv"], BF16)
    oa = _attn_fwd(qa, ka, va)
    x2 = _mm_nn("xa_o", oa, full["xa_wo"], F32, residual=x1)
    dx2, dgf, loss_lanes = _loss_head(x2, gf, tgt)

    doa = _mm_nt("xa_o_dx", dx2, full["xa_wo"], BF16)
    dwo = _mm_tn("xa_o_dw", oa, dx2, BF16)
    dqa, dka, dva = _attn_bwd(qa, ka, va, doa)
    dh2 = _mm_nt("xa_q_dx", dqa, full["xa_wq"], F32)
    dwq = _mm_tn("xa_q_dw", h2, dqa, BF16)
    dx1, dg2 = _rms_bwd("norm2_bwd", x1, g2, dh2, dx2)
    dwk = _mm_tn("xa_k_dw", mn, dka, BF16)
    dwv = _mm_tn("xa_v_dw", mn, dva, BF16)
    dmn = _mm_nt("xa_v_dx", dva, full["xa_wv"], F32, residual=_mm_nt("xa_k_dx", dka, full["xa_wk"], F32))
    _, dgm = _rms_bwd("norm_mem_bwd", mem2d, gm, dmn, None)
    dmix = _mm_nt("out_proj_dx", dx1, full["w_out"], BF16)
    dwout = _mm_tn("out_proj_dw", mix, dx1, BF16)
    dret, dgn, *got_a = _ret_bwd(proj, cos_t, sin_t, rconsts, ret_gn_g, o_saved, r_prev, dmix, ret_c,
                                 side=_scatter_job([dwout, dwq, dwk, dwv, dwo]))
    dug, y2, dz, dbbm, dccm, dabar, dd, dgb = _s5_bwd(proj, dmix, xstart, bbm, ccm, s5_d, full["s5_glu_w"], s5_glu_b,
                                                      a_z, s5_t)
    dglu = _mm_tn("s5_glu_dw", y2, dz, BF16)
    dproj = jnp.concatenate([dret, dug], axis=1)
    dwin_s, got_glu = _mm_tn_slots("in_proj_dw", h1, dproj, N_DEV, BF16, side=_scatter_job([dglu]))
    dh1, got_win = _mm_nt_slots("in_proj_dx", dproj, win_s, F32, side=_scatter_job([dwin_s]))
    grad_x, dg1 = _rms_bwd("norm1_bwd", x2d, g1, dh1, dx1)

    dab_re, dab_im = _s5_unz(dabar)
    dbb_re, dbb_im = _s5_block_diag_bb(dbbm)
    dc_re, dc_im = _s5_block_diag_cc(dccm)
    da_re, da_im, dlog_dt, db_re_t, db_im_t = _s5_discretize_bwd(*disc_args, dab_re, dab_im, to_gpn(dbb_re),
                                                                 to_gpn(dbb_im))
    db_re, db_im = from_gpn(db_re_t), from_gpn(db_im_t)
    small_g = dict(norm1_g=dg1, ret_gn_g=dgn, s5_d=dd, s5_glu_b=dgb, norm2_g=dg2, norm_mem_g=dgm, norm_f_g=dgf,
                   s5_a_re=da_re, s5_a_im=da_im, s5_log_dt=dlog_dt, s5_b_re=db_re, s5_b_im=db_im, s5_c_re=dc_re,
                   s5_c_im=dc_im)
    small_pack = _pack_small(small_g, extra_row=loss_lanes)

    res = {}
    got = dict(zip(("w_out", "xa_wq", "xa_wk", "xa_wv", "xa_wo"), got_a), w_in=got_win, s5_glu_w=got_glu)
    for n in ("w_in",) + _ROW_NAMES:
        res[n] = _adamw("adamw_" + n, got[n], w[n][0], mom[n][0], var[n][0])
    small_sum = _allreduce_small(small_pack)
    small_out = _adamw("adamw_small", small_sum[None], _pack_small(w), _pack_small(mom), _pack_small(var))
    unpacked = [_unpack_small(a, shapes) for a in small_out]
    loss = (0.5 / D_MODEL) * jnp.sum(small_out[0][7])
    for n in _SMALL_ORDER:
        res[n] = tuple(u[n] for u in unpacked)

    outs = [loss, grad_x[None]]
    for part in range(4):
        for n in _W_NAMES:
            outs.append(res[n][part].reshape(shapes[n]))
    return tuple(outs)
```

```python
import functools

import jax
import jax.numpy as jnp
from jax import lax
from jax.experimental import pallas as pl
from jax.experimental.pallas import tpu as pltpu

F32 = jnp.float32
BF16 = jnp.bfloat16
MESH = pl.DeviceIdType.MESH

D_MODEL = 1024
RET_HEADS, RET_DK, RET_DV = 8, 64, 128
RET_QK = RET_HEADS * RET_DK
S5_G, S5_N, S5_P = 64, 64, 16
S5_NB = 8
S5_GB = S5_G // S5_NB
S5_BS = S5_GB * S5_N
S5_COLS = 2 * S5_G * S5_N
XA_HEADS, XA_DH = 4, 256
EPS = 1e-6
ROPE_BASE = 10000.0
N_DEV = 8
W_IN_SHARD = 640
ROW_SHARDS = (128, 256, 128, 128, 128, 128)
ROWPACK = sum(ROW_SHARDS)
SMALL_ROWS = 320
ADAM_LR, ADAM_B1, ADAM_B2, ADAM_EPS, ADAM_WD, ADAM_STEP = 0.001, 0.9, 0.999, 1e-08, 0.01, 10

VMEM_LIMIT = 56 * 1024 * 1024


def _cp(*sem):
    return pltpu.CompilerParams(dimension_semantics=tuple(sem), vmem_limit_bytes=VMEM_LIMIT)


def _dot(a, b):
    return jnp.dot(a, b, preferred_element_type=F32)


def _dot_nt(a, b):
    return lax.dot_general(a, b, (((1,), (1,)), ((), ())), preferred_element_type=F32)


def _dot_tn(a, b):
    return lax.dot_general(a, b, (((0,), (0,)), ((), ())), preferred_element_type=F32)


def _sigmoid(x):
    return 1.0 / (1.0 + jnp.exp(-x))


def _silu(x):
    return x * _sigmoid(x)


def _dsilu(x):
    s = _sigmoid(x)
    return s * (1.0 + x * (1.0 - s))


_GELU_C = 0.7978845608028654


def _gelu(x):
    return 0.5 * x * (1.0 + jnp.tanh(_GELU_C * (x + 0.044715 * (x * x * x))))


def _dgelu(x):
    t = jnp.tanh(_GELU_C * (x + 0.044715 * (x * x * x)))
    return 0.5 * (1.0 + t) + 0.5 * x * (1.0 - t * t) * (_GELU_C * (1.0 + 3.0 * 0.044715 * (x * x)))


def _pick(n, cands):
    for c in cands:
        if n % c == 0:
            return c
    return n


def _mm_core(name, operands, in_specs, out_spec, out_shape, grid, nk, dims, acc_shape, has_res, side=None):
    n_in = 3 if has_res else 2
    n_side_in = len(side.srcs) if side else 0
    n_side_out = side.n if side else 0

    def body(*refs):
        a_ref, b_ref = refs[0], refs[1]
        r_ref = refs[2] if has_res else None
        side_in = refs[n_in:n_in + n_side_in]
        o_ref = refs[n_in + n_side_in]
        side_out = refs[n_in + n_side_in + 1:n_in + n_side_in + 1 + n_side_out]
        rest = refs[n_in + n_side_in + 1 + n_side_out:]
        acc, sems = (rest[0], rest[1:]) if nk > 1 else (None, rest)
        i, j, k = pl.program_id(0), pl.program_id(1), pl.program_id(2)
        if side:
            @pl.when((i == 0) & (j == 0) & (k == 0))
            def _():
                side.start(side_in, side_out, sems)

        def product():
            return lax.dot_general(a_ref[...].astype(BF16), b_ref[...].astype(BF16), (dims, ((), ())),
                                   preferred_element_type=F32)

        if nk == 1:
            o_ref[...] = (product() + r_ref[...] if has_res else product()).astype(o_ref.dtype)
        else:
            @pl.when(k == 0)
            def _():
                acc[...] = jnp.zeros_like(acc)

            acc[...] += product()

            @pl.when(k == nk - 1)
            def _():
                r = acc[...]
                if has_res:
                    r = r + r_ref[...]
                o_ref[...] = r.astype(o_ref.dtype)

        if side:
            @pl.when((i == grid[0] - 1) & (j == grid[1] - 1) & (k == grid[2] - 1))
            def _():
                side.wait(side_in, side_out, sems)

    acc_scratch = [pltpu.VMEM(acc_shape, F32)] if nk > 1 else []
    if side:
        return pl.pallas_call(
            body, name=name, grid=grid, in_specs=list(in_specs) + side.in_specs,
            out_specs=(out_spec, *side.out_specs), out_shape=(out_shape, *side.landing),
            scratch_shapes=acc_scratch + side.scratch,
            compiler_params=_cp("arbitrary", "arbitrary", "arbitrary"),
        )(*operands, *side.srcs)
    return pl.pallas_call(
        body, name=name, grid=grid, in_specs=in_specs, out_specs=out_spec, out_shape=out_shape,
        scratch_shapes=acc_scratch,
        compiler_params=_cp("parallel", "parallel", "arbitrary"),
    )(*operands)


def _mm_nn(name, a, b, out_dtype, residual=None):
    m, kk = a.shape
    n = b.shape[1]
    tm, tn, tk = _pick(m, (1024, 512, 256)), _pick(n, (1024, 512)), _pick(kk, (1024, 512))
    ops = [a, b]
    specs = [pl.BlockSpec((tm, tk), lambda i, j, k: (i, k)), pl.BlockSpec((tk, tn), lambda i, j, k: (k, j))]
    if residual is not None:
        ops.append(residual)
        specs.append(pl.BlockSpec((tm, tn), lambda i, j, k: (i, j)))
    return _mm_core(name, ops, specs, pl.BlockSpec((tm, tn), lambda i, j, k: (i, j)),
                    jax.ShapeDtypeStruct((m, n), out_dtype), (m // tm, n // tn, kk // tk), kk // tk,
                    ((1,), (0,)), (tm, tn), residual is not None)


def _mm_nt(name, a, b, out_dtype, residual=None):
    m, kk = a.shape
    n = b.shape[0]
    tm, tn, tk = _pick(m, (1024, 512, 256)), _pick(n, (1024, 512)), _pick(kk, (1024, 512))
    ops = [a, b]
    specs = [pl.BlockSpec((tm, tk), lambda i, j, k: (i, k)), pl.BlockSpec((tn, tk), lambda i, j, k: (j, k))]
    if residual is not None:
        ops.append(residual)
        specs.append(pl.BlockSpec((tm, tn), lambda i, j, k: (i, j)))
    return _mm_core(name, ops, specs, pl.BlockSpec((tm, tn), lambda i, j, k: (i, j)),
                    jax.ShapeDtypeStruct((m, n), out_dtype), (m // tm, n // tn, kk // tk), kk // tk,
                    ((1,), (1,)), (tm, tn), residual is not None)


def _mm_tn(name, a, b, out_dtype):
    kk, m = a.shape
    n = b.shape[1]
    tm, tn, tk = _pick(m, (1024, 512)), _pick(n, (1024, 512)), _pick(kk, (1024, 512, 256))
    specs = [pl.BlockSpec((tk, tm), lambda i, j, k: (k, i)), pl.BlockSpec((tk, tn), lambda i, j, k: (k, j))]
    return _mm_core(name, [a, b], specs, pl.BlockSpec((tm, tn), lambda i, j, k: (i, j)),
                    jax.ShapeDtypeStruct((m, n), out_dtype), (m // tm, n // tn, kk // tk), kk // tk,
                    ((0,), (0,)), (tm, tn), False)


def _mm_nn_slots(name, a, b_slots, out_dtype, side=None):
    m, kk = a.shape
    s, _, ns = b_slots.shape
    tm, tk = _pick(m, (1024, 512, 256)), _pick(kk, (1024, 512))
    specs = [pl.BlockSpec((tm, tk), lambda i, j, k: (i, k)), pl.BlockSpec((None, tk, ns), lambda i, j, k: (j, k, 0))]
    return _mm_core(name, [a, b_slots], specs, pl.BlockSpec((tm, ns), lambda i, j, k: (i, j)),
                    jax.ShapeDtypeStruct((m, s * ns), out_dtype), (m // tm, s, kk // tk), kk // tk,
                    ((1,), (0,)), (tm, ns), False, side)


def _mm_nt_slots(name, a, b_slots, out_dtype, side=None):
    m = a.shape[0]
    s, n, ns = b_slots.shape
    tm, tn = _pick(m, (1024, 512, 256)), _pick(n, (1024, 512))
    specs = [pl.BlockSpec((tm, ns), lambda i, j, k: (i, k)), pl.BlockSpec((None, tn, ns), lambda i, j, k: (k, j, 0))]
    return _mm_core(name, [a, b_slots], specs, pl.BlockSpec((tm, tn), lambda i, j, k: (i, j)),
                    jax.ShapeDtypeStruct((m, n), out_dtype), (m // tm, n // tn, s), s,
                    ((1,), (1,)), (tm, tn), False, side)


def _mm_tn_slots(name, a, b, s, out_dtype, side=None):
    kk, m = a.shape
    ns = b.shape[1] // s
    tm, tk = _pick(m, (1024, 512)), _pick(kk, (1024, 512, 256))
    specs = [pl.BlockSpec((tk, tm), lambda i, j, k: (k, i)), pl.BlockSpec((tk, ns), lambda i, j, k: (k, j))]
    return _mm_core(name, [a, b], specs, pl.BlockSpec((None, tm, ns), lambda i, j, k: (j, i, 0)),
                    jax.ShapeDtypeStruct((s, m, ns), out_dtype), (m // tm, s, kk // tk), kk // tk,
                    ((0,), (0,)), (tm, ns), False, side)


def _rms_fwd(name, x, g):
    r, d = x.shape
    tr = _pick(r, (1024, 512, 256))

    def body(x_ref, g_ref, o_ref):
        xv = x_ref[...]
        rs = lax.rsqrt(jnp.mean(xv * xv, axis=-1, keepdims=True) + EPS)
        o_ref[...] = (xv * rs * g_ref[...]).astype(o_ref.dtype)

    return pl.pallas_call(
        body, name=name, grid=(r // tr,),
        in_specs=[pl.BlockSpec((tr, d), lambda i: (i, 0)), pl.BlockSpec((1, d), lambda i: (0, 0))],
        out_specs=pl.BlockSpec((tr, d), lambda i: (i, 0)),
        out_shape=jax.ShapeDtypeStruct((r, d), BF16), compiler_params=_cp("parallel"),
    )(x, g)


def _rms_bwd(name, x, g, dh, dres):
    r, d = x.shape
    tr = _pick(r, (512, 256))
    has_res = dres is not None

    def body(*refs):
        if has_res:
            x_ref, g_ref, dh_ref, dr_ref, dx_ref, dg_ref = refs
        else:
            x_ref, g_ref, dh_ref, dx_ref, dg_ref = refs
        i = pl.program_id(0)

        @pl.when(i == 0)
        def _():
            dg_ref[...] = jnp.zeros_like(dg_ref)

        xv = x_ref[...]
        dhv = dh_ref[...].astype(F32)
        rs = lax.rsqrt(jnp.mean(xv * xv, axis=-1, keepdims=True) + EPS)
        xn = xv * rs
        dg_ref[...] += jnp.sum(dhv * xn, axis=0, keepdims=True)
        dn = dhv * g_ref[...]
        dx = rs * (dn - xn * jnp.mean(dn * xn, axis=-1, keepdims=True))
        if has_res:
            dx = dx + dr_ref[...]
        dx_ref[...] = dx

    row = pl.BlockSpec((tr, d), lambda i: (i, 0))
    vec = pl.BlockSpec((1, d), lambda i: (0, 0))
    ops = [x, g, dh] + ([dres] if has_res else [])
    return pl.pallas_call(
        body, name=name, grid=(r // tr,),
        in_specs=[row, vec, row] + ([row] if has_res else []),
        out_specs=(row, vec),
        out_shape=(jax.ShapeDtypeStruct((r, d), F32), jax.ShapeDtypeStruct((1, d), F32)),
        compiler_params=_cp("arbitrary"),
    )(*ops)


def _loss_head(x2, gf, target):
    r, d = x2.shape
    tr = _pick(r, (512, 256))

    def body(x_ref, g_ref, t_ref, dx_ref, dg_ref, ls_ref):
        i = pl.program_id(0)

        @pl.when(i == 0)
        def _():
            dg_ref[...] = jnp.zeros_like(dg_ref)
            ls_ref[...] = jnp.zeros_like(ls_ref)

        xv = x_ref[...]
        rs = lax.rsqrt(jnp.mean(xv * xv, axis=-1, keepdims=True) + EPS)
        xn = xv * rs
        e = xn * g_ref[...] - t_ref[...]
        ls_ref[...] += jnp.sum(e * e, axis=0, keepdims=True)
        dy = e * (1.0 / d)
        dg_ref[...] += jnp.sum(dy * xn, axis=0, keepdims=True)
        dn = dy * g_ref[...]
        dx_ref[...] = rs * (dn - xn * jnp.mean(dn * xn, axis=-1, keepdims=True))

    row = pl.BlockSpec((tr, d), lambda i: (i, 0))
    vec = pl.BlockSpec((1, d), lambda i: (0, 0))
    return pl.pallas_call(
        body, name="loss_head", grid=(r // tr,), in_specs=[row, vec, row], out_specs=(row, vec, vec),
        out_shape=(jax.ShapeDtypeStruct((r, d), F32), jax.ShapeDtypeStruct((1, d), F32),
                   jax.ShapeDtypeStruct((1, d), F32)),
        compiler_params=_cp("arbitrary"),
    )(x2, gf, target)


def _rope_tables(pos_col, inv_row):
    l = pos_col.shape[0]
    tl = _pick(l, (1024, 512, 256))

    def body(p_ref, inv_ref, cos_ref, sin_ref):
        ang = p_ref[...].astype(F32) * inv_ref[...]
        lane = lax.broadcasted_iota(jnp.int32, ang.shape, 1)
        c = jnp.cos(ang)
        s = jnp.where((lane % RET_DK) < RET_DK // 2, -jnp.sin(ang), jnp.sin(ang))
        cos_ref[...] = jnp.tile(c, (1, RET_QK // 128))
        sin_ref[...] = jnp.tile(s, (1, RET_QK // 128))

    return pl.pallas_call(
        body, name="rope_tables", grid=(l // tl,),
        in_specs=[pl.BlockSpec((tl, 1), lambda i: (i, 0)), pl.BlockSpec((1, 128), lambda i: (0, 0))],
        out_specs=(pl.BlockSpec((tl, RET_QK), lambda i: (i, 0)), pl.BlockSpec((tl, RET_QK), lambda i: (i, 0))),
        out_shape=(jax.ShapeDtypeStruct((l, RET_QK), F32), jax.ShapeDtypeStruct((l, RET_QK), F32)),
        compiler_params=_cp("parallel"),
    )(pos_col, inv_row)


def _rot(x, cos_t, sin_t):
    n = x.shape[-1]
    lane = lax.broadcasted_iota(jnp.int32, x.shape, 1)
    partner = jnp.where((lane % RET_DK) < RET_DK // 2, pltpu.roll(x, n - RET_DK // 2, 1), pltpu.roll(x, RET_DK // 2, 1))
    return x * cos_t + partner * sin_t


def _ret_constants(c):
    log_g = jnp.log1p(-jnp.exp2(-5.0 - jnp.arange(RET_HEADS, dtype=F32)))
    j = jnp.arange(c, dtype=F32)
    diff = j[:, None] - j[None, :]
    decay = jnp.where(diff[None] >= 0.0, jnp.exp(log_g[:, None, None] * jnp.maximum(diff, 0.0)[None]), 0.0)
    q_w = jnp.exp(log_g[None, :] * (j + 1.0)[:, None])
    k_w = jnp.exp(log_g[None, :] * (c - 1.0 - j)[:, None])
    cd = jnp.exp(log_g * c)
    rep = lambda t: jnp.repeat(t, RET_DK, axis=1)
    cd_row = jnp.repeat(cd, RET_DV)[None, :]
    return decay, rep(q_w), rep(k_w), cd_row


def _pair_of(h, c):
    lane = lax.broadcasted_iota(jnp.int32, (c, 2 * RET_DK), 1)
    mine = (lane < RET_DK) if h % 2 == 0 else (lane >= RET_DK)
    return slice((h // 2) * 2 * RET_DK, (h // 2 + 1) * 2 * RET_DK), mine


def _keep(x, mine):
    return jnp.where(mine, x, jnp.zeros_like(x))


def _ret_fwd(proj, cos_t, sin_t, consts, gn_g, c):
    l = proj.shape[0]
    nc = l // c
    decay, qw, kw, cd_row = consts

    def body(q_ref, k_ref, v_ref, g_ref, cos_ref, sin_ref, dec_ref, qw_ref, kw_ref, cd_ref, gn_ref,
             ret_ref, o_ref, rp_ref, qb_ref, kb_ref, state):
        @pl.when(pl.program_id(0) == 0)
        def _():
            state[...] = jnp.zeros_like(state)

        cs, sn = cos_ref[...], sin_ref[...]
        qr = _rot(q_ref[...].astype(F32), cs, sn)
        kr = _rot(k_ref[...].astype(F32), cs, sn) * (RET_DK ** -0.5)
        qb, kb = qr.astype(BF16), kr.astype(BF16)
        qb_ref[...] = qb
        kb_ref[...] = kb
        qwb = (qr * qw_ref[...]).astype(BF16)
        kwb = (kr * kw_ref[...]).astype(BF16)
        vb = v_ref[...].astype(BF16)
        for h in range(RET_HEADS):
            ps, mine = _pair_of(h, c)
            vs = slice(h * RET_DV, (h + 1) * RET_DV)
            s = _dot_nt(_keep(qb[:, ps], mine), kb[:, ps]) * dec_ref[h]
            r_prev = state[h]
            rp_ref[0, h] = r_prev
            o = _dot(s.astype(BF16), vb[:, vs]) + _dot(_keep(qwb[:, ps], mine), r_prev.astype(BF16))
            state[h] = cd_ref[:, vs] * r_prev + _dot_tn(_keep(kwb[:, ps], mine), vb[:, vs])
            o_ref[:, vs] = o
            mu = jnp.mean(o, axis=-1, keepdims=True)
            var = jnp.mean(jnp.square(o - mu), axis=-1, keepdims=True)
            on = (o - mu) * lax.rsqrt(var + EPS)
            ret_ref[:, vs] = (on * gn_ref[:, vs] * _silu(g_ref[:, vs].astype(F32))).astype(ret_ref.dtype)

    const2 = lambda shape: pl.BlockSpec(shape, lambda i: (0,) * len(shape))
    return pl.pallas_call(
        body, name="retention_fwd", grid=(nc,),
        in_specs=[pl.BlockSpec((c, RET_QK), lambda i: (i, 0)), pl.BlockSpec((c, RET_QK), lambda i: (i, 1)),
                  pl.BlockSpec((c, D_MODEL), lambda i: (i, 1)), pl.BlockSpec((c, D_MODEL), lambda i: (i, 2)),
                  pl.BlockSpec((c, RET_QK), lambda i: (i, 0)), pl.BlockSpec((c, RET_QK), lambda i: (i, 0)),
                  const2((RET_HEADS, c, c)), const2((c, RET_QK)), const2((c, RET_QK)), const2((1, D_MODEL)),
                  const2((1, D_MODEL))],
        out_specs=(pl.BlockSpec((c, D_MODEL), lambda i: (i, 0)), pl.BlockSpec((c, D_MODEL), lambda i: (i, 0)),
                   pl.BlockSpec((1, RET_HEADS, 2 * RET_DK, RET_DV), lambda i: (i, 0, 0, 0)),
                   pl.BlockSpec((c, RET_QK), lambda i: (i, 0)), pl.BlockSpec((c, RET_QK), lambda i: (i, 0))),
        out_shape=(jax.ShapeDtypeStruct((l, 2 * D_MODEL), BF16), jax.ShapeDtypeStruct((l, D_MODEL), F32),
                   jax.ShapeDtypeStruct((nc, RET_HEADS, 2 * RET_DK, RET_DV), F32),
                   jax.ShapeDtypeStruct((l, RET_QK), BF16), jax.ShapeDtypeStruct((l, RET_QK), BF16)),
        scratch_shapes=[pltpu.VMEM((RET_HEADS, 2 * RET_DK, RET_DV), F32)],
        compiler_params=_cp("arbitrary"),
    )(proj, proj, proj, proj, cos_t, sin_t, decay, qw, kw, cd_row, gn_g)


def _ret_bwd(proj, qb_saved, kb_saved, cos_t, sin_t, consts, gn_g, o_saved, r_prev_saved, dmix, c, side):
    l = proj.shape[0]
    nc = l // c
    decay, qw, kw, cd_row = consts
    n_in = 14

    def body(*refs):
        (q_ref, k_ref, v_ref, g_ref, cos_ref, sin_ref, dec_ref, qw_ref, kw_ref, cd_ref, gn_ref, o_ref, rp_ref,
         dr_ref) = refs[:n_in]
        side_in = refs[n_in:n_in + len(side.srcs)]
        out_ref, dgn_ref = refs[n_in + len(side.srcs):n_in + len(side.srcs) + 2]
        side_out = refs[n_in + len(side.srcs) + 2:n_in + len(side.srcs) + 2 + side.n]
        state, dq_s, dk_s = refs[n_in + len(side.srcs) + 2 + side.n:n_in + len(side.srcs) + 5 + side.n]
        sems = refs[n_in + len(side.srcs) + 5 + side.n:]

        @pl.when(pl.program_id(0) == 0)
        def _():
            side.start(side_in, side_out, sems)
            state[...] = jnp.zeros_like(state)
            dgn_ref[...] = jnp.zeros_like(dgn_ref)

        cs, sn = cos_ref[...], sin_ref[...]
        qb, kb = q_ref[...], k_ref[...]
        qwv, kwv = qw_ref[...], kw_ref[...]
        qwb = (qb.astype(F32) * qwv).astype(BF16)
        kwb = (kb.astype(F32) * kwv).astype(BF16)
        vb = v_ref[...].astype(BF16)
        dq2 = dk2 = None
        for h in range(RET_HEADS):
            ps, mine = _pair_of(h, c)
            vs = slice(h * RET_DV, (h + 1) * RET_DV)
            dec = dec_ref[h]
            qm, km = _keep(qb[:, ps], mine), _keep(kb[:, ps], mine)
            o = o_ref[:, vs]
            mu = jnp.mean(o, axis=-1, keepdims=True)
            var = jnp.mean(jnp.square(o - mu), axis=-1, keepdims=True)
            rstd = lax.rsqrt(var + EPS)
            on = (o - mu) * rstd
            gate = g_ref[:, vs].astype(F32)
            sg = _silu(gate)
            dret = dr_ref[:, vs].astype(F32)
            gn = gn_ref[:, vs]
            dgn_ref[:, vs] += jnp.sum(dret * on * sg, axis=0, keepdims=True)
            out_ref[:, 2 * RET_QK + D_MODEL + h * RET_DV:2 * RET_QK + D_MODEL + (h + 1) * RET_DV] = (
                dret * on * gn * _dsilu(gate)).astype(out_ref.dtype)
            don = dret * gn * sg
            do = rstd * (don - jnp.mean(don, axis=-1, keepdims=True)
                         - on * jnp.mean(don * on, axis=-1, keepdims=True))
            dob = do.astype(BF16)
            sn_h = state[h]
            snb = sn_h.astype(BF16)
            s = _dot_nt(qm, kb[:, ps]) * dec
            dv = _dot_tn(s.astype(BF16), dob) + _dot(_keep(kwb[:, ps], mine), snb)
            out_ref[:, 2 * RET_QK + h * RET_DV:2 * RET_QK + (h + 1) * RET_DV] = dv.astype(out_ref.dtype)
            ds = (_dot_nt(dob, vb[:, vs]) * dec).astype(BF16)
            dq_h = _dot(ds, km) + qwv[:, ps] * _dot_nt(dob, rp_ref[0, h].astype(BF16))
            dk_h = _dot_tn(ds, qm) + kwv[:, ps] * _dot_nt(vb[:, vs], snb)
            state[h] = cd_ref[:, vs] * sn_h + _dot_tn(_keep(qwb[:, ps], mine), dob)
            if h % 2 == 0:
                dq2, dk2 = dq_h, dk_h
            else:
                dq_s[:, ps] = dq2 + dq_h
                dk_s[:, ps] = dk2 + dk_h
        out_ref[:, 0:RET_QK] = _rot(dq_s[...], cs, -sn).astype(out_ref.dtype)
        out_ref[:, RET_QK:2 * RET_QK] = (_rot(dk_s[...], cs, -sn) * (RET_DK ** -0.5)).astype(out_ref.dtype)

        @pl.when(pl.program_id(0) == nc - 1)
        def _():
            side.wait(side_in, side_out, sems)

    rev = lambda i: nc - 1 - i
    const2 = lambda shape: pl.BlockSpec(shape, lambda i: (0,) * len(shape))
    return pl.pallas_call(
        body, name="retention_bwd", grid=(nc,),
        in_specs=[pl.BlockSpec((c, RET_QK), lambda i: (rev(i), 0)), pl.BlockSpec((c, RET_QK), lambda i: (rev(i), 0)),
                  pl.BlockSpec((c, D_MODEL), lambda i: (rev(i), 1)), pl.BlockSpec((c, D_MODEL), lambda i: (rev(i), 2)),
                  pl.BlockSpec((c, RET_QK), lambda i: (rev(i), 0)), pl.BlockSpec((c, RET_QK), lambda i: (rev(i), 0)),
                  const2((RET_HEADS, c, c)), const2((c, RET_QK)), const2((c, RET_QK)), const2((1, D_MODEL)),
                  const2((1, D_MODEL)),
                  pl.BlockSpec((c, D_MODEL), lambda i: (rev(i), 0)),
                  pl.BlockSpec((1, RET_HEADS, 2 * RET_DK, RET_DV), lambda i: (rev(i), 0, 0, 0)),
                  pl.BlockSpec((c, D_MODEL), lambda i: (rev(i), 0))] + side.in_specs,
        out_specs=(pl.BlockSpec((c, 2 * RET_QK + 2 * D_MODEL), lambda i: (rev(i), 0)), const2((1, D_MODEL)),
                   *side.out_specs),
        out_shape=(jax.ShapeDtypeStruct((l, 2 * RET_QK + 2 * D_MODEL), BF16), jax.ShapeDtypeStruct((1, D_MODEL), F32),
                   *side.landing),
        scratch_shapes=[pltpu.VMEM((RET_HEADS, 2 * RET_DK, RET_DV), F32), pltpu.VMEM((c, RET_QK), F32),
                        pltpu.VMEM((c, RET_QK), F32)] + side.scratch,
        compiler_params=_cp("arbitrary"),
    )(qb_saved, kb_saved, proj, proj, cos_t, sin_t, decay, qw, kw, cd_row, gn_g, o_saved, r_prev_saved, dmix,
      *side.srcs)


def _zoh(a_re, a_im, log_dt):
    dt = jnp.exp(log_dt)
    mag = jnp.exp(a_re * dt)
    abar_re = mag * jnp.cos(a_im * dt)
    abar_im = mag * jnp.sin(a_im * dt)
    den = a_re * a_re + a_im * a_im
    nr, ni = abar_re - 1.0, abar_im
    f_re = (nr * a_re + ni * a_im) / den
    f_im = (ni * a_re - nr * a_im) / den
    return dt, abar_re, abar_im, f_re, f_im, den


def _lanes_p(f):
    return jnp.tile(f, (1, S5_P))


def _s5_discretize(a_re, a_im, log_dt, b_re_t, b_im_t):
    def body(ar_ref, ai_ref, ld_ref, br_ref, bi_ref, abr_ref, abi_ref, bbr_ref, bbi_ref):
        _, abar_re, abar_im, f_re, f_im, _ = _zoh(ar_ref[...], ai_ref[...], ld_ref[...])
        abr_ref[...] = abar_re
        abi_ref[...] = abar_im
        fr, fi = _lanes_p(f_re), _lanes_p(f_im)
        bbr_ref[...] = fr * br_ref[...] - fi * bi_ref[...]
        bbi_ref[...] = fr * bi_ref[...] + fi * br_ref[...]

    gn = jax.ShapeDtypeStruct((S5_G, S5_N), F32)
    gpn = jax.ShapeDtypeStruct((S5_G, S5_P * S5_N), F32)
    return pl.pallas_call(body, name="s5_discretize", out_shape=(gn, gn, gpn, gpn))(a_re, a_im, log_dt, b_re_t, b_im_t)


def _s5_discretize_bwd(a_re, a_im, log_dt, b_re_t, b_im_t, dab_re, dab_im, dbb_re_t, dbb_im_t):
    def body(ar_ref, ai_ref, ld_ref, br_ref, bi_ref, gar_ref, gai_ref, gbr_ref, gbi_ref,
             dar_ref, dai_ref, dld_ref, dbr_ref, dbi_ref):
        a_r, a_i = ar_ref[...], ai_ref[...]
        dt, abar_re, abar_im, f_re, f_im, den = _zoh(a_r, a_i, ld_ref[...])
        b_r, b_i, g_br, g_bi = br_ref[...], bi_ref[...], gbr_ref[...], gbi_ref[...]
        fr, fi = _lanes_p(f_re), _lanes_p(f_im)
        dbr_ref[...] = fr * g_br + fi * g_bi
        dbi_ref[...] = fr * g_bi - fi * g_br
        t_r = b_r * g_br + b_i * g_bi
        t_i = b_r * g_bi - b_i * g_br
        gf_r = sum(t_r[:, p * S5_N:(p + 1) * S5_N] for p in range(S5_P))
        gf_i = sum(t_i[:, p * S5_N:(p + 1) * S5_N] for p in range(S5_P))
        inv_r, inv_i = a_r / den, a_i / den
        ga_r = gar_ref[...] + gf_r * inv_r - gf_i * inv_i
        ga_i = gai_ref[...] + gf_r * inv_i + gf_i * inv_r
        q_r = -(f_re * a_r + f_im * a_i) / den
        q_i = -(f_im * a_r - f_re * a_i) / den
        gl_r = q_r * gf_r + q_i * gf_i
        gl_i = q_r * gf_i - q_i * gf_r
        dar_ref[...] = gl_r + dt * (abar_re * ga_r + abar_im * ga_i)
        dai_ref[...] = gl_i + dt * (abar_re * ga_i - abar_im * ga_r)
        la_r = a_r * abar_re - a_i * abar_im
        la_i = a_r * abar_im + a_i * abar_re
        dld_ref[...] = dt * jnp.sum(ga_r * la_r + ga_i * la_i, axis=-1, keepdims=True)

    gn = jax.ShapeDtypeStruct((S5_G, S5_N), F32)
    gpn = jax.ShapeDtypeStruct((S5_G, S5_P * S5_N), F32)
    return pl.pallas_call(
        body, name="s5_discretize_bwd", out_shape=(gn, gn, jax.ShapeDtypeStruct((S5_G, 1), F32), gpn, gpn),
    )(a_re, a_im, log_dt, b_re_t, b_im_t, dab_re, dab_im, dbb_re_t, dbb_im_t)


S5_ZQ = S5_NB // 2


def _s5_z(re, im):
    return jnp.concatenate([re.reshape(S5_ZQ, 8, 128), im.reshape(S5_ZQ, 8, 128)], axis=0)


def _s5_unz(z):
    return z[:S5_ZQ].reshape(S5_G, S5_N), z[S5_ZQ:].reshape(S5_G, S5_N)


def _s5_block_mats(bb_re, bb_im, c_re, c_im):
    eye = jnp.eye(S5_GB, dtype=F32)
    bb = jnp.stack([bb_re, bb_im], axis=0).reshape(2, S5_NB, S5_GB, S5_N, S5_P)
    bbm = jnp.einsum("rbgnp,gh->bgprhn", bb, eye).reshape(S5_NB, S5_GB * S5_P, 2 * S5_BS)
    cc = jnp.stack([c_re, -c_im], axis=0).reshape(2, S5_NB, S5_GB, S5_P, S5_N)
    ccm = jnp.einsum("rbgpn,gh->brhngp", cc, eye).reshape(S5_NB, 2 * S5_BS, S5_GB * S5_P)
    return bbm.astype(BF16), ccm.astype(BF16)


def _s5_block_diag_bb(m):
    t = m.reshape(S5_NB, S5_GB, S5_P, 2, S5_GB, S5_N)
    d = jnp.einsum("bgprgn->rbgnp", t).reshape(2, S5_G, S5_N, S5_P)
    return d[0], d[1]


def _s5_block_diag_cc(m):
    t = m.reshape(S5_NB, 2, S5_GB, S5_N, S5_GB, S5_P)
    d = jnp.einsum("brgngp->rbgpn", t).reshape(2, S5_G, S5_P, S5_N)
    return d[0], -d[1]


SCAN_UNROLL = 8


def _z_store(zr, zi, blk, res, t, off):
    q, h = blk // 2, blk % 2
    for lt in range(4):
        zr[q, pl.ds(off + 4 * h + lt, t, stride=8), :] = res[:, lt * 128:(lt + 1) * 128]
        zi[q, pl.ds(off + 4 * h + lt, t, stride=8), :] = res[:, S5_BS + lt * 128:S5_BS + (lt + 1) * 128]


def _z_load(zr, zi, blk, t, off):
    q, h = blk // 2, blk % 2
    return jnp.concatenate([zr[q, pl.ds(off + 4 * h + lt, t, stride=8), :] for lt in range(4)]
                           + [zi[q, pl.ds(off + 4 * h + lt, t, stride=8), :] for lt in range(4)], axis=1)


def _z_scan_fwd(zr, zi, a_ref, carry_ref, t, off):
    ar = [a_ref[q] for q in range(S5_ZQ)]
    ai = [a_ref[S5_ZQ + q] for q in range(S5_ZQ)]

    def step(it, carry):
        carry = list(carry)
        base = pl.multiple_of(it * (8 * SCAN_UNROLL), 8 * SCAN_UNROLL) + off
        for tt in range(SCAN_UNROLL):
            rows = pl.ds(base + 8 * tt, 8)
            for q in range(S5_ZQ):
                c_r, c_i = carry[q], carry[S5_ZQ + q]
                n_r = ar[q] * c_r - ai[q] * c_i + zr[q, rows, :]
                n_i = ar[q] * c_i + ai[q] * c_r + zi[q, rows, :]
                zr[q, rows, :] = n_r
                zi[q, rows, :] = n_i
                carry[q], carry[S5_ZQ + q] = n_r, n_i
        return tuple(carry)

    out = lax.fori_loop(0, t // SCAN_UNROLL, step, tuple(carry_ref[k] for k in range(2 * S5_ZQ)))
    for k in range(2 * S5_ZQ):
        carry_ref[k] = out[k]


def _z_scan_bwd(lr, li, xr, xi, a_ref, carry_ref, acc_ref, t):
    ar = [a_ref[q] for q in range(S5_ZQ)]
    ai = [a_ref[S5_ZQ + q] for q in range(S5_ZQ)]
    n_it = t // SCAN_UNROLL

    def step(it, state):
        carry, acc = list(state[0]), list(state[1])
        base = pl.multiple_of((n_it - 1 - it) * (8 * SCAN_UNROLL), 8 * SCAN_UNROLL)
        for tt in reversed(range(SCAN_UNROLL)):
            rows = pl.ds(base + 8 * tt, 8)
            for q in range(S5_ZQ):
                c_r, c_i = carry[q], carry[S5_ZQ + q]
                n_r = ar[q] * c_r + ai[q] * c_i + lr[q, rows, :]
                n_i = ar[q] * c_i - ai[q] * c_r + li[q, rows, :]
                lr[q, rows, :] = n_r
                li[q, rows, :] = n_i
                p_r, p_i = xr[q, rows, :], xi[q, rows, :]
                acc[q] = acc[q] + n_r * p_r + n_i * p_i
                acc[S5_ZQ + q] = acc[S5_ZQ + q] + n_i * p_r - n_r * p_i
                carry[q], carry[S5_ZQ + q] = n_r, n_i
        return tuple(carry), tuple(acc)

    k8 = range(2 * S5_ZQ)
    carry, acc = lax.fori_loop(0, n_it, step, (tuple(carry_ref[k] for k in k8), tuple(acc_ref[k] for k in k8)))
    for k in k8:
        carry_ref[k] = carry[k]
        acc_ref[k] = acc[k]


def _s5_fwd(proj, mix, bbm, ccm, d_row, glu_w, glu_b, tabs, t, side):
    l = proj.shape[0]
    nt = l // t
    n_in = 9

    def body(*refs):
        u_ref, gs_ref, bb_ref, cc_ref, d_ref, gw_ref, gb_ref, a_ref, _ = refs[:n_in]
        side_in = refs[n_in:n_in + len(side.srcs)]
        ssm_ref, xst_ref = refs[n_in + len(side.srcs):n_in + len(side.srcs) + 2]
        side_out = refs[n_in + len(side.srcs) + 2:n_in + len(side.srcs) + 2 + side.n]
        zr, zi, carry = refs[n_in + len(side.srcs) + 2 + side.n:n_in + len(side.srcs) + 5 + side.n]
        sems = refs[n_in + len(side.srcs) + 5 + side.n:]

        @pl.when(pl.program_id(0) == 0)
        def _():
            side.start(side_in, side_out, sems)
            carry[...] = jnp.zeros_like(carry)

        xst_ref[0] = carry[...]
        ub = u_ref[...]
        u = ub.astype(F32)
        for blk in range(S5_NB):
            _z_store(zr, zi, blk, _dot(ub[:, blk * 128:(blk + 1) * 128], bb_ref[blk]), t, 0)
        _z_scan_fwd(zr, zi, a_ref, carry, t, 0)
        ys = jnp.concatenate(
            [_dot(_z_load(zr, zi, blk, t, 0).astype(BF16), cc_ref[blk]) for blk in range(S5_NB)], axis=1)
        y2 = _gelu(ys + d_ref[...] * u)
        z = _dot(y2.astype(BF16), gw_ref[...]) + gb_ref[...]
        ssm_ref[...] = (y2 * _sigmoid(z) * _silu(gs_ref[...].astype(F32))).astype(ssm_ref.dtype)

        @pl.when(pl.program_id(0) == nt - 1)
        def _():
            side.wait(side_in, side_out, sems)

    const2 = lambda shape: pl.BlockSpec(shape, lambda i: (0,) * len(shape))
    zshape = (2 * S5_ZQ, 8, 128)
    return pl.pallas_call(
        body, name="s5_fwd", grid=(nt,),
        in_specs=[pl.BlockSpec((t, D_MODEL), lambda i: (i, 3)), pl.BlockSpec((t, D_MODEL), lambda i: (i, 4)),
                  const2(bbm.shape), const2(ccm.shape), const2((1, D_MODEL)), const2((D_MODEL, D_MODEL)),
                  const2((1, D_MODEL)), const2(zshape), pl.BlockSpec(memory_space=pl.ANY)] + side.in_specs,
        out_specs=(pl.BlockSpec((t, D_MODEL), lambda i: (i, 1)), pl.BlockSpec((1,) + zshape, lambda i: (i, 0, 0, 0)),
                   *side.out_specs),
        out_shape=(jax.ShapeDtypeStruct((l, 2 * D_MODEL), BF16), jax.ShapeDtypeStruct((nt,) + zshape, F32),
                   *side.landing),
        scratch_shapes=[pltpu.VMEM((S5_ZQ, 8 * t, 128), F32), pltpu.VMEM((S5_ZQ, 8 * t, 128), F32),
                        pltpu.VMEM(zshape, F32)] + side.scratch,
        input_output_aliases={8: 0},
        compiler_params=_cp("arbitrary"),
    )(proj, proj, bbm, ccm, d_row, glu_w, glu_b, tabs, mix, *side.srcs)


def _s5_bwd(proj, dmix, xstart, bbm, ccm, d_row, glu_w, glu_b, tabs, t):
    l = proj.shape[0]
    nt = l // t

    def body(u_ref, gs_ref, dm_ref, xst_ref, bb_ref, cc_ref, d_ref, gw_ref, gb_ref, a_ref,
             dug_ref, y2_ref, dz_ref, dbb_ref, dcc_ref, da_ref, dd_ref, dgb_ref, xr, xi, lr, li, carry, lcarry):
        @pl.when(pl.program_id(0) == 0)
        def _():
            lcarry[...] = jnp.zeros_like(lcarry)
            dbb_ref[...] = jnp.zeros_like(dbb_ref)
            dcc_ref[...] = jnp.zeros_like(dcc_ref)
            da_ref[...] = jnp.zeros_like(da_ref)
            dd_ref[...] = jnp.zeros_like(dd_ref)
            dgb_ref[...] = jnp.zeros_like(dgb_ref)

        carry[...] = xst_ref[0]
        for q in range(S5_ZQ):
            xr[q, 0:8, :] = carry[q]
            xi[q, 0:8, :] = carry[S5_ZQ + q]
        ub = u_ref[...]
        u = ub.astype(F32)
        for blk in range(S5_NB):
            _z_store(xr, xi, blk, _dot(ub[:, blk * 128:(blk + 1) * 128], bb_ref[blk]), t, 8)
        _z_scan_fwd(xr, xi, a_ref, carry, t, 8)
        ys = jnp.concatenate(
            [_dot(_z_load(xr, xi, blk, t, 8).astype(BF16), cc_ref[blk]) for blk in range(S5_NB)], axis=1)
        dv = d_ref[...]
        y1 = ys + dv * u
        y2 = _gelu(y1)
        y2b = y2.astype(BF16)
        sg = _sigmoid(_dot(y2b, gw_ref[...]) + gb_ref[...])
        gs = gs_ref[...].astype(F32)
        dssm = dm_ref[...].astype(F32)
        dug_ref[:, D_MODEL:] = (dssm * (y2 * sg) * _dsilu(gs)).astype(dug_ref.dtype)
        dy3 = dssm * _silu(gs)
        dz = dy3 * y2 * sg * (1.0 - sg)
        dzb = dz.astype(BF16)
        y2_ref[...] = y2b
        dz_ref[...] = dzb
        dgb_ref[...] += jnp.sum(dz, axis=0, keepdims=True)
        dy1 = (dy3 * sg + _dot_nt(dzb, gw_ref[...])) * _dgelu(y1)
        dd_ref[...] += jnp.sum(dy1 * u, axis=0, keepdims=True)
        dyb = dy1.astype(BF16)
        for blk in range(S5_NB):
            ch = slice(blk * 128, (blk + 1) * 128)
            _z_store(lr, li, blk, _dot_nt(dyb[:, ch], cc_ref[blk]), t, 0)
            dcc_ref[blk] += _dot_tn(_z_load(xr, xi, blk, t, 8).astype(BF16), dyb[:, ch])
        _z_scan_bwd(lr, li, xr, xi, a_ref, lcarry, da_ref, t)
        du = []
        for blk in range(S5_NB):
            lb = _z_load(lr, li, blk, t, 0).astype(BF16)
            du.append(_dot_nt(lb, bb_ref[blk]))
            dbb_ref[blk] += _dot_tn(ub[:, blk * 128:(blk + 1) * 128], lb)
        dug_ref[:, :D_MODEL] = (jnp.concatenate(du, axis=1) + dy1 * dv).astype(dug_ref.dtype)

    rev = lambda i: nt - 1 - i
    const2 = lambda shape: pl.BlockSpec(shape, lambda i: (0,) * len(shape))
    row_out = lambda w: pl.BlockSpec((t, w), lambda i: (rev(i), 0))
    zshape = (2 * S5_ZQ, 8, 128)
    return pl.pallas_call(
        body, name="s5_bwd", grid=(nt,),
        in_specs=[pl.BlockSpec((t, D_MODEL), lambda i: (rev(i), 3)), pl.BlockSpec((t, D_MODEL), lambda i: (rev(i), 4)),
                  pl.BlockSpec((t, D_MODEL), lambda i: (rev(i), 1)),
                  pl.BlockSpec((1,) + zshape, lambda i: (rev(i), 0, 0, 0)),
                  const2(bbm.shape), const2(ccm.shape), const2((1, D_MODEL)), const2((D_MODEL, D_MODEL)),
                  const2((1, D_MODEL)), const2(zshape)],
        out_specs=(row_out(2 * D_MODEL), row_out(D_MODEL), row_out(D_MODEL), const2(bbm.shape), const2(ccm.shape),
                   const2(zshape), const2((1, D_MODEL)), const2((1, D_MODEL))),
        out_shape=(jax.ShapeDtypeStruct((l, 2 * D_MODEL), BF16), jax.ShapeDtypeStruct((l, D_MODEL), BF16),
                   jax.ShapeDtypeStruct((l, D_MODEL), BF16), jax.ShapeDtypeStruct(bbm.shape, F32),
                   jax.ShapeDtypeStruct(ccm.shape, F32), jax.ShapeDtypeStruct(zshape, F32),
                   jax.ShapeDtypeStruct((1, D_MODEL), F32), jax.ShapeDtypeStruct((1, D_MODEL), F32)),
        scratch_shapes=[pltpu.VMEM((S5_ZQ, 8 * t + 8, 128), F32), pltpu.VMEM((S5_ZQ, 8 * t + 8, 128), F32),
                        pltpu.VMEM((S5_ZQ, 8 * t, 128), F32), pltpu.VMEM((S5_ZQ, 8 * t, 128), F32),
                        pltpu.VMEM(zshape, F32), pltpu.VMEM(zshape, F32)],
        compiler_params=_cp("arbitrary"),
    )(proj, proj, dmix, xstart, bbm, ccm, d_row, glu_w, glu_b, tabs)


def _attn_probs(qh, kh):
    s = _dot_nt(qh, kh) * (XA_DH ** -0.5)
    e = jnp.exp(s - jnp.max(s, axis=-1, keepdims=True))
    return e / jnp.sum(e, axis=-1, keepdims=True)


def _attn_fwd(qa, ka, va):
    l = qa.shape[0]
    m = ka.shape[0]
    tl = _pick(l, (512, 256))

    def body(q_ref, k_ref, v_ref, o_ref):
        for h in range(XA_HEADS):
            hs = slice(h * XA_DH, (h + 1) * XA_DH)
            p = _attn_probs(q_ref[:, hs], k_ref[:, hs])
            o_ref[:, hs] = _dot(p.astype(BF16), v_ref[:, hs]).astype(o_ref.dtype)

    return pl.pallas_call(
        body, name="xattn_fwd", grid=(l // tl,),
        in_specs=[pl.BlockSpec((tl, D_MODEL), lambda i: (i, 0)), pl.BlockSpec((m, D_MODEL), lambda i: (0, 0)),
                  pl.BlockSpec((m, D_MODEL), lambda i: (0, 0))],
        out_specs=pl.BlockSpec((tl, D_MODEL), lambda i: (i, 0)),
        out_shape=jax.ShapeDtypeStruct((l, D_MODEL), BF16), compiler_params=_cp("parallel"),
    )(qa, ka, va)


def _attn_bwd(qa, ka, va, doa):
    l = qa.shape[0]
    m = ka.shape[0]
    tl = _pick(l, (512, 256))

    def body(q_ref, k_ref, v_ref, do_ref, dq_ref, dk_ref, dv_ref):
        @pl.when(pl.program_id(0) == 0)
        def _():
            dk_ref[...] = jnp.zeros_like(dk_ref)
            dv_ref[...] = jnp.zeros_like(dv_ref)

        for h in range(XA_HEADS):
            hs = slice(h * XA_DH, (h + 1) * XA_DH)
            qh, kh, vh, doh = q_ref[:, hs], k_ref[:, hs], v_ref[:, hs], do_ref[:, hs]
            p = _attn_probs(qh, kh)
            dv_ref[:, hs] += _dot_tn(p.astype(BF16), doh)
            dp = _dot_nt(doh, vh)
            ds = (p * (dp - jnp.sum(dp * p, axis=-1, keepdims=True)) * (XA_DH ** -0.5)).astype(BF16)
            dq_ref[:, hs] = _dot(ds, kh).astype(dq_ref.dtype)
            dk_ref[:, hs] += _dot_tn(ds, qh)

    row = pl.BlockSpec((tl, D_MODEL), lambda i: (i, 0))
    mem = pl.BlockSpec((m, D_MODEL), lambda i: (0, 0))
    return pl.pallas_call(
        body, name="xattn_bwd", grid=(l // tl,), in_specs=[row, mem, mem, row], out_specs=(row, mem, mem),
        out_shape=(jax.ShapeDtypeStruct((l, D_MODEL), BF16), jax.ShapeDtypeStruct((m, D_MODEL), F32),
                   jax.ShapeDtypeStruct((m, D_MODEL), F32)),
        compiler_params=_cp("arbitrary"),
    )(qa, ka, va, doa)


def _me_and_peers():
    x, y, c = lax.axis_index("x"), lax.axis_index("y"), lax.axis_index("c")
    flip = lambda v, bit: (1 - v) if bit else v
    peers = []
    for k in range(1, N_DEV):
        px, py, pc = flip(x, (k >> 2) & 1), flip(y, (k >> 1) & 1), flip(c, k & 1)
        peers.append(((px, py, pc), 4 * px + 2 * py + pc))
    return 4 * x + 2 * y + c, peers


class _SideJob:
    def __init__(self, srcs, landing, src_of, dst_of):
        self.srcs = list(srcs)
        self.landing = list(landing)
        self.n = len(self.landing)
        self.src_of, self.dst_of = src_of, dst_of
        hbm = pl.BlockSpec(memory_space=pl.ANY)
        self.in_specs = [hbm] * len(self.srcs)
        self.out_specs = [hbm] * self.n
        self.scratch = [pltpu.SemaphoreType.DMA((self.n * (N_DEV - 1),)), pltpu.SemaphoreType.DMA((self.n * (N_DEV - 1),)),
                        pltpu.SemaphoreType.DMA((self.n,))]

    def _copies(self, src_refs, out_refs, sems):
        send_sems, recv_sems, loc_sems = sems
        me, peers = _me_and_peers()
        local = [pltpu.make_async_copy(self.src_of(a, me, src_refs), self.dst_of(a, me, out_refs), loc_sems.at[a])
                 for a in range(self.n)]
        sends, recvs = [], []
        for k, (peer, peer_idx) in enumerate(peers):
            for a in range(self.n):
                s = self.n * k + a
                sends.append(pltpu.make_async_remote_copy(
                    src_ref=self.src_of(a, peer_idx, src_refs), dst_ref=self.dst_of(a, me, out_refs),
                    send_sem=send_sems.at[s], recv_sem=recv_sems.at[s], device_id=peer, device_id_type=MESH))
                recvs.append(pltpu.make_async_remote_copy(
                    src_ref=self.src_of(a, me, src_refs), dst_ref=self.dst_of(a, peer_idx, out_refs),
                    send_sem=send_sems.at[s], recv_sem=recv_sems.at[s], device_id=peer, device_id_type=MESH))
        return local, sends, recvs

    def start(self, src_refs, out_refs, sems):
        if not self.n:
            return
        local, sends, _ = self._copies(src_refs, out_refs, sems)
        for cp in local + sends:
            cp.start()

    def wait(self, src_refs, out_refs, sems):
        if not self.n:
            return
        local, sends, recvs = self._copies(src_refs, out_refs, sems)
        for cp in recvs:
            cp.wait_recv()
        for cp in sends:
            cp.wait_send()
        for cp in local:
            cp.wait()


def _gather_job(shards):
    return _SideJob(shards, [jax.ShapeDtypeStruct((N_DEV,) + s.shape, s.dtype) for s in shards],
                    src_of=lambda a, j, srcs: srcs[a], dst_of=lambda a, j, outs: outs[a].at[j])


def _scatter_job(grads):
    landing, parts = [], []
    for g in grads:
        if g.ndim == 3:
            landing.append(jax.ShapeDtypeStruct(g.shape, g.dtype))
            parts.append(None)
        else:
            r = g.shape[0] // N_DEV
            landing.append(jax.ShapeDtypeStruct((N_DEV, r, g.shape[1]), g.dtype))
            parts.append(r)

    def src_of(a, j, srcs):
        if parts[a] is None:
            return srcs[a].at[j]
        return srcs[a].at[pl.ds(pl.multiple_of(j * parts[a], 8), parts[a]), :]

    return _SideJob(grads, landing, src_of=src_of, dst_of=lambda a, j, outs: outs[a].at[j])


def _allgather_w_in(w_in_shard, row_shards):
    n_row = len(row_shards)

    def body(*refs):
        win_ref = refs[0]
        row_refs = refs[1:1 + n_row]
        out_win = refs[1 + n_row]
        row_outs = refs[2 + n_row:2 + 2 * n_row]
        win_b, send_sems, recv_sems, local_sem = refs[2 + 2 * n_row:]
        win_b[...] = win_ref[...].astype(BF16)
        x, y, c = lax.axis_index("x"), lax.axis_index("y"), lax.axis_index("c")
        me, sibling = (x, y, c), (x, y, 1 - c)
        chips = [(1 - x, y), (x, 1 - y), (1 - x, 1 - y)]
        slot = lambda p: out_win.at[4 * p[0] + 2 * p[1] + p[2]]

        def copy(k, block, to, src=None):
            return pltpu.make_async_remote_copy(
                src_ref=slot(block) if src is None else src, dst_ref=slot(block), send_sem=send_sems.at[k],
                recv_sem=recv_sems.at[k], device_id=to, device_id_type=MESH)

        mine = pltpu.make_async_copy(win_b, slot(me), local_sem)
        mine.start()
        first = [copy(0, me, sibling, src=win_b)]
        first += [copy(1 + j, me, (*chip, c), src=win_b) for j, chip in enumerate(chips)]
        for cp in first:
            cp.start()
        for r, o in zip(row_refs, row_outs):
            o[...] = r[...].astype(BF16)
        passed = [copy(4 + j, (*chip, c), sibling) for j, chip in enumerate(chips)]
        for j, chip in enumerate(chips):
            copy(1 + j, (*chip, c), me).wait_recv()
            passed[j].start()
        copy(0, sibling, me).wait_recv()
        for j, chip in enumerate(chips):
            copy(4 + j, (*chip, 1 - c), me).wait_recv()
        for cp in first + passed:
            cp.wait_send()
        mine.wait()

    vm = pl.BlockSpec(memory_space=pltpu.VMEM)
    return pl.pallas_call(
        body, name="allgather_w_in", in_specs=[vm] * (1 + n_row),
        out_specs=(pl.BlockSpec(memory_space=pl.ANY), *([vm] * n_row)),
        out_shape=(jax.ShapeDtypeStruct((N_DEV,) + w_in_shard.shape, BF16),
                   *[jax.ShapeDtypeStruct(r.shape, BF16) for r in row_shards]),
        scratch_shapes=[pltpu.VMEM(w_in_shard.shape, BF16), pltpu.SemaphoreType.DMA((N_DEV - 1,)),
                        pltpu.SemaphoreType.DMA((N_DEV - 1,)), pltpu.SemaphoreType.DMA],
        compiler_params=pltpu.CompilerParams(vmem_limit_bytes=VMEM_LIMIT),
    )(w_in_shard, *row_shards)


def _allreduce_small(small):
    rows = SMALL_ROWS // N_DEV

    def body(x_ref, out_ref, land, send1, recv1, send2, recv2):
        me, peers = _me_and_peers()
        block = lambda j: pl.ds(pl.multiple_of(j * rows, 8), rows)

        def phase(src_of, dst_of, send_sems, recv_sems):
            sends = [pltpu.make_async_remote_copy(src_ref=src_of(pidx), dst_ref=dst_of(me), send_sem=send_sems.at[k],
                                                  recv_sem=recv_sems.at[k], device_id=peer, device_id_type=MESH)
                     for k, (peer, pidx) in enumerate(peers)]
            recvs = [pltpu.make_async_remote_copy(src_ref=src_of(me), dst_ref=dst_of(pidx), send_sem=send_sems.at[k],
                                                  recv_sem=recv_sems.at[k], device_id=peer, device_id_type=MESH)
                     for k, (peer, pidx) in enumerate(peers)]
            for cp in sends:
                cp.start()
            for cp in recvs:
                cp.wait_recv()
            for cp in sends:
                cp.wait_send()

        land[me] = x_ref[block(me), :]
        phase(lambda j: x_ref.at[block(j), :], lambda j: land.at[j], send1, recv1)
        total = land[0]
        for j in range(1, N_DEV):
            total = total + land[j]
        out_ref[block(me), :] = total
        phase(lambda j: out_ref.at[block(me), :], lambda j: out_ref.at[block(j), :], send2, recv2)

    vm = pl.BlockSpec(memory_space=pltpu.VMEM)
    return pl.pallas_call(
        body, name="allreduce_small", in_specs=[vm], out_specs=vm, out_shape=jax.ShapeDtypeStruct(small.shape, F32),
        scratch_shapes=[pltpu.VMEM((N_DEV, rows, D_MODEL), F32)] + [pltpu.SemaphoreType.DMA((N_DEV - 1,))] * 4,
    )(small)


def _adamw(name, got, w, m, v):
    r, c = w.shape
    n_slots = got.shape[0]
    tr = _pick(r, (256, 128, 64))

    def body(got_ref, w_ref, m_ref, v_ref, g_ref, d_ref, nm_ref, nv_ref):
        g = got_ref[0].astype(F32)
        for j in range(1, n_slots):
            g = g + got_ref[j].astype(F32)
        nm = ADAM_B1 * m_ref[...] + (1.0 - ADAM_B1) * g
        nv = ADAM_B2 * v_ref[...] + (1.0 - ADAM_B2) * jnp.square(g)
        m_hat = nm / (1.0 - ADAM_B1 ** ADAM_STEP)
        v_hat = nv / (1.0 - ADAM_B2 ** ADAM_STEP)
        g_ref[...] = g
        d_ref[...] = -ADAM_LR * (m_hat / (jnp.sqrt(v_hat) + ADAM_EPS) + ADAM_WD * w_ref[...])
        nm_ref[...] = nm
        nv_ref[...] = nv

    blk = pl.BlockSpec((tr, c), lambda i: (i, 0))
    out = jax.ShapeDtypeStruct((r, c), F32)
    return pl.pallas_call(
        body, name=name, grid=(r // tr,),
        in_specs=[pl.BlockSpec((n_slots, tr, c), lambda i: (0, i, 0)), blk, blk, blk],
        out_specs=(blk, blk, blk, blk), out_shape=(out, out, out, out), compiler_params=_cp("parallel"),
    )(got, w, m, v)


_SMALL_VECS = ("norm1_g", "ret_gn_g", "s5_d", "s5_glu_b", "norm2_g", "norm_mem_g", "norm_f_g")
_SMALL_ORDER = _SMALL_VECS + ("s5_a_re", "s5_a_im", "s5_log_dt", "s5_b_re", "s5_b_im", "s5_c_re", "s5_c_im")


def _pack_small(t, extra_row=None):
    rows = [t[n].reshape(1, D_MODEL) for n in _SMALL_VECS]
    rows.append(jnp.zeros((1, D_MODEL), F32) if extra_row is None else extra_row)
    rows += [t["s5_a_re"].reshape(4, D_MODEL), t["s5_a_im"].reshape(4, D_MODEL)]
    rows.append(jnp.pad(t["s5_log_dt"].reshape(1, S5_G), ((0, 7), (0, D_MODEL - S5_G))))
    rows += [t[n].reshape(64, D_MODEL) for n in ("s5_b_re", "s5_b_im", "s5_c_re", "s5_c_im")]
    rows.append(jnp.zeros((SMALL_ROWS - 280, D_MODEL), F32))
    return jnp.concatenate(rows, axis=0)


def _unpack_small(p, shapes):
    out = {n: p[i].reshape(shapes[n]) for i, n in enumerate(_SMALL_VECS)}
    out["s5_a_re"] = p[8:12].reshape(shapes["s5_a_re"])
    out["s5_a_im"] = p[12:16].reshape(shapes["s5_a_im"])
    out["s5_log_dt"] = p[16, :S5_G].reshape(shapes["s5_log_dt"])
    for i, n in enumerate(("s5_b_re", "s5_b_im", "s5_c_re", "s5_c_im")):
        out[n] = p[24 + 64 * i:24 + 64 * (i + 1)].reshape(shapes[n])
    return out


_W_NAMES = ("norm1_g", "w_in", "ret_gn_g", "s5_a_re", "s5_a_im", "s5_log_dt", "s5_b_re", "s5_b_im", "s5_c_re", "s5_c_im",
            "s5_d", "s5_glu_w", "s5_glu_b", "w_out", "norm2_g", "norm_mem_g", "xa_wq", "xa_wk", "xa_wv", "xa_wo",
            "norm_f_g")
_ROW_NAMES = ("s5_glu_w", "w_out", "xa_wq", "xa_wk", "xa_wv", "xa_wo")


def kernel(x, mem, positions, norm1_g, w_in, ret_gn_g, s5_a_re, s5_a_im, s5_log_dt, s5_b_re, s5_b_im, s5_c_re, s5_c_im, s5_d, s5_glu_w, s5_glu_b, w_out, norm2_g, norm_mem_g, xa_wq, xa_wk, xa_wv, xa_wo, norm_f_g, loss_target, m_norm1_g, m_w_in, m_ret_gn_g, m_s5_a_re, m_s5_a_im, m_s5_log_dt, m_s5_b_re, m_s5_b_im, m_s5_c_re, m_s5_c_im, m_s5_d, m_s5_glu_w, m_s5_glu_b, m_w_out, m_norm2_g, m_norm_mem_g, m_xa_wq, m_xa_wk, m_xa_wv, m_xa_wo, m_norm_f_g, v_norm1_g, v_w_in, v_ret_gn_g, v_s5_a_re, v_s5_a_im, v_s5_log_dt, v_s5_b_re, v_s5_b_im, v_s5_c_re, v_s5_c_im, v_s5_d, v_s5_glu_w, v_s5_glu_b, v_w_out, v_norm2_g, v_norm_mem_g, v_xa_wq, v_xa_wk, v_xa_wv, v_xa_wo, v_norm_f_g):
    w = dict(norm1_g=norm1_g, w_in=w_in, ret_gn_g=ret_gn_g, s5_a_re=s5_a_re, s5_a_im=s5_a_im, s5_log_dt=s5_log_dt,
             s5_b_re=s5_b_re, s5_b_im=s5_b_im, s5_c_re=s5_c_re, s5_c_im=s5_c_im, s5_d=s5_d, s5_glu_w=s5_glu_w,
             s5_glu_b=s5_glu_b, w_out=w_out, norm2_g=norm2_g, norm_mem_g=norm_mem_g, xa_wq=xa_wq, xa_wk=xa_wk,
             xa_wv=xa_wv, xa_wo=xa_wo, norm_f_g=norm_f_g)
    mom = dict(norm1_g=m_norm1_g, w_in=m_w_in, ret_gn_g=m_ret_gn_g, s5_a_re=m_s5_a_re, s5_a_im=m_s5_a_im,
               s5_log_dt=m_s5_log_dt, s5_b_re=m_s5_b_re, s5_b_im=m_s5_b_im, s5_c_re=m_s5_c_re, s5_c_im=m_s5_c_im,
               s5_d=m_s5_d, s5_glu_w=m_s5_glu_w, s5_glu_b=m_s5_glu_b, w_out=m_w_out, norm2_g=m_norm2_g,
               norm_mem_g=m_norm_mem_g, xa_wq=m_xa_wq, xa_wk=m_xa_wk, xa_wv=m_xa_wv, xa_wo=m_xa_wo,
               norm_f_g=m_norm_f_g)
    var = dict(norm1_g=v_norm1_g, w_in=v_w_in, ret_gn_g=v_ret_gn_g, s5_a_re=v_s5_a_re, s5_a_im=v_s5_a_im,
               s5_log_dt=v_s5_log_dt, s5_b_re=v_s5_b_re, s5_b_im=v_s5_b_im, s5_c_re=v_s5_c_re, s5_c_im=v_s5_c_im,
               s5_d=v_s5_d, s5_glu_w=v_s5_glu_w, s5_glu_b=v_s5_glu_b, w_out=v_w_out, norm2_g=v_norm2_g,
               norm_mem_g=v_norm_mem_g, xa_wq=v_xa_wq, xa_wk=v_xa_wk, xa_wv=v_xa_wv, xa_wo=v_xa_wo,
               norm_f_g=v_norm_f_g)
    shapes = {n: w[n].shape for n in _W_NAMES}

    x2d, mem2d, tgt = x[0], mem[0], loss_target[0]
    l = x2d.shape[0]
    ret_c = _pick(l, (256, 128))
    s5_t = _pick(l, (256, 128))
    g1, g2, gm, gf = norm1_g, norm2_g, norm_mem_g, norm_f_g.reshape(1, D_MODEL)

    win_s, *row_shards_b = _allgather_w_in(w_in[0], [w[n][0] for n in _ROW_NAMES])

    to_gpn = lambda b: jnp.transpose(b, (0, 2, 1)).reshape(S5_G, S5_P * S5_N)
    from_gpn = lambda b: jnp.transpose(b.reshape(S5_G, S5_P, S5_N), (0, 2, 1))
    disc_args = (s5_a_re[0], s5_a_im[0], s5_log_dt[0].reshape(S5_G, 1), to_gpn(s5_b_re[0]), to_gpn(s5_b_im[0]))
    abar_re, abar_im, bb_re_t, bb_im_t = _s5_discretize(*disc_args)
    bbm, ccm = _s5_block_mats(from_gpn(bb_re_t), from_gpn(bb_im_t), s5_c_re[0], s5_c_im[0])
    a_z = _s5_z(abar_re, abar_im)

    h1 = _rms_fwd("norm1_fwd", x2d, g1)
    proj, *rows_01 = _mm_nn_slots("in_proj", h1, win_s, BF16, side=_gather_job(row_shards_b[:2]))
    full = {n: g.reshape(N_DEV * r, D_MODEL) for n, g, r in zip(_ROW_NAMES[:2], rows_01, ROW_SHARDS[:2])}
    half = RET_DK // 2
    inv = ROPE_BASE ** (-jnp.arange(half, dtype=F32) / half)
    cos_t, sin_t = _rope_tables(positions[0].reshape(l, 1), jnp.tile(inv, 128 // half)[None, :])
    rconsts = _ret_constants(ret_c)
    ret, o_saved, r_prev, q_rot, k_rot = _ret_fwd(proj, cos_t, sin_t, rconsts, ret_gn_g, ret_c)
    mix, xstart, *rows_xa = _s5_fwd(proj, ret, bbm, ccm, s5_d, full["s5_glu_w"], s5_glu_b, a_z, s5_t,
                                    side=_gather_job(row_shards_b[2:]))
    full.update({n: g.reshape(N_DEV * r, D_MODEL) for n, g, r in zip(_ROW_NAMES[2:], rows_xa, ROW_SHARDS[2:])})
    x1 = _mm_nn("out_proj", mix, full["w_out"], F32, residual=x2d)
    h2 = _rms_fwd("norm2_fwd", x1, g2)
    mn = _rms_fwd("norm_mem_fwd", mem2d, gm)
    qa = _mm_nn("xa_q", h2, full["xa_wq"], BF16)
    ka = _mm_nn("xa_k", mn, full["xa_wk"], BF16)
    va = _mm_nn("xa_v", mn, full["xa_wv"], BF16)
    oa = _attn_fwd(qa, ka, va)
    x2 = _mm_nn("xa_o", oa, full["xa_wo"], F32, residual=x1)
    dx2, dgf, loss_lanes = _loss_head(x2, gf, tgt)

    doa = _mm_nt("xa_o_dx", dx2, full["xa_wo"], BF16)
    dwo = _mm_tn("xa_o_dw", oa, dx2, BF16)
    dqa, dka, dva = _attn_bwd(qa, ka, va, doa)
    dh2 = _mm_nt("xa_q_dx", dqa, full["xa_wq"], F32)
    dwq = _mm_tn("xa_q_dw", h2, dqa, BF16)
    dx1, dg2 = _rms_bwd("norm2_bwd", x1, g2, dh2, dx2)
    dwk = _mm_tn("xa_k_dw", mn, dka, BF16)
    dwv = _mm_tn("xa_v_dw", mn, dva, BF16)
    dmn = _mm_nt("xa_v_dx", dva, full["xa_wv"], F32, residual=_mm_nt("xa_k_dx", dka, full["xa_wk"], F32))
    _, dgm = _rms_bwd("norm_mem_bwd", mem2d, gm, dmn, None)
    dmix = _mm_nt("out_proj_dx", dx1, full["w_out"], BF16)
    dwout = _mm_tn("out_proj_dw", mix, dx1, BF16)
    dret, dgn, *got_a = _ret_bwd(proj, q_rot, k_rot, cos_t, sin_t, rconsts, ret_gn_g, o_saved, r_prev, dmix, ret_c,
                                 side=_scatter_job([dwout, dwq, dwk, dwv, dwo]))
    dug, y2, dz, dbbm, dccm, dabar, dd, dgb = _s5_bwd(proj, dmix, xstart, bbm, ccm, s5_d, full["s5_glu_w"], s5_glu_b,
                                                      a_z, s5_t)
    dglu = _mm_tn("s5_glu_dw", y2, dz, BF16)
    dproj = jnp.concatenate([dret, dug], axis=1)
    dwin_s, got_glu = _mm_tn_slots("in_proj_dw", h1, dproj, N_DEV, BF16, side=_scatter_job([dglu]))
    dh1, got_win = _mm_nt_slots("in_proj_dx", dproj, win_s, F32, side=_scatter_job([dwin_s]))
    grad_x, dg1 = _rms_bwd("norm1_bwd", x2d, g1, dh1, dx1)

    dab_re, dab_im = _s5_unz(dabar)
    dbb_re, dbb_im = _s5_block_diag_bb(dbbm)
    dc_re, dc_im = _s5_block_diag_cc(dccm)
    da_re, da_im, dlog_dt, db_re_t, db_im_t = _s5_discretize_bwd(*disc_args, dab_re, dab_im, to_gpn(dbb_re),
                                                                 to_gpn(dbb_im))
    db_re, db_im = from_gpn(db_re_t), from_gpn(db_im_t)
    small_g = dict(norm1_g=dg1, ret_gn_g=dgn, s5_d=dd, s5_glu_b=dgb, norm2_g=dg2, norm_mem_g=dgm, norm_f_g=dgf,
                   s5_a_re=da_re, s5_a_im=da_im, s5_log_dt=dlog_dt, s5_b_re=db_re, s5_b_im=db_im, s5_c_re=dc_re,
                   s5_c_im=dc_im)
    small_pack = _pack_small(small_g, extra_row=loss_lanes)

    res = {}
    got = dict(zip(("w_out", "xa_wq", "xa_wk", "xa_wv", "xa_wo"), got_a), w_in=got_win, s5_glu_w=got_glu)
    for n in ("w_in",) + _ROW_NAMES:
        res[n] = _adamw("adamw_" + n, got[n], w[n][0], mom[n][0], var[n][0])
    small_sum = _allreduce_small(small_pack)
    small_out = _adamw("adamw_small", small_sum[None], _pack_small(w), _pack_small(mom), _pack_small(var))
    unpacked = [_unpack_small(a, shapes) for a in small_out]
    loss = (0.5 / D_MODEL) * jnp.sum(small_out[0][7])
    for n in _SMALL_ORDER:
        res[n] = tuple(u[n] for u in unpacked)

    outs = [loss, grad_x[None]]
    for part in range(4):
        for n in _W_NAMES:
            outs.append(res[n][part].reshape(shapes[n]))
    return tuple(outs)
```

```python
import functools

import jax
import jax.numpy as jnp
from jax import lax
from jax.experimental import pallas as pl
from jax.experimental.pallas import tpu as pltpu

F32 = jnp.float32
BF16 = jnp.bfloat16
MESH = pl.DeviceIdType.MESH

D_MODEL = 1024
RET_HEADS, RET_DK, RET_DV = 8, 64, 128
RET_QK = RET_HEADS * RET_DK
S5_G, S5_N, S5_P = 64, 64, 16
S5_NB = 8
S5_GB = S5_G // S5_NB
S5_BS = S5_GB * S5_N
S5_COLS = 2 * S5_G * S5_N
XA_HEADS, XA_DH = 4, 256
EPS = 1e-6
ROPE_BASE = 10000.0
N_DEV = 8
W_IN_SHARD = 640
ROW_SHARDS = (128, 256, 128, 128, 128, 128)
ROWPACK = sum(ROW_SHARDS)
SMALL_ROWS = 384
ADAM_LR, ADAM_B1, ADAM_B2, ADAM_EPS, ADAM_WD, ADAM_STEP = 0.001, 0.9, 0.999, 1e-08, 0.01, 10

VMEM_LIMIT = 56 * 1024 * 1024


def _cp(*sem):
    return pltpu.CompilerParams(dimension_semantics=tuple(sem), vmem_limit_bytes=VMEM_LIMIT)


def _dot(a, b):
    return jnp.dot(a, b, preferred_element_type=F32)


def _dot_nt(a, b):
    return lax.dot_general(a, b, (((1,), (1,)), ((), ())), preferred_element_type=F32)


def _dot_tn(a, b):
    return lax.dot_general(a, b, (((0,), (0,)), ((), ())), preferred_element_type=F32)


def _sigmoid(x):
    return 1.0 / (1.0 + jnp.exp(-x))


def _silu(x):
    return x * _sigmoid(x)


def _dsilu(x):
    s = _sigmoid(x)
    return s * (1.0 + x * (1.0 - s))


_GELU_C = 0.7978845608028654


def _gelu(x):
    return 0.5 * x * (1.0 + jnp.tanh(_GELU_C * (x + 0.044715 * (x * x * x))))


def _dgelu(x):
    t = jnp.tanh(_GELU_C * (x + 0.044715 * (x * x * x)))
    return 0.5 * (1.0 + t) + 0.5 * x * (1.0 - t * t) * (_GELU_C * (1.0 + 3.0 * 0.044715 * (x * x)))


def _pick(n, cands):
    for c in cands:
        if n % c == 0:
            return c
    return n


def _mm_core(name, operands, in_specs, out_spec, out_shape, grid, nk, dims, acc_shape, has_res, side=None):
    n_in = 3 if has_res else 2
    n_side_in = len(side.srcs) if side else 0
    n_side_out = side.n if side else 0

    def body(*refs):
        a_ref, b_ref = refs[0], refs[1]
        r_ref = refs[2] if has_res else None
        side_in = refs[n_in:n_in + n_side_in]
        o_ref = refs[n_in + n_side_in]
        side_out = refs[n_in + n_side_in + 1:n_in + n_side_in + 1 + n_side_out]
        rest = refs[n_in + n_side_in + 1 + n_side_out:]
        acc, sems = (rest[0], rest[1:]) if nk > 1 else (None, rest)
        i, j, k = pl.program_id(0), pl.program_id(1), pl.program_id(2)
        if side:
            @pl.when((i == 0) & (j == 0) & (k == 0))
            def _():
                side.start(side_in, side_out, sems)

        def product():
            return lax.dot_general(a_ref[...].astype(BF16), b_ref[...].astype(BF16), (dims, ((), ())),
                                   preferred_element_type=F32)

        if nk == 1:
            o_ref[...] = (product() + r_ref[...] if has_res else product()).astype(o_ref.dtype)
        else:
            @pl.when(k == 0)
            def _():
                acc[...] = jnp.zeros_like(acc)

            acc[...] += product()

            @pl.when(k == nk - 1)
            def _():
                r = acc[...]
                if has_res:
                    r = r + r_ref[...]
                o_ref[...] = r.astype(o_ref.dtype)

        if side:
            @pl.when((i == grid[0] - 1) & (j == grid[1] - 1) & (k == grid[2] - 1))
            def _():
                side.wait(side_in, side_out, sems)

    acc_scratch = [pltpu.VMEM(acc_shape, F32)] if nk > 1 else []
    if side:
        return pl.pallas_call(
            body, name=name, grid=grid, in_specs=list(in_specs) + side.in_specs,
            out_specs=(out_spec, *side.out_specs), out_shape=(out_shape, *side.landing),
            scratch_shapes=acc_scratch + side.scratch,
            compiler_params=_cp("arbitrary", "arbitrary", "arbitrary"),
        )(*operands, *side.srcs)
    return pl.pallas_call(
        body, name=name, grid=grid, in_specs=in_specs, out_specs=out_spec, out_shape=out_shape,
        scratch_shapes=acc_scratch,
        compiler_params=_cp("parallel", "parallel", "arbitrary"),
    )(*operands)


def _mm_nn(name, a, b, out_dtype, residual=None):
    m, kk = a.shape
    n = b.shape[1]
    tm, tn, tk = _pick(m, (1024, 512, 256)), _pick(n, (1024, 512)), _pick(kk, (1024, 512))
    ops = [a, b]
    specs = [pl.BlockSpec((tm, tk), lambda i, j, k: (i, k)), pl.BlockSpec((tk, tn), lambda i, j, k: (k, j))]
    if residual is not None:
        ops.append(residual)
        specs.append(pl.BlockSpec((tm, tn), lambda i, j, k: (i, j)))
    return _mm_core(name, ops, specs, pl.BlockSpec((tm, tn), lambda i, j, k: (i, j)),
                    jax.ShapeDtypeStruct((m, n), out_dtype), (m // tm, n // tn, kk // tk), kk // tk,
                    ((1,), (0,)), (tm, tn), residual is not None)


def _mm_nt(name, a, b, out_dtype, residual=None):
    m, kk = a.shape
    n = b.shape[0]
    tm, tn, tk = _pick(m, (1024, 512, 256)), _pick(n, (1024, 512)), _pick(kk, (1024, 512))
    ops = [a, b]
    specs = [pl.BlockSpec((tm, tk), lambda i, j, k: (i, k)), pl.BlockSpec((tn, tk), lambda i, j, k: (j, k))]
    if residual is not None:
        ops.append(residual)
        specs.append(pl.BlockSpec((tm, tn), lambda i, j, k: (i, j)))
    return _mm_core(name, ops, specs, pl.BlockSpec((tm, tn), lambda i, j, k: (i, j)),
                    jax.ShapeDtypeStruct((m, n), out_dtype), (m // tm, n // tn, kk // tk), kk // tk,
                    ((1,), (1,)), (tm, tn), residual is not None)


def _mm_tn(name, a, b, out_dtype):
    kk, m = a.shape
    n = b.shape[1]
    tm, tn, tk = _pick(m, (1024, 512)), _pick(n, (1024, 512)), _pick(kk, (1024, 512, 256))
    specs = [pl.BlockSpec((tk, tm), lambda i, j, k: (k, i)), pl.BlockSpec((tk, tn), lambda i, j, k: (k, j))]
    return _mm_core(name, [a, b], specs, pl.BlockSpec((tm, tn), lambda i, j, k: (i, j)),
                    jax.ShapeDtypeStruct((m, n), out_dtype), (m // tm, n // tn, kk // tk), kk // tk,
                    ((0,), (0,)), (tm, tn), False)


def _mm_nn_slots(name, a, b_slots, out_dtype, side=None):
    m, kk = a.shape
    s, _, ns = b_slots.shape
    tm, tk = _pick(m, (1024, 512, 256)), _pick(kk, (1024, 512))
    specs = [pl.BlockSpec((tm, tk), lambda i, j, k: (i, k)), pl.BlockSpec((None, tk, ns), lambda i, j, k: (j, k, 0))]
    return _mm_core(name, [a, b_slots], specs, pl.BlockSpec((tm, ns), lambda i, j, k: (i, j)),
                    jax.ShapeDtypeStruct((m, s * ns), out_dtype), (m // tm, s, kk // tk), kk // tk,
                    ((1,), (0,)), (tm, ns), False, side)


def _mm_nt_slots(name, a, b_slots, out_dtype, side=None):
    m = a.shape[0]
    s, n, ns = b_slots.shape
    tm, tn = _pick(m, (1024, 512, 256)), _pick(n, (1024, 512))
    specs = [pl.BlockSpec((tm, ns), lambda i, j, k: (i, k)), pl.BlockSpec((None, tn, ns), lambda i, j, k: (k, j, 0))]
    return _mm_core(name, [a, b_slots], specs, pl.BlockSpec((tm, tn), lambda i, j, k: (i, j)),
                    jax.ShapeDtypeStruct((m, n), out_dtype), (m // tm, n // tn, s), s,
                    ((1,), (1,)), (tm, tn), False, side)


def _mm_tn_slots(name, a, b, s, out_dtype, side=None):
    kk, m = a.shape
    ns = b.shape[1] // s
    tm, tk = _pick(m, (1024, 512)), _pick(kk, (1024, 512, 256))
    specs = [pl.BlockSpec((tk, tm), lambda i, j, k: (k, i)), pl.BlockSpec((tk, ns), lambda i, j, k: (k, j))]
    return _mm_core(name, [a, b], specs, pl.BlockSpec((None, tm, ns), lambda i, j, k: (j, i, 0)),
                    jax.ShapeDtypeStruct((s, m, ns), out_dtype), (m // tm, s, kk // tk), kk // tk,
                    ((0,), (0,)), (tm, ns), False, side)


def _rms_fwd(name, x, g):
    r, d = x.shape
    tr = _pick(r, (1024, 512, 256))

    def body(x_ref, g_ref, o_ref):
        xv = x_ref[...]
        rs = lax.rsqrt(jnp.mean(xv * xv, axis=-1, keepdims=True) + EPS)
        o_ref[...] = (xv * rs * g_ref[...]).astype(o_ref.dtype)

    return pl.pallas_call(
        body, name=name, grid=(r // tr,),
        in_specs=[pl.BlockSpec((tr, d), lambda i: (i, 0)), pl.BlockSpec((1, d), lambda i: (0, 0))],
        out_specs=pl.BlockSpec((tr, d), lambda i: (i, 0)),
        out_shape=jax.ShapeDtypeStruct((r, d), BF16), compiler_params=_cp("parallel"),
    )(x, g)


def _rms_bwd(name, x, g, dh, dres):
    r, d = x.shape
    tr = _pick(r, (512, 256))
    has_res = dres is not None

    def body(*refs):
        if has_res:
            x_ref, g_ref, dh_ref, dr_ref, dx_ref, dg_ref = refs
        else:
            x_ref, g_ref, dh_ref, dx_ref, dg_ref = refs
        i = pl.program_id(0)

        @pl.when(i == 0)
        def _():
            dg_ref[...] = jnp.zeros_like(dg_ref)

        xv = x_ref[...]
        dhv = dh_ref[...].astype(F32)
        rs = lax.rsqrt(jnp.mean(xv * xv, axis=-1, keepdims=True) + EPS)
        xn = xv * rs
        dg_ref[...] += jnp.sum(dhv * xn, axis=0, keepdims=True)
        dn = dhv * g_ref[...]
        dx = rs * (dn - xn * jnp.mean(dn * xn, axis=-1, keepdims=True))
        if has_res:
            dx = dx + dr_ref[...]
        dx_ref[...] = dx

    row = pl.BlockSpec((tr, d), lambda i: (i, 0))
    vec = pl.BlockSpec((1, d), lambda i: (0, 0))
    ops = [x, g, dh] + ([dres] if has_res else [])
    return pl.pallas_call(
        body, name=name, grid=(r // tr,),
        in_specs=[row, vec, row] + ([row] if has_res else []),
        out_specs=(row, vec),
        out_shape=(jax.ShapeDtypeStruct((r, d), F32), jax.ShapeDtypeStruct((1, d), F32)),
        compiler_params=_cp("arbitrary"),
    )(*ops)


def _loss_head(x2, gf, target):
    r, d = x2.shape
    tr = _pick(r, (512, 256))

    def body(x_ref, g_ref, t_ref, dx_ref, dg_ref, ls_ref):
        i = pl.program_id(0)

        @pl.when(i == 0)
        def _():
            dg_ref[...] = jnp.zeros_like(dg_ref)
            ls_ref[...] = jnp.zeros_like(ls_ref)

        xv = x_ref[...]
        rs = lax.rsqrt(jnp.mean(xv * xv, axis=-1, keepdims=True) + EPS)
        xn = xv * rs
        e = xn * g_ref[...] - t_ref[...]
        ls_ref[...] += jnp.sum(e * e, axis=0, keepdims=True)
        dy = e * (1.0 / d)
        dg_ref[...] += jnp.sum(dy * xn, axis=0, keepdims=True)
        dn = dy * g_ref[...]
        dx_ref[...] = rs * (dn - xn * jnp.mean(dn * xn, axis=-1, keepdims=True))

    row = pl.BlockSpec((tr, d), lambda i: (i, 0))
    vec = pl.BlockSpec((1, d), lambda i: (0, 0))
    return pl.pallas_call(
        body, name="loss_head", grid=(r // tr,), in_specs=[row, vec, row], out_specs=(row, vec, vec),
        out_shape=(jax.ShapeDtypeStruct((r, d), F32), jax.ShapeDtypeStruct((1, d), F32),
                   jax.ShapeDtypeStruct((1, d), F32)),
        compiler_params=_cp("arbitrary"),
    )(x2, gf, target)


def _rope_tables(pos_col, inv_row):
    l = pos_col.shape[0]
    tl = _pick(l, (1024, 512, 256))

    def body(p_ref, inv_ref, cos_ref, sin_ref):
        ang = p_ref[...].astype(F32) * inv_ref[...]
        lane = lax.broadcasted_iota(jnp.int32, ang.shape, 1)
        c = jnp.cos(ang)
        s = jnp.where((lane % RET_DK) < RET_DK // 2, -jnp.sin(ang), jnp.sin(ang))
        cos_ref[...] = jnp.tile(c, (1, RET_QK // 128))
        sin_ref[...] = jnp.tile(s, (1, RET_QK // 128))

    return pl.pallas_call(
        body, name="rope_tables", grid=(l // tl,),
        in_specs=[pl.BlockSpec((tl, 1), lambda i: (i, 0)), pl.BlockSpec((1, 128), lambda i: (0, 0))],
        out_specs=(pl.BlockSpec((tl, RET_QK), lambda i: (i, 0)), pl.BlockSpec((tl, RET_QK), lambda i: (i, 0))),
        out_shape=(jax.ShapeDtypeStruct((l, RET_QK), F32), jax.ShapeDtypeStruct((l, RET_QK), F32)),
        compiler_params=_cp("parallel"),
    )(pos_col, inv_row)


def _rot(x, cos_t, sin_t):
    n = x.shape[-1]
    lane = lax.broadcasted_iota(jnp.int32, x.shape, 1)
    partner = jnp.where((lane % RET_DK) < RET_DK // 2, pltpu.roll(x, n - RET_DK // 2, 1), pltpu.roll(x, RET_DK // 2, 1))
    return x * cos_t + partner * sin_t


def _ret_constants(c):
    log_g = jnp.log1p(-jnp.exp2(-5.0 - jnp.arange(RET_HEADS, dtype=F32)))
    j = jnp.arange(c, dtype=F32)
    diff = j[:, None] - j[None, :]
    decay = jnp.where(diff[None] >= 0.0, jnp.exp(log_g[:, None, None] * jnp.maximum(diff, 0.0)[None]), 0.0)
    q_w = jnp.exp(log_g[None, :] * (j + 1.0)[:, None])
    k_w = jnp.exp(log_g[None, :] * (c - 1.0 - j)[:, None])
    cd = jnp.exp(log_g * c)
    rep = lambda t: jnp.repeat(t, RET_DK, axis=1)
    cd_row = jnp.repeat(cd, RET_DV)[None, :]
    return decay, rep(q_w), rep(k_w), cd_row


def _pair_of(h, c):
    lane = lax.broadcasted_iota(jnp.int32, (c, 2 * RET_DK), 1)
    mine = (lane < RET_DK) if h % 2 == 0 else (lane >= RET_DK)
    return slice((h // 2) * 2 * RET_DK, (h // 2 + 1) * 2 * RET_DK), mine


def _keep(x, mine):
    return jnp.where(mine, x, jnp.zeros_like(x))


def _ret_fwd(proj, cos_t, sin_t, consts, gn_g, c):
    l = proj.shape[0]
    nc = l // c
    decay, qw, kw, cd_row = consts

    def body(q_ref, k_ref, v_ref, g_ref, cos_ref, sin_ref, dec_ref, qw_ref, kw_ref, cd_ref, gn_ref,
             ret_ref, o_ref, rp_ref, qb_ref, kb_ref, state):
        @pl.when(pl.program_id(0) == 0)
        def _():
            state[...] = jnp.zeros_like(state)

        cs, sn = cos_ref[...], sin_ref[...]
        qr = _rot(q_ref[...].astype(F32), cs, sn)
        kr = _rot(k_ref[...].astype(F32), cs, sn) * (RET_DK ** -0.5)
        qb, kb = qr.astype(BF16), kr.astype(BF16)
        qb_ref[...] = qb
        kb_ref[...] = kb
        qwb = (qr * qw_ref[...]).astype(BF16)
        kwb = (kr * kw_ref[...]).astype(BF16)
        vb = v_ref[...].astype(BF16)
        for h in range(RET_HEADS):
            ps, mine = _pair_of(h, c)
            vs = slice(h * RET_DV, (h + 1) * RET_DV)
            s = _dot_nt(_keep(qb[:, ps], mine), kb[:, ps]) * dec_ref[h]
            r_prev = state[h]
            rp_ref[0, h] = r_prev
            o = _dot(s.astype(BF16), vb[:, vs]) + _dot(_keep(qwb[:, ps], mine), r_prev.astype(BF16))
            state[h] = cd_ref[:, vs] * r_prev + _dot_tn(_keep(kwb[:, ps], mine), vb[:, vs])
            o_ref[:, vs] = o
            mu = jnp.mean(o, axis=-1, keepdims=True)
            var = jnp.mean(jnp.square(o - mu), axis=-1, keepdims=True)
            on = (o - mu) * lax.rsqrt(var + EPS)
            ret_ref[:, vs] = (on * gn_ref[:, vs] * _silu(g_ref[:, vs].astype(F32))).astype(ret_ref.dtype)

    const2 = lambda shape: pl.BlockSpec(shape, lambda i: (0,) * len(shape))
    return pl.pallas_call(
        body, name="retention_fwd", grid=(nc,),
        in_specs=[pl.BlockSpec((c, RET_QK), lambda i: (i, 0)), pl.BlockSpec((c, RET_QK), lambda i: (i, 1)),
                  pl.BlockSpec((c, D_MODEL), lambda i: (i, 1)), pl.BlockSpec((c, D_MODEL), lambda i: (i, 2)),
                  pl.BlockSpec((c, RET_QK), lambda i: (i, 0)), pl.BlockSpec((c, RET_QK), lambda i: (i, 0)),
                  const2((RET_HEADS, c, c)), const2((c, RET_QK)), const2((c, RET_QK)), const2((1, D_MODEL)),
                  const2((1, D_MODEL))],
        out_specs=(pl.BlockSpec((c, D_MODEL), lambda i: (i, 0)), pl.BlockSpec((c, D_MODEL), lambda i: (i, 0)),
                   pl.BlockSpec((1, RET_HEADS, 2 * RET_DK, RET_DV), lambda i: (i, 0, 0, 0)),
                   pl.BlockSpec((c, RET_QK), lambda i: (i, 0)), pl.BlockSpec((c, RET_QK), lambda i: (i, 0))),
        out_shape=(jax.ShapeDtypeStruct((l, 2 * D_MODEL), BF16), jax.ShapeDtypeStruct((l, D_MODEL), F32),
                   jax.ShapeDtypeStruct((nc, RET_HEADS, 2 * RET_DK, RET_DV), F32),
                   jax.ShapeDtypeStruct((l, RET_QK), BF16), jax.ShapeDtypeStruct((l, RET_QK), BF16)),
        scratch_shapes=[pltpu.VMEM((RET_HEADS, 2 * RET_DK, RET_DV), F32)],
        compiler_params=_cp("arbitrary"),
    )(proj, proj, proj, proj, cos_t, sin_t, decay, qw, kw, cd_row, gn_g)


def _ret_bwd(proj, qb_saved, kb_saved, cos_t, sin_t, consts, gn_g, o_saved, r_prev_saved, dmix, c, side):
    l = proj.shape[0]
    nc = l // c
    decay, qw, kw, cd_row = consts
    n_in = 14

    def body(*refs):
        (q_ref, k_ref, v_ref, g_ref, cos_ref, sin_ref, dec_ref, qw_ref, kw_ref, cd_ref, gn_ref, o_ref, rp_ref,
         dr_ref) = refs[:n_in]
        side_in = refs[n_in:n_in + len(side.srcs)]
        out_ref, dgn_ref = refs[n_in + len(side.srcs):n_in + len(side.srcs) + 2]
        side_out = refs[n_in + len(side.srcs) + 2:n_in + len(side.srcs) + 2 + side.n]
        state, dq_s, dk_s = refs[n_in + len(side.srcs) + 2 + side.n:n_in + len(side.srcs) + 5 + side.n]
        sems = refs[n_in + len(side.srcs) + 5 + side.n:]

        @pl.when(pl.program_id(0) == 0)
        def _():
            side.start(side_in, side_out, sems)
            state[...] = jnp.zeros_like(state)
            dgn_ref[...] = jnp.zeros_like(dgn_ref)

        cs, sn = cos_ref[...], sin_ref[...]
        qb, kb = q_ref[...], k_ref[...]
        qwv, kwv = qw_ref[...], kw_ref[...]
        qwb = (qb.astype(F32) * qwv).astype(BF16)
        kwb = (kb.astype(F32) * kwv).astype(BF16)
        vb = v_ref[...].astype(BF16)
        dq2 = dk2 = None
        for h in range(RET_HEADS):
            ps, mine = _pair_of(h, c)
            vs = slice(h * RET_DV, (h + 1) * RET_DV)
            dec = dec_ref[h]
            qm, km = _keep(qb[:, ps], mine), _keep(kb[:, ps], mine)
            o = o_ref[:, vs]
            mu = jnp.mean(o, axis=-1, keepdims=True)
            var = jnp.mean(jnp.square(o - mu), axis=-1, keepdims=True)
            rstd = lax.rsqrt(var + EPS)
            on = (o - mu) * rstd
            gate = g_ref[:, vs].astype(F32)
            sg = _silu(gate)
            dret = dr_ref[:, vs].astype(F32)
            gn = gn_ref[:, vs]
            dgn_ref[:, vs] += jnp.sum(dret * on * sg, axis=0, keepdims=True)
            out_ref[:, 2 * RET_QK + D_MODEL + h * RET_DV:2 * RET_QK + D_MODEL + (h + 1) * RET_DV] = (
                dret * on * gn * _dsilu(gate)).astype(out_ref.dtype)
            don = dret * gn * sg
            do = rstd * (don - jnp.mean(don, axis=-1, keepdims=True)
                         - on * jnp.mean(don * on, axis=-1, keepdims=True))
            dob = do.astype(BF16)
            sn_h = state[h]
            snb = sn_h.astype(BF16)
            s = _dot_nt(qm, kb[:, ps]) * dec
            dv = _dot_tn(s.astype(BF16), dob) + _dot(_keep(kwb[:, ps], mine), snb)
            out_ref[:, 2 * RET_QK + h * RET_DV:2 * RET_QK + (h + 1) * RET_DV] = dv.astype(out_ref.dtype)
            ds = (_dot_nt(dob, vb[:, vs]) * dec).astype(BF16)
            dq_h = _dot(ds, km) + qwv[:, ps] * _dot_nt(dob, rp_ref[0, h].astype(BF16))
            dk_h = _dot_tn(ds, qm) + kwv[:, ps] * _dot_nt(vb[:, vs], snb)
            state[h] = cd_ref[:, vs] * sn_h + _dot_tn(_keep(qwb[:, ps], mine), dob)
            if h % 2 == 0:
                dq2, dk2 = dq_h, dk_h
            else:
                dq_s[:, ps] = dq2 + dq_h
                dk_s[:, ps] = dk2 + dk_h
        out_ref[:, 0:RET_QK] = _rot(dq_s[...], cs, -sn).astype(out_ref.dtype)
        out_ref[:, RET_QK:2 * RET_QK] = (_rot(dk_s[...], cs, -sn) * (RET_DK ** -0.5)).astype(out_ref.dtype)

        @pl.when(pl.program_id(0) == nc - 1)
        def _():
            side.wait(side_in, side_out, sems)

    rev = lambda i: nc - 1 - i
    const2 = lambda shape: pl.BlockSpec(shape, lambda i: (0,) * len(shape))
    return pl.pallas_call(
        body, name="retention_bwd", grid=(nc,),
        in_specs=[pl.BlockSpec((c, RET_QK), lambda i: (rev(i), 0)), pl.BlockSpec((c, RET_QK), lambda i: (rev(i), 0)),
                  pl.BlockSpec((c, D_MODEL), lambda i: (rev(i), 1)), pl.BlockSpec((c, D_MODEL), lambda i: (rev(i), 2)),
                  pl.BlockSpec((c, RET_QK), lambda i: (rev(i), 0)), pl.BlockSpec((c, RET_QK), lambda i: (rev(i), 0)),
                  const2((RET_HEADS, c, c)), const2((c, RET_QK)), const2((c, RET_QK)), const2((1, D_MODEL)),
                  const2((1, D_MODEL)),
                  pl.BlockSpec((c, D_MODEL), lambda i: (rev(i), 0)),
                  pl.BlockSpec((1, RET_HEADS, 2 * RET_DK, RET_DV), lambda i: (rev(i), 0, 0, 0)),
                  pl.BlockSpec((c, D_MODEL), lambda i: (rev(i), 0))] + side.in_specs,
        out_specs=(pl.BlockSpec((c, 2 * RET_QK + 2 * D_MODEL), lambda i: (rev(i), 0)), const2((1, D_MODEL)),
                   *side.out_specs),
        out_shape=(jax.ShapeDtypeStruct((l, 2 * RET_QK + 2 * D_MODEL), BF16), jax.ShapeDtypeStruct((1, D_MODEL), F32),
                   *side.landing),
        scratch_shapes=[pltpu.VMEM((RET_HEADS, 2 * RET_DK, RET_DV), F32), pltpu.VMEM((c, RET_QK), F32),
                        pltpu.VMEM((c, RET_QK), F32)] + side.scratch,
        compiler_params=_cp("arbitrary"),
    )(qb_saved, kb_saved, proj, proj, cos_t, sin_t, decay, qw, kw, cd_row, gn_g, o_saved, r_prev_saved, dmix,
      *side.srcs)


def _zoh(a_re, a_im, log_dt):
    dt = jnp.exp(log_dt)
    mag = jnp.exp(a_re * dt)
    abar_re = mag * jnp.cos(a_im * dt)
    abar_im = mag * jnp.sin(a_im * dt)
    den = a_re * a_re + a_im * a_im
    nr, ni = abar_re - 1.0, abar_im
    f_re = (nr * a_re + ni * a_im) / den
    f_im = (ni * a_re - nr * a_im) / den
    return dt, abar_re, abar_im, f_re, f_im, den


def _lanes_p(f):
    return jnp.tile(f, (1, S5_P))


def _s5_discretize(a_re, a_im, log_dt, b_re_t, b_im_t):
    def body(ar_ref, ai_ref, ld_ref, br_ref, bi_ref, abr_ref, abi_ref, bbr_ref, bbi_ref):
        _, abar_re, abar_im, f_re, f_im, _ = _zoh(ar_ref[...], ai_ref[...], ld_ref[...])
        abr_ref[...] = abar_re
        abi_ref[...] = abar_im
        fr, fi = _lanes_p(f_re), _lanes_p(f_im)
        bbr_ref[...] = fr * br_ref[...] - fi * bi_ref[...]
        bbi_ref[...] = fr * bi_ref[...] + fi * br_ref[...]

    gn = jax.ShapeDtypeStruct((S5_G, S5_N), F32)
    gpn = jax.ShapeDtypeStruct((S5_G, S5_P * S5_N), F32)
    return pl.pallas_call(body, name="s5_discretize", out_shape=(gn, gn, gpn, gpn))(a_re, a_im, log_dt, b_re_t, b_im_t)


def _s5_discretize_bwd(a_re, a_im, log_dt, b_re_t, b_im_t, dab_re, dab_im, dbb_re_t, dbb_im_t):
    def body(ar_ref, ai_ref, ld_ref, br_ref, bi_ref, gar_ref, gai_ref, gbr_ref, gbi_ref,
             dar_ref, dai_ref, dld_ref, dbr_ref, dbi_ref):
        a_r, a_i = ar_ref[...], ai_ref[...]
        dt, abar_re, abar_im, f_re, f_im, den = _zoh(a_r, a_i, ld_ref[...])
        b_r, b_i, g_br, g_bi = br_ref[...], bi_ref[...], gbr_ref[...], gbi_ref[...]
        fr, fi = _lanes_p(f_re), _lanes_p(f_im)
        dbr_ref[...] = fr * g_br + fi * g_bi
        dbi_ref[...] = fr * g_bi - fi * g_br
        t_r = b_r * g_br + b_i * g_bi
        t_i = b_r * g_bi - b_i * g_br
        gf_r = sum(t_r[:, p * S5_N:(p + 1) * S5_N] for p in range(S5_P))
        gf_i = sum(t_i[:, p * S5_N:(p + 1) * S5_N] for p in range(S5_P))
        inv_r, inv_i = a_r / den, a_i / den
        ga_r = gar_ref[...] + gf_r * inv_r - gf_i * inv_i
        ga_i = gai_ref[...] + gf_r * inv_i + gf_i * inv_r
        q_r = -(f_re * a_r + f_im * a_i) / den
        q_i = -(f_im * a_r - f_re * a_i) / den
        gl_r = q_r * gf_r + q_i * gf_i
        gl_i = q_r * gf_i - q_i * gf_r
        dar_ref[...] = gl_r + dt * (abar_re * ga_r + abar_im * ga_i)
        dai_ref[...] = gl_i + dt * (abar_re * ga_i - abar_im * ga_r)
        la_r = a_r * abar_re - a_i * abar_im
        la_i = a_r * abar_im + a_i * abar_re
        dld_ref[...] = dt * jnp.sum(ga_r * la_r + ga_i * la_i, axis=-1, keepdims=True)

    gn = jax.ShapeDtypeStruct((S5_G, S5_N), F32)
    gpn = jax.ShapeDtypeStruct((S5_G, S5_P * S5_N), F32)
    return pl.pallas_call(
        body, name="s5_discretize_bwd", out_shape=(gn, gn, jax.ShapeDtypeStruct((S5_G, 1), F32), gpn, gpn),
    )(a_re, a_im, log_dt, b_re_t, b_im_t, dab_re, dab_im, dbb_re_t, dbb_im_t)


S5_ZQ = S5_NB // 2


def _s5_z(re, im):
    return jnp.concatenate([re.reshape(S5_ZQ, 8, 128), im.reshape(S5_ZQ, 8, 128)], axis=0)


def _s5_unz(z):
    return z[:S5_ZQ].reshape(S5_G, S5_N), z[S5_ZQ:].reshape(S5_G, S5_N)


def _s5_block_mats(bb_re, bb_im, c_re, c_im):
    eye = jnp.eye(S5_GB, dtype=F32)
    bb = jnp.stack([bb_re, bb_im], axis=0).reshape(2, S5_NB, S5_GB, S5_N, S5_P)
    bbm = jnp.einsum("rbgnp,gh->bgprhn", bb, eye).reshape(S5_NB, S5_GB * S5_P, 2 * S5_BS)
    cc = jnp.stack([c_re, -c_im], axis=0).reshape(2, S5_NB, S5_GB, S5_P, S5_N)
    ccm = jnp.einsum("rbgpn,gh->brhngp", cc, eye).reshape(S5_NB, 2 * S5_BS, S5_GB * S5_P)
    return bbm.astype(BF16), ccm.astype(BF16)


def _s5_block_diag_bb(m):
    t = m.reshape(S5_NB, S5_GB, S5_P, 2, S5_GB, S5_N)
    d = jnp.einsum("bgprgn->rbgnp", t).reshape(2, S5_G, S5_N, S5_P)
    return d[0], d[1]


def _s5_block_diag_cc(m):
    t = m.reshape(S5_NB, 2, S5_GB, S5_N, S5_GB, S5_P)
    d = jnp.einsum("brgngp->rbgpn", t).reshape(2, S5_G, S5_P, S5_N)
    return d[0], -d[1]


SCAN_UNROLL = 8


def _z_store(zr, zi, blk, res, t, off):
    q, h = blk // 2, blk % 2
    for lt in range(4):
        zr[q, pl.ds(off + 4 * h + lt, t, stride=8), :] = res[:, lt * 128:(lt + 1) * 128]
        zi[q, pl.ds(off + 4 * h + lt, t, stride=8), :] = res[:, S5_BS + lt * 128:S5_BS + (lt + 1) * 128]


def _z_load(zr, zi, blk, t, off):
    q, h = blk // 2, blk % 2
    return jnp.concatenate([zr[q, pl.ds(off + 4 * h + lt, t, stride=8), :] for lt in range(4)]
                           + [zi[q, pl.ds(off + 4 * h + lt, t, stride=8), :] for lt in range(4)], axis=1)


def _z_scan_fwd(zr, zi, a_ref, carry_ref, t, off):
    ar = [a_ref[q] for q in range(S5_ZQ)]
    ai = [a_ref[S5_ZQ + q] for q in range(S5_ZQ)]

    def step(it, carry):
        carry = list(carry)
        base = pl.multiple_of(it * (8 * SCAN_UNROLL), 8 * SCAN_UNROLL) + off
        for tt in range(SCAN_UNROLL):
            rows = pl.ds(base + 8 * tt, 8)
            for q in range(S5_ZQ):
                c_r, c_i = carry[q], carry[S5_ZQ + q]
                n_r = ar[q] * c_r - ai[q] * c_i + zr[q, rows, :]
                n_i = ar[q] * c_i + ai[q] * c_r + zi[q, rows, :]
                zr[q, rows, :] = n_r
                zi[q, rows, :] = n_i
                carry[q], carry[S5_ZQ + q] = n_r, n_i
        return tuple(carry)

    out = lax.fori_loop(0, t // SCAN_UNROLL, step, tuple(carry_ref[k] for k in range(2 * S5_ZQ)))
    for k in range(2 * S5_ZQ):
        carry_ref[k] = out[k]


def _z_scan_bwd(lr, li, xr, xi, a_ref, carry_ref, acc_ref, t):
    ar = [a_ref[q] for q in range(S5_ZQ)]
    ai = [a_ref[S5_ZQ + q] for q in range(S5_ZQ)]
    n_it = t // SCAN_UNROLL

    def step(it, state):
        carry, acc = list(state[0]), list(state[1])
        base = pl.multiple_of((n_it - 1 - it) * (8 * SCAN_UNROLL), 8 * SCAN_UNROLL)
        for tt in reversed(range(SCAN_UNROLL)):
            rows = pl.ds(base + 8 * tt, 8)
            for q in range(S5_ZQ):
                c_r, c_i = carry[q], carry[S5_ZQ + q]
                n_r = ar[q] * c_r + ai[q] * c_i + lr[q, rows, :]
                n_i = ar[q] * c_i - ai[q] * c_r + li[q, rows, :]
                lr[q, rows, :] = n_r
                li[q, rows, :] = n_i
                p_r, p_i = xr[q, rows, :], xi[q, rows, :]
                acc[q] = acc[q] + n_r * p_r + n_i * p_i
                acc[S5_ZQ + q] = acc[S5_ZQ + q] + n_i * p_r - n_r * p_i
                carry[q], carry[S5_ZQ + q] = n_r, n_i
        return tuple(carry), tuple(acc)

    k8 = range(2 * S5_ZQ)
    carry, acc = lax.fori_loop(0, n_it, step, (tuple(carry_ref[k] for k in k8), tuple(acc_ref[k] for k in k8)))
    for k in k8:
        carry_ref[k] = carry[k]
        acc_ref[k] = acc[k]


def _s5_fwd(proj, mix, bbm, ccm, d_row, glu_w, glu_b, tabs, t, side):
    l = proj.shape[0]
    nt = l // t
    n_in = 9

    def body(*refs):
        u_ref, gs_ref, bb_ref, cc_ref, d_ref, gw_ref, gb_ref, a_ref, _ = refs[:n_in]
        side_in = refs[n_in:n_in + len(side.srcs)]
        ssm_ref, xst_ref = refs[n_in + len(side.srcs):n_in + len(side.srcs) + 2]
        side_out = refs[n_in + len(side.srcs) + 2:n_in + len(side.srcs) + 2 + side.n]
        zr, zi, carry = refs[n_in + len(side.srcs) + 2 + side.n:n_in + len(side.srcs) + 5 + side.n]
        sems = refs[n_in + len(side.srcs) + 5 + side.n:]

        @pl.when(pl.program_id(0) == 0)
        def _():
            side.start(side_in, side_out, sems)
            carry[...] = jnp.zeros_like(carry)

        xst_ref[0] = carry[...]
        ub = u_ref[...]
        u = ub.astype(F32)
        for blk in range(S5_NB):
            _z_store(zr, zi, blk, _dot(ub[:, blk * 128:(blk + 1) * 128], bb_ref[blk]), t, 0)
        _z_scan_fwd(zr, zi, a_ref, carry, t, 0)
        ys = jnp.concatenate(
            [_dot(_z_load(zr, zi, blk, t, 0).astype(BF16), cc_ref[blk]) for blk in range(S5_NB)], axis=1)
        y2 = _gelu(ys + d_ref[...] * u)
        z = _dot(y2.astype(BF16), gw_ref[...]) + gb_ref[...]
        ssm_ref[...] = (y2 * _sigmoid(z) * _silu(gs_ref[...].astype(F32))).astype(ssm_ref.dtype)

        @pl.when(pl.program_id(0) == nt - 1)
        def _():
            side.wait(side_in, side_out, sems)

    const2 = lambda shape: pl.BlockSpec(shape, lambda i: (0,) * len(shape))
    zshape = (2 * S5_ZQ, 8, 128)
    return pl.pallas_call(
        body, name="s5_fwd", grid=(nt,),
        in_specs=[pl.BlockSpec((t, D_MODEL), lambda i: (i, 3)), pl.BlockSpec((t, D_MODEL), lambda i: (i, 4)),
                  const2(bbm.shape), const2(ccm.shape), const2((1, D_MODEL)), const2((D_MODEL, D_MODEL)),
                  const2((1, D_MODEL)), const2(zshape), pl.BlockSpec(memory_space=pl.ANY)] + side.in_specs,
        out_specs=(pl.BlockSpec((t, D_MODEL), lambda i: (i, 1)), pl.BlockSpec((1,) + zshape, lambda i: (i, 0, 0, 0)),
                   *side.out_specs),
        out_shape=(jax.ShapeDtypeStruct((l, 2 * D_MODEL), BF16), jax.ShapeDtypeStruct((nt,) + zshape, F32),
                   *side.landing),
        scratch_shapes=[pltpu.VMEM((S5_ZQ, 8 * t, 128), F32), pltpu.VMEM((S5_ZQ, 8 * t, 128), F32),
                        pltpu.VMEM(zshape, F32)] + side.scratch,
        input_output_aliases={8: 0},
        compiler_params=_cp("arbitrary"),
    )(proj, proj, bbm, ccm, d_row, glu_w, glu_b, tabs, mix, *side.srcs)


def _s5_bwd(proj, dmix, xstart, bbm, ccm, d_row, glu_w, glu_b, tabs, t):
    l = proj.shape[0]
    nt = l // t

    def body(u_ref, gs_ref, dm_ref, xst_ref, bb_ref, cc_ref, d_ref, gw_ref, gb_ref, a_ref,
             dug_ref, y2_ref, dz_ref, dbb_ref, dcc_ref, da_ref, dd_ref, dgb_ref, xr, xi, lr, li, carry, lcarry):
        @pl.when(pl.program_id(0) == 0)
        def _():
            lcarry[...] = jnp.zeros_like(lcarry)
            dbb_ref[...] = jnp.zeros_like(dbb_ref)
            dcc_ref[...] = jnp.zeros_like(dcc_ref)
            da_ref[...] = jnp.zeros_like(da_ref)
            dd_ref[...] = jnp.zeros_like(dd_ref)
            dgb_ref[...] = jnp.zeros_like(dgb_ref)

        carry[...] = xst_ref[0]
        for q in range(S5_ZQ):
            xr[q, 0:8, :] = carry[q]
            xi[q, 0:8, :] = carry[S5_ZQ + q]
        ub = u_ref[...]
        u = ub.astype(F32)
        for blk in range(S5_NB):
            _z_store(xr, xi, blk, _dot(ub[:, blk * 128:(blk + 1) * 128], bb_ref[blk]), t, 8)
        _z_scan_fwd(xr, xi, a_ref, carry, t, 8)
        ys = jnp.concatenate(
            [_dot(_z_load(xr, xi, blk, t, 8).astype(BF16), cc_ref[blk]) for blk in range(S5_NB)], axis=1)
        dv = d_ref[...]
        y1 = ys + dv * u
        y2 = _gelu(y1)
        y2b = y2.astype(BF16)
        sg = _sigmoid(_dot(y2b, gw_ref[...]) + gb_ref[...])
        gs = gs_ref[...].astype(F32)
        dssm = dm_ref[...].astype(F32)
        dug_ref[:, D_MODEL:] = (dssm * (y2 * sg) * _dsilu(gs)).astype(dug_ref.dtype)
        dy3 = dssm * _silu(gs)
        dz = dy3 * y2 * sg * (1.0 - sg)
        dzb = dz.astype(BF16)
        y2_ref[...] = y2b
        dz_ref[...] = dzb
        dgb_ref[...] += jnp.sum(dz, axis=0, keepdims=True)
        dy1 = (dy3 * sg + _dot_nt(dzb, gw_ref[...])) * _dgelu(y1)
        dd_ref[...] += jnp.sum(dy1 * u, axis=0, keepdims=True)
        dyb = dy1.astype(BF16)
        for blk in range(S5_NB):
            ch = slice(blk * 128, (blk + 1) * 128)
            _z_store(lr, li, blk, _dot_nt(dyb[:, ch], cc_ref[blk]), t, 0)
            dcc_ref[blk] += _dot_tn(_z_load(xr, xi, blk, t, 8).astype(BF16), dyb[:, ch])
        _z_scan_bwd(lr, li, xr, xi, a_ref, lcarry, da_ref, t)
        du = []
        for blk in range(S5_NB):
            lb = _z_load(lr, li, blk, t, 0).astype(BF16)
            du.append(_dot_nt(lb, bb_ref[blk]))
            dbb_ref[blk] += _dot_tn(ub[:, blk * 128:(blk + 1) * 128], lb)
        dug_ref[:, :D_MODEL] = (jnp.concatenate(du, axis=1) + dy1 * dv).astype(dug_ref.dtype)

    rev = lambda i: nt - 1 - i
    const2 = lambda shape: pl.BlockSpec(shape, lambda i: (0,) * len(shape))
    row_out = lambda w: pl.BlockSpec((t, w), lambda i: (rev(i), 0))
    zshape = (2 * S5_ZQ, 8, 128)
    return pl.pallas_call(
        body, name="s5_bwd", grid=(nt,),
        in_specs=[pl.BlockSpec((t, D_MODEL), lambda i: (rev(i), 3)), pl.BlockSpec((t, D_MODEL), lambda i: (rev(i), 4)),
                  pl.BlockSpec((t, D_MODEL), lambda i: (rev(i), 1)),
                  pl.BlockSpec((1,) + zshape, lambda i: (rev(i), 0, 0, 0)),
                  const2(bbm.shape), const2(ccm.shape), const2((1, D_MODEL)), const2((D_MODEL, D_MODEL)),
                  const2((1, D_MODEL)), const2(zshape)],
        out_specs=(row_out(2 * D_MODEL), row_out(D_MODEL), row_out(D_MODEL), const2(bbm.shape), const2(ccm.shape),
                   const2(zshape), const2((1, D_MODEL)), const2((1, D_MODEL))),
        out_shape=(jax.ShapeDtypeStruct((l, 2 * D_MODEL), BF16), jax.ShapeDtypeStruct((l, D_MODEL), BF16),
                   jax.ShapeDtypeStruct((l, D_MODEL), BF16), jax.ShapeDtypeStruct(bbm.shape, F32),
                   jax.ShapeDtypeStruct(ccm.shape, F32), jax.ShapeDtypeStruct(zshape, F32),
                   jax.ShapeDtypeStruct((1, D_MODEL), F32), jax.ShapeDtypeStruct((1, D_MODEL), F32)),
        scratch_shapes=[pltpu.VMEM((S5_ZQ, 8 * t + 8, 128), F32), pltpu.VMEM((S5_ZQ, 8 * t + 8, 128), F32),
                        pltpu.VMEM((S5_ZQ, 8 * t, 128), F32), pltpu.VMEM((S5_ZQ, 8 * t, 128), F32),
                        pltpu.VMEM(zshape, F32), pltpu.VMEM(zshape, F32)],
        compiler_params=_cp("arbitrary"),
    )(proj, proj, dmix, xstart, bbm, ccm, d_row, glu_w, glu_b, tabs)


def _attn_probs(qh, kh):
    s = _dot_nt(qh, kh) * (XA_DH ** -0.5)
    e = jnp.exp(s - jnp.max(s, axis=-1, keepdims=True))
    return e / jnp.sum(e, axis=-1, keepdims=True)


def _attn_fwd(qa, ka, va):
    l = qa.shape[0]
    m = ka.shape[0]
    tl = _pick(l, (512, 256))

    def body(q_ref, k_ref, v_ref, o_ref):
        for h in range(XA_HEADS):
            hs = slice(h * XA_DH, (h + 1) * XA_DH)
            p = _attn_probs(q_ref[:, hs], k_ref[:, hs])
            o_ref[:, hs] = _dot(p.astype(BF16), v_ref[:, hs]).astype(o_ref.dtype)

    return pl.pallas_call(
        body, name="xattn_fwd", grid=(l // tl,),
        in_specs=[pl.BlockSpec((tl, D_MODEL), lambda i: (i, 0)), pl.BlockSpec((m, D_MODEL), lambda i: (0, 0)),
                  pl.BlockSpec((m, D_MODEL), lambda i: (0, 0))],
        out_specs=pl.BlockSpec((tl, D_MODEL), lambda i: (i, 0)),
        out_shape=jax.ShapeDtypeStruct((l, D_MODEL), BF16), compiler_params=_cp("parallel"),
    )(qa, ka, va)


def _attn_bwd(qa, ka, va, doa):
    l = qa.shape[0]
    m = ka.shape[0]
    tl = _pick(l, (512, 256))

    def body(q_ref, k_ref, v_ref, do_ref, dq_ref, dk_ref, dv_ref):
        @pl.when(pl.program_id(0) == 0)
        def _():
            dk_ref[...] = jnp.zeros_like(dk_ref)
            dv_ref[...] = jnp.zeros_like(dv_ref)

        for h in range(XA_HEADS):
            hs = slice(h * XA_DH, (h + 1) * XA_DH)
            qh, kh, vh, doh = q_ref[:, hs], k_ref[:, hs], v_ref[:, hs], do_ref[:, hs]
            p = _attn_probs(qh, kh)
            dv_ref[:, hs] += _dot_tn(p.astype(BF16), doh)
            dp = _dot_nt(doh, vh)
            ds = (p * (dp - jnp.sum(dp * p, axis=-1, keepdims=True)) * (XA_DH ** -0.5)).astype(BF16)
            dq_ref[:, hs] = _dot(ds, kh).astype(dq_ref.dtype)
            dk_ref[:, hs] += _dot_tn(ds, qh)

    row = pl.BlockSpec((tl, D_MODEL), lambda i: (i, 0))
    mem = pl.BlockSpec((m, D_MODEL), lambda i: (0, 0))
    return pl.pallas_call(
        body, name="xattn_bwd", grid=(l // tl,), in_specs=[row, mem, mem, row], out_specs=(row, mem, mem),
        out_shape=(jax.ShapeDtypeStruct((l, D_MODEL), BF16), jax.ShapeDtypeStruct((m, D_MODEL), F32),
                   jax.ShapeDtypeStruct((m, D_MODEL), F32)),
        compiler_params=_cp("arbitrary"),
    )(qa, ka, va, doa)


def _me_and_peers():
    x, y, c = lax.axis_index("x"), lax.axis_index("y"), lax.axis_index("c")
    flip = lambda v, bit: (1 - v) if bit else v
    peers = []
    for k in range(1, N_DEV):
        px, py, pc = flip(x, (k >> 2) & 1), flip(y, (k >> 1) & 1), flip(c, k & 1)
        peers.append(((px, py, pc), 4 * px + 2 * py + pc))
    return 4 * x + 2 * y + c, peers


class _SideJob:
    def __init__(self, srcs, landing, src_of, dst_of):
        self.srcs = list(srcs)
        self.landing = list(landing)
        self.n = len(self.landing)
        self.src_of, self.dst_of = src_of, dst_of
        hbm = pl.BlockSpec(memory_space=pl.ANY)
        self.in_specs = [hbm] * len(self.srcs)
        self.out_specs = [hbm] * self.n
        self.scratch = [pltpu.SemaphoreType.DMA((self.n * (N_DEV - 1),)), pltpu.SemaphoreType.DMA((self.n * (N_DEV - 1),)),
                        pltpu.SemaphoreType.DMA((self.n,))]

    def _copies(self, src_refs, out_refs, sems):
        send_sems, recv_sems, loc_sems = sems
        me, peers = _me_and_peers()
        local = [pltpu.make_async_copy(self.src_of(a, me, src_refs), self.dst_of(a, me, out_refs), loc_sems.at[a])
                 for a in range(self.n)]
        sends, recvs = [], []
        for k, (peer, peer_idx) in enumerate(peers):
            for a in range(self.n):
                s = self.n * k + a
                sends.append(pltpu.make_async_remote_copy(
                    src_ref=self.src_of(a, peer_idx, src_refs), dst_ref=self.dst_of(a, me, out_refs),
                    send_sem=send_sems.at[s], recv_sem=recv_sems.at[s], device_id=peer, device_id_type=MESH))
                recvs.append(pltpu.make_async_remote_copy(
                    src_ref=self.src_of(a, me, src_refs), dst_ref=self.dst_of(a, peer_idx, out_refs),
                    send_sem=send_sems.at[s], recv_sem=recv_sems.at[s], device_id=peer, device_id_type=MESH))
        return local, sends, recvs

    def start(self, src_refs, out_refs, sems):
        if not self.n:
            return
        local, sends, _ = self._copies(src_refs, out_refs, sems)
        for cp in local + sends:
            cp.start()

    def wait(self, src_refs, out_refs, sems):
        if not self.n:
            return
        local, sends, recvs = self._copies(src_refs, out_refs, sems)
        for cp in recvs:
            cp.wait_recv()
        for cp in sends:
            cp.wait_send()
        for cp in local:
            cp.wait()


def _gather_job(shards):
    return _SideJob(shards, [jax.ShapeDtypeStruct((N_DEV,) + s.shape, s.dtype) for s in shards],
                    src_of=lambda a, j, srcs: srcs[a], dst_of=lambda a, j, outs: outs[a].at[j])


def _scatter_job(grads):
    landing, parts = [], []
    for g in grads:
        if g.ndim == 3:
            landing.append(jax.ShapeDtypeStruct(g.shape, g.dtype))
            parts.append(None)
        else:
            r = g.shape[0] // N_DEV
            landing.append(jax.ShapeDtypeStruct((N_DEV, r, g.shape[1]), g.dtype))
            parts.append(r)

    def src_of(a, j, srcs):
        if parts[a] is None:
            return srcs[a].at[j]
        return srcs[a].at[pl.ds(pl.multiple_of(j * parts[a], 8), parts[a]), :]

    return _SideJob(grads, landing, src_of=src_of, dst_of=lambda a, j, outs: outs[a].at[j])


def _allgather_w_in(w_in_shard, row_shards):
    n_row = len(row_shards)

    def body(*refs):
        win_ref = refs[0]
        row_refs = refs[1:1 + n_row]
        out_win = refs[1 + n_row]
        row_outs = refs[2 + n_row:2 + 2 * n_row]
        win_b, send_sems, recv_sems, local_sem = refs[2 + 2 * n_row:]
        win_b[...] = win_ref[...].astype(BF16)
        x, y, c = lax.axis_index("x"), lax.axis_index("y"), lax.axis_index("c")
        me, sibling = (x, y, c), (x, y, 1 - c)
        chips = [(1 - x, y), (x, 1 - y), (1 - x, 1 - y)]
        slot = lambda p: out_win.at[4 * p[0] + 2 * p[1] + p[2]]

        def copy(k, block, to, src=None):
            return pltpu.make_async_remote_copy(
                src_ref=slot(block) if src is None else src, dst_ref=slot(block), send_sem=send_sems.at[k],
                recv_sem=recv_sems.at[k], device_id=to, device_id_type=MESH)

        mine = pltpu.make_async_copy(win_b, slot(me), local_sem)
        mine.start()
        first = [copy(0, me, sibling, src=win_b)]
        first += [copy(1 + j, me, (*chip, c), src=win_b) for j, chip in enumerate(chips)]
        for cp in first:
            cp.start()
        for r, o in zip(row_refs, row_outs):
            o[...] = r[...].astype(BF16)
        passed = [copy(4 + j, (*chip, c), sibling) for j, chip in enumerate(chips)]
        for j, chip in enumerate(chips):
            copy(1 + j, (*chip, c), me).wait_recv()
            passed[j].start()
        copy(0, sibling, me).wait_recv()
        for j, chip in enumerate(chips):
            copy(4 + j, (*chip, 1 - c), me).wait_recv()
        for cp in first + passed:
            cp.wait_send()
        mine.wait()

    vm = pl.BlockSpec(memory_space=pltpu.VMEM)
    return pl.pallas_call(
        body, name="allgather_w_in", in_specs=[vm] * (1 + n_row),
        out_specs=(pl.BlockSpec(memory_space=pl.ANY), *([vm] * n_row)),
        out_shape=(jax.ShapeDtypeStruct((N_DEV,) + w_in_shard.shape, BF16),
                   *[jax.ShapeDtypeStruct(r.shape, BF16) for r in row_shards]),
        scratch_shapes=[pltpu.VMEM(w_in_shard.shape, BF16), pltpu.SemaphoreType.DMA((N_DEV - 1,)),
                        pltpu.SemaphoreType.DMA((N_DEV - 1,)), pltpu.SemaphoreType.DMA],
        compiler_params=pltpu.CompilerParams(vmem_limit_bytes=VMEM_LIMIT),
    )(w_in_shard, *row_shards)


def _allreduce_small(small):
    rows = SMALL_ROWS // N_DEV

    def body(x_ref, out_ref, land, send1, recv1, send2, recv2):
        me, peers = _me_and_peers()
        block = lambda j: pl.ds(pl.multiple_of(j * rows, 8), rows)

        def phase(src_of, dst_of, send_sems, recv_sems):
            sends = [pltpu.make_async_remote_copy(src_ref=src_of(pidx), dst_ref=dst_of(me), send_sem=send_sems.at[k],
                                                  recv_sem=recv_sems.at[k], device_id=peer, device_id_type=MESH)
                     for k, (peer, pidx) in enumerate(peers)]
            recvs = [pltpu.make_async_remote_copy(src_ref=src_of(me), dst_ref=dst_of(pidx), send_sem=send_sems.at[k],
                                                  recv_sem=recv_sems.at[k], device_id=peer, device_id_type=MESH)
                     for k, (peer, pidx) in enumerate(peers)]
            for cp in sends:
                cp.start()
            for cp in recvs:
                cp.wait_recv()
            for cp in sends:
                cp.wait_send()

        land[me] = x_ref[block(me), :]
        phase(lambda j: x_ref.at[block(j), :], lambda j: land.at[j], send1, recv1)
        total = land[0]
        for j in range(1, N_DEV):
            total = total + land[j]
        out_ref[block(me), :] = total
        phase(lambda j: out_ref.at[block(me), :], lambda j: out_ref.at[block(j), :], send2, recv2)

    vm = pl.BlockSpec(memory_space=pltpu.VMEM)
    return pl.pallas_call(
        body, name="allreduce_small", in_specs=[vm], out_specs=vm, out_shape=jax.ShapeDtypeStruct(small.shape, F32),
        scratch_shapes=[pltpu.VMEM((N_DEV, rows, D_MODEL), F32)] + [pltpu.SemaphoreType.DMA((N_DEV - 1,))] * 4,
    )(small)


def _adamw(name, got, w, m, v):
    r, c = w.shape
    n_slots = got.shape[0]
    tr = _pick(r, (256, 128, 64))

    def body(got_ref, w_ref, m_ref, v_ref, g_ref, d_ref, nm_ref, nv_ref):
        g = got_ref[0].astype(F32)
        for j in range(1, n_slots):
            g = g + got_ref[j].astype(F32)
        nm = ADAM_B1 * m_ref[...] + (1.0 - ADAM_B1) * g
        nv = ADAM_B2 * v_ref[...] + (1.0 - ADAM_B2) * jnp.square(g)
        m_hat = nm / (1.0 - ADAM_B1 ** ADAM_STEP)
        v_hat = nv / (1.0 - ADAM_B2 ** ADAM_STEP)
        g_ref[...] = g
        d_ref[...] = -ADAM_LR * (m_hat / (jnp.sqrt(v_hat) + ADAM_EPS) + ADAM_WD * w_ref[...])
        nm_ref[...] = nm
        nv_ref[...] = nv

    blk = pl.BlockSpec((tr, c), lambda i: (i, 0))
    out = jax.ShapeDtypeStruct((r, c), F32)
    return pl.pallas_call(
        body, name=name, grid=(r // tr,),
        in_specs=[pl.BlockSpec((n_slots, tr, c), lambda i: (0, i, 0)), blk, blk, blk],
        out_specs=(blk, blk, blk, blk), out_shape=(out, out, out, out), compiler_params=_cp("parallel"),
    )(got, w, m, v)


_SMALL_VECS = ("norm1_g", "ret_gn_g", "s5_d", "s5_glu_b", "norm2_g", "norm_mem_g", "norm_f_g")
_SMALL_ORDER = _SMALL_VECS + ("s5_a_re", "s5_a_im", "s5_log_dt", "s5_b_re", "s5_b_im", "s5_c_re", "s5_c_im")


def _small_layout():
    lay, row = {}, 0
    for n in _SMALL_VECS + ("loss",):
        lay[n] = (row, 1, D_MODEL)
        row += 8
    for n in ("s5_a_re", "s5_a_im"):
        lay[n] = (row, 4, D_MODEL)
        row += 8
    lay["s5_log_dt"] = (row, 1, S5_G)
    row += 8
    for n in ("s5_b_re", "s5_b_im", "s5_c_re", "s5_c_im"):
        lay[n] = (row, 64, D_MODEL)
        row += 64
    assert row <= SMALL_ROWS
    return lay


def _pack_small(t, loss_row=None):
    lay = _small_layout()
    pieces, row = [], 0
    for n, (r0, rows, lanes) in lay.items():
        if n == "loss":
            a = jnp.zeros((1, D_MODEL), F32) if loss_row is None else loss_row
        else:
            a = t[n].reshape(rows, lanes)
            if lanes < D_MODEL:
                a = jnp.pad(a, ((0, 0), (0, D_MODEL - lanes)))
        pieces.append(a)
        if rows % 8:
            pieces.append(jnp.zeros((8 - rows % 8, D_MODEL), F32))
        row = r0 + rows + (-rows) % 8
    pieces.append(jnp.zeros((SMALL_ROWS - row, D_MODEL), F32))
    return jnp.concatenate(pieces, axis=0)


def _adamw_small(g_sum, w, m, v):
    lay = _small_layout()
    names = [n for n in lay if n != "loss"]

    def body(g_ref, w_ref, m_ref, v_ref, *outs):
        g = g_ref[...]
        nm = ADAM_B1 * m_ref[...] + (1.0 - ADAM_B1) * g
        nv = ADAM_B2 * v_ref[...] + (1.0 - ADAM_B2) * jnp.square(g)
        m_hat = nm / (1.0 - ADAM_B1 ** ADAM_STEP)
        v_hat = nv / (1.0 - ADAM_B2 ** ADAM_STEP)
        delta = -ADAM_LR * (m_hat / (jnp.sqrt(v_hat) + ADAM_EPS) + ADAM_WD * w_ref[...])
        for i, n in enumerate(names):
            r0, rows, lanes = lay[n]
            for part, val in enumerate((g, delta, nm, nv)):
                outs[4 * i + part][...] = val[r0:r0 + rows, 0:lanes]
        r0 = lay["loss"][0]
        outs[-1][...] = g[r0:r0 + 1, :]

    shapes = []
    for n in names:
        shapes += [jax.ShapeDtypeStruct(lay[n][1:], F32)] * 4
    shapes.append(jax.ShapeDtypeStruct((1, D_MODEL), F32))
    outs = pl.pallas_call(body, name="adamw_small", out_shape=tuple(shapes),
                          compiler_params=pltpu.CompilerParams(vmem_limit_bytes=VMEM_LIMIT))(g_sum, w, m, v)
    return {n: tuple(outs[4 * i:4 * i + 4]) for i, n in enumerate(names)}, outs[-1]


_W_NAMES = ("norm1_g", "w_in", "ret_gn_g", "s5_a_re", "s5_a_im", "s5_log_dt", "s5_b_re", "s5_b_im", "s5_c_re", "s5_c_im",
            "s5_d", "s5_glu_w", "s5_glu_b", "w_out", "norm2_g", "norm_mem_g", "xa_wq", "xa_wk", "xa_wv", "xa_wo",
            "norm_f_g")
_ROW_NAMES = ("s5_glu_w", "w_out", "xa_wq", "xa_wk", "xa_wv", "xa_wo")


def kernel(x, mem, positions, norm1_g, w_in, ret_gn_g, s5_a_re, s5_a_im, s5_log_dt, s5_b_re, s5_b_im, s5_c_re, s5_c_im, s5_d, s5_glu_w, s5_glu_b, w_out, norm2_g, norm_mem_g, xa_wq, xa_wk, xa_wv, xa_wo, norm_f_g, loss_target, m_norm1_g, m_w_in, m_ret_gn_g, m_s5_a_re, m_s5_a_im, m_s5_log_dt, m_s5_b_re, m_s5_b_im, m_s5_c_re, m_s5_c_im, m_s5_d, m_s5_glu_w, m_s5_glu_b, m_w_out, m_norm2_g, m_norm_mem_g, m_xa_wq, m_xa_wk, m_xa_wv, m_xa_wo, m_norm_f_g, v_norm1_g, v_w_in, v_ret_gn_g, v_s5_a_re, v_s5_a_im, v_s5_log_dt, v_s5_b_re, v_s5_b_im, v_s5_c_re, v_s5_c_im, v_s5_d, v_s5_glu_w, v_s5_glu_b, v_w_out, v_norm2_g, v_norm_mem_g, v_xa_wq, v_xa_wk, v_xa_wv, v_xa_wo, v_norm_f_g):
    w = dict(norm1_g=norm1_g, w_in=w_in, ret_gn_g=ret_gn_g, s5_a_re=s5_a_re, s5_a_im=s5_a_im, s5_log_dt=s5_log_dt,
             s5_b_re=s5_b_re, s5_b_im=s5_b_im, s5_c_re=s5_c_re, s5_c_im=s5_c_im, s5_d=s5_d, s5_glu_w=s5_glu_w,
             s5_glu_b=s5_glu_b, w_out=w_out, norm2_g=norm2_g, norm_mem_g=norm_mem_g, xa_wq=xa_wq, xa_wk=xa_wk,
             xa_wv=xa_wv, xa_wo=xa_wo, norm_f_g=norm_f_g)
    mom = dict(norm1_g=m_norm1_g, w_in=m_w_in, ret_gn_g=m_ret_gn_g, s5_a_re=m_s5_a_re, s5_a_im=m_s5_a_im,
               s5_log_dt=m_s5_log_dt, s5_b_re=m_s5_b_re, s5_b_im=m_s5_b_im, s5_c_re=m_s5_c_re, s5_c_im=m_s5_c_im,
               s5_d=m_s5_d, s5_glu_w=m_s5_glu_w, s5_glu_b=m_s5_glu_b, w_out=m_w_out, norm2_g=m_norm2_g,
               norm_mem_g=m_norm_mem_g, xa_wq=m_xa_wq, xa_wk=m_xa_wk, xa_wv=m_xa_wv, xa_wo=m_xa_wo,
               norm_f_g=m_norm_f_g)
    var = dict(norm1_g=v_norm1_g, w_in=v_w_in, ret_gn_g=v_ret_gn_g, s5_a_re=v_s5_a_re, s5_a_im=v_s5_a_im,
               s5_log_dt=v_s5_log_dt, s5_b_re=v_s5_b_re, s5_b_im=v_s5_b_im, s5_c_re=v_s5_c_re, s5_c_im=v_s5_c_im,
               s5_d=v_s5_d, s5_glu_w=v_s5_glu_w, s5_glu_b=v_s5_glu_b, w_out=v_w_out, norm2_g=v_norm2_g,
               norm_mem_g=v_norm_mem_g, xa_wq=v_xa_wq, xa_wk=v_xa_wk, xa_wv=v_xa_wv, xa_wo=v_xa_wo,
               norm_f_g=v_norm_f_g)
    shapes = {n: w[n].shape for n in _W_NAMES}

    x2d, mem2d, tgt = x[0], mem[0], loss_target[0]
    l = x2d.shape[0]
    ret_c = _pick(l, (256, 128))
    s5_t = _pick(l, (256, 128))
    g1, g2, gm, gf = norm1_g, norm2_g, norm_mem_g, norm_f_g.reshape(1, D_MODEL)

    win_s, *row_shards_b = _allgather_w_in(w_in[0], [w[n][0] for n in _ROW_NAMES])

    to_gpn = lambda b: jnp.transpose(b, (0, 2, 1)).reshape(S5_G, S5_P * S5_N)
    from_gpn = lambda b: jnp.transpose(b.reshape(S5_G, S5_P, S5_N), (0, 2, 1))
    disc_args = (s5_a_re[0], s5_a_im[0], s5_log_dt[0].reshape(S5_G, 1), to_gpn(s5_b_re[0]), to_gpn(s5_b_im[0]))
    abar_re, abar_im, bb_re_t, bb_im_t = _s5_discretize(*disc_args)
    bbm, ccm = _s5_block_mats(from_gpn(bb_re_t), from_gpn(bb_im_t), s5_c_re[0], s5_c_im[0])
    a_z = _s5_z(abar_re, abar_im)

    h1 = _rms_fwd("norm1_fwd", x2d, g1)
    proj, *rows_01 = _mm_nn_slots("in_proj", h1, win_s, BF16, side=_gather_job(row_shards_b[:2]))
    full = {n: g.reshape(N_DEV * r, D_MODEL) for n, g, r in zip(_ROW_NAMES[:2], rows_01, ROW_SHARDS[:2])}
    half = RET_DK // 2
    inv = ROPE_BASE ** (-jnp.arange(half, dtype=F32) / half)
    cos_t, sin_t = _rope_tables(positions[0].reshape(l, 1), jnp.tile(inv, 128 // half)[None, :])
    rconsts = _ret_constants(ret_c)
    ret, o_saved, r_prev, q_rot, k_rot = _ret_fwd(proj, cos_t, sin_t, rconsts, ret_gn_g, ret_c)
    mix, xstart, *rows_xa = _s5_fwd(proj, ret, bbm, ccm, s5_d, full["s5_glu_w"], s5_glu_b, a_z, s5_t,
                                    side=_gather_job(row_shards_b[2:]))
    full.update({n: g.reshape(N_DEV * r, D_MODEL) for n, g, r in zip(_ROW_NAMES[2:], rows_xa, ROW_SHARDS[2:])})
    x1 = _mm_nn("out_proj", mix, full["w_out"], F32, residual=x2d)
    h2 = _rms_fwd("norm2_fwd", x1, g2)
    mn = _rms_fwd("norm_mem_fwd", mem2d, gm)
    qa = _mm_nn("xa_q", h2, full["xa_wq"], BF16)
    ka = _mm_nn("xa_k", mn, full["xa_wk"], BF16)
    va = _mm_nn("xa_v", mn, full["xa_wv"], BF16)
    oa = _attn_fwd(qa, ka, va)
    x2 = _mm_nn("xa_o", oa, full["xa_wo"], F32, residual=x1)
    dx2, dgf, loss_lanes = _loss_head(x2, gf, tgt)

    doa = _mm_nt("xa_o_dx", dx2, full["xa_wo"], BF16)
    dwo = _mm_tn("xa_o_dw", oa, dx2, BF16)
    dqa, dka, dva = _attn_bwd(qa, ka, va, doa)
    dh2 = _mm_nt("xa_q_dx", dqa, full["xa_wq"], F32)
    dwq = _mm_tn("xa_q_dw", h2, dqa, BF16)
    dx1, dg2 = _rms_bwd("norm2_bwd", x1, g2, dh2, dx2)
    dwk = _mm_tn("xa_k_dw", mn, dka, BF16)
    dwv = _mm_tn("xa_v_dw", mn, dva, BF16)
    dmn = _mm_nt("xa_v_dx", dva, full["xa_wv"], F32, residual=_mm_nt("xa_k_dx", dka, full["xa_wk"], F32))
    _, dgm = _rms_bwd("norm_mem_bwd", mem2d, gm, dmn, None)
    dmix = _mm_nt("out_proj_dx", dx1, full["w_out"], BF16)
    dwout = _mm_tn("out_proj_dw", mix, dx1, BF16)
    dret, dgn, *got_a = _ret_bwd(proj, q_rot, k_rot, cos_t, sin_t, rconsts, ret_gn_g, o_saved, r_prev, dmix, ret_c,
                                 side=_scatter_job([dwout, dwq, dwk, dwv, dwo]))
    dug, y2, dz, dbbm, dccm, dabar, dd, dgb = _s5_bwd(proj, dmix, xstart, bbm, ccm, s5_d, full["s5_glu_w"], s5_glu_b,
                                                      a_z, s5_t)
    dglu = _mm_tn("s5_glu_dw", y2, dz, BF16)
    dproj = jnp.concatenate([dret, dug], axis=1)
    dwin_s, got_glu = _mm_tn_slots("in_proj_dw", h1, dproj, N_DEV, BF16, side=_scatter_job([dglu]))
    dh1, got_win = _mm_nt_slots("in_proj_dx", dproj, win_s, F32, side=_scatter_job([dwin_s]))
    grad_x, dg1 = _rms_bwd("norm1_bwd", x2d, g1, dh1, dx1)

    dab_re, dab_im = _s5_unz(dabar)
    dbb_re, dbb_im = _s5_block_diag_bb(dbbm)
    dc_re, dc_im = _s5_block_diag_cc(dccm)
    da_re, da_im, dlog_dt, db_re_t, db_im_t = _s5_discretize_bwd(*disc_args, dab_re, dab_im, to_gpn(dbb_re),
                                                                 to_gpn(dbb_im))
    db_re, db_im = from_gpn(db_re_t), from_gpn(db_im_t)
    small_g = dict(norm1_g=dg1, ret_gn_g=dgn, s5_d=dd, s5_glu_b=dgb, norm2_g=dg2, norm_mem_g=dgm, norm_f_g=dgf,
                   s5_a_re=da_re, s5_a_im=da_im, s5_log_dt=dlog_dt, s5_b_re=db_re, s5_b_im=db_im, s5_c_re=dc_re,
                   s5_c_im=dc_im)
    small_pack = _pack_small(small_g, loss_row=loss_lanes)

    res = {}
    got = dict(zip(("w_out", "xa_wq", "xa_wk", "xa_wv", "xa_wo"), got_a), w_in=got_win, s5_glu_w=got_glu)
    for n in ("w_in",) + _ROW_NAMES:
        res[n] = _adamw("adamw_" + n, got[n], w[n][0], mom[n][0], var[n][0])
    small_sum = _allreduce_small(small_pack)
    small_res, loss_sum = _adamw_small(small_sum, _pack_small(w), _pack_small(mom), _pack_small(var))
    loss = (0.5 / D_MODEL) * jnp.sum(loss_sum)
    res.update(small_res)

    outs = [loss, grad_x[None]]
    for part in range(4):
        for n in _W_NAMES:
            outs.append(res[n][part].reshape(shapes[n]))
    return tuple(outs)
```

```python
import functools

import jax
import jax.numpy as jnp
from jax import lax
from jax.experimental import pallas as pl
from jax.experimental.pallas import tpu as pltpu

F32 = jnp.float32
BF16 = jnp.bfloat16
MESH = pl.DeviceIdType.MESH

D_MODEL = 1024
RET_HEADS, RET_DK, RET_DV = 8, 64, 128
RET_QK = RET_HEADS * RET_DK
S5_G, S5_N, S5_P = 64, 64, 16
S5_NB = 8
S5_GB = S5_G // S5_NB
S5_BS = S5_GB * S5_N
S5_COLS = 2 * S5_G * S5_N
XA_HEADS, XA_DH = 4, 256
EPS = 1e-6
ROPE_BASE = 10000.0
N_DEV = 8
W_IN_SHARD = 640
ROW_SHARDS = (128, 256, 128, 128, 128, 128)
ROWPACK = sum(ROW_SHARDS)
SMALL_ROWS = 320
ADAM_LR, ADAM_B1, ADAM_B2, ADAM_EPS, ADAM_WD, ADAM_STEP = 0.001, 0.9, 0.999, 1e-08, 0.01, 10

VMEM_LIMIT = 56 * 1024 * 1024


def _cp(*sem):
    return pltpu.CompilerParams(dimension_semantics=tuple(sem), vmem_limit_bytes=VMEM_LIMIT)


def _dot(a, b):
    return jnp.dot(a, b, preferred_element_type=F32)


def _dot_nt(a, b):
    return lax.dot_general(a, b, (((1,), (1,)), ((), ())), preferred_element_type=F32)


def _dot_tn(a, b):
    return lax.dot_general(a, b, (((0,), (0,)), ((), ())), preferred_element_type=F32)


def _sigmoid(x):
    return 1.0 / (1.0 + jnp.exp(-x))


def _silu(x):
    return x * _sigmoid(x)


def _dsilu(x):
    s = _sigmoid(x)
    return s * (1.0 + x * (1.0 - s))


_GELU_C = 0.7978845608028654


def _gelu(x):
    return 0.5 * x * (1.0 + jnp.tanh(_GELU_C * (x + 0.044715 * (x * x * x))))


def _dgelu(x):
    t = jnp.tanh(_GELU_C * (x + 0.044715 * (x * x * x)))
    return 0.5 * (1.0 + t) + 0.5 * x * (1.0 - t * t) * (_GELU_C * (1.0 + 3.0 * 0.044715 * (x * x)))


def _pick(n, cands):
    for c in cands:
        if n % c == 0:
            return c
    return n


class _Epilogue:
    def __init__(self, rows, vecs, row_out_dtypes, n_sums, fn):
        self.rows, self.vecs, self.row_out_dtypes, self.n_sums, self.fn = list(rows), list(vecs), list(row_out_dtypes), n_sums, fn


def _rms(x):
    rs = lax.rsqrt(jnp.mean(x * x, axis=-1, keepdims=True) + EPS)
    return rs, x * rs


def _rms_dx(dn, xn, rs):
    return rs * (dn - xn * jnp.mean(dn * xn, axis=-1, keepdims=True))


def _epi_norm_fwd(g):
    def fn(r, rows, vecs):
        return r, [_rms(r)[1] * vecs[0]], []

    return _Epilogue([], [g], [BF16], 0, fn)


def _epi_loss(gf, target):
    def fn(r, rows, vecs):
        rs, xn = _rms(r)
        e = xn * vecs[0] - rows[0]
        dy = e * (1.0 / r.shape[-1])
        return (_rms_dx(dy * vecs[0], xn, rs), [],
                [jnp.sum(dy * xn, axis=0, keepdims=True), jnp.sum(e * e, axis=0, keepdims=True)])

    return _Epilogue([target], [gf], [], 2, fn)


def _epi_norm_bwd(x, g, dres):
    def fn(r, rows, vecs):
        rs, xn = _rms(rows[0])
        return _rms_dx(r * vecs[0], xn, rs) + rows[1], [], [jnp.sum(r * xn, axis=0, keepdims=True)]

    return _Epilogue([x, dres], [g], [], 1, fn)


def _mm_core(name, operands, in_specs, out_spec, out_shape, grid, nk, dims, acc_shape, has_res, side=None, epi=None):
    n_in = 3 if has_res else 2
    n_epi_in = len(epi.rows) + len(epi.vecs) if epi else 0
    n_epi_out = len(epi.row_out_dtypes) + epi.n_sums if epi else 0
    n_side_in = len(side.srcs) if side else 0
    n_side_out = side.n if side else 0

    def body(*refs):
        a_ref, b_ref = refs[0], refs[1]
        r_ref = refs[2] if has_res else None
        epi_in = refs[n_in:n_in + n_epi_in]
        side_in = refs[n_in + n_epi_in:n_in + n_epi_in + n_side_in]
        n0 = n_in + n_epi_in + n_side_in
        o_ref = refs[n0]
        epi_out = refs[n0 + 1:n0 + 1 + n_epi_out]
        side_out = refs[n0 + 1 + n_epi_out:n0 + 1 + n_epi_out + n_side_out]
        rest = refs[n0 + 1 + n_epi_out + n_side_out:]
        acc, sems = (rest[0], rest[1:]) if nk > 1 else (None, rest)
        i, j, k = pl.program_id(0), pl.program_id(1), pl.program_id(2)
        if side:
            @pl.when((i == 0) & (j == 0) & (k == 0))
            def _():
                side.start(side_in, side_out, sems)

        def product():
            return lax.dot_general(a_ref[...].astype(BF16), b_ref[...].astype(BF16), (dims, ((), ())),
                                   preferred_element_type=F32)

        def finish(r):
            if has_res:
                r = r + r_ref[...]
            if epi is None:
                o_ref[...] = r.astype(o_ref.dtype)
                return
            n_rows = len(epi.rows)
            main, row_vals, sums = epi.fn(r, [t[...] for t in epi_in[:n_rows]], [t[...] for t in epi_in[n_rows:]])
            o_ref[...] = main.astype(o_ref.dtype)
            for ref, val in zip(epi_out, row_vals):
                ref[...] = val.astype(ref.dtype)
            for ref, val in zip(epi_out[len(row_vals):], sums):
                @pl.when(i == 0)
                def _(ref=ref):
                    ref[...] = jnp.zeros_like(ref)

                ref[...] += val

        if nk == 1:
            finish(product())
        else:
            @pl.when(k == 0)
            def _():
                acc[...] = jnp.zeros_like(acc)

            acc[...] += product()

            @pl.when(k == nk - 1)
            def _():
                finish(acc[...])

        if side:
            @pl.when((i == grid[0] - 1) & (j == grid[1] - 1) & (k == grid[2] - 1))
            def _():
                side.wait(side_in, side_out, sems)

    acc_scratch = [pltpu.VMEM(acc_shape, F32)] if nk > 1 else []
    in_specs, out_specs, out_shapes, operands = list(in_specs), [out_spec], [out_shape], list(operands)
    if epi:
        assert grid[1] == 1, "an epilogue needs tiles that span whole rows"
        tm, n = out_spec.block_shape
        row_spec = pl.BlockSpec((tm, n), lambda i, j, k: (i, 0))
        vec_spec = pl.BlockSpec((1, n), lambda i, j, k: (0, 0))
        in_specs += [row_spec] * len(epi.rows) + [vec_spec] * len(epi.vecs)
        operands += epi.rows + epi.vecs
        out_specs += [row_spec] * len(epi.row_out_dtypes) + [vec_spec] * epi.n_sums
        out_shapes += [jax.ShapeDtypeStruct(out_shape.shape, d) for d in epi.row_out_dtypes]
        out_shapes += [jax.ShapeDtypeStruct((1, n), F32)] * epi.n_sums
    scratch = acc_scratch
    if side:
        in_specs += side.in_specs
        operands += side.srcs
        out_specs += side.out_specs
        out_shapes += side.landing
        scratch = acc_scratch + side.scratch
    plain = side is None and epi is None
    res = pl.pallas_call(
        body, name=name, grid=grid, in_specs=in_specs, out_specs=out_specs[0] if plain else tuple(out_specs),
        out_shape=out_shapes[0] if plain else tuple(out_shapes), scratch_shapes=scratch,
        compiler_params=_cp("parallel", "parallel", "arbitrary") if plain else _cp("arbitrary", "arbitrary", "arbitrary"),
    )(*operands)
    return res


def _mm_nn(name, a, b, out_dtype, residual=None, epi=None):
    m, kk = a.shape
    n = b.shape[1]
    tm, tn, tk = _pick(m, (1024, 512, 256)), _pick(n, (1024, 512)), _pick(kk, (1024, 512))
    ops = [a, b]
    specs = [pl.BlockSpec((tm, tk), lambda i, j, k: (i, k)), pl.BlockSpec((tk, tn), lambda i, j, k: (k, j))]
    if residual is not None:
        ops.append(residual)
        specs.append(pl.BlockSpec((tm, tn), lambda i, j, k: (i, j)))
    return _mm_core(name, ops, specs, pl.BlockSpec((tm, tn), lambda i, j, k: (i, j)),
                    jax.ShapeDtypeStruct((m, n), out_dtype), (m // tm, n // tn, kk // tk), kk // tk,
                    ((1,), (0,)), (tm, tn), residual is not None, epi=epi)


def _mm_nt(name, a, b, out_dtype, residual=None, epi=None):
    m, kk = a.shape
    n = b.shape[0]
    tm, tn, tk = _pick(m, (1024, 512, 256)), _pick(n, (1024, 512)), _pick(kk, (1024, 512))
    ops = [a, b]
    specs = [pl.BlockSpec((tm, tk), lambda i, j, k: (i, k)), pl.BlockSpec((tn, tk), lambda i, j, k: (j, k))]
    if residual is not None:
        ops.append(residual)
        specs.append(pl.BlockSpec((tm, tn), lambda i, j, k: (i, j)))
    return _mm_core(name, ops, specs, pl.BlockSpec((tm, tn), lambda i, j, k: (i, j)),
                    jax.ShapeDtypeStruct((m, n), out_dtype), (m // tm, n // tn, kk // tk), kk // tk,
                    ((1,), (1,)), (tm, tn), residual is not None, epi=epi)


def _mm_tn(name, a, b, out_dtype):
    kk, m = a.shape
    n = b.shape[1]
    tm, tn, tk = _pick(m, (1024, 512)), _pick(n, (1024, 512)), _pick(kk, (1024, 512, 256))
    specs = [pl.BlockSpec((tk, tm), lambda i, j, k: (k, i)), pl.BlockSpec((tk, tn), lambda i, j, k: (k, j))]
    return _mm_core(name, [a, b], specs, pl.BlockSpec((tm, tn), lambda i, j, k: (i, j)),
                    jax.ShapeDtypeStruct((m, n), out_dtype), (m // tm, n // tn, kk // tk), kk // tk,
                    ((0,), (0,)), (tm, tn), False)


def _mm_nn_slots(name, a, b_slots, out_dtype, side=None):
    m, kk = a.shape
    s, _, ns = b_slots.shape
    tm, tk = _pick(m, (1024, 512, 256)), _pick(kk, (1024, 512))
    specs = [pl.BlockSpec((tm, tk), lambda i, j, k: (i, k)), pl.BlockSpec((None, tk, ns), lambda i, j, k: (j, k, 0))]
    return _mm_core(name, [a, b_slots], specs, pl.BlockSpec((tm, ns), lambda i, j, k: (i, j)),
                    jax.ShapeDtypeStruct((m, s * ns), out_dtype), (m // tm, s, kk // tk), kk // tk,
                    ((1,), (0,)), (tm, ns), False, side)


def _mm_nt_slots(name, a, b_slots, out_dtype, side=None, epi=None):
    m = a.shape[0]
    s, n, ns = b_slots.shape
    tm, tn = _pick(m, (1024, 512, 256)), _pick(n, (1024, 512))
    specs = [pl.BlockSpec((tm, ns), lambda i, j, k: (i, k)), pl.BlockSpec((None, tn, ns), lambda i, j, k: (k, j, 0))]
    return _mm_core(name, [a, b_slots], specs, pl.BlockSpec((tm, tn), lambda i, j, k: (i, j)),
                    jax.ShapeDtypeStruct((m, n), out_dtype), (m // tm, n // tn, s), s,
                    ((1,), (1,)), (tm, tn), False, side, epi)


def _mm_tn_slots(name, a, b, s, out_dtype, side=None):
    kk, m = a.shape
    ns = b.shape[1] // s
    tm, tk = _pick(m, (1024, 512)), _pick(kk, (1024, 512, 256))
    specs = [pl.BlockSpec((tk, tm), lambda i, j, k: (k, i)), pl.BlockSpec((tk, ns), lambda i, j, k: (k, j))]
    return _mm_core(name, [a, b], specs, pl.BlockSpec((None, tm, ns), lambda i, j, k: (j, i, 0)),
                    jax.ShapeDtypeStruct((s, m, ns), out_dtype), (m // tm, s, kk // tk), kk // tk,
                    ((0,), (0,)), (tm, ns), False, side)


def _rms_fwd(name, x, g):
    r, d = x.shape
    tr = _pick(r, (1024, 512, 256))

    def body(x_ref, g_ref, o_ref):
        xv = x_ref[...]
        rs = lax.rsqrt(jnp.mean(xv * xv, axis=-1, keepdims=True) + EPS)
        o_ref[...] = (xv * rs * g_ref[...]).astype(o_ref.dtype)

    return pl.pallas_call(
        body, name=name, grid=(r // tr,),
        in_specs=[pl.BlockSpec((tr, d), lambda i: (i, 0)), pl.BlockSpec((1, d), lambda i: (0, 0))],
        out_specs=pl.BlockSpec((tr, d), lambda i: (i, 0)),
        out_shape=jax.ShapeDtypeStruct((r, d), BF16), compiler_params=_cp("parallel"),
    )(x, g)


def _rms_bwd(name, x, g, dh, dres):
    r, d = x.shape
    tr = _pick(r, (512, 256))
    has_res = dres is not None

    def body(*refs):
        if has_res:
            x_ref, g_ref, dh_ref, dr_ref, dx_ref, dg_ref = refs
        else:
            x_ref, g_ref, dh_ref, dx_ref, dg_ref = refs
        i = pl.program_id(0)

        @pl.when(i == 0)
        def _():
            dg_ref[...] = jnp.zeros_like(dg_ref)

        xv = x_ref[...]
        dhv = dh_ref[...].astype(F32)
        rs = lax.rsqrt(jnp.mean(xv * xv, axis=-1, keepdims=True) + EPS)
        xn = xv * rs
        dg_ref[...] += jnp.sum(dhv * xn, axis=0, keepdims=True)
        dn = dhv * g_ref[...]
        dx = rs * (dn - xn * jnp.mean(dn * xn, axis=-1, keepdims=True))
        if has_res:
            dx = dx + dr_ref[...]
        dx_ref[...] = dx

    row = pl.BlockSpec((tr, d), lambda i: (i, 0))
    vec = pl.BlockSpec((1, d), lambda i: (0, 0))
    ops = [x, g, dh] + ([dres] if has_res else [])
    return pl.pallas_call(
        body, name=name, grid=(r // tr,),
        in_specs=[row, vec, row] + ([row] if has_res else []),
        out_specs=(row, vec),
        out_shape=(jax.ShapeDtypeStruct((r, d), F32), jax.ShapeDtypeStruct((1, d), F32)),
        compiler_params=_cp("arbitrary"),
    )(*ops)


def _loss_head(x2, gf, target):
    r, d = x2.shape
    tr = _pick(r, (512, 256))

    def body(x_ref, g_ref, t_ref, dx_ref, dg_ref, ls_ref):
        i = pl.program_id(0)

        @pl.when(i == 0)
        def _():
            dg_ref[...] = jnp.zeros_like(dg_ref)
            ls_ref[...] = jnp.zeros_like(ls_ref)

        xv = x_ref[...]
        rs = lax.rsqrt(jnp.mean(xv * xv, axis=-1, keepdims=True) + EPS)
        xn = xv * rs
        e = xn * g_ref[...] - t_ref[...]
        ls_ref[...] += jnp.sum(e * e, axis=0, keepdims=True)
        dy = e * (1.0 / d)
        dg_ref[...] += jnp.sum(dy * xn, axis=0, keepdims=True)
        dn = dy * g_ref[...]
        dx_ref[...] = rs * (dn - xn * jnp.mean(dn * xn, axis=-1, keepdims=True))

    row = pl.BlockSpec((tr, d), lambda i: (i, 0))
    vec = pl.BlockSpec((1, d), lambda i: (0, 0))
    return pl.pallas_call(
        body, name="loss_head", grid=(r // tr,), in_specs=[row, vec, row], out_specs=(row, vec, vec),
        out_shape=(jax.ShapeDtypeStruct((r, d), F32), jax.ShapeDtypeStruct((1, d), F32),
                   jax.ShapeDtypeStruct((1, d), F32)),
        compiler_params=_cp("arbitrary"),
    )(x2, gf, target)


def _rope_tables(pos_col, inv_row):
    l = pos_col.shape[0]
    tl = _pick(l, (1024, 512, 256))

    def body(p_ref, inv_ref, cos_ref, sin_ref):
        ang = p_ref[...].astype(F32) * inv_ref[...]
        lane = lax.broadcasted_iota(jnp.int32, ang.shape, 1)
        c = jnp.cos(ang)
        s = jnp.where((lane % RET_DK) < RET_DK // 2, -jnp.sin(ang), jnp.sin(ang))
        cos_ref[...] = jnp.tile(c, (1, RET_QK // 128))
        sin_ref[...] = jnp.tile(s, (1, RET_QK // 128))

    return pl.pallas_call(
        body, name="rope_tables", grid=(l // tl,),
        in_specs=[pl.BlockSpec((tl, 1), lambda i: (i, 0)), pl.BlockSpec((1, 128), lambda i: (0, 0))],
        out_specs=(pl.BlockSpec((tl, RET_QK), lambda i: (i, 0)), pl.BlockSpec((tl, RET_QK), lambda i: (i, 0))),
        out_shape=(jax.ShapeDtypeStruct((l, RET_QK), F32), jax.ShapeDtypeStruct((l, RET_QK), F32)),
        compiler_params=_cp("parallel"),
    )(pos_col, inv_row)


def _rot(x, cos_t, sin_t):
    n = x.shape[-1]
    lane = lax.broadcasted_iota(jnp.int32, x.shape, 1)
    partner = jnp.where((lane % RET_DK) < RET_DK // 2, pltpu.roll(x, n - RET_DK // 2, 1), pltpu.roll(x, RET_DK // 2, 1))
    return x * cos_t + partner * sin_t


def _ret_constants(c):
    log_g = jnp.log1p(-jnp.exp2(-5.0 - jnp.arange(RET_HEADS, dtype=F32)))
    j = jnp.arange(c, dtype=F32)
    diff = j[:, None] - j[None, :]
    decay = jnp.where(diff[None] >= 0.0, jnp.exp(log_g[:, None, None] * jnp.maximum(diff, 0.0)[None]), 0.0)
    q_w = jnp.exp(log_g[None, :] * (j + 1.0)[:, None])
    k_w = jnp.exp(log_g[None, :] * (c - 1.0 - j)[:, None])
    cd = jnp.exp(log_g * c)
    rep = lambda t: jnp.repeat(t, RET_DK, axis=1)
    cd_row = jnp.repeat(cd, RET_DV)[None, :]
    return decay, rep(q_w), rep(k_w), cd_row


def _pair_of(h, c):
    lane = lax.broadcasted_iota(jnp.int32, (c, 2 * RET_DK), 1)
    mine = (lane < RET_DK) if h % 2 == 0 else (lane >= RET_DK)
    return slice((h // 2) * 2 * RET_DK, (h // 2 + 1) * 2 * RET_DK), mine


def _keep(x, mine):
    return jnp.where(mine, x, jnp.zeros_like(x))


def _ret_fwd(proj, cos_t, sin_t, consts, gn_g, c):
    l = proj.shape[0]
    nc = l // c
    decay, qw, kw, cd_row = consts

    def body(q_ref, k_ref, v_ref, g_ref, cos_ref, sin_ref, dec_ref, qw_ref, kw_ref, cd_ref, gn_ref,
             ret_ref, o_ref, rp_ref, qb_ref, kb_ref, state):
        @pl.when(pl.program_id(0) == 0)
        def _():
            state[...] = jnp.zeros_like(state)

        cs, sn = cos_ref[...], sin_ref[...]
        qr = _rot(q_ref[...].astype(F32), cs, sn)
        kr = _rot(k_ref[...].astype(F32), cs, sn) * (RET_DK ** -0.5)
        qb, kb = qr.astype(BF16), kr.astype(BF16)
        qb_ref[...] = qb
        kb_ref[...] = kb
        qwb = (qr * qw_ref[...]).astype(BF16)
        kwb = (kr * kw_ref[...]).astype(BF16)
        vb = v_ref[...].astype(BF16)
        for h in range(RET_HEADS):
            ps, mine = _pair_of(h, c)
            vs = slice(h * RET_DV, (h + 1) * RET_DV)
            s = _dot_nt(_keep(qb[:, ps], mine), kb[:, ps]) * dec_ref[h]
            r_prev = state[h]
            rp_ref[0, h] = r_prev
            o = _dot(s.astype(BF16), vb[:, vs]) + _dot(_keep(qwb[:, ps], mine), r_prev.astype(BF16))
            state[h] = cd_ref[:, vs] * r_prev + _dot_tn(_keep(kwb[:, ps], mine), vb[:, vs])
            o_ref[:, vs] = o
            mu = jnp.mean(o, axis=-1, keepdims=True)
            var = jnp.mean(jnp.square(o - mu), axis=-1, keepdims=True)
            on = (o - mu) * lax.rsqrt(var + EPS)
            ret_ref[:, vs] = (on * gn_ref[:, vs] * _silu(g_ref[:, vs].astype(F32))).astype(ret_ref.dtype)

    const2 = lambda shape: pl.BlockSpec(shape, lambda i: (0,) * len(shape))
    return pl.pallas_call(
        body, name="retention_fwd", grid=(nc,),
        in_specs=[pl.BlockSpec((c, RET_QK), lambda i: (i, 0)), pl.BlockSpec((c, RET_QK), lambda i: (i, 1)),
                  pl.BlockSpec((c, D_MODEL), lambda i: (i, 1)), pl.BlockSpec((c, D_MODEL), lambda i: (i, 2)),
                  pl.BlockSpec((c, RET_QK), lambda i: (i, 0)), pl.BlockSpec((c, RET_QK), lambda i: (i, 0)),
                  const2((RET_HEADS, c, c)), const2((c, RET_QK)), const2((c, RET_QK)), const2((1, D_MODEL)),
                  const2((1, D_MODEL))],
        out_specs=(pl.BlockSpec((c, D_MODEL), lambda i: (i, 0)), pl.BlockSpec((c, D_MODEL), lambda i: (i, 0)),
                   pl.BlockSpec((1, RET_HEADS, 2 * RET_DK, RET_DV), lambda i: (i, 0, 0, 0)),
                   pl.BlockSpec((c, RET_QK), lambda i: (i, 0)), pl.BlockSpec((c, RET_QK), lambda i: (i, 0))),
        out_shape=(jax.ShapeDtypeStruct((l, 2 * D_MODEL), BF16), jax.ShapeDtypeStruct((l, D_MODEL), F32),
                   jax.ShapeDtypeStruct((nc, RET_HEADS, 2 * RET_DK, RET_DV), F32),
                   jax.ShapeDtypeStruct((l, RET_QK), BF16), jax.ShapeDtypeStruct((l, RET_QK), BF16)),
        scratch_shapes=[pltpu.VMEM((RET_HEADS, 2 * RET_DK, RET_DV), F32)],
        compiler_params=_cp("arbitrary"),
    )(proj, proj, proj, proj, cos_t, sin_t, decay, qw, kw, cd_row, gn_g)


def _ret_bwd(proj, qb_saved, kb_saved, cos_t, sin_t, consts, gn_g, o_saved, r_prev_saved, dmix, c, side):
    l = proj.shape[0]
    nc = l // c
    decay, qw, kw, cd_row = consts
    n_in = 14

    def body(*refs):
        (q_ref, k_ref, v_ref, g_ref, cos_ref, sin_ref, dec_ref, qw_ref, kw_ref, cd_ref, gn_ref, o_ref, rp_ref,
         dr_ref) = refs[:n_in]
        side_in = refs[n_in:n_in + len(side.srcs)]
        out_ref, dgn_ref = refs[n_in + len(side.srcs):n_in + len(side.srcs) + 2]
        side_out = refs[n_in + len(side.srcs) + 2:n_in + len(side.srcs) + 2 + side.n]
        state, dq_s, dk_s = refs[n_in + len(side.srcs) + 2 + side.n:n_in + len(side.srcs) + 5 + side.n]
        sems = refs[n_in + len(side.srcs) + 5 + side.n:]

        @pl.when(pl.program_id(0) == 0)
        def _():
            side.start(side_in, side_out, sems)
            state[...] = jnp.zeros_like(state)
            dgn_ref[...] = jnp.zeros_like(dgn_ref)

        cs, sn = cos_ref[...], sin_ref[...]
        qb, kb = q_ref[...], k_ref[...]
        qwv, kwv = qw_ref[...], kw_ref[...]
        qwb = (qb.astype(F32) * qwv).astype(BF16)
        kwb = (kb.astype(F32) * kwv).astype(BF16)
        vb = v_ref[...].astype(BF16)
        dq2 = dk2 = None
        for h in range(RET_HEADS):
            ps, mine = _pair_of(h, c)
            vs = slice(h * RET_DV, (h + 1) * RET_DV)
            dec = dec_ref[h]
            qm, km = _keep(qb[:, ps], mine), _keep(kb[:, ps], mine)
            o = o_ref[:, vs]
            mu = jnp.mean(o, axis=-1, keepdims=True)
            var = jnp.mean(jnp.square(o - mu), axis=-1, keepdims=True)
            rstd = lax.rsqrt(var + EPS)
            on = (o - mu) * rstd
            gate = g_ref[:, vs].astype(F32)
            sg = _silu(gate)
            dret = dr_ref[:, vs].astype(F32)
            gn = gn_ref[:, vs]
            dgn_ref[:, vs] += jnp.sum(dret * on * sg, axis=0, keepdims=True)
            out_ref[:, 2 * RET_QK + D_MODEL + h * RET_DV:2 * RET_QK + D_MODEL + (h + 1) * RET_DV] = (
                dret * on * gn * _dsilu(gate)).astype(out_ref.dtype)
            don = dret * gn * sg
            do = rstd * (don - jnp.mean(don, axis=-1, keepdims=True)
                         - on * jnp.mean(don * on, axis=-1, keepdims=True))
            dob = do.astype(BF16)
            sn_h = state[h]
            snb = sn_h.astype(BF16)
            s = _dot_nt(qm, kb[:, ps]) * dec
            dv = _dot_tn(s.astype(BF16), dob) + _dot(_keep(kwb[:, ps], mine), snb)
            out_ref[:, 2 * RET_QK + h * RET_DV:2 * RET_QK + (h + 1) * RET_DV] = dv.astype(out_ref.dtype)
            ds = (_dot_nt(dob, vb[:, vs]) * dec).astype(BF16)
            dq_h = _dot(ds, km) + qwv[:, ps] * _dot_nt(dob, rp_ref[0, h].astype(BF16))
            dk_h = _dot_tn(ds, qm) + kwv[:, ps] * _dot_nt(vb[:, vs], snb)
            state[h] = cd_ref[:, vs] * sn_h + _dot_tn(_keep(qwb[:, ps], mine), dob)
            if h % 2 == 0:
                dq2, dk2 = dq_h, dk_h
            else:
                dq_s[:, ps] = dq2 + dq_h
                dk_s[:, ps] = dk2 + dk_h
        out_ref[:, 0:RET_QK] = _rot(dq_s[...], cs, -sn).astype(out_ref.dtype)
        out_ref[:, RET_QK:2 * RET_QK] = (_rot(dk_s[...], cs, -sn) * (RET_DK ** -0.5)).astype(out_ref.dtype)

        @pl.when(pl.program_id(0) == nc - 1)
        def _():
            side.wait(side_in, side_out, sems)

    rev = lambda i: nc - 1 - i
    const2 = lambda shape: pl.BlockSpec(shape, lambda i: (0,) * len(shape))
    return pl.pallas_call(
        body, name="retention_bwd", grid=(nc,),
        in_specs=[pl.BlockSpec((c, RET_QK), lambda i: (rev(i), 0)), pl.BlockSpec((c, RET_QK), lambda i: (rev(i), 0)),
                  pl.BlockSpec((c, D_MODEL), lambda i: (rev(i), 1)), pl.BlockSpec((c, D_MODEL), lambda i: (rev(i), 2)),
                  pl.BlockSpec((c, RET_QK), lambda i: (rev(i), 0)), pl.BlockSpec((c, RET_QK), lambda i: (rev(i), 0)),
                  const2((RET_HEADS, c, c)), const2((c, RET_QK)), const2((c, RET_QK)), const2((1, D_MODEL)),
                  const2((1, D_MODEL)),
                  pl.BlockSpec((c, D_MODEL), lambda i: (rev(i), 0)),
                  pl.BlockSpec((1, RET_HEADS, 2 * RET_DK, RET_DV), lambda i: (rev(i), 0, 0, 0)),
                  pl.BlockSpec((c, D_MODEL), lambda i: (rev(i), 0))] + side.in_specs,
        out_specs=(pl.BlockSpec((c, 2 * RET_QK + 2 * D_MODEL), lambda i: (rev(i), 0)), const2((1, D_MODEL)),
                   *side.out_specs),
        out_shape=(jax.ShapeDtypeStruct((l, 2 * RET_QK + 2 * D_MODEL), BF16), jax.ShapeDtypeStruct((1, D_MODEL), F32),
                   *side.landing),
        scratch_shapes=[pltpu.VMEM((RET_HEADS, 2 * RET_DK, RET_DV), F32), pltpu.VMEM((c, RET_QK), F32),
                        pltpu.VMEM((c, RET_QK), F32)] + side.scratch,
        compiler_params=_cp("arbitrary"),
    )(qb_saved, kb_saved, proj, proj, cos_t, sin_t, decay, qw, kw, cd_row, gn_g, o_saved, r_prev_saved, dmix,
      *side.srcs)


def _zoh(a_re, a_im, log_dt):
    dt = jnp.exp(log_dt)
    mag = jnp.exp(a_re * dt)
    abar_re = mag * jnp.cos(a_im * dt)
    abar_im = mag * jnp.sin(a_im * dt)
    den = a_re * a_re + a_im * a_im
    nr, ni = abar_re - 1.0, abar_im
    f_re = (nr * a_re + ni * a_im) / den
    f_im = (ni * a_re - nr * a_im) / den
    return dt, abar_re, abar_im, f_re, f_im, den


def _lanes_p(f):
    return jnp.tile(f, (1, S5_P))


def _s5_discretize(a_re, a_im, log_dt, b_re_t, b_im_t):
    def body(ar_ref, ai_ref, ld_ref, br_ref, bi_ref, abr_ref, abi_ref, bbr_ref, bbi_ref):
        _, abar_re, abar_im, f_re, f_im, _ = _zoh(ar_ref[...], ai_ref[...], ld_ref[...])
        abr_ref[...] = abar_re
        abi_ref[...] = abar_im
        fr, fi = _lanes_p(f_re), _lanes_p(f_im)
        bbr_ref[...] = fr * br_ref[...] - fi * bi_ref[...]
        bbi_ref[...] = fr * bi_ref[...] + fi * br_ref[...]

    gn = jax.ShapeDtypeStruct((S5_G, S5_N), F32)
    gpn = jax.ShapeDtypeStruct((S5_G, S5_P * S5_N), F32)
    return pl.pallas_call(body, name="s5_discretize", out_shape=(gn, gn, gpn, gpn))(a_re, a_im, log_dt, b_re_t, b_im_t)


def _s5_discretize_bwd(a_re, a_im, log_dt, b_re_t, b_im_t, dab_re, dab_im, dbb_re_t, dbb_im_t):
    def body(ar_ref, ai_ref, ld_ref, br_ref, bi_ref, gar_ref, gai_ref, gbr_ref, gbi_ref,
             dar_ref, dai_ref, dld_ref, dbr_ref, dbi_ref):
        a_r, a_i = ar_ref[...], ai_ref[...]
        dt, abar_re, abar_im, f_re, f_im, den = _zoh(a_r, a_i, ld_ref[...])
        b_r, b_i, g_br, g_bi = br_ref[...], bi_ref[...], gbr_ref[...], gbi_ref[...]
        fr, fi = _lanes_p(f_re), _lanes_p(f_im)
        dbr_ref[...] = fr * g_br + fi * g_bi
        dbi_ref[...] = fr * g_bi - fi * g_br
        t_r = b_r * g_br + b_i * g_bi
        t_i = b_r * g_bi - b_i * g_br
        gf_r = sum(t_r[:, p * S5_N:(p + 1) * S5_N] for p in range(S5_P))
        gf_i = sum(t_i[:, p * S5_N:(p + 1) * S5_N] for p in range(S5_P))
        inv_r, inv_i = a_r / den, a_i / den
        ga_r = gar_ref[...] + gf_r * inv_r - gf_i * inv_i
        ga_i = gai_ref[...] + gf_r * inv_i + gf_i * inv_r
        q_r = -(f_re * a_r + f_im * a_i) / den
        q_i = -(f_im * a_r - f_re * a_i) / den
        gl_r = q_r * gf_r + q_i * gf_i
        gl_i = q_r * gf_i - q_i * gf_r
        dar_ref[...] = gl_r + dt * (abar_re * ga_r + abar_im * ga_i)
        dai_ref[...] = gl_i + dt * (abar_re * ga_i - abar_im * ga_r)
        la_r = a_r * abar_re - a_i * abar_im
        la_i = a_r * abar_im + a_i * abar_re
        dld_ref[...] = dt * jnp.sum(ga_r * la_r + ga_i * la_i, axis=-1, keepdims=True)

    gn = jax.ShapeDtypeStruct((S5_G, S5_N), F32)
    gpn = jax.ShapeDtypeStruct((S5_G, S5_P * S5_N), F32)
    return pl.pallas_call(
        body, name="s5_discretize_bwd", out_shape=(gn, gn, jax.ShapeDtypeStruct((S5_G, 1), F32), gpn, gpn),
    )(a_re, a_im, log_dt, b_re_t, b_im_t, dab_re, dab_im, dbb_re_t, dbb_im_t)


S5_ZQ = S5_NB // 2


def _s5_z(re, im):
    return jnp.concatenate([re.reshape(S5_ZQ, 8, 128), im.reshape(S5_ZQ, 8, 128)], axis=0)


def _s5_unz(z):
    return z[:S5_ZQ].reshape(S5_G, S5_N), z[S5_ZQ:].reshape(S5_G, S5_N)


def _s5_block_mats(bb_re, bb_im, c_re, c_im):
    eye = jnp.eye(S5_GB, dtype=F32)
    bb = jnp.stack([bb_re, bb_im], axis=0).reshape(2, S5_NB, S5_GB, S5_N, S5_P)
    bbm = jnp.einsum("rbgnp,gh->bgprhn", bb, eye).reshape(S5_NB, S5_GB * S5_P, 2 * S5_BS)
    cc = jnp.stack([c_re, -c_im], axis=0).reshape(2, S5_NB, S5_GB, S5_P, S5_N)
    ccm = jnp.einsum("rbgpn,gh->brhngp", cc, eye).reshape(S5_NB, 2 * S5_BS, S5_GB * S5_P)
    return bbm.astype(BF16), ccm.astype(BF16)


def _s5_block_diag_bb(m):
    t = m.reshape(S5_NB, S5_GB, S5_P, 2, S5_GB, S5_N)
    d = jnp.einsum("bgprgn->rbgnp", t).reshape(2, S5_G, S5_N, S5_P)
    return d[0], d[1]


def _s5_block_diag_cc(m):
    t = m.reshape(S5_NB, 2, S5_GB, S5_N, S5_GB, S5_P)
    d = jnp.einsum("brgngp->rbgpn", t).reshape(2, S5_G, S5_P, S5_N)
    return d[0], -d[1]


SCAN_UNROLL = 8


def _z_store(zr, zi, blk, res, t, off):
    q, h = blk // 2, blk % 2
    for lt in range(4):
        zr[q, pl.ds(off + 4 * h + lt, t, stride=8), :] = res[:, lt * 128:(lt + 1) * 128]
        zi[q, pl.ds(off + 4 * h + lt, t, stride=8), :] = res[:, S5_BS + lt * 128:S5_BS + (lt + 1) * 128]


def _z_load(zr, zi, blk, t, off):
    q, h = blk // 2, blk % 2
    return jnp.concatenate([zr[q, pl.ds(off + 4 * h + lt, t, stride=8), :] for lt in range(4)]
                           + [zi[q, pl.ds(off + 4 * h + lt, t, stride=8), :] for lt in range(4)], axis=1)


def _z_scan_fwd(zr, zi, a_ref, carry_ref, t, off):
    ar = [a_ref[q] for q in range(S5_ZQ)]
    ai = [a_ref[S5_ZQ + q] for q in range(S5_ZQ)]

    def step(it, carry):
        carry = list(carry)
        base = pl.multiple_of(it * (8 * SCAN_UNROLL), 8 * SCAN_UNROLL) + off
        for tt in range(SCAN_UNROLL):
            rows = pl.ds(base + 8 * tt, 8)
            for q in range(S5_ZQ):
                c_r, c_i = carry[q], carry[S5_ZQ + q]
                n_r = ar[q] * c_r - ai[q] * c_i + zr[q, rows, :]
                n_i = ar[q] * c_i + ai[q] * c_r + zi[q, rows, :]
                zr[q, rows, :] = n_r
                zi[q, rows, :] = n_i
                carry[q], carry[S5_ZQ + q] = n_r, n_i
        return tuple(carry)

    out = lax.fori_loop(0, t // SCAN_UNROLL, step, tuple(carry_ref[k] for k in range(2 * S5_ZQ)))
    for k in range(2 * S5_ZQ):
        carry_ref[k] = out[k]


def _z_scan_bwd(lr, li, xr, xi, a_ref, carry_ref, acc_ref, t):
    ar = [a_ref[q] for q in range(S5_ZQ)]
    ai = [a_ref[S5_ZQ + q] for q in range(S5_ZQ)]
    n_it = t // SCAN_UNROLL

    def step(it, state):
        carry, acc = list(state[0]), list(state[1])
        base = pl.multiple_of((n_it - 1 - it) * (8 * SCAN_UNROLL), 8 * SCAN_UNROLL)
        for tt in reversed(range(SCAN_UNROLL)):
            rows = pl.ds(base + 8 * tt, 8)
            for q in range(S5_ZQ):
                c_r, c_i = carry[q], carry[S5_ZQ + q]
                n_r = ar[q] * c_r + ai[q] * c_i + lr[q, rows, :]
                n_i = ar[q] * c_i - ai[q] * c_r + li[q, rows, :]
                lr[q, rows, :] = n_r
                li[q, rows, :] = n_i
                p_r, p_i = xr[q, rows, :], xi[q, rows, :]
                acc[q] = acc[q] + n_r * p_r + n_i * p_i
                acc[S5_ZQ + q] = acc[S5_ZQ + q] + n_i * p_r - n_r * p_i
                carry[q], carry[S5_ZQ + q] = n_r, n_i
        return tuple(carry), tuple(acc)

    k8 = range(2 * S5_ZQ)
    carry, acc = lax.fori_loop(0, n_it, step, (tuple(carry_ref[k] for k in k8), tuple(acc_ref[k] for k in k8)))
    for k in k8:
        carry_ref[k] = carry[k]
        acc_ref[k] = acc[k]


def _s5_fwd(proj, mix, bbm, ccm, d_row, glu_w, glu_b, tabs, t, side):
    l = proj.shape[0]
    nt = l // t
    n_in = 9

    def body(*refs):
        u_ref, gs_ref, bb_ref, cc_ref, d_ref, gw_ref, gb_ref, a_ref, _ = refs[:n_in]
        side_in = refs[n_in:n_in + len(side.srcs)]
        ssm_ref, xst_ref = refs[n_in + len(side.srcs):n_in + len(side.srcs) + 2]
        side_out = refs[n_in + len(side.srcs) + 2:n_in + len(side.srcs) + 2 + side.n]
        zr, zi, carry = refs[n_in + len(side.srcs) + 2 + side.n:n_in + len(side.srcs) + 5 + side.n]
        sems = refs[n_in + len(side.srcs) + 5 + side.n:]

        @pl.when(pl.program_id(0) == 0)
        def _():
            side.start(side_in, side_out, sems)
            carry[...] = jnp.zeros_like(carry)

        xst_ref[0] = carry[...]
        ub = u_ref[...]
        u = ub.astype(F32)
        for blk in range(S5_NB):
            _z_store(zr, zi, blk, _dot(ub[:, blk * 128:(blk + 1) * 128], bb_ref[blk]), t, 0)
        _z_scan_fwd(zr, zi, a_ref, carry, t, 0)
        ys = jnp.concatenate(
            [_dot(_z_load(zr, zi, blk, t, 0).astype(BF16), cc_ref[blk]) for blk in range(S5_NB)], axis=1)
        y2 = _gelu(ys + d_ref[...] * u)
        z = _dot(y2.astype(BF16), gw_ref[...]) + gb_ref[...]
        ssm_ref[...] = (y2 * _sigmoid(z) * _silu(gs_ref[...].astype(F32))).astype(ssm_ref.dtype)

        @pl.when(pl.program_id(0) == nt - 1)
        def _():
            side.wait(side_in, side_out, sems)

    const2 = lambda shape: pl.BlockSpec(shape, lambda i: (0,) * len(shape))
    zshape = (2 * S5_ZQ, 8, 128)
    return pl.pallas_call(
        body, name="s5_fwd", grid=(nt,),
        in_specs=[pl.BlockSpec((t, D_MODEL), lambda i: (i, 3)), pl.BlockSpec((t, D_MODEL), lambda i: (i, 4)),
                  const2(bbm.shape), const2(ccm.shape), const2((1, D_MODEL)), const2((D_MODEL, D_MODEL)),
                  const2((1, D_MODEL)), const2(zshape), pl.BlockSpec(memory_space=pl.ANY)] + side.in_specs,
        out_specs=(pl.BlockSpec((t, D_MODEL), lambda i: (i, 1)), pl.BlockSpec((1,) + zshape, lambda i: (i, 0, 0, 0)),
                   *side.out_specs),
        out_shape=(jax.ShapeDtypeStruct((l, 2 * D_MODEL), BF16), jax.ShapeDtypeStruct((nt,) + zshape, F32),
                   *side.landing),
        scratch_shapes=[pltpu.VMEM((S5_ZQ, 8 * t, 128), F32), pltpu.VMEM((S5_ZQ, 8 * t, 128), F32),
                        pltpu.VMEM(zshape, F32)] + side.scratch,
        input_output_aliases={8: 0},
        compiler_params=_cp("arbitrary"),
    )(proj, proj, bbm, ccm, d_row, glu_w, glu_b, tabs, mix, *side.srcs)


def _s5_bwd(proj, dmix, xstart, bbm, ccm, d_row, glu_w, glu_b, tabs, t):
    l = proj.shape[0]
    nt = l // t

    def body(u_ref, gs_ref, dm_ref, xst_ref, bb_ref, cc_ref, d_ref, gw_ref, gb_ref, a_ref,
             dug_ref, y2_ref, dz_ref, dbb_ref, dcc_ref, da_ref, dd_ref, dgb_ref, xr, xi, lr, li, carry, lcarry):
        @pl.when(pl.program_id(0) == 0)
        def _():
            lcarry[...] = jnp.zeros_like(lcarry)
            dbb_ref[...] = jnp.zeros_like(dbb_ref)
            dcc_ref[...] = jnp.zeros_like(dcc_ref)
            da_ref[...] = jnp.zeros_like(da_ref)
            dd_ref[...] = jnp.zeros_like(dd_ref)
            dgb_ref[...] = jnp.zeros_like(dgb_ref)

        carry[...] = xst_ref[0]
        for q in range(S5_ZQ):
            xr[q, 0:8, :] = carry[q]
            xi[q, 0:8, :] = carry[S5_ZQ + q]
        ub = u_ref[...]
        u = ub.astype(F32)
        for blk in range(S5_NB):
            _z_store(xr, xi, blk, _dot(ub[:, blk * 128:(blk + 1) * 128], bb_ref[blk]), t, 8)
        _z_scan_fwd(xr, xi, a_ref, carry, t, 8)
        ys = jnp.concatenate(
            [_dot(_z_load(xr, xi, blk, t, 8).astype(BF16), cc_ref[blk]) for blk in range(S5_NB)], axis=1)
        dv = d_ref[...]
        y1 = ys + dv * u
        y2 = _gelu(y1)
        y2b = y2.astype(BF16)
        sg = _sigmoid(_dot(y2b, gw_ref[...]) + gb_ref[...])
        gs = gs_ref[...].astype(F32)
        dssm = dm_ref[...].astype(F32)
        dug_ref[:, D_MODEL:] = (dssm * (y2 * sg) * _dsilu(gs)).astype(dug_ref.dtype)
        dy3 = dssm * _silu(gs)
        dz = dy3 * y2 * sg * (1.0 - sg)
        dzb = dz.astype(BF16)
        y2_ref[...] = y2b
        dz_ref[...] = dzb
        dgb_ref[...] += jnp.sum(dz, axis=0, keepdims=True)
        dy1 = (dy3 * sg + _dot_nt(dzb, gw_ref[...])) * _dgelu(y1)
        dd_ref[...] += jnp.sum(dy1 * u, axis=0, keepdims=True)
        dyb = dy1.astype(BF16)
        for blk in range(S5_NB):
            ch = slice(blk * 128, (blk + 1) * 128)
            _z_store(lr, li, blk, _dot_nt(dyb[:, ch], cc_ref[blk]), t, 0)
            dcc_ref[blk] += _dot_tn(_z_load(xr, xi, blk, t, 8).astype(BF16), dyb[:, ch])
        _z_scan_bwd(lr, li, xr, xi, a_ref, lcarry, da_ref, t)
        du = []
        for blk in range(S5_NB):
            lb = _z_load(lr, li, blk, t, 0).astype(BF16)
            du.append(_dot_nt(lb, bb_ref[blk]))
            dbb_ref[blk] += _dot_tn(ub[:, blk * 128:(blk + 1) * 128], lb)
        dug_ref[:, :D_MODEL] = (jnp.concatenate(du, axis=1) + dy1 * dv).astype(dug_ref.dtype)

    rev = lambda i: nt - 1 - i
    const2 = lambda shape: pl.BlockSpec(shape, lambda i: (0,) * len(shape))
    row_out = lambda w: pl.BlockSpec((t, w), lambda i: (rev(i), 0))
    zshape = (2 * S5_ZQ, 8, 128)
    return pl.pallas_call(
        body, name="s5_bwd", grid=(nt,),
        in_specs=[pl.BlockSpec((t, D_MODEL), lambda i: (rev(i), 3)), pl.BlockSpec((t, D_MODEL), lambda i: (rev(i), 4)),
                  pl.BlockSpec((t, D_MODEL), lambda i: (rev(i), 1)),
                  pl.BlockSpec((1,) + zshape, lambda i: (rev(i), 0, 0, 0)),
                  const2(bbm.shape), const2(ccm.shape), const2((1, D_MODEL)), const2((D_MODEL, D_MODEL)),
                  const2((1, D_MODEL)), const2(zshape)],
        out_specs=(row_out(2 * D_MODEL), row_out(D_MODEL), row_out(D_MODEL), const2(bbm.shape), const2(ccm.shape),
                   const2(zshape), const2((1, D_MODEL)), const2((1, D_MODEL))),
        out_shape=(jax.ShapeDtypeStruct((l, 2 * D_MODEL), BF16), jax.ShapeDtypeStruct((l, D_MODEL), BF16),
                   jax.ShapeDtypeStruct((l, D_MODEL), BF16), jax.ShapeDtypeStruct(bbm.shape, F32),
                   jax.ShapeDtypeStruct(ccm.shape, F32), jax.ShapeDtypeStruct(zshape, F32),
                   jax.ShapeDtypeStruct((1, D_MODEL), F32), jax.ShapeDtypeStruct((1, D_MODEL), F32)),
        scratch_shapes=[pltpu.VMEM((S5_ZQ, 8 * t + 8, 128), F32), pltpu.VMEM((S5_ZQ, 8 * t + 8, 128), F32),
                        pltpu.VMEM((S5_ZQ, 8 * t, 128), F32), pltpu.VMEM((S5_ZQ, 8 * t, 128), F32),
                        pltpu.VMEM(zshape, F32), pltpu.VMEM(zshape, F32)],
        compiler_params=_cp("arbitrary"),
    )(proj, proj, dmix, xstart, bbm, ccm, d_row, glu_w, glu_b, tabs)


def _attn_probs(qh, kh):
    s = _dot_nt(qh, kh) * (XA_DH ** -0.5)
    e = jnp.exp(s - jnp.max(s, axis=-1, keepdims=True))
    return e / jnp.sum(e, axis=-1, keepdims=True)


def _attn_fwd(qa, ka, va):
    l = qa.shape[0]
    m = ka.shape[0]
    tl = _pick(l, (512, 256))

    def body(q_ref, k_ref, v_ref, o_ref):
        for h in range(XA_HEADS):
            hs = slice(h * XA_DH, (h + 1) * XA_DH)
            p = _attn_probs(q_ref[:, hs], k_ref[:, hs])
            o_ref[:, hs] = _dot(p.astype(BF16), v_ref[:, hs]).astype(o_ref.dtype)

    return pl.pallas_call(
        body, name="xattn_fwd", grid=(l // tl,),
        in_specs=[pl.BlockSpec((tl, D_MODEL), lambda i: (i, 0)), pl.BlockSpec((m, D_MODEL), lambda i: (0, 0)),
                  pl.BlockSpec((m, D_MODEL), lambda i: (0, 0))],
        out_specs=pl.BlockSpec((tl, D_MODEL), lambda i: (i, 0)),
        out_shape=jax.ShapeDtypeStruct((l, D_MODEL), BF16), compiler_params=_cp("parallel"),
    )(qa, ka, va)


def _attn_bwd(qa, ka, va, doa):
    l = qa.shape[0]
    m = ka.shape[0]
    tl = _pick(l, (512, 256))

    def body(q_ref, k_ref, v_ref, do_ref, dq_ref, dk_ref, dv_ref):
        @pl.when(pl.program_id(0) == 0)
        def _():
            dk_ref[...] = jnp.zeros_like(dk_ref)
            dv_ref[...] = jnp.zeros_like(dv_ref)

        for h in range(XA_HEADS):
            hs = slice(h * XA_DH, (h + 1) * XA_DH)
            qh, kh, vh, doh = q_ref[:, hs], k_ref[:, hs], v_ref[:, hs], do_ref[:, hs]
            p = _attn_probs(qh, kh)
            dv_ref[:, hs] += _dot_tn(p.astype(BF16), doh)
            dp = _dot_nt(doh, vh)
            ds = (p * (dp - jnp.sum(dp * p, axis=-1, keepdims=True)) * (XA_DH ** -0.5)).astype(BF16)
            dq_ref[:, hs] = _dot(ds, kh).astype(dq_ref.dtype)
            dk_ref[:, hs] += _dot_tn(ds, qh)

    row = pl.BlockSpec((tl, D_MODEL), lambda i: (i, 0))
    mem = pl.BlockSpec((m, D_MODEL), lambda i: (0, 0))
    return pl.pallas_call(
        body, name="xattn_bwd", grid=(l // tl,), in_specs=[row, mem, mem, row], out_specs=(row, mem, mem),
        out_shape=(jax.ShapeDtypeStruct((l, D_MODEL), BF16), jax.ShapeDtypeStruct((m, D_MODEL), F32),
                   jax.ShapeDtypeStruct((m, D_MODEL), F32)),
        compiler_params=_cp("arbitrary"),
    )(qa, ka, va, doa)


def _me_and_peers():
    x, y, c = lax.axis_index("x"), lax.axis_index("y"), lax.axis_index("c")
    flip = lambda v, bit: (1 - v) if bit else v
    peers = []
    for k in range(1, N_DEV):
        px, py, pc = flip(x, (k >> 2) & 1), flip(y, (k >> 1) & 1), flip(c, k & 1)
        peers.append(((px, py, pc), 4 * px + 2 * py + pc))
    return 4 * x + 2 * y + c, peers


class _SideJob:
    def __init__(self, srcs, landing, src_of, dst_of):
        self.srcs = list(srcs)
        self.landing = list(landing)
        self.n = len(self.landing)
        self.src_of, self.dst_of = src_of, dst_of
        hbm = pl.BlockSpec(memory_space=pl.ANY)
        self.in_specs = [hbm] * len(self.srcs)
        self.out_specs = [hbm] * self.n
        self.scratch = [pltpu.SemaphoreType.DMA((self.n * (N_DEV - 1),)), pltpu.SemaphoreType.DMA((self.n * (N_DEV - 1),)),
                        pltpu.SemaphoreType.DMA((self.n,))]

    def _copies(self, src_refs, out_refs, sems):
        send_sems, recv_sems, loc_sems = sems
        me, peers = _me_and_peers()
        local = [pltpu.make_async_copy(self.src_of(a, me, src_refs), self.dst_of(a, me, out_refs), loc_sems.at[a])
                 for a in range(self.n)]
        sends, recvs = [], []
        for k, (peer, peer_idx) in enumerate(peers):
            for a in range(self.n):
                s = self.n * k + a
                sends.append(pltpu.make_async_remote_copy(
                    src_ref=self.src_of(a, peer_idx, src_refs), dst_ref=self.dst_of(a, me, out_refs),
                    send_sem=send_sems.at[s], recv_sem=recv_sems.at[s], device_id=peer, device_id_type=MESH))
                recvs.append(pltpu.make_async_remote_copy(
                    src_ref=self.src_of(a, me, src_refs), dst_ref=self.dst_of(a, peer_idx, out_refs),
                    send_sem=send_sems.at[s], recv_sem=recv_sems.at[s], device_id=peer, device_id_type=MESH))
        return local, sends, recvs

    def start(self, src_refs, out_refs, sems):
        if not self.n:
            return
        local, sends, _ = self._copies(src_refs, out_refs, sems)
        for cp in local + sends:
            cp.start()

    def wait(self, src_refs, out_refs, sems):
        if not self.n:
            return
        local, sends, recvs = self._copies(src_refs, out_refs, sems)
        for cp in recvs:
            cp.wait_recv()
        for cp in sends:
            cp.wait_send()
        for cp in local:
            cp.wait()


def _gather_job(shards):
    return _SideJob(shards, [jax.ShapeDtypeStruct((N_DEV,) + s.shape, s.dtype) for s in shards],
                    src_of=lambda a, j, srcs: srcs[a], dst_of=lambda a, j, outs: outs[a].at[j])


def _scatter_job(grads):
    landing, parts = [], []
    for g in grads:
        if g.ndim == 3:
            landing.append(jax.ShapeDtypeStruct(g.shape, g.dtype))
            parts.append(None)
        else:
            r = g.shape[0] // N_DEV
            landing.append(jax.ShapeDtypeStruct((N_DEV, r, g.shape[1]), g.dtype))
            parts.append(r)

    def src_of(a, j, srcs):
        if parts[a] is None:
            return srcs[a].at[j]
        return srcs[a].at[pl.ds(pl.multiple_of(j * parts[a], 8), parts[a]), :]

    return _SideJob(grads, landing, src_of=src_of, dst_of=lambda a, j, outs: outs[a].at[j])


def _allgather_w_in(w_in_shard, row_shards):
    n_row = len(row_shards)

    def body(*refs):
        win_ref = refs[0]
        row_refs = refs[1:1 + n_row]
        out_win = refs[1 + n_row]
        row_outs = refs[2 + n_row:2 + 2 * n_row]
        win_b, send_sems, recv_sems, local_sem = refs[2 + 2 * n_row:]
        win_b[...] = win_ref[...].astype(BF16)
        x, y, c = lax.axis_index("x"), lax.axis_index("y"), lax.axis_index("c")
        me, sibling = (x, y, c), (x, y, 1 - c)
        chips = [(1 - x, y), (x, 1 - y), (1 - x, 1 - y)]
        slot = lambda p: out_win.at[4 * p[0] + 2 * p[1] + p[2]]

        def copy(k, block, to, src=None):
            return pltpu.make_async_remote_copy(
                src_ref=slot(block) if src is None else src, dst_ref=slot(block), send_sem=send_sems.at[k],
                recv_sem=recv_sems.at[k], device_id=to, device_id_type=MESH)

        mine = pltpu.make_async_copy(win_b, slot(me), local_sem)
        mine.start()
        first = [copy(0, me, sibling, src=win_b)]
        first += [copy(1 + j, me, (*chip, c), src=win_b) for j, chip in enumerate(chips)]
        for cp in first:
            cp.start()
        for r, o in zip(row_refs, row_outs):
            o[...] = r[...].astype(BF16)
        passed = [copy(4 + j, (*chip, c), sibling) for j, chip in enumerate(chips)]
        for j, chip in enumerate(chips):
            copy(1 + j, (*chip, c), me).wait_recv()
            passed[j].start()
        copy(0, sibling, me).wait_recv()
        for j, chip in enumerate(chips):
            copy(4 + j, (*chip, 1 - c), me).wait_recv()
        for cp in first + passed:
            cp.wait_send()
        mine.wait()

    vm = pl.BlockSpec(memory_space=pltpu.VMEM)
    return pl.pallas_call(
        body, name="allgather_w_in", in_specs=[vm] * (1 + n_row),
        out_specs=(pl.BlockSpec(memory_space=pl.ANY), *([vm] * n_row)),
        out_shape=(jax.ShapeDtypeStruct((N_DEV,) + w_in_shard.shape, BF16),
                   *[jax.ShapeDtypeStruct(r.shape, BF16) for r in row_shards]),
        scratch_shapes=[pltpu.VMEM(w_in_shard.shape, BF16), pltpu.SemaphoreType.DMA((N_DEV - 1,)),
                        pltpu.SemaphoreType.DMA((N_DEV - 1,)), pltpu.SemaphoreType.DMA],
        compiler_params=pltpu.CompilerParams(vmem_limit_bytes=VMEM_LIMIT),
    )(w_in_shard, *row_shards)


def _allreduce_small(small):
    rows = SMALL_ROWS // N_DEV

    def body(x_ref, out_ref, land, send1, recv1, send2, recv2):
        me, peers = _me_and_peers()
        block = lambda j: pl.ds(pl.multiple_of(j * rows, 8), rows)

        def phase(src_of, dst_of, send_sems, recv_sems):
            sends = [pltpu.make_async_remote_copy(src_ref=src_of(pidx), dst_ref=dst_of(me), send_sem=send_sems.at[k],
                                                  recv_sem=recv_sems.at[k], device_id=peer, device_id_type=MESH)
                     for k, (peer, pidx) in enumerate(peers)]
            recvs = [pltpu.make_async_remote_copy(src_ref=src_of(me), dst_ref=dst_of(pidx), send_sem=send_sems.at[k],
                                                  recv_sem=recv_sems.at[k], device_id=peer, device_id_type=MESH)
                     for k, (peer, pidx) in enumerate(peers)]
            for cp in sends:
                cp.start()
            for cp in recvs:
                cp.wait_recv()
            for cp in sends:
                cp.wait_send()

        land[me] = x_ref[block(me), :]
        phase(lambda j: x_ref.at[block(j), :], lambda j: land.at[j], send1, recv1)
        total = land[0]
        for j in range(1, N_DEV):
            total = total + land[j]
        out_ref[block(me), :] = total
        phase(lambda j: out_ref.at[block(me), :], lambda j: out_ref.at[block(j), :], send2, recv2)

    vm = pl.BlockSpec(memory_space=pltpu.VMEM)
    return pl.pallas_call(
        body, name="allreduce_small", in_specs=[vm], out_specs=vm, out_shape=jax.ShapeDtypeStruct(small.shape, F32),
        scratch_shapes=[pltpu.VMEM((N_DEV, rows, D_MODEL), F32)] + [pltpu.SemaphoreType.DMA((N_DEV - 1,))] * 4,
    )(small)


def _adamw(name, got, w, m, v):
    r, c = w.shape
    n_slots = got.shape[0]
    tr = _pick(r, (256, 128, 64))

    def body(got_ref, w_ref, m_ref, v_ref, g_ref, d_ref, nm_ref, nv_ref):
        g = got_ref[0].astype(F32)
        for j in range(1, n_slots):
            g = g + got_ref[j].astype(F32)
        nm = ADAM_B1 * m_ref[...] + (1.0 - ADAM_B1) * g
        nv = ADAM_B2 * v_ref[...] + (1.0 - ADAM_B2) * jnp.square(g)
        m_hat = nm / (1.0 - ADAM_B1 ** ADAM_STEP)
        v_hat = nv / (1.0 - ADAM_B2 ** ADAM_STEP)
        g_ref[...] = g
        d_ref[...] = -ADAM_LR * (m_hat / (jnp.sqrt(v_hat) + ADAM_EPS) + ADAM_WD * w_ref[...])
        nm_ref[...] = nm
        nv_ref[...] = nv

    blk = pl.BlockSpec((tr, c), lambda i: (i, 0))
    out = jax.ShapeDtypeStruct((r, c), F32)
    return pl.pallas_call(
        body, name=name, grid=(r // tr,),
        in_specs=[pl.BlockSpec((n_slots, tr, c), lambda i: (0, i, 0)), blk, blk, blk],
        out_specs=(blk, blk, blk, blk), out_shape=(out, out, out, out), compiler_params=_cp("parallel"),
    )(got, w, m, v)


_SMALL_VECS = ("norm1_g", "ret_gn_g", "s5_d", "s5_glu_b", "norm2_g", "norm_mem_g", "norm_f_g")
_SMALL_ORDER = _SMALL_VECS + ("s5_a_re", "s5_a_im", "s5_log_dt", "s5_b_re", "s5_b_im", "s5_c_re", "s5_c_im")


def _small_layout():
    lay, row = {}, 0
    for n in _SMALL_VECS + ("loss",):
        lay[n] = (row, 1, D_MODEL)
        row += 1
    for n in ("s5_a_re", "s5_a_im"):
        lay[n] = (row, 4, D_MODEL)
        row += 4
    lay["s5_log_dt"] = (row, 1, S5_G)
    row += 8
    for n in ("s5_b_re", "s5_b_im", "s5_c_re", "s5_c_im"):
        lay[n] = (row, 64, D_MODEL)
        row += 64
    assert row <= SMALL_ROWS
    return lay


def _pack_small(t, loss_row=None):
    lay = _small_layout()
    pieces = [t[n].reshape(1, D_MODEL) for n in _SMALL_VECS]
    pieces.append(jnp.zeros((1, D_MODEL), F32) if loss_row is None else loss_row)
    pieces += [t["s5_a_re"].reshape(4, D_MODEL), t["s5_a_im"].reshape(4, D_MODEL)]
    pieces.append(jnp.pad(t["s5_log_dt"].reshape(1, S5_G), ((0, 7), (0, D_MODEL - S5_G))))
    pieces += [t[n].reshape(64, D_MODEL) for n in ("s5_b_re", "s5_b_im", "s5_c_re", "s5_c_im")]
    pieces.append(jnp.zeros((SMALL_ROWS - lay["s5_c_im"][0] - 64, D_MODEL), F32))
    return jnp.concatenate(pieces, axis=0)


def _adamw_small(g_sum, w, m, v):
    lay = _small_layout()
    names = [n for n in lay if n != "loss"]

    def body(g_ref, w_ref, m_ref, v_ref, *outs):
        g = g_ref[...]
        nm = ADAM_B1 * m_ref[...] + (1.0 - ADAM_B1) * g
        nv = ADAM_B2 * v_ref[...] + (1.0 - ADAM_B2) * jnp.square(g)
        m_hat = nm / (1.0 - ADAM_B1 ** ADAM_STEP)
        v_hat = nv / (1.0 - ADAM_B2 ** ADAM_STEP)
        delta = -ADAM_LR * (m_hat / (jnp.sqrt(v_hat) + ADAM_EPS) + ADAM_WD * w_ref[...])
        for i, n in enumerate(names):
            r0, rows, lanes = lay[n]
            for part, val in enumerate((g, delta, nm, nv)):
                outs[4 * i + part][...] = val[r0:r0 + rows, 0:lanes]
        r0 = lay["loss"][0]
        outs[-1][...] = g[r0:r0 + 1, :]

    shapes = []
    for n in names:
        shapes += [jax.ShapeDtypeStruct(lay[n][1:], F32)] * 4
    shapes.append(jax.ShapeDtypeStruct((1, D_MODEL), F32))
    outs = pl.pallas_call(body, name="adamw_small", out_shape=tuple(shapes),
                          compiler_params=pltpu.CompilerParams(vmem_limit_bytes=VMEM_LIMIT))(g_sum, w, m, v)
    return {n: tuple(outs[4 * i:4 * i + 4]) for i, n in enumerate(names)}, outs[-1]


_W_NAMES = ("norm1_g", "w_in", "ret_gn_g", "s5_a_re", "s5_a_im", "s5_log_dt", "s5_b_re", "s5_b_im", "s5_c_re", "s5_c_im",
            "s5_d", "s5_glu_w", "s5_glu_b", "w_out", "norm2_g", "norm_mem_g", "xa_wq", "xa_wk", "xa_wv", "xa_wo",
            "norm_f_g")
_ROW_NAMES = ("s5_glu_w", "w_out", "xa_wq", "xa_wk", "xa_wv", "xa_wo")


def kernel(x, mem, positions, norm1_g, w_in, ret_gn_g, s5_a_re, s5_a_im, s5_log_dt, s5_b_re, s5_b_im, s5_c_re, s5_c_im, s5_d, s5_glu_w, s5_glu_b, w_out, norm2_g, norm_mem_g, xa_wq, xa_wk, xa_wv, xa_wo, norm_f_g, loss_target, m_norm1_g, m_w_in, m_ret_gn_g, m_s5_a_re, m_s5_a_im, m_s5_log_dt, m_s5_b_re, m_s5_b_im, m_s5_c_re, m_s5_c_im, m_s5_d, m_s5_glu_w, m_s5_glu_b, m_w_out, m_norm2_g, m_norm_mem_g, m_xa_wq, m_xa_wk, m_xa_wv, m_xa_wo, m_norm_f_g, v_norm1_g, v_w_in, v_ret_gn_g, v_s5_a_re, v_s5_a_im, v_s5_log_dt, v_s5_b_re, v_s5_b_im, v_s5_c_re, v_s5_c_im, v_s5_d, v_s5_glu_w, v_s5_glu_b, v_w_out, v_norm2_g, v_norm_mem_g, v_xa_wq, v_xa_wk, v_xa_wv, v_xa_wo, v_norm_f_g):
    w = dict(norm1_g=norm1_g, w_in=w_in, ret_gn_g=ret_gn_g, s5_a_re=s5_a_re, s5_a_im=s5_a_im, s5_log_dt=s5_log_dt,
             s5_b_re=s5_b_re, s5_b_im=s5_b_im, s5_c_re=s5_c_re, s5_c_im=s5_c_im, s5_d=s5_d, s5_glu_w=s5_glu_w,
             s5_glu_b=s5_glu_b, w_out=w_out, norm2_g=norm2_g, norm_mem_g=norm_mem_g, xa_wq=xa_wq, xa_wk=xa_wk,
             xa_wv=xa_wv, xa_wo=xa_wo, norm_f_g=norm_f_g)
    mom = dict(norm1_g=m_norm1_g, w_in=m_w_in, ret_gn_g=m_ret_gn_g, s5_a_re=m_s5_a_re, s5_a_im=m_s5_a_im,
               s5_log_dt=m_s5_log_dt, s5_b_re=m_s5_b_re, s5_b_im=m_s5_b_im, s5_c_re=m_s5_c_re, s5_c_im=m_s5_c_im,
               s5_d=m_s5_d, s5_glu_w=m_s5_glu_w, s5_glu_b=m_s5_glu_b, w_out=m_w_out, norm2_g=m_norm2_g,
               norm_mem_g=m_norm_mem_g, xa_wq=m_xa_wq, xa_wk=m_xa_wk, xa_wv=m_xa_wv, xa_wo=m_xa_wo,
               norm_f_g=m_norm_f_g)
    var = dict(norm1_g=v_norm1_g, w_in=v_w_in, ret_gn_g=v_ret_gn_g, s5_a_re=v_s5_a_re, s5_a_im=v_s5_a_im,
               s5_log_dt=v_s5_log_dt, s5_b_re=v_s5_b_re, s5_b_im=v_s5_b_im, s5_c_re=v_s5_c_re, s5_c_im=v_s5_c_im,
               s5_d=v_s5_d, s5_glu_w=v_s5_glu_w, s5_glu_b=v_s5_glu_b, w_out=v_w_out, norm2_g=v_norm2_g,
               norm_mem_g=v_norm_mem_g, xa_wq=v_xa_wq, xa_wk=v_xa_wk, xa_wv=v_xa_wv, xa_wo=v_xa_wo,
               norm_f_g=v_norm_f_g)
    shapes = {n: w[n].shape for n in _W_NAMES}

    x2d, mem2d, tgt = x[0], mem[0], loss_target[0]
    l = x2d.shape[0]
    ret_c = _pick(l, (256, 128))
    s5_t = _pick(l, (256, 128))
    g1, g2, gm, gf = norm1_g, norm2_g, norm_mem_g, norm_f_g.reshape(1, D_MODEL)

    win_s, *row_shards_b = _allgather_w_in(w_in[0], [w[n][0] for n in _ROW_NAMES])

    to_gpn = lambda b: jnp.transpose(b, (0, 2, 1)).reshape(S5_G, S5_P * S5_N)
    from_gpn = lambda b: jnp.transpose(b.reshape(S5_G, S5_P, S5_N), (0, 2, 1))
    disc_args = (s5_a_re[0], s5_a_im[0], s5_log_dt[0].reshape(S5_G, 1), to_gpn(s5_b_re[0]), to_gpn(s5_b_im[0]))
    abar_re, abar_im, bb_re_t, bb_im_t = _s5_discretize(*disc_args)
    bbm, ccm = _s5_block_mats(from_gpn(bb_re_t), from_gpn(bb_im_t), s5_c_re[0], s5_c_im[0])
    a_z = _s5_z(abar_re, abar_im)

    h1 = _rms_fwd("norm1_fwd", x2d, g1)
    proj, *rows_01 = _mm_nn_slots("in_proj", h1, win_s, BF16, side=_gather_job(row_shards_b[:2]))
    full = {n: g.reshape(N_DEV * r, D_MODEL) for n, g, r in zip(_ROW_NAMES[:2], rows_01, ROW_SHARDS[:2])}
    half = RET_DK // 2
    inv = ROPE_BASE ** (-jnp.arange(half, dtype=F32) / half)
    cos_t, sin_t = _rope_tables(positions[0].reshape(l, 1), jnp.tile(inv, 128 // half)[None, :])
    rconsts = _ret_constants(ret_c)
    ret, o_saved, r_prev, q_rot, k_rot = _ret_fwd(proj, cos_t, sin_t, rconsts, ret_gn_g, ret_c)
    mix, xstart, *rows_xa = _s5_fwd(proj, ret, bbm, ccm, s5_d, full["s5_glu_w"], s5_glu_b, a_z, s5_t,
                                    side=_gather_job(row_shards_b[2:]))
    full.update({n: g.reshape(N_DEV * r, D_MODEL) for n, g, r in zip(_ROW_NAMES[2:], rows_xa, ROW_SHARDS[2:])})
    x1, h2 = _mm_nn("out_proj", mix, full["w_out"], F32, residual=x2d, epi=_epi_norm_fwd(g2))
    mn = _rms_fwd("norm_mem_fwd", mem2d, gm)
    qa = _mm_nn("xa_q", h2, full["xa_wq"], BF16)
    ka = _mm_nn("xa_k", mn, full["xa_wk"], BF16)
    va = _mm_nn("xa_v", mn, full["xa_wv"], BF16)
    oa = _attn_fwd(qa, ka, va)
    dx2, dgf, loss_lanes = _mm_nn("xa_o", oa, full["xa_wo"], F32, residual=x1, epi=_epi_loss(gf, tgt))

    doa = _mm_nt("xa_o_dx", dx2, full["xa_wo"], BF16)
    dwo = _mm_tn("xa_o_dw", oa, dx2, BF16)
    dqa, dka, dva = _attn_bwd(qa, ka, va, doa)
    dx1, dg2 = _mm_nt("xa_q_dx", dqa, full["xa_wq"], F32, epi=_epi_norm_bwd(x1, g2, dx2))
    dwq = _mm_tn("xa_q_dw", h2, dqa, BF16)
    dwk = _mm_tn("xa_k_dw", mn, dka, BF16)
    dwv = _mm_tn("xa_v_dw", mn, dva, BF16)
    dmn = _mm_nt("xa_v_dx", dva, full["xa_wv"], F32, residual=_mm_nt("xa_k_dx", dka, full["xa_wk"], F32))
    _, dgm = _rms_bwd("norm_mem_bwd", mem2d, gm, dmn, None)
    dmix = _mm_nt("out_proj_dx", dx1, full["w_out"], BF16)
    dwout = _mm_tn("out_proj_dw", mix, dx1, BF16)
    dret, dgn, *got_a = _ret_bwd(proj, q_rot, k_rot, cos_t, sin_t, rconsts, ret_gn_g, o_saved, r_prev, dmix, ret_c,
                                 side=_scatter_job([dwout, dwq, dwk, dwv, dwo]))
    dug, y2, dz, dbbm, dccm, dabar, dd, dgb = _s5_bwd(proj, dmix, xstart, bbm, ccm, s5_d, full["s5_glu_w"], s5_glu_b,
                                                      a_z, s5_t)
    dglu = _mm_tn("s5_glu_dw", y2, dz, BF16)
    dproj = jnp.concatenate([dret, dug], axis=1)
    dwin_s, got_glu = _mm_tn_slots("in_proj_dw", h1, dproj, N_DEV, BF16, side=_scatter_job([dglu]))
    grad_x, dg1, got_win = _mm_nt_slots("in_proj_dx", dproj, win_s, F32, side=_scatter_job([dwin_s]),
                                        epi=_epi_norm_bwd(x2d, g1, dx1))

    dab_re, dab_im = _s5_unz(dabar)
    dbb_re, dbb_im = _s5_block_diag_bb(dbbm)
    dc_re, dc_im = _s5_block_diag_cc(dccm)
    da_re, da_im, dlog_dt, db_re_t, db_im_t = _s5_discretize_bwd(*disc_args, dab_re, dab_im, to_gpn(dbb_re),
                                                                 to_gpn(dbb_im))
    db_re, db_im = from_gpn(db_re_t), from_gpn(db_im_t)
    small_g = dict(norm1_g=dg1, ret_gn_g=dgn, s5_d=dd, s5_glu_b=dgb, norm2_g=dg2, norm_mem_g=dgm, norm_f_g=dgf,
                   s5_a_re=da_re, s5_a_im=da_im, s5_log_dt=dlog_dt, s5_b_re=db_re, s5_b_im=db_im, s5_c_re=dc_re,
                   s5_c_im=dc_im)
    small_pack = _pack_small(small_g, loss_row=loss_lanes)

    res = {}
    got = dict(zip(("w_out", "xa_wq", "xa_wk", "xa_wv", "xa_wo"), got_a), w_in=got_win, s5_glu_w=got_glu)
    for n in ("w_in",) + _ROW_NAMES:
        res[n] = _adamw("adamw_" + n, got[n], w[n][0], mom[n][0], var[n][0])
    small_sum = _allreduce_small(small_pack)
    small_res, loss_sum = _adamw_small(small_sum, _pack_small(w), _pack_small(mom), _pack_small(var))
    loss = (0.5 / D_MODEL) * jnp.sum(loss_sum)
    res.update(small_res)

    outs = [loss, grad_x[None]]
    for part in range(4):
        for n in _W_NAMES:
            outs.append(res[n][part].reshape(shapes[n]))
    return tuple(outs)
```

```python
import functools

import jax
import jax.numpy as jnp
from jax import lax
from jax.experimental import pallas as pl
from jax.experimental.pallas import tpu as pltpu

F32 = jnp.float32
BF16 = jnp.bfloat16
MESH = pl.DeviceIdType.MESH

D_MODEL = 1024
RET_HEADS, RET_DK, RET_DV = 8, 64, 128
RET_QK = RET_HEADS * RET_DK
S5_G, S5_N, S5_P = 64, 64, 16
S5_NB = 8
S5_GB = S5_G // S5_NB
S5_BS = S5_GB * S5_N
S5_COLS = 2 * S5_G * S5_N
XA_HEADS, XA_DH = 4, 256
EPS = 1e-6
ROPE_BASE = 10000.0
N_DEV = 8
W_IN_SHARD = 640
ROW_SHARDS = (128, 256, 128, 128, 128, 128)
ROWPACK = sum(ROW_SHARDS)
SMALL_ROWS = 320
ADAM_LR, ADAM_B1, ADAM_B2, ADAM_EPS, ADAM_WD, ADAM_STEP = 0.001, 0.9, 0.999, 1e-08, 0.01, 10

VMEM_LIMIT = 56 * 1024 * 1024


def _cp(*sem):
    return pltpu.CompilerParams(dimension_semantics=tuple(sem), vmem_limit_bytes=VMEM_LIMIT)


def _dot(a, b):
    return jnp.dot(a, b, preferred_element_type=F32)


def _dot_nt(a, b):
    return lax.dot_general(a, b, (((1,), (1,)), ((), ())), preferred_element_type=F32)


def _dot_tn(a, b):
    return lax.dot_general(a, b, (((0,), (0,)), ((), ())), preferred_element_type=F32)


def _sigmoid(x):
    return 1.0 / (1.0 + jnp.exp(-x))


def _silu(x):
    return x * _sigmoid(x)


def _dsilu(x):
    s = _sigmoid(x)
    return s * (1.0 + x * (1.0 - s))


_GELU_C = 0.7978845608028654


def _gelu(x):
    return 0.5 * x * (1.0 + jnp.tanh(_GELU_C * (x + 0.044715 * (x * x * x))))


def _dgelu(x):
    t = jnp.tanh(_GELU_C * (x + 0.044715 * (x * x * x)))
    return 0.5 * (1.0 + t) + 0.5 * x * (1.0 - t * t) * (_GELU_C * (1.0 + 3.0 * 0.044715 * (x * x)))


def _pick(n, cands):
    for c in cands:
        if n % c == 0:
            return c
    return n


class _Epilogue:
    def __init__(self, rows, vecs, row_out_dtypes, n_sums, fn):
        self.rows, self.vecs, self.row_out_dtypes, self.n_sums, self.fn = list(rows), list(vecs), list(row_out_dtypes), n_sums, fn


def _rms(x):
    rs = lax.rsqrt(jnp.mean(x * x, axis=-1, keepdims=True) + EPS)
    return rs, x * rs


def _rms_dx(dn, xn, rs):
    return rs * (dn - xn * jnp.mean(dn * xn, axis=-1, keepdims=True))


def _epi_norm_fwd(g):
    def fn(r, rows, vecs):
        return r, [_rms(r)[1] * vecs[0]], []

    return _Epilogue([], [g], [BF16], 0, fn)


def _epi_loss(gf, target):
    def fn(r, rows, vecs):
        rs, xn = _rms(r)
        e = xn * vecs[0] - rows[0]
        dy = e * (1.0 / r.shape[-1])
        return (_rms_dx(dy * vecs[0], xn, rs), [],
                [jnp.sum(dy * xn, axis=0, keepdims=True), jnp.sum(e * e, axis=0, keepdims=True)])

    return _Epilogue([target], [gf], [], 2, fn)


def _epi_norm_bwd(x, g, dres):
    def fn(r, rows, vecs):
        rs, xn = _rms(rows[0])
        return _rms_dx(r * vecs[0], xn, rs) + rows[1], [], [jnp.sum(r * xn, axis=0, keepdims=True)]

    return _Epilogue([x, dres], [g], [], 1, fn)


def _mm_core(name, operands, in_specs, out_spec, out_shape, grid, nk, dims, acc_shape, has_res, side=None, epi=None):
    n_in = 3 if has_res else 2
    n_epi_in = len(epi.rows) + len(epi.vecs) if epi else 0
    n_epi_out = len(epi.row_out_dtypes) + epi.n_sums if epi else 0
    n_side_in = len(side.srcs) if side else 0
    n_side_out = side.n if side else 0

    def body(*refs):
        a_ref, b_ref = refs[0], refs[1]
        r_ref = refs[2] if has_res else None
        epi_in = refs[n_in:n_in + n_epi_in]
        side_in = refs[n_in + n_epi_in:n_in + n_epi_in + n_side_in]
        n0 = n_in + n_epi_in + n_side_in
        o_ref = refs[n0]
        epi_out = refs[n0 + 1:n0 + 1 + n_epi_out]
        side_out = refs[n0 + 1 + n_epi_out:n0 + 1 + n_epi_out + n_side_out]
        rest = refs[n0 + 1 + n_epi_out + n_side_out:]
        acc, sems = (rest[0], rest[1:]) if nk > 1 else (None, rest)
        i, j, k = pl.program_id(0), pl.program_id(1), pl.program_id(2)
        if side:
            @pl.when((i == 0) & (j == 0) & (k == 0))
            def _():
                side.start(side_in, side_out, sems)

        def product():
            return lax.dot_general(a_ref[...].astype(BF16), b_ref[...].astype(BF16), (dims, ((), ())),
                                   preferred_element_type=F32)

        def finish(r):
            if has_res:
                r = r + r_ref[...]
            if epi is None:
                o_ref[...] = r.astype(o_ref.dtype)
                return
            n_rows = len(epi.rows)
            main, row_vals, sums = epi.fn(r, [t[...] for t in epi_in[:n_rows]], [t[...] for t in epi_in[n_rows:]])
            o_ref[...] = main.astype(o_ref.dtype)
            for ref, val in zip(epi_out, row_vals):
                ref[...] = val.astype(ref.dtype)
            for ref, val in zip(epi_out[len(row_vals):], sums):
                @pl.when(i == 0)
                def _(ref=ref):
                    ref[...] = jnp.zeros_like(ref)

                ref[...] += val

        if nk == 1:
            finish(product())
        else:
            @pl.when(k == 0)
            def _():
                acc[...] = jnp.zeros_like(acc)

            acc[...] += product()

            @pl.when(k == nk - 1)
            def _():
                finish(acc[...])

        if side:
            @pl.when((i == grid[0] - 1) & (j == grid[1] - 1) & (k == grid[2] - 1))
            def _():
                side.wait(side_in, side_out, sems)

    acc_scratch = [pltpu.VMEM(acc_shape, F32)] if nk > 1 else []
    in_specs, out_specs, out_shapes, operands = list(in_specs), [out_spec], [out_shape], list(operands)
    if epi:
        assert grid[1] == 1, "an epilogue needs tiles that span whole rows"
        tm, n = out_spec.block_shape
        row_spec = pl.BlockSpec((tm, n), lambda i, j, k: (i, 0))
        vec_spec = pl.BlockSpec((1, n), lambda i, j, k: (0, 0))
        in_specs += [row_spec] * len(epi.rows) + [vec_spec] * len(epi.vecs)
        operands += epi.rows + epi.vecs
        out_specs += [row_spec] * len(epi.row_out_dtypes) + [vec_spec] * epi.n_sums
        out_shapes += [jax.ShapeDtypeStruct(out_shape.shape, d) for d in epi.row_out_dtypes]
        out_shapes += [jax.ShapeDtypeStruct((1, n), F32)] * epi.n_sums
    scratch = acc_scratch
    if side:
        in_specs += side.in_specs
        operands += side.srcs
        out_specs += side.out_specs
        out_shapes += side.landing
        scratch = acc_scratch + side.scratch
    plain = side is None and epi is None
    res = pl.pallas_call(
        body, name=name, grid=grid, in_specs=in_specs, out_specs=out_specs[0] if plain else tuple(out_specs),
        out_shape=out_shapes[0] if plain else tuple(out_shapes), scratch_shapes=scratch,
        compiler_params=_cp("parallel", "parallel", "arbitrary") if plain else _cp("arbitrary", "arbitrary", "arbitrary"),
    )(*operands)
    return res


def _mm_nn(name, a, b, out_dtype, residual=None, epi=None):
    m, kk = a.shape
    n = b.shape[1]
    tm, tn, tk = _pick(m, (1024, 512, 256)), _pick(n, (1024, 512)), _pick(kk, (1024, 512))
    ops = [a, b]
    specs = [pl.BlockSpec((tm, tk), lambda i, j, k: (i, k)), pl.BlockSpec((tk, tn), lambda i, j, k: (k, j))]
    if residual is not None:
        ops.append(residual)
        specs.append(pl.BlockSpec((tm, tn), lambda i, j, k: (i, j)))
    return _mm_core(name, ops, specs, pl.BlockSpec((tm, tn), lambda i, j, k: (i, j)),
                    jax.ShapeDtypeStruct((m, n), out_dtype), (m // tm, n // tn, kk // tk), kk // tk,
                    ((1,), (0,)), (tm, tn), residual is not None, epi=epi)


def _mm_nt(name, a, b, out_dtype, residual=None, epi=None):
    m, kk = a.shape
    n = b.shape[0]
    tm, tn, tk = _pick(m, (1024, 512, 256)), _pick(n, (1024, 512)), _pick(kk, (1024, 512))
    ops = [a, b]
    specs = [pl.BlockSpec((tm, tk), lambda i, j, k: (i, k)), pl.BlockSpec((tn, tk), lambda i, j, k: (j, k))]
    if residual is not None:
        ops.append(residual)
        specs.append(pl.BlockSpec((tm, tn), lambda i, j, k: (i, j)))
    return _mm_core(name, ops, specs, pl.BlockSpec((tm, tn), lambda i, j, k: (i, j)),
                    jax.ShapeDtypeStruct((m, n), out_dtype), (m // tm, n // tn, kk // tk), kk // tk,
                    ((1,), (1,)), (tm, tn), residual is not None, epi=epi)


def _mm_tn(name, a, b, out_dtype):
    kk, m = a.shape
    n = b.shape[1]
    tm, tn, tk = _pick(m, (1024, 512)), _pick(n, (1024, 512)), _pick(kk, (1024, 512, 256))
    specs = [pl.BlockSpec((tk, tm), lambda i, j, k: (k, i)), pl.BlockSpec((tk, tn), lambda i, j, k: (k, j))]
    return _mm_core(name, [a, b], specs, pl.BlockSpec((tm, tn), lambda i, j, k: (i, j)),
                    jax.ShapeDtypeStruct((m, n), out_dtype), (m // tm, n // tn, kk // tk), kk // tk,
                    ((0,), (0,)), (tm, tn), False)


def _mm_nn_slots(name, a, b_slots, out_dtype, side=None):
    m, kk = a.shape
    s, _, ns = b_slots.shape
    tm, tk = _pick(m, (1024, 512, 256)), _pick(kk, (1024, 512))
    specs = [pl.BlockSpec((tm, tk), lambda i, j, k: (i, k)), pl.BlockSpec((None, tk, ns), lambda i, j, k: (j, k, 0))]
    return _mm_core(name, [a, b_slots], specs, pl.BlockSpec((tm, ns), lambda i, j, k: (i, j)),
                    jax.ShapeDtypeStruct((m, s * ns), out_dtype), (m // tm, s, kk // tk), kk // tk,
                    ((1,), (0,)), (tm, ns), False, side)


def _mm_nt_slots(name, a, b_slots, out_dtype, side=None, epi=None):
    m = a.shape[0]
    s, n, ns = b_slots.shape
    tm, tn = _pick(m, (1024, 512, 256)), _pick(n, (1024, 512))
    specs = [pl.BlockSpec((tm, ns), lambda i, j, k: (i, k)), pl.BlockSpec((None, tn, ns), lambda i, j, k: (k, j, 0))]
    return _mm_core(name, [a, b_slots], specs, pl.BlockSpec((tm, tn), lambda i, j, k: (i, j)),
                    jax.ShapeDtypeStruct((m, n), out_dtype), (m // tm, n // tn, s), s,
                    ((1,), (1,)), (tm, tn), False, side, epi)


def _mm_tn_slots(name, a, b, s, out_dtype, side=None):
    kk, m = a.shape
    ns = b.shape[1] // s
    tm, tk = _pick(m, (1024, 512)), _pick(kk, (1024, 512, 256))
    specs = [pl.BlockSpec((tk, tm), lambda i, j, k: (k, i)), pl.BlockSpec((tk, ns), lambda i, j, k: (k, j))]
    return _mm_core(name, [a, b], specs, pl.BlockSpec((None, tm, ns), lambda i, j, k: (j, i, 0)),
                    jax.ShapeDtypeStruct((s, m, ns), out_dtype), (m // tm, s, kk // tk), kk // tk,
                    ((0,), (0,)), (tm, ns), False, side)


def _rms_fwd(name, x, g):
    r, d = x.shape
    tr = _pick(r, (1024, 512, 256))

    def body(x_ref, g_ref, o_ref):
        xv = x_ref[...]
        rs = lax.rsqrt(jnp.mean(xv * xv, axis=-1, keepdims=True) + EPS)
        o_ref[...] = (xv * rs * g_ref[...]).astype(o_ref.dtype)

    return pl.pallas_call(
        body, name=name, grid=(r // tr,),
        in_specs=[pl.BlockSpec((tr, d), lambda i: (i, 0)), pl.BlockSpec((1, d), lambda i: (0, 0))],
        out_specs=pl.BlockSpec((tr, d), lambda i: (i, 0)),
        out_shape=jax.ShapeDtypeStruct((r, d), BF16), compiler_params=_cp("parallel"),
    )(x, g)


def _rms_bwd(name, x, g, dh, dres):
    r, d = x.shape
    tr = _pick(r, (512, 256))
    has_res = dres is not None

    def body(*refs):
        if has_res:
            x_ref, g_ref, dh_ref, dr_ref, dx_ref, dg_ref = refs
        else:
            x_ref, g_ref, dh_ref, dx_ref, dg_ref = refs
        i = pl.program_id(0)

        @pl.when(i == 0)
        def _():
            dg_ref[...] = jnp.zeros_like(dg_ref)

        xv = x_ref[...]
        dhv = dh_ref[...].astype(F32)
        rs = lax.rsqrt(jnp.mean(xv * xv, axis=-1, keepdims=True) + EPS)
        xn = xv * rs
        dg_ref[...] += jnp.sum(dhv * xn, axis=0, keepdims=True)
        dn = dhv * g_ref[...]
        dx = rs * (dn - xn * jnp.mean(dn * xn, axis=-1, keepdims=True))
        if has_res:
            dx = dx + dr_ref[...]
        dx_ref[...] = dx

    row = pl.BlockSpec((tr, d), lambda i: (i, 0))
    vec = pl.BlockSpec((1, d), lambda i: (0, 0))
    ops = [x, g, dh] + ([dres] if has_res else [])
    return pl.pallas_call(
        body, name=name, grid=(r // tr,),
        in_specs=[row, vec, row] + ([row] if has_res else []),
        out_specs=(row, vec),
        out_shape=(jax.ShapeDtypeStruct((r, d), F32), jax.ShapeDtypeStruct((1, d), F32)),
        compiler_params=_cp("arbitrary"),
    )(*ops)


def _loss_head(x2, gf, target):
    r, d = x2.shape
    tr = _pick(r, (512, 256))

    def body(x_ref, g_ref, t_ref, dx_ref, dg_ref, ls_ref):
        i = pl.program_id(0)

        @pl.when(i == 0)
        def _():
            dg_ref[...] = jnp.zeros_like(dg_ref)
            ls_ref[...] = jnp.zeros_like(ls_ref)

        xv = x_ref[...]
        rs = lax.rsqrt(jnp.mean(xv * xv, axis=-1, keepdims=True) + EPS)
        xn = xv * rs
        e = xn * g_ref[...] - t_ref[...]
        ls_ref[...] += jnp.sum(e * e, axis=0, keepdims=True)
        dy = e * (1.0 / d)
        dg_ref[...] += jnp.sum(dy * xn, axis=0, keepdims=True)
        dn = dy * g_ref[...]
        dx_ref[...] = rs * (dn - xn * jnp.mean(dn * xn, axis=-1, keepdims=True))

    row = pl.BlockSpec((tr, d), lambda i: (i, 0))
    vec = pl.BlockSpec((1, d), lambda i: (0, 0))
    return pl.pallas_call(
        body, name="loss_head", grid=(r // tr,), in_specs=[row, vec, row], out_specs=(row, vec, vec),
        out_shape=(jax.ShapeDtypeStruct((r, d), F32), jax.ShapeDtypeStruct((1, d), F32),
                   jax.ShapeDtypeStruct((1, d), F32)),
        compiler_params=_cp("arbitrary"),
    )(x2, gf, target)


def _rope_tables(pos_col, inv_row):
    l = pos_col.shape[0]
    tl = _pick(l, (1024, 512, 256))

    def body(p_ref, inv_ref, cos_ref, sin_ref):
        ang = p_ref[...].astype(F32) * inv_ref[...]
        lane = lax.broadcasted_iota(jnp.int32, ang.shape, 1)
        c = jnp.cos(ang)
        s = jnp.where((lane % RET_DK) < RET_DK // 2, -jnp.sin(ang), jnp.sin(ang))
        cos_ref[...] = jnp.tile(c, (1, RET_QK // 128))
        sin_ref[...] = jnp.tile(s, (1, RET_QK // 128))

    return pl.pallas_call(
        body, name="rope_tables", grid=(l // tl,),
        in_specs=[pl.BlockSpec((tl, 1), lambda i: (i, 0)), pl.BlockSpec((1, 128), lambda i: (0, 0))],
        out_specs=(pl.BlockSpec((tl, RET_QK), lambda i: (i, 0)), pl.BlockSpec((tl, RET_QK), lambda i: (i, 0))),
        out_shape=(jax.ShapeDtypeStruct((l, RET_QK), F32), jax.ShapeDtypeStruct((l, RET_QK), F32)),
        compiler_params=_cp("parallel"),
    )(pos_col, inv_row)


def _rot(x, cos_t, sin_t):
    n = x.shape[-1]
    lane = lax.broadcasted_iota(jnp.int32, x.shape, 1)
    partner = jnp.where((lane % RET_DK) < RET_DK // 2, pltpu.roll(x, n - RET_DK // 2, 1), pltpu.roll(x, RET_DK // 2, 1))
    return x * cos_t + partner * sin_t


def _ret_constants(c):
    log_g = jnp.log1p(-jnp.exp2(-5.0 - jnp.arange(RET_HEADS, dtype=F32)))
    j = jnp.arange(c, dtype=F32)
    diff = j[:, None] - j[None, :]
    decay = jnp.where(diff[None] >= 0.0, jnp.exp(log_g[:, None, None] * jnp.maximum(diff, 0.0)[None]), 0.0)
    q_w = jnp.exp(log_g[None, :] * (j + 1.0)[:, None])
    k_w = jnp.exp(log_g[None, :] * (c - 1.0 - j)[:, None])
    cd = jnp.exp(log_g * c)
    rep = lambda t: jnp.repeat(t, RET_DK, axis=1)
    cd_row = jnp.repeat(cd, RET_DV)[None, :]
    return decay, rep(q_w), rep(k_w), cd_row


def _pair_of(h, c):
    lane = lax.broadcasted_iota(jnp.int32, (c, 2 * RET_DK), 1)
    mine = (lane < RET_DK) if h % 2 == 0 else (lane >= RET_DK)
    return slice((h // 2) * 2 * RET_DK, (h // 2 + 1) * 2 * RET_DK), mine


def _keep(x, mine):
    return jnp.where(mine, x, jnp.zeros_like(x))


def _ret_fwd(proj, cos_t, sin_t, consts, gn_g, c):
    l = proj.shape[0]
    nc = l // c
    decay, qw, kw, cd_row = consts

    def body(q_ref, k_ref, v_ref, g_ref, cos_ref, sin_ref, dec_ref, qw_ref, kw_ref, cd_ref, gn_ref,
             ret_ref, o_ref, rp_ref, qb_ref, kb_ref, state):
        @pl.when(pl.program_id(0) == 0)
        def _():
            state[...] = jnp.zeros_like(state)

        cs, sn = cos_ref[...], sin_ref[...]
        qr = _rot(q_ref[...].astype(F32), cs, sn)
        kr = _rot(k_ref[...].astype(F32), cs, sn) * (RET_DK ** -0.5)
        qb, kb = qr.astype(BF16), kr.astype(BF16)
        qb_ref[...] = qb
        kb_ref[...] = kb
        qwb = (qr * qw_ref[...]).astype(BF16)
        kwb = (kr * kw_ref[...]).astype(BF16)
        vb = v_ref[...].astype(BF16)
        for h in range(RET_HEADS):
            ps, mine = _pair_of(h, c)
            vs = slice(h * RET_DV, (h + 1) * RET_DV)
            s = _dot_nt(_keep(qb[:, ps], mine), kb[:, ps]) * dec_ref[h]
            r_prev = state[h]
            rp_ref[0, h] = r_prev
            o = _dot(s.astype(BF16), vb[:, vs]) + _dot(_keep(qwb[:, ps], mine), r_prev.astype(BF16))
            state[h] = cd_ref[:, vs] * r_prev + _dot_tn(_keep(kwb[:, ps], mine), vb[:, vs])
            o_ref[:, vs] = o
            mu = jnp.mean(o, axis=-1, keepdims=True)
            var = jnp.mean(jnp.square(o - mu), axis=-1, keepdims=True)
            on = (o - mu) * lax.rsqrt(var + EPS)
            ret_ref[:, vs] = (on * gn_ref[:, vs] * _silu(g_ref[:, vs].astype(F32))).astype(ret_ref.dtype)

    const2 = lambda shape: pl.BlockSpec(shape, lambda i: (0,) * len(shape))
    return pl.pallas_call(
        body, name="retention_fwd", grid=(nc,),
        in_specs=[pl.BlockSpec((c, RET_QK), lambda i: (i, 0)), pl.BlockSpec((c, RET_QK), lambda i: (i, 1)),
                  pl.BlockSpec((c, D_MODEL), lambda i: (i, 1)), pl.BlockSpec((c, D_MODEL), lambda i: (i, 2)),
                  pl.BlockSpec((c, RET_QK), lambda i: (i, 0)), pl.BlockSpec((c, RET_QK), lambda i: (i, 0)),
                  const2((RET_HEADS, c, c)), const2((c, RET_QK)), const2((c, RET_QK)), const2((1, D_MODEL)),
                  const2((1, D_MODEL))],
        out_specs=(pl.BlockSpec((c, D_MODEL), lambda i: (i, 0)), pl.BlockSpec((c, D_MODEL), lambda i: (i, 0)),
                   pl.BlockSpec((1, RET_HEADS, 2 * RET_DK, RET_DV), lambda i: (i, 0, 0, 0)),
                   pl.BlockSpec((c, RET_QK), lambda i: (i, 0)), pl.BlockSpec((c, RET_QK), lambda i: (i, 0))),
        out_shape=(jax.ShapeDtypeStruct((l, 2 * D_MODEL), BF16), jax.ShapeDtypeStruct((l, D_MODEL), F32),
                   jax.ShapeDtypeStruct((nc, RET_HEADS, 2 * RET_DK, RET_DV), F32),
                   jax.ShapeDtypeStruct((l, RET_QK), BF16), jax.ShapeDtypeStruct((l, RET_QK), BF16)),
        scratch_shapes=[pltpu.VMEM((RET_HEADS, 2 * RET_DK, RET_DV), F32)],
        compiler_params=_cp("arbitrary"),
    )(proj, proj, proj, proj, cos_t, sin_t, decay, qw, kw, cd_row, gn_g)


def _ret_bwd(proj, qb_saved, kb_saved, cos_t, sin_t, consts, gn_g, o_saved, r_prev_saved, dmix, c, side):
    l = proj.shape[0]
    nc = l // c
    decay, qw, kw, cd_row = consts
    n_in = 14

    def body(*refs):
        (q_ref, k_ref, v_ref, g_ref, cos_ref, sin_ref, dec_ref, qw_ref, kw_ref, cd_ref, gn_ref, o_ref, rp_ref,
         dr_ref) = refs[:n_in]
        side_in = refs[n_in:n_in + len(side.srcs)]
        out_ref, dgn_ref = refs[n_in + len(side.srcs):n_in + len(side.srcs) + 2]
        side_out = refs[n_in + len(side.srcs) + 2:n_in + len(side.srcs) + 2 + side.n]
        state, dq_s, dk_s = refs[n_in + len(side.srcs) + 2 + side.n:n_in + len(side.srcs) + 5 + side.n]
        sems = refs[n_in + len(side.srcs) + 5 + side.n:]

        @pl.when(pl.program_id(0) == 0)
        def _():
            side.start(side_in, side_out, sems)
            state[...] = jnp.zeros_like(state)
            dgn_ref[...] = jnp.zeros_like(dgn_ref)

        cs, sn = cos_ref[...], sin_ref[...]
        qb, kb = q_ref[...], k_ref[...]
        qwv, kwv = qw_ref[...], kw_ref[...]
        qwb = (qb.astype(F32) * qwv).astype(BF16)
        kwb = (kb.astype(F32) * kwv).astype(BF16)
        vb = v_ref[...].astype(BF16)
        dq2 = dk2 = None
        for h in range(RET_HEADS):
            ps, mine = _pair_of(h, c)
            vs = slice(h * RET_DV, (h + 1) * RET_DV)
            dec = dec_ref[h]
            qm, km = _keep(qb[:, ps], mine), _keep(kb[:, ps], mine)
            o = o_ref[:, vs]
            mu = jnp.mean(o, axis=-1, keepdims=True)
            var = jnp.mean(jnp.square(o - mu), axis=-1, keepdims=True)
            rstd = lax.rsqrt(var + EPS)
            on = (o - mu) * rstd
            gate = g_ref[:, vs].astype(F32)
            sg = _silu(gate)
            dret = dr_ref[:, vs].astype(F32)
            gn = gn_ref[:, vs]
            dgn_ref[:, vs] += jnp.sum(dret * on * sg, axis=0, keepdims=True)
            out_ref[:, 2 * RET_QK + D_MODEL + h * RET_DV:2 * RET_QK + D_MODEL + (h + 1) * RET_DV] = (
                dret * on * gn * _dsilu(gate)).astype(out_ref.dtype)
            don = dret * gn * sg
            do = rstd * (don - jnp.mean(don, axis=-1, keepdims=True)
                         - on * jnp.mean(don * on, axis=-1, keepdims=True))
            dob = do.astype(BF16)
            sn_h = state[h]
            snb = sn_h.astype(BF16)
            s = _dot_nt(qm, kb[:, ps]) * dec
            dv = _dot_tn(s.astype(BF16), dob) + _dot(_keep(kwb[:, ps], mine), snb)
            out_ref[:, 2 * RET_QK + h * RET_DV:2 * RET_QK + (h + 1) * RET_DV] = dv.astype(out_ref.dtype)
            ds = (_dot_nt(dob, vb[:, vs]) * dec).astype(BF16)
            dq_h = _dot(ds, km) + qwv[:, ps] * _dot_nt(dob, rp_ref[0, h].astype(BF16))
            dk_h = _dot_tn(ds, qm) + kwv[:, ps] * _dot_nt(vb[:, vs], snb)
            state[h] = cd_ref[:, vs] * sn_h + _dot_tn(_keep(qwb[:, ps], mine), dob)
            if h % 2 == 0:
                dq2, dk2 = dq_h, dk_h
            else:
                dq_s[:, ps] = dq2 + dq_h
                dk_s[:, ps] = dk2 + dk_h
        out_ref[:, 0:RET_QK] = _rot(dq_s[...], cs, -sn).astype(out_ref.dtype)
        out_ref[:, RET_QK:2 * RET_QK] = (_rot(dk_s[...], cs, -sn) * (RET_DK ** -0.5)).astype(out_ref.dtype)

        @pl.when(pl.program_id(0) == nc - 1)
        def _():
            side.wait(side_in, side_out, sems)

    rev = lambda i: nc - 1 - i
    const2 = lambda shape: pl.BlockSpec(shape, lambda i: (0,) * len(shape))
    return pl.pallas_call(
        body, name="retention_bwd", grid=(nc,),
        in_specs=[pl.BlockSpec((c, RET_QK), lambda i: (rev(i), 0)), pl.BlockSpec((c, RET_QK), lambda i: (rev(i), 0)),
                  pl.BlockSpec((c, D_MODEL), lambda i: (rev(i), 1)), pl.BlockSpec((c, D_MODEL), lambda i: (rev(i), 2)),
                  pl.BlockSpec((c, RET_QK), lambda i: (rev(i), 0)), pl.BlockSpec((c, RET_QK), lambda i: (rev(i), 0)),
                  const2((RET_HEADS, c, c)), const2((c, RET_QK)), const2((c, RET_QK)), const2((1, D_MODEL)),
                  const2((1, D_MODEL)),
                  pl.BlockSpec((c, D_MODEL), lambda i: (rev(i), 0)),
                  pl.BlockSpec((1, RET_HEADS, 2 * RET_DK, RET_DV), lambda i: (rev(i), 0, 0, 0)),
                  pl.BlockSpec((c, D_MODEL), lambda i: (rev(i), 0))] + side.in_specs,
        out_specs=(pl.BlockSpec((c, 2 * RET_QK + 2 * D_MODEL), lambda i: (rev(i), 0)), const2((1, D_MODEL)),
                   *side.out_specs),
        out_shape=(jax.ShapeDtypeStruct((l, 2 * RET_QK + 4 * D_MODEL), BF16), jax.ShapeDtypeStruct((1, D_MODEL), F32),
                   *side.landing),
        scratch_shapes=[pltpu.VMEM((RET_HEADS, 2 * RET_DK, RET_DV), F32), pltpu.VMEM((c, RET_QK), F32),
                        pltpu.VMEM((c, RET_QK), F32)] + side.scratch,
        compiler_params=_cp("arbitrary"),
    )(qb_saved, kb_saved, proj, proj, cos_t, sin_t, decay, qw, kw, cd_row, gn_g, o_saved, r_prev_saved, dmix,
      *side.srcs)


def _zoh(a_re, a_im, log_dt):
    dt = jnp.exp(log_dt)
    mag = jnp.exp(a_re * dt)
    abar_re = mag * jnp.cos(a_im * dt)
    abar_im = mag * jnp.sin(a_im * dt)
    den = a_re * a_re + a_im * a_im
    nr, ni = abar_re - 1.0, abar_im
    f_re = (nr * a_re + ni * a_im) / den
    f_im = (ni * a_re - nr * a_im) / den
    return dt, abar_re, abar_im, f_re, f_im, den


def _lanes_p(f):
    return jnp.tile(f, (1, S5_P))


def _s5_discretize(a_re, a_im, log_dt, b_re_t, b_im_t):
    def body(ar_ref, ai_ref, ld_ref, br_ref, bi_ref, abr_ref, abi_ref, bbr_ref, bbi_ref):
        _, abar_re, abar_im, f_re, f_im, _ = _zoh(ar_ref[...], ai_ref[...], ld_ref[...])
        abr_ref[...] = abar_re
        abi_ref[...] = abar_im
        fr, fi = _lanes_p(f_re), _lanes_p(f_im)
        bbr_ref[...] = fr * br_ref[...] - fi * bi_ref[...]
        bbi_ref[...] = fr * bi_ref[...] + fi * br_ref[...]

    gn = jax.ShapeDtypeStruct((S5_G, S5_N), F32)
    gpn = jax.ShapeDtypeStruct((S5_G, S5_P * S5_N), F32)
    return pl.pallas_call(body, name="s5_discretize", out_shape=(gn, gn, gpn, gpn))(a_re, a_im, log_dt, b_re_t, b_im_t)


def _s5_discretize_bwd(a_re, a_im, log_dt, b_re_t, b_im_t, dab_re, dab_im, dbb_re_t, dbb_im_t):
    def body(ar_ref, ai_ref, ld_ref, br_ref, bi_ref, gar_ref, gai_ref, gbr_ref, gbi_ref,
             dar_ref, dai_ref, dld_ref, dbr_ref, dbi_ref):
        a_r, a_i = ar_ref[...], ai_ref[...]
        dt, abar_re, abar_im, f_re, f_im, den = _zoh(a_r, a_i, ld_ref[...])
        b_r, b_i, g_br, g_bi = br_ref[...], bi_ref[...], gbr_ref[...], gbi_ref[...]
        fr, fi = _lanes_p(f_re), _lanes_p(f_im)
        dbr_ref[...] = fr * g_br + fi * g_bi
        dbi_ref[...] = fr * g_bi - fi * g_br
        t_r = b_r * g_br + b_i * g_bi
        t_i = b_r * g_bi - b_i * g_br
        gf_r = sum(t_r[:, p * S5_N:(p + 1) * S5_N] for p in range(S5_P))
        gf_i = sum(t_i[:, p * S5_N:(p + 1) * S5_N] for p in range(S5_P))
        inv_r, inv_i = a_r / den, a_i / den
        ga_r = gar_ref[...] + gf_r * inv_r - gf_i * inv_i
        ga_i = gai_ref[...] + gf_r * inv_i + gf_i * inv_r
        q_r = -(f_re * a_r + f_im * a_i) / den
        q_i = -(f_im * a_r - f_re * a_i) / den
        gl_r = q_r * gf_r + q_i * gf_i
        gl_i = q_r * gf_i - q_i * gf_r
        dar_ref[...] = gl_r + dt * (abar_re * ga_r + abar_im * ga_i)
        dai_ref[...] = gl_i + dt * (abar_re * ga_i - abar_im * ga_r)
        la_r = a_r * abar_re - a_i * abar_im
        la_i = a_r * abar_im + a_i * abar_re
        dld_ref[...] = dt * jnp.sum(ga_r * la_r + ga_i * la_i, axis=-1, keepdims=True)

    gn = jax.ShapeDtypeStruct((S5_G, S5_N), F32)
    gpn = jax.ShapeDtypeStruct((S5_G, S5_P * S5_N), F32)
    return pl.pallas_call(
        body, name="s5_discretize_bwd", out_shape=(gn, gn, jax.ShapeDtypeStruct((S5_G, 1), F32), gpn, gpn),
    )(a_re, a_im, log_dt, b_re_t, b_im_t, dab_re, dab_im, dbb_re_t, dbb_im_t)


S5_ZQ = S5_NB // 2


def _s5_z(re, im):
    return jnp.concatenate([re.reshape(S5_ZQ, 8, 128), im.reshape(S5_ZQ, 8, 128)], axis=0)


def _s5_unz(z):
    return z[:S5_ZQ].reshape(S5_G, S5_N), z[S5_ZQ:].reshape(S5_G, S5_N)


def _s5_block_mats(bb_re, bb_im, c_re, c_im):
    eye = jnp.eye(S5_GB, dtype=F32)
    bb = jnp.stack([bb_re, bb_im], axis=0).reshape(2, S5_NB, S5_GB, S5_N, S5_P)
    bbm = jnp.einsum("rbgnp,gh->bgprhn", bb, eye).reshape(S5_NB, S5_GB * S5_P, 2 * S5_BS)
    cc = jnp.stack([c_re, -c_im], axis=0).reshape(2, S5_NB, S5_GB, S5_P, S5_N)
    ccm = jnp.einsum("rbgpn,gh->brhngp", cc, eye).reshape(S5_NB, 2 * S5_BS, S5_GB * S5_P)
    return bbm.astype(BF16), ccm.astype(BF16)


def _s5_block_diag_bb(m):
    t = m.reshape(S5_NB, S5_GB, S5_P, 2, S5_GB, S5_N)
    d = jnp.einsum("bgprgn->rbgnp", t).reshape(2, S5_G, S5_N, S5_P)
    return d[0], d[1]


def _s5_block_diag_cc(m):
    t = m.reshape(S5_NB, 2, S5_GB, S5_N, S5_GB, S5_P)
    d = jnp.einsum("brgngp->rbgpn", t).reshape(2, S5_G, S5_P, S5_N)
    return d[0], -d[1]


SCAN_UNROLL = 8


def _z_store(zr, zi, blk, res, t, off):
    q, h = blk // 2, blk % 2
    for lt in range(4):
        zr[q, pl.ds(off + 4 * h + lt, t, stride=8), :] = res[:, lt * 128:(lt + 1) * 128]
        zi[q, pl.ds(off + 4 * h + lt, t, stride=8), :] = res[:, S5_BS + lt * 128:S5_BS + (lt + 1) * 128]


def _z_load(zr, zi, blk, t, off):
    q, h = blk // 2, blk % 2
    return jnp.concatenate([zr[q, pl.ds(off + 4 * h + lt, t, stride=8), :] for lt in range(4)]
                           + [zi[q, pl.ds(off + 4 * h + lt, t, stride=8), :] for lt in range(4)], axis=1)


def _z_scan_fwd(zr, zi, a_ref, carry_ref, t, off):
    ar = [a_ref[q] for q in range(S5_ZQ)]
    ai = [a_ref[S5_ZQ + q] for q in range(S5_ZQ)]

    def step(it, carry):
        carry = list(carry)
        base = pl.multiple_of(it * (8 * SCAN_UNROLL), 8 * SCAN_UNROLL) + off
        for tt in range(SCAN_UNROLL):
            rows = pl.ds(base + 8 * tt, 8)
            for q in range(S5_ZQ):
                c_r, c_i = carry[q], carry[S5_ZQ + q]
                n_r = ar[q] * c_r - ai[q] * c_i + zr[q, rows, :]
                n_i = ar[q] * c_i + ai[q] * c_r + zi[q, rows, :]
                zr[q, rows, :] = n_r
                zi[q, rows, :] = n_i
                carry[q], carry[S5_ZQ + q] = n_r, n_i
        return tuple(carry)

    out = lax.fori_loop(0, t // SCAN_UNROLL, step, tuple(carry_ref[k] for k in range(2 * S5_ZQ)))
    for k in range(2 * S5_ZQ):
        carry_ref[k] = out[k]


def _z_scan_bwd(lr, li, xr, xi, a_ref, carry_ref, acc_ref, t):
    ar = [a_ref[q] for q in range(S5_ZQ)]
    ai = [a_ref[S5_ZQ + q] for q in range(S5_ZQ)]
    n_it = t // SCAN_UNROLL

    def step(it, state):
        carry, acc = list(state[0]), list(state[1])
        base = pl.multiple_of((n_it - 1 - it) * (8 * SCAN_UNROLL), 8 * SCAN_UNROLL)
        for tt in reversed(range(SCAN_UNROLL)):
            rows = pl.ds(base + 8 * tt, 8)
            for q in range(S5_ZQ):
                c_r, c_i = carry[q], carry[S5_ZQ + q]
                n_r = ar[q] * c_r + ai[q] * c_i + lr[q, rows, :]
                n_i = ar[q] * c_i - ai[q] * c_r + li[q, rows, :]
                lr[q, rows, :] = n_r
                li[q, rows, :] = n_i
                p_r, p_i = xr[q, rows, :], xi[q, rows, :]
                acc[q] = acc[q] + n_r * p_r + n_i * p_i
                acc[S5_ZQ + q] = acc[S5_ZQ + q] + n_i * p_r - n_r * p_i
                carry[q], carry[S5_ZQ + q] = n_r, n_i
        return tuple(carry), tuple(acc)

    k8 = range(2 * S5_ZQ)
    carry, acc = lax.fori_loop(0, n_it, step, (tuple(carry_ref[k] for k in k8), tuple(acc_ref[k] for k in k8)))
    for k in k8:
        carry_ref[k] = carry[k]
        acc_ref[k] = acc[k]


def _s5_fwd(proj, mix, bbm, ccm, d_row, glu_w, glu_b, tabs, t, side):
    l = proj.shape[0]
    nt = l // t
    n_in = 9

    def body(*refs):
        u_ref, gs_ref, bb_ref, cc_ref, d_ref, gw_ref, gb_ref, a_ref, _ = refs[:n_in]
        side_in = refs[n_in:n_in + len(side.srcs)]
        ssm_ref, xst_ref = refs[n_in + len(side.srcs):n_in + len(side.srcs) + 2]
        side_out = refs[n_in + len(side.srcs) + 2:n_in + len(side.srcs) + 2 + side.n]
        zr, zi, carry = refs[n_in + len(side.srcs) + 2 + side.n:n_in + len(side.srcs) + 5 + side.n]
        sems = refs[n_in + len(side.srcs) + 5 + side.n:]

        @pl.when(pl.program_id(0) == 0)
        def _():
            side.start(side_in, side_out, sems)
            carry[...] = jnp.zeros_like(carry)

        xst_ref[0] = carry[...]
        ub = u_ref[...]
        u = ub.astype(F32)
        for blk in range(S5_NB):
            _z_store(zr, zi, blk, _dot(ub[:, blk * 128:(blk + 1) * 128], bb_ref[blk]), t, 0)
        _z_scan_fwd(zr, zi, a_ref, carry, t, 0)
        ys = jnp.concatenate(
            [_dot(_z_load(zr, zi, blk, t, 0).astype(BF16), cc_ref[blk]) for blk in range(S5_NB)], axis=1)
        y2 = _gelu(ys + d_ref[...] * u)
        z = _dot(y2.astype(BF16), gw_ref[...]) + gb_ref[...]
        ssm_ref[...] = (y2 * _sigmoid(z) * _silu(gs_ref[...].astype(F32))).astype(ssm_ref.dtype)

        @pl.when(pl.program_id(0) == nt - 1)
        def _():
            side.wait(side_in, side_out, sems)

    const2 = lambda shape: pl.BlockSpec(shape, lambda i: (0,) * len(shape))
    zshape = (2 * S5_ZQ, 8, 128)
    return pl.pallas_call(
        body, name="s5_fwd", grid=(nt,),
        in_specs=[pl.BlockSpec((t, D_MODEL), lambda i: (i, 3)), pl.BlockSpec((t, D_MODEL), lambda i: (i, 4)),
                  const2(bbm.shape), const2(ccm.shape), const2((1, D_MODEL)), const2((D_MODEL, D_MODEL)),
                  const2((1, D_MODEL)), const2(zshape), pl.BlockSpec(memory_space=pl.ANY)] + side.in_specs,
        out_specs=(pl.BlockSpec((t, D_MODEL), lambda i: (i, 1)), pl.BlockSpec((1,) + zshape, lambda i: (i, 0, 0, 0)),
                   *side.out_specs),
        out_shape=(jax.ShapeDtypeStruct((l, 2 * D_MODEL), BF16), jax.ShapeDtypeStruct((nt,) + zshape, F32),
                   *side.landing),
        scratch_shapes=[pltpu.VMEM((S5_ZQ, 8 * t, 128), F32), pltpu.VMEM((S5_ZQ, 8 * t, 128), F32),
                        pltpu.VMEM(zshape, F32)] + side.scratch,
        input_output_aliases={8: 0},
        compiler_params=_cp("arbitrary"),
    )(proj, proj, bbm, ccm, d_row, glu_w, glu_b, tabs, mix, *side.srcs)


def _s5_bwd(proj, dmix, dproj, xstart, bbm, ccm, d_row, glu_w, glu_b, tabs, t):
    l = proj.shape[0]
    nt = l // t
    col0 = 2 * RET_QK + 2 * D_MODEL

    def body(u_ref, gs_ref, dm_ref, xst_ref, bb_ref, cc_ref, d_ref, gw_ref, gb_ref, a_ref, _,
             dp_ref, y2_ref, dz_ref, dbb_ref, dcc_ref, da_ref, dd_ref, dgb_ref, xr, xi, lr, li, carry, lcarry,
             dug_s, dug_sem):
        step = pl.program_id(0)
        slot = step % 2
        dug_ref = dug_s.at[slot]

        def put(s, at_step):
            rows = pl.ds(pl.multiple_of((nt - 1 - at_step) * t, t), t)
            return pltpu.make_async_copy(dug_s.at[s], dp_ref.at[rows, pl.ds(col0, 2 * D_MODEL)], dug_sem.at[s])

        @pl.when(step >= 2)
        def _():
            put(slot, step - 2).wait()

        @pl.when(step == 0)
        def _():
            lcarry[...] = jnp.zeros_like(lcarry)
            dbb_ref[...] = jnp.zeros_like(dbb_ref)
            dcc_ref[...] = jnp.zeros_like(dcc_ref)
            da_ref[...] = jnp.zeros_like(da_ref)
            dd_ref[...] = jnp.zeros_like(dd_ref)
            dgb_ref[...] = jnp.zeros_like(dgb_ref)

        carry[...] = xst_ref[0]
        for q in range(S5_ZQ):
            xr[q, 0:8, :] = carry[q]
            xi[q, 0:8, :] = carry[S5_ZQ + q]
        ub = u_ref[...]
        u = ub.astype(F32)
        for blk in range(S5_NB):
            _z_store(xr, xi, blk, _dot(ub[:, blk * 128:(blk + 1) * 128], bb_ref[blk]), t, 8)
        _z_scan_fwd(xr, xi, a_ref, carry, t, 8)
        ys = jnp.concatenate(
            [_dot(_z_load(xr, xi, blk, t, 8).astype(BF16), cc_ref[blk]) for blk in range(S5_NB)], axis=1)
        dv = d_ref[...]
        y1 = ys + dv * u
        y2 = _gelu(y1)
        y2b = y2.astype(BF16)
        sg = _sigmoid(_dot(y2b, gw_ref[...]) + gb_ref[...])
        gs = gs_ref[...].astype(F32)
        dssm = dm_ref[...].astype(F32)
        dug_ref[:, D_MODEL:] = (dssm * (y2 * sg) * _dsilu(gs)).astype(dug_ref.dtype)
        dy3 = dssm * _silu(gs)
        dz = dy3 * y2 * sg * (1.0 - sg)
        dzb = dz.astype(BF16)
        y2_ref[...] = y2b
        dz_ref[...] = dzb
        dgb_ref[...] += jnp.sum(dz, axis=0, keepdims=True)
        dy1 = (dy3 * sg + _dot_nt(dzb, gw_ref[...])) * _dgelu(y1)
        dd_ref[...] += jnp.sum(dy1 * u, axis=0, keepdims=True)
        dyb = dy1.astype(BF16)
        for blk in range(S5_NB):
            ch = slice(blk * 128, (blk + 1) * 128)
            _z_store(lr, li, blk, _dot_nt(dyb[:, ch], cc_ref[blk]), t, 0)
            dcc_ref[blk] += _dot_tn(_z_load(xr, xi, blk, t, 8).astype(BF16), dyb[:, ch])
        _z_scan_bwd(lr, li, xr, xi, a_ref, lcarry, da_ref, t)
        du = []
        for blk in range(S5_NB):
            lb = _z_load(lr, li, blk, t, 0).astype(BF16)
            du.append(_dot_nt(lb, bb_ref[blk]))
            dbb_ref[blk] += _dot_tn(ub[:, blk * 128:(blk + 1) * 128], lb)
        dug_ref[:, :D_MODEL] = (jnp.concatenate(du, axis=1) + dy1 * dv).astype(dug_ref.dtype)
        put(slot, step).start()

        @pl.when(step == nt - 1)
        def _():
            put(slot, step).wait()
            if nt > 1:
                put(1 - slot, step - 1).wait()

    rev = lambda i: nt - 1 - i
    const2 = lambda shape: pl.BlockSpec(shape, lambda i: (0,) * len(shape))
    row_out = lambda w: pl.BlockSpec((t, w), lambda i: (rev(i), 0))
    zshape = (2 * S5_ZQ, 8, 128)
    hbm = pl.BlockSpec(memory_space=pl.ANY)
    return pl.pallas_call(
        body, name="s5_bwd", grid=(nt,),
        in_specs=[pl.BlockSpec((t, D_MODEL), lambda i: (rev(i), 3)), pl.BlockSpec((t, D_MODEL), lambda i: (rev(i), 4)),
                  pl.BlockSpec((t, D_MODEL), lambda i: (rev(i), 1)),
                  pl.BlockSpec((1,) + zshape, lambda i: (rev(i), 0, 0, 0)),
                  const2(bbm.shape), const2(ccm.shape), const2((1, D_MODEL)), const2((D_MODEL, D_MODEL)),
                  const2((1, D_MODEL)), const2(zshape), hbm],
        out_specs=(hbm, row_out(D_MODEL), row_out(D_MODEL), const2(bbm.shape), const2(ccm.shape),
                   const2(zshape), const2((1, D_MODEL)), const2((1, D_MODEL))),
        out_shape=(jax.ShapeDtypeStruct(dproj.shape, BF16), jax.ShapeDtypeStruct((l, D_MODEL), BF16),
                   jax.ShapeDtypeStruct((l, D_MODEL), BF16), jax.ShapeDtypeStruct(bbm.shape, F32),
                   jax.ShapeDtypeStruct(ccm.shape, F32), jax.ShapeDtypeStruct(zshape, F32),
                   jax.ShapeDtypeStruct((1, D_MODEL), F32), jax.ShapeDtypeStruct((1, D_MODEL), F32)),
        scratch_shapes=[pltpu.VMEM((S5_ZQ, 8 * t + 8, 128), F32), pltpu.VMEM((S5_ZQ, 8 * t + 8, 128), F32),
                        pltpu.VMEM((S5_ZQ, 8 * t, 128), F32), pltpu.VMEM((S5_ZQ, 8 * t, 128), F32),
                        pltpu.VMEM(zshape, F32), pltpu.VMEM(zshape, F32),
                        pltpu.VMEM((2, t, 2 * D_MODEL), BF16), pltpu.SemaphoreType.DMA((2,))],
        input_output_aliases={10: 0},
        compiler_params=_cp("arbitrary"),
    )(proj, proj, dmix, xstart, bbm, ccm, d_row, glu_w, glu_b, tabs, dproj)


def _attn_probs(qh, kh):
    s = _dot_nt(qh, kh) * (XA_DH ** -0.5)
    e = jnp.exp(s - jnp.max(s, axis=-1, keepdims=True))
    return e / jnp.sum(e, axis=-1, keepdims=True)


def _attn_fwd(qa, ka, va):
    l = qa.shape[0]
    m = ka.shape[0]
    tl = _pick(l, (512, 256))

    def body(q_ref, k_ref, v_ref, o_ref):
        for h in range(XA_HEADS):
            hs = slice(h * XA_DH, (h + 1) * XA_DH)
            p = _attn_probs(q_ref[:, hs], k_ref[:, hs])
            o_ref[:, hs] = _dot(p.astype(BF16), v_ref[:, hs]).astype(o_ref.dtype)

    return pl.pallas_call(
        body, name="xattn_fwd", grid=(l // tl,),
        in_specs=[pl.BlockSpec((tl, D_MODEL), lambda i: (i, 0)), pl.BlockSpec((m, D_MODEL), lambda i: (0, 0)),
                  pl.BlockSpec((m, D_MODEL), lambda i: (0, 0))],
        out_specs=pl.BlockSpec((tl, D_MODEL), lambda i: (i, 0)),
        out_shape=jax.ShapeDtypeStruct((l, D_MODEL), BF16), compiler_params=_cp("parallel"),
    )(qa, ka, va)


def _attn_bwd(qa, ka, va, doa):
    l = qa.shape[0]
    m = ka.shape[0]
    tl = _pick(l, (512, 256))

    def body(q_ref, k_ref, v_ref, do_ref, dq_ref, dk_ref, dv_ref):
        @pl.when(pl.program_id(0) == 0)
        def _():
            dk_ref[...] = jnp.zeros_like(dk_ref)
            dv_ref[...] = jnp.zeros_like(dv_ref)

        for h in range(XA_HEADS):
            hs = slice(h * XA_DH, (h + 1) * XA_DH)
            qh, kh, vh, doh = q_ref[:, hs], k_ref[:, hs], v_ref[:, hs], do_ref[:, hs]
            p = _attn_probs(qh, kh)
            dv_ref[:, hs] += _dot_tn(p.astype(BF16), doh)
            dp = _dot_nt(doh, vh)
            ds = (p * (dp - jnp.sum(dp * p, axis=-1, keepdims=True)) * (XA_DH ** -0.5)).astype(BF16)
            dq_ref[:, hs] = _dot(ds, kh).astype(dq_ref.dtype)
            dk_ref[:, hs] += _dot_tn(ds, qh)

    row = pl.BlockSpec((tl, D_MODEL), lambda i: (i, 0))
    mem = pl.BlockSpec((m, D_MODEL), lambda i: (0, 0))
    return pl.pallas_call(
        body, name="xattn_bwd", grid=(l // tl,), in_specs=[row, mem, mem, row], out_specs=(row, mem, mem),
        out_shape=(jax.ShapeDtypeStruct((l, D_MODEL), BF16), jax.ShapeDtypeStruct((m, D_MODEL), F32),
                   jax.ShapeDtypeStruct((m, D_MODEL), F32)),
        compiler_params=_cp("arbitrary"),
    )(qa, ka, va, doa)


def _me_and_peers():
    x, y, c = lax.axis_index("x"), lax.axis_index("y"), lax.axis_index("c")
    flip = lambda v, bit: (1 - v) if bit else v
    peers = []
    for k in range(1, N_DEV):
        px, py, pc = flip(x, (k >> 2) & 1), flip(y, (k >> 1) & 1), flip(c, k & 1)
        peers.append(((px, py, pc), 4 * px + 2 * py + pc))
    return 4 * x + 2 * y + c, peers


class _SideJob:
    def __init__(self, srcs, landing, src_of, dst_of):
        self.srcs = list(srcs)
        self.landing = list(landing)
        self.n = len(self.landing)
        self.src_of, self.dst_of = src_of, dst_of
        hbm = pl.BlockSpec(memory_space=pl.ANY)
        self.in_specs = [hbm] * len(self.srcs)
        self.out_specs = [hbm] * self.n
        self.scratch = [pltpu.SemaphoreType.DMA((self.n * (N_DEV - 1),)), pltpu.SemaphoreType.DMA((self.n * (N_DEV - 1),)),
                        pltpu.SemaphoreType.DMA((self.n,))]

    def _copies(self, src_refs, out_refs, sems):
        send_sems, recv_sems, loc_sems = sems
        me, peers = _me_and_peers()
        local = [pltpu.make_async_copy(self.src_of(a, me, src_refs), self.dst_of(a, me, out_refs), loc_sems.at[a])
                 for a in range(self.n)]
        sends, recvs = [], []
        for k, (peer, peer_idx) in enumerate(peers):
            for a in range(self.n):
                s = self.n * k + a
                sends.append(pltpu.make_async_remote_copy(
                    src_ref=self.src_of(a, peer_idx, src_refs), dst_ref=self.dst_of(a, me, out_refs),
                    send_sem=send_sems.at[s], recv_sem=recv_sems.at[s], device_id=peer, device_id_type=MESH))
                recvs.append(pltpu.make_async_remote_copy(
                    src_ref=self.src_of(a, me, src_refs), dst_ref=self.dst_of(a, peer_idx, out_refs),
                    send_sem=send_sems.at[s], recv_sem=recv_sems.at[s], device_id=peer, device_id_type=MESH))
        return local, sends, recvs

    def start(self, src_refs, out_refs, sems):
        if not self.n:
            return
        local, sends, _ = self._copies(src_refs, out_refs, sems)
        for cp in local + sends:
            cp.start()

    def wait(self, src_refs, out_refs, sems):
        if not self.n:
            return
        local, sends, recvs = self._copies(src_refs, out_refs, sems)
        for cp in recvs:
            cp.wait_recv()
        for cp in sends:
            cp.wait_send()
        for cp in local:
            cp.wait()


def _gather_job(shards):
    return _SideJob(shards, [jax.ShapeDtypeStruct((N_DEV,) + s.shape, s.dtype) for s in shards],
                    src_of=lambda a, j, srcs: srcs[a], dst_of=lambda a, j, outs: outs[a].at[j])


def _scatter_job(grads):
    landing, parts = [], []
    for g in grads:
        if g.ndim == 3:
            landing.append(jax.ShapeDtypeStruct(g.shape, g.dtype))
            parts.append(None)
        else:
            r = g.shape[0] // N_DEV
            landing.append(jax.ShapeDtypeStruct((N_DEV, r, g.shape[1]), g.dtype))
            parts.append(r)

    def src_of(a, j, srcs):
        if parts[a] is None:
            return srcs[a].at[j]
        return srcs[a].at[pl.ds(pl.multiple_of(j * parts[a], 8), parts[a]), :]

    return _SideJob(grads, landing, src_of=src_of, dst_of=lambda a, j, outs: outs[a].at[j])


def _allgather_w_in(w_in_shard, row_shards):
    n_row = len(row_shards)

    def body(*refs):
        win_ref = refs[0]
        row_refs = refs[1:1 + n_row]
        out_win = refs[1 + n_row]
        row_outs = refs[2 + n_row:2 + 2 * n_row]
        win_b, send_sems, recv_sems, local_sem = refs[2 + 2 * n_row:]
        win_b[...] = win_ref[...].astype(BF16)
        x, y, c = lax.axis_index("x"), lax.axis_index("y"), lax.axis_index("c")
        me, sibling = (x, y, c), (x, y, 1 - c)
        chips = [(1 - x, y), (x, 1 - y), (1 - x, 1 - y)]
        slot = lambda p: out_win.at[4 * p[0] + 2 * p[1] + p[2]]

        def copy(k, block, to, src=None):
            return pltpu.make_async_remote_copy(
                src_ref=slot(block) if src is None else src, dst_ref=slot(block), send_sem=send_sems.at[k],
                recv_sem=recv_sems.at[k], device_id=to, device_id_type=MESH)

        mine = pltpu.make_async_copy(win_b, slot(me), local_sem)
        mine.start()
        first = [copy(0, me, sibling, src=win_b)]
        first += [copy(1 + j, me, (*chip, c), src=win_b) for j, chip in enumerate(chips)]
        for cp in first:
            cp.start()
        for r, o in zip(row_refs, row_outs):
            o[...] = r[...].astype(BF16)
        passed = [copy(4 + j, (*chip, c), sibling) for j, chip in enumerate(chips)]
        for j, chip in enumerate(chips):
            copy(1 + j, (*chip, c), me).wait_recv()
            passed[j].start()
        copy(0, sibling, me).wait_recv()
        for j, chip in enumerate(chips):
            copy(4 + j, (*chip, 1 - c), me).wait_recv()
        for cp in first + passed:
            cp.wait_send()
        mine.wait()

    vm = pl.BlockSpec(memory_space=pltpu.VMEM)
    return pl.pallas_call(
        body, name="allgather_w_in", in_specs=[vm] * (1 + n_row),
        out_specs=(pl.BlockSpec(memory_space=pl.ANY), *([vm] * n_row)),
        out_shape=(jax.ShapeDtypeStruct((N_DEV,) + w_in_shard.shape, BF16),
                   *[jax.ShapeDtypeStruct(r.shape, BF16) for r in row_shards]),
        scratch_shapes=[pltpu.VMEM(w_in_shard.shape, BF16), pltpu.SemaphoreType.DMA((N_DEV - 1,)),
                        pltpu.SemaphoreType.DMA((N_DEV - 1,)), pltpu.SemaphoreType.DMA],
        compiler_params=pltpu.CompilerParams(vmem_limit_bytes=VMEM_LIMIT),
    )(w_in_shard, *row_shards)


def _allreduce_small(small):
    rows = SMALL_ROWS // N_DEV

    def body(x_ref, out_ref, land, send1, recv1, send2, recv2):
        me, peers = _me_and_peers()
        block = lambda j: pl.ds(pl.multiple_of(j * rows, 8), rows)

        def phase(src_of, dst_of, send_sems, recv_sems):
            sends = [pltpu.make_async_remote_copy(src_ref=src_of(pidx), dst_ref=dst_of(me), send_sem=send_sems.at[k],
                                                  recv_sem=recv_sems.at[k], device_id=peer, device_id_type=MESH)
                     for k, (peer, pidx) in enumerate(peers)]
            recvs = [pltpu.make_async_remote_copy(src_ref=src_of(me), dst_ref=dst_of(pidx), send_sem=send_sems.at[k],
                                                  recv_sem=recv_sems.at[k], device_id=peer, device_id_type=MESH)
                     for k, (peer, pidx) in enumerate(peers)]
            for cp in sends:
                cp.start()
            for cp in recvs:
                cp.wait_recv()
            for cp in sends:
                cp.wait_send()

        land[me] = x_ref[block(me), :]
        phase(lambda j: x_ref.at[block(j), :], lambda j: land.at[j], send1, recv1)
        total = land[0]
        for j in range(1, N_DEV):
            total = total + land[j]
        out_ref[block(me), :] = total
        phase(lambda j: out_ref.at[block(me), :], lambda j: out_ref.at[block(j), :], send2, recv2)

    vm = pl.BlockSpec(memory_space=pltpu.VMEM)
    return pl.pallas_call(
        body, name="allreduce_small", in_specs=[vm], out_specs=vm, out_shape=jax.ShapeDtypeStruct(small.shape, F32),
        scratch_shapes=[pltpu.VMEM((N_DEV, rows, D_MODEL), F32)] + [pltpu.SemaphoreType.DMA((N_DEV - 1,))] * 4,
    )(small)


def _adamw(name, got, w, m, v):
    r, c = w.shape
    n_slots = got.shape[0]
    tr = _pick(r, (256, 128, 64))

    def body(got_ref, w_ref, m_ref, v_ref, g_ref, d_ref, nm_ref, nv_ref):
        g = got_ref[0].astype(F32)
        for j in range(1, n_slots):
            g = g + got_ref[j].astype(F32)
        nm = ADAM_B1 * m_ref[...] + (1.0 - ADAM_B1) * g
        nv = ADAM_B2 * v_ref[...] + (1.0 - ADAM_B2) * jnp.square(g)
        m_hat = nm / (1.0 - ADAM_B1 ** ADAM_STEP)
        v_hat = nv / (1.0 - ADAM_B2 ** ADAM_STEP)
        g_ref[...] = g
        d_ref[...] = -ADAM_LR * (m_hat / (jnp.sqrt(v_hat) + ADAM_EPS) + ADAM_WD * w_ref[...])
        nm_ref[...] = nm
        nv_ref[...] = nv

    blk = pl.BlockSpec((tr, c), lambda i: (i, 0))
    out = jax.ShapeDtypeStruct((r, c), F32)
    return pl.pallas_call(
        body, name=name, grid=(r // tr,),
        in_specs=[pl.BlockSpec((n_slots, tr, c), lambda i: (0, i, 0)), blk, blk, blk],
        out_specs=(blk, blk, blk, blk), out_shape=(out, out, out, out), compiler_params=_cp("parallel"),
    )(got, w, m, v)


_SMALL_VECS = ("norm1_g", "ret_gn_g", "s5_d", "s5_glu_b", "norm2_g", "norm_mem_g", "norm_f_g")
_SMALL_ORDER = _SMALL_VECS + ("s5_a_re", "s5_a_im", "s5_log_dt", "s5_b_re", "s5_b_im", "s5_c_re", "s5_c_im")


def _small_layout():
    lay, row = {}, 0
    for n in _SMALL_VECS + ("loss",):
        lay[n] = (row, 1, D_MODEL)
        row += 1
    for n in ("s5_a_re", "s5_a_im"):
        lay[n] = (row, 4, D_MODEL)
        row += 4
    lay["s5_log_dt"] = (row, 1, S5_G)
    row += 8
    for n in ("s5_b_re", "s5_b_im", "s5_c_re", "s5_c_im"):
        lay[n] = (row, 64, D_MODEL)
        row += 64
    assert row <= SMALL_ROWS
    return lay


def _pack_small(t, loss_row=None):
    lay = _small_layout()
    pieces = [t[n].reshape(1, D_MODEL) for n in _SMALL_VECS]
    pieces.append(jnp.zeros((1, D_MODEL), F32) if loss_row is None else loss_row)
    pieces += [t["s5_a_re"].reshape(4, D_MODEL), t["s5_a_im"].reshape(4, D_MODEL)]
    pieces.append(jnp.pad(t["s5_log_dt"].reshape(1, S5_G), ((0, 7), (0, D_MODEL - S5_G))))
    pieces += [t[n].reshape(64, D_MODEL) for n in ("s5_b_re", "s5_b_im", "s5_c_re", "s5_c_im")]
    pieces.append(jnp.zeros((SMALL_ROWS - lay["s5_c_im"][0] - 64, D_MODEL), F32))
    return jnp.concatenate(pieces, axis=0)


def _adamw_small(g_sum, w, m, v):
    lay = _small_layout()
    names = [n for n in lay if n != "loss"]

    def body(g_ref, w_ref, m_ref, v_ref, *outs):
        g = g_ref[...]
        nm = ADAM_B1 * m_ref[...] + (1.0 - ADAM_B1) * g
        nv = ADAM_B2 * v_ref[...] + (1.0 - ADAM_B2) * jnp.square(g)
        m_hat = nm / (1.0 - ADAM_B1 ** ADAM_STEP)
        v_hat = nv / (1.0 - ADAM_B2 ** ADAM_STEP)
        delta = -ADAM_LR * (m_hat / (jnp.sqrt(v_hat) + ADAM_EPS) + ADAM_WD * w_ref[...])
        for i, n in enumerate(names):
            r0, rows, lanes = lay[n]
            for part, val in enumerate((g, delta, nm, nv)):
                outs[4 * i + part][...] = val[r0:r0 + rows, 0:lanes]
        r0 = lay["loss"][0]
        outs[-1][...] = g[r0:r0 + 1, :]

    shapes = []
    for n in names:
        shapes += [jax.ShapeDtypeStruct(lay[n][1:], F32)] * 4
    shapes.append(jax.ShapeDtypeStruct((1, D_MODEL), F32))
    outs = pl.pallas_call(body, name="adamw_small", out_shape=tuple(shapes),
                          compiler_params=pltpu.CompilerParams(vmem_limit_bytes=VMEM_LIMIT))(g_sum, w, m, v)
    return {n: tuple(outs[4 * i:4 * i + 4]) for i, n in enumerate(names)}, outs[-1]


_W_NAMES = ("norm1_g", "w_in", "ret_gn_g", "s5_a_re", "s5_a_im", "s5_log_dt", "s5_b_re", "s5_b_im", "s5_c_re", "s5_c_im",
            "s5_d", "s5_glu_w", "s5_glu_b", "w_out", "norm2_g", "norm_mem_g", "xa_wq", "xa_wk", "xa_wv", "xa_wo",
            "norm_f_g")
_ROW_NAMES = ("s5_glu_w", "w_out", "xa_wq", "xa_wk", "xa_wv", "xa_wo")


def kernel(x, mem, positions, norm1_g, w_in, ret_gn_g, s5_a_re, s5_a_im, s5_log_dt, s5_b_re, s5_b_im, s5_c_re, s5_c_im, s5_d, s5_glu_w, s5_glu_b, w_out, norm2_g, norm_mem_g, xa_wq, xa_wk, xa_wv, xa_wo, norm_f_g, loss_target, m_norm1_g, m_w_in, m_ret_gn_g, m_s5_a_re, m_s5_a_im, m_s5_log_dt, m_s5_b_re, m_s5_b_im, m_s5_c_re, m_s5_c_im, m_s5_d, m_s5_glu_w, m_s5_glu_b, m_w_out, m_norm2_g, m_norm_mem_g, m_xa_wq, m_xa_wk, m_xa_wv, m_xa_wo, m_norm_f_g, v_norm1_g, v_w_in, v_ret_gn_g, v_s5_a_re, v_s5_a_im, v_s5_log_dt, v_s5_b_re, v_s5_b_im, v_s5_c_re, v_s5_c_im, v_s5_d, v_s5_glu_w, v_s5_glu_b, v_w_out, v_norm2_g, v_norm_mem_g, v_xa_wq, v_xa_wk, v_xa_wv, v_xa_wo, v_norm_f_g):
    w = dict(norm1_g=norm1_g, w_in=w_in, ret_gn_g=ret_gn_g, s5_a_re=s5_a_re, s5_a_im=s5_a_im, s5_log_dt=s5_log_dt,
             s5_b_re=s5_b_re, s5_b_im=s5_b_im, s5_c_re=s5_c_re, s5_c_im=s5_c_im, s5_d=s5_d, s5_glu_w=s5_glu_w,
             s5_glu_b=s5_glu_b, w_out=w_out, norm2_g=norm2_g, norm_mem_g=norm_mem_g, xa_wq=xa_wq, xa_wk=xa_wk,
             xa_wv=xa_wv, xa_wo=xa_wo, norm_f_g=norm_f_g)
    mom = dict(norm1_g=m_norm1_g, w_in=m_w_in, ret_gn_g=m_ret_gn_g, s5_a_re=m_s5_a_re, s5_a_im=m_s5_a_im,
               s5_log_dt=m_s5_log_dt, s5_b_re=m_s5_b_re, s5_b_im=m_s5_b_im, s5_c_re=m_s5_c_re, s5_c_im=m_s5_c_im,
               s5_d=m_s5_d, s5_glu_w=m_s5_glu_w, s5_glu_b=m_s5_glu_b, w_out=m_w_out, norm2_g=m_norm2_g,
               norm_mem_g=m_norm_mem_g, xa_wq=m_xa_wq, xa_wk=m_xa_wk, xa_wv=m_xa_wv, xa_wo=m_xa_wo,
               norm_f_g=m_norm_f_g)
    var = dict(norm1_g=v_norm1_g, w_in=v_w_in, ret_gn_g=v_ret_gn_g, s5_a_re=v_s5_a_re, s5_a_im=v_s5_a_im,
               s5_log_dt=v_s5_log_dt, s5_b_re=v_s5_b_re, s5_b_im=v_s5_b_im, s5_c_re=v_s5_c_re, s5_c_im=v_s5_c_im,
               s5_d=v_s5_d, s5_glu_w=v_s5_glu_w, s5_glu_b=v_s5_glu_b, w_out=v_w_out, norm2_g=v_norm2_g,
               norm_mem_g=v_norm_mem_g, xa_wq=v_xa_wq, xa_wk=v_xa_wk, xa_wv=v_xa_wv, xa_wo=v_xa_wo,
               norm_f_g=v_norm_f_g)
    shapes = {n: w[n].shape for n in _W_NAMES}

    x2d, mem2d, tgt = x[0], mem[0], loss_target[0]
    l = x2d.shape[0]
    ret_c = _pick(l, (256, 128))
    s5_t = _pick(l, (256, 128))
    g1, g2, gm, gf = norm1_g, norm2_g, norm_mem_g, norm_f_g.reshape(1, D_MODEL)

    win_s, *row_shards_b = _allgather_w_in(w_in[0], [w[n][0] for n in _ROW_NAMES])

    to_gpn = lambda b: jnp.transpose(b, (0, 2, 1)).reshape(S5_G, S5_P * S5_N)
    from_gpn = lambda b: jnp.transpose(b.reshape(S5_G, S5_P, S5_N), (0, 2, 1))
    disc_args = (s5_a_re[0], s5_a_im[0], s5_log_dt[0].reshape(S5_G, 1), to_gpn(s5_b_re[0]), to_gpn(s5_b_im[0]))
    abar_re, abar_im, bb_re_t, bb_im_t = _s5_discretize(*disc_args)
    bbm, ccm = _s5_block_mats(from_gpn(bb_re_t), from_gpn(bb_im_t), s5_c_re[0], s5_c_im[0])
    a_z = _s5_z(abar_re, abar_im)

    h1 = _rms_fwd("norm1_fwd", x2d, g1)
    proj, *rows_01 = _mm_nn_slots("in_proj", h1, win_s, BF16, side=_gather_job(row_shards_b[:2]))
    full = {n: g.reshape(N_DEV * r, D_MODEL) for n, g, r in zip(_ROW_NAMES[:2], rows_01, ROW_SHARDS[:2])}
    half = RET_DK // 2
    inv = ROPE_BASE ** (-jnp.arange(half, dtype=F32) / half)
    cos_t, sin_t = _rope_tables(positions[0].reshape(l, 1), jnp.tile(inv, 128 // half)[None, :])
    rconsts = _ret_constants(ret_c)
    ret, o_saved, r_prev, q_rot, k_rot = _ret_fwd(proj, cos_t, sin_t, rconsts, ret_gn_g, ret_c)
    mix, xstart, *rows_xa = _s5_fwd(proj, ret, bbm, ccm, s5_d, full["s5_glu_w"], s5_glu_b, a_z, s5_t,
                                    side=_gather_job(row_shards_b[2:]))
    full.update({n: g.reshape(N_DEV * r, D_MODEL) for n, g, r in zip(_ROW_NAMES[2:], rows_xa, ROW_SHARDS[2:])})
    x1, h2 = _mm_nn("out_proj", mix, full["w_out"], F32, residual=x2d, epi=_epi_norm_fwd(g2))
    mn = _rms_fwd("norm_mem_fwd", mem2d, gm)
    qa = _mm_nn("xa_q", h2, full["xa_wq"], BF16)
    ka = _mm_nn("xa_k", mn, full["xa_wk"], BF16)
    va = _mm_nn("xa_v", mn, full["xa_wv"], BF16)
    oa = _attn_fwd(qa, ka, va)
    dx2, dgf, loss_lanes = _mm_nn("xa_o", oa, full["xa_wo"], F32, residual=x1, epi=_epi_loss(gf, tgt))

    doa = _mm_nt("xa_o_dx", dx2, full["xa_wo"], BF16)
    dwo = _mm_tn("xa_o_dw", oa, dx2, BF16)
    dqa, dka, dva = _attn_bwd(qa, ka, va, doa)
    dx1, dg2 = _mm_nt("xa_q_dx", dqa, full["xa_wq"], F32, epi=_epi_norm_bwd(x1, g2, dx2))
    dwq = _mm_tn("xa_q_dw", h2, dqa, BF16)
    dwk = _mm_tn("xa_k_dw", mn, dka, BF16)
    dwv = _mm_tn("xa_v_dw", mn, dva, BF16)
    dmn = _mm_nt("xa_v_dx", dva, full["xa_wv"], F32, residual=_mm_nt("xa_k_dx", dka, full["xa_wk"], F32))
    _, dgm = _rms_bwd("norm_mem_bwd", mem2d, gm, dmn, None)
    dmix = _mm_nt("out_proj_dx", dx1, full["w_out"], BF16)
    dwout = _mm_tn("out_proj_dw", mix, dx1, BF16)
    dret, dgn, *got_a = _ret_bwd(proj, q_rot, k_rot, cos_t, sin_t, rconsts, ret_gn_g, o_saved, r_prev, dmix, ret_c,
                                 side=_scatter_job([dwout, dwq, dwk, dwv, dwo]))
    dproj, y2, dz, dbbm, dccm, dabar, dd, dgb = _s5_bwd(proj, dmix, dret, xstart, bbm, ccm, s5_d, full["s5_glu_w"],
                                                        s5_glu_b, a_z, s5_t)
    dglu = _mm_tn("s5_glu_dw", y2, dz, BF16)
    dwin_s, got_glu = _mm_tn_slots("in_proj_dw", h1, dproj, N_DEV, BF16, side=_scatter_job([dglu]))
    grad_x, dg1, got_win = _mm_nt_slots("in_proj_dx", dproj, win_s, F32, side=_scatter_job([dwin_s]),
                                        epi=_epi_norm_bwd(x2d, g1, dx1))

    dab_re, dab_im = _s5_unz(dabar)
    dbb_re, dbb_im = _s5_block_diag_bb(dbbm)
    dc_re, dc_im = _s5_block_diag_cc(dccm)
    da_re, da_im, dlog_dt, db_re_t, db_im_t = _s5_discretize_bwd(*disc_args, dab_re, dab_im, to_gpn(dbb_re),
                                                                 to_gpn(dbb_im))
    db_re, db_im = from_gpn(db_re_t), from_gpn(db_im_t)
    small_g = dict(norm1_g=dg1, ret_gn_g=dgn, s5_d=dd, s5_glu_b=dgb, norm2_g=dg2, norm_mem_g=dgm, norm_f_g=dgf,
                   s5_a_re=da_re, s5_a_im=da_im, s5_log_dt=dlog_dt, s5_b_re=db_re, s5_b_im=db_im, s5_c_re=dc_re,
                   s5_c_im=dc_im)
    small_pack = _pack_small(small_g, loss_row=loss_lanes)

    res = {}
    got = dict(zip(("w_out", "xa_wq", "xa_wk", "xa_wv", "xa_wo"), got_a), w_in=got_win, s5_glu_w=got_glu)
    for n in ("w_in",) + _ROW_NAMES:
        res[n] = _adamw("adamw_" + n, got[n], w[n][0], mom[n][0], var[n][0])
    small_sum = _allreduce_small(small_pack)
    small_res, loss_sum = _adamw_small(small_sum, _pack_small(w), _pack_small(mom), _pack_small(var))
    loss = (0.5 / D_MODEL) * jnp.sum(loss_sum)
    res.update(small_res)

    outs = [loss, grad_x[None]]
    for part in range(4):
        for n in _W_NAMES:
            outs.append(res[n][part].reshape(shapes[n]))
    return tuple(outs)
```

```python
import functools

import jax
import jax.numpy as jnp
from jax import lax
from jax.experimental import pallas as pl
from jax.experimental.pallas import tpu as pltpu

F32 = jnp.float32
BF16 = jnp.bfloat16
MESH = pl.DeviceIdType.MESH

D_MODEL = 1024
RET_HEADS, RET_DK, RET_DV = 8, 64, 128
RET_QK = RET_HEADS * RET_DK
S5_G, S5_N, S5_P = 64, 64, 16
S5_NB = 8
S5_GB = S5_G // S5_NB
S5_BS = S5_GB * S5_N
S5_COLS = 2 * S5_G * S5_N
XA_HEADS, XA_DH = 4, 256
EPS = 1e-6
ROPE_BASE = 10000.0
N_DEV = 8
W_IN_SHARD = 640
ROW_SHARDS = (128, 256, 128, 128, 128, 128)
ROWPACK = sum(ROW_SHARDS)
SMALL_ROWS = 320
ADAM_LR, ADAM_B1, ADAM_B2, ADAM_EPS, ADAM_WD, ADAM_STEP = 0.001, 0.9, 0.999, 1e-08, 0.01, 10

VMEM_LIMIT = 56 * 1024 * 1024


def _cp(*sem):
    return pltpu.CompilerParams(dimension_semantics=tuple(sem), vmem_limit_bytes=VMEM_LIMIT)


def _dot(a, b):
    return jnp.dot(a, b, preferred_element_type=F32)


def _dot_nt(a, b):
    return lax.dot_general(a, b, (((1,), (1,)), ((), ())), preferred_element_type=F32)


def _dot_tn(a, b):
    return lax.dot_general(a, b, (((0,), (0,)), ((), ())), preferred_element_type=F32)


def _sigmoid(x):
    return 1.0 / (1.0 + jnp.exp(-x))


def _silu(x):
    return x * _sigmoid(x)


def _dsilu(x):
    s = _sigmoid(x)
    return s * (1.0 + x * (1.0 - s))


_GELU_C = 0.7978845608028654


def _gelu(x):
    return 0.5 * x * (1.0 + jnp.tanh(_GELU_C * (x + 0.044715 * (x * x * x))))


def _gelu_and_grad(x):
    t = jnp.tanh(_GELU_C * (x + 0.044715 * (x * x * x)))
    half = 0.5 * (1.0 + t)
    return x * half, half + 0.5 * x * (1.0 - t * t) * (_GELU_C * (1.0 + 3.0 * 0.044715 * (x * x)))


def _pick(n, cands):
    for c in cands:
        if n % c == 0:
            return c
    return n


class _Epilogue:
    def __init__(self, rows, vecs, row_out_dtypes, n_sums, fn):
        self.rows, self.vecs, self.row_out_dtypes, self.n_sums, self.fn = list(rows), list(vecs), list(row_out_dtypes), n_sums, fn


def _rms(x):
    rs = lax.rsqrt(jnp.mean(x * x, axis=-1, keepdims=True) + EPS)
    return rs, x * rs


def _rms_dx(dn, xn, rs):
    return rs * (dn - xn * jnp.mean(dn * xn, axis=-1, keepdims=True))


def _epi_norm_fwd(g):
    def fn(r, rows, vecs):
        return r, [_rms(r)[1] * vecs[0]], []

    return _Epilogue([], [g], [BF16], 0, fn)


def _epi_loss(gf, target):
    def fn(r, rows, vecs):
        rs, xn = _rms(r)
        e = xn * vecs[0] - rows[0]
        dy = e * (1.0 / r.shape[-1])
        return (_rms_dx(dy * vecs[0], xn, rs), [],
                [jnp.sum(dy * xn, axis=0, keepdims=True), jnp.sum(e * e, axis=0, keepdims=True)])

    return _Epilogue([target], [gf], [], 2, fn)


def _epi_norm_bwd(x, g, dres):
    def fn(r, rows, vecs):
        rs, xn = _rms(rows[0])
        return _rms_dx(r * vecs[0], xn, rs) + rows[1], [], [jnp.sum(r * xn, axis=0, keepdims=True)]

    return _Epilogue([x, dres], [g], [], 1, fn)


def _mm_core(name, operands, in_specs, out_spec, out_shape, grid, nk, dims, acc_shape, has_res, side=None, epi=None):
    n_in = 3 if has_res else 2
    n_epi_in = len(epi.rows) + len(epi.vecs) if epi else 0
    n_epi_out = len(epi.row_out_dtypes) + epi.n_sums if epi else 0
    n_side_in = len(side.srcs) if side else 0
    n_side_out = side.n if side else 0

    def body(*refs):
        a_ref, b_ref = refs[0], refs[1]
        r_ref = refs[2] if has_res else None
        epi_in = refs[n_in:n_in + n_epi_in]
        side_in = refs[n_in + n_epi_in:n_in + n_epi_in + n_side_in]
        n0 = n_in + n_epi_in + n_side_in
        o_ref = refs[n0]
        epi_out = refs[n0 + 1:n0 + 1 + n_epi_out]
        side_out = refs[n0 + 1 + n_epi_out:n0 + 1 + n_epi_out + n_side_out]
        rest = refs[n0 + 1 + n_epi_out + n_side_out:]
        acc, sems = (rest[0], rest[1:]) if nk > 1 else (None, rest)
        i, j, k = pl.program_id(0), pl.program_id(1), pl.program_id(2)
        if side:
            @pl.when((i == 0) & (j == 0) & (k == 0))
            def _():
                side.start(side_in, side_out, sems)

        def product():
            return lax.dot_general(a_ref[...].astype(BF16), b_ref[...].astype(BF16), (dims, ((), ())),
                                   preferred_element_type=F32)

        def finish(r):
            if has_res:
                r = r + r_ref[...]
            if epi is None:
                o_ref[...] = r.astype(o_ref.dtype)
                return
            n_rows = len(epi.rows)
            main, row_vals, sums = epi.fn(r, [t[...] for t in epi_in[:n_rows]], [t[...] for t in epi_in[n_rows:]])
            o_ref[...] = main.astype(o_ref.dtype)
            for ref, val in zip(epi_out, row_vals):
                ref[...] = val.astype(ref.dtype)
            for ref, val in zip(epi_out[len(row_vals):], sums):
                @pl.when(i == 0)
                def _(ref=ref):
                    ref[...] = jnp.zeros_like(ref)

                ref[...] += val

        if nk == 1:
            finish(product())
        else:
            @pl.when(k == 0)
            def _():
                acc[...] = jnp.zeros_like(acc)

            acc[...] += product()

            @pl.when(k == nk - 1)
            def _():
                finish(acc[...])

        if side:
            @pl.when((i == grid[0] - 1) & (j == grid[1] - 1) & (k == grid[2] - 1))
            def _():
                side.wait(side_in, side_out, sems)

    acc_scratch = [pltpu.VMEM(acc_shape, F32)] if nk > 1 else []
    in_specs, out_specs, out_shapes, operands = list(in_specs), [out_spec], [out_shape], list(operands)
    if epi:
        assert grid[1] == 1, "an epilogue needs tiles that span whole rows"
        tm, n = out_spec.block_shape
        row_spec = pl.BlockSpec((tm, n), lambda i, j, k: (i, 0))
        vec_spec = pl.BlockSpec((1, n), lambda i, j, k: (0, 0))
        in_specs += [row_spec] * len(epi.rows) + [vec_spec] * len(epi.vecs)
        operands += epi.rows + epi.vecs
        out_specs += [row_spec] * len(epi.row_out_dtypes) + [vec_spec] * epi.n_sums
        out_shapes += [jax.ShapeDtypeStruct(out_shape.shape, d) for d in epi.row_out_dtypes]
        out_shapes += [jax.ShapeDtypeStruct((1, n), F32)] * epi.n_sums
    scratch = acc_scratch
    if side:
        in_specs += side.in_specs
        operands += side.srcs
        out_specs += side.out_specs
        out_shapes += side.landing
        scratch = acc_scratch + side.scratch
    plain = side is None and epi is None
    res = pl.pallas_call(
        body, name=name, grid=grid, in_specs=in_specs, out_specs=out_specs[0] if plain else tuple(out_specs),
        out_shape=out_shapes[0] if plain else tuple(out_shapes), scratch_shapes=scratch,
        compiler_params=_cp("parallel", "parallel", "arbitrary") if plain else _cp("arbitrary", "arbitrary", "arbitrary"),
    )(*operands)
    return res


def _mm_nn(name, a, b, out_dtype, residual=None, epi=None):
    m, kk = a.shape
    n = b.shape[1]
    tm, tn, tk = _pick(m, (1024, 512, 256)), _pick(n, (1024, 512)), _pick(kk, (1024, 512))
    ops = [a, b]
    specs = [pl.BlockSpec((tm, tk), lambda i, j, k: (i, k)), pl.BlockSpec((tk, tn), lambda i, j, k: (k, j))]
    if residual is not None:
        ops.append(residual)
        specs.append(pl.BlockSpec((tm, tn), lambda i, j, k: (i, j)))
    return _mm_core(name, ops, specs, pl.BlockSpec((tm, tn), lambda i, j, k: (i, j)),
                    jax.ShapeDtypeStruct((m, n), out_dtype), (m // tm, n // tn, kk // tk), kk // tk,
                    ((1,), (0,)), (tm, tn), residual is not None, epi=epi)


def _mm_nt(name, a, b, out_dtype, residual=None, epi=None):
    m, kk = a.shape
    n = b.shape[0]
    tm, tn, tk = _pick(m, (1024, 512, 256)), _pick(n, (1024, 512)), _pick(kk, (1024, 512))
    ops = [a, b]
    specs = [pl.BlockSpec((tm, tk), lambda i, j, k: (i, k)), pl.BlockSpec((tn, tk), lambda i, j, k: (j, k))]
    if residual is not None:
        ops.append(residual)
        specs.append(pl.BlockSpec((tm, tn), lambda i, j, k: (i, j)))
    return _mm_core(name, ops, specs, pl.BlockSpec((tm, tn), lambda i, j, k: (i, j)),
                    jax.ShapeDtypeStruct((m, n), out_dtype), (m // tm, n // tn, kk // tk), kk // tk,
                    ((1,), (1,)), (tm, tn), residual is not None, epi=epi)


def _mm_tn(name, a, b, out_dtype):
    kk, m = a.shape
    n = b.shape[1]
    tm, tn, tk = _pick(m, (1024, 512)), _pick(n, (1024, 512)), _pick(kk, (2048, 1024, 512, 256))
    specs = [pl.BlockSpec((tk, tm), lambda i, j, k: (k, i)), pl.BlockSpec((tk, tn), lambda i, j, k: (k, j))]
    return _mm_core(name, [a, b], specs, pl.BlockSpec((tm, tn), lambda i, j, k: (i, j)),
                    jax.ShapeDtypeStruct((m, n), out_dtype), (m // tm, n // tn, kk // tk), kk // tk,
                    ((0,), (0,)), (tm, tn), False)


def _mm_nn_slots(name, a, b_slots, out_dtype, side=None):
    m, kk = a.shape
    s, _, ns = b_slots.shape
    tm, tk = _pick(m, (1024, 512, 256)), _pick(kk, (1024, 512))
    specs = [pl.BlockSpec((tm, tk), lambda i, j, k: (i, k)), pl.BlockSpec((None, tk, ns), lambda i, j, k: (j, k, 0))]
    return _mm_core(name, [a, b_slots], specs, pl.BlockSpec((tm, ns), lambda i, j, k: (i, j)),
                    jax.ShapeDtypeStruct((m, s * ns), out_dtype), (m // tm, s, kk // tk), kk // tk,
                    ((1,), (0,)), (tm, ns), False, side)


def _mm_nt_slots(name, a, b_slots, out_dtype, side=None, epi=None):
    m = a.shape[0]
    s, n, ns = b_slots.shape
    tm, tn = _pick(m, (1024, 512, 256)), _pick(n, (1024, 512))
    specs = [pl.BlockSpec((tm, ns), lambda i, j, k: (i, k)), pl.BlockSpec((None, tn, ns), lambda i, j, k: (k, j, 0))]
    return _mm_core(name, [a, b_slots], specs, pl.BlockSpec((tm, tn), lambda i, j, k: (i, j)),
                    jax.ShapeDtypeStruct((m, n), out_dtype), (m // tm, n // tn, s), s,
                    ((1,), (1,)), (tm, tn), False, side, epi)


def _mm_tn_slots(name, a, b, s, out_dtype, side=None):
    kk, m = a.shape
    ns = b.shape[1] // s
    tm, tk = _pick(m, (1024, 512)), _pick(kk, (2048, 1024, 512, 256))
    specs = [pl.BlockSpec((tk, tm), lambda i, j, k: (k, i)), pl.BlockSpec((tk, ns), lambda i, j, k: (k, j))]
    return _mm_core(name, [a, b], specs, pl.BlockSpec((None, tm, ns), lambda i, j, k: (j, i, 0)),
                    jax.ShapeDtypeStruct((s, m, ns), out_dtype), (m // tm, s, kk // tk), kk // tk,
                    ((0,), (0,)), (tm, ns), False, side)


def _rms_fwd(name, x, g):
    r, d = x.shape
    tr = _pick(r, (1024, 512, 256))

    def body(x_ref, g_ref, o_ref):
        xv = x_ref[...]
        rs = lax.rsqrt(jnp.mean(xv * xv, axis=-1, keepdims=True) + EPS)
        o_ref[...] = (xv * rs * g_ref[...]).astype(o_ref.dtype)

    return pl.pallas_call(
        body, name=name, grid=(r // tr,),
        in_specs=[pl.BlockSpec((tr, d), lambda i: (i, 0)), pl.BlockSpec((1, d), lambda i: (0, 0))],
        out_specs=pl.BlockSpec((tr, d), lambda i: (i, 0)),
        out_shape=jax.ShapeDtypeStruct((r, d), BF16), compiler_params=_cp("parallel"),
    )(x, g)


def _rms_bwd(name, x, g, dh, dres):
    r, d = x.shape
    tr = _pick(r, (512, 256))
    has_res = dres is not None

    def body(*refs):
        if has_res:
            x_ref, g_ref, dh_ref, dr_ref, dx_ref, dg_ref = refs
        else:
            x_ref, g_ref, dh_ref, dx_ref, dg_ref = refs
        i = pl.program_id(0)

        @pl.when(i == 0)
        def _():
            dg_ref[...] = jnp.zeros_like(dg_ref)

        xv = x_ref[...]
        dhv = dh_ref[...].astype(F32)
        rs = lax.rsqrt(jnp.mean(xv * xv, axis=-1, keepdims=True) + EPS)
        xn = xv * rs
        dg_ref[...] += jnp.sum(dhv * xn, axis=0, keepdims=True)
        dn = dhv * g_ref[...]
        dx = rs * (dn - xn * jnp.mean(dn * xn, axis=-1, keepdims=True))
        if has_res:
            dx = dx + dr_ref[...]
        dx_ref[...] = dx

    row = pl.BlockSpec((tr, d), lambda i: (i, 0))
    vec = pl.BlockSpec((1, d), lambda i: (0, 0))
    ops = [x, g, dh] + ([dres] if has_res else [])
    return pl.pallas_call(
        body, name=name, grid=(r // tr,),
        in_specs=[row, vec, row] + ([row] if has_res else []),
        out_specs=(row, vec),
        out_shape=(jax.ShapeDtypeStruct((r, d), F32), jax.ShapeDtypeStruct((1, d), F32)),
        compiler_params=_cp("arbitrary"),
    )(*ops)


def _loss_head(x2, gf, target):
    r, d = x2.shape
    tr = _pick(r, (512, 256))

    def body(x_ref, g_ref, t_ref, dx_ref, dg_ref, ls_ref):
        i = pl.program_id(0)

        @pl.when(i == 0)
        def _():
            dg_ref[...] = jnp.zeros_like(dg_ref)
            ls_ref[...] = jnp.zeros_like(ls_ref)

        xv = x_ref[...]
        rs = lax.rsqrt(jnp.mean(xv * xv, axis=-1, keepdims=True) + EPS)
        xn = xv * rs
        e = xn * g_ref[...] - t_ref[...]
        ls_ref[...] += jnp.sum(e * e, axis=0, keepdims=True)
        dy = e * (1.0 / d)
        dg_ref[...] += jnp.sum(dy * xn, axis=0, keepdims=True)
        dn = dy * g_ref[...]
        dx_ref[...] = rs * (dn - xn * jnp.mean(dn * xn, axis=-1, keepdims=True))

    row = pl.BlockSpec((tr, d), lambda i: (i, 0))
    vec = pl.BlockSpec((1, d), lambda i: (0, 0))
    return pl.pallas_call(
        body, name="loss_head", grid=(r // tr,), in_specs=[row, vec, row], out_specs=(row, vec, vec),
        out_shape=(jax.ShapeDtypeStruct((r, d), F32), jax.ShapeDtypeStruct((1, d), F32),
                   jax.ShapeDtypeStruct((1, d), F32)),
        compiler_params=_cp("arbitrary"),
    )(x2, gf, target)


def _rope_tables(pos_col, inv_row):
    l = pos_col.shape[0]
    tl = _pick(l, (1024, 512, 256))

    def body(p_ref, inv_ref, cos_ref, sin_ref):
        ang = p_ref[...].astype(F32) * inv_ref[...]
        lane = lax.broadcasted_iota(jnp.int32, ang.shape, 1)
        c = jnp.cos(ang)
        s = jnp.where((lane % RET_DK) < RET_DK // 2, -jnp.sin(ang), jnp.sin(ang))
        cos_ref[...] = jnp.tile(c, (1, RET_QK // 128))
        sin_ref[...] = jnp.tile(s, (1, RET_QK // 128))

    return pl.pallas_call(
        body, name="rope_tables", grid=(l // tl,),
        in_specs=[pl.BlockSpec((tl, 1), lambda i: (i, 0)), pl.BlockSpec((1, 128), lambda i: (0, 0))],
        out_specs=(pl.BlockSpec((tl, RET_QK), lambda i: (i, 0)), pl.BlockSpec((tl, RET_QK), lambda i: (i, 0))),
        out_shape=(jax.ShapeDtypeStruct((l, RET_QK), F32), jax.ShapeDtypeStruct((l, RET_QK), F32)),
        compiler_params=_cp("parallel"),
    )(pos_col, inv_row)


def _rot(x, cos_t, sin_t):
    n = x.shape[-1]
    lane = lax.broadcasted_iota(jnp.int32, x.shape, 1)
    partner = jnp.where((lane % RET_DK) < RET_DK // 2, pltpu.roll(x, n - RET_DK // 2, 1), pltpu.roll(x, RET_DK // 2, 1))
    return x * cos_t + partner * sin_t


def _ret_constants(c):
    log_g = jnp.log1p(-jnp.exp2(-5.0 - jnp.arange(RET_HEADS, dtype=F32)))
    j = jnp.arange(c, dtype=F32)
    diff = j[:, None] - j[None, :]
    decay = jnp.where(diff[None] >= 0.0, jnp.exp(log_g[:, None, None] * jnp.maximum(diff, 0.0)[None]), 0.0)
    q_w = jnp.exp(log_g[None, :] * (j + 1.0)[:, None])
    k_w = jnp.exp(log_g[None, :] * (c - 1.0 - j)[:, None])
    cd = jnp.exp(log_g * c)
    rep = lambda t: jnp.repeat(t, RET_DK, axis=1)
    cd_row = jnp.repeat(cd, RET_DV)[None, :]
    return decay, rep(q_w), rep(k_w), cd_row


def _pair_of(h, c):
    lane = lax.broadcasted_iota(jnp.int32, (c, 2 * RET_DK), 1)
    mine = (lane < RET_DK) if h % 2 == 0 else (lane >= RET_DK)
    return slice((h // 2) * 2 * RET_DK, (h // 2 + 1) * 2 * RET_DK), mine


def _keep(x, mine):
    return jnp.where(mine, x, jnp.zeros_like(x))


def _ret_fwd(proj, cos_t, sin_t, consts, gn_g, c):
    l = proj.shape[0]
    nc = l // c
    decay, qw, kw, cd_row = consts

    def body(q_ref, k_ref, v_ref, g_ref, cos_ref, sin_ref, dec_ref, qw_ref, kw_ref, cd_ref, gn_ref,
             ret_ref, o_ref, rp_ref, qb_ref, kb_ref, state):
        @pl.when(pl.program_id(0) == 0)
        def _():
            state[...] = jnp.zeros_like(state)

        cs, sn = cos_ref[...], sin_ref[...]
        qr = _rot(q_ref[...].astype(F32), cs, sn)
        kr = _rot(k_ref[...].astype(F32), cs, sn) * (RET_DK ** -0.5)
        qb, kb = qr.astype(BF16), kr.astype(BF16)
        qb_ref[...] = qb
        kb_ref[...] = kb
        qwb = (qr * qw_ref[...]).astype(BF16)
        kwb = (kr * kw_ref[...]).astype(BF16)
        vb = v_ref[...].astype(BF16)
        for h in range(RET_HEADS):
            ps, mine = _pair_of(h, c)
            vs = slice(h * RET_DV, (h + 1) * RET_DV)
            s = _dot_nt(_keep(qb[:, ps], mine), kb[:, ps]) * dec_ref[h]
            r_prev = state[h]
            rp_ref[0, h] = r_prev
            o = _dot(s.astype(BF16), vb[:, vs]) + _dot(_keep(qwb[:, ps], mine), r_prev.astype(BF16))
            state[h] = cd_ref[:, vs] * r_prev + _dot_tn(_keep(kwb[:, ps], mine), vb[:, vs])
            o_ref[:, vs] = o
            mu = jnp.mean(o, axis=-1, keepdims=True)
            var = jnp.mean(jnp.square(o - mu), axis=-1, keepdims=True)
            on = (o - mu) * lax.rsqrt(var + EPS)
            ret_ref[:, vs] = (on * gn_ref[:, vs] * _silu(g_ref[:, vs].astype(F32))).astype(ret_ref.dtype)

    const2 = lambda shape: pl.BlockSpec(shape, lambda i: (0,) * len(shape))
    return pl.pallas_call(
        body, name="retention_fwd", grid=(nc,),
        in_specs=[pl.BlockSpec((c, RET_QK), lambda i: (i, 0)), pl.BlockSpec((c, RET_QK), lambda i: (i, 1)),
                  pl.BlockSpec((c, D_MODEL), lambda i: (i, 1)), pl.BlockSpec((c, D_MODEL), lambda i: (i, 2)),
                  pl.BlockSpec((c, RET_QK), lambda i: (i, 0)), pl.BlockSpec((c, RET_QK), lambda i: (i, 0)),
                  const2((RET_HEADS, c, c)), const2((c, RET_QK)), const2((c, RET_QK)), const2((1, D_MODEL)),
                  const2((1, D_MODEL))],
        out_specs=(pl.BlockSpec((c, D_MODEL), lambda i: (i, 0)), pl.BlockSpec((c, D_MODEL), lambda i: (i, 0)),
                   pl.BlockSpec((1, RET_HEADS, 2 * RET_DK, RET_DV), lambda i: (i, 0, 0, 0)),
                   pl.BlockSpec((c, RET_QK), lambda i: (i, 0)), pl.BlockSpec((c, RET_QK), lambda i: (i, 0))),
        out_shape=(jax.ShapeDtypeStruct((l, 2 * D_MODEL), BF16), jax.ShapeDtypeStruct((l, D_MODEL), F32),
                   jax.ShapeDtypeStruct((nc, RET_HEADS, 2 * RET_DK, RET_DV), F32),
                   jax.ShapeDtypeStruct((l, RET_QK), BF16), jax.ShapeDtypeStruct((l, RET_QK), BF16)),
        scratch_shapes=[pltpu.VMEM((RET_HEADS, 2 * RET_DK, RET_DV), F32)],
        compiler_params=_cp("arbitrary"),
    )(proj, proj, proj, proj, cos_t, sin_t, decay, qw, kw, cd_row, gn_g)


def _ret_bwd(proj, qb_saved, kb_saved, cos_t, sin_t, consts, gn_g, o_saved, r_prev_saved, dmix, c, side):
    l = proj.shape[0]
    nc = l // c
    decay, qw, kw, cd_row = consts
    n_in = 14

    def body(*refs):
        (q_ref, k_ref, v_ref, g_ref, cos_ref, sin_ref, dec_ref, qw_ref, kw_ref, cd_ref, gn_ref, o_ref, rp_ref,
         dr_ref) = refs[:n_in]
        side_in = refs[n_in:n_in + len(side.srcs)]
        out_ref, dgn_ref = refs[n_in + len(side.srcs):n_in + len(side.srcs) + 2]
        side_out = refs[n_in + len(side.srcs) + 2:n_in + len(side.srcs) + 2 + side.n]
        state, dq_s, dk_s = refs[n_in + len(side.srcs) + 2 + side.n:n_in + len(side.srcs) + 5 + side.n]
        sems = refs[n_in + len(side.srcs) + 5 + side.n:]

        @pl.when(pl.program_id(0) == 0)
        def _():
            side.start(side_in, side_out, sems)
            state[...] = jnp.zeros_like(state)
            dgn_ref[...] = jnp.zeros_like(dgn_ref)

        cs, sn = cos_ref[...], sin_ref[...]
        qb, kb = q_ref[...], k_ref[...]
        qwv, kwv = qw_ref[...], kw_ref[...]
        qwb = (qb.astype(F32) * qwv).astype(BF16)
        kwb = (kb.astype(F32) * kwv).astype(BF16)
        vb = v_ref[...].astype(BF16)
        dq2 = dk2 = None
        for h in range(RET_HEADS):
            ps, mine = _pair_of(h, c)
            vs = slice(h * RET_DV, (h + 1) * RET_DV)
            dec = dec_ref[h]
            qm, km = _keep(qb[:, ps], mine), _keep(kb[:, ps], mine)
            o = o_ref[:, vs]
            mu = jnp.mean(o, axis=-1, keepdims=True)
            var = jnp.mean(jnp.square(o - mu), axis=-1, keepdims=True)
            rstd = lax.rsqrt(var + EPS)
            on = (o - mu) * rstd
            gate = g_ref[:, vs].astype(F32)
            sg = _silu(gate)
            dret = dr_ref[:, vs].astype(F32)
            gn = gn_ref[:, vs]
            dgn_ref[:, vs] += jnp.sum(dret * on * sg, axis=0, keepdims=True)
            out_ref[:, 2 * RET_QK + D_MODEL + h * RET_DV:2 * RET_QK + D_MODEL + (h + 1) * RET_DV] = (
                dret * on * gn * _dsilu(gate)).astype(out_ref.dtype)
            don = dret * gn * sg
            do = rstd * (don - jnp.mean(don, axis=-1, keepdims=True)
                         - on * jnp.mean(don * on, axis=-1, keepdims=True))
            dob = do.astype(BF16)
            sn_h = state[h]
            snb = sn_h.astype(BF16)
            s = _dot_nt(qm, kb[:, ps]) * dec
            dv = _dot_tn(s.astype(BF16), dob) + _dot(_keep(kwb[:, ps], mine), snb)
            out_ref[:, 2 * RET_QK + h * RET_DV:2 * RET_QK + (h + 1) * RET_DV] = dv.astype(out_ref.dtype)
            ds = (_dot_nt(dob, vb[:, vs]) * dec).astype(BF16)
            dq_h = _dot(ds, km) + qwv[:, ps] * _dot_nt(dob, rp_ref[0, h].astype(BF16))
            dk_h = _dot_tn(ds, qm) + kwv[:, ps] * _dot_nt(vb[:, vs], snb)
            state[h] = cd_ref[:, vs] * sn_h + _dot_tn(_keep(qwb[:, ps], mine), dob)
            if h % 2 == 0:
                dq2, dk2 = dq_h, dk_h
            else:
                dq_s[:, ps] = dq2 + dq_h
                dk_s[:, ps] = dk2 + dk_h
        out_ref[:, 0:RET_QK] = _rot(dq_s[...], cs, -sn).astype(out_ref.dtype)
        out_ref[:, RET_QK:2 * RET_QK] = (_rot(dk_s[...], cs, -sn) * (RET_DK ** -0.5)).astype(out_ref.dtype)

        @pl.when(pl.program_id(0) == nc - 1)
        def _():
            side.wait(side_in, side_out, sems)

    rev = lambda i: nc - 1 - i
    const2 = lambda shape: pl.BlockSpec(shape, lambda i: (0,) * len(shape))
    return pl.pallas_call(
        body, name="retention_bwd", grid=(nc,),
        in_specs=[pl.BlockSpec((c, RET_QK), lambda i: (rev(i), 0)), pl.BlockSpec((c, RET_QK), lambda i: (rev(i), 0)),
                  pl.BlockSpec((c, D_MODEL), lambda i: (rev(i), 1)), pl.BlockSpec((c, D_MODEL), lambda i: (rev(i), 2)),
                  pl.BlockSpec((c, RET_QK), lambda i: (rev(i), 0)), pl.BlockSpec((c, RET_QK), lambda i: (rev(i), 0)),
                  const2((RET_HEADS, c, c)), const2((c, RET_QK)), const2((c, RET_QK)), const2((1, D_MODEL)),
                  const2((1, D_MODEL)),
                  pl.BlockSpec((c, D_MODEL), lambda i: (rev(i), 0)),
                  pl.BlockSpec((1, RET_HEADS, 2 * RET_DK, RET_DV), lambda i: (rev(i), 0, 0, 0)),
                  pl.BlockSpec((c, D_MODEL), lambda i: (rev(i), 0))] + side.in_specs,
        out_specs=(pl.BlockSpec((c, 2 * RET_QK + 2 * D_MODEL), lambda i: (rev(i), 0)), const2((1, D_MODEL)),
                   *side.out_specs),
        out_shape=(jax.ShapeDtypeStruct((l, 2 * RET_QK + 4 * D_MODEL), BF16), jax.ShapeDtypeStruct((1, D_MODEL), F32),
                   *side.landing),
        scratch_shapes=[pltpu.VMEM((RET_HEADS, 2 * RET_DK, RET_DV), F32), pltpu.VMEM((c, RET_QK), F32),
                        pltpu.VMEM((c, RET_QK), F32)] + side.scratch,
        compiler_params=_cp("arbitrary"),
    )(qb_saved, kb_saved, proj, proj, cos_t, sin_t, decay, qw, kw, cd_row, gn_g, o_saved, r_prev_saved, dmix,
      *side.srcs)


def _zoh(a_re, a_im, log_dt):
    dt = jnp.exp(log_dt)
    mag = jnp.exp(a_re * dt)
    abar_re = mag * jnp.cos(a_im * dt)
    abar_im = mag * jnp.sin(a_im * dt)
    den = a_re * a_re + a_im * a_im
    nr, ni = abar_re - 1.0, abar_im
    f_re = (nr * a_re + ni * a_im) / den
    f_im = (ni * a_re - nr * a_im) / den
    return dt, abar_re, abar_im, f_re, f_im, den


def _lanes_p(f):
    return jnp.tile(f, (1, S5_P))


def _s5_discretize(a_re, a_im, log_dt, b_re_t, b_im_t):
    def body(ar_ref, ai_ref, ld_ref, br_ref, bi_ref, abr_ref, abi_ref, bbr_ref, bbi_ref):
        _, abar_re, abar_im, f_re, f_im, _ = _zoh(ar_ref[...], ai_ref[...], ld_ref[...])
        abr_ref[...] = abar_re
        abi_ref[...] = abar_im
        fr, fi = _lanes_p(f_re), _lanes_p(f_im)
        bbr_ref[...] = fr * br_ref[...] - fi * bi_ref[...]
        bbi_ref[...] = fr * bi_ref[...] + fi * br_ref[...]

    gn = jax.ShapeDtypeStruct((S5_G, S5_N), F32)
    gpn = jax.ShapeDtypeStruct((S5_G, S5_P * S5_N), F32)
    return pl.pallas_call(body, name="s5_discretize", out_shape=(gn, gn, gpn, gpn))(a_re, a_im, log_dt, b_re_t, b_im_t)


def _s5_discretize_bwd(a_re, a_im, log_dt, b_re_t, b_im_t, dab_re, dab_im, dbb_re_t, dbb_im_t):
    def body(ar_ref, ai_ref, ld_ref, br_ref, bi_ref, gar_ref, gai_ref, gbr_ref, gbi_ref,
             dar_ref, dai_ref, dld_ref, dbr_ref, dbi_ref):
        a_r, a_i = ar_ref[...], ai_ref[...]
        dt, abar_re, abar_im, f_re, f_im, den = _zoh(a_r, a_i, ld_ref[...])
        b_r, b_i, g_br, g_bi = br_ref[...], bi_ref[...], gbr_ref[...], gbi_ref[...]
        fr, fi = _lanes_p(f_re), _lanes_p(f_im)
        dbr_ref[...] = fr * g_br + fi * g_bi
        dbi_ref[...] = fr * g_bi - fi * g_br
        t_r = b_r * g_br + b_i * g_bi
        t_i = b_r * g_bi - b_i * g_br
        gf_r = sum(t_r[:, p * S5_N:(p + 1) * S5_N] for p in range(S5_P))
        gf_i = sum(t_i[:, p * S5_N:(p + 1) * S5_N] for p in range(S5_P))
        inv_r, inv_i = a_r / den, a_i / den
        ga_r = gar_ref[...] + gf_r * inv_r - gf_i * inv_i
        ga_i = gai_ref[...] + gf_r * inv_i + gf_i * inv_r
        q_r = -(f_re * a_r + f_im * a_i) / den
        q_i = -(f_im * a_r - f_re * a_i) / den
        gl_r = q_r * gf_r + q_i * gf_i
        gl_i = q_r * gf_i - q_i * gf_r
        dar_ref[...] = gl_r + dt * (abar_re * ga_r + abar_im * ga_i)
        dai_ref[...] = gl_i + dt * (abar_re * ga_i - abar_im * ga_r)
        la_r = a_r * abar_re - a_i * abar_im
        la_i = a_r * abar_im + a_i * abar_re
        dld_ref[...] = dt * jnp.sum(ga_r * la_r + ga_i * la_i, axis=-1, keepdims=True)

    gn = jax.ShapeDtypeStruct((S5_G, S5_N), F32)
    gpn = jax.ShapeDtypeStruct((S5_G, S5_P * S5_N), F32)
    return pl.pallas_call(
        body, name="s5_discretize_bwd", out_shape=(gn, gn, jax.ShapeDtypeStruct((S5_G, 1), F32), gpn, gpn),
    )(a_re, a_im, log_dt, b_re_t, b_im_t, dab_re, dab_im, dbb_re_t, dbb_im_t)


S5_ZQ = S5_NB // 2


def _s5_z(re, im):
    return jnp.concatenate([re.reshape(S5_ZQ, 8, 128), im.reshape(S5_ZQ, 8, 128)], axis=0)


def _s5_unz(z):
    return z[:S5_ZQ].reshape(S5_G, S5_N), z[S5_ZQ:].reshape(S5_G, S5_N)


def _s5_block_mats(bb_re, bb_im, c_re, c_im):
    eye = jnp.eye(S5_GB, dtype=F32)
    bb = jnp.stack([bb_re, bb_im], axis=0).reshape(2, S5_NB, S5_GB, S5_N, S5_P)
    bbm = jnp.einsum("rbgnp,gh->bgprhn", bb, eye).reshape(S5_NB, S5_GB * S5_P, 2 * S5_BS)
    cc = jnp.stack([c_re, -c_im], axis=0).reshape(2, S5_NB, S5_GB, S5_P, S5_N)
    ccm = jnp.einsum("rbgpn,gh->brhngp", cc, eye).reshape(S5_NB, 2 * S5_BS, S5_GB * S5_P)
    return bbm.astype(BF16), ccm.astype(BF16)


def _s5_block_diag_bb(m):
    t = m.reshape(S5_NB, S5_GB, S5_P, 2, S5_GB, S5_N)
    d = jnp.einsum("bgprgn->rbgnp", t).reshape(2, S5_G, S5_N, S5_P)
    return d[0], d[1]


def _s5_block_diag_cc(m):
    t = m.reshape(S5_NB, 2, S5_GB, S5_N, S5_GB, S5_P)
    d = jnp.einsum("brgngp->rbgpn", t).reshape(2, S5_G, S5_P, S5_N)
    return d[0], -d[1]


SCAN_UNROLL = 8


def _z_store(zr, zi, blk, res, t, off):
    q, h = blk // 2, blk % 2
    for lt in range(4):
        zr[q, pl.ds(off + 4 * h + lt, t, stride=8), :] = res[:, lt * 128:(lt + 1) * 128]
        zi[q, pl.ds(off + 4 * h + lt, t, stride=8), :] = res[:, S5_BS + lt * 128:S5_BS + (lt + 1) * 128]


def _z_load(zr, zi, blk, t, off):
    q, h = blk // 2, blk % 2
    return jnp.concatenate([zr[q, pl.ds(off + 4 * h + lt, t, stride=8), :] for lt in range(4)]
                           + [zi[q, pl.ds(off + 4 * h + lt, t, stride=8), :] for lt in range(4)], axis=1)


def _z_scan_fwd(zr, zi, a_ref, carry_ref, t, off):
    ar = [a_ref[q] for q in range(S5_ZQ)]
    ai = [a_ref[S5_ZQ + q] for q in range(S5_ZQ)]

    def step(it, carry):
        carry = list(carry)
        base = pl.multiple_of(it * (8 * SCAN_UNROLL), 8 * SCAN_UNROLL) + off
        for tt in range(SCAN_UNROLL):
            rows = pl.ds(base + 8 * tt, 8)
            for q in range(S5_ZQ):
                c_r, c_i = carry[q], carry[S5_ZQ + q]
                n_r = ar[q] * c_r - ai[q] * c_i + zr[q, rows, :]
                n_i = ar[q] * c_i + ai[q] * c_r + zi[q, rows, :]
                zr[q, rows, :] = n_r
                zi[q, rows, :] = n_i
                carry[q], carry[S5_ZQ + q] = n_r, n_i
        return tuple(carry)

    out = lax.fori_loop(0, t // SCAN_UNROLL, step, tuple(carry_ref[k] for k in range(2 * S5_ZQ)))
    for k in range(2 * S5_ZQ):
        carry_ref[k] = out[k]


def _z_scan_bwd(lr, li, xr, xi, a_ref, carry_ref, acc_ref, t):
    ar = [a_ref[q] for q in range(S5_ZQ)]
    ai = [a_ref[S5_ZQ + q] for q in range(S5_ZQ)]
    n_it = t // SCAN_UNROLL

    def step(it, state):
        carry, acc = list(state[0]), list(state[1])
        base = pl.multiple_of((n_it - 1 - it) * (8 * SCAN_UNROLL), 8 * SCAN_UNROLL)
        for tt in reversed(range(SCAN_UNROLL)):
            rows = pl.ds(base + 8 * tt, 8)
            for q in range(S5_ZQ):
                c_r, c_i = carry[q], carry[S5_ZQ + q]
                n_r = ar[q] * c_r + ai[q] * c_i + lr[q, rows, :]
                n_i = ar[q] * c_i - ai[q] * c_r + li[q, rows, :]
                lr[q, rows, :] = n_r
                li[q, rows, :] = n_i
                p_r, p_i = xr[q, rows, :], xi[q, rows, :]
                acc[q] = acc[q] + n_r * p_r + n_i * p_i
                acc[S5_ZQ + q] = acc[S5_ZQ + q] + n_i * p_r - n_r * p_i
                carry[q], carry[S5_ZQ + q] = n_r, n_i
        return tuple(carry), tuple(acc)

    k8 = range(2 * S5_ZQ)
    carry, acc = lax.fori_loop(0, n_it, step, (tuple(carry_ref[k] for k in k8), tuple(acc_ref[k] for k in k8)))
    for k in k8:
        carry_ref[k] = carry[k]
        acc_ref[k] = acc[k]


def _s5_fwd(proj, mix, bbm, ccm, d_row, glu_w, glu_b, tabs, t, side):
    l = proj.shape[0]
    nt = l // t
    n_in = 9

    def body(*refs):
        u_ref, gs_ref, bb_ref, cc_ref, d_ref, gw_ref, gb_ref, a_ref, _ = refs[:n_in]
        side_in = refs[n_in:n_in + len(side.srcs)]
        ssm_ref, xst_ref = refs[n_in + len(side.srcs):n_in + len(side.srcs) + 2]
        side_out = refs[n_in + len(side.srcs) + 2:n_in + len(side.srcs) + 2 + side.n]
        zr, zi, carry = refs[n_in + len(side.srcs) + 2 + side.n:n_in + len(side.srcs) + 5 + side.n]
        sems = refs[n_in + len(side.srcs) + 5 + side.n:]

        @pl.when(pl.program_id(0) == 0)
        def _():
            side.start(side_in, side_out, sems)
            carry[...] = jnp.zeros_like(carry)

        xst_ref[0] = carry[...]
        ub = u_ref[...]
        u = ub.astype(F32)
        for blk in range(S5_NB):
            _z_store(zr, zi, blk, _dot(ub[:, blk * 128:(blk + 1) * 128], bb_ref[blk]), t, 0)
        _z_scan_fwd(zr, zi, a_ref, carry, t, 0)
        ys = jnp.concatenate(
            [_dot(_z_load(zr, zi, blk, t, 0).astype(BF16), cc_ref[blk]) for blk in range(S5_NB)], axis=1)
        y2 = _gelu(ys + d_ref[...] * u)
        z = _dot(y2.astype(BF16), gw_ref[...]) + gb_ref[...]
        ssm_ref[...] = (y2 * _sigmoid(z) * _silu(gs_ref[...].astype(F32))).astype(ssm_ref.dtype)

        @pl.when(pl.program_id(0) == nt - 1)
        def _():
            side.wait(side_in, side_out, sems)

    const2 = lambda shape: pl.BlockSpec(shape, lambda i: (0,) * len(shape))
    zshape = (2 * S5_ZQ, 8, 128)
    return pl.pallas_call(
        body, name="s5_fwd", grid=(nt,),
        in_specs=[pl.BlockSpec((t, D_MODEL), lambda i: (i, 3)), pl.BlockSpec((t, D_MODEL), lambda i: (i, 4)),
                  const2(bbm.shape), const2(ccm.shape), const2((1, D_MODEL)), const2((D_MODEL, D_MODEL)),
                  const2((1, D_MODEL)), const2(zshape), pl.BlockSpec(memory_space=pl.ANY)] + side.in_specs,
        out_specs=(pl.BlockSpec((t, D_MODEL), lambda i: (i, 1)), pl.BlockSpec((1,) + zshape, lambda i: (i, 0, 0, 0)),
                   *side.out_specs),
        out_shape=(jax.ShapeDtypeStruct((l, 2 * D_MODEL), BF16), jax.ShapeDtypeStruct((nt,) + zshape, F32),
                   *side.landing),
        scratch_shapes=[pltpu.VMEM((S5_ZQ, 8 * t, 128), F32), pltpu.VMEM((S5_ZQ, 8 * t, 128), F32),
                        pltpu.VMEM(zshape, F32)] + side.scratch,
        input_output_aliases={8: 0},
        compiler_params=_cp("arbitrary"),
    )(proj, proj, bbm, ccm, d_row, glu_w, glu_b, tabs, mix, *side.srcs)


def _s5_bwd(proj, dmix, dproj, xstart, bbm, ccm, d_row, glu_w, glu_b, tabs, t):
    l = proj.shape[0]
    nt = l // t
    col0 = 2 * RET_QK + 2 * D_MODEL

    def body(u_ref, gs_ref, dm_ref, xst_ref, bb_ref, cc_ref, d_ref, gw_ref, gb_ref, a_ref, _,
             dp_ref, y2_ref, dz_ref, dbb_ref, dcc_ref, da_ref, dd_ref, dgb_ref, xr, xi, lr, li, carry, lcarry,
             dug_s, dug_sem):
        step = pl.program_id(0)
        slot = step % 2
        dug_ref = dug_s.at[slot]

        def put(s, at_step):
            rows = pl.ds(pl.multiple_of((nt - 1 - at_step) * t, t), t)
            return pltpu.make_async_copy(dug_s.at[s], dp_ref.at[rows, pl.ds(col0, 2 * D_MODEL)], dug_sem.at[s])

        @pl.when(step >= 2)
        def _():
            put(slot, step - 2).wait()

        @pl.when(step == 0)
        def _():
            lcarry[...] = jnp.zeros_like(lcarry)
            dbb_ref[...] = jnp.zeros_like(dbb_ref)
            dcc_ref[...] = jnp.zeros_like(dcc_ref)
            da_ref[...] = jnp.zeros_like(da_ref)
            dd_ref[...] = jnp.zeros_like(dd_ref)
            dgb_ref[...] = jnp.zeros_like(dgb_ref)

        carry[...] = xst_ref[0]
        for q in range(S5_ZQ):
            xr[q, 0:8, :] = carry[q]
            xi[q, 0:8, :] = carry[S5_ZQ + q]
        ub = u_ref[...]
        u = ub.astype(F32)
        for blk in range(S5_NB):
            _z_store(xr, xi, blk, _dot(ub[:, blk * 128:(blk + 1) * 128], bb_ref[blk]), t, 8)
        _z_scan_fwd(xr, xi, a_ref, carry, t, 8)
        ys = jnp.concatenate(
            [_dot(_z_load(xr, xi, blk, t, 8).astype(BF16), cc_ref[blk]) for blk in range(S5_NB)], axis=1)
        dv = d_ref[...]
        y1 = ys + dv * u
        y2, dgelu = _gelu_and_grad(y1)
        y2b = y2.astype(BF16)
        sg = _sigmoid(_dot(y2b, gw_ref[...]) + gb_ref[...])
        gs = gs_ref[...].astype(F32)
        dssm = dm_ref[...].astype(F32)
        dug_ref[:, D_MODEL:] = (dssm * (y2 * sg) * _dsilu(gs)).astype(dug_ref.dtype)
        dy3 = dssm * _silu(gs)
        dz = dy3 * y2 * sg * (1.0 - sg)
        dzb = dz.astype(BF16)
        y2_ref[...] = y2b
        dz_ref[...] = dzb
        dgb_ref[...] += jnp.sum(dz, axis=0, keepdims=True)
        dy1 = (dy3 * sg + _dot_nt(dzb, gw_ref[...])) * dgelu
        dd_ref[...] += jnp.sum(dy1 * u, axis=0, keepdims=True)
        dyb = dy1.astype(BF16)
        for blk in range(S5_NB):
            ch = slice(blk * 128, (blk + 1) * 128)
            _z_store(lr, li, blk, _dot_nt(dyb[:, ch], cc_ref[blk]), t, 0)
            dcc_ref[blk] += _dot_tn(_z_load(xr, xi, blk, t, 8).astype(BF16), dyb[:, ch])
        _z_scan_bwd(lr, li, xr, xi, a_ref, lcarry, da_ref, t)
        du = []
        for blk in range(S5_NB):
            lb = _z_load(lr, li, blk, t, 0).astype(BF16)
            du.append(_dot_nt(lb, bb_ref[blk]))
            dbb_ref[blk] += _dot_tn(ub[:, blk * 128:(blk + 1) * 128], lb)
        dug_ref[:, :D_MODEL] = (jnp.concatenate(du, axis=1) + dy1 * dv).astype(dug_ref.dtype)
        put(slot, step).start()

        @pl.when(step == nt - 1)
        def _():
            put(slot, step).wait()
            if nt > 1:
                put(1 - slot, step - 1).wait()

    rev = lambda i: nt - 1 - i
    const2 = lambda shape: pl.BlockSpec(shape, lambda i: (0,) * len(shape))
    row_out = lambda w: pl.BlockSpec((t, w), lambda i: (rev(i), 0))
    zshape = (2 * S5_ZQ, 8, 128)
    hbm = pl.BlockSpec(memory_space=pl.ANY)
    return pl.pallas_call(
        body, name="s5_bwd", grid=(nt,),
        in_specs=[pl.BlockSpec((t, D_MODEL), lambda i: (rev(i), 3)), pl.BlockSpec((t, D_MODEL), lambda i: (rev(i), 4)),
                  pl.BlockSpec((t, D_MODEL), lambda i: (rev(i), 1)),
                  pl.BlockSpec((1,) + zshape, lambda i: (rev(i), 0, 0, 0)),
                  const2(bbm.shape), const2(ccm.shape), const2((1, D_MODEL)), const2((D_MODEL, D_MODEL)),
                  const2((1, D_MODEL)), const2(zshape), hbm],
        out_specs=(hbm, row_out(D_MODEL), row_out(D_MODEL), const2(bbm.shape), const2(ccm.shape),
                   const2(zshape), const2((1, D_MODEL)), const2((1, D_MODEL))),
        out_shape=(jax.ShapeDtypeStruct(dproj.shape, BF16), jax.ShapeDtypeStruct((l, D_MODEL), BF16),
                   jax.ShapeDtypeStruct((l, D_MODEL), BF16), jax.ShapeDtypeStruct(bbm.shape, F32),
                   jax.ShapeDtypeStruct(ccm.shape, F32), jax.ShapeDtypeStruct(zshape, F32),
                   jax.ShapeDtypeStruct((1, D_MODEL), F32), jax.ShapeDtypeStruct((1, D_MODEL), F32)),
        scratch_shapes=[pltpu.VMEM((S5_ZQ, 8 * t + 8, 128), F32), pltpu.VMEM((S5_ZQ, 8 * t + 8, 128), F32),
                        pltpu.VMEM((S5_ZQ, 8 * t, 128), F32), pltpu.VMEM((S5_ZQ, 8 * t, 128), F32),
                        pltpu.VMEM(zshape, F32), pltpu.VMEM(zshape, F32),
                        pltpu.VMEM((2, t, 2 * D_MODEL), BF16), pltpu.SemaphoreType.DMA((2,))],
        input_output_aliases={10: 0},
        compiler_params=_cp("arbitrary"),
    )(proj, proj, dmix, xstart, bbm, ccm, d_row, glu_w, glu_b, tabs, dproj)


def _attn_probs(qh, kh):
    s = _dot_nt(qh, kh) * (XA_DH ** -0.5)
    e = jnp.exp(s - jnp.max(s, axis=-1, keepdims=True))
    return e / jnp.sum(e, axis=-1, keepdims=True)


def _attn_fwd(qa, ka, va):
    l = qa.shape[0]
    m = ka.shape[0]
    tl = _pick(l, (512, 256))

    def body(q_ref, k_ref, v_ref, o_ref):
        for h in range(XA_HEADS):
            hs = slice(h * XA_DH, (h + 1) * XA_DH)
            p = _attn_probs(q_ref[:, hs], k_ref[:, hs])
            o_ref[:, hs] = _dot(p.astype(BF16), v_ref[:, hs]).astype(o_ref.dtype)

    return pl.pallas_call(
        body, name="xattn_fwd", grid=(l // tl,),
        in_specs=[pl.BlockSpec((tl, D_MODEL), lambda i: (i, 0)), pl.BlockSpec((m, D_MODEL), lambda i: (0, 0)),
                  pl.BlockSpec((m, D_MODEL), lambda i: (0, 0))],
        out_specs=pl.BlockSpec((tl, D_MODEL), lambda i: (i, 0)),
        out_shape=jax.ShapeDtypeStruct((l, D_MODEL), BF16), compiler_params=_cp("parallel"),
    )(qa, ka, va)


def _attn_bwd(qa, ka, va, doa):
    l = qa.shape[0]
    m = ka.shape[0]
    tl = _pick(l, (512, 256))

    def body(q_ref, k_ref, v_ref, do_ref, dq_ref, dk_ref, dv_ref):
        @pl.when(pl.program_id(0) == 0)
        def _():
            dk_ref[...] = jnp.zeros_like(dk_ref)
            dv_ref[...] = jnp.zeros_like(dv_ref)

        for h in range(XA_HEADS):
            hs = slice(h * XA_DH, (h + 1) * XA_DH)
            qh, kh, vh, doh = q_ref[:, hs], k_ref[:, hs], v_ref[:, hs], do_ref[:, hs]
            p = _attn_probs(qh, kh)
            dv_ref[:, hs] += _dot_tn(p.astype(BF16), doh)
            dp = _dot_nt(doh, vh)
            ds = (p * (dp - jnp.sum(dp * p, axis=-1, keepdims=True)) * (XA_DH ** -0.5)).astype(BF16)
            dq_ref[:, hs] = _dot(ds, kh).astype(dq_ref.dtype)
            dk_ref[:, hs] += _dot_tn(ds, qh)

    row = pl.BlockSpec((tl, D_MODEL), lambda i: (i, 0))
    mem = pl.BlockSpec((m, D_MODEL), lambda i: (0, 0))
    return pl.pallas_call(
        body, name="xattn_bwd", grid=(l // tl,), in_specs=[row, mem, mem, row], out_specs=(row, mem, mem),
        out_shape=(jax.ShapeDtypeStruct((l, D_MODEL), BF16), jax.ShapeDtypeStruct((m, D_MODEL), F32),
                   jax.ShapeDtypeStruct((m, D_MODEL), F32)),
        compiler_params=_cp("arbitrary"),
    )(qa, ka, va, doa)


def _me_and_peers():
    x, y, c = lax.axis_index("x"), lax.axis_index("y"), lax.axis_index("c")
    flip = lambda v, bit: (1 - v) if bit else v
    peers = []
    for k in range(1, N_DEV):
        px, py, pc = flip(x, (k >> 2) & 1), flip(y, (k >> 1) & 1), flip(c, k & 1)
        peers.append(((px, py, pc), 4 * px + 2 * py + pc))
    return 4 * x + 2 * y + c, peers


class _SideJob:
    def __init__(self, srcs, landing, src_of, dst_of):
        self.srcs = list(srcs)
        self.landing = list(landing)
        self.n = len(self.landing)
        self.src_of, self.dst_of = src_of, dst_of
        hbm = pl.BlockSpec(memory_space=pl.ANY)
        self.in_specs = [hbm] * len(self.srcs)
        self.out_specs = [hbm] * self.n
        self.scratch = [pltpu.SemaphoreType.DMA((self.n * (N_DEV - 1),)), pltpu.SemaphoreType.DMA((self.n * (N_DEV - 1),)),
                        pltpu.SemaphoreType.DMA((self.n,))]

    def _copies(self, src_refs, out_refs, sems):
        send_sems, recv_sems, loc_sems = sems
        me, peers = _me_and_peers()
        local = [pltpu.make_async_copy(self.src_of(a, me, src_refs), self.dst_of(a, me, out_refs), loc_sems.at[a])
                 for a in range(self.n)]
        sends, recvs = [], []
        for k, (peer, peer_idx) in enumerate(peers):
            for a in range(self.n):
                s = self.n * k + a
                sends.append(pltpu.make_async_remote_copy(
                    src_ref=self.src_of(a, peer_idx, src_refs), dst_ref=self.dst_of(a, me, out_refs),
                    send_sem=send_sems.at[s], recv_sem=recv_sems.at[s], device_id=peer, device_id_type=MESH))
                recvs.append(pltpu.make_async_remote_copy(
                    src_ref=self.src_of(a, me, src_refs), dst_ref=self.dst_of(a, peer_idx, out_refs),
                    send_sem=send_sems.at[s], recv_sem=recv_sems.at[s], device_id=peer, device_id_type=MESH))
        return local, sends, recvs

    def start(self, src_refs, out_refs, sems):
        if not self.n:
            return
        local, sends, _ = self._copies(src_refs, out_refs, sems)
        for cp in local + sends:
            cp.start()

    def wait(self, src_refs, out_refs, sems):
        if not self.n:
            return
        local, sends, recvs = self._copies(src_refs, out_refs, sems)
        for cp in recvs:
            cp.wait_recv()
        for cp in sends:
            cp.wait_send()
        for cp in local:
            cp.wait()


def _gather_job(shards):
    return _SideJob(shards, [jax.ShapeDtypeStruct((N_DEV,) + s.shape, s.dtype) for s in shards],
                    src_of=lambda a, j, srcs: srcs[a], dst_of=lambda a, j, outs: outs[a].at[j])


def _scatter_job(grads):
    landing, parts = [], []
    for g in grads:
        if g.ndim == 3:
            landing.append(jax.ShapeDtypeStruct(g.shape, g.dtype))
            parts.append(None)
        else:
            r = g.shape[0] // N_DEV
            landing.append(jax.ShapeDtypeStruct((N_DEV, r, g.shape[1]), g.dtype))
            parts.append(r)

    def src_of(a, j, srcs):
        if parts[a] is None:
            return srcs[a].at[j]
        return srcs[a].at[pl.ds(pl.multiple_of(j * parts[a], 8), parts[a]), :]

    return _SideJob(grads, landing, src_of=src_of, dst_of=lambda a, j, outs: outs[a].at[j])


def _prologue(w_in_shard, row_shards, x, g, pos_col, inv_row):
    n_row = len(row_shards)
    l, d = x.shape
    tr = _pick(l, (1024, 512, 256))
    nt = l // tr
    mid = nt // 2

    def body(*refs):
        win_ref = refs[0]
        row_refs = refs[1:1 + n_row]
        x_ref, g_ref, p_ref, inv_ref = refs[1 + n_row:5 + n_row]
        out_win = refs[5 + n_row]
        row_outs = refs[6 + n_row:6 + 2 * n_row]
        h_ref, cos_ref, sin_ref = refs[6 + 2 * n_row:9 + 2 * n_row]
        win_b, send_sems, recv_sems, local_sem = refs[9 + 2 * n_row:]
        step = pl.program_id(0)
        cx, cy, cc = lax.axis_index("x"), lax.axis_index("y"), lax.axis_index("c")
        me, sibling = (cx, cy, cc), (cx, cy, 1 - cc)
        chips = [(1 - cx, cy), (cx, 1 - cy), (1 - cx, 1 - cy)]
        slot = lambda p: out_win.at[4 * p[0] + 2 * p[1] + p[2]]

        def copy(k, block, to, src=None):
            return pltpu.make_async_remote_copy(
                src_ref=slot(block) if src is None else src, dst_ref=slot(block), send_sem=send_sems.at[k],
                recv_sem=recv_sems.at[k], device_id=to, device_id_type=MESH)

        mine = pltpu.make_async_copy(win_b, slot(me), local_sem)
        first = [copy(0, me, sibling, src=win_b)]
        first += [copy(1 + j, me, (*chip, cc), src=win_b) for j, chip in enumerate(chips)]
        passed = [copy(4 + j, (*chip, cc), sibling) for j, chip in enumerate(chips)]

        @pl.when(step == 0)
        def _():
            win_b[...] = win_ref[...].astype(BF16)
            mine.start()
            for cp in first:
                cp.start()
            for r, o in zip(row_refs, row_outs):
                o[...] = r[...].astype(BF16)

        h_ref[...] = (_rms(x_ref[...])[1] * g_ref[...]).astype(h_ref.dtype)
        ang = p_ref[...].astype(F32) * inv_ref[...]
        lane = lax.broadcasted_iota(jnp.int32, ang.shape, 1)
        cos_ref[...] = jnp.tile(jnp.cos(ang), (1, RET_QK // 128))
        sin_ref[...] = jnp.tile(jnp.where((lane % RET_DK) < RET_DK // 2, -jnp.sin(ang), jnp.sin(ang)),
                                (1, RET_QK // 128))

        @pl.when(step == mid)
        def _():
            for j, chip in enumerate(chips):
                copy(1 + j, (*chip, cc), me).wait_recv()
                passed[j].start()

        @pl.when(step == nt - 1)
        def _():
            copy(0, sibling, me).wait_recv()
            for j, chip in enumerate(chips):
                copy(4 + j, (*chip, 1 - cc), me).wait_recv()
            for cp in first + passed:
                cp.wait_send()
            mine.wait()

    whole = lambda a: pl.BlockSpec(a.shape, lambda i: (0,) * a.ndim)
    rows = lambda w: pl.BlockSpec((tr, w), lambda i: (i, 0))
    return pl.pallas_call(
        body, name="prologue_allgather_w_in", grid=(nt,),
        in_specs=[whole(w_in_shard)] + [whole(r) for r in row_shards] + [rows(d), whole(g), rows(1), whole(inv_row)],
        out_specs=(pl.BlockSpec(memory_space=pl.ANY), *[whole(r) for r in row_shards], rows(d), rows(RET_QK),
                   rows(RET_QK)),
        out_shape=(jax.ShapeDtypeStruct((N_DEV,) + w_in_shard.shape, BF16),
                   *[jax.ShapeDtypeStruct(r.shape, BF16) for r in row_shards],
                   jax.ShapeDtypeStruct((l, d), BF16), jax.ShapeDtypeStruct((l, RET_QK), F32),
                   jax.ShapeDtypeStruct((l, RET_QK), F32)),
        scratch_shapes=[pltpu.VMEM(w_in_shard.shape, BF16), pltpu.SemaphoreType.DMA((N_DEV - 1,)),
                        pltpu.SemaphoreType.DMA((N_DEV - 1,)), pltpu.SemaphoreType.DMA],
        compiler_params=_cp("arbitrary"),
    )(w_in_shard, *row_shards, x, g, pos_col, inv_row)


def _allreduce_small(small):
    rows = SMALL_ROWS // N_DEV

    def body(x_ref, out_ref, land, send1, recv1, send2, recv2):
        me, peers = _me_and_peers()
        block = lambda j: pl.ds(pl.multiple_of(j * rows, 8), rows)

        def phase(src_of, dst_of, send_sems, recv_sems):
            sends = [pltpu.make_async_remote_copy(src_ref=src_of(pidx), dst_ref=dst_of(me), send_sem=send_sems.at[k],
                                                  recv_sem=recv_sems.at[k], device_id=peer, device_id_type=MESH)
                     for k, (peer, pidx) in enumerate(peers)]
            recvs = [pltpu.make_async_remote_copy(src_ref=src_of(me), dst_ref=dst_of(pidx), send_sem=send_sems.at[k],
                                                  recv_sem=recv_sems.at[k], device_id=peer, device_id_type=MESH)
                     for k, (peer, pidx) in enumerate(peers)]
            for cp in sends:
                cp.start()
            for cp in recvs:
                cp.wait_recv()
            for cp in sends:
                cp.wait_send()

        land[me] = x_ref[block(me), :]
        phase(lambda j: x_ref.at[block(j), :], lambda j: land.at[j], send1, recv1)
        total = land[0]
        for j in range(1, N_DEV):
            total = total + land[j]
        out_ref[block(me), :] = total
        phase(lambda j: out_ref.at[block(me), :], lambda j: out_ref.at[block(j), :], send2, recv2)

    vm = pl.BlockSpec(memory_space=pltpu.VMEM)
    return pl.pallas_call(
        body, name="allreduce_small", in_specs=[vm], out_specs=vm, out_shape=jax.ShapeDtypeStruct(small.shape, F32),
        scratch_shapes=[pltpu.VMEM((N_DEV, rows, D_MODEL), F32)] + [pltpu.SemaphoreType.DMA((N_DEV - 1,))] * 4,
    )(small)


def _adamw(name, got, w, m, v):
    r, c = w.shape
    n_slots = got.shape[0]
    tr = _pick(r, (256, 128, 64))

    def body(got_ref, w_ref, m_ref, v_ref, g_ref, d_ref, nm_ref, nv_ref):
        g = got_ref[0].astype(F32)
        for j in range(1, n_slots):
            g = g + got_ref[j].astype(F32)
        nm = ADAM_B1 * m_ref[...] + (1.0 - ADAM_B1) * g
        nv = ADAM_B2 * v_ref[...] + (1.0 - ADAM_B2) * jnp.square(g)
        m_hat = nm / (1.0 - ADAM_B1 ** ADAM_STEP)
        v_hat = nv / (1.0 - ADAM_B2 ** ADAM_STEP)
        g_ref[...] = g
        d_ref[...] = -ADAM_LR * (m_hat / (jnp.sqrt(v_hat) + ADAM_EPS) + ADAM_WD * w_ref[...])
        nm_ref[...] = nm
        nv_ref[...] = nv

    blk = pl.BlockSpec((tr, c), lambda i: (i, 0))
    out = jax.ShapeDtypeStruct((r, c), F32)
    return pl.pallas_call(
        body, name=name, grid=(r // tr,),
        in_specs=[pl.BlockSpec((n_slots, tr, c), lambda i: (0, i, 0)), blk, blk, blk],
        out_specs=(blk, blk, blk, blk), out_shape=(out, out, out, out), compiler_params=_cp("parallel"),
    )(got, w, m, v)


_SMALL_VECS = ("norm1_g", "ret_gn_g", "s5_d", "s5_glu_b", "norm2_g", "norm_mem_g", "norm_f_g")
_SMALL_ORDER = _SMALL_VECS + ("s5_a_re", "s5_a_im", "s5_log_dt", "s5_b_re", "s5_b_im", "s5_c_re", "s5_c_im")


def _small_layout():
    lay, row = {}, 0
    for n in _SMALL_VECS + ("loss",):
        lay[n] = (row, 1, D_MODEL)
        row += 1
    for n in ("s5_a_re", "s5_a_im"):
        lay[n] = (row, 4, D_MODEL)
        row += 4
    lay["s5_log_dt"] = (row, 1, S5_G)
    row += 8
    for n in ("s5_b_re", "s5_b_im", "s5_c_re", "s5_c_im"):
        lay[n] = (row, 64, D_MODEL)
        row += 64
    assert row <= SMALL_ROWS
    return lay


def _pack_small(t, loss_row=None):
    lay = _small_layout()
    pieces = [t[n].reshape(1, D_MODEL) for n in _SMALL_VECS]
    pieces.append(jnp.zeros((1, D_MODEL), F32) if loss_row is None else loss_row)
    pieces += [t["s5_a_re"].reshape(4, D_MODEL), t["s5_a_im"].reshape(4, D_MODEL)]
    pieces.append(jnp.pad(t["s5_log_dt"].reshape(1, S5_G), ((0, 7), (0, D_MODEL - S5_G))))
    pieces += [t[n].reshape(64, D_MODEL) for n in ("s5_b_re", "s5_b_im", "s5_c_re", "s5_c_im")]
    pieces.append(jnp.zeros((SMALL_ROWS - lay["s5_c_im"][0] - 64, D_MODEL), F32))
    return jnp.concatenate(pieces, axis=0)


def _adamw_small(g_sum, w, m, v):
    lay = _small_layout()
    names = [n for n in lay if n != "loss"]

    def body(g_ref, w_ref, m_ref, v_ref, *outs):
        g = g_ref[...]
        nm = ADAM_B1 * m_ref[...] + (1.0 - ADAM_B1) * g
        nv = ADAM_B2 * v_ref[...] + (1.0 - ADAM_B2) * jnp.square(g)
        m_hat = nm / (1.0 - ADAM_B1 ** ADAM_STEP)
        v_hat = nv / (1.0 - ADAM_B2 ** ADAM_STEP)
        delta = -ADAM_LR * (m_hat / (jnp.sqrt(v_hat) + ADAM_EPS) + ADAM_WD * w_ref[...])
        for i, n in enumerate(names):
            r0, rows, lanes = lay[n]
            for part, val in enumerate((g, delta, nm, nv)):
                outs[4 * i + part][...] = val[r0:r0 + rows, 0:lanes]
        r0 = lay["loss"][0]
        outs[-1][...] = g[r0:r0 + 1, :]

    shapes = []
    for n in names:
        shapes += [jax.ShapeDtypeStruct(lay[n][1:], F32)] * 4
    shapes.append(jax.ShapeDtypeStruct((1, D_MODEL), F32))
    outs = pl.pallas_call(body, name="adamw_small", out_shape=tuple(shapes),
                          compiler_params=pltpu.CompilerParams(vmem_limit_bytes=VMEM_LIMIT))(g_sum, w, m, v)
    return {n: tuple(outs[4 * i:4 * i + 4]) for i, n in enumerate(names)}, outs[-1]


_W_NAMES = ("norm1_g", "w_in", "ret_gn_g", "s5_a_re", "s5_a_im", "s5_log_dt", "s5_b_re", "s5_b_im", "s5_c_re", "s5_c_im",
            "s5_d", "s5_glu_w", "s5_glu_b", "w_out", "norm2_g", "norm_mem_g", "xa_wq", "xa_wk", "xa_wv", "xa_wo",
            "norm_f_g")
_ROW_NAMES = ("s5_glu_w", "w_out", "xa_wq", "xa_wk", "xa_wv", "xa_wo")


def kernel(x, mem, positions, norm1_g, w_in, ret_gn_g, s5_a_re, s5_a_im, s5_log_dt, s5_b_re, s5_b_im, s5_c_re, s5_c_im, s5_d, s5_glu_w, s5_glu_b, w_out, norm2_g, norm_mem_g, xa_wq, xa_wk, xa_wv, xa_wo, norm_f_g, loss_target, m_norm1_g, m_w_in, m_ret_gn_g, m_s5_a_re, m_s5_a_im, m_s5_log_dt, m_s5_b_re, m_s5_b_im, m_s5_c_re, m_s5_c_im, m_s5_d, m_s5_glu_w, m_s5_glu_b, m_w_out, m_norm2_g, m_norm_mem_g, m_xa_wq, m_xa_wk, m_xa_wv, m_xa_wo, m_norm_f_g, v_norm1_g, v_w_in, v_ret_gn_g, v_s5_a_re, v_s5_a_im, v_s5_log_dt, v_s5_b_re, v_s5_b_im, v_s5_c_re, v_s5_c_im, v_s5_d, v_s5_glu_w, v_s5_glu_b, v_w_out, v_norm2_g, v_norm_mem_g, v_xa_wq, v_xa_wk, v_xa_wv, v_xa_wo, v_norm_f_g):
    w = dict(norm1_g=norm1_g, w_in=w_in, ret_gn_g=ret_gn_g, s5_a_re=s5_a_re, s5_a_im=s5_a_im, s5_log_dt=s5_log_dt,
             s5_b_re=s5_b_re, s5_b_im=s5_b_im, s5_c_re=s5_c_re, s5_c_im=s5_c_im, s5_d=s5_d, s5_glu_w=s5_glu_w,
             s5_glu_b=s5_glu_b, w_out=w_out, norm2_g=norm2_g, norm_mem_g=norm_mem_g, xa_wq=xa_wq, xa_wk=xa_wk,
             xa_wv=xa_wv, xa_wo=xa_wo, norm_f_g=norm_f_g)
    mom = dict(norm1_g=m_norm1_g, w_in=m_w_in, ret_gn_g=m_ret_gn_g, s5_a_re=m_s5_a_re, s5_a_im=m_s5_a_im,
               s5_log_dt=m_s5_log_dt, s5_b_re=m_s5_b_re, s5_b_im=m_s5_b_im, s5_c_re=m_s5_c_re, s5_c_im=m_s5_c_im,
               s5_d=m_s5_d, s5_glu_w=m_s5_glu_w, s5_glu_b=m_s5_glu_b, w_out=m_w_out, norm2_g=m_norm2_g,
               norm_mem_g=m_norm_mem_g, xa_wq=m_xa_wq, xa_wk=m_xa_wk, xa_wv=m_xa_wv, xa_wo=m_xa_wo,
               norm_f_g=m_norm_f_g)
    var = dict(norm1_g=v_norm1_g, w_in=v_w_in, ret_gn_g=v_ret_gn_g, s5_a_re=v_s5_a_re, s5_a_im=v_s5_a_im,
               s5_log_dt=v_s5_log_dt, s5_b_re=v_s5_b_re, s5_b_im=v_s5_b_im, s5_c_re=v_s5_c_re, s5_c_im=v_s5_c_im,
               s5_d=v_s5_d, s5_glu_w=v_s5_glu_w, s5_glu_b=v_s5_glu_b, w_out=v_w_out, norm2_g=v_norm2_g,
               norm_mem_g=v_norm_mem_g, xa_wq=v_xa_wq, xa_wk=v_xa_wk, xa_wv=v_xa_wv, xa_wo=v_xa_wo,
               norm_f_g=v_norm_f_g)
    shapes = {n: w[n].shape for n in _W_NAMES}

    x2d, mem2d, tgt = x[0], mem[0], loss_target[0]
    l = x2d.shape[0]
    ret_c = _pick(l, (256, 128))
    s5_t = _pick(l, (256, 128))
    g1, g2, gm, gf = norm1_g, norm2_g, norm_mem_g, norm_f_g.reshape(1, D_MODEL)

    half = RET_DK // 2
    inv = ROPE_BASE ** (-jnp.arange(half, dtype=F32) / half)
    win_s, *rest = _prologue(w_in[0], [w[n][0] for n in _ROW_NAMES], x2d, g1, positions[0].reshape(l, 1),
                             jnp.tile(inv, 128 // half)[None, :])
    row_shards_b, (h1, cos_t, sin_t) = rest[:len(_ROW_NAMES)], rest[len(_ROW_NAMES):]

    to_gpn = lambda b: jnp.transpose(b, (0, 2, 1)).reshape(S5_G, S5_P * S5_N)
    from_gpn = lambda b: jnp.transpose(b.reshape(S5_G, S5_P, S5_N), (0, 2, 1))
    disc_args = (s5_a_re[0], s5_a_im[0], s5_log_dt[0].reshape(S5_G, 1), to_gpn(s5_b_re[0]), to_gpn(s5_b_im[0]))
    abar_re, abar_im, bb_re_t, bb_im_t = _s5_discretize(*disc_args)
    bbm, ccm = _s5_block_mats(from_gpn(bb_re_t), from_gpn(bb_im_t), s5_c_re[0], s5_c_im[0])
    a_z = _s5_z(abar_re, abar_im)

    proj, *rows_01 = _mm_nn_slots("in_proj", h1, win_s, BF16, side=_gather_job(row_shards_b[:2]))
    full = {n: g.reshape(N_DEV * r, D_MODEL) for n, g, r in zip(_ROW_NAMES[:2], rows_01, ROW_SHARDS[:2])}
    rconsts = _ret_constants(ret_c)
    ret, o_saved, r_prev, q_rot, k_rot = _ret_fwd(proj, cos_t, sin_t, rconsts, ret_gn_g, ret_c)
    mix, xstart, *rows_xa = _s5_fwd(proj, ret, bbm, ccm, s5_d, full["s5_glu_w"], s5_glu_b, a_z, s5_t,
                                    side=_gather_job(row_shards_b[2:]))
    full.update({n: g.reshape(N_DEV * r, D_MODEL) for n, g, r in zip(_ROW_NAMES[2:], rows_xa, ROW_SHARDS[2:])})
    x1, h2 = _mm_nn("out_proj", mix, full["w_out"], F32, residual=x2d, epi=_epi_norm_fwd(g2))
    mn = _rms_fwd("norm_mem_fwd", mem2d, gm)
    qa = _mm_nn("xa_q", h2, full["xa_wq"], BF16)
    ka = _mm_nn("xa_k", mn, full["xa_wk"], BF16)
    va = _mm_nn("xa_v", mn, full["xa_wv"], BF16)
    oa = _attn_fwd(qa, ka, va)
    dx2, dgf, loss_lanes = _mm_nn("xa_o", oa, full["xa_wo"], F32, residual=x1, epi=_epi_loss(gf, tgt))

    doa = _mm_nt("xa_o_dx", dx2, full["xa_wo"], BF16)
    dwo = _mm_tn("xa_o_dw", oa, dx2, BF16)
    dqa, dka, dva = _attn_bwd(qa, ka, va, doa)
    dx1, dg2 = _mm_nt("xa_q_dx", dqa, full["xa_wq"], F32, epi=_epi_norm_bwd(x1, g2, dx2))
    dwq = _mm_tn("xa_q_dw", h2, dqa, BF16)
    dwk = _mm_tn("xa_k_dw", mn, dka, BF16)
    dwv = _mm_tn("xa_v_dw", mn, dva, BF16)
    dmn = _mm_nt("xa_v_dx", dva, full["xa_wv"], F32, residual=_mm_nt("xa_k_dx", dka, full["xa_wk"], F32))
    _, dgm = _rms_bwd("norm_mem_bwd", mem2d, gm, dmn, None)
    dmix = _mm_nt("out_proj_dx", dx1, full["w_out"], BF16)
    dwout = _mm_tn("out_proj_dw", mix, dx1, BF16)
    dret, dgn, *got_a = _ret_bwd(proj, q_rot, k_rot, cos_t, sin_t, rconsts, ret_gn_g, o_saved, r_prev, dmix, ret_c,
                                 side=_scatter_job([dwout, dwq, dwk, dwv, dwo]))
    dproj, y2, dz, dbbm, dccm, dabar, dd, dgb = _s5_bwd(proj, dmix, dret, xstart, bbm, ccm, s5_d, full["s5_glu_w"],
                                                        s5_glu_b, a_z, s5_t)
    dglu = _mm_tn("s5_glu_dw", y2, dz, BF16)
    dwin_s, got_glu = _mm_tn_slots("in_proj_dw", h1, dproj, N_DEV, BF16, side=_scatter_job([dglu]))
    grad_x, dg1, got_win = _mm_nt_slots("in_proj_dx", dproj, win_s, F32, side=_scatter_job([dwin_s]),
                                        epi=_epi_norm_bwd(x2d, g1, dx1))

    dab_re, dab_im = _s5_unz(dabar)
    dbb_re, dbb_im = _s5_block_diag_bb(dbbm)
    dc_re, dc_im = _s5_block_diag_cc(dccm)
    da_re, da_im, dlog_dt, db_re_t, db_im_t = _s5_discretize_bwd(*disc_args, dab_re, dab_im, to_gpn(dbb_re),
                                                                 to_gpn(dbb_im))
    db_re, db_im = from_gpn(db_re_t), from_gpn(db_im_t)
    small_g = dict(norm1_g=dg1, ret_gn_g=dgn, s5_d=dd, s5_glu_b=dgb, norm2_g=dg2, norm_mem_g=dgm, norm_f_g=dgf,
                   s5_a_re=da_re, s5_a_im=da_im, s5_log_dt=dlog_dt, s5_b_re=db_re, s5_b_im=db_im, s5_c_re=dc_re,
                   s5_c_im=dc_im)
    small_pack = _pack_small(small_g, loss_row=loss_lanes)

    res = {}
    got = dict(zip(("w_out", "xa_wq", "xa_wk", "xa_wv", "xa_wo"), got_a), w_in=got_win, s5_glu_w=got_glu)
    for n in ("w_in",) + _ROW_NAMES:
        res[n] = _adamw("adamw_" + n, got[n], w[n][0], mom[n][0], var[n][0])
    small_sum = _allreduce_small(small_pack)
    small_res, loss_sum = _adamw_small(small_sum, _pack_small(w), _pack_small(mom), _pack_small(var))
    loss = (0.5 / D_MODEL) * jnp.sum(loss_sum)
    res.update(small_res)

    outs = [loss, grad_x[None]]
    for part in range(4):
        for n in _W_NAMES:
            outs.append(res[n][part].reshape(shapes[n]))
    return tuple(outs)
```

```python
import jax
import jax.numpy as jnp
from jax import lax
from jax.experimental import pallas as pl
from jax.experimental.pallas import tpu as pltpu

F32 = jnp.float32
BF16 = jnp.bfloat16
MESH = pl.DeviceIdType.MESH

D_MODEL = 1024
RET_HEADS, RET_DK, RET_DV = 8, 64, 128
RET_QK = RET_HEADS * RET_DK
S5_G, S5_N, S5_P = 64, 64, 16
S5_NB = 8
S5_GB = S5_G // S5_NB
S5_BS = S5_GB * S5_N
S5_COLS = 2 * S5_G * S5_N
XA_HEADS, XA_DH = 4, 256
EPS = 1e-6
ROPE_BASE = 10000.0
N_DEV = 8
W_IN_SHARD = 640
ROW_SHARDS = (128, 256, 128, 128, 128, 128)
ROWPACK = sum(ROW_SHARDS)
SMALL_ROWS = 320
ADAM_LR, ADAM_B1, ADAM_B2, ADAM_EPS, ADAM_WD, ADAM_STEP = 0.001, 0.9, 0.999, 1e-08, 0.01, 10

VMEM_LIMIT = 56 * 1024 * 1024


def _cp(*sem):
    return pltpu.CompilerParams(dimension_semantics=tuple(sem), vmem_limit_bytes=VMEM_LIMIT)


def _dot(a, b):
    return jnp.dot(a, b, preferred_element_type=F32)


def _dot_nt(a, b):
    return lax.dot_general(a, b, (((1,), (1,)), ((), ())), preferred_element_type=F32)


def _dot_tn(a, b):
    return lax.dot_general(a, b, (((0,), (0,)), ((), ())), preferred_element_type=F32)


def _sigmoid(x):
    return 1.0 / (1.0 + jnp.exp(-x))


def _silu(x):
    return x * _sigmoid(x)


def _dsilu(x):
    s = _sigmoid(x)
    return s * (1.0 + x * (1.0 - s))


_GELU_C = 0.7978845608028654


def _gelu(x):
    return 0.5 * x * (1.0 + jnp.tanh(_GELU_C * (x + 0.044715 * (x * x * x))))


def _gelu_and_grad(x):
    t = jnp.tanh(_GELU_C * (x + 0.044715 * (x * x * x)))
    half = 0.5 * (1.0 + t)
    return x * half, half + 0.5 * x * (1.0 - t * t) * (_GELU_C * (1.0 + 3.0 * 0.044715 * (x * x)))


def _pick(n, cands):
    for c in cands:
        if n % c == 0:
            return c
    return n


class _Epilogue:
    def __init__(self, rows, vecs, row_out_dtypes, n_sums, fn):
        self.rows, self.vecs, self.row_out_dtypes, self.n_sums, self.fn = list(rows), list(vecs), list(row_out_dtypes), n_sums, fn


def _rms(x):
    rs = lax.rsqrt(jnp.mean(x * x, axis=-1, keepdims=True) + EPS)
    return rs, x * rs


def _rms_dx(dn, xn, rs):
    return rs * (dn - xn * jnp.mean(dn * xn, axis=-1, keepdims=True))


def _epi_norm_fwd(g):
    def fn(r, rows, vecs):
        return r, [_rms(r)[1] * vecs[0]], []

    return _Epilogue([], [g], [BF16], 0, fn)


def _epi_loss(gf, target):
    def fn(r, rows, vecs):
        rs, xn = _rms(r)
        e = xn * vecs[0] - rows[0]
        dy = e * (1.0 / r.shape[-1])
        return (_rms_dx(dy * vecs[0], xn, rs), [],
                [jnp.sum(dy * xn, axis=0, keepdims=True), jnp.sum(e * e, axis=0, keepdims=True)])

    return _Epilogue([target], [gf], [], 2, fn)


def _epi_norm_bwd(x, g, dres):
    def fn(r, rows, vecs):
        rs, xn = _rms(rows[0])
        return _rms_dx(r * vecs[0], xn, rs) + rows[1], [], [jnp.sum(r * xn, axis=0, keepdims=True)]

    return _Epilogue([x, dres], [g], [], 1, fn)


def _mm_core(name, operands, in_specs, out_spec, out_shape, grid, nk, dims, acc_shape, has_res, side=None, epi=None):
    n_in = 3 if has_res else 2
    n_epi_in = len(epi.rows) + len(epi.vecs) if epi else 0
    n_epi_out = len(epi.row_out_dtypes) + epi.n_sums if epi else 0
    n_side_in = len(side.srcs) if side else 0
    n_side_out = side.n if side else 0

    def body(*refs):
        a_ref, b_ref = refs[0], refs[1]
        r_ref = refs[2] if has_res else None
        epi_in = refs[n_in:n_in + n_epi_in]
        side_in = refs[n_in + n_epi_in:n_in + n_epi_in + n_side_in]
        n0 = n_in + n_epi_in + n_side_in
        o_ref = refs[n0]
        epi_out = refs[n0 + 1:n0 + 1 + n_epi_out]
        side_out = refs[n0 + 1 + n_epi_out:n0 + 1 + n_epi_out + n_side_out]
        rest = refs[n0 + 1 + n_epi_out + n_side_out:]
        acc, sems = (rest[0], rest[1:]) if nk > 1 else (None, rest)
        i, j, k = pl.program_id(0), pl.program_id(1), pl.program_id(2)
        if side:
            @pl.when((i == 0) & (j == 0) & (k == 0))
            def _():
                side.start(side_in, side_out, sems)

        def product():
            if len(b_ref.shape) == 3:
                ns = b_ref.shape[2]
                return sum(lax.dot_general(a_ref[:, p * ns:(p + 1) * ns].astype(BF16), b_ref[p].astype(BF16),
                                           (dims, ((), ())), preferred_element_type=F32)
                           for p in range(b_ref.shape[0]))
            return lax.dot_general(a_ref[...].astype(BF16), b_ref[...].astype(BF16), (dims, ((), ())),
                                   preferred_element_type=F32)

        def finish(r):
            if has_res:
                r = r + r_ref[...]
            if epi is None:
                o_ref[...] = r.astype(o_ref.dtype)
                return
            n_rows = len(epi.rows)
            main, row_vals, sums = epi.fn(r, [t[...] for t in epi_in[:n_rows]], [t[...] for t in epi_in[n_rows:]])
            o_ref[...] = main.astype(o_ref.dtype)
            for ref, val in zip(epi_out, row_vals):
                ref[...] = val.astype(ref.dtype)
            for ref, val in zip(epi_out[len(row_vals):], sums):
                @pl.when(i == 0)
                def _(ref=ref):
                    ref[...] = jnp.zeros_like(ref)

                ref[...] += val

        if nk == 1:
            finish(product())
        else:
            @pl.when(k == 0)
            def _():
                acc[...] = jnp.zeros_like(acc)

            acc[...] += product()

            @pl.when(k == nk - 1)
            def _():
                finish(acc[...])

        if side:
            @pl.when((i == grid[0] - 1) & (j == grid[1] - 1) & (k == grid[2] - 1))
            def _():
                side.wait(side_in, side_out, sems)

    acc_scratch = [pltpu.VMEM(acc_shape, F32)] if nk > 1 else []
    in_specs, out_specs, out_shapes, operands = list(in_specs), [out_spec], [out_shape], list(operands)
    if epi:
        assert grid[1] == 1, "an epilogue needs tiles that span whole rows"
        tm, n = out_spec.block_shape
        row_spec = pl.BlockSpec((tm, n), lambda i, j, k: (i, 0))
        vec_spec = pl.BlockSpec((1, n), lambda i, j, k: (0, 0))
        in_specs += [row_spec] * len(epi.rows) + [vec_spec] * len(epi.vecs)
        operands += epi.rows + epi.vecs
        out_specs += [row_spec] * len(epi.row_out_dtypes) + [vec_spec] * epi.n_sums
        out_shapes += [jax.ShapeDtypeStruct(out_shape.shape, d) for d in epi.row_out_dtypes]
        out_shapes += [jax.ShapeDtypeStruct((1, n), F32)] * epi.n_sums
    scratch = acc_scratch
    if side:
        in_specs += side.in_specs
        operands += side.srcs
        out_specs += side.out_specs
        out_shapes += side.landing
        scratch = acc_scratch + side.scratch
    plain = side is None and epi is None
    res = pl.pallas_call(
        body, name=name, grid=grid, in_specs=in_specs, out_specs=out_specs[0] if plain else tuple(out_specs),
        out_shape=out_shapes[0] if plain else tuple(out_shapes), scratch_shapes=scratch,
        compiler_params=_cp("parallel", "parallel", "arbitrary") if plain else _cp("arbitrary", "arbitrary", "arbitrary"),
    )(*operands)
    return res


def _mm_nn(name, a, b, out_dtype, residual=None, epi=None):
    m, kk = a.shape
    n = b.shape[1]
    tm, tn, tk = _pick(m, (1024, 512, 256)), _pick(n, (1024, 512)), _pick(kk, (1024, 512))
    ops = [a, b]
    specs = [pl.BlockSpec((tm, tk), lambda i, j, k: (i, k)), pl.BlockSpec((tk, tn), lambda i, j, k: (k, j))]
    if residual is not None:
        ops.append(residual)
        specs.append(pl.BlockSpec((tm, tn), lambda i, j, k: (i, j)))
    return _mm_core(name, ops, specs, pl.BlockSpec((tm, tn), lambda i, j, k: (i, j)),
                    jax.ShapeDtypeStruct((m, n), out_dtype), (m // tm, n // tn, kk // tk), kk // tk,
                    ((1,), (0,)), (tm, tn), residual is not None, epi=epi)


def _mm_nt(name, a, b, out_dtype, residual=None, epi=None):
    m, kk = a.shape
    n = b.shape[0]
    tm, tn, tk = _pick(m, (1024, 512, 256)), _pick(n, (1024, 512)), _pick(kk, (1024, 512))
    ops = [a, b]
    specs = [pl.BlockSpec((tm, tk), lambda i, j, k: (i, k)), pl.BlockSpec((tn, tk), lambda i, j, k: (j, k))]
    if residual is not None:
        ops.append(residual)
        specs.append(pl.BlockSpec((tm, tn), lambda i, j, k: (i, j)))
    return _mm_core(name, ops, specs, pl.BlockSpec((tm, tn), lambda i, j, k: (i, j)),
                    jax.ShapeDtypeStruct((m, n), out_dtype), (m // tm, n // tn, kk // tk), kk // tk,
                    ((1,), (1,)), (tm, tn), residual is not None, epi=epi)


def _mm_tn(name, a, b, out_dtype):
    kk, m = a.shape
    n = b.shape[1]
    tm, tn, tk = _pick(m, (1024, 512)), _pick(n, (1024, 512)), _pick(kk, (2048, 1024, 512, 256))
    specs = [pl.BlockSpec((tk, tm), lambda i, j, k: (k, i)), pl.BlockSpec((tk, tn), lambda i, j, k: (k, j))]
    return _mm_core(name, [a, b], specs, pl.BlockSpec((tm, tn), lambda i, j, k: (i, j)),
                    jax.ShapeDtypeStruct((m, n), out_dtype), (m // tm, n // tn, kk // tk), kk // tk,
                    ((0,), (0,)), (tm, tn), False)


def _mm_nn_slots(name, a, b_slots, out_dtype, side=None):
    m, kk = a.shape
    s, _, ns = b_slots.shape
    tm, tk = _pick(m, (2048, 1024, 512, 256)), _pick(kk, (1024, 512))
    specs = [pl.BlockSpec((tm, tk), lambda i, j, k: (i, k)), pl.BlockSpec((None, tk, ns), lambda i, j, k: (j, k, 0))]
    return _mm_core(name, [a, b_slots], specs, pl.BlockSpec((tm, ns), lambda i, j, k: (i, j)),
                    jax.ShapeDtypeStruct((m, s * ns), out_dtype), (m // tm, s, kk // tk), kk // tk,
                    ((1,), (0,)), (tm, ns), False, side)


def _mm_nt_slots(name, a, b_slots, out_dtype, side=None, epi=None):
    m = a.shape[0]
    s, n, ns = b_slots.shape
    tm, tn = _pick(m, (1024, 512, 256)), _pick(n, (1024, 512))
    per = 2 if s % 2 == 0 else 1
    specs = [pl.BlockSpec((tm, per * ns), lambda i, j, k: (i, k)),
             pl.BlockSpec((per, tn, ns), lambda i, j, k: (k, j, 0))]
    return _mm_core(name, [a, b_slots], specs, pl.BlockSpec((tm, tn), lambda i, j, k: (i, j)),
                    jax.ShapeDtypeStruct((m, n), out_dtype), (m // tm, n // tn, s // per), s // per,
                    ((1,), (1,)), (tm, tn), False, side, epi)


def _mm_tn_slots(name, a, b, s, out_dtype, side=None):
    kk, m = a.shape
    ns = b.shape[1] // s
    tm, tk = _pick(m, (1024, 512)), _pick(kk, (4096, 2048, 1024, 512, 256))
    specs = [pl.BlockSpec((tk, tm), lambda i, j, k: (k, i)), pl.BlockSpec((tk, ns), lambda i, j, k: (k, j))]
    return _mm_core(name, [a, b], specs, pl.BlockSpec((None, tm, ns), lambda i, j, k: (j, i, 0)),
                    jax.ShapeDtypeStruct((s, m, ns), out_dtype), (m // tm, s, kk // tk), kk // tk,
                    ((0,), (0,)), (tm, ns), False, side)


def _rms_fwd(name, x, g):
    r, d = x.shape
    tr = _pick(r, (1024, 512, 256))

    def body(x_ref, g_ref, o_ref):
        xv = x_ref[...]
        rs = lax.rsqrt(jnp.mean(xv * xv, axis=-1, keepdims=True) + EPS)
        o_ref[...] = (xv * rs * g_ref[...]).astype(o_ref.dtype)

    return pl.pallas_call(
        body, name=name, grid=(r // tr,),
        in_specs=[pl.BlockSpec((tr, d), lambda i: (i, 0)), pl.BlockSpec((1, d), lambda i: (0, 0))],
        out_specs=pl.BlockSpec((tr, d), lambda i: (i, 0)),
        out_shape=jax.ShapeDtypeStruct((r, d), BF16), compiler_params=_cp("parallel"),
    )(x, g)


def _rms_bwd(name, x, g, dh, dres):
    r, d = x.shape
    tr = _pick(r, (512, 256))
    has_res = dres is not None

    def body(*refs):
        if has_res:
            x_ref, g_ref, dh_ref, dr_ref, dx_ref, dg_ref = refs
        else:
            x_ref, g_ref, dh_ref, dx_ref, dg_ref = refs
        i = pl.program_id(0)

        @pl.when(i == 0)
        def _():
            dg_ref[...] = jnp.zeros_like(dg_ref)

        xv = x_ref[...]
        dhv = dh_ref[...].astype(F32)
        rs = lax.rsqrt(jnp.mean(xv * xv, axis=-1, keepdims=True) + EPS)
        xn = xv * rs
        dg_ref[...] += jnp.sum(dhv * xn, axis=0, keepdims=True)
        dn = dhv * g_ref[...]
        dx = rs * (dn - xn * jnp.mean(dn * xn, axis=-1, keepdims=True))
        if has_res:
            dx = dx + dr_ref[...]
        dx_ref[...] = dx

    row = pl.BlockSpec((tr, d), lambda i: (i, 0))
    vec = pl.BlockSpec((1, d), lambda i: (0, 0))
    ops = [x, g, dh] + ([dres] if has_res else [])
    return pl.pallas_call(
        body, name=name, grid=(r // tr,),
        in_specs=[row, vec, row] + ([row] if has_res else []),
        out_specs=(row, vec),
        out_shape=(jax.ShapeDtypeStruct((r, d), F32), jax.ShapeDtypeStruct((1, d), F32)),
        compiler_params=_cp("arbitrary"),
    )(*ops)


def _rot(x, cos_t, sin_t):
    n = x.shape[-1]
    lane = lax.broadcasted_iota(jnp.int32, x.shape, 1)
    partner = jnp.where((lane % RET_DK) < RET_DK // 2, pltpu.roll(x, n - RET_DK // 2, 1), pltpu.roll(x, RET_DK // 2, 1))
    return x * cos_t + partner * sin_t


def _ret_constants(c):
    log_g = jnp.log1p(-jnp.exp2(-5.0 - jnp.arange(RET_HEADS, dtype=F32)))
    j = jnp.arange(c, dtype=F32)
    diff = j[:, None] - j[None, :]
    decay = jnp.where(diff[None] >= 0.0, jnp.exp(log_g[:, None, None] * jnp.maximum(diff, 0.0)[None]), 0.0)
    q_w = jnp.exp(log_g[None, :] * (j + 1.0)[:, None])
    k_w = jnp.exp(log_g[None, :] * (c - 1.0 - j)[:, None])
    cd = jnp.exp(log_g * c)
    rep = lambda t: jnp.repeat(t, RET_DK, axis=1)
    cd_row = jnp.repeat(cd, RET_DV)[None, :]
    return decay, rep(q_w), rep(k_w), cd_row


def _pair_of(h, c):
    lane = lax.broadcasted_iota(jnp.int32, (c, 2 * RET_DK), 1)
    mine = (lane < RET_DK) if h % 2 == 0 else (lane >= RET_DK)
    return slice((h // 2) * 2 * RET_DK, (h // 2 + 1) * 2 * RET_DK), mine


def _keep(x, mine):
    return jnp.where(mine, x, jnp.zeros_like(x))


def _ret_fwd(proj, cos_t, sin_t, consts, gn_g, c):
    l = proj.shape[0]
    nc = l // c
    decay, qw, kw, cd_row = consts

    def body(q_ref, k_ref, v_ref, g_ref, cos_ref, sin_ref, dec_ref, qw_ref, kw_ref, cd_ref, gn_ref,
             ret_ref, o_ref, rp_ref, qb_ref, kb_ref, state):
        @pl.when(pl.program_id(0) == 0)
        def _():
            state[...] = jnp.zeros_like(state)

        cs, sn = cos_ref[...], sin_ref[...]
        qr = _rot(q_ref[...].astype(F32), cs, sn)
        kr = _rot(k_ref[...].astype(F32), cs, sn) * (RET_DK ** -0.5)
        qb, kb = qr.astype(BF16), kr.astype(BF16)
        qb_ref[...] = qb
        kb_ref[...] = kb
        qwb = (qr * qw_ref[...]).astype(BF16)
        kwb = (kr * kw_ref[...]).astype(BF16)
        vb = v_ref[...].astype(BF16)
        for h in range(RET_HEADS):
            ps, mine = _pair_of(h, c)
            vs = slice(h * RET_DV, (h + 1) * RET_DV)
            s = _dot_nt(_keep(qb[:, ps], mine), kb[:, ps]) * dec_ref[h]
            r_prev = state[h]
            rp_ref[0, h] = r_prev
            o = _dot(s.astype(BF16), vb[:, vs]) + _dot(_keep(qwb[:, ps], mine), r_prev.astype(BF16))
            state[h] = cd_ref[:, vs] * r_prev + _dot_tn(_keep(kwb[:, ps], mine), vb[:, vs])
            o_ref[:, vs] = o
            mu = jnp.mean(o, axis=-1, keepdims=True)
            var = jnp.mean(jnp.square(o - mu), axis=-1, keepdims=True)
            on = (o - mu) * lax.rsqrt(var + EPS)
            ret_ref[:, vs] = (on * gn_ref[:, vs] * _silu(g_ref[:, vs].astype(F32))).astype(ret_ref.dtype)

    const2 = lambda shape: pl.BlockSpec(shape, lambda i: (0,) * len(shape))
    return pl.pallas_call(
        body, name="retention_fwd", grid=(nc,),
        in_specs=[pl.BlockSpec((c, RET_QK), lambda i: (i, 0)), pl.BlockSpec((c, RET_QK), lambda i: (i, 1)),
                  pl.BlockSpec((c, D_MODEL), lambda i: (i, 1)), pl.BlockSpec((c, D_MODEL), lambda i: (i, 2)),
                  pl.BlockSpec((c, RET_QK), lambda i: (i, 0)), pl.BlockSpec((c, RET_QK), lambda i: (i, 0)),
                  const2((RET_HEADS, c, c)), const2((c, RET_QK)), const2((c, RET_QK)), const2((1, D_MODEL)),
                  const2((1, D_MODEL))],
        out_specs=(pl.BlockSpec((c, D_MODEL), lambda i: (i, 0)), pl.BlockSpec((c, D_MODEL), lambda i: (i, 0)),
                   pl.BlockSpec((1, RET_HEADS, 2 * RET_DK, RET_DV), lambda i: (i, 0, 0, 0)),
                   pl.BlockSpec((c, RET_QK), lambda i: (i, 0)), pl.BlockSpec((c, RET_QK), lambda i: (i, 0))),
        out_shape=(jax.ShapeDtypeStruct((l, 2 * D_MODEL), BF16), jax.ShapeDtypeStruct((l, D_MODEL), F32),
                   jax.ShapeDtypeStruct((nc, RET_HEADS, 2 * RET_DK, RET_DV), F32),
                   jax.ShapeDtypeStruct((l, RET_QK), BF16), jax.ShapeDtypeStruct((l, RET_QK), BF16)),
        scratch_shapes=[pltpu.VMEM((RET_HEADS, 2 * RET_DK, RET_DV), F32)],
        compiler_params=_cp("arbitrary"),
    )(proj, proj, proj, proj, cos_t, sin_t, decay, qw, kw, cd_row, gn_g)


def _ret_bwd(proj, qb_saved, kb_saved, cos_t, sin_t, consts, gn_g, o_saved, r_prev_saved, dmix, c, side):
    l = proj.shape[0]
    nc = l // c
    decay, qw, kw, cd_row = consts
    n_in = 14

    def body(*refs):
        (q_ref, k_ref, v_ref, g_ref, cos_ref, sin_ref, dec_ref, qw_ref, kw_ref, cd_ref, gn_ref, o_ref, rp_ref,
         dr_ref) = refs[:n_in]
        side_in = refs[n_in:n_in + len(side.srcs)]
        out_ref, dgn_ref = refs[n_in + len(side.srcs):n_in + len(side.srcs) + 2]
        side_out = refs[n_in + len(side.srcs) + 2:n_in + len(side.srcs) + 2 + side.n]
        state, dq_s, dk_s = refs[n_in + len(side.srcs) + 2 + side.n:n_in + len(side.srcs) + 5 + side.n]
        sems = refs[n_in + len(side.srcs) + 5 + side.n:]

        @pl.when(pl.program_id(0) == 0)
        def _():
            side.start(side_in, side_out, sems)
            state[...] = jnp.zeros_like(state)
            dgn_ref[...] = jnp.zeros_like(dgn_ref)

        cs, sn = cos_ref[...], sin_ref[...]
        qb, kb = q_ref[...], k_ref[...]
        qwv, kwv = qw_ref[...], kw_ref[...]
        qwb = (qb.astype(F32) * qwv).astype(BF16)
        kwb = (kb.astype(F32) * kwv).astype(BF16)
        vb = v_ref[...].astype(BF16)
        dq2 = dk2 = None
        for h in range(RET_HEADS):
            ps, mine = _pair_of(h, c)
            vs = slice(h * RET_DV, (h + 1) * RET_DV)
            dec = dec_ref[h]
            qm, km = _keep(qb[:, ps], mine), _keep(kb[:, ps], mine)
            o = o_ref[:, vs]
            mu = jnp.mean(o, axis=-1, keepdims=True)
            var = jnp.mean(jnp.square(o - mu), axis=-1, keepdims=True)
            rstd = lax.rsqrt(var + EPS)
            on = (o - mu) * rstd
            gate = g_ref[:, vs].astype(F32)
            sg = _silu(gate)
            dret = dr_ref[:, vs].astype(F32)
            gn = gn_ref[:, vs]
            dgn_ref[:, vs] += jnp.sum(dret * on * sg, axis=0, keepdims=True)
            out_ref[:, 2 * RET_QK + D_MODEL + h * RET_DV:2 * RET_QK + D_MODEL + (h + 1) * RET_DV] = (
                dret * on * gn * _dsilu(gate)).astype(out_ref.dtype)
            don = dret * gn * sg
            do = rstd * (don - jnp.mean(don, axis=-1, keepdims=True)
                         - on * jnp.mean(don * on, axis=-1, keepdims=True))
            dob = do.astype(BF16)
            sn_h = state[h]
            snb = sn_h.astype(BF16)
            s = _dot_nt(qm, kb[:, ps]) * dec
            dv = _dot_tn(s.astype(BF16), dob) + _dot(_keep(kwb[:, ps], mine), snb)
            out_ref[:, 2 * RET_QK + h * RET_DV:2 * RET_QK + (h + 1) * RET_DV] = dv.astype(out_ref.dtype)
            ds = (_dot_nt(dob, vb[:, vs]) * dec).astype(BF16)
            dq_h = _dot(ds, km) + qwv[:, ps] * _dot_nt(dob, rp_ref[0, h].astype(BF16))
            dk_h = _dot_tn(ds, qm) + kwv[:, ps] * _dot_nt(vb[:, vs], snb)
            state[h] = cd_ref[:, vs] * sn_h + _dot_tn(_keep(qwb[:, ps], mine), dob)
            if h % 2 == 0:
                dq2, dk2 = dq_h, dk_h
            else:
                dq_s[:, ps] = dq2 + dq_h
                dk_s[:, ps] = dk2 + dk_h
        out_ref[:, 0:RET_QK] = _rot(dq_s[...], cs, -sn).astype(out_ref.dtype)
        out_ref[:, RET_QK:2 * RET_QK] = (_rot(dk_s[...], cs, -sn) * (RET_DK ** -0.5)).astype(out_ref.dtype)

        @pl.when(pl.program_id(0) == nc - 1)
        def _():
            side.wait(side_in, side_out, sems)

    rev = lambda i: nc - 1 - i
    const2 = lambda shape: pl.BlockSpec(shape, lambda i: (0,) * len(shape))
    return pl.pallas_call(
        body, name="retention_bwd", grid=(nc,),
        in_specs=[pl.BlockSpec((c, RET_QK), lambda i: (rev(i), 0)), pl.BlockSpec((c, RET_QK), lambda i: (rev(i), 0)),
                  pl.BlockSpec((c, D_MODEL), lambda i: (rev(i), 1)), pl.BlockSpec((c, D_MODEL), lambda i: (rev(i), 2)),
                  pl.BlockSpec((c, RET_QK), lambda i: (rev(i), 0)), pl.BlockSpec((c, RET_QK), lambda i: (rev(i), 0)),
                  const2((RET_HEADS, c, c)), const2((c, RET_QK)), const2((c, RET_QK)), const2((1, D_MODEL)),
                  const2((1, D_MODEL)),
                  pl.BlockSpec((c, D_MODEL), lambda i: (rev(i), 0)),
                  pl.BlockSpec((1, RET_HEADS, 2 * RET_DK, RET_DV), lambda i: (rev(i), 0, 0, 0)),
                  pl.BlockSpec((c, D_MODEL), lambda i: (rev(i), 0))] + side.in_specs,
        out_specs=(pl.BlockSpec((c, 2 * RET_QK + 2 * D_MODEL), lambda i: (rev(i), 0)), const2((1, D_MODEL)),
                   *side.out_specs),
        out_shape=(jax.ShapeDtypeStruct((l, 2 * RET_QK + 4 * D_MODEL), BF16), jax.ShapeDtypeStruct((1, D_MODEL), F32),
                   *side.landing),
        scratch_shapes=[pltpu.VMEM((RET_HEADS, 2 * RET_DK, RET_DV), F32), pltpu.VMEM((c, RET_QK), F32),
                        pltpu.VMEM((c, RET_QK), F32)] + side.scratch,
        compiler_params=_cp("arbitrary"),
    )(qb_saved, kb_saved, proj, proj, cos_t, sin_t, decay, qw, kw, cd_row, gn_g, o_saved, r_prev_saved, dmix,
      *side.srcs)


def _zoh(a_re, a_im, log_dt):
    dt = jnp.exp(log_dt)
    mag = jnp.exp(a_re * dt)
    abar_re = mag * jnp.cos(a_im * dt)
    abar_im = mag * jnp.sin(a_im * dt)
    den = a_re * a_re + a_im * a_im
    nr, ni = abar_re - 1.0, abar_im
    f_re = (nr * a_re + ni * a_im) / den
    f_im = (ni * a_re - nr * a_im) / den
    return dt, abar_re, abar_im, f_re, f_im, den


def _lanes_p(f):
    return jnp.tile(f, (1, S5_P))


def _s5_discretize(a_re, a_im, log_dt, b_re_t, b_im_t):
    def body(ar_ref, ai_ref, ld_ref, br_ref, bi_ref, abr_ref, abi_ref, bbr_ref, bbi_ref):
        _, abar_re, abar_im, f_re, f_im, _ = _zoh(ar_ref[...], ai_ref[...], ld_ref[...])
        abr_ref[...] = abar_re
        abi_ref[...] = abar_im
        fr, fi = _lanes_p(f_re), _lanes_p(f_im)
        bbr_ref[...] = fr * br_ref[...] - fi * bi_ref[...]
        bbi_ref[...] = fr * bi_ref[...] + fi * br_ref[...]

    gn = jax.ShapeDtypeStruct((S5_G, S5_N), F32)
    gpn = jax.ShapeDtypeStruct((S5_G, S5_P * S5_N), F32)
    return pl.pallas_call(body, name="s5_discretize", out_shape=(gn, gn, gpn, gpn))(a_re, a_im, log_dt, b_re_t, b_im_t)


def _s5_discretize_bwd(a_re, a_im, log_dt, b_re_t, b_im_t, dab_re, dab_im, dbb_re_t, dbb_im_t):
    def body(ar_ref, ai_ref, ld_ref, br_ref, bi_ref, gar_ref, gai_ref, gbr_ref, gbi_ref,
             dar_ref, dai_ref, dld_ref, dbr_ref, dbi_ref):
        a_r, a_i = ar_ref[...], ai_ref[...]
        dt, abar_re, abar_im, f_re, f_im, den = _zoh(a_r, a_i, ld_ref[...])
        b_r, b_i, g_br, g_bi = br_ref[...], bi_ref[...], gbr_ref[...], gbi_ref[...]
        fr, fi = _lanes_p(f_re), _lanes_p(f_im)
        dbr_ref[...] = fr * g_br + fi * g_bi
        dbi_ref[...] = fr * g_bi - fi * g_br
        t_r = b_r * g_br + b_i * g_bi
        t_i = b_r * g_bi - b_i * g_br
        gf_r = sum(t_r[:, p * S5_N:(p + 1) * S5_N] for p in range(S5_P))
        gf_i = sum(t_i[:, p * S5_N:(p + 1) * S5_N] for p in range(S5_P))
        inv_r, inv_i = a_r / den, a_i / den
        ga_r = gar_ref[...] + gf_r * inv_r - gf_i * inv_i
        ga_i = gai_ref[...] + gf_r * inv_i + gf_i * inv_r
        q_r = -(f_re * a_r + f_im * a_i) / den
        q_i = -(f_im * a_r - f_re * a_i) / den
        gl_r = q_r * gf_r + q_i * gf_i
        gl_i = q_r * gf_i - q_i * gf_r
        dar_ref[...] = gl_r + dt * (abar_re * ga_r + abar_im * ga_i)
        dai_ref[...] = gl_i + dt * (abar_re * ga_i - abar_im * ga_r)
        la_r = a_r * abar_re - a_i * abar_im
        la_i = a_r * abar_im + a_i * abar_re
        dld_ref[...] = dt * jnp.sum(ga_r * la_r + ga_i * la_i, axis=-1, keepdims=True)

    gn = jax.ShapeDtypeStruct((S5_G, S5_N), F32)
    gpn = jax.ShapeDtypeStruct((S5_G, S5_P * S5_N), F32)
    return pl.pallas_call(
        body, name="s5_discretize_bwd", out_shape=(gn, gn, jax.ShapeDtypeStruct((S5_G, 1), F32), gpn, gpn),
    )(a_re, a_im, log_dt, b_re_t, b_im_t, dab_re, dab_im, dbb_re_t, dbb_im_t)


S5_ZQ = S5_NB // 2


def _s5_z(re, im):
    return jnp.concatenate([re.reshape(S5_ZQ, 8, 128), im.reshape(S5_ZQ, 8, 128)], axis=0)


def _s5_unz(z):
    return z[:S5_ZQ].reshape(S5_G, S5_N), z[S5_ZQ:].reshape(S5_G, S5_N)


def _s5_block_mats(bb_re, bb_im, c_re, c_im):
    eye = jnp.eye(S5_GB, dtype=F32)
    bb = jnp.stack([bb_re, bb_im], axis=0).reshape(2, S5_NB, S5_GB, S5_N, S5_P)
    bbm = jnp.einsum("rbgnp,gh->bgprhn", bb, eye).reshape(S5_NB, S5_GB * S5_P, 2 * S5_BS)
    cc = jnp.stack([c_re, -c_im], axis=0).reshape(2, S5_NB, S5_GB, S5_P, S5_N)
    ccm = jnp.einsum("rbgpn,gh->brhngp", cc, eye).reshape(S5_NB, 2 * S5_BS, S5_GB * S5_P)
    return bbm.astype(BF16), ccm.astype(BF16)


def _s5_block_diag_bb(m):
    t = m.reshape(S5_NB, S5_GB, S5_P, 2, S5_GB, S5_N)
    d = jnp.einsum("bgprgn->rbgnp", t).reshape(2, S5_G, S5_N, S5_P)
    return d[0], d[1]


def _s5_block_diag_cc(m):
    t = m.reshape(S5_NB, 2, S5_GB, S5_N, S5_GB, S5_P)
    d = jnp.einsum("brgngp->rbgpn", t).reshape(2, S5_G, S5_P, S5_N)
    return d[0], -d[1]


SCAN_UNROLL = 8


def _z_store(zr, zi, blk, res, t, off):
    q, h = blk // 2, blk % 2
    for lt in range(4):
        zr[q, pl.ds(off + 4 * h + lt, t, stride=8), :] = res[:, lt * 128:(lt + 1) * 128]
        zi[q, pl.ds(off + 4 * h + lt, t, stride=8), :] = res[:, S5_BS + lt * 128:S5_BS + (lt + 1) * 128]


def _z_load(zr, zi, blk, t, off):
    q, h = blk // 2, blk % 2
    return jnp.concatenate([zr[q, pl.ds(off + 4 * h + lt, t, stride=8), :] for lt in range(4)]
                           + [zi[q, pl.ds(off + 4 * h + lt, t, stride=8), :] for lt in range(4)], axis=1)


def _z_scan_fwd(zr, zi, a_ref, carry_ref, t, off):
    ar = [a_ref[q] for q in range(S5_ZQ)]
    ai = [a_ref[S5_ZQ + q] for q in range(S5_ZQ)]

    def step(it, carry):
        carry = list(carry)
        base = pl.multiple_of(it * (8 * SCAN_UNROLL), 8 * SCAN_UNROLL) + off
        for tt in range(SCAN_UNROLL):
            rows = pl.ds(base + 8 * tt, 8)
            for q in range(S5_ZQ):
                c_r, c_i = carry[q], carry[S5_ZQ + q]
                n_r = ar[q] * c_r - ai[q] * c_i + zr[q, rows, :]
                n_i = ar[q] * c_i + ai[q] * c_r + zi[q, rows, :]
                zr[q, rows, :] = n_r
                zi[q, rows, :] = n_i
                carry[q], carry[S5_ZQ + q] = n_r, n_i
        return tuple(carry)

    out = lax.fori_loop(0, t // SCAN_UNROLL, step, tuple(carry_ref[k] for k in range(2 * S5_ZQ)))
    for k in range(2 * S5_ZQ):
        carry_ref[k] = out[k]


def _z_scan_bwd(lr, li, xr, xi, a_ref, carry_ref, acc_ref, t):
    ar = [a_ref[q] for q in range(S5_ZQ)]
    ai = [a_ref[S5_ZQ + q] for q in range(S5_ZQ)]
    n_it = t // SCAN_UNROLL

    def step(it, state):
        carry, acc = list(state[0]), list(state[1])
        base = pl.multiple_of((n_it - 1 - it) * (8 * SCAN_UNROLL), 8 * SCAN_UNROLL)
        for tt in reversed(range(SCAN_UNROLL)):
            rows = pl.ds(base + 8 * tt, 8)
            for q in range(S5_ZQ):
                c_r, c_i = carry[q], carry[S5_ZQ + q]
                n_r = ar[q] * c_r + ai[q] * c_i + lr[q, rows, :]
                n_i = ar[q] * c_i - ai[q] * c_r + li[q, rows, :]
                lr[q, rows, :] = n_r
                li[q, rows, :] = n_i
                p_r, p_i = xr[q, rows, :], xi[q, rows, :]
                acc[q] = acc[q] + n_r * p_r + n_i * p_i
                acc[S5_ZQ + q] = acc[S5_ZQ + q] + n_i * p_r - n_r * p_i
                carry[q], carry[S5_ZQ + q] = n_r, n_i
        return tuple(carry), tuple(acc)

    k8 = range(2 * S5_ZQ)
    carry, acc = lax.fori_loop(0, n_it, step, (tuple(carry_ref[k] for k in k8), tuple(acc_ref[k] for k in k8)))
    for k in k8:
        carry_ref[k] = carry[k]
        acc_ref[k] = acc[k]


def _s5_fwd(proj, mix, bbm, ccm, d_row, glu_w, glu_b, tabs, t, side):
    l = proj.shape[0]
    nt = l // t
    n_in = 9

    def body(*refs):
        u_ref, gs_ref, bb_ref, cc_ref, d_ref, gw_ref, gb_ref, a_ref, _ = refs[:n_in]
        side_in = refs[n_in:n_in + len(side.srcs)]
        ssm_ref, xst_ref = refs[n_in + len(side.srcs):n_in + len(side.srcs) + 2]
        side_out = refs[n_in + len(side.srcs) + 2:n_in + len(side.srcs) + 2 + side.n]
        zr, zi, carry = refs[n_in + len(side.srcs) + 2 + side.n:n_in + len(side.srcs) + 5 + side.n]
        sems = refs[n_in + len(side.srcs) + 5 + side.n:]

        @pl.when(pl.program_id(0) == 0)
        def _():
            side.start(side_in, side_out, sems)
            carry[...] = jnp.zeros_like(carry)

        xst_ref[0] = carry[...]
        ub = u_ref[...]
        u = ub.astype(F32)
        for blk in range(S5_NB):
            _z_store(zr, zi, blk, _dot(ub[:, blk * 128:(blk + 1) * 128], bb_ref[blk]), t, 0)
        _z_scan_fwd(zr, zi, a_ref, carry, t, 0)
        ys = jnp.concatenate(
            [_dot(_z_load(zr, zi, blk, t, 0).astype(BF16), cc_ref[blk]) for blk in range(S5_NB)], axis=1)
        y2 = _gelu(ys + d_ref[...] * u)
        z = _dot(y2.astype(BF16), gw_ref[...]) + gb_ref[...]
        ssm_ref[...] = (y2 * _sigmoid(z) * _silu(gs_ref[...].astype(F32))).astype(ssm_ref.dtype)

        @pl.when(pl.program_id(0) == nt - 1)
        def _():
            side.wait(side_in, side_out, sems)

    const2 = lambda shape: pl.BlockSpec(shape, lambda i: (0,) * len(shape))
    zshape = (2 * S5_ZQ, 8, 128)
    return pl.pallas_call(
        body, name="s5_fwd", grid=(nt,),
        in_specs=[pl.BlockSpec((t, D_MODEL), lambda i: (i, 3)), pl.BlockSpec((t, D_MODEL), lambda i: (i, 4)),
                  const2(bbm.shape), const2(ccm.shape), const2((1, D_MODEL)), const2((D_MODEL, D_MODEL)),
                  const2((1, D_MODEL)), const2(zshape), pl.BlockSpec(memory_space=pl.ANY)] + side.in_specs,
        out_specs=(pl.BlockSpec((t, D_MODEL), lambda i: (i, 1)), pl.BlockSpec((1,) + zshape, lambda i: (i, 0, 0, 0)),
                   *side.out_specs),
        out_shape=(jax.ShapeDtypeStruct((l, 2 * D_MODEL), BF16), jax.ShapeDtypeStruct((nt,) + zshape, F32),
                   *side.landing),
        scratch_shapes=[pltpu.VMEM((S5_ZQ, 8 * t, 128), F32), pltpu.VMEM((S5_ZQ, 8 * t, 128), F32),
                        pltpu.VMEM(zshape, F32)] + side.scratch,
        input_output_aliases={8: 0},
        compiler_params=_cp("arbitrary"),
    )(proj, proj, bbm, ccm, d_row, glu_w, glu_b, tabs, mix, *side.srcs)


def _s5_bwd(proj, dmix, dproj, xstart, bbm, ccm, d_row, glu_w, glu_b, tabs, t):
    l = proj.shape[0]
    nt = l // t
    col0 = 2 * RET_QK + 2 * D_MODEL

    def body(u_ref, gs_ref, dm_ref, xst_ref, bb_ref, cc_ref, d_ref, gw_ref, gb_ref, a_ref, _,
             dp_ref, y2_ref, dz_ref, dbb_ref, dcc_ref, da_ref, dd_ref, dgb_ref, xr, xi, lr, li, carry, lcarry,
             dug_s, dug_sem):
        step = pl.program_id(0)
        slot = step % 2
        dug_ref = dug_s.at[slot]

        def put(s, at_step):
            rows = pl.ds(pl.multiple_of((nt - 1 - at_step) * t, t), t)
            return pltpu.make_async_copy(dug_s.at[s], dp_ref.at[rows, pl.ds(col0, 2 * D_MODEL)], dug_sem.at[s])

        @pl.when(step >= 2)
        def _():
            put(slot, step - 2).wait()

        @pl.when(step == 0)
        def _():
            lcarry[...] = jnp.zeros_like(lcarry)
            dbb_ref[...] = jnp.zeros_like(dbb_ref)
            dcc_ref[...] = jnp.zeros_like(dcc_ref)
            da_ref[...] = jnp.zeros_like(da_ref)
            dd_ref[...] = jnp.zeros_like(dd_ref)
            dgb_ref[...] = jnp.zeros_like(dgb_ref)

        carry[...] = xst_ref[0]
        for q in range(S5_ZQ):
            xr[q, 0:8, :] = carry[q]
            xi[q, 0:8, :] = carry[S5_ZQ + q]
        ub = u_ref[...]
        u = ub.astype(F32)
        for blk in range(S5_NB):
            _z_store(xr, xi, blk, _dot(ub[:, blk * 128:(blk + 1) * 128], bb_ref[blk]), t, 8)
        _z_scan_fwd(xr, xi, a_ref, carry, t, 8)
        ys = jnp.concatenate(
            [_dot(_z_load(xr, xi, blk, t, 8).astype(BF16), cc_ref[blk]) for blk in range(S5_NB)], axis=1)
        dv = d_ref[...]
        y1 = ys + dv * u
        y2, dgelu = _gelu_and_grad(y1)
        y2b = y2.astype(BF16)
        sg = _sigmoid(_dot(y2b, gw_ref[...]) + gb_ref[...])
        gs = gs_ref[...].astype(F32)
        dssm = dm_ref[...].astype(F32)
        dug_ref[:, D_MODEL:] = (dssm * (y2 * sg) * _dsilu(gs)).astype(dug_ref.dtype)
        dy3 = dssm * _silu(gs)
        dz = dy3 * y2 * sg * (1.0 - sg)
        dzb = dz.astype(BF16)
        y2_ref[...] = y2b
        dz_ref[...] = dzb
        dgb_ref[...] += jnp.sum(dz, axis=0, keepdims=True)
        dy1 = (dy3 * sg + _dot_nt(dzb, gw_ref[...])) * dgelu
        dd_ref[...] += jnp.sum(dy1 * u, axis=0, keepdims=True)
        dyb = dy1.astype(BF16)
        for blk in range(S5_NB):
            ch = slice(blk * 128, (blk + 1) * 128)
            _z_store(lr, li, blk, _dot_nt(dyb[:, ch], cc_ref[blk]), t, 0)
            dcc_ref[blk] += _dot_tn(_z_load(xr, xi, blk, t, 8).astype(BF16), dyb[:, ch])
        _z_scan_bwd(lr, li, xr, xi, a_ref, lcarry, da_ref, t)
        du = []
        for blk in range(S5_NB):
            lb = _z_load(lr, li, blk, t, 0).astype(BF16)
            du.append(_dot_nt(lb, bb_ref[blk]))
            dbb_ref[blk] += _dot_tn(ub[:, blk * 128:(blk + 1) * 128], lb)
        dug_ref[:, :D_MODEL] = (jnp.concatenate(du, axis=1) + dy1 * dv).astype(dug_ref.dtype)
        put(slot, step).start()

        @pl.when(step == nt - 1)
        def _():
            put(slot, step).wait()
            if nt > 1:
                put(1 - slot, step - 1).wait()

    rev = lambda i: nt - 1 - i
    const2 = lambda shape: pl.BlockSpec(shape, lambda i: (0,) * len(shape))
    row_out = lambda w: pl.BlockSpec((t, w), lambda i: (rev(i), 0))
    zshape = (2 * S5_ZQ, 8, 128)
    hbm = pl.BlockSpec(memory_space=pl.ANY)
    return pl.pallas_call(
        body, name="s5_bwd", grid=(nt,),
        in_specs=[pl.BlockSpec((t, D_MODEL), lambda i: (rev(i), 3)), pl.BlockSpec((t, D_MODEL), lambda i: (rev(i), 4)),
                  pl.BlockSpec((t, D_MODEL), lambda i: (rev(i), 1)),
                  pl.BlockSpec((1,) + zshape, lambda i: (rev(i), 0, 0, 0)),
                  const2(bbm.shape), const2(ccm.shape), const2((1, D_MODEL)), const2((D_MODEL, D_MODEL)),
                  const2((1, D_MODEL)), const2(zshape), hbm],
        out_specs=(hbm, row_out(D_MODEL), row_out(D_MODEL), const2(bbm.shape), const2(ccm.shape),
                   const2(zshape), const2((1, D_MODEL)), const2((1, D_MODEL))),
        out_shape=(jax.ShapeDtypeStruct(dproj.shape, BF16), jax.ShapeDtypeStruct((l, D_MODEL), BF16),
                   jax.ShapeDtypeStruct((l, D_MODEL), BF16), jax.ShapeDtypeStruct(bbm.shape, F32),
                   jax.ShapeDtypeStruct(ccm.shape, F32), jax.ShapeDtypeStruct(zshape, F32),
                   jax.ShapeDtypeStruct((1, D_MODEL), F32), jax.ShapeDtypeStruct((1, D_MODEL), F32)),
        scratch_shapes=[pltpu.VMEM((S5_ZQ, 8 * t + 8, 128), F32), pltpu.VMEM((S5_ZQ, 8 * t + 8, 128), F32),
                        pltpu.VMEM((S5_ZQ, 8 * t, 128), F32), pltpu.VMEM((S5_ZQ, 8 * t, 128), F32),
                        pltpu.VMEM(zshape, F32), pltpu.VMEM(zshape, F32),
                        pltpu.VMEM((2, t, 2 * D_MODEL), BF16), pltpu.SemaphoreType.DMA((2,))],
        input_output_aliases={10: 0},
        compiler_params=_cp("arbitrary"),
    )(proj, proj, dmix, xstart, bbm, ccm, d_row, glu_w, glu_b, tabs, dproj)


def _attn_probs(qh, kh):
    s = _dot_nt(qh, kh) * (XA_DH ** -0.5)
    e = jnp.exp(s - jnp.max(s, axis=-1, keepdims=True))
    return e / jnp.sum(e, axis=-1, keepdims=True)


def _attn_fwd(qa, ka, va):
    l = qa.shape[0]
    m = ka.shape[0]
    tl = _pick(l, (512, 256))

    def body(q_ref, k_ref, v_ref, o_ref):
        for h in range(XA_HEADS):
            hs = slice(h * XA_DH, (h + 1) * XA_DH)
            p = _attn_probs(q_ref[:, hs], k_ref[:, hs])
            o_ref[:, hs] = _dot(p.astype(BF16), v_ref[:, hs]).astype(o_ref.dtype)

    return pl.pallas_call(
        body, name="xattn_fwd", grid=(l // tl,),
        in_specs=[pl.BlockSpec((tl, D_MODEL), lambda i: (i, 0)), pl.BlockSpec((m, D_MODEL), lambda i: (0, 0)),
                  pl.BlockSpec((m, D_MODEL), lambda i: (0, 0))],
        out_specs=pl.BlockSpec((tl, D_MODEL), lambda i: (i, 0)),
        out_shape=jax.ShapeDtypeStruct((l, D_MODEL), BF16), compiler_params=_cp("parallel"),
    )(qa, ka, va)


def _attn_bwd(qa, ka, va, doa):
    l = qa.shape[0]
    m = ka.shape[0]
    tl = _pick(l, (512, 256))

    def body(q_ref, k_ref, v_ref, do_ref, dq_ref, dk_ref, dv_ref):
        @pl.when(pl.program_id(0) == 0)
        def _():
            dk_ref[...] = jnp.zeros_like(dk_ref)
            dv_ref[...] = jnp.zeros_like(dv_ref)

        for h in range(XA_HEADS):
            hs = slice(h * XA_DH, (h + 1) * XA_DH)
            qh, kh, vh, doh = q_ref[:, hs], k_ref[:, hs], v_ref[:, hs], do_ref[:, hs]
            p = _attn_probs(qh, kh)
            dv_ref[:, hs] += _dot_tn(p.astype(BF16), doh)
            dp = _dot_nt(doh, vh)
            ds = (p * (dp - jnp.sum(dp * p, axis=-1, keepdims=True)) * (XA_DH ** -0.5)).astype(BF16)
            dq_ref[:, hs] = _dot(ds, kh).astype(dq_ref.dtype)
            dk_ref[:, hs] += _dot_tn(ds, qh)

    row = pl.BlockSpec((tl, D_MODEL), lambda i: (i, 0))
    mem = pl.BlockSpec((m, D_MODEL), lambda i: (0, 0))
    return pl.pallas_call(
        body, name="xattn_bwd", grid=(l // tl,), in_specs=[row, mem, mem, row], out_specs=(row, mem, mem),
        out_shape=(jax.ShapeDtypeStruct((l, D_MODEL), BF16), jax.ShapeDtypeStruct((m, D_MODEL), F32),
                   jax.ShapeDtypeStruct((m, D_MODEL), F32)),
        compiler_params=_cp("arbitrary"),
    )(qa, ka, va, doa)


def _me_and_peers():
    x, y, c = lax.axis_index("x"), lax.axis_index("y"), lax.axis_index("c")
    flip = lambda v, bit: (1 - v) if bit else v
    peers = []
    for k in range(1, N_DEV):
        px, py, pc = flip(x, (k >> 2) & 1), flip(y, (k >> 1) & 1), flip(c, k & 1)
        peers.append(((px, py, pc), 4 * px + 2 * py + pc))
    return 4 * x + 2 * y + c, peers


class _SideJob:
    def __init__(self, srcs, landing, src_of, dst_of):
        self.srcs = list(srcs)
        self.landing = list(landing)
        self.n = len(self.landing)
        self.src_of, self.dst_of = src_of, dst_of
        hbm = pl.BlockSpec(memory_space=pl.ANY)
        self.in_specs = [hbm] * len(self.srcs)
        self.out_specs = [hbm] * self.n
        self.scratch = [pltpu.SemaphoreType.DMA((self.n * (N_DEV - 1),)), pltpu.SemaphoreType.DMA((self.n * (N_DEV - 1),)),
                        pltpu.SemaphoreType.DMA((self.n,))]

    def _copies(self, src_refs, out_refs, sems):
        send_sems, recv_sems, loc_sems = sems
        me, peers = _me_and_peers()
        local = [pltpu.make_async_copy(self.src_of(a, me, src_refs), self.dst_of(a, me, out_refs), loc_sems.at[a])
                 for a in range(self.n)]
        sends, recvs = [], []
        for k, (peer, peer_idx) in enumerate(peers):
            for a in range(self.n):
                s = self.n * k + a
                sends.append(pltpu.make_async_remote_copy(
                    src_ref=self.src_of(a, peer_idx, src_refs), dst_ref=self.dst_of(a, me, out_refs),
                    send_sem=send_sems.at[s], recv_sem=recv_sems.at[s], device_id=peer, device_id_type=MESH))
                recvs.append(pltpu.make_async_remote_copy(
                    src_ref=self.src_of(a, me, src_refs), dst_ref=self.dst_of(a, peer_idx, out_refs),
                    send_sem=send_sems.at[s], recv_sem=recv_sems.at[s], device_id=peer, device_id_type=MESH))
        return local, sends, recvs

    def start(self, src_refs, out_refs, sems):
        if not self.n:
            return
        local, sends, _ = self._copies(src_refs, out_refs, sems)
        for cp in local + sends:
            cp.start()

    def wait(self, src_refs, out_refs, sems):
        if not self.n:
            return
        local, sends, recvs = self._copies(src_refs, out_refs, sems)
        for cp in recvs:
            cp.wait_recv()
        for cp in sends:
            cp.wait_send()
        for cp in local:
            cp.wait()


def _gather_job(shards):
    return _SideJob(shards, [jax.ShapeDtypeStruct((N_DEV,) + s.shape, s.dtype) for s in shards],
                    src_of=lambda a, j, srcs: srcs[a], dst_of=lambda a, j, outs: outs[a].at[j])


def _scatter_job(grads):
    landing, parts = [], []
    for g in grads:
        if g.ndim == 3:
            landing.append(jax.ShapeDtypeStruct(g.shape, g.dtype))
            parts.append(None)
        else:
            r = g.shape[0] // N_DEV
            landing.append(jax.ShapeDtypeStruct((N_DEV, r, g.shape[1]), g.dtype))
            parts.append(r)

    def src_of(a, j, srcs):
        if parts[a] is None:
            return srcs[a].at[j]
        return srcs[a].at[pl.ds(pl.multiple_of(j * parts[a], 8), parts[a]), :]

    return _SideJob(grads, landing, src_of=src_of, dst_of=lambda a, j, outs: outs[a].at[j])


def _prologue(w_in_shard, row_shards, x, g, pos_col, inv_row):
    n_row = len(row_shards)
    l, d = x.shape
    tr = _pick(l, (1024, 512, 256))
    nt = l // tr
    mid = nt // 2

    def body(*refs):
        win_ref = refs[0]
        row_refs = refs[1:1 + n_row]
        x_ref, g_ref, p_ref, inv_ref = refs[1 + n_row:5 + n_row]
        out_win = refs[5 + n_row]
        row_outs = refs[6 + n_row:6 + 2 * n_row]
        h_ref, cos_ref, sin_ref = refs[6 + 2 * n_row:9 + 2 * n_row]
        win_b, send_sems, recv_sems, local_sem = refs[9 + 2 * n_row:]
        step = pl.program_id(0)
        cx, cy, cc = lax.axis_index("x"), lax.axis_index("y"), lax.axis_index("c")
        me, sibling = (cx, cy, cc), (cx, cy, 1 - cc)
        chips = [(1 - cx, cy), (cx, 1 - cy), (1 - cx, 1 - cy)]
        slot = lambda p: out_win.at[4 * p[0] + 2 * p[1] + p[2]]

        def copy(k, block, to, src=None):
            return pltpu.make_async_remote_copy(
                src_ref=slot(block) if src is None else src, dst_ref=slot(block), send_sem=send_sems.at[k],
                recv_sem=recv_sems.at[k], device_id=to, device_id_type=MESH)

        mine = pltpu.make_async_copy(win_b, slot(me), local_sem)
        first = [copy(0, me, sibling, src=win_b)]
        first += [copy(1 + j, me, (*chip, cc), src=win_b) for j, chip in enumerate(chips)]
        passed = [copy(4 + j, (*chip, cc), sibling) for j, chip in enumerate(chips)]

        @pl.when(step == 0)
        def _():
            win_b[...] = win_ref[...].astype(BF16)
            mine.start()
            for cp in first:
                cp.start()
            for r, o in zip(row_refs, row_outs):
                o[...] = r[...].astype(BF16)

        h_ref[...] = (_rms(x_ref[...])[1] * g_ref[...]).astype(h_ref.dtype)
        ang = p_ref[...].astype(F32) * inv_ref[...]
        lane = lax.broadcasted_iota(jnp.int32, ang.shape, 1)
        cos_ref[...] = jnp.tile(jnp.cos(ang), (1, RET_QK // 128))
        sin_ref[...] = jnp.tile(jnp.where((lane % RET_DK) < RET_DK // 2, -jnp.sin(ang), jnp.sin(ang)),
                                (1, RET_QK // 128))

        @pl.when(step == mid)
        def _():
            for j, chip in enumerate(chips):
                copy(1 + j, (*chip, cc), me).wait_recv()
                passed[j].start()

        @pl.when(step == nt - 1)
        def _():
            copy(0, sibling, me).wait_recv()
            for j, chip in enumerate(chips):
                copy(4 + j, (*chip, 1 - cc), me).wait_recv()
            for cp in first + passed:
                cp.wait_send()
            mine.wait()

    whole = lambda a: pl.BlockSpec(a.shape, lambda i: (0,) * a.ndim)
    rows = lambda w: pl.BlockSpec((tr, w), lambda i: (i, 0))
    return pl.pallas_call(
        body, name="prologue_allgather_w_in", grid=(nt,),
        in_specs=[whole(w_in_shard)] + [whole(r) for r in row_shards] + [rows(d), whole(g), rows(1), whole(inv_row)],
        out_specs=(pl.BlockSpec(memory_space=pl.ANY), *[whole(r) for r in row_shards], rows(d), rows(RET_QK),
                   rows(RET_QK)),
        out_shape=(jax.ShapeDtypeStruct((N_DEV,) + w_in_shard.shape, BF16),
                   *[jax.ShapeDtypeStruct(r.shape, BF16) for r in row_shards],
                   jax.ShapeDtypeStruct((l, d), BF16), jax.ShapeDtypeStruct((l, RET_QK), F32),
                   jax.ShapeDtypeStruct((l, RET_QK), F32)),
        scratch_shapes=[pltpu.VMEM(w_in_shard.shape, BF16), pltpu.SemaphoreType.DMA((N_DEV - 1,)),
                        pltpu.SemaphoreType.DMA((N_DEV - 1,)), pltpu.SemaphoreType.DMA],
        compiler_params=_cp("arbitrary"),
    )(w_in_shard, *row_shards, x, g, pos_col, inv_row)


def _allreduce_small(small):
    rows = SMALL_ROWS // N_DEV

    def body(x_ref, out_ref, land, send1, recv1, send2, recv2):
        me, peers = _me_and_peers()
        block = lambda j: pl.ds(pl.multiple_of(j * rows, 8), rows)

        def phase(src_of, dst_of, send_sems, recv_sems):
            sends = [pltpu.make_async_remote_copy(src_ref=src_of(pidx), dst_ref=dst_of(me), send_sem=send_sems.at[k],
                                                  recv_sem=recv_sems.at[k], device_id=peer, device_id_type=MESH)
                     for k, (peer, pidx) in enumerate(peers)]
            recvs = [pltpu.make_async_remote_copy(src_ref=src_of(me), dst_ref=dst_of(pidx), send_sem=send_sems.at[k],
                                                  recv_sem=recv_sems.at[k], device_id=peer, device_id_type=MESH)
                     for k, (peer, pidx) in enumerate(peers)]
            for cp in sends:
                cp.start()
            for cp in recvs:
                cp.wait_recv()
            for cp in sends:
                cp.wait_send()

        land[me] = x_ref[block(me), :]
        phase(lambda j: x_ref.at[block(j), :], lambda j: land.at[j], send1, recv1)
        total = land[0]
        for j in range(1, N_DEV):
            total = total + land[j]
        out_ref[block(me), :] = total
        phase(lambda j: out_ref.at[block(me), :], lambda j: out_ref.at[block(j), :], send2, recv2)

    vm = pl.BlockSpec(memory_space=pltpu.VMEM)
    return pl.pallas_call(
        body, name="allreduce_small", in_specs=[vm], out_specs=vm, out_shape=jax.ShapeDtypeStruct(small.shape, F32),
        scratch_shapes=[pltpu.VMEM((N_DEV, rows, D_MODEL), F32)] + [pltpu.SemaphoreType.DMA((N_DEV - 1,))] * 4,
    )(small)


def _adamw(name, got, w, m, v):
    r, c = w.shape
    n_slots = got.shape[0]
    tr = _pick(r, (256, 128, 64))

    def body(got_ref, w_ref, m_ref, v_ref, g_ref, d_ref, nm_ref, nv_ref):
        g = got_ref[0].astype(F32)
        for j in range(1, n_slots):
            g = g + got_ref[j].astype(F32)
        nm = ADAM_B1 * m_ref[...] + (1.0 - ADAM_B1) * g
        nv = ADAM_B2 * v_ref[...] + (1.0 - ADAM_B2) * jnp.square(g)
        m_hat = nm / (1.0 - ADAM_B1 ** ADAM_STEP)
        v_hat = nv / (1.0 - ADAM_B2 ** ADAM_STEP)
        g_ref[...] = g
        d_ref[...] = -ADAM_LR * (m_hat / (jnp.sqrt(v_hat) + ADAM_EPS) + ADAM_WD * w_ref[...])
        nm_ref[...] = nm
        nv_ref[...] = nv

    blk = pl.BlockSpec((tr, c), lambda i: (i, 0))
    out = jax.ShapeDtypeStruct((r, c), F32)
    return pl.pallas_call(
        body, name=name, grid=(r // tr,),
        in_specs=[pl.BlockSpec((n_slots, tr, c), lambda i: (0, i, 0)), blk, blk, blk],
        out_specs=(blk, blk, blk, blk), out_shape=(out, out, out, out), compiler_params=_cp("parallel"),
    )(got, w, m, v)


_SMALL_VECS = ("norm1_g", "ret_gn_g", "s5_d", "s5_glu_b", "norm2_g", "norm_mem_g", "norm_f_g")


def _small_layout():
    lay, row = {}, 0
    for n in _SMALL_VECS + ("loss",):
        lay[n] = (row, 1, D_MODEL)
        row += 1
    for n in ("s5_a_re", "s5_a_im"):
        lay[n] = (row, 4, D_MODEL)
        row += 4
    lay["s5_log_dt"] = (row, 1, S5_G)
    row += 8
    for n in ("s5_b_re", "s5_b_im", "s5_c_re", "s5_c_im"):
        lay[n] = (row, 64, D_MODEL)
        row += 64
    assert row <= SMALL_ROWS
    return lay


def _pack_small(t, loss_row=None):
    lay = _small_layout()
    pieces = [t[n].reshape(1, D_MODEL) for n in _SMALL_VECS]
    pieces.append(jnp.zeros((1, D_MODEL), F32) if loss_row is None else loss_row)
    pieces += [t["s5_a_re"].reshape(4, D_MODEL), t["s5_a_im"].reshape(4, D_MODEL)]
    pieces.append(jnp.pad(t["s5_log_dt"].reshape(1, S5_G), ((0, 7), (0, D_MODEL - S5_G))))
    pieces += [t[n].reshape(64, D_MODEL) for n in ("s5_b_re", "s5_b_im", "s5_c_re", "s5_c_im")]
    pieces.append(jnp.zeros((SMALL_ROWS - lay["s5_c_im"][0] - 64, D_MODEL), F32))
    return jnp.concatenate(pieces, axis=0)


def _adamw_small(g_sum, w, m, v):
    lay = _small_layout()
    names = [n for n in lay if n != "loss"]

    def body(g_ref, w_ref, m_ref, v_ref, *outs):
        g = g_ref[...]
        nm = ADAM_B1 * m_ref[...] + (1.0 - ADAM_B1) * g
        nv = ADAM_B2 * v_ref[...] + (1.0 - ADAM_B2) * jnp.square(g)
        m_hat = nm / (1.0 - ADAM_B1 ** ADAM_STEP)
        v_hat = nv / (1.0 - ADAM_B2 ** ADAM_STEP)
        delta = -ADAM_LR * (m_hat / (jnp.sqrt(v_hat) + ADAM_EPS) + ADAM_WD * w_ref[...])
        for i, n in enumerate(names):
            r0, rows, lanes = lay[n]
            for part, val in enumerate((g, delta, nm, nv)):
                outs[4 * i + part][...] = val[r0:r0 + rows, 0:lanes]
        r0 = lay["loss"][0]
        outs[-1][...] = g[r0:r0 + 1, :]

    shapes = []
    for n in names:
        shapes += [jax.ShapeDtypeStruct(lay[n][1:], F32)] * 4
    shapes.append(jax.ShapeDtypeStruct((1, D_MODEL), F32))
    outs = pl.pallas_call(body, name="adamw_small", out_shape=tuple(shapes),
                          compiler_params=pltpu.CompilerParams(vmem_limit_bytes=VMEM_LIMIT))(g_sum, w, m, v)
    return {n: tuple(outs[4 * i:4 * i + 4]) for i, n in enumerate(names)}, outs[-1]


_W_NAMES = ("norm1_g", "w_in", "ret_gn_g", "s5_a_re", "s5_a_im", "s5_log_dt", "s5_b_re", "s5_b_im", "s5_c_re", "s5_c_im",
            "s5_d", "s5_glu_w", "s5_glu_b", "w_out", "norm2_g", "norm_mem_g", "xa_wq", "xa_wk", "xa_wv", "xa_wo",
            "norm_f_g")
_ROW_NAMES = ("s5_glu_w", "w_out", "xa_wq", "xa_wk", "xa_wv", "xa_wo")


def kernel(x, mem, positions, norm1_g, w_in, ret_gn_g, s5_a_re, s5_a_im, s5_log_dt, s5_b_re, s5_b_im, s5_c_re, s5_c_im, s5_d, s5_glu_w, s5_glu_b, w_out, norm2_g, norm_mem_g, xa_wq, xa_wk, xa_wv, xa_wo, norm_f_g, loss_target, m_norm1_g, m_w_in, m_ret_gn_g, m_s5_a_re, m_s5_a_im, m_s5_log_dt, m_s5_b_re, m_s5_b_im, m_s5_c_re, m_s5_c_im, m_s5_d, m_s5_glu_w, m_s5_glu_b, m_w_out, m_norm2_g, m_norm_mem_g, m_xa_wq, m_xa_wk, m_xa_wv, m_xa_wo, m_norm_f_g, v_norm1_g, v_w_in, v_ret_gn_g, v_s5_a_re, v_s5_a_im, v_s5_log_dt, v_s5_b_re, v_s5_b_im, v_s5_c_re, v_s5_c_im, v_s5_d, v_s5_glu_w, v_s5_glu_b, v_w_out, v_norm2_g, v_norm_mem_g, v_xa_wq, v_xa_wk, v_xa_wv, v_xa_wo, v_norm_f_g):
    w = dict(norm1_g=norm1_g, w_in=w_in, ret_gn_g=ret_gn_g, s5_a_re=s5_a_re, s5_a_im=s5_a_im, s5_log_dt=s5_log_dt,
             s5_b_re=s5_b_re, s5_b_im=s5_b_im, s5_c_re=s5_c_re, s5_c_im=s5_c_im, s5_d=s5_d, s5_glu_w=s5_glu_w,
             s5_glu_b=s5_glu_b, w_out=w_out, norm2_g=norm2_g, norm_mem_g=norm_mem_g, xa_wq=xa_wq, xa_wk=xa_wk,
             xa_wv=xa_wv, xa_wo=xa_wo, norm_f_g=norm_f_g)
    mom = dict(norm1_g=m_norm1_g, w_in=m_w_in, ret_gn_g=m_ret_gn_g, s5_a_re=m_s5_a_re, s5_a_im=m_s5_a_im,
               s5_log_dt=m_s5_log_dt, s5_b_re=m_s5_b_re, s5_b_im=m_s5_b_im, s5_c_re=m_s5_c_re, s5_c_im=m_s5_c_im,
               s5_d=m_s5_d, s5_glu_w=m_s5_glu_w, s5_glu_b=m_s5_glu_b, w_out=m_w_out, norm2_g=m_norm2_g,
               norm_mem_g=m_norm_mem_g, xa_wq=m_xa_wq, xa_wk=m_xa_wk, xa_wv=m_xa_wv, xa_wo=m_xa_wo,
               norm_f_g=m_norm_f_g)
    var = dict(norm1_g=v_norm1_g, w_in=v_w_in, ret_gn_g=v_ret_gn_g, s5_a_re=v_s5_a_re, s5_a_im=v_s5_a_im,
               s5_log_dt=v_s5_log_dt, s5_b_re=v_s5_b_re, s5_b_im=v_s5_b_im, s5_c_re=v_s5_c_re, s5_c_im=v_s5_c_im,
               s5_d=v_s5_d, s5_glu_w=v_s5_glu_w, s5_glu_b=v_s5_glu_b, w_out=v_w_out, norm2_g=v_norm2_g,
               norm_mem_g=v_norm_mem_g, xa_wq=v_xa_wq, xa_wk=v_xa_wk, xa_wv=v_xa_wv, xa_wo=v_xa_wo,
               norm_f_g=v_norm_f_g)
    shapes = {n: w[n].shape for n in _W_NAMES}

    x2d, mem2d, tgt = x[0], mem[0], loss_target[0]
    l = x2d.shape[0]
    ret_c = _pick(l, (256, 128))
    s5_t = _pick(l, (256, 128))
    g1, g2, gm, gf = norm1_g, norm2_g, norm_mem_g, norm_f_g.reshape(1, D_MODEL)

    half = RET_DK // 2
    inv = ROPE_BASE ** (-jnp.arange(half, dtype=F32) / half)
    win_s, *rest = _prologue(w_in[0], [w[n][0] for n in _ROW_NAMES], x2d, g1, positions[0].reshape(l, 1),
                             jnp.tile(inv, 128 // half)[None, :])
    row_shards_b, (h1, cos_t, sin_t) = rest[:len(_ROW_NAMES)], rest[len(_ROW_NAMES):]

    to_gpn = lambda b: jnp.transpose(b, (0, 2, 1)).reshape(S5_G, S5_P * S5_N)
    from_gpn = lambda b: jnp.transpose(b.reshape(S5_G, S5_P, S5_N), (0, 2, 1))
    disc_args = (s5_a_re[0], s5_a_im[0], s5_log_dt[0].reshape(S5_G, 1), to_gpn(s5_b_re[0]), to_gpn(s5_b_im[0]))
    abar_re, abar_im, bb_re_t, bb_im_t = _s5_discretize(*disc_args)
    bbm, ccm = _s5_block_mats(from_gpn(bb_re_t), from_gpn(bb_im_t), s5_c_re[0], s5_c_im[0])
    a_z = _s5_z(abar_re, abar_im)

    proj, *rows_01 = _mm_nn_slots("in_proj", h1, win_s, BF16, side=_gather_job(row_shards_b[:2]))
    full = {n: g.reshape(N_DEV * r, D_MODEL) for n, g, r in zip(_ROW_NAMES[:2], rows_01, ROW_SHARDS[:2])}
    rconsts = _ret_constants(ret_c)
    ret, o_saved, r_prev, q_rot, k_rot = _ret_fwd(proj, cos_t, sin_t, rconsts, ret_gn_g, ret_c)
    mix, xstart, *rows_xa = _s5_fwd(proj, ret, bbm, ccm, s5_d, full["s5_glu_w"], s5_glu_b, a_z, s5_t,
                                    side=_gather_job(row_shards_b[2:]))
    full.update({n: g.reshape(N_DEV * r, D_MODEL) for n, g, r in zip(_ROW_NAMES[2:], rows_xa, ROW_SHARDS[2:])})
    x1, h2 = _mm_nn("out_proj", mix, full["w_out"], F32, residual=x2d, epi=_epi_norm_fwd(g2))
    mn = _rms_fwd("norm_mem_fwd", mem2d, gm)
    qa = _mm_nn("xa_q", h2, full["xa_wq"], BF16)
    ka = _mm_nn("xa_k", mn, full["xa_wk"], BF16)
    va = _mm_nn("xa_v", mn, full["xa_wv"], BF16)
    oa = _attn_fwd(qa, ka, va)
    dx2, dgf, loss_lanes = _mm_nn("xa_o", oa, full["xa_wo"], F32, residual=x1, epi=_epi_loss(gf, tgt))

    doa = _mm_nt("xa_o_dx", dx2, full["xa_wo"], BF16)
    dwo = _mm_tn("xa_o_dw", oa, dx2, BF16)
    dqa, dka, dva = _attn_bwd(qa, ka, va, doa)
    dx1, dg2 = _mm_nt("xa_q_dx", dqa, full["xa_wq"], F32, epi=_epi_norm_bwd(x1, g2, dx2))
    dwq = _mm_tn("xa_q_dw", h2, dqa, BF16)
    dwk = _mm_tn("xa_k_dw", mn, dka, BF16)
    dwv = _mm_tn("xa_v_dw", mn, dva, BF16)
    dmn = _mm_nt("xa_v_dx", dva, full["xa_wv"], F32, residual=_mm_nt("xa_k_dx", dka, full["xa_wk"], F32))
    _, dgm = _rms_bwd("norm_mem_bwd", mem2d, gm, dmn, None)
    dmix = _mm_nt("out_proj_dx", dx1, full["w_out"], BF16)
    dwout = _mm_tn("out_proj_dw", mix, dx1, BF16)
    dret, dgn, *got_a = _ret_bwd(proj, q_rot, k_rot, cos_t, sin_t, rconsts, ret_gn_g, o_saved, r_prev, dmix, ret_c,
                                 side=_scatter_job([dwout, dwq, dwk, dwv, dwo]))
    dproj, y2, dz, dbbm, dccm, dabar, dd, dgb = _s5_bwd(proj, dmix, dret, xstart, bbm, ccm, s5_d, full["s5_glu_w"],
                                                        s5_glu_b, a_z, s5_t)
    dglu = _mm_tn("s5_glu_dw", y2, dz, BF16)
    dwin_s, got_glu = _mm_tn_slots("in_proj_dw", h1, dproj, N_DEV, BF16, side=_scatter_job([dglu]))
    grad_x, dg1, got_win = _mm_nt_slots("in_proj_dx", dproj, win_s, F32, side=_scatter_job([dwin_s]),
                                        epi=_epi_norm_bwd(x2d, g1, dx1))

    dab_re, dab_im = _s5_unz(dabar)
    dbb_re, dbb_im = _s5_block_diag_bb(dbbm)
    dc_re, dc_im = _s5_block_diag_cc(dccm)
    da_re, da_im, dlog_dt, db_re_t, db_im_t = _s5_discretize_bwd(*disc_args, dab_re, dab_im, to_gpn(dbb_re),
                                                                 to_gpn(dbb_im))
    db_re, db_im = from_gpn(db_re_t), from_gpn(db_im_t)
    small_g = dict(norm1_g=dg1, ret_gn_g=dgn, s5_d=dd, s5_glu_b=dgb, norm2_g=dg2, norm_mem_g=dgm, norm_f_g=dgf,
                   s5_a_re=da_re, s5_a_im=da_im, s5_log_dt=dlog_dt, s5_b_re=db_re, s5_b_im=db_im, s5_c_re=dc_re,
                   s5_c_im=dc_im)
    small_pack = _pack_small(small_g, loss_row=loss_lanes)

    res = {}
    got = dict(zip(("w_out", "xa_wq", "xa_wk", "xa_wv", "xa_wo"), got_a), w_in=got_win, s5_glu_w=got_glu)
    for n in ("w_in",) + _ROW_NAMES:
        res[n] = _adamw("adamw_" + n, got[n], w[n][0], mom[n][0], var[n][0])
    small_sum = _allreduce_small(small_pack)
    small_res, loss_sum = _adamw_small(small_sum, _pack_small(w), _pack_small(mom), _pack_small(var))
    loss = (0.5 / D_MODEL) * jnp.sum(loss_sum)
    res.update(small_res)

    outs = [loss, grad_x[None]]
    for part in range(4):
        for n in _W_NAMES:
            outs.append(res[n][part].reshape(shapes[n]))
    return tuple(outs)
```

```python
import jax
import jax.numpy as jnp
from jax import lax
from jax.experimental import pallas as pl
from jax.experimental.pallas import tpu as pltpu

F32 = jnp.float32
BF16 = jnp.bfloat16
MESH = pl.DeviceIdType.MESH

D_MODEL = 1024
RET_HEADS, RET_DK, RET_DV = 8, 64, 128
RET_QK = RET_HEADS * RET_DK
S5_G, S5_N, S5_P = 64, 64, 16
S5_NB = 8
S5_GB = S5_G // S5_NB
S5_BS = S5_GB * S5_N
S5_COLS = 2 * S5_G * S5_N
XA_HEADS, XA_DH = 4, 256
EPS = 1e-6
ROPE_BASE = 10000.0
N_DEV = 8
W_IN_SHARD = 640
ROW_SHARDS = (128, 256, 128, 128, 128, 128)
ROWPACK = sum(ROW_SHARDS)
SMALL_ROWS = 320
ADAM_LR, ADAM_B1, ADAM_B2, ADAM_EPS, ADAM_WD, ADAM_STEP = 0.001, 0.9, 0.999, 1e-08, 0.01, 10

VMEM_LIMIT = 56 * 1024 * 1024


def _cp(*sem):
    return pltpu.CompilerParams(dimension_semantics=tuple(sem), vmem_limit_bytes=VMEM_LIMIT)


def _dot(a, b):
    return jnp.dot(a, b, preferred_element_type=F32)


def _dot_nt(a, b):
    return lax.dot_general(a, b, (((1,), (1,)), ((), ())), preferred_element_type=F32)


def _dot_tn(a, b):
    return lax.dot_general(a, b, (((0,), (0,)), ((), ())), preferred_element_type=F32)


def _sigmoid(x):
    return 1.0 / (1.0 + jnp.exp(-x))


def _silu(x):
    return x * _sigmoid(x)


def _dsilu(x):
    s = _sigmoid(x)
    return s * (1.0 + x * (1.0 - s))


_GELU_C = 0.7978845608028654


def _gelu(x):
    return 0.5 * x * (1.0 + jnp.tanh(_GELU_C * (x + 0.044715 * (x * x * x))))


def _gelu_and_grad(x):
    t = jnp.tanh(_GELU_C * (x + 0.044715 * (x * x * x)))
    half = 0.5 * (1.0 + t)
    return x * half, half + 0.5 * x * (1.0 - t * t) * (_GELU_C * (1.0 + 3.0 * 0.044715 * (x * x)))


def _pick(n, cands):
    for c in cands:
        if n % c == 0:
            return c
    return n


class _Epilogue:
    def __init__(self, rows, vecs, row_out_dtypes, n_sums, fn):
        self.rows, self.vecs, self.row_out_dtypes, self.n_sums, self.fn = list(rows), list(vecs), list(row_out_dtypes), n_sums, fn


def _rms(x):
    rs = lax.rsqrt(jnp.mean(x * x, axis=-1, keepdims=True) + EPS)
    return rs, x * rs


def _rms_dx(dn, xn, rs):
    return rs * (dn - xn * jnp.mean(dn * xn, axis=-1, keepdims=True))


def _epi_norm_fwd(g):
    def fn(r, rows, vecs):
        return r, [_rms(r)[1] * vecs[0]], []

    return _Epilogue([], [g], [BF16], 0, fn)


def _epi_loss(gf, target):
    def fn(r, rows, vecs):
        rs, xn = _rms(r)
        e = xn * vecs[0] - rows[0]
        dy = e * (1.0 / r.shape[-1])
        return (_rms_dx(dy * vecs[0], xn, rs), [],
                [jnp.sum(dy * xn, axis=0, keepdims=True), jnp.sum(e * e, axis=0, keepdims=True)])

    return _Epilogue([target], [gf], [], 2, fn)


def _epi_norm_bwd(x, g, dres):
    def fn(r, rows, vecs):
        rs, xn = _rms(rows[0])
        return _rms_dx(r * vecs[0], xn, rs) + rows[1], [], [jnp.sum(r * xn, axis=0, keepdims=True)]

    return _Epilogue([x, dres], [g], [], 1, fn)


def _mm_core(name, operands, in_specs, out_spec, out_shape, grid, nk, dims, acc_shape, has_res, side=None, epi=None):
    n_in = 3 if has_res else 2
    n_epi_in = len(epi.rows) + len(epi.vecs) if epi else 0
    n_epi_out = len(epi.row_out_dtypes) + epi.n_sums if epi else 0
    n_side_in = len(side.srcs) if side else 0
    n_side_out = side.n if side else 0

    def body(*refs):
        a_ref, b_ref = refs[0], refs[1]
        r_ref = refs[2] if has_res else None
        epi_in = refs[n_in:n_in + n_epi_in]
        side_in = refs[n_in + n_epi_in:n_in + n_epi_in + n_side_in]
        n0 = n_in + n_epi_in + n_side_in
        o_ref = refs[n0]
        epi_out = refs[n0 + 1:n0 + 1 + n_epi_out]
        side_out = refs[n0 + 1 + n_epi_out:n0 + 1 + n_epi_out + n_side_out]
        rest = refs[n0 + 1 + n_epi_out + n_side_out:]
        acc, sems = (rest[0], rest[1:]) if nk > 1 else (None, rest)
        i, j, k = pl.program_id(0), pl.program_id(1), pl.program_id(2)
        if side:
            @pl.when((i == 0) & (j == 0) & (k == 0))
            def _():
                side.start(side_in, side_out, sems)

        def product():
            if len(b_ref.shape) == 3:
                ns = b_ref.shape[2]
                return sum(lax.dot_general(a_ref[:, p * ns:(p + 1) * ns].astype(BF16), b_ref[p].astype(BF16),
                                           (dims, ((), ())), preferred_element_type=F32)
                           for p in range(b_ref.shape[0]))
            return lax.dot_general(a_ref[...].astype(BF16), b_ref[...].astype(BF16), (dims, ((), ())),
                                   preferred_element_type=F32)

        def finish(r):
            if has_res:
                r = r + r_ref[...]
            if epi is None:
                o_ref[...] = r.astype(o_ref.dtype)
                return
            n_rows = len(epi.rows)
            main, row_vals, sums = epi.fn(r, [t[...] for t in epi_in[:n_rows]], [t[...] for t in epi_in[n_rows:]])
            o_ref[...] = main.astype(o_ref.dtype)
            for ref, val in zip(epi_out, row_vals):
                ref[...] = val.astype(ref.dtype)
            for ref, val in zip(epi_out[len(row_vals):], sums):
                @pl.when(i == 0)
                def _(ref=ref):
                    ref[...] = jnp.zeros_like(ref)

                ref[...] += val

        if nk == 1:
            finish(product())
        else:
            @pl.when(k == 0)
            def _():
                acc[...] = jnp.zeros_like(acc)

            acc[...] += product()

            @pl.when(k == nk - 1)
            def _():
                finish(acc[...])

        if side:
            @pl.when((i == grid[0] - 1) & (j == grid[1] - 1) & (k == grid[2] - 1))
            def _():
                side.wait(side_in, side_out, sems)

    acc_scratch = [pltpu.VMEM(acc_shape, F32)] if nk > 1 else []
    in_specs, out_specs, out_shapes, operands = list(in_specs), [out_spec], [out_shape], list(operands)
    if epi:
        assert grid[1] == 1, "an epilogue needs tiles that span whole rows"
        tm, n = out_spec.block_shape
        row_spec = pl.BlockSpec((tm, n), lambda i, j, k: (i, 0))
        vec_spec = pl.BlockSpec((1, n), lambda i, j, k: (0, 0))
        in_specs += [row_spec] * len(epi.rows) + [vec_spec] * len(epi.vecs)
        operands += epi.rows + epi.vecs
        out_specs += [row_spec] * len(epi.row_out_dtypes) + [vec_spec] * epi.n_sums
        out_shapes += [jax.ShapeDtypeStruct(out_shape.shape, d) for d in epi.row_out_dtypes]
        out_shapes += [jax.ShapeDtypeStruct((1, n), F32)] * epi.n_sums
    scratch = acc_scratch
    if side:
        in_specs += side.in_specs
        operands += side.srcs
        out_specs += side.out_specs
        out_shapes += side.landing
        scratch = acc_scratch + side.scratch
    plain = side is None and epi is None
    res = pl.pallas_call(
        body, name=name, grid=grid, in_specs=in_specs, out_specs=out_specs[0] if plain else tuple(out_specs),
        out_shape=out_shapes[0] if plain else tuple(out_shapes), scratch_shapes=scratch,
        compiler_params=_cp("parallel", "parallel", "arbitrary") if plain else _cp("arbitrary", "arbitrary", "arbitrary"),
    )(*operands)
    return res


def _mm_nn(name, a, b, out_dtype, residual=None, epi=None):
    m, kk = a.shape
    n = b.shape[1]
    tm, tn, tk = _pick(m, (1024, 512, 256)), _pick(n, (1024, 512)), _pick(kk, (2048, 1024, 512))
    ops = [a, b]
    specs = [pl.BlockSpec((tm, tk), lambda i, j, k: (i, k)), pl.BlockSpec((tk, tn), lambda i, j, k: (k, j))]
    if residual is not None:
        ops.append(residual)
        specs.append(pl.BlockSpec((tm, tn), lambda i, j, k: (i, j)))
    return _mm_core(name, ops, specs, pl.BlockSpec((tm, tn), lambda i, j, k: (i, j)),
                    jax.ShapeDtypeStruct((m, n), out_dtype), (m // tm, n // tn, kk // tk), kk // tk,
                    ((1,), (0,)), (tm, tn), residual is not None, epi=epi)


def _mm_nt(name, a, b, out_dtype, residual=None, epi=None):
    m, kk = a.shape
    n = b.shape[0]
    tm, tn, tk = _pick(m, (1024, 512, 256)), _pick(n, (1024, 512)), _pick(kk, (2048, 1024, 512))
    ops = [a, b]
    specs = [pl.BlockSpec((tm, tk), lambda i, j, k: (i, k)), pl.BlockSpec((tn, tk), lambda i, j, k: (j, k))]
    if residual is not None:
        ops.append(residual)
        specs.append(pl.BlockSpec((tm, tn), lambda i, j, k: (i, j)))
    return _mm_core(name, ops, specs, pl.BlockSpec((tm, tn), lambda i, j, k: (i, j)),
                    jax.ShapeDtypeStruct((m, n), out_dtype), (m // tm, n // tn, kk // tk), kk // tk,
                    ((1,), (1,)), (tm, tn), residual is not None, epi=epi)


def _mm_tn(name, a, b, out_dtype):
    kk, m = a.shape
    n = b.shape[1]
    tm, tn, tk = _pick(m, (1024, 512)), _pick(n, (1024, 512)), _pick(kk, (2048, 1024, 512, 256))
    specs = [pl.BlockSpec((tk, tm), lambda i, j, k: (k, i)), pl.BlockSpec((tk, tn), lambda i, j, k: (k, j))]
    return _mm_core(name, [a, b], specs, pl.BlockSpec((tm, tn), lambda i, j, k: (i, j)),
                    jax.ShapeDtypeStruct((m, n), out_dtype), (m // tm, n // tn, kk // tk), kk // tk,
                    ((0,), (0,)), (tm, tn), False)


def _mm_nn_slots(name, a, b_slots, out_dtype, side=None):
    m, kk = a.shape
    s, _, ns = b_slots.shape
    tm, tk = _pick(m, (4096, 2048, 1024, 512, 256)), _pick(kk, (1024, 512))
    specs = [pl.BlockSpec((tm, tk), lambda i, j, k: (i, k)), pl.BlockSpec((None, tk, ns), lambda i, j, k: (j, k, 0))]
    return _mm_core(name, [a, b_slots], specs, pl.BlockSpec((tm, ns), lambda i, j, k: (i, j)),
                    jax.ShapeDtypeStruct((m, s * ns), out_dtype), (m // tm, s, kk // tk), kk // tk,
                    ((1,), (0,)), (tm, ns), False, side)


def _mm_nt_slots(name, a, b_slots, out_dtype, side=None, epi=None):
    m = a.shape[0]
    s, n, ns = b_slots.shape
    tm, tn = _pick(m, (1024, 512, 256)), _pick(n, (1024, 512))
    per = _pick(s, (2, 1))
    specs = [pl.BlockSpec((tm, per * ns), lambda i, j, k: (i, k)),
             pl.BlockSpec((per, tn, ns), lambda i, j, k: (k, j, 0))]
    return _mm_core(name, [a, b_slots], specs, pl.BlockSpec((tm, tn), lambda i, j, k: (i, j)),
                    jax.ShapeDtypeStruct((m, n), out_dtype), (m // tm, n // tn, s // per), s // per,
                    ((1,), (1,)), (tm, tn), False, side, epi)


def _mm_tn_slots(name, a, b, s, out_dtype, side=None):
    kk, m = a.shape
    ns = b.shape[1] // s
    tm, tk = _pick(m, (1024, 512)), _pick(kk, (4096, 2048, 1024, 512, 256))
    specs = [pl.BlockSpec((tk, tm), lambda i, j, k: (k, i)), pl.BlockSpec((tk, ns), lambda i, j, k: (k, j))]
    return _mm_core(name, [a, b], specs, pl.BlockSpec((None, tm, ns), lambda i, j, k: (j, i, 0)),
                    jax.ShapeDtypeStruct((s, m, ns), out_dtype), (m // tm, s, kk // tk), kk // tk,
                    ((0,), (0,)), (tm, ns), False, side)


def _rms_fwd(name, x, g):
    r, d = x.shape
    tr = _pick(r, (1024, 512, 256))

    def body(x_ref, g_ref, o_ref):
        xv = x_ref[...]
        rs = lax.rsqrt(jnp.mean(xv * xv, axis=-1, keepdims=True) + EPS)
        o_ref[...] = (xv * rs * g_ref[...]).astype(o_ref.dtype)

    return pl.pallas_call(
        body, name=name, grid=(r // tr,),
        in_specs=[pl.BlockSpec((tr, d), lambda i: (i, 0)), pl.BlockSpec((1, d), lambda i: (0, 0))],
        out_specs=pl.BlockSpec((tr, d), lambda i: (i, 0)),
        out_shape=jax.ShapeDtypeStruct((r, d), BF16), compiler_params=_cp("parallel"),
    )(x, g)


def _rms_bwd(name, x, g, dh, dres):
    r, d = x.shape
    tr = _pick(r, (512, 256))
    has_res = dres is not None

    def body(*refs):
        if has_res:
            x_ref, g_ref, dh_ref, dr_ref, dx_ref, dg_ref = refs
        else:
            x_ref, g_ref, dh_ref, dx_ref, dg_ref = refs
        i = pl.program_id(0)

        @pl.when(i == 0)
        def _():
            dg_ref[...] = jnp.zeros_like(dg_ref)

        xv = x_ref[...]
        dhv = dh_ref[...].astype(F32)
        rs = lax.rsqrt(jnp.mean(xv * xv, axis=-1, keepdims=True) + EPS)
        xn = xv * rs
        dg_ref[...] += jnp.sum(dhv * xn, axis=0, keepdims=True)
        dn = dhv * g_ref[...]
        dx = rs * (dn - xn * jnp.mean(dn * xn, axis=-1, keepdims=True))
        if has_res:
            dx = dx + dr_ref[...]
        dx_ref[...] = dx

    row = pl.BlockSpec((tr, d), lambda i: (i, 0))
    vec = pl.BlockSpec((1, d), lambda i: (0, 0))
    ops = [x, g, dh] + ([dres] if has_res else [])
    return pl.pallas_call(
        body, name=name, grid=(r // tr,),
        in_specs=[row, vec, row] + ([row] if has_res else []),
        out_specs=(row, vec),
        out_shape=(jax.ShapeDtypeStruct((r, d), F32), jax.ShapeDtypeStruct((1, d), F32)),
        compiler_params=_cp("arbitrary"),
    )(*ops)


def _rot(x, cos_t, sin_t):
    n = x.shape[-1]
    lane = lax.broadcasted_iota(jnp.int32, x.shape, 1)
    partner = jnp.where((lane % RET_DK) < RET_DK // 2, pltpu.roll(x, n - RET_DK // 2, 1), pltpu.roll(x, RET_DK // 2, 1))
    return x * cos_t + partner * sin_t


def _ret_constants(c):
    log_g = jnp.log1p(-jnp.exp2(-5.0 - jnp.arange(RET_HEADS, dtype=F32)))
    j = jnp.arange(c, dtype=F32)
    diff = j[:, None] - j[None, :]
    decay = jnp.where(diff[None] >= 0.0, jnp.exp(log_g[:, None, None] * jnp.maximum(diff, 0.0)[None]), 0.0)
    q_w = jnp.exp(log_g[None, :] * (j + 1.0)[:, None])
    k_w = jnp.exp(log_g[None, :] * (c - 1.0 - j)[:, None])
    cd = jnp.exp(log_g * c)
    rep = lambda t: jnp.repeat(t, RET_DK, axis=1)
    cd_row = jnp.repeat(cd, RET_DV)[None, :]
    return decay, rep(q_w), rep(k_w), cd_row


def _pair_of(h, c):
    lane = lax.broadcasted_iota(jnp.int32, (c, 2 * RET_DK), 1)
    mine = (lane < RET_DK) if h % 2 == 0 else (lane >= RET_DK)
    return slice((h // 2) * 2 * RET_DK, (h // 2 + 1) * 2 * RET_DK), mine


def _keep(x, mine):
    return jnp.where(mine, x, jnp.zeros_like(x))


def _ret_fwd(proj, cos_t, sin_t, consts, gn_g, c):
    l = proj.shape[0]
    nc = l // c
    decay, qw, kw, cd_row = consts

    def body(q_ref, k_ref, v_ref, g_ref, cos_ref, sin_ref, dec_ref, qw_ref, kw_ref, cd_ref, gn_ref,
             ret_ref, o_ref, rp_ref, qb_ref, kb_ref, state):
        @pl.when(pl.program_id(0) == 0)
        def _():
            state[...] = jnp.zeros_like(state)

        cs, sn = cos_ref[...], sin_ref[...]
        qr = _rot(q_ref[...].astype(F32), cs, sn)
        kr = _rot(k_ref[...].astype(F32), cs, sn) * (RET_DK ** -0.5)
        qb, kb = qr.astype(BF16), kr.astype(BF16)
        qb_ref[...] = qb
        kb_ref[...] = kb
        qwb = (qr * qw_ref[...]).astype(BF16)
        kwb = (kr * kw_ref[...]).astype(BF16)
        vb = v_ref[...].astype(BF16)
        for h in range(RET_HEADS):
            ps, mine = _pair_of(h, c)
            vs = slice(h * RET_DV, (h + 1) * RET_DV)
            s = _dot_nt(_keep(qb[:, ps], mine), kb[:, ps]) * dec_ref[h]
            r_prev = state[h]
            rp_ref[0, h] = r_prev
            o = _dot(s.astype(BF16), vb[:, vs]) + _dot(_keep(qwb[:, ps], mine), r_prev.astype(BF16))
            state[h] = cd_ref[:, vs] * r_prev + _dot_tn(_keep(kwb[:, ps], mine), vb[:, vs])
            o_ref[:, vs] = o
            mu = jnp.mean(o, axis=-1, keepdims=True)
            var = jnp.mean(jnp.square(o - mu), axis=-1, keepdims=True)
            on = (o - mu) * lax.rsqrt(var + EPS)
            ret_ref[:, vs] = (on * gn_ref[:, vs] * _silu(g_ref[:, vs].astype(F32))).astype(ret_ref.dtype)

    const2 = lambda shape: pl.BlockSpec(shape, lambda i: (0,) * len(shape))
    return pl.pallas_call(
        body, name="retention_fwd", grid=(nc,),
        in_specs=[pl.BlockSpec((c, RET_QK), lambda i: (i, 0)), pl.BlockSpec((c, RET_QK), lambda i: (i, 1)),
                  pl.BlockSpec((c, D_MODEL), lambda i: (i, 1)), pl.BlockSpec((c, D_MODEL), lambda i: (i, 2)),
                  pl.BlockSpec((c, RET_QK), lambda i: (i, 0)), pl.BlockSpec((c, RET_QK), lambda i: (i, 0)),
                  const2((RET_HEADS, c, c)), const2((c, RET_QK)), const2((c, RET_QK)), const2((1, D_MODEL)),
                  const2((1, D_MODEL))],
        out_specs=(pl.BlockSpec((c, D_MODEL), lambda i: (i, 0)), pl.BlockSpec((c, D_MODEL), lambda i: (i, 0)),
                   pl.BlockSpec((1, RET_HEADS, 2 * RET_DK, RET_DV), lambda i: (i, 0, 0, 0)),
                   pl.BlockSpec((c, RET_QK), lambda i: (i, 0)), pl.BlockSpec((c, RET_QK), lambda i: (i, 0))),
        out_shape=(jax.ShapeDtypeStruct((l, 2 * D_MODEL), BF16), jax.ShapeDtypeStruct((l, D_MODEL), F32),
                   jax.ShapeDtypeStruct((nc, RET_HEADS, 2 * RET_DK, RET_DV), F32),
                   jax.ShapeDtypeStruct((l, RET_QK), BF16), jax.ShapeDtypeStruct((l, RET_QK), BF16)),
        scratch_shapes=[pltpu.VMEM((RET_HEADS, 2 * RET_DK, RET_DV), F32)],
        compiler_params=_cp("arbitrary"),
    )(proj, proj, proj, proj, cos_t, sin_t, decay, qw, kw, cd_row, gn_g)


def _ret_bwd(proj, qb_saved, kb_saved, cos_t, sin_t, consts, gn_g, o_saved, r_prev_saved, dmix, c, side):
    l = proj.shape[0]
    nc = l // c
    decay, qw, kw, cd_row = consts
    n_in = 14

    def body(*refs):
        (q_ref, k_ref, v_ref, g_ref, cos_ref, sin_ref, dec_ref, qw_ref, kw_ref, cd_ref, gn_ref, o_ref, rp_ref,
         dr_ref) = refs[:n_in]
        side_in = refs[n_in:n_in + len(side.srcs)]
        out_ref, dgn_ref = refs[n_in + len(side.srcs):n_in + len(side.srcs) + 2]
        side_out = refs[n_in + len(side.srcs) + 2:n_in + len(side.srcs) + 2 + side.n]
        state, dq_s, dk_s = refs[n_in + len(side.srcs) + 2 + side.n:n_in + len(side.srcs) + 5 + side.n]
        sems = refs[n_in + len(side.srcs) + 5 + side.n:]

        @pl.when(pl.program_id(0) == 0)
        def _():
            side.start(side_in, side_out, sems)
            state[...] = jnp.zeros_like(state)
            dgn_ref[...] = jnp.zeros_like(dgn_ref)

        cs, sn = cos_ref[...], sin_ref[...]
        qb, kb = q_ref[...], k_ref[...]
        qwv, kwv = qw_ref[...], kw_ref[...]
        qwb = (qb.astype(F32) * qwv).astype(BF16)
        kwb = (kb.astype(F32) * kwv).astype(BF16)
        vb = v_ref[...].astype(BF16)
        dq2 = dk2 = None
        for h in range(RET_HEADS):
            ps, mine = _pair_of(h, c)
            vs = slice(h * RET_DV, (h + 1) * RET_DV)
            dec = dec_ref[h]
            qm, km = _keep(qb[:, ps], mine), _keep(kb[:, ps], mine)
            o = o_ref[:, vs]
            mu = jnp.mean(o, axis=-1, keepdims=True)
            var = jnp.mean(jnp.square(o - mu), axis=-1, keepdims=True)
            rstd = lax.rsqrt(var + EPS)
            on = (o - mu) * rstd
            gate = g_ref[:, vs].astype(F32)
            sg = _silu(gate)
            dret = dr_ref[:, vs].astype(F32)
            gn = gn_ref[:, vs]
            dgn_ref[:, vs] += jnp.sum(dret * on * sg, axis=0, keepdims=True)
            out_ref[:, 2 * RET_QK + D_MODEL + h * RET_DV:2 * RET_QK + D_MODEL + (h + 1) * RET_DV] = (
                dret * on * gn * _dsilu(gate)).astype(out_ref.dtype)
            don = dret * gn * sg
            do = rstd * (don - jnp.mean(don, axis=-1, keepdims=True)
                         - on * jnp.mean(don * on, axis=-1, keepdims=True))
            dob = do.astype(BF16)
            sn_h = state[h]
            snb = sn_h.astype(BF16)
            s = _dot_nt(qm, kb[:, ps]) * dec
            dv = _dot_tn(s.astype(BF16), dob) + _dot(_keep(kwb[:, ps], mine), snb)
            out_ref[:, 2 * RET_QK + h * RET_DV:2 * RET_QK + (h + 1) * RET_DV] = dv.astype(out_ref.dtype)
            ds = (_dot_nt(dob, vb[:, vs]) * dec).astype(BF16)
            dq_h = _dot(ds, km) + qwv[:, ps] * _dot_nt(dob, rp_ref[0, h].astype(BF16))
            dk_h = _dot_tn(ds, qm) + kwv[:, ps] * _dot_nt(vb[:, vs], snb)
            state[h] = cd_ref[:, vs] * sn_h + _dot_tn(_keep(qwb[:, ps], mine), dob)
            if h % 2 == 0:
                dq2, dk2 = dq_h, dk_h
            else:
                dq_s[:, ps] = dq2 + dq_h
                dk_s[:, ps] = dk2 + dk_h
        out_ref[:, 0:RET_QK] = _rot(dq_s[...], cs, -sn).astype(out_ref.dtype)
        out_ref[:, RET_QK:2 * RET_QK] = (_rot(dk_s[...], cs, -sn) * (RET_DK ** -0.5)).astype(out_ref.dtype)

        @pl.when(pl.program_id(0) == nc - 1)
        def _():
            side.wait(side_in, side_out, sems)

    rev = lambda i: nc - 1 - i
    const2 = lambda shape: pl.BlockSpec(shape, lambda i: (0,) * len(shape))
    return pl.pallas_call(
        body, name="retention_bwd", grid=(nc,),
        in_specs=[pl.BlockSpec((c, RET_QK), lambda i: (rev(i), 0)), pl.BlockSpec((c, RET_QK), lambda i: (rev(i), 0)),
                  pl.BlockSpec((c, D_MODEL), lambda i: (rev(i), 1)), pl.BlockSpec((c, D_MODEL), lambda i: (rev(i), 2)),
                  pl.BlockSpec((c, RET_QK), lambda i: (rev(i), 0)), pl.BlockSpec((c, RET_QK), lambda i: (rev(i), 0)),
                  const2((RET_HEADS, c, c)), const2((c, RET_QK)), const2((c, RET_QK)), const2((1, D_MODEL)),
                  const2((1, D_MODEL)),
                  pl.BlockSpec((c, D_MODEL), lambda i: (rev(i), 0)),
                  pl.BlockSpec((1, RET_HEADS, 2 * RET_DK, RET_DV), lambda i: (rev(i), 0, 0, 0)),
                  pl.BlockSpec((c, D_MODEL), lambda i: (rev(i), 0))] + side.in_specs,
        out_specs=(pl.BlockSpec((c, 2 * RET_QK + 2 * D_MODEL), lambda i: (rev(i), 0)), const2((1, D_MODEL)),
                   *side.out_specs),
        out_shape=(jax.ShapeDtypeStruct((l, 2 * RET_QK + 4 * D_MODEL), BF16), jax.ShapeDtypeStruct((1, D_MODEL), F32),
                   *side.landing),
        scratch_shapes=[pltpu.VMEM((RET_HEADS, 2 * RET_DK, RET_DV), F32), pltpu.VMEM((c, RET_QK), F32),
                        pltpu.VMEM((c, RET_QK), F32)] + side.scratch,
        compiler_params=_cp("arbitrary"),
    )(qb_saved, kb_saved, proj, proj, cos_t, sin_t, decay, qw, kw, cd_row, gn_g, o_saved, r_prev_saved, dmix,
      *side.srcs)


def _zoh(a_re, a_im, log_dt):
    dt = jnp.exp(log_dt)
    mag = jnp.exp(a_re * dt)
    abar_re = mag * jnp.cos(a_im * dt)
    abar_im = mag * jnp.sin(a_im * dt)
    den = a_re * a_re + a_im * a_im
    nr, ni = abar_re - 1.0, abar_im
    f_re = (nr * a_re + ni * a_im) / den
    f_im = (ni * a_re - nr * a_im) / den
    return dt, abar_re, abar_im, f_re, f_im, den


def _lanes_p(f):
    return jnp.tile(f, (1, S5_P))


def _s5_discretize(a_re, a_im, log_dt, b_re_t, b_im_t):
    def body(ar_ref, ai_ref, ld_ref, br_ref, bi_ref, abr_ref, abi_ref, bbr_ref, bbi_ref):
        _, abar_re, abar_im, f_re, f_im, _ = _zoh(ar_ref[...], ai_ref[...], ld_ref[...])
        abr_ref[...] = abar_re
        abi_ref[...] = abar_im
        fr, fi = _lanes_p(f_re), _lanes_p(f_im)
        bbr_ref[...] = fr * br_ref[...] - fi * bi_ref[...]
        bbi_ref[...] = fr * bi_ref[...] + fi * br_ref[...]

    gn = jax.ShapeDtypeStruct((S5_G, S5_N), F32)
    gpn = jax.ShapeDtypeStruct((S5_G, S5_P * S5_N), F32)
    return pl.pallas_call(body, name="s5_discretize", out_shape=(gn, gn, gpn, gpn))(a_re, a_im, log_dt, b_re_t, b_im_t)


def _s5_discretize_bwd(a_re, a_im, log_dt, b_re_t, b_im_t, dab_re, dab_im, dbb_re_t, dbb_im_t):
    def body(ar_ref, ai_ref, ld_ref, br_ref, bi_ref, gar_ref, gai_ref, gbr_ref, gbi_ref,
             dar_ref, dai_ref, dld_ref, dbr_ref, dbi_ref):
        a_r, a_i = ar_ref[...], ai_ref[...]
        dt, abar_re, abar_im, f_re, f_im, den = _zoh(a_r, a_i, ld_ref[...])
        b_r, b_i, g_br, g_bi = br_ref[...], bi_ref[...], gbr_ref[...], gbi_ref[...]
        fr, fi = _lanes_p(f_re), _lanes_p(f_im)
        dbr_ref[...] = fr * g_br + fi * g_bi
        dbi_ref[...] = fr * g_bi - fi * g_br
        t_r = b_r * g_br + b_i * g_bi
        t_i = b_r * g_bi - b_i * g_br
        gf_r = sum(t_r[:, p * S5_N:(p + 1) * S5_N] for p in range(S5_P))
        gf_i = sum(t_i[:, p * S5_N:(p + 1) * S5_N] for p in range(S5_P))
        inv_r, inv_i = a_r / den, a_i / den
        ga_r = gar_ref[...] + gf_r * inv_r - gf_i * inv_i
        ga_i = gai_ref[...] + gf_r * inv_i + gf_i * inv_r
        q_r = -(f_re * a_r + f_im * a_i) / den
        q_i = -(f_im * a_r - f_re * a_i) / den
        gl_r = q_r * gf_r + q_i * gf_i
        gl_i = q_r * gf_i - q_i * gf_r
        dar_ref[...] = gl_r + dt * (abar_re * ga_r + abar_im * ga_i)
        dai_ref[...] = gl_i + dt * (abar_re * ga_i - abar_im * ga_r)
        la_r = a_r * abar_re - a_i * abar_im
        la_i = a_r * abar_im + a_i * abar_re
        dld_ref[...] = dt * jnp.sum(ga_r * la_r + ga_i * la_i, axis=-1, keepdims=True)

    gn = jax.ShapeDtypeStruct((S5_G, S5_N), F32)
    gpn = jax.ShapeDtypeStruct((S5_G, S5_P * S5_N), F32)
    return pl.pallas_call(
        body, name="s5_discretize_bwd", out_shape=(gn, gn, jax.ShapeDtypeStruct((S5_G, 1), F32), gpn, gpn),
    )(a_re, a_im, log_dt, b_re_t, b_im_t, dab_re, dab_im, dbb_re_t, dbb_im_t)


S5_ZQ = S5_NB // 2


def _s5_z(re, im):
    return jnp.concatenate([re.reshape(S5_ZQ, 8, 128), im.reshape(S5_ZQ, 8, 128)], axis=0)


def _s5_unz(z):
    return z[:S5_ZQ].reshape(S5_G, S5_N), z[S5_ZQ:].reshape(S5_G, S5_N)


def _s5_block_mats(bb_re, bb_im, c_re, c_im):
    eye = jnp.eye(S5_GB, dtype=F32)
    bb = jnp.stack([bb_re, bb_im], axis=0).reshape(2, S5_NB, S5_GB, S5_N, S5_P)
    bbm = jnp.einsum("rbgnp,gh->bgprhn", bb, eye).reshape(S5_NB, S5_GB * S5_P, 2 * S5_BS)
    cc = jnp.stack([c_re, -c_im], axis=0).reshape(2, S5_NB, S5_GB, S5_P, S5_N)
    ccm = jnp.einsum("rbgpn,gh->brhngp", cc, eye).reshape(S5_NB, 2 * S5_BS, S5_GB * S5_P)
    return bbm.astype(BF16), ccm.astype(BF16)


def _s5_block_diag_bb(m):
    t = m.reshape(S5_NB, S5_GB, S5_P, 2, S5_GB, S5_N)
    d = jnp.einsum("bgprgn->rbgnp", t).reshape(2, S5_G, S5_N, S5_P)
    return d[0], d[1]


def _s5_block_diag_cc(m):
    t = m.reshape(S5_NB, 2, S5_GB, S5_N, S5_GB, S5_P)
    d = jnp.einsum("brgngp->rbgpn", t).reshape(2, S5_G, S5_P, S5_N)
    return d[0], -d[1]


SCAN_UNROLL = 8


def _z_store(zr, zi, blk, res, t, off):
    q, h = blk // 2, blk % 2
    for lt in range(4):
        zr[q, pl.ds(off + 4 * h + lt, t, stride=8), :] = res[:, lt * 128:(lt + 1) * 128]
        zi[q, pl.ds(off + 4 * h + lt, t, stride=8), :] = res[:, S5_BS + lt * 128:S5_BS + (lt + 1) * 128]


def _z_load(zr, zi, blk, t, off):
    q, h = blk // 2, blk % 2
    return jnp.concatenate([zr[q, pl.ds(off + 4 * h + lt, t, stride=8), :] for lt in range(4)]
                           + [zi[q, pl.ds(off + 4 * h + lt, t, stride=8), :] for lt in range(4)], axis=1)


def _z_scan_fwd(zr, zi, a_ref, carry_ref, t, off):
    ar = [a_ref[q] for q in range(S5_ZQ)]
    ai = [a_ref[S5_ZQ + q] for q in range(S5_ZQ)]

    def step(it, carry):
        carry = list(carry)
        base = pl.multiple_of(it * (8 * SCAN_UNROLL), 8 * SCAN_UNROLL) + off
        for tt in range(SCAN_UNROLL):
            rows = pl.ds(base + 8 * tt, 8)
            for q in range(S5_ZQ):
                c_r, c_i = carry[q], carry[S5_ZQ + q]
                n_r = ar[q] * c_r - ai[q] * c_i + zr[q, rows, :]
                n_i = ar[q] * c_i + ai[q] * c_r + zi[q, rows, :]
                zr[q, rows, :] = n_r
                zi[q, rows, :] = n_i
                carry[q], carry[S5_ZQ + q] = n_r, n_i
        return tuple(carry)

    out = lax.fori_loop(0, t // SCAN_UNROLL, step, tuple(carry_ref[k] for k in range(2 * S5_ZQ)))
    for k in range(2 * S5_ZQ):
        carry_ref[k] = out[k]


def _z_scan_bwd(lr, li, xr, xi, a_ref, carry_ref, acc_ref, t):
    ar = [a_ref[q] for q in range(S5_ZQ)]
    ai = [a_ref[S5_ZQ + q] for q in range(S5_ZQ)]
    n_it = t // SCAN_UNROLL

    def step(it, state):
        carry, acc = list(state[0]), list(state[1])
        base = pl.multiple_of((n_it - 1 - it) * (8 * SCAN_UNROLL), 8 * SCAN_UNROLL)
        for tt in reversed(range(SCAN_UNROLL)):
            rows = pl.ds(base + 8 * tt, 8)
            for q in range(S5_ZQ):
                c_r, c_i = carry[q], carry[S5_ZQ + q]
                n_r = ar[q] * c_r + ai[q] * c_i + lr[q, rows, :]
                n_i = ar[q] * c_i - ai[q] * c_r + li[q, rows, :]
                lr[q, rows, :] = n_r
                li[q, rows, :] = n_i
                p_r, p_i = xr[q, rows, :], xi[q, rows, :]
                acc[q] = acc[q] + n_r * p_r + n_i * p_i
                acc[S5_ZQ + q] = acc[S5_ZQ + q] + n_i * p_r - n_r * p_i
                carry[q], carry[S5_ZQ + q] = n_r, n_i
        return tuple(carry), tuple(acc)

    k8 = range(2 * S5_ZQ)
    carry, acc = lax.fori_loop(0, n_it, step, (tuple(carry_ref[k] for k in k8), tuple(acc_ref[k] for k in k8)))
    for k in k8:
        carry_ref[k] = carry[k]
        acc_ref[k] = acc[k]


def _s5_fwd(proj, mix, bbm, ccm, d_row, glu_w, glu_b, tabs, t, side):
    l = proj.shape[0]
    nt = l // t
    n_in = 9

    def body(*refs):
        u_ref, gs_ref, bb_ref, cc_ref, d_ref, gw_ref, gb_ref, a_ref, _ = refs[:n_in]
        side_in = refs[n_in:n_in + len(side.srcs)]
        ssm_ref, xst_ref = refs[n_in + len(side.srcs):n_in + len(side.srcs) + 2]
        side_out = refs[n_in + len(side.srcs) + 2:n_in + len(side.srcs) + 2 + side.n]
        zr, zi, carry = refs[n_in + len(side.srcs) + 2 + side.n:n_in + len(side.srcs) + 5 + side.n]
        sems = refs[n_in + len(side.srcs) + 5 + side.n:]

        @pl.when(pl.program_id(0) == 0)
        def _():
            side.start(side_in, side_out, sems)
            carry[...] = jnp.zeros_like(carry)

        xst_ref[0] = carry[...]
        ub = u_ref[...]
        u = ub.astype(F32)
        for blk in range(S5_NB):
            _z_store(zr, zi, blk, _dot(ub[:, blk * 128:(blk + 1) * 128], bb_ref[blk]), t, 0)
        _z_scan_fwd(zr, zi, a_ref, carry, t, 0)
        ys = jnp.concatenate(
            [_dot(_z_load(zr, zi, blk, t, 0).astype(BF16), cc_ref[blk]) for blk in range(S5_NB)], axis=1)
        y2 = _gelu(ys + d_ref[...] * u)
        z = _dot(y2.astype(BF16), gw_ref[...]) + gb_ref[...]
        ssm_ref[...] = (y2 * _sigmoid(z) * _silu(gs_ref[...].astype(F32))).astype(ssm_ref.dtype)

        @pl.when(pl.program_id(0) == nt - 1)
        def _():
            side.wait(side_in, side_out, sems)

    const2 = lambda shape: pl.BlockSpec(shape, lambda i: (0,) * len(shape))
    zshape = (2 * S5_ZQ, 8, 128)
    return pl.pallas_call(
        body, name="s5_fwd", grid=(nt,),
        in_specs=[pl.BlockSpec((t, D_MODEL), lambda i: (i, 3)), pl.BlockSpec((t, D_MODEL), lambda i: (i, 4)),
                  const2(bbm.shape), const2(ccm.shape), const2((1, D_MODEL)), const2((D_MODEL, D_MODEL)),
                  const2((1, D_MODEL)), const2(zshape), pl.BlockSpec(memory_space=pl.ANY)] + side.in_specs,
        out_specs=(pl.BlockSpec((t, D_MODEL), lambda i: (i, 1)), pl.BlockSpec((1,) + zshape, lambda i: (i, 0, 0, 0)),
                   *side.out_specs),
        out_shape=(jax.ShapeDtypeStruct((l, 2 * D_MODEL), BF16), jax.ShapeDtypeStruct((nt,) + zshape, F32),
                   *side.landing),
        scratch_shapes=[pltpu.VMEM((S5_ZQ, 8 * t, 128), F32), pltpu.VMEM((S5_ZQ, 8 * t, 128), F32),
                        pltpu.VMEM(zshape, F32)] + side.scratch,
        input_output_aliases={8: 0},
        compiler_params=_cp("arbitrary"),
    )(proj, proj, bbm, ccm, d_row, glu_w, glu_b, tabs, mix, *side.srcs)


def _s5_bwd(proj, dmix, dproj, xstart, bbm, ccm, d_row, glu_w, glu_b, tabs, t):
    l = proj.shape[0]
    nt = l // t
    col0 = 2 * RET_QK + 2 * D_MODEL

    def body(u_ref, gs_ref, dm_ref, xst_ref, bb_ref, cc_ref, d_ref, gw_ref, gb_ref, a_ref, _,
             dp_ref, y2_ref, dz_ref, dbb_ref, dcc_ref, da_ref, dd_ref, dgb_ref, xr, xi, lr, li, carry, lcarry,
             dug_s, dug_sem):
        step = pl.program_id(0)
        slot = step % 2
        dug_ref = dug_s.at[slot]

        def put(s, at_step):
            rows = pl.ds(pl.multiple_of((nt - 1 - at_step) * t, t), t)
            return pltpu.make_async_copy(dug_s.at[s], dp_ref.at[rows, pl.ds(col0, 2 * D_MODEL)], dug_sem.at[s])

        @pl.when(step >= 2)
        def _():
            put(slot, step - 2).wait()

        @pl.when(step == 0)
        def _():
            lcarry[...] = jnp.zeros_like(lcarry)
            dbb_ref[...] = jnp.zeros_like(dbb_ref)
            dcc_ref[...] = jnp.zeros_like(dcc_ref)
            da_ref[...] = jnp.zeros_like(da_ref)
            dd_ref[...] = jnp.zeros_like(dd_ref)
            dgb_ref[...] = jnp.zeros_like(dgb_ref)

        carry[...] = xst_ref[0]
        for q in range(S5_ZQ):
            xr[q, 0:8, :] = carry[q]
            xi[q, 0:8, :] = carry[S5_ZQ + q]
        ub = u_ref[...]
        u = ub.astype(F32)
        for blk in range(S5_NB):
            _z_store(xr, xi, blk, _dot(ub[:, blk * 128:(blk + 1) * 128], bb_ref[blk]), t, 8)
        _z_scan_fwd(xr, xi, a_ref, carry, t, 8)
        ys = jnp.concatenate(
            [_dot(_z_load(xr, xi, blk, t, 8).astype(BF16), cc_ref[blk]) for blk in range(S5_NB)], axis=1)
        dv = d_ref[...]
        y1 = ys + dv * u
        y2, dgelu = _gelu_and_grad(y1)
        y2b = y2.astype(BF16)
        sg = _sigmoid(_dot(y2b, gw_ref[...]) + gb_ref[...])
        gs = gs_ref[...].astype(F32)
        dssm = dm_ref[...].astype(F32)
        dug_ref[:, D_MODEL:] = (dssm * (y2 * sg) * _dsilu(gs)).astype(dug_ref.dtype)
        dy3 = dssm * _silu(gs)
        dz = dy3 * y2 * sg * (1.0 - sg)
        dzb = dz.astype(BF16)
        y2_ref[...] = y2b
        dz_ref[...] = dzb
        dgb_ref[...] += jnp.sum(dz, axis=0, keepdims=True)
        dy1 = (dy3 * sg + _dot_nt(dzb, gw_ref[...])) * dgelu
        dd_ref[...] += jnp.sum(dy1 * u, axis=0, keepdims=True)
        dyb = dy1.astype(BF16)
        for blk in range(S5_NB):
            ch = slice(blk * 128, (blk + 1) * 128)
            _z_store(lr, li, blk, _dot_nt(dyb[:, ch], cc_ref[blk]), t, 0)
            dcc_ref[blk] += _dot_tn(_z_load(xr, xi, blk, t, 8).astype(BF16), dyb[:, ch])
        _z_scan_bwd(lr, li, xr, xi, a_ref, lcarry, da_ref, t)
        du = []
        for blk in range(S5_NB):
            lb = _z_load(lr, li, blk, t, 0).astype(BF16)
            du.append(_dot_nt(lb, bb_ref[blk]))
            dbb_ref[blk] += _dot_tn(ub[:, blk * 128:(blk + 1) * 128], lb)
        dug_ref[:, :D_MODEL] = (jnp.concatenate(du, axis=1) + dy1 * dv).astype(dug_ref.dtype)
        put(slot, step).start()

        @pl.when(step == nt - 1)
        def _():
            put(slot, step).wait()
            if nt > 1:
                put(1 - slot, step - 1).wait()

    rev = lambda i: nt - 1 - i
    const2 = lambda shape: pl.BlockSpec(shape, lambda i: (0,) * len(shape))
    row_out = lambda w: pl.BlockSpec((t, w), lambda i: (rev(i), 0))
    zshape = (2 * S5_ZQ, 8, 128)
    hbm = pl.BlockSpec(memory_space=pl.ANY)
    return pl.pallas_call(
        body, name="s5_bwd", grid=(nt,),
        in_specs=[pl.BlockSpec((t, D_MODEL), lambda i: (rev(i), 3)), pl.BlockSpec((t, D_MODEL), lambda i: (rev(i), 4)),
                  pl.BlockSpec((t, D_MODEL), lambda i: (rev(i), 1)),
                  pl.BlockSpec((1,) + zshape, lambda i: (rev(i), 0, 0, 0)),
                  const2(bbm.shape), const2(ccm.shape), const2((1, D_MODEL)), const2((D_MODEL, D_MODEL)),
                  const2((1, D_MODEL)), const2(zshape), hbm],
        out_specs=(hbm, row_out(D_MODEL), row_out(D_MODEL), const2(bbm.shape), const2(ccm.shape),
                   const2(zshape), const2((1, D_MODEL)), const2((1, D_MODEL))),
        out_shape=(jax.ShapeDtypeStruct(dproj.shape, BF16), jax.ShapeDtypeStruct((l, D_MODEL), BF16),
                   jax.ShapeDtypeStruct((l, D_MODEL), BF16), jax.ShapeDtypeStruct(bbm.shape, F32),
                   jax.ShapeDtypeStruct(ccm.shape, F32), jax.ShapeDtypeStruct(zshape, F32),
                   jax.ShapeDtypeStruct((1, D_MODEL), F32), jax.ShapeDtypeStruct((1, D_MODEL), F32)),
        scratch_shapes=[pltpu.VMEM((S5_ZQ, 8 * t + 8, 128), F32), pltpu.VMEM((S5_ZQ, 8 * t + 8, 128), F32),
                        pltpu.VMEM((S5_ZQ, 8 * t, 128), F32), pltpu.VMEM((S5_ZQ, 8 * t, 128), F32),
                        pltpu.VMEM(zshape, F32), pltpu.VMEM(zshape, F32),
                        pltpu.VMEM((2, t, 2 * D_MODEL), BF16), pltpu.SemaphoreType.DMA((2,))],
        input_output_aliases={10: 0},
        compiler_params=_cp("arbitrary"),
    )(proj, proj, dmix, xstart, bbm, ccm, d_row, glu_w, glu_b, tabs, dproj)


def _attn_probs(qh, kh):
    s = _dot_nt(qh, kh) * (XA_DH ** -0.5)
    e = jnp.exp(s - jnp.max(s, axis=-1, keepdims=True))
    return e / jnp.sum(e, axis=-1, keepdims=True)


def _attn_fwd(qa, ka, va):
    l = qa.shape[0]
    m = ka.shape[0]
    tl = _pick(l, (1024, 512, 256))

    def body(q_ref, k_ref, v_ref, o_ref):
        for h in range(XA_HEADS):
            hs = slice(h * XA_DH, (h + 1) * XA_DH)
            p = _attn_probs(q_ref[:, hs], k_ref[:, hs])
            o_ref[:, hs] = _dot(p.astype(BF16), v_ref[:, hs]).astype(o_ref.dtype)

    return pl.pallas_call(
        body, name="xattn_fwd", grid=(l // tl,),
        in_specs=[pl.BlockSpec((tl, D_MODEL), lambda i: (i, 0)), pl.BlockSpec((m, D_MODEL), lambda i: (0, 0)),
                  pl.BlockSpec((m, D_MODEL), lambda i: (0, 0))],
        out_specs=pl.BlockSpec((tl, D_MODEL), lambda i: (i, 0)),
        out_shape=jax.ShapeDtypeStruct((l, D_MODEL), BF16), compiler_params=_cp("parallel"),
    )(qa, ka, va)


def _attn_bwd(qa, ka, va, doa):
    l = qa.shape[0]
    m = ka.shape[0]
    tl = _pick(l, (1024, 512, 256))

    def body(q_ref, k_ref, v_ref, do_ref, dq_ref, dk_ref, dv_ref):
        @pl.when(pl.program_id(0) == 0)
        def _():
            dk_ref[...] = jnp.zeros_like(dk_ref)
            dv_ref[...] = jnp.zeros_like(dv_ref)

        for h in range(XA_HEADS):
            hs = slice(h * XA_DH, (h + 1) * XA_DH)
            qh, kh, vh, doh = q_ref[:, hs], k_ref[:, hs], v_ref[:, hs], do_ref[:, hs]
            p = _attn_probs(qh, kh)
            dv_ref[:, hs] += _dot_tn(p.astype(BF16), doh)
            dp = _dot_nt(doh, vh)
            ds = (p * (dp - jnp.sum(dp * p, axis=-1, keepdims=True)) * (XA_DH ** -0.5)).astype(BF16)
            dq_ref[:, hs] = _dot(ds, kh).astype(dq_ref.dtype)
            dk_ref[:, hs] += _dot_tn(ds, qh)

    row = pl.BlockSpec((tl, D_MODEL), lambda i: (i, 0))
    mem = pl.BlockSpec((m, D_MODEL), lambda i: (0, 0))
    return pl.pallas_call(
        body, name="xattn_bwd", grid=(l // tl,), in_specs=[row, mem, mem, row], out_specs=(row, mem, mem),
        out_shape=(jax.ShapeDtypeStruct((l, D_MODEL), BF16), jax.ShapeDtypeStruct((m, D_MODEL), F32),
                   jax.ShapeDtypeStruct((m, D_MODEL), F32)),
        compiler_params=_cp("arbitrary"),
    )(qa, ka, va, doa)


def _me_and_peers():
    x, y, c = lax.axis_index("x"), lax.axis_index("y"), lax.axis_index("c")
    flip = lambda v, bit: (1 - v) if bit else v
    peers = []
    for k in range(1, N_DEV):
        px, py, pc = flip(x, (k >> 2) & 1), flip(y, (k >> 1) & 1), flip(c, k & 1)
        peers.append(((px, py, pc), 4 * px + 2 * py + pc))
    return 4 * x + 2 * y + c, peers


class _SideJob:
    def __init__(self, srcs, landing, src_of, dst_of):
        self.srcs = list(srcs)
        self.landing = list(landing)
        self.n = len(self.landing)
        self.src_of, self.dst_of = src_of, dst_of
        hbm = pl.BlockSpec(memory_space=pl.ANY)
        self.in_specs = [hbm] * len(self.srcs)
        self.out_specs = [hbm] * self.n
        self.scratch = [pltpu.SemaphoreType.DMA((self.n * (N_DEV - 1),)), pltpu.SemaphoreType.DMA((self.n * (N_DEV - 1),)),
                        pltpu.SemaphoreType.DMA((self.n,))]

    def _copies(self, src_refs, out_refs, sems):
        send_sems, recv_sems, loc_sems = sems
        me, peers = _me_and_peers()
        local = [pltpu.make_async_copy(self.src_of(a, me, src_refs), self.dst_of(a, me, out_refs), loc_sems.at[a])
                 for a in range(self.n)]
        sends, recvs = [], []
        for k, (peer, peer_idx) in enumerate(peers):
            for a in range(self.n):
                s = self.n * k + a
                sends.append(pltpu.make_async_remote_copy(
                    src_ref=self.src_of(a, peer_idx, src_refs), dst_ref=self.dst_of(a, me, out_refs),
                    send_sem=send_sems.at[s], recv_sem=recv_sems.at[s], device_id=peer, device_id_type=MESH))
                recvs.append(pltpu.make_async_remote_copy(
                    src_ref=self.src_of(a, me, src_refs), dst_ref=self.dst_of(a, peer_idx, out_refs),
                    send_sem=send_sems.at[s], recv_sem=recv_sems.at[s], device_id=peer, device_id_type=MESH))
        return local, sends, recvs

    def start(self, src_refs, out_refs, sems):
        if not self.n:
            return
        local, sends, _ = self._copies(src_refs, out_refs, sems)
        for cp in local + sends:
            cp.start()

    def wait(self, src_refs, out_refs, sems):
        if not self.n:
            return
        local, sends, recvs = self._copies(src_refs, out_refs, sems)
        for cp in recvs:
            cp.wait_recv()
        for cp in sends:
            cp.wait_send()
        for cp in local:
            cp.wait()


def _gather_job(shards):
    return _SideJob(shards, [jax.ShapeDtypeStruct((N_DEV,) + s.shape, s.dtype) for s in shards],
                    src_of=lambda a, j, srcs: srcs[a], dst_of=lambda a, j, outs: outs[a].at[j])


def _scatter_job(grads):
    landing, parts = [], []
    for g in grads:
        if g.ndim == 3:
            landing.append(jax.ShapeDtypeStruct(g.shape, g.dtype))
            parts.append(None)
        else:
            r = g.shape[0] // N_DEV
            landing.append(jax.ShapeDtypeStruct((N_DEV, r, g.shape[1]), g.dtype))
            parts.append(r)

    def src_of(a, j, srcs):
        if parts[a] is None:
            return srcs[a].at[j]
        return srcs[a].at[pl.ds(pl.multiple_of(j * parts[a], 8), parts[a]), :]

    return _SideJob(grads, landing, src_of=src_of, dst_of=lambda a, j, outs: outs[a].at[j])


def _prologue(w_in_shard, row_shards, x, g, pos_col, inv_row):
    n_row = len(row_shards)
    l, d = x.shape
    tr = _pick(l, (1024, 512, 256))
    nt = l // tr
    mid = nt // 2

    def body(*refs):
        win_ref = refs[0]
        row_refs = refs[1:1 + n_row]
        x_ref, g_ref, p_ref, inv_ref = refs[1 + n_row:5 + n_row]
        out_win = refs[5 + n_row]
        row_outs = refs[6 + n_row:6 + 2 * n_row]
        h_ref, cos_ref, sin_ref = refs[6 + 2 * n_row:9 + 2 * n_row]
        win_b, send_sems, recv_sems, local_sem = refs[9 + 2 * n_row:]
        step = pl.program_id(0)
        cx, cy, cc = lax.axis_index("x"), lax.axis_index("y"), lax.axis_index("c")
        me, sibling = (cx, cy, cc), (cx, cy, 1 - cc)
        chips = [(1 - cx, cy), (cx, 1 - cy), (1 - cx, 1 - cy)]
        slot = lambda p: out_win.at[4 * p[0] + 2 * p[1] + p[2]]

        def copy(k, block, to, src=None):
            return pltpu.make_async_remote_copy(
                src_ref=slot(block) if src is None else src, dst_ref=slot(block), send_sem=send_sems.at[k],
                recv_sem=recv_sems.at[k], device_id=to, device_id_type=MESH)

        mine = pltpu.make_async_copy(win_b, slot(me), local_sem)
        first = [copy(0, me, sibling, src=win_b)]
        first += [copy(1 + j, me, (*chip, cc), src=win_b) for j, chip in enumerate(chips)]
        passed = [copy(4 + j, (*chip, cc), sibling) for j, chip in enumerate(chips)]

        @pl.when(step == 0)
        def _():
            win_b[...] = win_ref[...].astype(BF16)
            mine.start()
            for cp in first:
                cp.start()
            for r, o in zip(row_refs, row_outs):
                o[...] = r[...].astype(BF16)

        h_ref[...] = (_rms(x_ref[...])[1] * g_ref[...]).astype(h_ref.dtype)
        ang = p_ref[...].astype(F32) * inv_ref[...]
        lane = lax.broadcasted_iota(jnp.int32, ang.shape, 1)
        cos_ref[...] = jnp.tile(jnp.cos(ang), (1, RET_QK // 128))
        sin_ref[...] = jnp.tile(jnp.where((lane % RET_DK) < RET_DK // 2, -jnp.sin(ang), jnp.sin(ang)),
                                (1, RET_QK // 128))

        @pl.when(step == mid)
        def _():
            for j, chip in enumerate(chips):
                copy(1 + j, (*chip, cc), me).wait_recv()
                passed[j].start()

        @pl.when(step == nt - 1)
        def _():
            copy(0, sibling, me).wait_recv()
            for j, chip in enumerate(chips):
                copy(4 + j, (*chip, 1 - cc), me).wait_recv()
            for cp in first + passed:
                cp.wait_send()
            mine.wait()

    whole = lambda a: pl.BlockSpec(a.shape, lambda i: (0,) * a.ndim)
    rows = lambda w: pl.BlockSpec((tr, w), lambda i: (i, 0))
    return pl.pallas_call(
        body, name="prologue_allgather_w_in", grid=(nt,),
        in_specs=[whole(w_in_shard)] + [whole(r) for r in row_shards] + [rows(d), whole(g), rows(1), whole(inv_row)],
        out_specs=(pl.BlockSpec(memory_space=pl.ANY), *[whole(r) for r in row_shards], rows(d), rows(RET_QK),
                   rows(RET_QK)),
        out_shape=(jax.ShapeDtypeStruct((N_DEV,) + w_in_shard.shape, BF16),
                   *[jax.ShapeDtypeStruct(r.shape, BF16) for r in row_shards],
                   jax.ShapeDtypeStruct((l, d), BF16), jax.ShapeDtypeStruct((l, RET_QK), F32),
                   jax.ShapeDtypeStruct((l, RET_QK), F32)),
        scratch_shapes=[pltpu.VMEM(w_in_shard.shape, BF16), pltpu.SemaphoreType.DMA((N_DEV - 1,)),
                        pltpu.SemaphoreType.DMA((N_DEV - 1,)), pltpu.SemaphoreType.DMA],
        compiler_params=_cp("arbitrary"),
    )(w_in_shard, *row_shards, x, g, pos_col, inv_row)


def _allreduce_small(small):
    rows = SMALL_ROWS // N_DEV

    def body(x_ref, out_ref, land, send1, recv1, send2, recv2):
        me, peers = _me_and_peers()
        block = lambda j: pl.ds(pl.multiple_of(j * rows, 8), rows)

        def phase(src_of, dst_of, send_sems, recv_sems):
            sends = [pltpu.make_async_remote_copy(src_ref=src_of(pidx), dst_ref=dst_of(me), send_sem=send_sems.at[k],
                                                  recv_sem=recv_sems.at[k], device_id=peer, device_id_type=MESH)
                     for k, (peer, pidx) in enumerate(peers)]
            recvs = [pltpu.make_async_remote_copy(src_ref=src_of(me), dst_ref=dst_of(pidx), send_sem=send_sems.at[k],
                                                  recv_sem=recv_sems.at[k], device_id=peer, device_id_type=MESH)
                     for k, (peer, pidx) in enumerate(peers)]
            for cp in sends:
                cp.start()
            for cp in recvs:
                cp.wait_recv()
            for cp in sends:
                cp.wait_send()

        land[me] = x_ref[block(me), :]
        phase(lambda j: x_ref.at[block(j), :], lambda j: land.at[j], send1, recv1)
        total = land[0]
        for j in range(1, N_DEV):
            total = total + land[j]
        out_ref[block(me), :] = total
        phase(lambda j: out_ref.at[block(me), :], lambda j: out_ref.at[block(j), :], send2, recv2)

    vm = pl.BlockSpec(memory_space=pltpu.VMEM)
    return pl.pallas_call(
        body, name="allreduce_small", in_specs=[vm], out_specs=vm, out_shape=jax.ShapeDtypeStruct(small.shape, F32),
        scratch_shapes=[pltpu.VMEM((N_DEV, rows, D_MODEL), F32)] + [pltpu.SemaphoreType.DMA((N_DEV - 1,))] * 4,
    )(small)


def _adamw(name, got, w, m, v):
    r, c = w.shape
    n_slots = got.shape[0]
    tr = _pick(r, (256, 128, 64))

    def body(got_ref, w_ref, m_ref, v_ref, g_ref, d_ref, nm_ref, nv_ref):
        g = got_ref[0].astype(F32)
        for j in range(1, n_slots):
            g = g + got_ref[j].astype(F32)
        nm = ADAM_B1 * m_ref[...] + (1.0 - ADAM_B1) * g
        nv = ADAM_B2 * v_ref[...] + (1.0 - ADAM_B2) * jnp.square(g)
        m_hat = nm / (1.0 - ADAM_B1 ** ADAM_STEP)
        v_hat = nv / (1.0 - ADAM_B2 ** ADAM_STEP)
        g_ref[...] = g
        d_ref[...] = -ADAM_LR * (m_hat / (jnp.sqrt(v_hat) + ADAM_EPS) + ADAM_WD * w_ref[...])
        nm_ref[...] = nm
        nv_ref[...] = nv

    blk = pl.BlockSpec((tr, c), lambda i: (i, 0))
    out = jax.ShapeDtypeStruct((r, c), F32)
    return pl.pallas_call(
        body, name=name, grid=(r // tr,),
        in_specs=[pl.BlockSpec((n_slots, tr, c), lambda i: (0, i, 0)), blk, blk, blk],
        out_specs=(blk, blk, blk, blk), out_shape=(out, out, out, out), compiler_params=_cp("parallel"),
    )(got, w, m, v)


_SMALL_VECS = ("norm1_g", "ret_gn_g", "s5_d", "s5_glu_b", "norm2_g", "norm_mem_g", "norm_f_g")


def _small_layout():
    lay, row = {}, 0
    for n in _SMALL_VECS + ("loss",):
        lay[n] = (row, 1, D_MODEL)
        row += 1
    for n in ("s5_a_re", "s5_a_im"):
        lay[n] = (row, 4, D_MODEL)
        row += 4
    lay["s5_log_dt"] = (row, 1, S5_G)
    row += 8
    for n in ("s5_b_re", "s5_b_im", "s5_c_re", "s5_c_im"):
        lay[n] = (row, 64, D_MODEL)
        row += 64
    assert row <= SMALL_ROWS
    return lay


def _pack_small(t, loss_row=None):
    lay = _small_layout()
    pieces = [t[n].reshape(1, D_MODEL) for n in _SMALL_VECS]
    pieces.append(jnp.zeros((1, D_MODEL), F32) if loss_row is None else loss_row)
    pieces += [t["s5_a_re"].reshape(4, D_MODEL), t["s5_a_im"].reshape(4, D_MODEL)]
    pieces.append(jnp.pad(t["s5_log_dt"].reshape(1, S5_G), ((0, 7), (0, D_MODEL - S5_G))))
    pieces += [t[n].reshape(64, D_MODEL) for n in ("s5_b_re", "s5_b_im", "s5_c_re", "s5_c_im")]
    pieces.append(jnp.zeros((SMALL_ROWS - lay["s5_c_im"][0] - 64, D_MODEL), F32))
    return jnp.concatenate(pieces, axis=0)


def _adamw_small(g_sum, w, m, v):
    lay = _small_layout()
    names = [n for n in lay if n != "loss"]

    def body(g_ref, w_ref, m_ref, v_ref, *outs):
        g = g_ref[...]
        nm = ADAM_B1 * m_ref[...] + (1.0 - ADAM_B1) * g
        nv = ADAM_B2 * v_ref[...] + (1.0 - ADAM_B2) * jnp.square(g)
        m_hat = nm / (1.0 - ADAM_B1 ** ADAM_STEP)
        v_hat = nv / (1.0 - ADAM_B2 ** ADAM_STEP)
        delta = -ADAM_LR * (m_hat / (jnp.sqrt(v_hat) + ADAM_EPS) + ADAM_WD * w_ref[...])
        for i, n in enumerate(names):
            r0, rows, lanes = lay[n]
            for part, val in enumerate((g, delta, nm, nv)):
                outs[4 * i + part][...] = val[r0:r0 + rows, 0:lanes]
        r0 = lay["loss"][0]
        outs[-1][...] = g[r0:r0 + 1, :]

    shapes = []
    for n in names:
        shapes += [jax.ShapeDtypeStruct(lay[n][1:], F32)] * 4
    shapes.append(jax.ShapeDtypeStruct((1, D_MODEL), F32))
    outs = pl.pallas_call(body, name="adamw_small", out_shape=tuple(shapes),
                          compiler_params=pltpu.CompilerParams(vmem_limit_bytes=VMEM_LIMIT))(g_sum, w, m, v)
    return {n: tuple(outs[4 * i:4 * i + 4]) for i, n in enumerate(names)}, outs[-1]


_W_NAMES = ("norm1_g", "w_in", "ret_gn_g", "s5_a_re", "s5_a_im", "s5_log_dt", "s5_b_re", "s5_b_im", "s5_c_re", "s5_c_im",
            "s5_d", "s5_glu_w", "s5_glu_b", "w_out", "norm2_g", "norm_mem_g", "xa_wq", "xa_wk", "xa_wv", "xa_wo",
            "norm_f_g")
_ROW_NAMES = ("s5_glu_w", "w_out", "xa_wq", "xa_wk", "xa_wv", "xa_wo")


def kernel(x, mem, positions, norm1_g, w_in, ret_gn_g, s5_a_re, s5_a_im, s5_log_dt, s5_b_re, s5_b_im, s5_c_re, s5_c_im, s5_d, s5_glu_w, s5_glu_b, w_out, norm2_g, norm_mem_g, xa_wq, xa_wk, xa_wv, xa_wo, norm_f_g, loss_target, m_norm1_g, m_w_in, m_ret_gn_g, m_s5_a_re, m_s5_a_im, m_s5_log_dt, m_s5_b_re, m_s5_b_im, m_s5_c_re, m_s5_c_im, m_s5_d, m_s5_glu_w, m_s5_glu_b, m_w_out, m_norm2_g, m_norm_mem_g, m_xa_wq, m_xa_wk, m_xa_wv, m_xa_wo, m_norm_f_g, v_norm1_g, v_w_in, v_ret_gn_g, v_s5_a_re, v_s5_a_im, v_s5_log_dt, v_s5_b_re, v_s5_b_im, v_s5_c_re, v_s5_c_im, v_s5_d, v_s5_glu_w, v_s5_glu_b, v_w_out, v_norm2_g, v_norm_mem_g, v_xa_wq, v_xa_wk, v_xa_wv, v_xa_wo, v_norm_f_g):
    w = dict(norm1_g=norm1_g, w_in=w_in, ret_gn_g=ret_gn_g, s5_a_re=s5_a_re, s5_a_im=s5_a_im, s5_log_dt=s5_log_dt,
             s5_b_re=s5_b_re, s5_b_im=s5_b_im, s5_c_re=s5_c_re, s5_c_im=s5_c_im, s5_d=s5_d, s5_glu_w=s5_glu_w,
             s5_glu_b=s5_glu_b, w_out=w_out, norm2_g=norm2_g, norm_mem_g=norm_mem_g, xa_wq=xa_wq, xa_wk=xa_wk,
             xa_wv=xa_wv, xa_wo=xa_wo, norm_f_g=norm_f_g)
    mom = dict(norm1_g=m_norm1_g, w_in=m_w_in, ret_gn_g=m_ret_gn_g, s5_a_re=m_s5_a_re, s5_a_im=m_s5_a_im,
               s5_log_dt=m_s5_log_dt, s5_b_re=m_s5_b_re, s5_b_im=m_s5_b_im, s5_c_re=m_s5_c_re, s5_c_im=m_s5_c_im,
               s5_d=m_s5_d, s5_glu_w=m_s5_glu_w, s5_glu_b=m_s5_glu_b, w_out=m_w_out, norm2_g=m_norm2_g,
               norm_mem_g=m_norm_mem_g, xa_wq=m_xa_wq, xa_wk=m_xa_wk, xa_wv=m_xa_wv, xa_wo=m_xa_wo,
               norm_f_g=m_norm_f_g)
    var = dict(norm1_g=v_norm1_g, w_in=v_w_in, ret_gn_g=v_ret_gn_g, s5_a_re=v_s5_a_re, s5_a_im=v_s5_a_im,
               s5_log_dt=v_s5_log_dt, s5_b_re=v_s5_b_re, s5_b_im=v_s5_b_im, s5_c_re=v_s5_c_re, s5_c_im=v_s5_c_im,
               s5_d=v_s5_d, s5_glu_w=v_s5_glu_w, s5_glu_b=v_s5_glu_b, w_out=v_w_out, norm2_g=v_norm2_g,
               norm_mem_g=v_norm_mem_g, xa_wq=v_xa_wq, xa_wk=v_xa_wk, xa_wv=v_xa_wv, xa_wo=v_xa_wo,
               norm_f_g=v_norm_f_g)
    shapes = {n: w[n].shape for n in _W_NAMES}

    x2d, mem2d, tgt = x[0], mem[0], loss_target[0]
    l = x2d.shape[0]
    ret_c = _pick(l, (256, 128))
    s5_t = _pick(l, (256, 128))
    g1, g2, gm, gf = norm1_g, norm2_g, norm_mem_g, norm_f_g.reshape(1, D_MODEL)

    half = RET_DK // 2
    inv = ROPE_BASE ** (-jnp.arange(half, dtype=F32) / half)
    win_s, *rest = _prologue(w_in[0], [w[n][0] for n in _ROW_NAMES], x2d, g1, positions[0].reshape(l, 1),
                             jnp.tile(inv, 128 // half)[None, :])
    row_shards_b, (h1, cos_t, sin_t) = rest[:len(_ROW_NAMES)], rest[len(_ROW_NAMES):]

    to_gpn = lambda b: jnp.transpose(b, (0, 2, 1)).reshape(S5_G, S5_P * S5_N)
    from_gpn = lambda b: jnp.transpose(b.reshape(S5_G, S5_P, S5_N), (0, 2, 1))
    disc_args = (s5_a_re[0], s5_a_im[0], s5_log_dt[0].reshape(S5_G, 1), to_gpn(s5_b_re[0]), to_gpn(s5_b_im[0]))
    abar_re, abar_im, bb_re_t, bb_im_t = _s5_discretize(*disc_args)
    bbm, ccm = _s5_block_mats(from_gpn(bb_re_t), from_gpn(bb_im_t), s5_c_re[0], s5_c_im[0])
    a_z = _s5_z(abar_re, abar_im)

    proj, *rows_01 = _mm_nn_slots("in_proj", h1, win_s, BF16, side=_gather_job(row_shards_b[:2]))
    full = {n: g.reshape(N_DEV * r, D_MODEL) for n, g, r in zip(_ROW_NAMES[:2], rows_01, ROW_SHARDS[:2])}
    rconsts = _ret_constants(ret_c)
    ret, o_saved, r_prev, q_rot, k_rot = _ret_fwd(proj, cos_t, sin_t, rconsts, ret_gn_g, ret_c)
    mix, xstart, *rows_xa = _s5_fwd(proj, ret, bbm, ccm, s5_d, full["s5_glu_w"], s5_glu_b, a_z, s5_t,
                                    side=_gather_job(row_shards_b[2:]))
    full.update({n: g.reshape(N_DEV * r, D_MODEL) for n, g, r in zip(_ROW_NAMES[2:], rows_xa, ROW_SHARDS[2:])})
    x1, h2 = _mm_nn("out_proj", mix, full["w_out"], F32, residual=x2d, epi=_epi_norm_fwd(g2))
    mn = _rms_fwd("norm_mem_fwd", mem2d, gm)
    qa = _mm_nn("xa_q", h2, full["xa_wq"], BF16)
    ka = _mm_nn("xa_k", mn, full["xa_wk"], BF16)
    va = _mm_nn("xa_v", mn, full["xa_wv"], BF16)
    oa = _attn_fwd(qa, ka, va)
    dx2, dgf, loss_lanes = _mm_nn("xa_o", oa, full["xa_wo"], F32, residual=x1, epi=_epi_loss(gf, tgt))

    doa = _mm_nt("xa_o_dx", dx2, full["xa_wo"], BF16)
    dwo = _mm_tn("xa_o_dw", oa, dx2, BF16)
    dqa, dka, dva = _attn_bwd(qa, ka, va, doa)
    dx1, dg2 = _mm_nt("xa_q_dx", dqa, full["xa_wq"], F32, epi=_epi_norm_bwd(x1, g2, dx2))
    dwq = _mm_tn("xa_q_dw", h2, dqa, BF16)
    dwk = _mm_tn("xa_k_dw", mn, dka, BF16)
    dwv = _mm_tn("xa_v_dw", mn, dva, BF16)
    dmn = _mm_nt("xa_v_dx", dva, full["xa_wv"], F32, residual=_mm_nt("xa_k_dx", dka, full["xa_wk"], F32))
    _, dgm = _rms_bwd("norm_mem_bwd", mem2d, gm, dmn, None)
    dmix = _mm_nt("out_proj_dx", dx1, full["w_out"], BF16)
    dwout = _mm_tn("out_proj_dw", mix, dx1, BF16)
    dret, dgn, *got_a = _ret_bwd(proj, q_rot, k_rot, cos_t, sin_t, rconsts, ret_gn_g, o_saved, r_prev, dmix, ret_c,
                                 side=_scatter_job([dwout, dwq, dwk, dwv, dwo]))
    dproj, y2, dz, dbbm, dccm, dabar, dd, dgb = _s5_bwd(proj, dmix, dret, xstart, bbm, ccm, s5_d, full["s5_glu_w"],
                                                        s5_glu_b, a_z, s5_t)
    dglu = _mm_tn("s5_glu_dw", y2, dz, BF16)
    dwin_s, got_glu = _mm_tn_slots("in_proj_dw", h1, dproj, N_DEV, BF16, side=_scatter_job([dglu]))
    grad_x, dg1, got_win = _mm_nt_slots("in_proj_dx", dproj, win_s, F32, side=_scatter_job([dwin_s]),
                                        epi=_epi_norm_bwd(x2d, g1, dx1))

    dab_re, dab_im = _s5_unz(dabar)
    dbb_re, dbb_im = _s5_block_diag_bb(dbbm)
    dc_re, dc_im = _s5_block_diag_cc(dccm)
    da_re, da_im, dlog_dt, db_re_t, db_im_t = _s5_discretize_bwd(*disc_args, dab_re, dab_im, to_gpn(dbb_re),
                                                                 to_gpn(dbb_im))
    db_re, db_im = from_gpn(db_re_t), from_gpn(db_im_t)
    small_g = dict(norm1_g=dg1, ret_gn_g=dgn, s5_d=dd, s5_glu_b=dgb, norm2_g=dg2, norm_mem_g=dgm, norm_f_g=dgf,
                   s5_a_re=da_re, s5_a_im=da_im, s5_log_dt=dlog_dt, s5_b_re=db_re, s5_b_im=db_im, s5_c_re=dc_re,
                   s5_c_im=dc_im)
    small_pack = _pack_small(small_g, loss_row=loss_lanes)

    res = {}
    got = dict(zip(("w_out", "xa_wq", "xa_wk", "xa_wv", "xa_wo"), got_a), w_in=got_win, s5_glu_w=got_glu)
    for n in ("w_in",) + _ROW_NAMES:
        res[n] = _adamw("adamw_" + n, got[n], w[n][0], mom[n][0], var[n][0])
    small_sum = _allreduce_small(small_pack)
    small_res, loss_sum = _adamw_small(small_sum, _pack_small(w), _pack_small(mom), _pack_small(var))
    loss = (0.5 / D_MODEL) * jnp.sum(loss_sum)
    res.update(small_res)

    outs = [loss, grad_x[None]]
    for part in range(4):
        for n in _W_NAMES:
            outs.append(res[n][part].reshape(shapes[n]))
    return tuple(outs)
```

```python
import jax
import jax.numpy as jnp
from jax import lax
from jax.experimental import pallas as pl
from jax.experimental.pallas import tpu as pltpu

F32 = jnp.float32
BF16 = jnp.bfloat16
MESH = pl.DeviceIdType.MESH

D_MODEL = 1024
RET_HEADS, RET_DK, RET_DV = 8, 64, 128
RET_QK = RET_HEADS * RET_DK
S5_G, S5_N, S5_P = 64, 64, 16
S5_NB = 8
S5_GB = S5_G // S5_NB
S5_BS = S5_GB * S5_N
S5_COLS = 2 * S5_G * S5_N
XA_HEADS, XA_DH = 4, 256
EPS = 1e-6
ROPE_BASE = 10000.0
N_DEV = 8
W_IN_SHARD = 640
ROW_SHARDS = (128, 256, 128, 128, 128, 128)
ROWPACK = sum(ROW_SHARDS)
SMALL_ROWS = 320
ADAM_LR, ADAM_B1, ADAM_B2, ADAM_EPS, ADAM_WD, ADAM_STEP = 0.001, 0.9, 0.999, 1e-08, 0.01, 10

VMEM_LIMIT = 56 * 1024 * 1024


def _cp(*sem):
    return pltpu.CompilerParams(dimension_semantics=tuple(sem), vmem_limit_bytes=VMEM_LIMIT)


def _dot(a, b):
    return jnp.dot(a, b, preferred_element_type=F32)


def _dot_nt(a, b):
    return lax.dot_general(a, b, (((1,), (1,)), ((), ())), preferred_element_type=F32)


def _dot_tn(a, b):
    return lax.dot_general(a, b, (((0,), (0,)), ((), ())), preferred_element_type=F32)


def _sigmoid(x):
    return 1.0 / (1.0 + jnp.exp(-x))


def _silu(x):
    return x * _sigmoid(x)


def _dsilu(x):
    s = _sigmoid(x)
    return s * (1.0 + x * (1.0 - s))


_GELU_C = 0.7978845608028654


def _gelu(x):
    return 0.5 * x * (1.0 + jnp.tanh(_GELU_C * (x + 0.044715 * (x * x * x))))


def _gelu_and_grad(x):
    t = jnp.tanh(_GELU_C * (x + 0.044715 * (x * x * x)))
    half = 0.5 * (1.0 + t)
    return x * half, half + 0.5 * x * (1.0 - t * t) * (_GELU_C * (1.0 + 3.0 * 0.044715 * (x * x)))


def _pick(n, cands):
    for c in cands:
        if n % c == 0:
            return c
    return n


class _Epilogue:
    def __init__(self, rows, vecs, row_out_dtypes, n_sums, fn):
        self.rows, self.vecs, self.row_out_dtypes, self.n_sums, self.fn = list(rows), list(vecs), list(row_out_dtypes), n_sums, fn


def _rms(x):
    rs = lax.rsqrt(jnp.mean(x * x, axis=-1, keepdims=True) + EPS)
    return rs, x * rs


def _rms_dx(dn, xn, rs):
    return rs * (dn - xn * jnp.mean(dn * xn, axis=-1, keepdims=True))


def _epi_norm_fwd(g):
    def fn(r, rows, vecs):
        return r, [_rms(r)[1] * vecs[0]], []

    return _Epilogue([], [g], [BF16], 0, fn)


def _epi_loss(gf, target):
    def fn(r, rows, vecs):
        rs, xn = _rms(r)
        e = xn * vecs[0] - rows[0]
        dy = e * (1.0 / r.shape[-1])
        return (_rms_dx(dy * vecs[0], xn, rs), [],
                [jnp.sum(dy * xn, axis=0, keepdims=True), jnp.sum(e * e, axis=0, keepdims=True)])

    return _Epilogue([target], [gf], [], 2, fn)


def _epi_norm_bwd(x, g, dres):
    def fn(r, rows, vecs):
        rs, xn = _rms(rows[0])
        return _rms_dx(r * vecs[0], xn, rs) + rows[1], [], [jnp.sum(r * xn, axis=0, keepdims=True)]

    return _Epilogue([x, dres], [g], [], 1, fn)


def _mm_core(name, operands, in_specs, out_spec, out_shape, grid, nk, dims, acc_shape, has_res, side=None, epi=None):
    n_in = 3 if has_res else 2
    n_epi_in = len(epi.rows) + len(epi.vecs) if epi else 0
    n_epi_out = len(epi.row_out_dtypes) + epi.n_sums if epi else 0
    n_side_in = len(side.srcs) if side else 0
    n_side_out = side.n if side else 0

    def body(*refs):
        a_ref, b_ref = refs[0], refs[1]
        r_ref = refs[2] if has_res else None
        epi_in = refs[n_in:n_in + n_epi_in]
        side_in = refs[n_in + n_epi_in:n_in + n_epi_in + n_side_in]
        n0 = n_in + n_epi_in + n_side_in
        o_ref = refs[n0]
        epi_out = refs[n0 + 1:n0 + 1 + n_epi_out]
        side_out = refs[n0 + 1 + n_epi_out:n0 + 1 + n_epi_out + n_side_out]
        rest = refs[n0 + 1 + n_epi_out + n_side_out:]
        acc, sems = (rest[0], rest[1:]) if nk > 1 else (None, rest)
        i, j, k = pl.program_id(0), pl.program_id(1), pl.program_id(2)
        if side:
            @pl.when((i == 0) & (j == 0) & (k == 0))
            def _():
                side.start(side_in, side_out, sems)

        def product():
            if len(b_ref.shape) == 3:
                ns = b_ref.shape[2]
                return sum(lax.dot_general(a_ref[:, p * ns:(p + 1) * ns].astype(BF16), b_ref[p].astype(BF16),
                                           (dims, ((), ())), preferred_element_type=F32)
                           for p in range(b_ref.shape[0]))
            return lax.dot_general(a_ref[...].astype(BF16), b_ref[...].astype(BF16), (dims, ((), ())),
                                   preferred_element_type=F32)

        def finish(r):
            if has_res:
                r = r + r_ref[...]
            if epi is None:
                o_ref[...] = r.astype(o_ref.dtype)
                return
            n_rows = len(epi.rows)
            main, row_vals, sums = epi.fn(r, [t[...] for t in epi_in[:n_rows]], [t[...] for t in epi_in[n_rows:]])
            o_ref[...] = main.astype(o_ref.dtype)
            for ref, val in zip(epi_out, row_vals):
                ref[...] = val.astype(ref.dtype)
            for ref, val in zip(epi_out[len(row_vals):], sums):
                @pl.when(i == 0)
                def _(ref=ref):
                    ref[...] = jnp.zeros_like(ref)

                ref[...] += val

        if nk == 1:
            finish(product())
        else:
            @pl.when(k == 0)
            def _():
                acc[...] = jnp.zeros_like(acc)

            acc[...] += product()

            @pl.when(k == nk - 1)
            def _():
                finish(acc[...])

        if side:
            @pl.when((i == grid[0] - 1) & (j == grid[1] - 1) & (k == grid[2] - 1))
            def _():
                side.wait(side_in, side_out, sems)

    acc_scratch = [pltpu.VMEM(acc_shape, F32)] if nk > 1 else []
    in_specs, out_specs, out_shapes, operands = list(in_specs), [out_spec], [out_shape], list(operands)
    if epi:
        assert grid[1] == 1, "an epilogue needs tiles that span whole rows"
        tm, n = out_spec.block_shape
        row_spec = pl.BlockSpec((tm, n), lambda i, j, k: (i, 0))
        vec_spec = pl.BlockSpec((1, n), lambda i, j, k: (0, 0))
        in_specs += [row_spec] * len(epi.rows) + [vec_spec] * len(epi.vecs)
        operands += epi.rows + epi.vecs
        out_specs += [row_spec] * len(epi.row_out_dtypes) + [vec_spec] * epi.n_sums
        out_shapes += [jax.ShapeDtypeStruct(out_shape.shape, d) for d in epi.row_out_dtypes]
        out_shapes += [jax.ShapeDtypeStruct((1, n), F32)] * epi.n_sums
    scratch = acc_scratch
    if side:
        in_specs += side.in_specs
        operands += side.srcs
        out_specs += side.out_specs
        out_shapes += side.landing
        scratch = acc_scratch + side.scratch
    plain = side is None and epi is None
    res = pl.pallas_call(
        body, name=name, grid=grid, in_specs=in_specs, out_specs=out_specs[0] if plain else tuple(out_specs),
        out_shape=out_shapes[0] if plain else tuple(out_shapes), scratch_shapes=scratch,
        compiler_params=_cp("parallel", "parallel", "arbitrary") if plain else _cp("arbitrary", "arbitrary", "arbitrary"),
    )(*operands)
    return res


def _mm_nn(name, a, b, out_dtype, residual=None, epi=None):
    m, kk = a.shape
    n = b.shape[1]
    tm, tn, tk = _pick(m, (1024, 512, 256)), _pick(n, (1024, 512)), _pick(kk, (2048, 1024, 512))
    ops = [a, b]
    specs = [pl.BlockSpec((tm, tk), lambda i, j, k: (i, k)), pl.BlockSpec((tk, tn), lambda i, j, k: (k, j))]
    if residual is not None:
        ops.append(residual)
        specs.append(pl.BlockSpec((tm, tn), lambda i, j, k: (i, j)))
    return _mm_core(name, ops, specs, pl.BlockSpec((tm, tn), lambda i, j, k: (i, j)),
                    jax.ShapeDtypeStruct((m, n), out_dtype), (m // tm, n // tn, kk // tk), kk // tk,
                    ((1,), (0,)), (tm, tn), residual is not None, epi=epi)


def _mm_nt(name, a, b, out_dtype, residual=None, epi=None):
    m, kk = a.shape
    n = b.shape[0]
    tm, tn, tk = _pick(m, (1024, 512, 256)), _pick(n, (2048, 1024, 512)), _pick(kk, (2048, 1024, 512))
    ops = [a, b]
    specs = [pl.BlockSpec((tm, tk), lambda i, j, k: (i, k)), pl.BlockSpec((tn, tk), lambda i, j, k: (j, k))]
    if residual is not None:
        ops.append(residual)
        specs.append(pl.BlockSpec((tm, tn), lambda i, j, k: (i, j)))
    return _mm_core(name, ops, specs, pl.BlockSpec((tm, tn), lambda i, j, k: (i, j)),
                    jax.ShapeDtypeStruct((m, n), out_dtype), (m // tm, n // tn, kk // tk), kk // tk,
                    ((1,), (1,)), (tm, tn), residual is not None, epi=epi)


def _mm_tn(name, a, b, out_dtype):
    kk, m = a.shape
    n = b.shape[1]
    tm, tn, tk = _pick(m, (1024, 512)), _pick(n, (1024, 512)), _pick(kk, (2048, 1024, 512, 256))
    specs = [pl.BlockSpec((tk, tm), lambda i, j, k: (k, i)), pl.BlockSpec((tk, tn), lambda i, j, k: (k, j))]
    return _mm_core(name, [a, b], specs, pl.BlockSpec((tm, tn), lambda i, j, k: (i, j)),
                    jax.ShapeDtypeStruct((m, n), out_dtype), (m // tm, n // tn, kk // tk), kk // tk,
                    ((0,), (0,)), (tm, tn), False)


def _mm_nn_slots(name, a, b_slots, out_dtype, side=None):
    m, kk = a.shape
    s, _, ns = b_slots.shape
    tm, tk = _pick(m, (4096, 2048, 1024, 512, 256)), _pick(kk, (1024, 512))
    specs = [pl.BlockSpec((tm, tk), lambda i, j, k: (i, k)), pl.BlockSpec((None, tk, ns), lambda i, j, k: (j, k, 0))]
    return _mm_core(name, [a, b_slots], specs, pl.BlockSpec((tm, ns), lambda i, j, k: (i, j)),
                    jax.ShapeDtypeStruct((m, s * ns), out_dtype), (m // tm, s, kk // tk), kk // tk,
                    ((1,), (0,)), (tm, ns), False, side)


def _mm_nt_slots(name, a, b_slots, out_dtype, side=None, epi=None):
    m = a.shape[0]
    s, n, ns = b_slots.shape
    tm, tn = _pick(m, (1024, 512, 256)), _pick(n, (1024, 512))
    per = _pick(s, (2, 1))
    specs = [pl.BlockSpec((tm, per * ns), lambda i, j, k: (i, k)),
             pl.BlockSpec((per, tn, ns), lambda i, j, k: (k, j, 0))]
    return _mm_core(name, [a, b_slots], specs, pl.BlockSpec((tm, tn), lambda i, j, k: (i, j)),
                    jax.ShapeDtypeStruct((m, n), out_dtype), (m // tm, n // tn, s // per), s // per,
                    ((1,), (1,)), (tm, tn), False, side, epi)


def _mm_tn_slots(name, a, b, s, out_dtype, side=None):
    kk, m = a.shape
    ns = b.shape[1] // s
    tm, tk = _pick(m, (1024, 512)), _pick(kk, (4096, 2048, 1024, 512, 256))
    specs = [pl.BlockSpec((tk, tm), lambda i, j, k: (k, i)), pl.BlockSpec((tk, ns), lambda i, j, k: (k, j))]
    return _mm_core(name, [a, b], specs, pl.BlockSpec((None, tm, ns), lambda i, j, k: (j, i, 0)),
                    jax.ShapeDtypeStruct((s, m, ns), out_dtype), (m // tm, s, kk // tk), kk // tk,
                    ((0,), (0,)), (tm, ns), False, side)


def _rms_fwd(name, x, g):
    r, d = x.shape
    tr = _pick(r, (1024, 512, 256))

    def body(x_ref, g_ref, o_ref):
        xv = x_ref[...]
        rs = lax.rsqrt(jnp.mean(xv * xv, axis=-1, keepdims=True) + EPS)
        o_ref[...] = (xv * rs * g_ref[...]).astype(o_ref.dtype)

    return pl.pallas_call(
        body, name=name, grid=(r // tr,),
        in_specs=[pl.BlockSpec((tr, d), lambda i: (i, 0)), pl.BlockSpec((1, d), lambda i: (0, 0))],
        out_specs=pl.BlockSpec((tr, d), lambda i: (i, 0)),
        out_shape=jax.ShapeDtypeStruct((r, d), BF16), compiler_params=_cp("parallel"),
    )(x, g)


def _rms_bwd(name, x, g, dh, dres):
    r, d = x.shape
    tr = _pick(r, (512, 256))
    has_res = dres is not None

    def body(*refs):
        if has_res:
            x_ref, g_ref, dh_ref, dr_ref, dx_ref, dg_ref = refs
        else:
            x_ref, g_ref, dh_ref, dx_ref, dg_ref = refs
        i = pl.program_id(0)

        @pl.when(i == 0)
        def _():
            dg_ref[...] = jnp.zeros_like(dg_ref)

        xv = x_ref[...]
        dhv = dh_ref[...].astype(F32)
        rs = lax.rsqrt(jnp.mean(xv * xv, axis=-1, keepdims=True) + EPS)
        xn = xv * rs
        dg_ref[...] += jnp.sum(dhv * xn, axis=0, keepdims=True)
        dn = dhv * g_ref[...]
        dx = rs * (dn - xn * jnp.mean(dn * xn, axis=-1, keepdims=True))
        if has_res:
            dx = dx + dr_ref[...]
        dx_ref[...] = dx

    row = pl.BlockSpec((tr, d), lambda i: (i, 0))
    vec = pl.BlockSpec((1, d), lambda i: (0, 0))
    ops = [x, g, dh] + ([dres] if has_res else [])
    return pl.pallas_call(
        body, name=name, grid=(r // tr,),
        in_specs=[row, vec, row] + ([row] if has_res else []),
        out_specs=(row, vec),
        out_shape=(jax.ShapeDtypeStruct((r, d), F32), jax.ShapeDtypeStruct((1, d), F32)),
        compiler_params=_cp("arbitrary"),
    )(*ops)


def _rot(x, cos_t, sin_t):
    n = x.shape[-1]
    lane = lax.broadcasted_iota(jnp.int32, x.shape, 1)
    partner = jnp.where((lane % RET_DK) < RET_DK // 2, pltpu.roll(x, n - RET_DK // 2, 1), pltpu.roll(x, RET_DK // 2, 1))
    return x * cos_t + partner * sin_t


def _ret_constants(c):
    log_g = jnp.log1p(-jnp.exp2(-5.0 - jnp.arange(RET_HEADS, dtype=F32)))
    j = jnp.arange(c, dtype=F32)
    diff = j[:, None] - j[None, :]
    decay = jnp.where(diff[None] >= 0.0, jnp.exp(log_g[:, None, None] * jnp.maximum(diff, 0.0)[None]), 0.0)
    q_w = jnp.exp(log_g[None, :] * (j + 1.0)[:, None])
    k_w = jnp.exp(log_g[None, :] * (c - 1.0 - j)[:, None])
    cd = jnp.exp(log_g * c)
    rep = lambda t: jnp.repeat(t, RET_DK, axis=1)
    cd_row = jnp.repeat(cd, RET_DV)[None, :]
    return decay, rep(q_w), rep(k_w), cd_row


def _pair_of(h, c):
    lane = lax.broadcasted_iota(jnp.int32, (c, 2 * RET_DK), 1)
    mine = (lane < RET_DK) if h % 2 == 0 else (lane >= RET_DK)
    return slice((h // 2) * 2 * RET_DK, (h // 2 + 1) * 2 * RET_DK), mine


def _keep(x, mine):
    return jnp.where(mine, x, jnp.zeros_like(x))


def _ret_fwd(proj, cos_t, sin_t, consts, gn_g, c):
    l = proj.shape[0]
    nc = l // c
    decay, qw, kw, cd_row = consts

    def body(q_ref, k_ref, v_ref, g_ref, cos_ref, sin_ref, dec_ref, qw_ref, kw_ref, cd_ref, gn_ref,
             ret_ref, o_ref, rp_ref, qb_ref, kb_ref, state):
        @pl.when(pl.program_id(0) == 0)
        def _():
            state[...] = jnp.zeros_like(state)

        cs, sn = cos_ref[...], sin_ref[...]
        qr = _rot(q_ref[...].astype(F32), cs, sn)
        kr = _rot(k_ref[...].astype(F32), cs, sn) * (RET_DK ** -0.5)
        qb, kb = qr.astype(BF16), kr.astype(BF16)
        qb_ref[...] = qb
        kb_ref[...] = kb
        qwb = (qr * qw_ref[...]).astype(BF16)
        kwb = (kr * kw_ref[...]).astype(BF16)
        vb = v_ref[...].astype(BF16)
        for h in range(RET_HEADS):
            ps, mine = _pair_of(h, c)
            vs = slice(h * RET_DV, (h + 1) * RET_DV)
            s = _dot_nt(_keep(qb[:, ps], mine), kb[:, ps]) * dec_ref[h]
            r_prev = state[h]
            rp_ref[0, h] = r_prev
            o = _dot(s.astype(BF16), vb[:, vs]) + _dot(_keep(qwb[:, ps], mine), r_prev.astype(BF16))
            state[h] = cd_ref[:, vs] * r_prev + _dot_tn(_keep(kwb[:, ps], mine), vb[:, vs])
            o_ref[:, vs] = o
            mu = jnp.mean(o, axis=-1, keepdims=True)
            var = jnp.mean(jnp.square(o - mu), axis=-1, keepdims=True)
            on = (o - mu) * lax.rsqrt(var + EPS)
            ret_ref[:, vs] = (on * gn_ref[:, vs] * _silu(g_ref[:, vs].astype(F32))).astype(ret_ref.dtype)

    const2 = lambda shape: pl.BlockSpec(shape, lambda i: (0,) * len(shape))
    return pl.pallas_call(
        body, name="retention_fwd", grid=(nc,),
        in_specs=[pl.BlockSpec((c, RET_QK), lambda i: (i, 0)), pl.BlockSpec((c, RET_QK), lambda i: (i, 1)),
                  pl.BlockSpec((c, D_MODEL), lambda i: (i, 1)), pl.BlockSpec((c, D_MODEL), lambda i: (i, 2)),
                  pl.BlockSpec((c, RET_QK), lambda i: (i, 0)), pl.BlockSpec((c, RET_QK), lambda i: (i, 0)),
                  const2((RET_HEADS, c, c)), const2((c, RET_QK)), const2((c, RET_QK)), const2((1, D_MODEL)),
                  const2((1, D_MODEL))],
        out_specs=(pl.BlockSpec((c, D_MODEL), lambda i: (i, 0)), pl.BlockSpec((c, D_MODEL), lambda i: (i, 0)),
                   pl.BlockSpec((1, RET_HEADS, 2 * RET_DK, RET_DV), lambda i: (i, 0, 0, 0)),
                   pl.BlockSpec((c, RET_QK), lambda i: (i, 0)), pl.BlockSpec((c, RET_QK), lambda i: (i, 0))),
        out_shape=(jax.ShapeDtypeStruct((l, 2 * D_MODEL), BF16), jax.ShapeDtypeStruct((l, D_MODEL), F32),
                   jax.ShapeDtypeStruct((nc, RET_HEADS, 2 * RET_DK, RET_DV), F32),
                   jax.ShapeDtypeStruct((l, RET_QK), BF16), jax.ShapeDtypeStruct((l, RET_QK), BF16)),
        scratch_shapes=[pltpu.VMEM((RET_HEADS, 2 * RET_DK, RET_DV), F32)],
        compiler_params=_cp("arbitrary"),
    )(proj, proj, proj, proj, cos_t, sin_t, decay, qw, kw, cd_row, gn_g)


def _ret_bwd(proj, qb_saved, kb_saved, cos_t, sin_t, consts, gn_g, o_saved, r_prev_saved, dmix, c, side):
    l = proj.shape[0]
    nc = l // c
    decay, qw, kw, cd_row = consts
    n_in = 14

    def body(*refs):
        (q_ref, k_ref, v_ref, g_ref, cos_ref, sin_ref, dec_ref, qw_ref, kw_ref, cd_ref, gn_ref, o_ref, rp_ref,
         dr_ref) = refs[:n_in]
        side_in = refs[n_in:n_in + len(side.srcs)]
        out_ref, dgn_ref = refs[n_in + len(side.srcs):n_in + len(side.srcs) + 2]
        side_out = refs[n_in + len(side.srcs) + 2:n_in + len(side.srcs) + 2 + side.n]
        state, dq_s, dk_s = refs[n_in + len(side.srcs) + 2 + side.n:n_in + len(side.srcs) + 5 + side.n]
        sems = refs[n_in + len(side.srcs) + 5 + side.n:]

        @pl.when(pl.program_id(0) == 0)
        def _():
            side.start(side_in, side_out, sems)
            state[...] = jnp.zeros_like(state)
            dgn_ref[...] = jnp.zeros_like(dgn_ref)

        cs, sn = cos_ref[...], sin_ref[...]
        qb, kb = q_ref[...], k_ref[...]
        qwv, kwv = qw_ref[...], kw_ref[...]
        qwb = (qb.astype(F32) * qwv).astype(BF16)
        kwb = (kb.astype(F32) * kwv).astype(BF16)
        vb = v_ref[...].astype(BF16)
        dq2 = dk2 = None
        for h in range(RET_HEADS):
            ps, mine = _pair_of(h, c)
            vs = slice(h * RET_DV, (h + 1) * RET_DV)
            dec = dec_ref[h]
            qm, km = _keep(qb[:, ps], mine), _keep(kb[:, ps], mine)
            o = o_ref[:, vs]
            mu = jnp.mean(o, axis=-1, keepdims=True)
            var = jnp.mean(jnp.square(o - mu), axis=-1, keepdims=True)
            rstd = lax.rsqrt(var + EPS)
            on = (o - mu) * rstd
            gate = g_ref[:, vs].astype(F32)
            sg = _silu(gate)
            dret = dr_ref[:, vs].astype(F32)
            gn = gn_ref[:, vs]
            dgn_ref[:, vs] += jnp.sum(dret * on * sg, axis=0, keepdims=True)
            out_ref[:, 2 * RET_QK + D_MODEL + h * RET_DV:2 * RET_QK + D_MODEL + (h + 1) * RET_DV] = (
                dret * on * gn * _dsilu(gate)).astype(out_ref.dtype)
            don = dret * gn * sg
            do = rstd * (don - jnp.mean(don, axis=-1, keepdims=True)
                         - on * jnp.mean(don * on, axis=-1, keepdims=True))
            dob = do.astype(BF16)
            sn_h = state[h]
            snb = sn_h.astype(BF16)
            s = _dot_nt(qm, kb[:, ps]) * dec
            dv = _dot_tn(s.astype(BF16), dob) + _dot(_keep(kwb[:, ps], mine), snb)
            out_ref[:, 2 * RET_QK + h * RET_DV:2 * RET_QK + (h + 1) * RET_DV] = dv.astype(out_ref.dtype)
            ds = (_dot_nt(dob, vb[:, vs]) * dec).astype(BF16)
            dq_h = _dot(ds, km) + qwv[:, ps] * _dot_nt(dob, rp_ref[0, h].astype(BF16))
            dk_h = _dot_tn(ds, qm) + kwv[:, ps] * _dot_nt(vb[:, vs], snb)
            state[h] = cd_ref[:, vs] * sn_h + _dot_tn(_keep(qwb[:, ps], mine), dob)
            if h % 2 == 0:
                dq2, dk2 = dq_h, dk_h
            else:
                dq_s[:, ps] = dq2 + dq_h
                dk_s[:, ps] = dk2 + dk_h
        out_ref[:, 0:RET_QK] = _rot(dq_s[...], cs, -sn).astype(out_ref.dtype)
        out_ref[:, RET_QK:2 * RET_QK] = (_rot(dk_s[...], cs, -sn) * (RET_DK ** -0.5)).astype(out_ref.dtype)

        @pl.when(pl.program_id(0) == nc - 1)
        def _():
            side.wait(side_in, side_out, sems)

    rev = lambda i: nc - 1 - i
    const2 = lambda shape: pl.BlockSpec(shape, lambda i: (0,) * len(shape))
    return pl.pallas_call(
        body, name="retention_bwd", grid=(nc,),
        in_specs=[pl.BlockSpec((c, RET_QK), lambda i: (rev(i), 0)), pl.BlockSpec((c, RET_QK), lambda i: (rev(i), 0)),
                  pl.BlockSpec((c, D_MODEL), lambda i: (rev(i), 1)), pl.BlockSpec((c, D_MODEL), lambda i: (rev(i), 2)),
                  pl.BlockSpec((c, RET_QK), lambda i: (rev(i), 0)), pl.BlockSpec((c, RET_QK), lambda i: (rev(i), 0)),
                  const2((RET_HEADS, c, c)), const2((c, RET_QK)), const2((c, RET_QK)), const2((1, D_MODEL)),
                  const2((1, D_MODEL)),
                  pl.BlockSpec((c, D_MODEL), lambda i: (rev(i), 0)),
                  pl.BlockSpec((1, RET_HEADS, 2 * RET_DK, RET_DV), lambda i: (rev(i), 0, 0, 0)),
                  pl.BlockSpec((c, D_MODEL), lambda i: (rev(i), 0))] + side.in_specs,
        out_specs=(pl.BlockSpec((c, 2 * RET_QK + 2 * D_MODEL), lambda i: (rev(i), 0)), const2((1, D_MODEL)),
                   *side.out_specs),
        out_shape=(jax.ShapeDtypeStruct((l, 2 * RET_QK + 4 * D_MODEL), BF16), jax.ShapeDtypeStruct((1, D_MODEL), F32),
                   *side.landing),
        scratch_shapes=[pltpu.VMEM((RET_HEADS, 2 * RET_DK, RET_DV), F32), pltpu.VMEM((c, RET_QK), F32),
                        pltpu.VMEM((c, RET_QK), F32)] + side.scratch,
        compiler_params=_cp("arbitrary"),
    )(qb_saved, kb_saved, proj, proj, cos_t, sin_t, decay, qw, kw, cd_row, gn_g, o_saved, r_prev_saved, dmix,
      *side.srcs)


def _zoh(a_re, a_im, log_dt):
    dt = jnp.exp(log_dt)
    mag = jnp.exp(a_re * dt)
    abar_re = mag * jnp.cos(a_im * dt)
    abar_im = mag * jnp.sin(a_im * dt)
    den = a_re * a_re + a_im * a_im
    nr, ni = abar_re - 1.0, abar_im
    f_re = (nr * a_re + ni * a_im) / den
    f_im = (ni * a_re - nr * a_im) / den
    return dt, abar_re, abar_im, f_re, f_im, den


def _lanes_p(f):
    return jnp.tile(f, (1, S5_P))


def _s5_discretize(a_re, a_im, log_dt, b_re_t, b_im_t):
    def body(ar_ref, ai_ref, ld_ref, br_ref, bi_ref, abr_ref, abi_ref, bbr_ref, bbi_ref):
        _, abar_re, abar_im, f_re, f_im, _ = _zoh(ar_ref[...], ai_ref[...], ld_ref[...])
        abr_ref[...] = abar_re
        abi_ref[...] = abar_im
        fr, fi = _lanes_p(f_re), _lanes_p(f_im)
        bbr_ref[...] = fr * br_ref[...] - fi * bi_ref[...]
        bbi_ref[...] = fr * bi_ref[...] + fi * br_ref[...]

    gn = jax.ShapeDtypeStruct((S5_G, S5_N), F32)
    gpn = jax.ShapeDtypeStruct((S5_G, S5_P * S5_N), F32)
    return pl.pallas_call(body, name="s5_discretize", out_shape=(gn, gn, gpn, gpn))(a_re, a_im, log_dt, b_re_t, b_im_t)


def _s5_discretize_bwd(a_re, a_im, log_dt, b_re_t, b_im_t, dab_re, dab_im, dbb_re_t, dbb_im_t):
    def body(ar_ref, ai_ref, ld_ref, br_ref, bi_ref, gar_ref, gai_ref, gbr_ref, gbi_ref,
             dar_ref, dai_ref, dld_ref, dbr_ref, dbi_ref):
        a_r, a_i = ar_ref[...], ai_ref[...]
        dt, abar_re, abar_im, f_re, f_im, den = _zoh(a_r, a_i, ld_ref[...])
        b_r, b_i, g_br, g_bi = br_ref[...], bi_ref[...], gbr_ref[...], gbi_ref[...]
        fr, fi = _lanes_p(f_re), _lanes_p(f_im)
        dbr_ref[...] = fr * g_br + fi * g_bi
        dbi_ref[...] = fr * g_bi - fi * g_br
        t_r = b_r * g_br + b_i * g_bi
        t_i = b_r * g_bi - b_i * g_br
        gf_r = sum(t_r[:, p * S5_N:(p + 1) * S5_N] for p in range(S5_P))
        gf_i = sum(t_i[:, p * S5_N:(p + 1) * S5_N] for p in range(S5_P))
        inv_r, inv_i = a_r / den, a_i / den
        ga_r = gar_ref[...] + gf_r * inv_r - gf_i * inv_i
        ga_i = gai_ref[...] + gf_r * inv_i + gf_i * inv_r
        q_r = -(f_re * a_r + f_im * a_i) / den
        q_i = -(f_im * a_r - f_re * a_i) / den
        gl_r = q_r * gf_r + q_i * gf_i
        gl_i = q_r * gf_i - q_i * gf_r
        dar_ref[...] = gl_r + dt * (abar_re * ga_r + abar_im * ga_i)
        dai_ref[...] = gl_i + dt * (abar_re * ga_i - abar_im * ga_r)
        la_r = a_r * abar_re - a_i * abar_im
        la_i = a_r * abar_im + a_i * abar_re
        dld_ref[...] = dt * jnp.sum(ga_r * la_r + ga_i * la_i, axis=-1, keepdims=True)

    gn = jax.ShapeDtypeStruct((S5_G, S5_N), F32)
    gpn = jax.ShapeDtypeStruct((S5_G, S5_P * S5_N), F32)
    return pl.pallas_call(
        body, name="s5_discretize_bwd", out_shape=(gn, gn, jax.ShapeDtypeStruct((S5_G, 1), F32), gpn, gpn),
    )(a_re, a_im, log_dt, b_re_t, b_im_t, dab_re, dab_im, dbb_re_t, dbb_im_t)


S5_ZQ = S5_NB // 2


def _s5_z(re, im):
    return jnp.concatenate([re.reshape(S5_ZQ, 8, 128), im.reshape(S5_ZQ, 8, 128)], axis=0)


def _s5_unz(z):
    return z[:S5_ZQ].reshape(S5_G, S5_N), z[S5_ZQ:].reshape(S5_G, S5_N)


def _s5_block_mats(bb_re, bb_im, c_re, c_im):
    eye = jnp.eye(S5_GB, dtype=F32)
    bb = jnp.stack([bb_re, bb_im], axis=0).reshape(2, S5_NB, S5_GB, S5_N, S5_P)
    bbm = jnp.einsum("rbgnp,gh->bgprhn", bb, eye).reshape(S5_NB, S5_GB * S5_P, 2 * S5_BS)
    cc = jnp.stack([c_re, -c_im], axis=0).reshape(2, S5_NB, S5_GB, S5_P, S5_N)
    ccm = jnp.einsum("rbgpn,gh->brhngp", cc, eye).reshape(S5_NB, 2 * S5_BS, S5_GB * S5_P)
    return bbm.astype(BF16), ccm.astype(BF16)


def _s5_block_diag_bb(m):
    t = m.reshape(S5_NB, S5_GB, S5_P, 2, S5_GB, S5_N)
    d = jnp.einsum("bgprgn->rbgnp", t).reshape(2, S5_G, S5_N, S5_P)
    return d[0], d[1]


def _s5_block_diag_cc(m):
    t = m.reshape(S5_NB, 2, S5_GB, S5_N, S5_GB, S5_P)
    d = jnp.einsum("brgngp->rbgpn", t).reshape(2, S5_G, S5_P, S5_N)
    return d[0], -d[1]


SCAN_UNROLL = 8


def _z_store(zr, zi, blk, res, t, off):
    q, h = blk // 2, blk % 2
    for lt in range(4):
        zr[q, pl.ds(off + 4 * h + lt, t, stride=8), :] = res[:, lt * 128:(lt + 1) * 128]
        zi[q, pl.ds(off + 4 * h + lt, t, stride=8), :] = res[:, S5_BS + lt * 128:S5_BS + (lt + 1) * 128]


def _z_load(zr, zi, blk, t, off):
    q, h = blk // 2, blk % 2
    return jnp.concatenate([zr[q, pl.ds(off + 4 * h + lt, t, stride=8), :] for lt in range(4)]
                           + [zi[q, pl.ds(off + 4 * h + lt, t, stride=8), :] for lt in range(4)], axis=1)


def _z_scan_fwd(zr, zi, a_ref, carry_ref, t, off):
    ar = [a_ref[q] for q in range(S5_ZQ)]
    ai = [a_ref[S5_ZQ + q] for q in range(S5_ZQ)]

    def step(it, carry):
        carry = list(carry)
        base = pl.multiple_of(it * (8 * SCAN_UNROLL), 8 * SCAN_UNROLL) + off
        for tt in range(SCAN_UNROLL):
            rows = pl.ds(base + 8 * tt, 8)
            for q in range(S5_ZQ):
                c_r, c_i = carry[q], carry[S5_ZQ + q]
                n_r = ar[q] * c_r - ai[q] * c_i + zr[q, rows, :]
                n_i = ar[q] * c_i + ai[q] * c_r + zi[q, rows, :]
                zr[q, rows, :] = n_r
                zi[q, rows, :] = n_i
                carry[q], carry[S5_ZQ + q] = n_r, n_i
        return tuple(carry)

    out = lax.fori_loop(0, t // SCAN_UNROLL, step, tuple(carry_ref[k] for k in range(2 * S5_ZQ)))
    for k in range(2 * S5_ZQ):
        carry_ref[k] = out[k]


def _z_scan_bwd(lr, li, xr, xi, a_ref, carry_ref, acc_ref, t):
    ar = [a_ref[q] for q in range(S5_ZQ)]
    ai = [a_ref[S5_ZQ + q] for q in range(S5_ZQ)]
    n_it = t // SCAN_UNROLL

    def step(it, state):
        carry, acc = list(state[0]), list(state[1])
        base = pl.multiple_of((n_it - 1 - it) * (8 * SCAN_UNROLL), 8 * SCAN_UNROLL)
        for tt in reversed(range(SCAN_UNROLL)):
            rows = pl.ds(base + 8 * tt, 8)
            for q in range(S5_ZQ):
                c_r, c_i = carry[q], carry[S5_ZQ + q]
                n_r = ar[q] * c_r + ai[q] * c_i + lr[q, rows, :]
                n_i = ar[q] * c_i - ai[q] * c_r + li[q, rows, :]
                lr[q, rows, :] = n_r
                li[q, rows, :] = n_i
                p_r, p_i = xr[q, rows, :], xi[q, rows, :]
                acc[q] = acc[q] + n_r * p_r + n_i * p_i
                acc[S5_ZQ + q] = acc[S5_ZQ + q] + n_i * p_r - n_r * p_i
                carry[q], carry[S5_ZQ + q] = n_r, n_i
        return tuple(carry), tuple(acc)

    k8 = range(2 * S5_ZQ)
    carry, acc = lax.fori_loop(0, n_it, step, (tuple(carry_ref[k] for k in k8), tuple(acc_ref[k] for k in k8)))
    for k in k8:
        carry_ref[k] = carry[k]
        acc_ref[k] = acc[k]


def _s5_fwd(proj, mix, bbm, ccm, d_row, glu_w, glu_b, tabs, t, side):
    l = proj.shape[0]
    nt = l // t
    n_in = 9

    def body(*refs):
        u_ref, gs_ref, bb_ref, cc_ref, d_ref, gw_ref, gb_ref, a_ref, _ = refs[:n_in]
        side_in = refs[n_in:n_in + len(side.srcs)]
        ssm_ref, xst_ref = refs[n_in + len(side.srcs):n_in + len(side.srcs) + 2]
        side_out = refs[n_in + len(side.srcs) + 2:n_in + len(side.srcs) + 2 + side.n]
        zr, zi, carry = refs[n_in + len(side.srcs) + 2 + side.n:n_in + len(side.srcs) + 5 + side.n]
        sems = refs[n_in + len(side.srcs) + 5 + side.n:]

        @pl.when(pl.program_id(0) == 0)
        def _():
            side.start(side_in, side_out, sems)
            carry[...] = jnp.zeros_like(carry)

        xst_ref[0] = carry[...]
        ub = u_ref[...]
        u = ub.astype(F32)
        for blk in range(S5_NB):
            _z_store(zr, zi, blk, _dot(ub[:, blk * 128:(blk + 1) * 128], bb_ref[blk]), t, 0)
        _z_scan_fwd(zr, zi, a_ref, carry, t, 0)
        ys = jnp.concatenate(
            [_dot(_z_load(zr, zi, blk, t, 0).astype(BF16), cc_ref[blk]) for blk in range(S5_NB)], axis=1)
        y2 = _gelu(ys + d_ref[...] * u)
        z = _dot(y2.astype(BF16), gw_ref[...]) + gb_ref[...]
        ssm_ref[...] = (y2 * _sigmoid(z) * _silu(gs_ref[...].astype(F32))).astype(ssm_ref.dtype)

        @pl.when(pl.program_id(0) == nt - 1)
        def _():
            side.wait(side_in, side_out, sems)

    const2 = lambda shape: pl.BlockSpec(shape, lambda i: (0,) * len(shape))
    zshape = (2 * S5_ZQ, 8, 128)
    return pl.pallas_call(
        body, name="s5_fwd", grid=(nt,),
        in_specs=[pl.BlockSpec((t, D_MODEL), lambda i: (i, 3)), pl.BlockSpec((t, D_MODEL), lambda i: (i, 4)),
                  const2(bbm.shape), const2(ccm.shape), const2((1, D_MODEL)), const2((D_MODEL, D_MODEL)),
                  const2((1, D_MODEL)), const2(zshape), pl.BlockSpec(memory_space=pl.ANY)] + side.in_specs,
        out_specs=(pl.BlockSpec((t, D_MODEL), lambda i: (i, 1)), pl.BlockSpec((1,) + zshape, lambda i: (i, 0, 0, 0)),
                   *side.out_specs),
        out_shape=(jax.ShapeDtypeStruct((l, 2 * D_MODEL), BF16), jax.ShapeDtypeStruct((nt,) + zshape, F32),
                   *side.landing),
        scratch_shapes=[pltpu.VMEM((S5_ZQ, 8 * t, 128), F32), pltpu.VMEM((S5_ZQ, 8 * t, 128), F32),
                        pltpu.VMEM(zshape, F32)] + side.scratch,
        input_output_aliases={8: 0},
        compiler_params=_cp("arbitrary"),
    )(proj, proj, bbm, ccm, d_row, glu_w, glu_b, tabs, mix, *side.srcs)


def _s5_bwd(proj, dmix, dproj, xstart, bbm, ccm, d_row, glu_w, glu_b, tabs, t):
    l = proj.shape[0]
    nt = l // t
    col0 = 2 * RET_QK + 2 * D_MODEL

    def body(u_ref, gs_ref, dm_ref, xst_ref, bb_ref, cc_ref, d_ref, gw_ref, gb_ref, a_ref, _,
             dp_ref, y2_ref, dz_ref, dbb_ref, dcc_ref, da_ref, dd_ref, dgb_ref, xr, xi, lr, li, carry, lcarry,
             dug_s, dug_sem):
        step = pl.program_id(0)
        slot = step % 2
        dug_ref = dug_s.at[slot]

        def put(s, at_step):
            rows = pl.ds(pl.multiple_of((nt - 1 - at_step) * t, t), t)
            return pltpu.make_async_copy(dug_s.at[s], dp_ref.at[rows, pl.ds(col0, 2 * D_MODEL)], dug_sem.at[s])

        @pl.when(step >= 2)
        def _():
            put(slot, step - 2).wait()

        @pl.when(step == 0)
        def _():
            lcarry[...] = jnp.zeros_like(lcarry)
            dbb_ref[...] = jnp.zeros_like(dbb_ref)
            dcc_ref[...] = jnp.zeros_like(dcc_ref)
            da_ref[...] = jnp.zeros_like(da_ref)
            dd_ref[...] = jnp.zeros_like(dd_ref)
            dgb_ref[...] = jnp.zeros_like(dgb_ref)

        carry[...] = xst_ref[0]
        for q in range(S5_ZQ):
            xr[q, 0:8, :] = carry[q]
            xi[q, 0:8, :] = carry[S5_ZQ + q]
        ub = u_ref[...]
        u = ub.astype(F32)
        for blk in range(S5_NB):
            _z_store(xr, xi, blk, _dot(ub[:, blk * 128:(blk + 1) * 128], bb_ref[blk]), t, 8)
        _z_scan_fwd(xr, xi, a_ref, carry, t, 8)
        ys = jnp.concatenate(
            [_dot(_z_load(xr, xi, blk, t, 8).astype(BF16), cc_ref[blk]) for blk in range(S5_NB)], axis=1)
        dv = d_ref[...]
        y1 = ys + dv * u
        y2, dgelu = _gelu_and_grad(y1)
        y2b = y2.astype(BF16)
        sg = _sigmoid(_dot(y2b, gw_ref[...]) + gb_ref[...])
        gs = gs_ref[...].astype(F32)
        dssm = dm_ref[...].astype(F32)
        dug_ref[:, D_MODEL:] = (dssm * (y2 * sg) * _dsilu(gs)).astype(dug_ref.dtype)
        dy3 = dssm * _silu(gs)
        dz = dy3 * y2 * sg * (1.0 - sg)
        dzb = dz.astype(BF16)
        y2_ref[...] = y2b
        dz_ref[...] = dzb
        dgb_ref[...] += jnp.sum(dz, axis=0, keepdims=True)
        dy1 = (dy3 * sg + _dot_nt(dzb, gw_ref[...])) * dgelu
        dd_ref[...] += jnp.sum(dy1 * u, axis=0, keepdims=True)
        dyb = dy1.astype(BF16)
        for blk in range(S5_NB):
            ch = slice(blk * 128, (blk + 1) * 128)
            _z_store(lr, li, blk, _dot_nt(dyb[:, ch], cc_ref[blk]), t, 0)
            dcc_ref[blk] += _dot_tn(_z_load(xr, xi, blk, t, 8).astype(BF16), dyb[:, ch])
        _z_scan_bwd(lr, li, xr, xi, a_ref, lcarry, da_ref, t)
        du = []
        for blk in range(S5_NB):
            lb = _z_load(lr, li, blk, t, 0).astype(BF16)
            du.append(_dot_nt(lb, bb_ref[blk]))
            dbb_ref[blk] += _dot_tn(ub[:, blk * 128:(blk + 1) * 128], lb)
        dug_ref[:, :D_MODEL] = (jnp.concatenate(du, axis=1) + dy1 * dv).astype(dug_ref.dtype)
        put(slot, step).start()

        @pl.when(step == nt - 1)
        def _():
            put(slot, step).wait()
            if nt > 1:
                put(1 - slot, step - 1).wait()

    rev = lambda i: nt - 1 - i
    const2 = lambda shape: pl.BlockSpec(shape, lambda i: (0,) * len(shape))
    row_out = lambda w: pl.BlockSpec((t, w), lambda i: (rev(i), 0))
    zshape = (2 * S5_ZQ, 8, 128)
    hbm = pl.BlockSpec(memory_space=pl.ANY)
    return pl.pallas_call(
        body, name="s5_bwd", grid=(nt,),
        in_specs=[pl.BlockSpec((t, D_MODEL), lambda i: (rev(i), 3)), pl.BlockSpec((t, D_MODEL), lambda i: (rev(i), 4)),
                  pl.BlockSpec((t, D_MODEL), lambda i: (rev(i), 1)),
                  pl.BlockSpec((1,) + zshape, lambda i: (rev(i), 0, 0, 0)),
                  const2(bbm.shape), const2(ccm.shape), const2((1, D_MODEL)), const2((D_MODEL, D_MODEL)),
                  const2((1, D_MODEL)), const2(zshape), hbm],
        out_specs=(hbm, row_out(D_MODEL), row_out(D_MODEL), const2(bbm.shape), const2(ccm.shape),
                   const2(zshape), const2((1, D_MODEL)), const2((1, D_MODEL))),
        out_shape=(jax.ShapeDtypeStruct(dproj.shape, BF16), jax.ShapeDtypeStruct((l, D_MODEL), BF16),
                   jax.ShapeDtypeStruct((l, D_MODEL), BF16), jax.ShapeDtypeStruct(bbm.shape, F32),
                   jax.ShapeDtypeStruct(ccm.shape, F32), jax.ShapeDtypeStruct(zshape, F32),
                   jax.ShapeDtypeStruct((1, D_MODEL), F32), jax.ShapeDtypeStruct((1, D_MODEL), F32)),
        scratch_shapes=[pltpu.VMEM((S5_ZQ, 8 * t + 8, 128), F32), pltpu.VMEM((S5_ZQ, 8 * t + 8, 128), F32),
                        pltpu.VMEM((S5_ZQ, 8 * t, 128), F32), pltpu.VMEM((S5_ZQ, 8 * t, 128), F32),
                        pltpu.VMEM(zshape, F32), pltpu.VMEM(zshape, F32),
                        pltpu.VMEM((2, t, 2 * D_MODEL), BF16), pltpu.SemaphoreType.DMA((2,))],
        input_output_aliases={10: 0},
        compiler_params=_cp("arbitrary"),
    )(proj, proj, dmix, xstart, bbm, ccm, d_row, glu_w, glu_b, tabs, dproj)


def _attn_probs(qh, kh):
    s = _dot_nt(qh, kh) * (XA_DH ** -0.5)
    e = jnp.exp(s - jnp.max(s, axis=-1, keepdims=True))
    return e / jnp.sum(e, axis=-1, keepdims=True)


def _attn_fwd(qa, ka, va):
    l = qa.shape[0]
    m = ka.shape[0]
    tl = _pick(l, (2048, 1024, 512, 256))

    def body(q_ref, k_ref, v_ref, o_ref):
        for h in range(XA_HEADS):
            hs = slice(h * XA_DH, (h + 1) * XA_DH)
            p = _attn_probs(q_ref[:, hs], k_ref[:, hs])
            o_ref[:, hs] = _dot(p.astype(BF16), v_ref[:, hs]).astype(o_ref.dtype)

    return pl.pallas_call(
        body, name="xattn_fwd", grid=(l // tl,),
        in_specs=[pl.BlockSpec((tl, D_MODEL), lambda i: (i, 0)), pl.BlockSpec((m, D_MODEL), lambda i: (0, 0)),
                  pl.BlockSpec((m, D_MODEL), lambda i: (0, 0))],
        out_specs=pl.BlockSpec((tl, D_MODEL), lambda i: (i, 0)),
        out_shape=jax.ShapeDtypeStruct((l, D_MODEL), BF16), compiler_params=_cp("parallel"),
    )(qa, ka, va)


def _attn_bwd(qa, ka, va, doa):
    l = qa.shape[0]
    m = ka.shape[0]
    tl = _pick(l, (2048, 1024, 512, 256))

    def body(q_ref, k_ref, v_ref, do_ref, dq_ref, dk_ref, dv_ref):
        @pl.when(pl.program_id(0) == 0)
        def _():
            dk_ref[...] = jnp.zeros_like(dk_ref)
            dv_ref[...] = jnp.zeros_like(dv_ref)

        for h in range(XA_HEADS):
            hs = slice(h * XA_DH, (h + 1) * XA_DH)
            qh, kh, vh, doh = q_ref[:, hs], k_ref[:, hs], v_ref[:, hs], do_ref[:, hs]
            p = _attn_probs(qh, kh)
            dv_ref[:, hs] += _dot_tn(p.astype(BF16), doh)
            dp = _dot_nt(doh, vh)
            ds = (p * (dp - jnp.sum(dp * p, axis=-1, keepdims=True)) * (XA_DH ** -0.5)).astype(BF16)
            dq_ref[:, hs] = _dot(ds, kh).astype(dq_ref.dtype)
            dk_ref[:, hs] += _dot_tn(ds, qh)

    row = pl.BlockSpec((tl, D_MODEL), lambda i: (i, 0))
    mem = pl.BlockSpec((m, D_MODEL), lambda i: (0, 0))
    return pl.pallas_call(
        body, name="xattn_bwd", grid=(l // tl,), in_specs=[row, mem, mem, row], out_specs=(row, mem, mem),
        out_shape=(jax.ShapeDtypeStruct((l, D_MODEL), BF16), jax.ShapeDtypeStruct((m, D_MODEL), F32),
                   jax.ShapeDtypeStruct((m, D_MODEL), F32)),
        compiler_params=_cp("arbitrary"),
    )(qa, ka, va, doa)


def _me_and_peers():
    x, y, c = lax.axis_index("x"), lax.axis_index("y"), lax.axis_index("c")
    flip = lambda v, bit: (1 - v) if bit else v
    peers = []
    for k in range(1, N_DEV):
        px, py, pc = flip(x, (k >> 2) & 1), flip(y, (k >> 1) & 1), flip(c, k & 1)
        peers.append(((px, py, pc), 4 * px + 2 * py + pc))
    return 4 * x + 2 * y + c, peers


class _SideJob:
    def __init__(self, srcs, landing, src_of, dst_of):
        self.srcs = list(srcs)
        self.landing = list(landing)
        self.n = len(self.landing)
        self.src_of, self.dst_of = src_of, dst_of
        hbm = pl.BlockSpec(memory_space=pl.ANY)
        self.in_specs = [hbm] * len(self.srcs)
        self.out_specs = [hbm] * self.n
        self.scratch = [pltpu.SemaphoreType.DMA((self.n * (N_DEV - 1),)), pltpu.SemaphoreType.DMA((self.n * (N_DEV - 1),)),
                        pltpu.SemaphoreType.DMA((self.n,))]

    def _copies(self, src_refs, out_refs, sems):
        send_sems, recv_sems, loc_sems = sems
        me, peers = _me_and_peers()
        local = [pltpu.make_async_copy(self.src_of(a, me, src_refs), self.dst_of(a, me, out_refs), loc_sems.at[a])
                 for a in range(self.n)]
        sends, recvs = [], []
        for k, (peer, peer_idx) in enumerate(peers):
            for a in range(self.n):
                s = self.n * k + a
                sends.append(pltpu.make_async_remote_copy(
                    src_ref=self.src_of(a, peer_idx, src_refs), dst_ref=self.dst_of(a, me, out_refs),
                    send_sem=send_sems.at[s], recv_sem=recv_sems.at[s], device_id=peer, device_id_type=MESH))
                recvs.append(pltpu.make_async_remote_copy(
                    src_ref=self.src_of(a, me, src_refs), dst_ref=self.dst_of(a, peer_idx, out_refs),
                    send_sem=send_sems.at[s], recv_sem=recv_sems.at[s], device_id=peer, device_id_type=MESH))
        return local, sends, recvs

    def start(self, src_refs, out_refs, sems):
        if not self.n:
            return
        local, sends, _ = self._copies(src_refs, out_refs, sems)
        for cp in local + sends:
            cp.start()

    def wait(self, src_refs, out_refs, sems):
        if not self.n:
            return
        local, sends, recvs = self._copies(src_refs, out_refs, sems)
        for cp in recvs:
            cp.wait_recv()
        for cp in sends:
            cp.wait_send()
        for cp in local:
            cp.wait()


def _gather_job(shards):
    return _SideJob(shards, [jax.ShapeDtypeStruct((N_DEV,) + s.shape, s.dtype) for s in shards],
                    src_of=lambda a, j, srcs: srcs[a], dst_of=lambda a, j, outs: outs[a].at[j])


def _scatter_job(grads):
    landing, parts = [], []
    for g in grads:
        if g.ndim == 3:
            landing.append(jax.ShapeDtypeStruct(g.shape, g.dtype))
            parts.append(None)
        else:
            r = g.shape[0] // N_DEV
            landing.append(jax.ShapeDtypeStruct((N_DEV, r, g.shape[1]), g.dtype))
            parts.append(r)

    def src_of(a, j, srcs):
        if parts[a] is None:
            return srcs[a].at[j]
        return srcs[a].at[pl.ds(pl.multiple_of(j * parts[a], 8), parts[a]), :]

    return _SideJob(grads, landing, src_of=src_of, dst_of=lambda a, j, outs: outs[a].at[j])


def _prologue(w_in_shard, row_shards, x, g, pos_col, inv_row):
    n_row = len(row_shards)
    l, d = x.shape
    tr = _pick(l, (1024, 512, 256))
    nt = l // tr
    mid = nt // 2

    def body(*refs):
        win_ref = refs[0]
        row_refs = refs[1:1 + n_row]
        x_ref, g_ref, p_ref, inv_ref = refs[1 + n_row:5 + n_row]
        out_win = refs[5 + n_row]
        row_outs = refs[6 + n_row:6 + 2 * n_row]
        h_ref, cos_ref, sin_ref = refs[6 + 2 * n_row:9 + 2 * n_row]
        win_b, send_sems, recv_sems, local_sem = refs[9 + 2 * n_row:]
        step = pl.program_id(0)
        cx, cy, cc = lax.axis_index("x"), lax.axis_index("y"), lax.axis_index("c")
        me, sibling = (cx, cy, cc), (cx, cy, 1 - cc)
        chips = [(1 - cx, cy), (cx, 1 - cy), (1 - cx, 1 - cy)]
        slot = lambda p: out_win.at[4 * p[0] + 2 * p[1] + p[2]]

        def copy(k, block, to, src=None):
            return pltpu.make_async_remote_copy(
                src_ref=slot(block) if src is None else src, dst_ref=slot(block), send_sem=send_sems.at[k],
                recv_sem=recv_sems.at[k], device_id=to, device_id_type=MESH)

        mine = pltpu.make_async_copy(win_b, slot(me), local_sem)
        first = [copy(0, me, sibling, src=win_b)]
        first += [copy(1 + j, me, (*chip, cc), src=win_b) for j, chip in enumerate(chips)]
        passed = [copy(4 + j, (*chip, cc), sibling) for j, chip in enumerate(chips)]

        @pl.when(step == 0)
        def _():
            win_b[...] = win_ref[...].astype(BF16)
            mine.start()
            for cp in first:
                cp.start()
            for r, o in zip(row_refs, row_outs):
                o[...] = r[...].astype(BF16)

        h_ref[...] = (_rms(x_ref[...])[1] * g_ref[...]).astype(h_ref.dtype)
        ang = p_ref[...].astype(F32) * inv_ref[...]
        lane = lax.broadcasted_iota(jnp.int32, ang.shape, 1)
        cos_ref[...] = jnp.tile(jnp.cos(ang), (1, RET_QK // 128))
        sin_ref[...] = jnp.tile(jnp.where((lane % RET_DK) < RET_DK // 2, -jnp.sin(ang), jnp.sin(ang)),
                                (1, RET_QK // 128))

        @pl.when(step == mid)
        def _():
            for j, chip in enumerate(chips):
                copy(1 + j, (*chip, cc), me).wait_recv()
                passed[j].start()

        @pl.when(step == nt - 1)
        def _():
            copy(0, sibling, me).wait_recv()
            for j, chip in enumerate(chips):
                copy(4 + j, (*chip, 1 - cc), me).wait_recv()
            for cp in first + passed:
                cp.wait_send()
            mine.wait()

    whole = lambda a: pl.BlockSpec(a.shape, lambda i: (0,) * a.ndim)
    rows = lambda w: pl.BlockSpec((tr, w), lambda i: (i, 0))
    return pl.pallas_call(
        body, name="prologue_allgather_w_in", grid=(nt,),
        in_specs=[whole(w_in_shard)] + [whole(r) for r in row_shards] + [rows(d), whole(g), rows(1), whole(inv_row)],
        out_specs=(pl.BlockSpec(memory_space=pl.ANY), *[whole(r) for r in row_shards], rows(d), rows(RET_QK),
                   rows(RET_QK)),
        out_shape=(jax.ShapeDtypeStruct((N_DEV,) + w_in_shard.shape, BF16),
                   *[jax.ShapeDtypeStruct(r.shape, BF16) for r in row_shards],
                   jax.ShapeDtypeStruct((l, d), BF16), jax.ShapeDtypeStruct((l, RET_QK), F32),
                   jax.ShapeDtypeStruct((l, RET_QK), F32)),
        scratch_shapes=[pltpu.VMEM(w_in_shard.shape, BF16), pltpu.SemaphoreType.DMA((N_DEV - 1,)),
                        pltpu.SemaphoreType.DMA((N_DEV - 1,)), pltpu.SemaphoreType.DMA],
        compiler_params=_cp("arbitrary"),
    )(w_in_shard, *row_shards, x, g, pos_col, inv_row)


def _allreduce_small(small):
    rows = SMALL_ROWS // N_DEV

    def body(x_ref, out_ref, land, send1, recv1, send2, recv2):
        me, peers = _me_and_peers()
        block = lambda j: pl.ds(pl.multiple_of(j * rows, 8), rows)

        def phase(src_of, dst_of, send_sems, recv_sems):
            sends = [pltpu.make_async_remote_copy(src_ref=src_of(pidx), dst_ref=dst_of(me), send_sem=send_sems.at[k],
                                                  recv_sem=recv_sems.at[k], device_id=peer, device_id_type=MESH)
                     for k, (peer, pidx) in enumerate(peers)]
            recvs = [pltpu.make_async_remote_copy(src_ref=src_of(me), dst_ref=dst_of(pidx), send_sem=send_sems.at[k],
                                                  recv_sem=recv_sems.at[k], device_id=peer, device_id_type=MESH)
                     for k, (peer, pidx) in enumerate(peers)]
            for cp in sends:
                cp.start()
            for cp in recvs:
                cp.wait_recv()
            for cp in sends:
                cp.wait_send()

        land[me] = x_ref[block(me), :]
        phase(lambda j: x_ref.at[block(j), :], lambda j: land.at[j], send1, recv1)
        total = land[0]
        for j in range(1, N_DEV):
            total = total + land[j]
        out_ref[block(me), :] = total
        phase(lambda j: out_ref.at[block(me), :], lambda j: out_ref.at[block(j), :], send2, recv2)

    vm = pl.BlockSpec(memory_space=pltpu.VMEM)
    return pl.pallas_call(
        body, name="allreduce_small", in_specs=[vm], out_specs=vm, out_shape=jax.ShapeDtypeStruct(small.shape, F32),
        scratch_shapes=[pltpu.VMEM((N_DEV, rows, D_MODEL), F32)] + [pltpu.SemaphoreType.DMA((N_DEV - 1,))] * 4,
    )(small)


def _adamw(name, got, w, m, v):
    r, c = w.shape
    n_slots = got.shape[0]
    tr = _pick(r, (256, 128, 64))

    def body(got_ref, w_ref, m_ref, v_ref, g_ref, d_ref, nm_ref, nv_ref):
        g = got_ref[0].astype(F32)
        for j in range(1, n_slots):
            g = g + got_ref[j].astype(F32)
        nm = ADAM_B1 * m_ref[...] + (1.0 - ADAM_B1) * g
        nv = ADAM_B2 * v_ref[...] + (1.0 - ADAM_B2) * jnp.square(g)
        m_hat = nm / (1.0 - ADAM_B1 ** ADAM_STEP)
        v_hat = nv / (1.0 - ADAM_B2 ** ADAM_STEP)
        g_ref[...] = g
        d_ref[...] = -ADAM_LR * (m_hat / (jnp.sqrt(v_hat) + ADAM_EPS) + ADAM_WD * w_ref[...])
        nm_ref[...] = nm
        nv_ref[...] = nv

    blk = pl.BlockSpec((tr, c), lambda i: (i, 0))
    out = jax.ShapeDtypeStruct((r, c), F32)
    return pl.pallas_call(
        body, name=name, grid=(r // tr,),
        in_specs=[pl.BlockSpec((n_slots, tr, c), lambda i: (0, i, 0)), blk, blk, blk],
        out_specs=(blk, blk, blk, blk), out_shape=(out, out, out, out), compiler_params=_cp("parallel"),
    )(got, w, m, v)


_SMALL_VECS = ("norm1_g", "ret_gn_g", "s5_d", "s5_glu_b", "norm2_g", "norm_mem_g", "norm_f_g")


def _small_layout():
    lay, row = {}, 0
    for n in _SMALL_VECS + ("loss",):
        lay[n] = (row, 1, D_MODEL)
        row += 1
    for n in ("s5_a_re", "s5_a_im"):
        lay[n] = (row, 4, D_MODEL)
        row += 4
    lay["s5_log_dt"] = (row, 1, S5_G)
    row += 8
    for n in ("s5_b_re", "s5_b_im", "s5_c_re", "s5_c_im"):
        lay[n] = (row, 64, D_MODEL)
        row += 64
    assert row <= SMALL_ROWS
    return lay


def _pack_small(t, loss_row=None):
    lay = _small_layout()
    pieces = [t[n].reshape(1, D_MODEL) for n in _SMALL_VECS]
    pieces.append(jnp.zeros((1, D_MODEL), F32) if loss_row is None else loss_row)
    pieces += [t["s5_a_re"].reshape(4, D_MODEL), t["s5_a_im"].reshape(4, D_MODEL)]
    pieces.append(jnp.pad(t["s5_log_dt"].reshape(1, S5_G), ((0, 7), (0, D_MODEL - S5_G))))
    pieces += [t[n].reshape(64, D_MODEL) for n in ("s5_b_re", "s5_b_im", "s5_c_re", "s5_c_im")]
    pieces.append(jnp.zeros((SMALL_ROWS - lay["s5_c_im"][0] - 64, D_MODEL), F32))
    return jnp.concatenate(pieces, axis=0)


def _adamw_small(g_sum, w, m, v):
    lay = _small_layout()
    names = [n for n in lay if n != "loss"]

    def body(g_ref, w_ref, m_ref, v_ref, *outs):
        g = g_ref[...]
        nm = ADAM_B1 * m_ref[...] + (1.0 - ADAM_B1) * g
        nv = ADAM_B2 * v_ref[...] + (1.0 - ADAM_B2) * jnp.square(g)
        m_hat = nm / (1.0 - ADAM_B1 ** ADAM_STEP)
        v_hat = nv / (1.0 - ADAM_B2 ** ADAM_STEP)
        delta = -ADAM_LR * (m_hat / (jnp.sqrt(v_hat) + ADAM_EPS) + ADAM_WD * w_ref[...])
        for i, n in enumerate(names):
            r0, rows, lanes = lay[n]
            for part, val in enumerate((g, delta, nm, nv)):
                outs[4 * i + part][...] = val[r0:r0 + rows, 0:lanes]
        r0 = lay["loss"][0]
        outs[-1][...] = g[r0:r0 + 1, :]

    shapes = []
    for n in names:
        shapes += [jax.ShapeDtypeStruct(lay[n][1:], F32)] * 4
    shapes.append(jax.ShapeDtypeStruct((1, D_MODEL), F32))
    outs = pl.pallas_call(body, name="adamw_small", out_shape=tuple(shapes),
                          compiler_params=pltpu.CompilerParams(vmem_limit_bytes=VMEM_LIMIT))(g_sum, w, m, v)
    return {n: tuple(outs[4 * i:4 * i + 4]) for i, n in enumerate(names)}, outs[-1]


_W_NAMES = ("norm1_g", "w_in", "ret_gn_g", "s5_a_re", "s5_a_im", "s5_log_dt", "s5_b_re", "s5_b_im", "s5_c_re", "s5_c_im",
            "s5_d", "s5_glu_w", "s5_glu_b", "w_out", "norm2_g", "norm_mem_g", "xa_wq", "xa_wk", "xa_wv", "xa_wo",
            "norm_f_g")
_ROW_NAMES = ("s5_glu_w", "w_out", "xa_wq", "xa_wk", "xa_wv", "xa_wo")


def kernel(x, mem, positions, norm1_g, w_in, ret_gn_g, s5_a_re, s5_a_im, s5_log_dt, s5_b_re, s5_b_im, s5_c_re, s5_c_im, s5_d, s5_glu_w, s5_glu_b, w_out, norm2_g, norm_mem_g, xa_wq, xa_wk, xa_wv, xa_wo, norm_f_g, loss_target, m_norm1_g, m_w_in, m_ret_gn_g, m_s5_a_re, m_s5_a_im, m_s5_log_dt, m_s5_b_re, m_s5_b_im, m_s5_c_re, m_s5_c_im, m_s5_d, m_s5_glu_w, m_s5_glu_b, m_w_out, m_norm2_g, m_norm_mem_g, m_xa_wq, m_xa_wk, m_xa_wv, m_xa_wo, m_norm_f_g, v_norm1_g, v_w_in, v_ret_gn_g, v_s5_a_re, v_s5_a_im, v_s5_log_dt, v_s5_b_re, v_s5_b_im, v_s5_c_re, v_s5_c_im, v_s5_d, v_s5_glu_w, v_s5_glu_b, v_w_out, v_norm2_g, v_norm_mem_g, v_xa_wq, v_xa_wk, v_xa_wv, v_xa_wo, v_norm_f_g):
    w = dict(norm1_g=norm1_g, w_in=w_in, ret_gn_g=ret_gn_g, s5_a_re=s5_a_re, s5_a_im=s5_a_im, s5_log_dt=s5_log_dt,
             s5_b_re=s5_b_re, s5_b_im=s5_b_im, s5_c_re=s5_c_re, s5_c_im=s5_c_im, s5_d=s5_d, s5_glu_w=s5_glu_w,
             s5_glu_b=s5_glu_b, w_out=w_out, norm2_g=norm2_g, norm_mem_g=norm_mem_g, xa_wq=xa_wq, xa_wk=xa_wk,
             xa_wv=xa_wv, xa_wo=xa_wo, norm_f_g=norm_f_g)
    mom = dict(norm1_g=m_norm1_g, w_in=m_w_in, ret_gn_g=m_ret_gn_g, s5_a_re=m_s5_a_re, s5_a_im=m_s5_a_im,
               s5_log_dt=m_s5_log_dt, s5_b_re=m_s5_b_re, s5_b_im=m_s5_b_im, s5_c_re=m_s5_c_re, s5_c_im=m_s5_c_im,
               s5_d=m_s5_d, s5_glu_w=m_s5_glu_w, s5_glu_b=m_s5_glu_b, w_out=m_w_out, norm2_g=m_norm2_g,
               norm_mem_g=m_norm_mem_g, xa_wq=m_xa_wq, xa_wk=m_xa_wk, xa_wv=m_xa_wv, xa_wo=m_xa_wo,
               norm_f_g=m_norm_f_g)
    var = dict(norm1_g=v_norm1_g, w_in=v_w_in, ret_gn_g=v_ret_gn_g, s5_a_re=v_s5_a_re, s5_a_im=v_s5_a_im,
               s5_log_dt=v_s5_log_dt, s5_b_re=v_s5_b_re, s5_b_im=v_s5_b_im, s5_c_re=v_s5_c_re, s5_c_im=v_s5_c_im,
               s5_d=v_s5_d, s5_glu_w=v_s5_glu_w, s5_glu_b=v_s5_glu_b, w_out=v_w_out, norm2_g=v_norm2_g,
               norm_mem_g=v_norm_mem_g, xa_wq=v_xa_wq, xa_wk=v_xa_wk, xa_wv=v_xa_wv, xa_wo=v_xa_wo,
               norm_f_g=v_norm_f_g)
    shapes = {n: w[n].shape for n in _W_NAMES}

    x2d, mem2d, tgt = x[0], mem[0], loss_target[0]
    l = x2d.shape[0]
    ret_c = _pick(l, (512, 256, 128))
    s5_t = _pick(l, (256, 128))
    g1, g2, gm, gf = norm1_g, norm2_g, norm_mem_g, norm_f_g.reshape(1, D_MODEL)

    half = RET_DK // 2
    inv = ROPE_BASE ** (-jnp.arange(half, dtype=F32) / half)
    win_s, *rest = _prologue(w_in[0], [w[n][0] for n in _ROW_NAMES], x2d, g1, positions[0].reshape(l, 1),
                             jnp.tile(inv, 128 // half)[None, :])
    row_shards_b, (h1, cos_t, sin_t) = rest[:len(_ROW_NAMES)], rest[len(_ROW_NAMES):]

    to_gpn = lambda b: jnp.transpose(b, (0, 2, 1)).reshape(S5_G, S5_P * S5_N)
    from_gpn = lambda b: jnp.transpose(b.reshape(S5_G, S5_P, S5_N), (0, 2, 1))
    disc_args = (s5_a_re[0], s5_a_im[0], s5_log_dt[0].reshape(S5_G, 1), to_gpn(s5_b_re[0]), to_gpn(s5_b_im[0]))
    abar_re, abar_im, bb_re_t, bb_im_t = _s5_discretize(*disc_args)
    bbm, ccm = _s5_block_mats(from_gpn(bb_re_t), from_gpn(bb_im_t), s5_c_re[0], s5_c_im[0])
    a_z = _s5_z(abar_re, abar_im)

    proj, *rows_01 = _mm_nn_slots("in_proj", h1, win_s, BF16, side=_gather_job(row_shards_b[:2]))
    full = {n: g.reshape(N_DEV * r, D_MODEL) for n, g, r in zip(_ROW_NAMES[:2], rows_01, ROW_SHARDS[:2])}
    rconsts = _ret_constants(ret_c)
    ret, o_saved, r_prev, q_rot, k_rot = _ret_fwd(proj, cos_t, sin_t, rconsts, ret_gn_g, ret_c)
    mix, xstart, *rows_xa = _s5_fwd(proj, ret, bbm, ccm, s5_d, full["s5_glu_w"], s5_glu_b, a_z, s5_t,
                                    side=_gather_job(row_shards_b[2:]))
    full.update({n: g.reshape(N_DEV * r, D_MODEL) for n, g, r in zip(_ROW_NAMES[2:], rows_xa, ROW_SHARDS[2:])})
    x1, h2 = _mm_nn("out_proj", mix, full["w_out"], F32, residual=x2d, epi=_epi_norm_fwd(g2))
    mn = _rms_fwd("norm_mem_fwd", mem2d, gm)
    qa = _mm_nn("xa_q", h2, full["xa_wq"], BF16)
    ka = _mm_nn("xa_k", mn, full["xa_wk"], BF16)
    va = _mm_nn("xa_v", mn, full["xa_wv"], BF16)
    oa = _attn_fwd(qa, ka, va)
    dx2, dgf, loss_lanes = _mm_nn("xa_o", oa, full["xa_wo"], F32, residual=x1, epi=_epi_loss(gf, tgt))

    doa = _mm_nt("xa_o_dx", dx2, full["xa_wo"], BF16)
    dwo = _mm_tn("xa_o_dw", oa, dx2, BF16)
    dqa, dka, dva = _attn_bwd(qa, ka, va, doa)
    dx1, dg2 = _mm_nt("xa_q_dx", dqa, full["xa_wq"], F32, epi=_epi_norm_bwd(x1, g2, dx2))
    dwq = _mm_tn("xa_q_dw", h2, dqa, BF16)
    dwk = _mm_tn("xa_k_dw", mn, dka, BF16)
    dwv = _mm_tn("xa_v_dw", mn, dva, BF16)
    dmn = _mm_nt("xa_v_dx", dva, full["xa_wv"], F32, residual=_mm_nt("xa_k_dx", dka, full["xa_wk"], F32))
    _, dgm = _rms_bwd("norm_mem_bwd", mem2d, gm, dmn, None)
    dmix = _mm_nt("out_proj_dx", dx1, full["w_out"], BF16)
    dwout = _mm_tn("out_proj_dw", mix, dx1, BF16)
    dret, dgn, *got_a = _ret_bwd(proj, q_rot, k_rot, cos_t, sin_t, rconsts, ret_gn_g, o_saved, r_prev, dmix, ret_c,
                                 side=_scatter_job([dwout, dwq, dwk, dwv, dwo]))
    dproj, y2, dz, dbbm, dccm, dabar, dd, dgb = _s5_bwd(proj, dmix, dret, xstart, bbm, ccm, s5_d, full["s5_glu_w"],
                                                        s5_glu_b, a_z, s5_t)
    dglu = _mm_tn("s5_glu_dw", y2, dz, BF16)
    dwin_s, got_glu = _mm_tn_slots("in_proj_dw", h1, dproj, N_DEV, BF16, side=_scatter_job([dglu]))
    grad_x, dg1, got_win = _mm_nt_slots("in_proj_dx", dproj, win_s, F32, side=_scatter_job([dwin_s]),
                                        epi=_epi_norm_bwd(x2d, g1, dx1))

    dab_re, dab_im = _s5_unz(dabar)
    dbb_re, dbb_im = _s5_block_diag_bb(dbbm)
    dc_re, dc_im = _s5_block_diag_cc(dccm)
    da_re, da_im, dlog_dt, db_re_t, db_im_t = _s5_discretize_bwd(*disc_args, dab_re, dab_im, to_gpn(dbb_re),
                                                                 to_gpn(dbb_im))
    db_re, db_im = from_gpn(db_re_t), from_gpn(db_im_t)
    small_g = dict(norm1_g=dg1, ret_gn_g=dgn, s5_d=dd, s5_glu_b=dgb, norm2_g=dg2, norm_mem_g=dgm, norm_f_g=dgf,
                   s5_a_re=da_re, s5_a_im=da_im, s5_log_dt=dlog_dt, s5_b_re=db_re, s5_b_im=db_im, s5_c_re=dc_re,
                   s5_c_im=dc_im)
    small_pack = _pack_small(small_g, loss_row=loss_lanes)

    res = {}
    got = dict(zip(("w_out", "xa_wq", "xa_wk", "xa_wv", "xa_wo"), got_a), w_in=got_win, s5_glu_w=got_glu)
    for n in ("w_in",) + _ROW_NAMES:
        res[n] = _adamw("adamw_" + n, got[n], w[n][0], mom[n][0], var[n][0])
    small_sum = _allreduce_small(small_pack)
    small_res, loss_sum = _adamw_small(small_sum, _pack_small(w), _pack_small(mom), _pack_small(var))
    loss = (0.5 / D_MODEL) * jnp.sum(loss_sum)
    res.update(small_res)

    outs = [loss, grad_x[None]]
    for part in range(4):
        for n in _W_NAMES:
            outs.append(res[n][part].reshape(shapes[n]))
    return tuple(outs)
```

```python
import jax
import jax.numpy as jnp
from jax import lax
from jax.experimental import pallas as pl
from jax.experimental.pallas import tpu as pltpu

F32 = jnp.float32
BF16 = jnp.bfloat16
MESH = pl.DeviceIdType.MESH

D_MODEL = 1024
RET_HEADS, RET_DK, RET_DV = 8, 64, 128
RET_QK = RET_HEADS * RET_DK
S5_G, S5_N, S5_P = 64, 64, 16
S5_NB = 8
S5_GB = S5_G // S5_NB
S5_BS = S5_GB * S5_N
S5_COLS = 2 * S5_G * S5_N
XA_HEADS, XA_DH = 4, 256
EPS = 1e-6
ROPE_BASE = 10000.0
N_DEV = 8
W_IN_SHARD = 640
ROW_SHARDS = (128, 256, 128, 128, 128, 128)
ROWPACK = sum(ROW_SHARDS)
SMALL_ROWS = 320
ADAM_LR, ADAM_B1, ADAM_B2, ADAM_EPS, ADAM_WD, ADAM_STEP = 0.001, 0.9, 0.999, 1e-08, 0.01, 10

VMEM_LIMIT = 56 * 1024 * 1024


def _cp(*sem):
    return pltpu.CompilerParams(dimension_semantics=tuple(sem), vmem_limit_bytes=VMEM_LIMIT)


def _dot(a, b):
    return jnp.dot(a, b, preferred_element_type=F32)


def _dot_nt(a, b):
    return lax.dot_general(a, b, (((1,), (1,)), ((), ())), preferred_element_type=F32)


def _dot_tn(a, b):
    return lax.dot_general(a, b, (((0,), (0,)), ((), ())), preferred_element_type=F32)


def _sigmoid(x):
    return 1.0 / (1.0 + jnp.exp(-x))


def _silu(x):
    return x * _sigmoid(x)


def _dsilu(x):
    s = _sigmoid(x)
    return s * (1.0 + x * (1.0 - s))


_GELU_C = 0.7978845608028654


def _gelu(x):
    return 0.5 * x * (1.0 + jnp.tanh(_GELU_C * (x + 0.044715 * (x * x * x))))


def _gelu_and_grad(x):
    t = jnp.tanh(_GELU_C * (x + 0.044715 * (x * x * x)))
    half = 0.5 * (1.0 + t)
    return x * half, half + 0.5 * x * (1.0 - t * t) * (_GELU_C * (1.0 + 3.0 * 0.044715 * (x * x)))


def _pick(n, cands):
    for c in cands:
        if n % c == 0:
            return c
    return n


class _Epilogue:
    def __init__(self, rows, vecs, row_out_dtypes, n_sums, fn):
        self.rows, self.vecs, self.row_out_dtypes, self.n_sums, self.fn = list(rows), list(vecs), list(row_out_dtypes), n_sums, fn


def _rms(x):
    rs = lax.rsqrt(jnp.mean(x * x, axis=-1, keepdims=True) + EPS)
    return rs, x * rs


def _rms_dx(dn, xn, rs):
    return rs * (dn - xn * jnp.mean(dn * xn, axis=-1, keepdims=True))


def _epi_norm_fwd(g):
    def fn(r, rows, vecs):
        return r, [_rms(r)[1] * vecs[0]], []

    return _Epilogue([], [g], [BF16], 0, fn)


def _epi_loss(gf, target):
    def fn(r, rows, vecs):
        rs, xn = _rms(r)
        e = xn * vecs[0] - rows[0]
        dy = e * (1.0 / r.shape[-1])
        return (_rms_dx(dy * vecs[0], xn, rs), [],
                [jnp.sum(dy * xn, axis=0, keepdims=True), jnp.sum(e * e, axis=0, keepdims=True)])

    return _Epilogue([target], [gf], [], 2, fn)


def _epi_norm_bwd(x, g, dres):
    def fn(r, rows, vecs):
        rs, xn = _rms(rows[0])
        return _rms_dx(r * vecs[0], xn, rs) + rows[1], [], [jnp.sum(r * xn, axis=0, keepdims=True)]

    return _Epilogue([x, dres], [g], [], 1, fn)


def _mm_core(name, operands, in_specs, out_spec, out_shape, grid, nk, dims, acc_shape, has_res, side=None, epi=None):
    n_in = 3 if has_res else 2
    n_epi_in = len(epi.rows) + len(epi.vecs) if epi else 0
    n_epi_out = len(epi.row_out_dtypes) + epi.n_sums if epi else 0
    n_side_in = len(side.srcs) if side else 0
    n_side_out = side.n if side else 0

    def body(*refs):
        a_ref, b_ref = refs[0], refs[1]
        r_ref = refs[2] if has_res else None
        epi_in = refs[n_in:n_in + n_epi_in]
        side_in = refs[n_in + n_epi_in:n_in + n_epi_in + n_side_in]
        n0 = n_in + n_epi_in + n_side_in
        o_ref = refs[n0]
        epi_out = refs[n0 + 1:n0 + 1 + n_epi_out]
        side_out = refs[n0 + 1 + n_epi_out:n0 + 1 + n_epi_out + n_side_out]
        rest = refs[n0 + 1 + n_epi_out + n_side_out:]
        acc, sems = (rest[0], rest[1:]) if nk > 1 else (None, rest)
        i, j, k = pl.program_id(0), pl.program_id(1), pl.program_id(2)
        if side:
            @pl.when((i == 0) & (j == 0) & (k == 0))
            def _():
                side.start(side_in, side_out, sems)

        def product():
            if len(b_ref.shape) == 3:
                ns = b_ref.shape[2]
                return sum(lax.dot_general(a_ref[:, p * ns:(p + 1) * ns].astype(BF16), b_ref[p].astype(BF16),
                                           (dims, ((), ())), preferred_element_type=F32)
                           for p in range(b_ref.shape[0]))
            return lax.dot_general(a_ref[...].astype(BF16), b_ref[...].astype(BF16), (dims, ((), ())),
                                   preferred_element_type=F32)

        def finish(r):
            if has_res:
                r = r + r_ref[...]
            if epi is None:
                o_ref[...] = r.astype(o_ref.dtype)
                return
            n_rows = len(epi.rows)
            main, row_vals, sums = epi.fn(r, [t[...] for t in epi_in[:n_rows]], [t[...] for t in epi_in[n_rows:]])
            o_ref[...] = main.astype(o_ref.dtype)
            for ref, val in zip(epi_out, row_vals):
                ref[...] = val.astype(ref.dtype)
            for ref, val in zip(epi_out[len(row_vals):], sums):
                @pl.when(i == 0)
                def _(ref=ref):
                    ref[...] = jnp.zeros_like(ref)

                ref[...] += val

        if nk == 1:
            finish(product())
        else:
            @pl.when(k == 0)
            def _():
                acc[...] = jnp.zeros_like(acc)

            acc[...] += product()

            @pl.when(k == nk - 1)
            def _():
                finish(acc[...])

        if side:
            @pl.when((i == grid[0] - 1) & (j == grid[1] - 1) & (k == grid[2] - 1))
            def _():
                side.wait(side_in, side_out, sems)

    acc_scratch = [pltpu.VMEM(acc_shape, F32)] if nk > 1 else []
    in_specs, out_specs, out_shapes, operands = list(in_specs), [out_spec], [out_shape], list(operands)
    if epi:
        assert grid[1] == 1, "an epilogue needs tiles that span whole rows"
        tm, n = out_spec.block_shape
        row_spec = pl.BlockSpec((tm, n), lambda i, j, k: (i, 0))
        vec_spec = pl.BlockSpec((1, n), lambda i, j, k: (0, 0))
        in_specs += [row_spec] * len(epi.rows) + [vec_spec] * len(epi.vecs)
        operands += epi.rows + epi.vecs
        out_specs += [row_spec] * len(epi.row_out_dtypes) + [vec_spec] * epi.n_sums
        out_shapes += [jax.ShapeDtypeStruct(out_shape.shape, d) for d in epi.row_out_dtypes]
        out_shapes += [jax.ShapeDtypeStruct((1, n), F32)] * epi.n_sums
    scratch = acc_scratch
    if side:
        in_specs += side.in_specs
        operands += side.srcs
        out_specs += side.out_specs
        out_shapes += side.landing
        scratch = acc_scratch + side.scratch
    plain = side is None and epi is None
    res = pl.pallas_call(
        body, name=name, grid=grid, in_specs=in_specs, out_specs=out_specs[0] if plain else tuple(out_specs),
        out_shape=out_shapes[0] if plain else tuple(out_shapes), scratch_shapes=scratch,
        compiler_params=_cp("parallel", "parallel", "arbitrary") if plain else _cp("arbitrary", "arbitrary", "arbitrary"),
    )(*operands)
    return res


def _mm_nn(name, a, b, out_dtype, residual=None, epi=None):
    m, kk = a.shape
    n = b.shape[1]
    tm, tn, tk = _pick(m, (1024, 512, 256)), _pick(n, (1024, 512)), _pick(kk, (2048, 1024, 512))
    ops = [a, b]
    specs = [pl.BlockSpec((tm, tk), lambda i, j, k: (i, k)), pl.BlockSpec((tk, tn), lambda i, j, k: (k, j))]
    if residual is not None:
        ops.append(residual)
        specs.append(pl.BlockSpec((tm, tn), lambda i, j, k: (i, j)))
    return _mm_core(name, ops, specs, pl.BlockSpec((tm, tn), lambda i, j, k: (i, j)),
                    jax.ShapeDtypeStruct((m, n), out_dtype), (m // tm, n // tn, kk // tk), kk // tk,
                    ((1,), (0,)), (tm, tn), residual is not None, epi=epi)


def _mm_nt(name, a, b, out_dtype, residual=None, epi=None):
    m, kk = a.shape
    n = b.shape[0]
    tm, tn, tk = _pick(m, (1024, 512, 256)), _pick(n, (2048, 1024, 512)), _pick(kk, (2048, 1024, 512))
    ops = [a, b]
    specs = [pl.BlockSpec((tm, tk), lambda i, j, k: (i, k)), pl.BlockSpec((tn, tk), lambda i, j, k: (j, k))]
    if residual is not None:
        ops.append(residual)
        specs.append(pl.BlockSpec((tm, tn), lambda i, j, k: (i, j)))
    return _mm_core(name, ops, specs, pl.BlockSpec((tm, tn), lambda i, j, k: (i, j)),
                    jax.ShapeDtypeStruct((m, n), out_dtype), (m // tm, n // tn, kk // tk), kk // tk,
                    ((1,), (1,)), (tm, tn), residual is not None, epi=epi)


def _mm_tn(name, a, b, out_dtype):
    kk, m = a.shape
    n = b.shape[1]
    tm, tn, tk = _pick(m, (1024, 512)), _pick(n, (1024, 512)), _pick(kk, (2048, 1024, 512, 256))
    specs = [pl.BlockSpec((tk, tm), lambda i, j, k: (k, i)), pl.BlockSpec((tk, tn), lambda i, j, k: (k, j))]
    return _mm_core(name, [a, b], specs, pl.BlockSpec((tm, tn), lambda i, j, k: (i, j)),
                    jax.ShapeDtypeStruct((m, n), out_dtype), (m // tm, n // tn, kk // tk), kk // tk,
                    ((0,), (0,)), (tm, tn), False)


def _mm_nn_slots(name, a, b_slots, out_dtype, side=None):
    m, kk = a.shape
    s, _, ns = b_slots.shape
    tm, tk = _pick(m, (4096, 2048, 1024, 512, 256)), _pick(kk, (1024, 512))
    specs = [pl.BlockSpec((tm, tk), lambda i, j, k: (i, k)), pl.BlockSpec((None, tk, ns), lambda i, j, k: (j, k, 0))]
    return _mm_core(name, [a, b_slots], specs, pl.BlockSpec((tm, ns), lambda i, j, k: (i, j)),
                    jax.ShapeDtypeStruct((m, s * ns), out_dtype), (m // tm, s, kk // tk), kk // tk,
                    ((1,), (0,)), (tm, ns), False, side)


def _mm_nt_slots(name, a, b_slots, out_dtype, side=None, epi=None):
    m = a.shape[0]
    s, n, ns = b_slots.shape
    tm, tn = _pick(m, (1024, 512, 256)), _pick(n, (1024, 512))
    per = _pick(s, (2, 1))
    specs = [pl.BlockSpec((tm, per * ns), lambda i, j, k: (i, k)),
             pl.BlockSpec((per, tn, ns), lambda i, j, k: (k, j, 0))]
    return _mm_core(name, [a, b_slots], specs, pl.BlockSpec((tm, tn), lambda i, j, k: (i, j)),
                    jax.ShapeDtypeStruct((m, n), out_dtype), (m // tm, n // tn, s // per), s // per,
                    ((1,), (1,)), (tm, tn), False, side, epi)


def _mm_tn_slots(name, a, b, s, out_dtype, side=None):
    kk, m = a.shape
    ns = b.shape[1] // s
    tm, tk = _pick(m, (1024, 512)), _pick(kk, (4096, 2048, 1024, 512, 256))
    specs = [pl.BlockSpec((tk, tm), lambda i, j, k: (k, i)), pl.BlockSpec((tk, ns), lambda i, j, k: (k, j))]
    return _mm_core(name, [a, b], specs, pl.BlockSpec((None, tm, ns), lambda i, j, k: (j, i, 0)),
                    jax.ShapeDtypeStruct((s, m, ns), out_dtype), (m // tm, s, kk // tk), kk // tk,
                    ((0,), (0,)), (tm, ns), False, side)


def _rms_fwd(name, x, g):
    r, d = x.shape
    tr = _pick(r, (1024, 512, 256))

    def body(x_ref, g_ref, o_ref):
        xv = x_ref[...]
        rs = lax.rsqrt(jnp.mean(xv * xv, axis=-1, keepdims=True) + EPS)
        o_ref[...] = (xv * rs * g_ref[...]).astype(o_ref.dtype)

    return pl.pallas_call(
        body, name=name, grid=(r // tr,),
        in_specs=[pl.BlockSpec((tr, d), lambda i: (i, 0)), pl.BlockSpec((1, d), lambda i: (0, 0))],
        out_specs=pl.BlockSpec((tr, d), lambda i: (i, 0)),
        out_shape=jax.ShapeDtypeStruct((r, d), BF16), compiler_params=_cp("parallel"),
    )(x, g)


def _rms_bwd(name, x, g, dh, dres):
    r, d = x.shape
    tr = _pick(r, (512, 256))
    has_res = dres is not None

    def body(*refs):
        if has_res:
            x_ref, g_ref, dh_ref, dr_ref, dx_ref, dg_ref = refs
        else:
            x_ref, g_ref, dh_ref, dx_ref, dg_ref = refs
        i = pl.program_id(0)

        @pl.when(i == 0)
        def _():
            dg_ref[...] = jnp.zeros_like(dg_ref)

        xv = x_ref[...]
        dhv = dh_ref[...].astype(F32)
        rs = lax.rsqrt(jnp.mean(xv * xv, axis=-1, keepdims=True) + EPS)
        xn = xv * rs
        dg_ref[...] += jnp.sum(dhv * xn, axis=0, keepdims=True)
        dn = dhv * g_ref[...]
        dx = rs * (dn - xn * jnp.mean(dn * xn, axis=-1, keepdims=True))
        if has_res:
            dx = dx + dr_ref[...]
        dx_ref[...] = dx

    row = pl.BlockSpec((tr, d), lambda i: (i, 0))
    vec = pl.BlockSpec((1, d), lambda i: (0, 0))
    ops = [x, g, dh] + ([dres] if has_res else [])
    return pl.pallas_call(
        body, name=name, grid=(r // tr,),
        in_specs=[row, vec, row] + ([row] if has_res else []),
        out_specs=(row, vec),
        out_shape=(jax.ShapeDtypeStruct((r, d), F32), jax.ShapeDtypeStruct((1, d), F32)),
        compiler_params=_cp("arbitrary"),
    )(*ops)


def _rot(x, cos_t, sin_t):
    n = x.shape[-1]
    lane = lax.broadcasted_iota(jnp.int32, x.shape, 1)
    partner = jnp.where((lane % RET_DK) < RET_DK // 2, pltpu.roll(x, n - RET_DK // 2, 1), pltpu.roll(x, RET_DK // 2, 1))
    return x * cos_t + partner * sin_t


def _ret_constants(c):
    log_g = jnp.log1p(-jnp.exp2(-5.0 - jnp.arange(RET_HEADS, dtype=F32)))
    j = jnp.arange(c, dtype=F32)
    diff = j[:, None] - j[None, :]
    decay = jnp.where(diff[None] >= 0.0, jnp.exp(log_g[:, None, None] * jnp.maximum(diff, 0.0)[None]), 0.0)
    q_w = jnp.exp(log_g[None, :] * (j + 1.0)[:, None])
    k_w = jnp.exp(log_g[None, :] * (c - 1.0 - j)[:, None])
    cd = jnp.exp(log_g * c)
    rep = lambda t: jnp.repeat(t, RET_DK, axis=1)
    cd_row = jnp.repeat(cd, RET_DV)[None, :]
    return decay, rep(q_w), rep(k_w), cd_row


def _pair_of(h, c):
    lane = lax.broadcasted_iota(jnp.int32, (c, 2 * RET_DK), 1)
    mine = (lane < RET_DK) if h % 2 == 0 else (lane >= RET_DK)
    return slice((h // 2) * 2 * RET_DK, (h // 2 + 1) * 2 * RET_DK), mine


def _keep(x, mine):
    return jnp.where(mine, x, jnp.zeros_like(x))


def _ret_fwd(proj, cos_t, sin_t, consts, gn_g, c):
    l = proj.shape[0]
    nc = l // c
    decay, qw, kw, cd_row = consts

    def body(q_ref, k_ref, v_ref, g_ref, cos_ref, sin_ref, dec_ref, qw_ref, kw_ref, cd_ref, gn_ref,
             ret_ref, o_ref, rp_ref, qb_ref, kb_ref, state):
        @pl.when(pl.program_id(0) == 0)
        def _():
            state[...] = jnp.zeros_like(state)

        cs, sn = cos_ref[...], sin_ref[...]
        qr = _rot(q_ref[...].astype(F32), cs, sn)
        kr = _rot(k_ref[...].astype(F32), cs, sn) * (RET_DK ** -0.5)
        qb, kb = qr.astype(BF16), kr.astype(BF16)
        qb_ref[...] = qb
        kb_ref[...] = kb
        qwb = (qr * qw_ref[...]).astype(BF16)
        kwb = (kr * kw_ref[...]).astype(BF16)
        vb = v_ref[...].astype(BF16)
        for h in range(RET_HEADS):
            ps, mine = _pair_of(h, c)
            vs = slice(h * RET_DV, (h + 1) * RET_DV)
            s = _dot_nt(_keep(qb[:, ps], mine), kb[:, ps]) * dec_ref[h]
            r_prev = state[h]
            rp_ref[0, h] = r_prev
            o = _dot(s.astype(BF16), vb[:, vs]) + _dot(_keep(qwb[:, ps], mine), r_prev.astype(BF16))
            state[h] = cd_ref[:, vs] * r_prev + _dot_tn(_keep(kwb[:, ps], mine), vb[:, vs])
            o_ref[:, vs] = o
            mu = jnp.mean(o, axis=-1, keepdims=True)
            var = jnp.mean(jnp.square(o - mu), axis=-1, keepdims=True)
            on = (o - mu) * lax.rsqrt(var + EPS)
            ret_ref[:, vs] = (on * gn_ref[:, vs] * _silu(g_ref[:, vs].astype(F32))).astype(ret_ref.dtype)

    const2 = lambda shape: pl.BlockSpec(shape, lambda i: (0,) * len(shape))
    return pl.pallas_call(
        body, name="retention_fwd", grid=(nc,),
        in_specs=[pl.BlockSpec((c, RET_QK), lambda i: (i, 0)), pl.BlockSpec((c, RET_QK), lambda i: (i, 1)),
                  pl.BlockSpec((c, D_MODEL), lambda i: (i, 1)), pl.BlockSpec((c, D_MODEL), lambda i: (i, 2)),
                  pl.BlockSpec((c, RET_QK), lambda i: (i, 0)), pl.BlockSpec((c, RET_QK), lambda i: (i, 0)),
                  const2((RET_HEADS, c, c)), const2((c, RET_QK)), const2((c, RET_QK)), const2((1, D_MODEL)),
                  const2((1, D_MODEL))],
        out_specs=(pl.BlockSpec((c, D_MODEL), lambda i: (i, 0)), pl.BlockSpec((c, D_MODEL), lambda i: (i, 0)),
                   pl.BlockSpec((1, RET_HEADS, 2 * RET_DK, RET_DV), lambda i: (i, 0, 0, 0)),
                   pl.BlockSpec((c, RET_QK), lambda i: (i, 0)), pl.BlockSpec((c, RET_QK), lambda i: (i, 0))),
        out_shape=(jax.ShapeDtypeStruct((l, 2 * D_MODEL), BF16), jax.ShapeDtypeStruct((l, D_MODEL), F32),
                   jax.ShapeDtypeStruct((nc, RET_HEADS, 2 * RET_DK, RET_DV), F32),
                   jax.ShapeDtypeStruct((l, RET_QK), BF16), jax.ShapeDtypeStruct((l, RET_QK), BF16)),
        scratch_shapes=[pltpu.VMEM((RET_HEADS, 2 * RET_DK, RET_DV), F32)],
        compiler_params=_cp("arbitrary"),
    )(proj, proj, proj, proj, cos_t, sin_t, decay, qw, kw, cd_row, gn_g)


def _ret_bwd(proj, qb_saved, kb_saved, cos_t, sin_t, consts, gn_g, o_saved, r_prev_saved, dmix, c, side):
    l = proj.shape[0]
    nc = l // c
    decay, qw, kw, cd_row = consts
    n_in = 14

    def body(*refs):
        (q_ref, k_ref, v_ref, g_ref, cos_ref, sin_ref, dec_ref, qw_ref, kw_ref, cd_ref, gn_ref, o_ref, rp_ref,
         dr_ref) = refs[:n_in]
        side_in = refs[n_in:n_in + len(side.srcs)]
        out_ref, dgn_ref = refs[n_in + len(side.srcs):n_in + len(side.srcs) + 2]
        side_out = refs[n_in + len(side.srcs) + 2:n_in + len(side.srcs) + 2 + side.n]
        state, dq_s, dk_s = refs[n_in + len(side.srcs) + 2 + side.n:n_in + len(side.srcs) + 5 + side.n]
        sems = refs[n_in + len(side.srcs) + 5 + side.n:]

        @pl.when(pl.program_id(0) == 0)
        def _():
            side.start(side_in, side_out, sems)
            state[...] = jnp.zeros_like(state)
            dgn_ref[...] = jnp.zeros_like(dgn_ref)

        cs, sn = cos_ref[...], sin_ref[...]
        qb, kb = q_ref[...], k_ref[...]
        qwv, kwv = qw_ref[...], kw_ref[...]
        qwb = (qb.astype(F32) * qwv).astype(BF16)
        kwb = (kb.astype(F32) * kwv).astype(BF16)
        vb = v_ref[...].astype(BF16)
        dq2 = dk2 = None
        for h in range(RET_HEADS):
            ps, mine = _pair_of(h, c)
            vs = slice(h * RET_DV, (h + 1) * RET_DV)
            dec = dec_ref[h]
            qm, km = _keep(qb[:, ps], mine), _keep(kb[:, ps], mine)
            o = o_ref[:, vs]
            mu = jnp.mean(o, axis=-1, keepdims=True)
            var = jnp.mean(jnp.square(o - mu), axis=-1, keepdims=True)
            rstd = lax.rsqrt(var + EPS)
            on = (o - mu) * rstd
            gate = g_ref[:, vs].astype(F32)
            sg = _silu(gate)
            dret = dr_ref[:, vs].astype(F32)
            gn = gn_ref[:, vs]
            dgn_ref[:, vs] += jnp.sum(dret * on * sg, axis=0, keepdims=True)
            out_ref[:, 2 * RET_QK + D_MODEL + h * RET_DV:2 * RET_QK + D_MODEL + (h + 1) * RET_DV] = (
                dret * on * gn * _dsilu(gate)).astype(out_ref.dtype)
            don = dret * gn * sg
            do = rstd * (don - jnp.mean(don, axis=-1, keepdims=True)
                         - on * jnp.mean(don * on, axis=-1, keepdims=True))
            dob = do.astype(BF16)
            sn_h = state[h]
            snb = sn_h.astype(BF16)
            s = _dot_nt(qm, kb[:, ps]) * dec
            dv = _dot_tn(s.astype(BF16), dob) + _dot(_keep(kwb[:, ps], mine), snb)
            out_ref[:, 2 * RET_QK + h * RET_DV:2 * RET_QK + (h + 1) * RET_DV] = dv.astype(out_ref.dtype)
            ds = (_dot_nt(dob, vb[:, vs]) * dec).astype(BF16)
            dq_h = _dot(ds, km) + qwv[:, ps] * _dot_nt(dob, rp_ref[0, h].astype(BF16))
            dk_h = _dot_tn(ds, qm) + kwv[:, ps] * _dot_nt(vb[:, vs], snb)
            state[h] = cd_ref[:, vs] * sn_h + _dot_tn(_keep(qwb[:, ps], mine), dob)
            if h % 2 == 0:
                dq2, dk2 = dq_h, dk_h
            else:
                dq_s[:, ps] = dq2 + dq_h
                dk_s[:, ps] = dk2 + dk_h
        out_ref[:, 0:RET_QK] = _rot(dq_s[...], cs, -sn).astype(out_ref.dtype)
        out_ref[:, RET_QK:2 * RET_QK] = (_rot(dk_s[...], cs, -sn) * (RET_DK ** -0.5)).astype(out_ref.dtype)

        @pl.when(pl.program_id(0) == nc - 1)
        def _():
            side.wait(side_in, side_out, sems)

    rev = lambda i: nc - 1 - i
    const2 = lambda shape: pl.BlockSpec(shape, lambda i: (0,) * len(shape))
    return pl.pallas_call(
        body, name="retention_bwd", grid=(nc,),
        in_specs=[pl.BlockSpec((c, RET_QK), lambda i: (rev(i), 0)), pl.BlockSpec((c, RET_QK), lambda i: (rev(i), 0)),
                  pl.BlockSpec((c, D_MODEL), lambda i: (rev(i), 1)), pl.BlockSpec((c, D_MODEL), lambda i: (rev(i), 2)),
                  pl.BlockSpec((c, RET_QK), lambda i: (rev(i), 0)), pl.BlockSpec((c, RET_QK), lambda i: (rev(i), 0)),
                  const2((RET_HEADS, c, c)), const2((c, RET_QK)), const2((c, RET_QK)), const2((1, D_MODEL)),
                  const2((1, D_MODEL)),
                  pl.BlockSpec((c, D_MODEL), lambda i: (rev(i), 0)),
                  pl.BlockSpec((1, RET_HEADS, 2 * RET_DK, RET_DV), lambda i: (rev(i), 0, 0, 0)),
                  pl.BlockSpec((c, D_MODEL), lambda i: (rev(i), 0))] + side.in_specs,
        out_specs=(pl.BlockSpec((c, 2 * RET_QK + 2 * D_MODEL), lambda i: (rev(i), 0)), const2((1, D_MODEL)),
                   *side.out_specs),
        out_shape=(jax.ShapeDtypeStruct((l, 2 * RET_QK + 4 * D_MODEL), BF16), jax.ShapeDtypeStruct((1, D_MODEL), F32),
                   *side.landing),
        scratch_shapes=[pltpu.VMEM((RET_HEADS, 2 * RET_DK, RET_DV), F32), pltpu.VMEM((c, RET_QK), F32),
                        pltpu.VMEM((c, RET_QK), F32)] + side.scratch,
        compiler_params=_cp("arbitrary"),
    )(qb_saved, kb_saved, proj, proj, cos_t, sin_t, decay, qw, kw, cd_row, gn_g, o_saved, r_prev_saved, dmix,
      *side.srcs)


def _zoh(a_re, a_im, log_dt):
    dt = jnp.exp(log_dt)
    mag = jnp.exp(a_re * dt)
    abar_re = mag * jnp.cos(a_im * dt)
    abar_im = mag * jnp.sin(a_im * dt)
    den = a_re * a_re + a_im * a_im
    nr, ni = abar_re - 1.0, abar_im
    f_re = (nr * a_re + ni * a_im) / den
    f_im = (ni * a_re - nr * a_im) / den
    return dt, abar_re, abar_im, f_re, f_im, den


def _lanes_p(f):
    return jnp.tile(f, (1, S5_P))


def _s5_discretize(a_re, a_im, log_dt, b_re_t, b_im_t):
    def body(ar_ref, ai_ref, ld_ref, br_ref, bi_ref, abr_ref, abi_ref, bbr_ref, bbi_ref):
        _, abar_re, abar_im, f_re, f_im, _ = _zoh(ar_ref[...], ai_ref[...], ld_ref[...])
        abr_ref[...] = abar_re
        abi_ref[...] = abar_im
        fr, fi = _lanes_p(f_re), _lanes_p(f_im)
        bbr_ref[...] = fr * br_ref[...] - fi * bi_ref[...]
        bbi_ref[...] = fr * bi_ref[...] + fi * br_ref[...]

    gn = jax.ShapeDtypeStruct((S5_G, S5_N), F32)
    gpn = jax.ShapeDtypeStruct((S5_G, S5_P * S5_N), F32)
    return pl.pallas_call(body, name="s5_discretize", out_shape=(gn, gn, gpn, gpn))(a_re, a_im, log_dt, b_re_t, b_im_t)


def _s5_discretize_bwd(a_re, a_im, log_dt, b_re_t, b_im_t, dab_re, dab_im, dbb_re_t, dbb_im_t):
    def body(ar_ref, ai_ref, ld_ref, br_ref, bi_ref, gar_ref, gai_ref, gbr_ref, gbi_ref,
             dar_ref, dai_ref, dld_ref, dbr_ref, dbi_ref):
        a_r, a_i = ar_ref[...], ai_ref[...]
        dt, abar_re, abar_im, f_re, f_im, den = _zoh(a_r, a_i, ld_ref[...])
        b_r, b_i, g_br, g_bi = br_ref[...], bi_ref[...], gbr_ref[...], gbi_ref[...]
        fr, fi = _lanes_p(f_re), _lanes_p(f_im)
        dbr_ref[...] = fr * g_br + fi * g_bi
        dbi_ref[...] = fr * g_bi - fi * g_br
        t_r = b_r * g_br + b_i * g_bi
        t_i = b_r * g_bi - b_i * g_br
        gf_r = sum(t_r[:, p * S5_N:(p + 1) * S5_N] for p in range(S5_P))
        gf_i = sum(t_i[:, p * S5_N:(p + 1) * S5_N] for p in range(S5_P))
        inv_r, inv_i = a_r / den, a_i / den
        ga_r = gar_ref[...] + gf_r * inv_r - gf_i * inv_i
        ga_i = gai_ref[...] + gf_r * inv_i + gf_i * inv_r
        q_r = -(f_re * a_r + f_im * a_i) / den
        q_i = -(f_im * a_r - f_re * a_i) / den
        gl_r = q_r * gf_r + q_i * gf_i
        gl_i = q_r * gf_i - q_i * gf_r
        dar_ref[...] = gl_r + dt * (abar_re * ga_r + abar_im * ga_i)
        dai_ref[...] = gl_i + dt * (abar_re * ga_i - abar_im * ga_r)
        la_r = a_r * abar_re - a_i * abar_im
        la_i = a_r * abar_im + a_i * abar_re
        dld_ref[...] = dt * jnp.sum(ga_r * la_r + ga_i * la_i, axis=-1, keepdims=True)

    gn = jax.ShapeDtypeStruct((S5_G, S5_N), F32)
    gpn = jax.ShapeDtypeStruct((S5_G, S5_P * S5_N), F32)
    return pl.pallas_call(
        body, name="s5_discretize_bwd", out_shape=(gn, gn, jax.ShapeDtypeStruct((S5_G, 1), F32), gpn, gpn),
    )(a_re, a_im, log_dt, b_re_t, b_im_t, dab_re, dab_im, dbb_re_t, dbb_im_t)


S5_ZQ = S5_NB // 2


def _s5_z(re, im):
    return jnp.concatenate([re.reshape(S5_ZQ, 8, 128), im.reshape(S5_ZQ, 8, 128)], axis=0)


def _s5_unz(z):
    return z[:S5_ZQ].reshape(S5_G, S5_N), z[S5_ZQ:].reshape(S5_G, S5_N)


def _s5_block_mats(bb_re, bb_im, c_re, c_im):
    eye = jnp.eye(S5_GB, dtype=F32)
    bb = jnp.stack([bb_re, bb_im], axis=0).reshape(2, S5_NB, S5_GB, S5_N, S5_P)
    bbm = jnp.einsum("rbgnp,gh->bgprhn", bb, eye).reshape(S5_NB, S5_GB * S5_P, 2 * S5_BS)
    cc = jnp.stack([c_re, -c_im], axis=0).reshape(2, S5_NB, S5_GB, S5_P, S5_N)
    ccm = jnp.einsum("rbgpn,gh->brhngp", cc, eye).reshape(S5_NB, 2 * S5_BS, S5_GB * S5_P)
    return bbm.astype(BF16), ccm.astype(BF16)


def _s5_block_diag_bb(m):
    t = m.reshape(S5_NB, S5_GB, S5_P, 2, S5_GB, S5_N)
    d = jnp.einsum("bgprgn->rbgnp", t).reshape(2, S5_G, S5_N, S5_P)
    return d[0], d[1]


def _s5_block_diag_cc(m):
    t = m.reshape(S5_NB, 2, S5_GB, S5_N, S5_GB, S5_P)
    d = jnp.einsum("brgngp->rbgpn", t).reshape(2, S5_G, S5_P, S5_N)
    return d[0], -d[1]


SCAN_UNROLL = 8


def _z_store(zr, zi, blk, res, t, off):
    q, h = blk // 2, blk % 2
    for lt in range(4):
        zr[q, pl.ds(off + 4 * h + lt, t, stride=8), :] = res[:, lt * 128:(lt + 1) * 128]
        zi[q, pl.ds(off + 4 * h + lt, t, stride=8), :] = res[:, S5_BS + lt * 128:S5_BS + (lt + 1) * 128]


def _z_load(zr, zi, blk, t, off):
    q, h = blk // 2, blk % 2
    return jnp.concatenate([zr[q, pl.ds(off + 4 * h + lt, t, stride=8), :] for lt in range(4)]
                           + [zi[q, pl.ds(off + 4 * h + lt, t, stride=8), :] for lt in range(4)], axis=1)


def _z_scan_fwd(zr, zi, a_ref, carry_ref, t, off):
    ar = [a_ref[q] for q in range(S5_ZQ)]
    ai = [a_ref[S5_ZQ + q] for q in range(S5_ZQ)]

    def step(it, carry):
        carry = list(carry)
        base = pl.multiple_of(it * (8 * SCAN_UNROLL), 8 * SCAN_UNROLL) + off
        for tt in range(SCAN_UNROLL):
            rows = pl.ds(base + 8 * tt, 8)
            for q in range(S5_ZQ):
                c_r, c_i = carry[q], carry[S5_ZQ + q]
                n_r = ar[q] * c_r - ai[q] * c_i + zr[q, rows, :]
                n_i = ar[q] * c_i + ai[q] * c_r + zi[q, rows, :]
                zr[q, rows, :] = n_r
                zi[q, rows, :] = n_i
                carry[q], carry[S5_ZQ + q] = n_r, n_i
        return tuple(carry)

    out = lax.fori_loop(0, t // SCAN_UNROLL, step, tuple(carry_ref[k] for k in range(2 * S5_ZQ)))
    for k in range(2 * S5_ZQ):
        carry_ref[k] = out[k]


def _z_scan_bwd(lr, li, xr, xi, a_ref, carry_ref, acc_ref, t):
    ar = [a_ref[q] for q in range(S5_ZQ)]
    ai = [a_ref[S5_ZQ + q] for q in range(S5_ZQ)]
    n_it = t // SCAN_UNROLL

    def step(it, state):
        carry, acc = list(state[0]), list(state[1])
        base = pl.multiple_of((n_it - 1 - it) * (8 * SCAN_UNROLL), 8 * SCAN_UNROLL)
        for tt in reversed(range(SCAN_UNROLL)):
            rows = pl.ds(base + 8 * tt, 8)
            for q in range(S5_ZQ):
                c_r, c_i = carry[q], carry[S5_ZQ + q]
                n_r = ar[q] * c_r + ai[q] * c_i + lr[q, rows, :]
                n_i = ar[q] * c_i - ai[q] * c_r + li[q, rows, :]
                lr[q, rows, :] = n_r
                li[q, rows, :] = n_i
                p_r, p_i = xr[q, rows, :], xi[q, rows, :]
                acc[q] = acc[q] + n_r * p_r + n_i * p_i
                acc[S5_ZQ + q] = acc[S5_ZQ + q] + n_i * p_r - n_r * p_i
                carry[q], carry[S5_ZQ + q] = n_r, n_i
        return tuple(carry), tuple(acc)

    k8 = range(2 * S5_ZQ)
    carry, acc = lax.fori_loop(0, n_it, step, (tuple(carry_ref[k] for k in k8), tuple(acc_ref[k] for k in k8)))
    for k in k8:
        carry_ref[k] = carry[k]
        acc_ref[k] = acc[k]


def _s5_fwd(proj, mix, bbm, ccm, d_row, glu_w, glu_b, tabs, t, side):
    l = proj.shape[0]
    nt = l // t
    n_in = 9

    def body(*refs):
        u_ref, gs_ref, bb_ref, cc_ref, d_ref, gw_ref, gb_ref, a_ref, _ = refs[:n_in]
        side_in = refs[n_in:n_in + len(side.srcs)]
        ssm_ref, xst_ref = refs[n_in + len(side.srcs):n_in + len(side.srcs) + 2]
        side_out = refs[n_in + len(side.srcs) + 2:n_in + len(side.srcs) + 2 + side.n]
        zr, zi, carry = refs[n_in + len(side.srcs) + 2 + side.n:n_in + len(side.srcs) + 5 + side.n]
        sems = refs[n_in + len(side.srcs) + 5 + side.n:]

        @pl.when(pl.program_id(0) == 0)
        def _():
            side.start(side_in, side_out, sems)
            carry[...] = jnp.zeros_like(carry)

        xst_ref[0] = carry[...]
        ub = u_ref[...]
        u = ub.astype(F32)
        for blk in range(S5_NB):
            _z_store(zr, zi, blk, _dot(ub[:, blk * 128:(blk + 1) * 128], bb_ref[blk]), t, 0)
        _z_scan_fwd(zr, zi, a_ref, carry, t, 0)
        ys = jnp.concatenate(
            [_dot(_z_load(zr, zi, blk, t, 0).astype(BF16), cc_ref[blk]) for blk in range(S5_NB)], axis=1)
        y2 = _gelu(ys + d_ref[...] * u)
        z = _dot(y2.astype(BF16), gw_ref[...]) + gb_ref[...]
        ssm_ref[...] = (y2 * _sigmoid(z) * _silu(gs_ref[...].astype(F32))).astype(ssm_ref.dtype)

        @pl.when(pl.program_id(0) == nt - 1)
        def _():
            side.wait(side_in, side_out, sems)

    const2 = lambda shape: pl.BlockSpec(shape, lambda i: (0,) * len(shape))
    zshape = (2 * S5_ZQ, 8, 128)
    return pl.pallas_call(
        body, name="s5_fwd", grid=(nt,),
        in_specs=[pl.BlockSpec((t, D_MODEL), lambda i: (i, 3)), pl.BlockSpec((t, D_MODEL), lambda i: (i, 4)),
                  const2(bbm.shape), const2(ccm.shape), const2((1, D_MODEL)), const2((D_MODEL, D_MODEL)),
                  const2((1, D_MODEL)), const2(zshape), pl.BlockSpec(memory_space=pl.ANY)] + side.in_specs,
        out_specs=(pl.BlockSpec((t, D_MODEL), lambda i: (i, 1)), pl.BlockSpec((1,) + zshape, lambda i: (i, 0, 0, 0)),
                   *side.out_specs),
        out_shape=(jax.ShapeDtypeStruct((l, 2 * D_MODEL), BF16), jax.ShapeDtypeStruct((nt,) + zshape, F32),
                   *side.landing),
        scratch_shapes=[pltpu.VMEM((S5_ZQ, 8 * t, 128), F32), pltpu.VMEM((S5_ZQ, 8 * t, 128), F32),
                        pltpu.VMEM(zshape, F32)] + side.scratch,
        input_output_aliases={8: 0},
        compiler_params=_cp("arbitrary"),
    )(proj, proj, bbm, ccm, d_row, glu_w, glu_b, tabs, mix, *side.srcs)


def _s5_bwd(proj, dmix, dproj, xstart, bbm, ccm, d_row, glu_w, glu_b, tabs, t):
    l = proj.shape[0]
    nt = l // t
    col0 = 2 * RET_QK + 2 * D_MODEL

    def body(u_ref, gs_ref, dm_ref, xst_ref, bb_ref, cc_ref, d_ref, gw_ref, gb_ref, a_ref, _,
             dp_ref, y2_ref, dz_ref, dbb_ref, dcc_ref, da_ref, dd_ref, dgb_ref, xr, xi, lr, li, carry, lcarry,
             dug_s, dug_sem):
        step = pl.program_id(0)
        slot = step % 2
        dug_ref = dug_s.at[slot]

        def put(s, at_step):
            rows = pl.ds(pl.multiple_of((nt - 1 - at_step) * t, t), t)
            return pltpu.make_async_copy(dug_s.at[s], dp_ref.at[rows, pl.ds(col0, 2 * D_MODEL)], dug_sem.at[s])

        @pl.when(step >= 2)
        def _():
            put(slot, step - 2).wait()

        @pl.when(step == 0)
        def _():
            lcarry[...] = jnp.zeros_like(lcarry)
            dbb_ref[...] = jnp.zeros_like(dbb_ref)
            dcc_ref[...] = jnp.zeros_like(dcc_ref)
            da_ref[...] = jnp.zeros_like(da_ref)
            dd_ref[...] = jnp.zeros_like(dd_ref)
            dgb_ref[...] = jnp.zeros_like(dgb_ref)

        carry[...] = xst_ref[0]
        for q in range(S5_ZQ):
            xr[q, 0:8, :] = carry[q]
            xi[q, 0:8, :] = carry[S5_ZQ + q]
        ub = u_ref[...]
        u = ub.astype(F32)
        for blk in range(S5_NB):
            _z_store(xr, xi, blk, _dot(ub[:, blk * 128:(blk + 1) * 128], bb_ref[blk]), t, 8)
        _z_scan_fwd(xr, xi, a_ref, carry, t, 8)
        ys = jnp.concatenate(
            [_dot(_z_load(xr, xi, blk, t, 8).astype(BF16), cc_ref[blk]) for blk in range(S5_NB)], axis=1)
        dv = d_ref[...]
        y1 = ys + dv * u
        y2, dgelu = _gelu_and_grad(y1)
        y2b = y2.astype(BF16)
        sg = _sigmoid(_dot(y2b, gw_ref[...]) + gb_ref[...])
        gs = gs_ref[...].astype(F32)
        dssm = dm_ref[...].astype(F32)
        dug_ref[:, D_MODEL:] = (dssm * (y2 * sg) * _dsilu(gs)).astype(dug_ref.dtype)
        dy3 = dssm * _silu(gs)
        dz = dy3 * y2 * sg * (1.0 - sg)
        dzb = dz.astype(BF16)
        y2_ref[...] = y2b
        dz_ref[...] = dzb
        dgb_ref[...] += jnp.sum(dz, axis=0, keepdims=True)
        dy1 = (dy3 * sg + _dot_nt(dzb, gw_ref[...])) * dgelu
        dd_ref[...] += jnp.sum(dy1 * u, axis=0, keepdims=True)
        dyb = dy1.astype(BF16)
        for blk in range(S5_NB):
            ch = slice(blk * 128, (blk + 1) * 128)
            _z_store(lr, li, blk, _dot_nt(dyb[:, ch], cc_ref[blk]), t, 0)
            dcc_ref[blk] += _dot_tn(_z_load(xr, xi, blk, t, 8).astype(BF16), dyb[:, ch])
        _z_scan_bwd(lr, li, xr, xi, a_ref, lcarry, da_ref, t)
        du = []
        for blk in range(S5_NB):
            lb = _z_load(lr, li, blk, t, 0).astype(BF16)
            du.append(_dot_nt(lb, bb_ref[blk]))
            dbb_ref[blk] += _dot_tn(ub[:, blk * 128:(blk + 1) * 128], lb)
        dug_ref[:, :D_MODEL] = (jnp.concatenate(du, axis=1) + dy1 * dv).astype(dug_ref.dtype)
        put(slot, step).start()

        @pl.when(step == nt - 1)
        def _():
            put(slot, step).wait()
            if nt > 1:
                put(1 - slot, step - 1).wait()

    rev = lambda i: nt - 1 - i
    const2 = lambda shape: pl.BlockSpec(shape, lambda i: (0,) * len(shape))
    row_out = lambda w: pl.BlockSpec((t, w), lambda i: (rev(i), 0))
    zshape = (2 * S5_ZQ, 8, 128)
    hbm = pl.BlockSpec(memory_space=pl.ANY)
    return pl.pallas_call(
        body, name="s5_bwd", grid=(nt,),
        in_specs=[pl.BlockSpec((t, D_MODEL), lambda i: (rev(i), 3)), pl.BlockSpec((t, D_MODEL), lambda i: (rev(i), 4)),
                  pl.BlockSpec((t, D_MODEL), lambda i: (rev(i), 1)),
                  pl.BlockSpec((1,) + zshape, lambda i: (rev(i), 0, 0, 0)),
                  const2(bbm.shape), const2(ccm.shape), const2((1, D_MODEL)), const2((D_MODEL, D_MODEL)),
                  const2((1, D_MODEL)), const2(zshape), hbm],
        out_specs=(hbm, row_out(D_MODEL), row_out(D_MODEL), const2(bbm.shape), const2(ccm.shape),
                   const2(zshape), const2((1, D_MODEL)), const2((1, D_MODEL))),
        out_shape=(jax.ShapeDtypeStruct(dproj.shape, BF16), jax.ShapeDtypeStruct((l, D_MODEL), BF16),
                   jax.ShapeDtypeStruct((l, D_MODEL), BF16), jax.ShapeDtypeStruct(bbm.shape, F32),
                   jax.ShapeDtypeStruct(ccm.shape, F32), jax.ShapeDtypeStruct(zshape, F32),
                   jax.ShapeDtypeStruct((1, D_MODEL), F32), jax.ShapeDtypeStruct((1, D_MODEL), F32)),
        scratch_shapes=[pltpu.VMEM((S5_ZQ, 8 * t + 8, 128), F32), pltpu.VMEM((S5_ZQ, 8 * t + 8, 128), F32),
                        pltpu.VMEM((S5_ZQ, 8 * t, 128), F32), pltpu.VMEM((S5_ZQ, 8 * t, 128), F32),
                        pltpu.VMEM(zshape, F32), pltpu.VMEM(zshape, F32),
                        pltpu.VMEM((2, t, 2 * D_MODEL), BF16), pltpu.SemaphoreType.DMA((2,))],
        input_output_aliases={10: 0},
        compiler_params=_cp("arbitrary"),
    )(proj, proj, dmix, xstart, bbm, ccm, d_row, glu_w, glu_b, tabs, dproj)


def _attn_probs(qh, kh):
    s = _dot_nt(qh, kh) * (XA_DH ** -0.5)
    e = jnp.exp(s - jnp.max(s, axis=-1, keepdims=True))
    return e / jnp.sum(e, axis=-1, keepdims=True)


def _attn_fwd(qa, ka, va):
    l = qa.shape[0]
    m = ka.shape[0]
    tl = _pick(l, (2048, 1024, 512, 256))

    def body(q_ref, k_ref, v_ref, o_ref):
        for h in range(XA_HEADS):
            hs = slice(h * XA_DH, (h + 1) * XA_DH)
            p = _attn_probs(q_ref[:, hs], k_ref[:, hs])
            o_ref[:, hs] = _dot(p.astype(BF16), v_ref[:, hs]).astype(o_ref.dtype)

    return pl.pallas_call(
        body, name="xattn_fwd", grid=(l // tl,),
        in_specs=[pl.BlockSpec((tl, D_MODEL), lambda i: (i, 0)), pl.BlockSpec((m, D_MODEL), lambda i: (0, 0)),
                  pl.BlockSpec((m, D_MODEL), lambda i: (0, 0))],
        out_specs=pl.BlockSpec((tl, D_MODEL), lambda i: (i, 0)),
        out_shape=jax.ShapeDtypeStruct((l, D_MODEL), BF16), compiler_params=_cp("parallel"),
    )(qa, ka, va)


def _attn_bwd(qa, ka, va, doa):
    l = qa.shape[0]
    m = ka.shape[0]
    tl = _pick(l, (2048, 1024, 512, 256))

    def body(q_ref, k_ref, v_ref, do_ref, dq_ref, dk_ref, dv_ref):
        @pl.when(pl.program_id(0) == 0)
        def _():
            dk_ref[...] = jnp.zeros_like(dk_ref)
            dv_ref[...] = jnp.zeros_like(dv_ref)

        for h in range(XA_HEADS):
            hs = slice(h * XA_DH, (h + 1) * XA_DH)
            qh, kh, vh, doh = q_ref[:, hs], k_ref[:, hs], v_ref[:, hs], do_ref[:, hs]
            p = _attn_probs(qh, kh)
            dv_ref[:, hs] += _dot_tn(p.astype(BF16), doh)
            dp = _dot_nt(doh, vh)
            ds = (p * (dp - jnp.sum(dp * p, axis=-1, keepdims=True)) * (XA_DH ** -0.5)).astype(BF16)
            dq_ref[:, hs] = _dot(ds, kh).astype(dq_ref.dtype)
            dk_ref[:, hs] += _dot_tn(ds, qh)

    row = pl.BlockSpec((tl, D_MODEL), lambda i: (i, 0))
    mem = pl.BlockSpec((m, D_MODEL), lambda i: (0, 0))
    return pl.pallas_call(
        body, name="xattn_bwd", grid=(l // tl,), in_specs=[row, mem, mem, row], out_specs=(row, mem, mem),
        out_shape=(jax.ShapeDtypeStruct((l, D_MODEL), BF16), jax.ShapeDtypeStruct((m, D_MODEL), F32),
                   jax.ShapeDtypeStruct((m, D_MODEL), F32)),
        compiler_params=_cp("arbitrary"),
    )(qa, ka, va, doa)


def _me_and_peers():
    x, y, c = lax.axis_index("x"), lax.axis_index("y"), lax.axis_index("c")
    flip = lambda v, bit: (1 - v) if bit else v
    peers = []
    for k in range(1, N_DEV):
        px, py, pc = flip(x, (k >> 2) & 1), flip(y, (k >> 1) & 1), flip(c, k & 1)
        peers.append(((px, py, pc), 4 * px + 2 * py + pc))
    return 4 * x + 2 * y + c, peers


class _SideJob:
    def __init__(self, srcs, landing, src_of, dst_of):
        self.srcs = list(srcs)
        self.landing = list(landing)
        self.n = len(self.landing)
        self.src_of, self.dst_of = src_of, dst_of
        hbm = pl.BlockSpec(memory_space=pl.ANY)
        self.in_specs = [hbm] * len(self.srcs)
        self.out_specs = [hbm] * self.n
        self.scratch = [pltpu.SemaphoreType.DMA((self.n * (N_DEV - 1),)), pltpu.SemaphoreType.DMA((self.n * (N_DEV - 1),)),
                        pltpu.SemaphoreType.DMA((self.n,))]

    def _copies(self, src_refs, out_refs, sems):
        send_sems, recv_sems, loc_sems = sems
        me, peers = _me_and_peers()
        local = [pltpu.make_async_copy(self.src_of(a, me, src_refs), self.dst_of(a, me, out_refs), loc_sems.at[a])
                 for a in range(self.n)]
        sends, recvs = [], []
        for k, (peer, peer_idx) in enumerate(peers):
            for a in range(self.n):
                s = self.n * k + a
                sends.append(pltpu.make_async_remote_copy(
                    src_ref=self.src_of(a, peer_idx, src_refs), dst_ref=self.dst_of(a, me, out_refs),
                    send_sem=send_sems.at[s], recv_sem=recv_sems.at[s], device_id=peer, device_id_type=MESH))
                recvs.append(pltpu.make_async_remote_copy(
                    src_ref=self.src_of(a, me, src_refs), dst_ref=self.dst_of(a, peer_idx, out_refs),
                    send_sem=send_sems.at[s], recv_sem=recv_sems.at[s], device_id=peer, device_id_type=MESH))
        return local, sends, recvs

    def start(self, src_refs, out_refs, sems):
        if not self.n:
            return
        local, sends, _ = self._copies(src_refs, out_refs, sems)
        for cp in local + sends:
            cp.start()

    def wait(self, src_refs, out_refs, sems):
        if not self.n:
            return
        local, sends, recvs = self._copies(src_refs, out_refs, sems)
        for cp in recvs:
            cp.wait_recv()
        for cp in sends:
            cp.wait_send()
        for cp in local:
            cp.wait()


def _gather_job(shards):
    return _SideJob(shards, [jax.ShapeDtypeStruct((N_DEV,) + s.shape, s.dtype) for s in shards],
                    src_of=lambda a, j, srcs: srcs[a], dst_of=lambda a, j, outs: outs[a].at[j])


def _scatter_job(grads):
    landing, parts = [], []
    for g in grads:
        if g.ndim == 3:
            landing.append(jax.ShapeDtypeStruct(g.shape, g.dtype))
            parts.append(None)
        else:
            r = g.shape[0] // N_DEV
            landing.append(jax.ShapeDtypeStruct((N_DEV, r, g.shape[1]), g.dtype))
            parts.append(r)

    def src_of(a, j, srcs):
        if parts[a] is None:
            return srcs[a].at[j]
        return srcs[a].at[pl.ds(pl.multiple_of(j * parts[a], 8), parts[a]), :]

    return _SideJob(grads, landing, src_of=src_of, dst_of=lambda a, j, outs: outs[a].at[j])


def _prologue(w_in_shard, row_shards, x, g, pos_col, inv_row):
    n_row = len(row_shards)
    l, d = x.shape
    tr = _pick(l, (1024, 512, 256))
    nt = l // tr
    mid = nt // 2

    def body(*refs):
        win_ref = refs[0]
        row_refs = refs[1:1 + n_row]
        x_ref, g_ref, p_ref, inv_ref = refs[1 + n_row:5 + n_row]
        out_win = refs[5 + n_row]
        row_outs = refs[6 + n_row:6 + 2 * n_row]
        h_ref, cos_ref, sin_ref = refs[6 + 2 * n_row:9 + 2 * n_row]
        win_b, send_sems, recv_sems, local_sem = refs[9 + 2 * n_row:]
        step = pl.program_id(0)
        cx, cy, cc = lax.axis_index("x"), lax.axis_index("y"), lax.axis_index("c")
        me, sibling = (cx, cy, cc), (cx, cy, 1 - cc)
        chips = [(1 - cx, cy), (cx, 1 - cy), (1 - cx, 1 - cy)]
        slot = lambda p: out_win.at[4 * p[0] + 2 * p[1] + p[2]]

        def copy(k, block, to, src=None):
            return pltpu.make_async_remote_copy(
                src_ref=slot(block) if src is None else src, dst_ref=slot(block), send_sem=send_sems.at[k],
                recv_sem=recv_sems.at[k], device_id=to, device_id_type=MESH)

        mine = pltpu.make_async_copy(win_b, slot(me), local_sem)
        first = [copy(0, me, sibling, src=win_b)]
        first += [copy(1 + j, me, (*chip, cc), src=win_b) for j, chip in enumerate(chips)]
        passed = [copy(4 + j, (*chip, cc), sibling) for j, chip in enumerate(chips)]

        @pl.when(step == 0)
        def _():
            win_b[...] = win_ref[...].astype(BF16)
            mine.start()
            for cp in first:
                cp.start()
            for r, o in zip(row_refs, row_outs):
                o[...] = r[...].astype(BF16)

        h_ref[...] = (_rms(x_ref[...])[1] * g_ref[...]).astype(h_ref.dtype)
        ang = p_ref[...].astype(F32) * inv_ref[...]
        lane = lax.broadcasted_iota(jnp.int32, ang.shape, 1)
        cos_ref[...] = jnp.tile(jnp.cos(ang), (1, RET_QK // 128))
        sin_ref[...] = jnp.tile(jnp.where((lane % RET_DK) < RET_DK // 2, -jnp.sin(ang), jnp.sin(ang)),
                                (1, RET_QK // 128))

        @pl.when(step == mid)
        def _():
            for j, chip in enumerate(chips):
                copy(1 + j, (*chip, cc), me).wait_recv()
                passed[j].start()

        @pl.when(step == nt - 1)
        def _():
            copy(0, sibling, me).wait_recv()
            for j, chip in enumerate(chips):
                copy(4 + j, (*chip, 1 - cc), me).wait_recv()
            for cp in first + passed:
                cp.wait_send()
            mine.wait()

    whole = lambda a: pl.BlockSpec(a.shape, lambda i: (0,) * a.ndim)
    rows = lambda w: pl.BlockSpec((tr, w), lambda i: (i, 0))
    return pl.pallas_call(
        body, name="prologue_allgather_w_in", grid=(nt,),
        in_specs=[whole(w_in_shard)] + [whole(r) for r in row_shards] + [rows(d), whole(g), rows(1), whole(inv_row)],
        out_specs=(pl.BlockSpec(memory_space=pl.ANY), *[whole(r) for r in row_shards], rows(d), rows(RET_QK),
                   rows(RET_QK)),
        out_shape=(jax.ShapeDtypeStruct((N_DEV,) + w_in_shard.shape, BF16),
                   *[jax.ShapeDtypeStruct(r.shape, BF16) for r in row_shards],
                   jax.ShapeDtypeStruct((l, d), BF16), jax.ShapeDtypeStruct((l, RET_QK), F32),
                   jax.ShapeDtypeStruct((l, RET_QK), F32)),
        scratch_shapes=[pltpu.VMEM(w_in_shard.shape, BF16), pltpu.SemaphoreType.DMA((N_DEV - 1,)),
                        pltpu.SemaphoreType.DMA((N_DEV - 1,)), pltpu.SemaphoreType.DMA],
        compiler_params=_cp("arbitrary"),
    )(w_in_shard, *row_shards, x, g, pos_col, inv_row)


def _allreduce_small(small):
    rows = SMALL_ROWS // N_DEV

    def body(x_ref, out_ref, land, send1, recv1, send2, recv2):
        me, peers = _me_and_peers()
        block = lambda j: pl.ds(pl.multiple_of(j * rows, 8), rows)

        def phase(src_of, dst_of, send_sems, recv_sems):
            sends = [pltpu.make_async_remote_copy(src_ref=src_of(pidx), dst_ref=dst_of(me), send_sem=send_sems.at[k],
                                                  recv_sem=recv_sems.at[k], device_id=peer, device_id_type=MESH)
                     for k, (peer, pidx) in enumerate(peers)]
            recvs = [pltpu.make_async_remote_copy(src_ref=src_of(me), dst_ref=dst_of(pidx), send_sem=send_sems.at[k],
                                                  recv_sem=recv_sems.at[k], device_id=peer, device_id_type=MESH)
                     for k, (peer, pidx) in enumerate(peers)]
            for cp in sends:
                cp.start()
            for cp in recvs:
                cp.wait_recv()
            for cp in sends:
                cp.wait_send()

        land[me] = x_ref[block(me), :]
        phase(lambda j: x_ref.at[block(j), :], lambda j: land.at[j], send1, recv1)
        total = land[0]
        for j in range(1, N_DEV):
            total = total + land[j]
        out_ref[block(me), :] = total
        phase(lambda j: out_ref.at[block(me), :], lambda j: out_ref.at[block(j), :], send2, recv2)

    vm = pl.BlockSpec(memory_space=pltpu.VMEM)
    return pl.pallas_call(
        body, name="allreduce_small", in_specs=[vm], out_specs=vm, out_shape=jax.ShapeDtypeStruct(small.shape, F32),
        scratch_shapes=[pltpu.VMEM((N_DEV, rows, D_MODEL), F32)] + [pltpu.SemaphoreType.DMA((N_DEV - 1,))] * 4,
    )(small)


def _adamw(name, got, w, m, v):
    r, c = w.shape
    n_slots = got.shape[0]
    tr = _pick(r, (256, 128, 64))

    def body(got_ref, w_ref, m_ref, v_ref, g_ref, d_ref, nm_ref, nv_ref):
        g = got_ref[0].astype(F32)
        for j in range(1, n_slots):
            g = g + got_ref[j].astype(F32)
        nm = ADAM_B1 * m_ref[...] + (1.0 - ADAM_B1) * g
        nv = ADAM_B2 * v_ref[...] + (1.0 - ADAM_B2) * jnp.square(g)
        m_hat = nm / (1.0 - ADAM_B1 ** ADAM_STEP)
        v_hat = nv / (1.0 - ADAM_B2 ** ADAM_STEP)
        g_ref[...] = g
        d_ref[...] = -ADAM_LR * (m_hat / (jnp.sqrt(v_hat) + ADAM_EPS) + ADAM_WD * w_ref[...])
        nm_ref[...] = nm
        nv_ref[...] = nv

    blk = pl.BlockSpec((tr, c), lambda i: (i, 0))
    out = jax.ShapeDtypeStruct((r, c), F32)
    return pl.pallas_call(
        body, name=name, grid=(r // tr,),
        in_specs=[pl.BlockSpec((n_slots, tr, c), lambda i: (0, i, 0)), blk, blk, blk],
        out_specs=(blk, blk, blk, blk), out_shape=(out, out, out, out), compiler_params=_cp("parallel"),
    )(got, w, m, v)


_SMALL_VECS = ("norm1_g", "ret_gn_g", "s5_d", "s5_glu_b", "norm2_g", "norm_mem_g", "norm_f_g")


def _small_layout():
    lay, row = {}, 0
    for n in _SMALL_VECS + ("loss",):
        lay[n] = (row, 1, D_MODEL)
        row += 1
    for n in ("s5_a_re", "s5_a_im"):
        lay[n] = (row, 4, D_MODEL)
        row += 4
    lay["s5_log_dt"] = (row, 1, S5_G)
    row += 8
    for n in ("s5_b_re", "s5_b_im", "s5_c_re", "s5_c_im"):
        lay[n] = (row, 64, D_MODEL)
        row += 64
    assert row <= SMALL_ROWS
    return lay


def _pack_small(t, loss_row=None):
    lay = _small_layout()
    pieces = [t[n].reshape(1, D_MODEL) for n in _SMALL_VECS]
    pieces.append(jnp.zeros((1, D_MODEL), F32) if loss_row is None else loss_row)
    pieces += [t["s5_a_re"].reshape(4, D_MODEL), t["s5_a_im"].reshape(4, D_MODEL)]
    pieces.append(jnp.pad(t["s5_log_dt"].reshape(1, S5_G), ((0, 7), (0, D_MODEL - S5_G))))
    pieces += [t[n].reshape(64, D_MODEL) for n in ("s5_b_re", "s5_b_im", "s5_c_re", "s5_c_im")]
    pieces.append(jnp.zeros((SMALL_ROWS - lay["s5_c_im"][0] - 64, D_MODEL), F32))
    return jnp.concatenate(pieces, axis=0)


def _adamw_small(g_sum, w, m, v):
    lay = _small_layout()
    names = [n for n in lay if n != "loss"]

    def body(g_ref, w_ref, m_ref, v_ref, *outs):
        g = g_ref[...]
        nm = ADAM_B1 * m_ref[...] + (1.0 - ADAM_B1) * g
        nv = ADAM_B2 * v_ref[...] + (1.0 - ADAM_B2) * jnp.square(g)
        m_hat = nm / (1.0 - ADAM_B1 ** ADAM_STEP)
        v_hat = nv / (1.0 - ADAM_B2 ** ADAM_STEP)
        delta = -ADAM_LR * (m_hat / (jnp.sqrt(v_hat) + ADAM_EPS) + ADAM_WD * w_ref[...])
        for i, n in enumerate(names):
            r0, rows, lanes = lay[n]
            for part, val in enumerate((g, delta, nm, nv)):
                outs[4 * i + part][...] = val[r0:r0 + rows, 0:lanes]
        r0 = lay["loss"][0]
        outs[-1][...] = g[r0:r0 + 1, :]

    shapes = []
    for n in names:
        shapes += [jax.ShapeDtypeStruct(lay[n][1:], F32)] * 4
    shapes.append(jax.ShapeDtypeStruct((1, D_MODEL), F32))
    outs = pl.pallas_call(body, name="adamw_small", out_shape=tuple(shapes),
                          compiler_params=pltpu.CompilerParams(vmem_limit_bytes=VMEM_LIMIT))(g_sum, w, m, v)
    return {n: tuple(outs[4 * i:4 * i + 4]) for i, n in enumerate(names)}, outs[-1]


_W_NAMES = ("norm1_g", "w_in", "ret_gn_g", "s5_a_re", "s5_a_im", "s5_log_dt", "s5_b_re", "s5_b_im", "s5_c_re", "s5_c_im",
            "s5_d", "s5_glu_w", "s5_glu_b", "w_out", "norm2_g", "norm_mem_g", "xa_wq", "xa_wk", "xa_wv", "xa_wo",
            "norm_f_g")
_ROW_NAMES = ("s5_glu_w", "w_out", "xa_wq", "xa_wk", "xa_wv", "xa_wo")


def kernel(x, mem, positions, norm1_g, w_in, ret_gn_g, s5_a_re, s5_a_im, s5_log_dt, s5_b_re, s5_b_im, s5_c_re, s5_c_im, s5_d, s5_glu_w, s5_glu_b, w_out, norm2_g, norm_mem_g, xa_wq, xa_wk, xa_wv, xa_wo, norm_f_g, loss_target, m_norm1_g, m_w_in, m_ret_gn_g, m_s5_a_re, m_s5_a_im, m_s5_log_dt, m_s5_b_re, m_s5_b_im, m_s5_c_re, m_s5_c_im, m_s5_d, m_s5_glu_w, m_s5_glu_b, m_w_out, m_norm2_g, m_norm_mem_g, m_xa_wq, m_xa_wk, m_xa_wv, m_xa_wo, m_norm_f_g, v_norm1_g, v_w_in, v_ret_gn_g, v_s5_a_re, v_s5_a_im, v_s5_log_dt, v_s5_b_re, v_s5_b_im, v_s5_c_re, v_s5_c_im, v_s5_d, v_s5_glu_w, v_s5_glu_b, v_w_out, v_norm2_g, v_norm_mem_g, v_xa_wq, v_xa_wk, v_xa_wv, v_xa_wo, v_norm_f_g):
    w = dict(norm1_g=norm1_g, w_in=w_in, ret_gn_g=ret_gn_g, s5_a_re=s5_a_re, s5_a_im=s5_a_im, s5_log_dt=s5_log_dt,
             s5_b_re=s5_b_re, s5_b_im=s5_b_im, s5_c_re=s5_c_re, s5_c_im=s5_c_im, s5_d=s5_d, s5_glu_w=s5_glu_w,
             s5_glu_b=s5_glu_b, w_out=w_out, norm2_g=norm2_g, norm_mem_g=norm_mem_g, xa_wq=xa_wq, xa_wk=xa_wk,
             xa_wv=xa_wv, xa_wo=xa_wo, norm_f_g=norm_f_g)
    mom = dict(norm1_g=m_norm1_g, w_in=m_w_in, ret_gn_g=m_ret_gn_g, s5_a_re=m_s5_a_re, s5_a_im=m_s5_a_im,
               s5_log_dt=m_s5_log_dt, s5_b_re=m_s5_b_re, s5_b_im=m_s5_b_im, s5_c_re=m_s5_c_re, s5_c_im=m_s5_c_im,
               s5_d=m_s5_d, s5_glu_w=m_s5_glu_w, s5_glu_b=m_s5_glu_b, w_out=m_w_out, norm2_g=m_norm2_g,
               norm_mem_g=m_norm_mem_g, xa_wq=m_xa_wq, xa_wk=m_xa_wk, xa_wv=m_xa_wv, xa_wo=m_xa_wo,
               norm_f_g=m_norm_f_g)
    var = dict(norm1_g=v_norm1_g, w_in=v_w_in, ret_gn_g=v_ret_gn_g, s5_a_re=v_s5_a_re, s5_a_im=v_s5_a_im,
               s5_log_dt=v_s5_log_dt, s5_b_re=v_s5_b_re, s5_b_im=v_s5_b_im, s5_c_re=v_s5_c_re, s5_c_im=v_s5_c_im,
               s5_d=v_s5_d, s5_glu_w=v_s5_glu_w, s5_glu_b=v_s5_glu_b, w_out=v_w_out, norm2_g=v_norm2_g,
               norm_mem_g=v_norm_mem_g, xa_wq=v_xa_wq, xa_wk=v_xa_wk, xa_wv=v_xa_wv, xa_wo=v_xa_wo,
               norm_f_g=v_norm_f_g)
    shapes = {n: w[n].shape for n in _W_NAMES}

    x2d, mem2d, tgt = x[0], mem[0], loss_target[0]
    l = x2d.shape[0]
    ret_c = _pick(l, (256, 128))
    s5_t = _pick(l, (256, 128))
    g1, g2, gm, gf = norm1_g, norm2_g, norm_mem_g, norm_f_g.reshape(1, D_MODEL)

    half = RET_DK // 2
    inv = ROPE_BASE ** (-jnp.arange(half, dtype=F32) / half)
    win_s, *rest = _prologue(w_in[0], [w[n][0] for n in _ROW_NAMES], x2d, g1, positions[0].reshape(l, 1),
                             jnp.tile(inv, 128 // half)[None, :])
    row_shards_b, (h1, cos_t, sin_t) = rest[:len(_ROW_NAMES)], rest[len(_ROW_NAMES):]

    to_gpn = lambda b: jnp.transpose(b, (0, 2, 1)).reshape(S5_G, S5_P * S5_N)
    from_gpn = lambda b: jnp.transpose(b.reshape(S5_G, S5_P, S5_N), (0, 2, 1))
    disc_args = (s5_a_re[0], s5_a_im[0], s5_log_dt[0].reshape(S5_G, 1), to_gpn(s5_b_re[0]), to_gpn(s5_b_im[0]))
    abar_re, abar_im, bb_re_t, bb_im_t = _s5_discretize(*disc_args)
    bbm, ccm = _s5_block_mats(from_gpn(bb_re_t), from_gpn(bb_im_t), s5_c_re[0], s5_c_im[0])
    a_z = _s5_z(abar_re, abar_im)

    proj, *rows_01 = _mm_nn_slots("in_proj", h1, win_s, BF16, side=_gather_job(row_shards_b[:2]))
    full = {n: g.reshape(N_DEV * r, D_MODEL) for n, g, r in zip(_ROW_NAMES[:2], rows_01, ROW_SHARDS[:2])}
    rconsts = _ret_constants(ret_c)
    ret, o_saved, r_prev, q_rot, k_rot = _ret_fwd(proj, cos_t, sin_t, rconsts, ret_gn_g, ret_c)
    mix, xstart, *rows_xa = _s5_fwd(proj, ret, bbm, ccm, s5_d, full["s5_glu_w"], s5_glu_b, a_z, s5_t,
                                    side=_gather_job(row_shards_b[2:]))
    full.update({n: g.reshape(N_DEV * r, D_MODEL) for n, g, r in zip(_ROW_NAMES[2:], rows_xa, ROW_SHARDS[2:])})
    x1, h2 = _mm_nn("out_proj", mix, full["w_out"], F32, residual=x2d, epi=_epi_norm_fwd(g2))
    mn = _rms_fwd("norm_mem_fwd", mem2d, gm)
    qa = _mm_nn("xa_q", h2, full["xa_wq"], BF16)
    ka = _mm_nn("xa_k", mn, full["xa_wk"], BF16)
    va = _mm_nn("xa_v", mn, full["xa_wv"], BF16)
    oa = _attn_fwd(qa, ka, va)
    dx2, dgf, loss_lanes = _mm_nn("xa_o", oa, full["xa_wo"], F32, residual=x1, epi=_epi_loss(gf, tgt))

    doa = _mm_nt("xa_o_dx", dx2, full["xa_wo"], BF16)
    dwo = _mm_tn("xa_o_dw", oa, dx2, BF16)
    dqa, dka, dva = _attn_bwd(qa, ka, va, doa)
    dx1, dg2 = _mm_nt("xa_q_dx", dqa, full["xa_wq"], F32, epi=_epi_norm_bwd(x1, g2, dx2))
    dwq = _mm_tn("xa_q_dw", h2, dqa, BF16)
    dwk = _mm_tn("xa_k_dw", mn, dka, BF16)
    dwv = _mm_tn("xa_v_dw", mn, dva, BF16)
    dmn = _mm_nt("xa_v_dx", dva, full["xa_wv"], F32, residual=_mm_nt("xa_k_dx", dka, full["xa_wk"], F32))
    _, dgm = _rms_bwd("norm_mem_bwd", mem2d, gm, dmn, None)
    dmix = _mm_nt("out_proj_dx", dx1, full["w_out"], BF16)
    dwout = _mm_tn("out_proj_dw", mix, dx1, BF16)
    dret, dgn, *got_a = _ret_bwd(proj, q_rot, k_rot, cos_t, sin_t, rconsts, ret_gn_g, o_saved, r_prev, dmix, ret_c,
                                 side=_scatter_job([dwout, dwq, dwk, dwv, dwo]))
    dproj, y2, dz, dbbm, dccm, dabar, dd, dgb = _s5_bwd(proj, dmix, dret, xstart, bbm, ccm, s5_d, full["s5_glu_w"],
                                                        s5_glu_b, a_z, s5_t)
    dglu = _mm_tn("s5_glu_dw", y2, dz, BF16)
    dwin_s, got_glu = _mm_tn_slots("in_proj_dw", h1, dproj, N_DEV, BF16, side=_scatter_job([dglu]))
    grad_x, dg1, got_win = _mm_nt_slots("in_proj_dx", dproj, win_s, F32, side=_scatter_job([dwin_s]),
                                        epi=_epi_norm_bwd(x2d, g1, dx1))

    dab_re, dab_im = _s5_unz(dabar)
    dbb_re, dbb_im = _s5_block_diag_bb(dbbm)
    dc_re, dc_im = _s5_block_diag_cc(dccm)
    da_re, da_im, dlog_dt, db_re_t, db_im_t = _s5_discretize_bwd(*disc_args, dab_re, dab_im, to_gpn(dbb_re),
                                                                 to_gpn(dbb_im))
    db_re, db_im = from_gpn(db_re_t), from_gpn(db_im_t)
    small_g = dict(norm1_g=dg1, ret_gn_g=dgn, s5_d=dd, s5_glu_b=dgb, norm2_g=dg2, norm_mem_g=dgm, norm_f_g=dgf,
                   s5_a_re=da_re, s5_a_im=da_im, s5_log_dt=dlog_dt, s5_b_re=db_re, s5_b_im=db_im, s5_c_re=dc_re,
                   s5_c_im=dc_im)
    small_pack = _pack_small(small_g, loss_row=loss_lanes)

    res = {}
    got = dict(zip(("w_out", "xa_wq", "xa_wk", "xa_wv", "xa_wo"), got_a), w_in=got_win, s5_glu_w=got_glu)
    for n in ("w_in",) + _ROW_NAMES:
        res[n] = _adamw("adamw_" + n, got[n], w[n][0], mom[n][0], var[n][0])
    small_sum = _allreduce_small(small_pack)
    small_res, loss_sum = _adamw_small(small_sum, _pack_small(w), _pack_small(mom), _pack_small(var))
    loss = (0.5 / D_MODEL) * jnp.sum(loss_sum)
    res.update(small_res)

    outs = [loss, grad_x[None]]
    for part in range(4):
        for n in _W_NAMES:
            outs.append(res[n][part].reshape(shapes[n]))
    return tuple(outs)
```

```python
import jax
import jax.numpy as jnp
from jax import lax
from jax.experimental import pallas as pl
from jax.experimental.pallas import tpu as pltpu

F32 = jnp.float32
BF16 = jnp.bfloat16
MESH = pl.DeviceIdType.MESH

D_MODEL = 1024
RET_HEADS, RET_DK, RET_DV = 8, 64, 128
RET_QK = RET_HEADS * RET_DK
S5_G, S5_N, S5_P = 64, 64, 16
S5_NB = 8
S5_GB = S5_G // S5_NB
S5_BS = S5_GB * S5_N
S5_COLS = 2 * S5_G * S5_N
XA_HEADS, XA_DH = 4, 256
EPS = 1e-6
ROPE_BASE = 10000.0
N_DEV = 8
W_IN_SHARD = 640
ROW_SHARDS = (128, 256, 128, 128, 128, 128)
ROWPACK = sum(ROW_SHARDS)
SMALL_ROWS = 320
ADAM_LR, ADAM_B1, ADAM_B2, ADAM_EPS, ADAM_WD, ADAM_STEP = 0.001, 0.9, 0.999, 1e-08, 0.01, 10

VMEM_LIMIT = 56 * 1024 * 1024


def _cp(*sem):
    return pltpu.CompilerParams(dimension_semantics=tuple(sem), vmem_limit_bytes=VMEM_LIMIT)


def _dot(a, b):
    return jnp.dot(a, b, preferred_element_type=F32)


def _dot_nt(a, b):
    return lax.dot_general(a, b, (((1,), (1,)), ((), ())), preferred_element_type=F32)


def _dot_tn(a, b):
    return lax.dot_general(a, b, (((0,), (0,)), ((), ())), preferred_element_type=F32)


def _sigmoid(x):
    return 1.0 / (1.0 + jnp.exp(-x))


def _silu(x):
    return x * _sigmoid(x)


def _dsilu(x):
    s = _sigmoid(x)
    return s * (1.0 + x * (1.0 - s))


_GELU_C = 0.7978845608028654


def _gelu(x):
    return 0.5 * x * (1.0 + jnp.tanh(_GELU_C * (x + 0.044715 * (x * x * x))))


def _gelu_and_grad(x):
    t = jnp.tanh(_GELU_C * (x + 0.044715 * (x * x * x)))
    half = 0.5 * (1.0 + t)
    return x * half, half + 0.5 * x * (1.0 - t * t) * (_GELU_C * (1.0 + 3.0 * 0.044715 * (x * x)))


def _pick(n, cands):
    for c in cands:
        if n % c == 0:
            return c
    return n


class _Epilogue:
    def __init__(self, rows, vecs, row_out_dtypes, n_sums, fn):
        self.rows, self.vecs, self.row_out_dtypes, self.n_sums, self.fn = list(rows), list(vecs), list(row_out_dtypes), n_sums, fn


def _rms(x):
    rs = lax.rsqrt(jnp.mean(x * x, axis=-1, keepdims=True) + EPS)
    return rs, x * rs


def _rms_dx(dn, xn, rs):
    return rs * (dn - xn * jnp.mean(dn * xn, axis=-1, keepdims=True))


def _epi_norm_fwd(g):
    def fn(r, rows, vecs):
        return r, [_rms(r)[1] * vecs[0]], []

    return _Epilogue([], [g], [BF16], 0, fn)


def _epi_loss(gf, target):
    def fn(r, rows, vecs):
        rs, xn = _rms(r)
        e = xn * vecs[0] - rows[0]
        dy = e * (1.0 / r.shape[-1])
        return (_rms_dx(dy * vecs[0], xn, rs), [],
                [jnp.sum(dy * xn, axis=0, keepdims=True), jnp.sum(e * e, axis=0, keepdims=True)])

    return _Epilogue([target], [gf], [], 2, fn)


def _epi_norm_bwd(x, g, dres):
    def fn(r, rows, vecs):
        rs, xn = _rms(rows[0])
        return _rms_dx(r * vecs[0], xn, rs) + rows[1], [], [jnp.sum(r * xn, axis=0, keepdims=True)]

    return _Epilogue([x, dres], [g], [], 1, fn)


def _mm_core(name, operands, in_specs, out_spec, out_shape, grid, nk, dims, acc_shape, has_res, side=None, epi=None):
    n_in = 3 if has_res else 2
    n_epi_in = len(epi.rows) + len(epi.vecs) if epi else 0
    n_epi_out = len(epi.row_out_dtypes) + epi.n_sums if epi else 0
    n_side_in = len(side.srcs) if side else 0
    n_side_out = side.n if side else 0

    def body(*refs):
        a_ref, b_ref = refs[0], refs[1]
        r_ref = refs[2] if has_res else None
        epi_in = refs[n_in:n_in + n_epi_in]
        side_in = refs[n_in + n_epi_in:n_in + n_epi_in + n_side_in]
        n0 = n_in + n_epi_in + n_side_in
        o_ref = refs[n0]
        epi_out = refs[n0 + 1:n0 + 1 + n_epi_out]
        side_out = refs[n0 + 1 + n_epi_out:n0 + 1 + n_epi_out + n_side_out]
        rest = refs[n0 + 1 + n_epi_out + n_side_out:]
        acc, sems = (rest[0], rest[1:]) if nk > 1 else (None, rest)
        i, j, k = pl.program_id(0), pl.program_id(1), pl.program_id(2)
        if side:
            @pl.when((i == 0) & (j == 0) & (k == 0))
            def _():
                side.start(side_in, side_out, sems)

        def product():
            if len(b_ref.shape) == 3:
                ns = b_ref.shape[2]
                return sum(lax.dot_general(a_ref[:, p * ns:(p + 1) * ns].astype(BF16), b_ref[p].astype(BF16),
                                           (dims, ((), ())), preferred_element_type=F32)
                           for p in range(b_ref.shape[0]))
            return lax.dot_general(a_ref[...].astype(BF16), b_ref[...].astype(BF16), (dims, ((), ())),
                                   preferred_element_type=F32)

        def finish(r):
            if has_res:
                r = r + r_ref[...]
            if epi is None:
                o_ref[...] = r.astype(o_ref.dtype)
                return
            n_rows = len(epi.rows)
            main, row_vals, sums = epi.fn(r, [t[...] for t in epi_in[:n_rows]], [t[...] for t in epi_in[n_rows:]])
            o_ref[...] = main.astype(o_ref.dtype)
            for ref, val in zip(epi_out, row_vals):
                ref[...] = val.astype(ref.dtype)
            for ref, val in zip(epi_out[len(row_vals):], sums):
                @pl.when(i == 0)
                def _(ref=ref):
                    ref[...] = jnp.zeros_like(ref)

                ref[...] += val

        if nk == 1:
            finish(product())
        else:
            @pl.when(k == 0)
            def _():
                acc[...] = jnp.zeros_like(acc)

            acc[...] += product()

            @pl.when(k == nk - 1)
            def _():
                finish(acc[...])

        if side:
            @pl.when((i == grid[0] - 1) & (j == grid[1] - 1) & (k == grid[2] - 1))
            def _():
                side.wait(side_in, side_out, sems)

    acc_scratch = [pltpu.VMEM(acc_shape, F32)] if nk > 1 else []
    in_specs, out_specs, out_shapes, operands = list(in_specs), [out_spec], [out_shape], list(operands)
    if epi:
        assert grid[1] == 1, "an epilogue needs tiles that span whole rows"
        tm, n = out_spec.block_shape
        row_spec = pl.BlockSpec((tm, n), lambda i, j, k: (i, 0))
        vec_spec = pl.BlockSpec((1, n), lambda i, j, k: (0, 0))
        in_specs += [row_spec] * len(epi.rows) + [vec_spec] * len(epi.vecs)
        operands += epi.rows + epi.vecs
        out_specs += [row_spec] * len(epi.row_out_dtypes) + [vec_spec] * epi.n_sums
        out_shapes += [jax.ShapeDtypeStruct(out_shape.shape, d) for d in epi.row_out_dtypes]
        out_shapes += [jax.ShapeDtypeStruct((1, n), F32)] * epi.n_sums
    scratch = acc_scratch
    if side:
        in_specs += side.in_specs
        operands += side.srcs
        out_specs += side.out_specs
        out_shapes += side.landing
        scratch = acc_scratch + side.scratch
    plain = side is None and epi is None
    res = pl.pallas_call(
        body, name=name, grid=grid, in_specs=in_specs, out_specs=out_specs[0] if plain else tuple(out_specs),
        out_shape=out_shapes[0] if plain else tuple(out_shapes), scratch_shapes=scratch,
        compiler_params=_cp("parallel", "parallel", "arbitrary") if plain else _cp("arbitrary", "arbitrary", "arbitrary"),
    )(*operands)
    return res


def _row_tiles(residual, epi):
    return (2048, 1024, 512, 256) if residual is None and epi is None else (1024, 512, 256)


def _mm_nn(name, a, b, out_dtype, residual=None, epi=None):
    m, kk = a.shape
    n = b.shape[1]
    tm, tn, tk = _pick(m, _row_tiles(residual, epi)), _pick(n, (1024, 512)), _pick(kk, (2048, 1024, 512))
    ops = [a, b]
    specs = [pl.BlockSpec((tm, tk), lambda i, j, k: (i, k)), pl.BlockSpec((tk, tn), lambda i, j, k: (k, j))]
    if residual is not None:
        ops.append(residual)
        specs.append(pl.BlockSpec((tm, tn), lambda i, j, k: (i, j)))
    return _mm_core(name, ops, specs, pl.BlockSpec((tm, tn), lambda i, j, k: (i, j)),
                    jax.ShapeDtypeStruct((m, n), out_dtype), (m // tm, n // tn, kk // tk), kk // tk,
                    ((1,), (0,)), (tm, tn), residual is not None, epi=epi)


def _mm_nt(name, a, b, out_dtype, residual=None, epi=None):
    m, kk = a.shape
    n = b.shape[0]
    tm, tn, tk = _pick(m, _row_tiles(residual, epi)), _pick(n, (2048, 1024, 512)), _pick(kk, (2048, 1024, 512))
    ops = [a, b]
    specs = [pl.BlockSpec((tm, tk), lambda i, j, k: (i, k)), pl.BlockSpec((tn, tk), lambda i, j, k: (j, k))]
    if residual is not None:
        ops.append(residual)
        specs.append(pl.BlockSpec((tm, tn), lambda i, j, k: (i, j)))
    return _mm_core(name, ops, specs, pl.BlockSpec((tm, tn), lambda i, j, k: (i, j)),
                    jax.ShapeDtypeStruct((m, n), out_dtype), (m // tm, n // tn, kk // tk), kk // tk,
                    ((1,), (1,)), (tm, tn), residual is not None, epi=epi)


def _mm_tn(name, a, b, out_dtype):
    kk, m = a.shape
    n = b.shape[1]
    k_tiles = (4096, 2048, 1024, 512, 256) if a.dtype == BF16 and b.dtype == BF16 else (2048, 1024, 512, 256)
    tm, tn, tk = _pick(m, (1024, 512)), _pick(n, (1024, 512)), _pick(kk, k_tiles)
    specs = [pl.BlockSpec((tk, tm), lambda i, j, k: (k, i)), pl.BlockSpec((tk, tn), lambda i, j, k: (k, j))]
    return _mm_core(name, [a, b], specs, pl.BlockSpec((tm, tn), lambda i, j, k: (i, j)),
                    jax.ShapeDtypeStruct((m, n), out_dtype), (m // tm, n // tn, kk // tk), kk // tk,
                    ((0,), (0,)), (tm, tn), False)


def _mm_nn_slots(name, a, b_slots, out_dtype, side=None):
    m, kk = a.shape
    s, _, ns = b_slots.shape
    tm, tk = _pick(m, (4096, 2048, 1024, 512, 256)), _pick(kk, (1024, 512))
    specs = [pl.BlockSpec((tm, tk), lambda i, j, k: (i, k)), pl.BlockSpec((None, tk, ns), lambda i, j, k: (j, k, 0))]
    return _mm_core(name, [a, b_slots], specs, pl.BlockSpec((tm, ns), lambda i, j, k: (i, j)),
                    jax.ShapeDtypeStruct((m, s * ns), out_dtype), (m // tm, s, kk // tk), kk // tk,
                    ((1,), (0,)), (tm, ns), False, side)


def _mm_nt_slots(name, a, b_slots, out_dtype, side=None, epi=None):
    m = a.shape[0]
    s, n, ns = b_slots.shape
    tm, tn = _pick(m, (1024, 512, 256)), _pick(n, (1024, 512))
    per = _pick(s, (2, 1))
    specs = [pl.BlockSpec((tm, per * ns), lambda i, j, k: (i, k)),
             pl.BlockSpec((per, tn, ns), lambda i, j, k: (k, j, 0))]
    return _mm_core(name, [a, b_slots], specs, pl.BlockSpec((tm, tn), lambda i, j, k: (i, j)),
                    jax.ShapeDtypeStruct((m, n), out_dtype), (m // tm, n // tn, s // per), s // per,
                    ((1,), (1,)), (tm, tn), False, side, epi)


def _mm_tn_slots(name, a, b, s, out_dtype, side=None):
    kk, m = a.shape
    ns = b.shape[1] // s
    tm, tk = _pick(m, (1024, 512)), _pick(kk, (4096, 2048, 1024, 512, 256))
    specs = [pl.BlockSpec((tk, tm), lambda i, j, k: (k, i)), pl.BlockSpec((tk, ns), lambda i, j, k: (k, j))]
    return _mm_core(name, [a, b], specs, pl.BlockSpec((None, tm, ns), lambda i, j, k: (j, i, 0)),
                    jax.ShapeDtypeStruct((s, m, ns), out_dtype), (m // tm, s, kk // tk), kk // tk,
                    ((0,), (0,)), (tm, ns), False, side)


def _rms_fwd(name, x, g):
    r, d = x.shape
    tr = _pick(r, (1024, 512, 256))

    def body(x_ref, g_ref, o_ref):
        xv = x_ref[...]
        rs = lax.rsqrt(jnp.mean(xv * xv, axis=-1, keepdims=True) + EPS)
        o_ref[...] = (xv * rs * g_ref[...]).astype(o_ref.dtype)

    return pl.pallas_call(
        body, name=name, grid=(r // tr,),
        in_specs=[pl.BlockSpec((tr, d), lambda i: (i, 0)), pl.BlockSpec((1, d), lambda i: (0, 0))],
        out_specs=pl.BlockSpec((tr, d), lambda i: (i, 0)),
        out_shape=jax.ShapeDtypeStruct((r, d), BF16), compiler_params=_cp("parallel"),
    )(x, g)


def _rms_bwd(name, x, g, dh, dres):
    r, d = x.shape
    tr = _pick(r, (512, 256))
    has_res = dres is not None

    def body(*refs):
        if has_res:
            x_ref, g_ref, dh_ref, dr_ref, dx_ref, dg_ref = refs
        else:
            x_ref, g_ref, dh_ref, dx_ref, dg_ref = refs
        i = pl.program_id(0)

        @pl.when(i == 0)
        def _():
            dg_ref[...] = jnp.zeros_like(dg_ref)

        xv = x_ref[...]
        dhv = dh_ref[...].astype(F32)
        rs = lax.rsqrt(jnp.mean(xv * xv, axis=-1, keepdims=True) + EPS)
        xn = xv * rs
        dg_ref[...] += jnp.sum(dhv * xn, axis=0, keepdims=True)
        dn = dhv * g_ref[...]
        dx = rs * (dn - xn * jnp.mean(dn * xn, axis=-1, keepdims=True))
        if has_res:
            dx = dx + dr_ref[...]
        dx_ref[...] = dx

    row = pl.BlockSpec((tr, d), lambda i: (i, 0))
    vec = pl.BlockSpec((1, d), lambda i: (0, 0))
    ops = [x, g, dh] + ([dres] if has_res else [])
    return pl.pallas_call(
        body, name=name, grid=(r // tr,),
        in_specs=[row, vec, row] + ([row] if has_res else []),
        out_specs=(row, vec),
        out_shape=(jax.ShapeDtypeStruct((r, d), F32), jax.ShapeDtypeStruct((1, d), F32)),
        compiler_params=_cp("arbitrary"),
    )(*ops)


def _rot(x, cos_t, sin_t):
    n = x.shape[-1]
    lane = lax.broadcasted_iota(jnp.int32, x.shape, 1)
    partner = jnp.where((lane % RET_DK) < RET_DK // 2, pltpu.roll(x, n - RET_DK // 2, 1), pltpu.roll(x, RET_DK // 2, 1))
    return x * cos_t + partner * sin_t


def _ret_constants(c):
    log_g = jnp.log1p(-jnp.exp2(-5.0 - jnp.arange(RET_HEADS, dtype=F32)))
    j = jnp.arange(c, dtype=F32)
    diff = j[:, None] - j[None, :]
    decay = jnp.where(diff[None] >= 0.0, jnp.exp(log_g[:, None, None] * jnp.maximum(diff, 0.0)[None]), 0.0)
    q_w = jnp.exp(log_g[None, :] * (j + 1.0)[:, None])
    k_w = jnp.exp(log_g[None, :] * (c - 1.0 - j)[:, None])
    cd = jnp.exp(log_g * c)
    rep = lambda t: jnp.repeat(t, RET_DK, axis=1)
    cd_row = jnp.repeat(cd, RET_DV)[None, :]
    return decay, rep(q_w), rep(k_w), cd_row


def _pair_of(h, c):
    lane = lax.broadcasted_iota(jnp.int32, (c, 2 * RET_DK), 1)
    mine = (lane < RET_DK) if h % 2 == 0 else (lane >= RET_DK)
    return slice((h // 2) * 2 * RET_DK, (h // 2 + 1) * 2 * RET_DK), mine


def _keep(x, mine):
    return jnp.where(mine, x, jnp.zeros_like(x))


def _ret_fwd(proj, cos_t, sin_t, consts, gn_g, c):
    l = proj.shape[0]
    nc = l // c
    decay, qw, kw, cd_row = consts

    def body(q_ref, k_ref, v_ref, g_ref, cos_ref, sin_ref, dec_ref, qw_ref, kw_ref, cd_ref, gn_ref,
             ret_ref, o_ref, rp_ref, qb_ref, kb_ref, state):
        @pl.when(pl.program_id(0) == 0)
        def _():
            state[...] = jnp.zeros_like(state)

        cs, sn = cos_ref[...], sin_ref[...]
        qr = _rot(q_ref[...].astype(F32), cs, sn)
        kr = _rot(k_ref[...].astype(F32), cs, sn) * (RET_DK ** -0.5)
        qb, kb = qr.astype(BF16), kr.astype(BF16)
        qb_ref[...] = qb
        kb_ref[...] = kb
        qwb = (qr * qw_ref[...]).astype(BF16)
        kwb = (kr * kw_ref[...]).astype(BF16)
        vb = v_ref[...].astype(BF16)
        for h in range(RET_HEADS):
            ps, mine = _pair_of(h, c)
            vs = slice(h * RET_DV, (h + 1) * RET_DV)
            s = _dot_nt(_keep(qb[:, ps], mine), kb[:, ps]) * dec_ref[h]
            r_prev = state[h]
            rp_ref[0, h] = r_prev
            o = _dot(s.astype(BF16), vb[:, vs]) + _dot(_keep(qwb[:, ps], mine), r_prev.astype(BF16))
            state[h] = cd_ref[:, vs] * r_prev + _dot_tn(_keep(kwb[:, ps], mine), vb[:, vs])
            o_ref[:, vs] = o
            mu = jnp.mean(o, axis=-1, keepdims=True)
            var = jnp.mean(jnp.square(o - mu), axis=-1, keepdims=True)
            on = (o - mu) * lax.rsqrt(var + EPS)
            ret_ref[:, vs] = (on * gn_ref[:, vs] * _silu(g_ref[:, vs].astype(F32))).astype(ret_ref.dtype)

    const2 = lambda shape: pl.BlockSpec(shape, lambda i: (0,) * len(shape))
    return pl.pallas_call(
        body, name="retention_fwd", grid=(nc,),
        in_specs=[pl.BlockSpec((c, RET_QK), lambda i: (i, 0)), pl.BlockSpec((c, RET_QK), lambda i: (i, 1)),
                  pl.BlockSpec((c, D_MODEL), lambda i: (i, 1)), pl.BlockSpec((c, D_MODEL), lambda i: (i, 2)),
                  pl.BlockSpec((c, RET_QK), lambda i: (i, 0)), pl.BlockSpec((c, RET_QK), lambda i: (i, 0)),
                  const2((RET_HEADS, c, c)), const2((c, RET_QK)), const2((c, RET_QK)), const2((1, D_MODEL)),
                  const2((1, D_MODEL))],
        out_specs=(pl.BlockSpec((c, D_MODEL), lambda i: (i, 0)), pl.BlockSpec((c, D_MODEL), lambda i: (i, 0)),
                   pl.BlockSpec((1, RET_HEADS, 2 * RET_DK, RET_DV), lambda i: (i, 0, 0, 0)),
                   pl.BlockSpec((c, RET_QK), lambda i: (i, 0)), pl.BlockSpec((c, RET_QK), lambda i: (i, 0))),
        out_shape=(jax.ShapeDtypeStruct((l, 2 * D_MODEL), BF16), jax.ShapeDtypeStruct((l, D_MODEL), F32),
                   jax.ShapeDtypeStruct((nc, RET_HEADS, 2 * RET_DK, RET_DV), F32),
                   jax.ShapeDtypeStruct((l, RET_QK), BF16), jax.ShapeDtypeStruct((l, RET_QK), BF16)),
        scratch_shapes=[pltpu.VMEM((RET_HEADS, 2 * RET_DK, RET_DV), F32)],
        compiler_params=_cp("arbitrary"),
    )(proj, proj, proj, proj, cos_t, sin_t, decay, qw, kw, cd_row, gn_g)


def _ret_bwd(proj, qb_saved, kb_saved, cos_t, sin_t, consts, gn_g, o_saved, r_prev_saved, dmix, c, side):
    l = proj.shape[0]
    nc = l // c
    decay, qw, kw, cd_row = consts
    n_in = 14

    def body(*refs):
        (q_ref, k_ref, v_ref, g_ref, cos_ref, sin_ref, dec_ref, qw_ref, kw_ref, cd_ref, gn_ref, o_ref, rp_ref,
         dr_ref) = refs[:n_in]
        side_in = refs[n_in:n_in + len(side.srcs)]
        out_ref, dgn_ref = refs[n_in + len(side.srcs):n_in + len(side.srcs) + 2]
        side_out = refs[n_in + len(side.srcs) + 2:n_in + len(side.srcs) + 2 + side.n]
        state, dq_s, dk_s = refs[n_in + len(side.srcs) + 2 + side.n:n_in + len(side.srcs) + 5 + side.n]
        sems = refs[n_in + len(side.srcs) + 5 + side.n:]

        @pl.when(pl.program_id(0) == 0)
        def _():
            side.start(side_in, side_out, sems)
            state[...] = jnp.zeros_like(state)
            dgn_ref[...] = jnp.zeros_like(dgn_ref)

        cs, sn = cos_ref[...], sin_ref[...]
        qb, kb = q_ref[...], k_ref[...]
        qwv, kwv = qw_ref[...], kw_ref[...]
        qwb = (qb.astype(F32) * qwv).astype(BF16)
        kwb = (kb.astype(F32) * kwv).astype(BF16)
        vb = v_ref[...].astype(BF16)
        dq2 = dk2 = None
        for h in range(RET_HEADS):
            ps, mine = _pair_of(h, c)
            vs = slice(h * RET_DV, (h + 1) * RET_DV)
            dec = dec_ref[h]
            qm, km = _keep(qb[:, ps], mine), _keep(kb[:, ps], mine)
            o = o_ref[:, vs]
            mu = jnp.mean(o, axis=-1, keepdims=True)
            var = jnp.mean(jnp.square(o - mu), axis=-1, keepdims=True)
            rstd = lax.rsqrt(var + EPS)
            on = (o - mu) * rstd
            gate = g_ref[:, vs].astype(F32)
            sg = _silu(gate)
            dret = dr_ref[:, vs].astype(F32)
            gn = gn_ref[:, vs]
            dgn_ref[:, vs] += jnp.sum(dret * on * sg, axis=0, keepdims=True)
            out_ref[:, 2 * RET_QK + D_MODEL + h * RET_DV:2 * RET_QK + D_MODEL + (h + 1) * RET_DV] = (
                dret * on * gn * _dsilu(gate)).astype(out_ref.dtype)
            don = dret * gn * sg
            do = rstd * (don - jnp.mean(don, axis=-1, keepdims=True)
                         - on * jnp.mean(don * on, axis=-1, keepdims=True))
            dob = do.astype(BF16)
            sn_h = state[h]
            snb = sn_h.astype(BF16)
            s = _dot_nt(qm, kb[:, ps]) * dec
            dv = _dot_tn(s.astype(BF16), dob) + _dot(_keep(kwb[:, ps], mine), snb)
            out_ref[:, 2 * RET_QK + h * RET_DV:2 * RET_QK + (h + 1) * RET_DV] = dv.astype(out_ref.dtype)
            ds = (_dot_nt(dob, vb[:, vs]) * dec).astype(BF16)
            dq_h = _dot(ds, km) + qwv[:, ps] * _dot_nt(dob, rp_ref[0, h].astype(BF16))
            dk_h = _dot_tn(ds, qm) + kwv[:, ps] * _dot_nt(vb[:, vs], snb)
            state[h] = cd_ref[:, vs] * sn_h + _dot_tn(_keep(qwb[:, ps], mine), dob)
            if h % 2 == 0:
                dq2, dk2 = dq_h, dk_h
            else:
                dq_s[:, ps] = dq2 + dq_h
                dk_s[:, ps] = dk2 + dk_h
        out_ref[:, 0:RET_QK] = _rot(dq_s[...], cs, -sn).astype(out_ref.dtype)
        out_ref[:, RET_QK:2 * RET_QK] = (_rot(dk_s[...], cs, -sn) * (RET_DK ** -0.5)).astype(out_ref.dtype)

        @pl.when(pl.program_id(0) == nc - 1)
        def _():
            side.wait(side_in, side_out, sems)

    rev = lambda i: nc - 1 - i
    const2 = lambda shape: pl.BlockSpec(shape, lambda i: (0,) * len(shape))
    return pl.pallas_call(
        body, name="retention_bwd", grid=(nc,),
        in_specs=[pl.BlockSpec((c, RET_QK), lambda i: (rev(i), 0)), pl.BlockSpec((c, RET_QK), lambda i: (rev(i), 0)),
                  pl.BlockSpec((c, D_MODEL), lambda i: (rev(i), 1)), pl.BlockSpec((c, D_MODEL), lambda i: (rev(i), 2)),
                  pl.BlockSpec((c, RET_QK), lambda i: (rev(i), 0)), pl.BlockSpec((c, RET_QK), lambda i: (rev(i), 0)),
                  const2((RET_HEADS, c, c)), const2((c, RET_QK)), const2((c, RET_QK)), const2((1, D_MODEL)),
                  const2((1, D_MODEL)),
                  pl.BlockSpec((c, D_MODEL), lambda i: (rev(i), 0)),
                  pl.BlockSpec((1, RET_HEADS, 2 * RET_DK, RET_DV), lambda i: (rev(i), 0, 0, 0)),
                  pl.BlockSpec((c, D_MODEL), lambda i: (rev(i), 0))] + side.in_specs,
        out_specs=(pl.BlockSpec((c, 2 * RET_QK + 2 * D_MODEL), lambda i: (rev(i), 0)), const2((1, D_MODEL)),
                   *side.out_specs),
        out_shape=(jax.ShapeDtypeStruct((l, 2 * RET_QK + 4 * D_MODEL), BF16), jax.ShapeDtypeStruct((1, D_MODEL), F32),
                   *side.landing),
        scratch_shapes=[pltpu.VMEM((RET_HEADS, 2 * RET_DK, RET_DV), F32), pltpu.VMEM((c, RET_QK), F32),
                        pltpu.VMEM((c, RET_QK), F32)] + side.scratch,
        compiler_params=_cp("arbitrary"),
    )(qb_saved, kb_saved, proj, proj, cos_t, sin_t, decay, qw, kw, cd_row, gn_g, o_saved, r_prev_saved, dmix,
      *side.srcs)


def _zoh(a_re, a_im, log_dt):
    dt = jnp.exp(log_dt)
    mag = jnp.exp(a_re * dt)
    abar_re = mag * jnp.cos(a_im * dt)
    abar_im = mag * jnp.sin(a_im * dt)
    den = a_re * a_re + a_im * a_im
    nr, ni = abar_re - 1.0, abar_im
    f_re = (nr * a_re + ni * a_im) / den
    f_im = (ni * a_re - nr * a_im) / den
    return dt, abar_re, abar_im, f_re, f_im, den


def _lanes_p(f):
    return jnp.tile(f, (1, S5_P))


def _s5_discretize(a_re, a_im, log_dt, b_re_t, b_im_t):
    def body(ar_ref, ai_ref, ld_ref, br_ref, bi_ref, abr_ref, abi_ref, bbr_ref, bbi_ref):
        _, abar_re, abar_im, f_re, f_im, _ = _zoh(ar_ref[...], ai_ref[...], ld_ref[...])
        abr_ref[...] = abar_re
        abi_ref[...] = abar_im
        fr, fi = _lanes_p(f_re), _lanes_p(f_im)
        bbr_ref[...] = fr * br_ref[...] - fi * bi_ref[...]
        bbi_ref[...] = fr * bi_ref[...] + fi * br_ref[...]

    gn = jax.ShapeDtypeStruct((S5_G, S5_N), F32)
    gpn = jax.ShapeDtypeStruct((S5_G, S5_P * S5_N), F32)
    return pl.pallas_call(body, name="s5_discretize", out_shape=(gn, gn, gpn, gpn))(a_re, a_im, log_dt, b_re_t, b_im_t)


def _s5_discretize_bwd(a_re, a_im, log_dt, b_re_t, b_im_t, dab_re, dab_im, dbb_re_t, dbb_im_t):
    def body(ar_ref, ai_ref, ld_ref, br_ref, bi_ref, gar_ref, gai_ref, gbr_ref, gbi_ref,
             dar_ref, dai_ref, dld_ref, dbr_ref, dbi_ref):
        a_r, a_i = ar_ref[...], ai_ref[...]
        dt, abar_re, abar_im, f_re, f_im, den = _zoh(a_r, a_i, ld_ref[...])
        b_r, b_i, g_br, g_bi = br_ref[...], bi_ref[...], gbr_ref[...], gbi_ref[...]
        fr, fi = _lanes_p(f_re), _lanes_p(f_im)
        dbr_ref[...] = fr * g_br + fi * g_bi
        dbi_ref[...] = fr * g_bi - fi * g_br
        t_r = b_r * g_br + b_i * g_bi
        t_i = b_r * g_bi - b_i * g_br
        gf_r = sum(t_r[:, p * S5_N:(p + 1) * S5_N] for p in range(S5_P))
        gf_i = sum(t_i[:, p * S5_N:(p + 1) * S5_N] for p in range(S5_P))
        inv_r, inv_i = a_r / den, a_i / den
        ga_r = gar_ref[...] + gf_r * inv_r - gf_i * inv_i
        ga_i = gai_ref[...] + gf_r * inv_i + gf_i * inv_r
        q_r = -(f_re * a_r + f_im * a_i) / den
        q_i = -(f_im * a_r - f_re * a_i) / den
        gl_r = q_r * gf_r + q_i * gf_i
        gl_i = q_r * gf_i - q_i * gf_r
        dar_ref[...] = gl_r + dt * (abar_re * ga_r + abar_im * ga_i)
        dai_ref[...] = gl_i + dt * (abar_re * ga_i - abar_im * ga_r)
        la_r = a_r * abar_re - a_i * abar_im
        la_i = a_r * abar_im + a_i * abar_re
        dld_ref[...] = dt * jnp.sum(ga_r * la_r + ga_i * la_i, axis=-1, keepdims=True)

    gn = jax.ShapeDtypeStruct((S5_G, S5_N), F32)
    gpn = jax.ShapeDtypeStruct((S5_G, S5_P * S5_N), F32)
    return pl.pallas_call(
        body, name="s5_discretize_bwd", out_shape=(gn, gn, jax.ShapeDtypeStruct((S5_G, 1), F32), gpn, gpn),
    )(a_re, a_im, log_dt, b_re_t, b_im_t, dab_re, dab_im, dbb_re_t, dbb_im_t)


S5_ZQ = S5_NB // 2


def _s5_z(re, im):
    return jnp.concatenate([re.reshape(S5_ZQ, 8, 128), im.reshape(S5_ZQ, 8, 128)], axis=0)


def _s5_unz(z):
    return z[:S5_ZQ].reshape(S5_G, S5_N), z[S5_ZQ:].reshape(S5_G, S5_N)


def _s5_block_mats(bb_re, bb_im, c_re, c_im):
    eye = jnp.eye(S5_GB, dtype=F32)
    bb = jnp.stack([bb_re, bb_im], axis=0).reshape(2, S5_NB, S5_GB, S5_N, S5_P)
    bbm = jnp.einsum("rbgnp,gh->bgprhn", bb, eye).reshape(S5_NB, S5_GB * S5_P, 2 * S5_BS)
    cc = jnp.stack([c_re, -c_im], axis=0).reshape(2, S5_NB, S5_GB, S5_P, S5_N)
    ccm = jnp.einsum("rbgpn,gh->brhngp", cc, eye).reshape(S5_NB, 2 * S5_BS, S5_GB * S5_P)
    return bbm.astype(BF16), ccm.astype(BF16)


def _s5_block_diag_bb(m):
    t = m.reshape(S5_NB, S5_GB, S5_P, 2, S5_GB, S5_N)
    d = jnp.einsum("bgprgn->rbgnp", t).reshape(2, S5_G, S5_N, S5_P)
    return d[0], d[1]


def _s5_block_diag_cc(m):
    t = m.reshape(S5_NB, 2, S5_GB, S5_N, S5_GB, S5_P)
    d = jnp.einsum("brgngp->rbgpn", t).reshape(2, S5_G, S5_P, S5_N)
    return d[0], -d[1]


SCAN_UNROLL = 8


def _z_store(zr, zi, blk, res, t, off):
    q, h = blk // 2, blk % 2
    for lt in range(4):
        zr[q, pl.ds(off + 4 * h + lt, t, stride=8), :] = res[:, lt * 128:(lt + 1) * 128]
        zi[q, pl.ds(off + 4 * h + lt, t, stride=8), :] = res[:, S5_BS + lt * 128:S5_BS + (lt + 1) * 128]


def _z_load(zr, zi, blk, t, off):
    q, h = blk // 2, blk % 2
    return jnp.concatenate([zr[q, pl.ds(off + 4 * h + lt, t, stride=8), :] for lt in range(4)]
                           + [zi[q, pl.ds(off + 4 * h + lt, t, stride=8), :] for lt in range(4)], axis=1)


def _z_scan_fwd(zr, zi, a_ref, carry_ref, t, off):
    ar = [a_ref[q] for q in range(S5_ZQ)]
    ai = [a_ref[S5_ZQ + q] for q in range(S5_ZQ)]

    def step(it, carry):
        carry = list(carry)
        base = pl.multiple_of(it * (8 * SCAN_UNROLL), 8 * SCAN_UNROLL) + off
        for tt in range(SCAN_UNROLL):
            rows = pl.ds(base + 8 * tt, 8)
            for q in range(S5_ZQ):
                c_r, c_i = carry[q], carry[S5_ZQ + q]
                n_r = ar[q] * c_r - ai[q] * c_i + zr[q, rows, :]
                n_i = ar[q] * c_i + ai[q] * c_r + zi[q, rows, :]
                zr[q, rows, :] = n_r
                zi[q, rows, :] = n_i
                carry[q], carry[S5_ZQ + q] = n_r, n_i
        return tuple(carry)

    out = lax.fori_loop(0, t // SCAN_UNROLL, step, tuple(carry_ref[k] for k in range(2 * S5_ZQ)))
    for k in range(2 * S5_ZQ):
        carry_ref[k] = out[k]


def _z_scan_bwd(lr, li, xr, xi, a_ref, carry_ref, acc_ref, t):
    ar = [a_ref[q] for q in range(S5_ZQ)]
    ai = [a_ref[S5_ZQ + q] for q in range(S5_ZQ)]
    n_it = t // SCAN_UNROLL

    def step(it, state):
        carry, acc = list(state[0]), list(state[1])
        base = pl.multiple_of((n_it - 1 - it) * (8 * SCAN_UNROLL), 8 * SCAN_UNROLL)
        for tt in reversed(range(SCAN_UNROLL)):
            rows = pl.ds(base + 8 * tt, 8)
            for q in range(S5_ZQ):
                c_r, c_i = carry[q], carry[S5_ZQ + q]
                n_r = ar[q] * c_r + ai[q] * c_i + lr[q, rows, :]
                n_i = ar[q] * c_i - ai[q] * c_r + li[q, rows, :]
                lr[q, rows, :] = n_r
                li[q, rows, :] = n_i
                p_r, p_i = xr[q, rows, :], xi[q, rows, :]
                acc[q] = acc[q] + n_r * p_r + n_i * p_i
                acc[S5_ZQ + q] = acc[S5_ZQ + q] + n_i * p_r - n_r * p_i
                carry[q], carry[S5_ZQ + q] = n_r, n_i
        return tuple(carry), tuple(acc)

    k8 = range(2 * S5_ZQ)
    carry, acc = lax.fori_loop(0, n_it, step, (tuple(carry_ref[k] for k in k8), tuple(acc_ref[k] for k in k8)))
    for k in k8:
        carry_ref[k] = carry[k]
        acc_ref[k] = acc[k]


def _s5_fwd(proj, mix, bbm, ccm, d_row, glu_w, glu_b, tabs, t, side):
    l = proj.shape[0]
    nt = l // t
    n_in = 9

    def body(*refs):
        u_ref, gs_ref, bb_ref, cc_ref, d_ref, gw_ref, gb_ref, a_ref, _ = refs[:n_in]
        side_in = refs[n_in:n_in + len(side.srcs)]
        ssm_ref, xst_ref = refs[n_in + len(side.srcs):n_in + len(side.srcs) + 2]
        side_out = refs[n_in + len(side.srcs) + 2:n_in + len(side.srcs) + 2 + side.n]
        zr, zi, carry = refs[n_in + len(side.srcs) + 2 + side.n:n_in + len(side.srcs) + 5 + side.n]
        sems = refs[n_in + len(side.srcs) + 5 + side.n:]

        @pl.when(pl.program_id(0) == 0)
        def _():
            side.start(side_in, side_out, sems)
            carry[...] = jnp.zeros_like(carry)

        xst_ref[0] = carry[...]
        ub = u_ref[...]
        u = ub.astype(F32)
        for blk in range(S5_NB):
            _z_store(zr, zi, blk, _dot(ub[:, blk * 128:(blk + 1) * 128], bb_ref[blk]), t, 0)
        _z_scan_fwd(zr, zi, a_ref, carry, t, 0)
        ys = jnp.concatenate(
            [_dot(_z_load(zr, zi, blk, t, 0).astype(BF16), cc_ref[blk]) for blk in range(S5_NB)], axis=1)
        y2 = _gelu(ys + d_ref[...] * u)
        z = _dot(y2.astype(BF16), gw_ref[...]) + gb_ref[...]
        ssm_ref[...] = (y2 * _sigmoid(z) * _silu(gs_ref[...].astype(F32))).astype(ssm_ref.dtype)

        @pl.when(pl.program_id(0) == nt - 1)
        def _():
            side.wait(side_in, side_out, sems)

    const2 = lambda shape: pl.BlockSpec(shape, lambda i: (0,) * len(shape))
    zshape = (2 * S5_ZQ, 8, 128)
    return pl.pallas_call(
        body, name="s5_fwd", grid=(nt,),
        in_specs=[pl.BlockSpec((t, D_MODEL), lambda i: (i, 3)), pl.BlockSpec((t, D_MODEL), lambda i: (i, 4)),
                  const2(bbm.shape), const2(ccm.shape), const2((1, D_MODEL)), const2((D_MODEL, D_MODEL)),
                  const2((1, D_MODEL)), const2(zshape), pl.BlockSpec(memory_space=pl.ANY)] + side.in_specs,
        out_specs=(pl.BlockSpec((t, D_MODEL), lambda i: (i, 1)), pl.BlockSpec((1,) + zshape, lambda i: (i, 0, 0, 0)),
                   *side.out_specs),
        out_shape=(jax.ShapeDtypeStruct((l, 2 * D_MODEL), BF16), jax.ShapeDtypeStruct((nt,) + zshape, F32),
                   *side.landing),
        scratch_shapes=[pltpu.VMEM((S5_ZQ, 8 * t, 128), F32), pltpu.VMEM((S5_ZQ, 8 * t, 128), F32),
                        pltpu.VMEM(zshape, F32)] + side.scratch,
        input_output_aliases={8: 0},
        compiler_params=_cp("arbitrary"),
    )(proj, proj, bbm, ccm, d_row, glu_w, glu_b, tabs, mix, *side.srcs)


def _s5_bwd(proj, dmix, dproj, xstart, bbm, ccm, d_row, glu_w, glu_b, tabs, t):
    l = proj.shape[0]
    nt = l // t
    col0 = 2 * RET_QK + 2 * D_MODEL

    def body(u_ref, gs_ref, dm_ref, xst_ref, bb_ref, cc_ref, d_ref, gw_ref, gb_ref, a_ref, _,
             dp_ref, y2_ref, dz_ref, dbb_ref, dcc_ref, da_ref, dd_ref, dgb_ref, xr, xi, lr, li, carry, lcarry,
             dug_s, dug_sem):
        step = pl.program_id(0)
        slot = step % 2
        dug_ref = dug_s.at[slot]

        def put(s, at_step):
            rows = pl.ds(pl.multiple_of((nt - 1 - at_step) * t, t), t)
            return pltpu.make_async_copy(dug_s.at[s], dp_ref.at[rows, pl.ds(col0, 2 * D_MODEL)], dug_sem.at[s])

        @pl.when(step >= 2)
        def _():
            put(slot, step - 2).wait()

        @pl.when(step == 0)
        def _():
            lcarry[...] = jnp.zeros_like(lcarry)
            dbb_ref[...] = jnp.zeros_like(dbb_ref)
            dcc_ref[...] = jnp.zeros_like(dcc_ref)
            da_ref[...] = jnp.zeros_like(da_ref)
            dd_ref[...] = jnp.zeros_like(dd_ref)
            dgb_ref[...] = jnp.zeros_like(dgb_ref)

        carry[...] = xst_ref[0]
        for q in range(S5_ZQ):
            xr[q, 0:8, :] = carry[q]
            xi[q, 0:8, :] = carry[S5_ZQ + q]
        ub = u_ref[...]
        u = ub.astype(F32)
        for blk in range(S5_NB):
            _z_store(xr, xi, blk, _dot(ub[:, blk * 128:(blk + 1) * 128], bb_ref[blk]), t, 8)
        _z_scan_fwd(xr, xi, a_ref, carry, t, 8)
        ys = jnp.concatenate(
            [_dot(_z_load(xr, xi, blk, t, 8).astype(BF16), cc_ref[blk]) for blk in range(S5_NB)], axis=1)
        dv = d_ref[...]
        y1 = ys + dv * u
        y2, dgelu = _gelu_and_grad(y1)
        y2b = y2.astype(BF16)
        sg = _sigmoid(_dot(y2b, gw_ref[...]) + gb_ref[...])
        gs = gs_ref[...].astype(F32)
        dssm = dm_ref[...].astype(F32)
        dug_ref[:, D_MODEL:] = (dssm * (y2 * sg) * _dsilu(gs)).astype(dug_ref.dtype)
        dy3 = dssm * _silu(gs)
        dz = dy3 * y2 * sg * (1.0 - sg)
        dzb = dz.astype(BF16)
        y2_ref[...] = y2b
        dz_ref[...] = dzb
        dgb_ref[...] += jnp.sum(dz, axis=0, keepdims=True)
        dy1 = (dy3 * sg + _dot_nt(dzb, gw_ref[...])) * dgelu
        dd_ref[...] += jnp.sum(dy1 * u, axis=0, keepdims=True)
        dyb = dy1.astype(BF16)
        for blk in range(S5_NB):
            ch = slice(blk * 128, (blk + 1) * 128)
            _z_store(lr, li, blk, _dot_nt(dyb[:, ch], cc_ref[blk]), t, 0)
            dcc_ref[blk] += _dot_tn(_z_load(xr, xi, blk, t, 8).astype(BF16), dyb[:, ch])
        _z_scan_bwd(lr, li, xr, xi, a_ref, lcarry, da_ref, t)
        du = []
        for blk in range(S5_NB):
            lb = _z_load(lr, li, blk, t, 0).astype(BF16)
            du.append(_dot_nt(lb, bb_ref[blk]))
            dbb_ref[blk] += _dot_tn(ub[:, blk * 128:(blk + 1) * 128], lb)
        dug_ref[:, :D_MODEL] = (jnp.concatenate(du, axis=1) + dy1 * dv).astype(dug_ref.dtype)
        put(slot, step).start()

        @pl.when(step == nt - 1)
        def _():
            put(slot, step).wait()
            if nt > 1:
                put(1 - slot, step - 1).wait()

    rev = lambda i: nt - 1 - i
    const2 = lambda shape: pl.BlockSpec(shape, lambda i: (0,) * len(shape))
    row_out = lambda w: pl.BlockSpec((t, w), lambda i: (rev(i), 0))
    zshape = (2 * S5_ZQ, 8, 128)
    hbm = pl.BlockSpec(memory_space=pl.ANY)
    return pl.pallas_call(
        body, name="s5_bwd", grid=(nt,),
        in_specs=[pl.BlockSpec((t, D_MODEL), lambda i: (rev(i), 3)), pl.BlockSpec((t, D_MODEL), lambda i: (rev(i), 4)),
                  pl.BlockSpec((t, D_MODEL), lambda i: (rev(i), 1)),
                  pl.BlockSpec((1,) + zshape, lambda i: (rev(i), 0, 0, 0)),
                  const2(bbm.shape), const2(ccm.shape), const2((1, D_MODEL)), const2((D_MODEL, D_MODEL)),
                  const2((1, D_MODEL)), const2(zshape), hbm],
        out_specs=(hbm, row_out(D_MODEL), row_out(D_MODEL), const2(bbm.shape), const2(ccm.shape),
                   const2(zshape), const2((1, D_MODEL)), const2((1, D_MODEL))),
        out_shape=(jax.ShapeDtypeStruct(dproj.shape, BF16), jax.ShapeDtypeStruct((l, D_MODEL), BF16),
                   jax.ShapeDtypeStruct((l, D_MODEL), BF16), jax.ShapeDtypeStruct(bbm.shape, F32),
                   jax.ShapeDtypeStruct(ccm.shape, F32), jax.ShapeDtypeStruct(zshape, F32),
                   jax.ShapeDtypeStruct((1, D_MODEL), F32), jax.ShapeDtypeStruct((1, D_MODEL), F32)),
        scratch_shapes=[pltpu.VMEM((S5_ZQ, 8 * t + 8, 128), F32), pltpu.VMEM((S5_ZQ, 8 * t + 8, 128), F32),
                        pltpu.VMEM((S5_ZQ, 8 * t, 128), F32), pltpu.VMEM((S5_ZQ, 8 * t, 128), F32),
                        pltpu.VMEM(zshape, F32), pltpu.VMEM(zshape, F32),
                        pltpu.VMEM((2, t, 2 * D_MODEL), BF16), pltpu.SemaphoreType.DMA((2,))],
        input_output_aliases={10: 0},
        compiler_params=_cp("arbitrary"),
    )(proj, proj, dmix, xstart, bbm, ccm, d_row, glu_w, glu_b, tabs, dproj)


def _attn_probs(qh, kh):
    s = _dot_nt(qh, kh) * (XA_DH ** -0.5)
    e = jnp.exp(s - jnp.max(s, axis=-1, keepdims=True))
    return e / jnp.sum(e, axis=-1, keepdims=True)


def _attn_fwd(qa, ka, va):
    l = qa.shape[0]
    m = ka.shape[0]
    tl = _pick(l, (2048, 1024, 512, 256))

    def body(q_ref, k_ref, v_ref, o_ref):
        for h in range(XA_HEADS):
            hs = slice(h * XA_DH, (h + 1) * XA_DH)
            p = _attn_probs(q_ref[:, hs], k_ref[:, hs])
            o_ref[:, hs] = _dot(p.astype(BF16), v_ref[:, hs]).astype(o_ref.dtype)

    return pl.pallas_call(
        body, name="xattn_fwd", grid=(l // tl,),
        in_specs=[pl.BlockSpec((tl, D_MODEL), lambda i: (i, 0)), pl.BlockSpec((m, D_MODEL), lambda i: (0, 0)),
                  pl.BlockSpec((m, D_MODEL), lambda i: (0, 0))],
        out_specs=pl.BlockSpec((tl, D_MODEL), lambda i: (i, 0)),
        out_shape=jax.ShapeDtypeStruct((l, D_MODEL), BF16), compiler_params=_cp("parallel"),
    )(qa, ka, va)


def _attn_bwd(qa, ka, va, doa):
    l = qa.shape[0]
    m = ka.shape[0]
    tl = _pick(l, (2048, 1024, 512, 256))

    def body(q_ref, k_ref, v_ref, do_ref, dq_ref, dk_ref, dv_ref):
        @pl.when(pl.program_id(0) == 0)
        def _():
            dk_ref[...] = jnp.zeros_like(dk_ref)
            dv_ref[...] = jnp.zeros_like(dv_ref)

        for h in range(XA_HEADS):
            hs = slice(h * XA_DH, (h + 1) * XA_DH)
            qh, kh, vh, doh = q_ref[:, hs], k_ref[:, hs], v_ref[:, hs], do_ref[:, hs]
            p = _attn_probs(qh, kh)
            dv_ref[:, hs] += _dot_tn(p.astype(BF16), doh)
            dp = _dot_nt(doh, vh)
            ds = (p * (dp - jnp.sum(dp * p, axis=-1, keepdims=True)) * (XA_DH ** -0.5)).astype(BF16)
            dq_ref[:, hs] = _dot(ds, kh).astype(dq_ref.dtype)
            dk_ref[:, hs] += _dot_tn(ds, qh)

    row = pl.BlockSpec((tl, D_MODEL), lambda i: (i, 0))
    mem = pl.BlockSpec((m, D_MODEL), lambda i: (0, 0))
    return pl.pallas_call(
        body, name="xattn_bwd", grid=(l // tl,), in_specs=[row, mem, mem, row], out_specs=(row, mem, mem),
        out_shape=(jax.ShapeDtypeStruct((l, D_MODEL), BF16), jax.ShapeDtypeStruct((m, D_MODEL), F32),
                   jax.ShapeDtypeStruct((m, D_MODEL), F32)),
        compiler_params=_cp("arbitrary"),
    )(qa, ka, va, doa)


def _me_and_peers():
    x, y, c = lax.axis_index("x"), lax.axis_index("y"), lax.axis_index("c")
    flip = lambda v, bit: (1 - v) if bit else v
    peers = []
    for k in range(1, N_DEV):
        px, py, pc = flip(x, (k >> 2) & 1), flip(y, (k >> 1) & 1), flip(c, k & 1)
        peers.append(((px, py, pc), 4 * px + 2 * py + pc))
    return 4 * x + 2 * y + c, peers


class _SideJob:
    def __init__(self, srcs, landing, src_of, dst_of):
        self.srcs = list(srcs)
        self.landing = list(landing)
        self.n = len(self.landing)
        self.src_of, self.dst_of = src_of, dst_of
        hbm = pl.BlockSpec(memory_space=pl.ANY)
        self.in_specs = [hbm] * len(self.srcs)
        self.out_specs = [hbm] * self.n
        self.scratch = [pltpu.SemaphoreType.DMA((self.n * (N_DEV - 1),)), pltpu.SemaphoreType.DMA((self.n * (N_DEV - 1),)),
                        pltpu.SemaphoreType.DMA((self.n,))]

    def _copies(self, src_refs, out_refs, sems):
        send_sems, recv_sems, loc_sems = sems
        me, peers = _me_and_peers()
        local = [pltpu.make_async_copy(self.src_of(a, me, src_refs), self.dst_of(a, me, out_refs), loc_sems.at[a])
                 for a in range(self.n)]
        sends, recvs = [], []
        for k, (peer, peer_idx) in enumerate(peers):
            for a in range(self.n):
                s = self.n * k + a
                sends.append(pltpu.make_async_remote_copy(
                    src_ref=self.src_of(a, peer_idx, src_refs), dst_ref=self.dst_of(a, me, out_refs),
                    send_sem=send_sems.at[s], recv_sem=recv_sems.at[s], device_id=peer, device_id_type=MESH))
                recvs.append(pltpu.make_async_remote_copy(
                    src_ref=self.src_of(a, me, src_refs), dst_ref=self.dst_of(a, peer_idx, out_refs),
                    send_sem=send_sems.at[s], recv_sem=recv_sems.at[s], device_id=peer, device_id_type=MESH))
        return local, sends, recvs

    def start(self, src_refs, out_refs, sems):
        if not self.n:
            return
        local, sends, _ = self._copies(src_refs, out_refs, sems)
        for cp in local + sends:
            cp.start()

    def wait(self, src_refs, out_refs, sems):
        if not self.n:
            return
        local, sends, recvs = self._copies(src_refs, out_refs, sems)
        for cp in recvs:
            cp.wait_recv()
        for cp in sends:
            cp.wait_send()
        for cp in local:
            cp.wait()


def _gather_job(shards):
    return _SideJob(shards, [jax.ShapeDtypeStruct((N_DEV,) + s.shape, s.dtype) for s in shards],
                    src_of=lambda a, j, srcs: srcs[a], dst_of=lambda a, j, outs: outs[a].at[j])


def _scatter_job(grads):
    landing, parts = [], []
    for g in grads:
        if g.ndim == 3:
            landing.append(jax.ShapeDtypeStruct(g.shape, g.dtype))
            parts.append(None)
        else:
            r = g.shape[0] // N_DEV
            landing.append(jax.ShapeDtypeStruct((N_DEV, r, g.shape[1]), g.dtype))
            parts.append(r)

    def src_of(a, j, srcs):
        if parts[a] is None:
            return srcs[a].at[j]
        return srcs[a].at[pl.ds(pl.multiple_of(j * parts[a], 8), parts[a]), :]

    return _SideJob(grads, landing, src_of=src_of, dst_of=lambda a, j, outs: outs[a].at[j])


def _prologue(w_in_shard, row_shards, x, g, pos_col, inv_row):
    n_row = len(row_shards)
    l, d = x.shape
    tr = _pick(l, (1024, 512, 256))
    nt = l // tr
    mid = nt // 2

    def body(*refs):
        win_ref = refs[0]
        row_refs = refs[1:1 + n_row]
        x_ref, g_ref, p_ref, inv_ref = refs[1 + n_row:5 + n_row]
        out_win = refs[5 + n_row]
        row_outs = refs[6 + n_row:6 + 2 * n_row]
        h_ref, cos_ref, sin_ref = refs[6 + 2 * n_row:9 + 2 * n_row]
        win_b, send_sems, recv_sems, local_sem = refs[9 + 2 * n_row:]
        step = pl.program_id(0)
        cx, cy, cc = lax.axis_index("x"), lax.axis_index("y"), lax.axis_index("c")
        me, sibling = (cx, cy, cc), (cx, cy, 1 - cc)
        chips = [(1 - cx, cy), (cx, 1 - cy), (1 - cx, 1 - cy)]
        slot = lambda p: out_win.at[4 * p[0] + 2 * p[1] + p[2]]

        def copy(k, block, to, src=None):
            return pltpu.make_async_remote_copy(
                src_ref=slot(block) if src is None else src, dst_ref=slot(block), send_sem=send_sems.at[k],
                recv_sem=recv_sems.at[k], device_id=to, device_id_type=MESH)

        mine = pltpu.make_async_copy(win_b, slot(me), local_sem)
        first = [copy(0, me, sibling, src=win_b)]
        first += [copy(1 + j, me, (*chip, cc), src=win_b) for j, chip in enumerate(chips)]
        passed = [copy(4 + j, (*chip, cc), sibling) for j, chip in enumerate(chips)]

        @pl.when(step == 0)
        def _():
            win_b[...] = win_ref[...].astype(BF16)
            mine.start()
            for cp in first:
                cp.start()
            for r, o in zip(row_refs, row_outs):
                o[...] = r[...].astype(BF16)

        h_ref[...] = (_rms(x_ref[...])[1] * g_ref[...]).astype(h_ref.dtype)
        ang = p_ref[...].astype(F32) * inv_ref[...]
        lane = lax.broadcasted_iota(jnp.int32, ang.shape, 1)
        cos_ref[...] = jnp.tile(jnp.cos(ang), (1, RET_QK // 128))
        sin_ref[...] = jnp.tile(jnp.where((lane % RET_DK) < RET_DK // 2, -jnp.sin(ang), jnp.sin(ang)),
                                (1, RET_QK // 128))

        @pl.when(step == mid)
        def _():
            for j, chip in enumerate(chips):
                copy(1 + j, (*chip, cc), me).wait_recv()
                passed[j].start()

        @pl.when(step == nt - 1)
        def _():
            copy(0, sibling, me).wait_recv()
            for j, chip in enumerate(chips):
                copy(4 + j, (*chip, 1 - cc), me).wait_recv()
            for cp in first + passed:
                cp.wait_send()
            mine.wait()

    whole = lambda a: pl.BlockSpec(a.shape, lambda i: (0,) * a.ndim)
    rows = lambda w: pl.BlockSpec((tr, w), lambda i: (i, 0))
    return pl.pallas_call(
        body, name="prologue_allgather_w_in", grid=(nt,),
        in_specs=[whole(w_in_shard)] + [whole(r) for r in row_shards] + [rows(d), whole(g), rows(1), whole(inv_row)],
        out_specs=(pl.BlockSpec(memory_space=pl.ANY), *[whole(r) for r in row_shards], rows(d), rows(RET_QK),
                   rows(RET_QK)),
        out_shape=(jax.ShapeDtypeStruct((N_DEV,) + w_in_shard.shape, BF16),
                   *[jax.ShapeDtypeStruct(r.shape, BF16) for r in row_shards],
                   jax.ShapeDtypeStruct((l, d), BF16), jax.ShapeDtypeStruct((l, RET_QK), F32),
                   jax.ShapeDtypeStruct((l, RET_QK), F32)),
        scratch_shapes=[pltpu.VMEM(w_in_shard.shape, BF16), pltpu.SemaphoreType.DMA((N_DEV - 1,)),
                        pltpu.SemaphoreType.DMA((N_DEV - 1,)), pltpu.SemaphoreType.DMA],
        compiler_params=_cp("arbitrary"),
    )(w_in_shard, *row_shards, x, g, pos_col, inv_row)


def _allreduce_small(small):
    rows = SMALL_ROWS // N_DEV

    def body(x_ref, out_ref, land, send1, recv1, send2, recv2):
        me, peers = _me_and_peers()
        block = lambda j: pl.ds(pl.multiple_of(j * rows, 8), rows)

        def phase(src_of, dst_of, send_sems, recv_sems):
            sends = [pltpu.make_async_remote_copy(src_ref=src_of(pidx), dst_ref=dst_of(me), send_sem=send_sems.at[k],
                                                  recv_sem=recv_sems.at[k], device_id=peer, device_id_type=MESH)
                     for k, (peer, pidx) in enumerate(peers)]
            recvs = [pltpu.make_async_remote_copy(src_ref=src_of(me), dst_ref=dst_of(pidx), send_sem=send_sems.at[k],
                                                  recv_sem=recv_sems.at[k], device_id=peer, device_id_type=MESH)
                     for k, (peer, pidx) in enumerate(peers)]
            for cp in sends:
                cp.start()
            for cp in recvs:
                cp.wait_recv()
            for cp in sends:
                cp.wait_send()

        land[me] = x_ref[block(me), :]
        phase(lambda j: x_ref.at[block(j), :], lambda j: land.at[j], send1, recv1)
        total = land[0]
        for j in range(1, N_DEV):
            total = total + land[j]
        out_ref[block(me), :] = total
        phase(lambda j: out_ref.at[block(me), :], lambda j: out_ref.at[block(j), :], send2, recv2)

    vm = pl.BlockSpec(memory_space=pltpu.VMEM)
    return pl.pallas_call(
        body, name="allreduce_small", in_specs=[vm], out_specs=vm, out_shape=jax.ShapeDtypeStruct(small.shape, F32),
        scratch_shapes=[pltpu.VMEM((N_DEV, rows, D_MODEL), F32)] + [pltpu.SemaphoreType.DMA((N_DEV - 1,))] * 4,
    )(small)


def _adamw(name, got, w, m, v):
    r, c = w.shape
    n_slots = got.shape[0]
    tr = _pick(r, (256, 128, 64))

    def body(got_ref, w_ref, m_ref, v_ref, g_ref, d_ref, nm_ref, nv_ref):
        g = got_ref[0].astype(F32)
        for j in range(1, n_slots):
            g = g + got_ref[j].astype(F32)
        nm = ADAM_B1 * m_ref[...] + (1.0 - ADAM_B1) * g
        nv = ADAM_B2 * v_ref[...] + (1.0 - ADAM_B2) * jnp.square(g)
        m_hat = nm / (1.0 - ADAM_B1 ** ADAM_STEP)
        v_hat = nv / (1.0 - ADAM_B2 ** ADAM_STEP)
        g_ref[...] = g
        d_ref[...] = -ADAM_LR * (m_hat / (jnp.sqrt(v_hat) + ADAM_EPS) + ADAM_WD * w_ref[...])
        nm_ref[...] = nm
        nv_ref[...] = nv

    blk = pl.BlockSpec((tr, c), lambda i: (i, 0))
    out = jax.ShapeDtypeStruct((r, c), F32)
    return pl.pallas_call(
        body, name=name, grid=(r // tr,),
        in_specs=[pl.BlockSpec((n_slots, tr, c), lambda i: (0, i, 0)), blk, blk, blk],
        out_specs=(blk, blk, blk, blk), out_shape=(out, out, out, out), compiler_params=_cp("parallel"),
    )(got, w, m, v)


_SMALL_VECS = ("norm1_g", "ret_gn_g", "s5_d", "s5_glu_b", "norm2_g", "norm_mem_g", "norm_f_g")


def _small_layout():
    lay, row = {}, 0
    for n in _SMALL_VECS + ("loss",):
        lay[n] = (row, 1, D_MODEL)
        row += 1
    for n in ("s5_a_re", "s5_a_im"):
        lay[n] = (row, 4, D_MODEL)
        row += 4
    lay["s5_log_dt"] = (row, 1, S5_G)
    row += 8
    for n in ("s5_b_re", "s5_b_im", "s5_c_re", "s5_c_im"):
        lay[n] = (row, 64, D_MODEL)
        row += 64
    assert row <= SMALL_ROWS
    return lay


def _pack_small(t, loss_row=None):
    lay = _small_layout()
    pieces = [t[n].reshape(1, D_MODEL) for n in _SMALL_VECS]
    pieces.append(jnp.zeros((1, D_MODEL), F32) if loss_row is None else loss_row)
    pieces += [t["s5_a_re"].reshape(4, D_MODEL), t["s5_a_im"].reshape(4, D_MODEL)]
    pieces.append(jnp.pad(t["s5_log_dt"].reshape(1, S5_G), ((0, 7), (0, D_MODEL - S5_G))))
    pieces += [t[n].reshape(64, D_MODEL) for n in ("s5_b_re", "s5_b_im", "s5_c_re", "s5_c_im")]
    pieces.append(jnp.zeros((SMALL_ROWS - lay["s5_c_im"][0] - 64, D_MODEL), F32))
    return jnp.concatenate(pieces, axis=0)


def _adamw_small(g_sum, w, m, v):
    lay = _small_layout()
    names = [n for n in lay if n != "loss"]

    def body(g_ref, w_ref, m_ref, v_ref, *outs):
        g = g_ref[...]
        nm = ADAM_B1 * m_ref[...] + (1.0 - ADAM_B1) * g
        nv = ADAM_B2 * v_ref[...] + (1.0 - ADAM_B2) * jnp.square(g)
        m_hat = nm / (1.0 - ADAM_B1 ** ADAM_STEP)
        v_hat = nv / (1.0 - ADAM_B2 ** ADAM_STEP)
        delta = -ADAM_LR * (m_hat / (jnp.sqrt(v_hat) + ADAM_EPS) + ADAM_WD * w_ref[...])
        for i, n in enumerate(names):
            r0, rows, lanes = lay[n]
            for part, val in enumerate((g, delta, nm, nv)):
                outs[4 * i + part][...] = val[r0:r0 + rows, 0:lanes]
        r0 = lay["loss"][0]
        outs[-1][...] = g[r0:r0 + 1, :]

    shapes = []
    for n in names:
        shapes += [jax.ShapeDtypeStruct(lay[n][1:], F32)] * 4
    shapes.append(jax.ShapeDtypeStruct((1, D_MODEL), F32))
    outs = pl.pallas_call(body, name="adamw_small", out_shape=tuple(shapes),
                          compiler_params=pltpu.CompilerParams(vmem_limit_bytes=VMEM_LIMIT))(g_sum, w, m, v)
    return {n: tuple(outs[4 * i:4 * i + 4]) for i, n in enumerate(names)}, outs[-1]


_W_NAMES = ("norm1_g", "w_in", "ret_gn_g", "s5_a_re", "s5_a_im", "s5_log_dt", "s5_b_re", "s5_b_im", "s5_c_re", "s5_c_im",
            "s5_d", "s5_glu_w", "s5_glu_b", "w_out", "norm2_g", "norm_mem_g", "xa_wq", "xa_wk", "xa_wv", "xa_wo",
            "norm_f_g")
_ROW_NAMES = ("s5_glu_w", "w_out", "xa_wq", "xa_wk", "xa_wv", "xa_wo")


def kernel(x, mem, positions, norm1_g, w_in, ret_gn_g, s5_a_re, s5_a_im, s5_log_dt, s5_b_re, s5_b_im, s5_c_re, s5_c_im, s5_d, s5_glu_w, s5_glu_b, w_out, norm2_g, norm_mem_g, xa_wq, xa_wk, xa_wv, xa_wo, norm_f_g, loss_target, m_norm1_g, m_w_in, m_ret_gn_g, m_s5_a_re, m_s5_a_im, m_s5_log_dt, m_s5_b_re, m_s5_b_im, m_s5_c_re, m_s5_c_im, m_s5_d, m_s5_glu_w, m_s5_glu_b, m_w_out, m_norm2_g, m_norm_mem_g, m_xa_wq, m_xa_wk, m_xa_wv, m_xa_wo, m_norm_f_g, v_norm1_g, v_w_in, v_ret_gn_g, v_s5_a_re, v_s5_a_im, v_s5_log_dt, v_s5_b_re, v_s5_b_im, v_s5_c_re, v_s5_c_im, v_s5_d, v_s5_glu_w, v_s5_glu_b, v_w_out, v_norm2_g, v_norm_mem_g, v_xa_wq, v_xa_wk, v_xa_wv, v_xa_wo, v_norm_f_g):
    w = dict(norm1_g=norm1_g, w_in=w_in, ret_gn_g=ret_gn_g, s5_a_re=s5_a_re, s5_a_im=s5_a_im, s5_log_dt=s5_log_dt,
             s5_b_re=s5_b_re, s5_b_im=s5_b_im, s5_c_re=s5_c_re, s5_c_im=s5_c_im, s5_d=s5_d, s5_glu_w=s5_glu_w,
             s5_glu_b=s5_glu_b, w_out=w_out, norm2_g=norm2_g, norm_mem_g=norm_mem_g, xa_wq=xa_wq, xa_wk=xa_wk,
             xa_wv=xa_wv, xa_wo=xa_wo, norm_f_g=norm_f_g)
    mom = dict(norm1_g=m_norm1_g, w_in=m_w_in, ret_gn_g=m_ret_gn_g, s5_a_re=m_s5_a_re, s5_a_im=m_s5_a_im,
               s5_log_dt=m_s5_log_dt, s5_b_re=m_s5_b_re, s5_b_im=m_s5_b_im, s5_c_re=m_s5_c_re, s5_c_im=m_s5_c_im,
               s5_d=m_s5_d, s5_glu_w=m_s5_glu_w, s5_glu_b=m_s5_glu_b, w_out=m_w_out, norm2_g=m_norm2_g,
               norm_mem_g=m_norm_mem_g, xa_wq=m_xa_wq, xa_wk=m_xa_wk, xa_wv=m_xa_wv, xa_wo=m_xa_wo,
               norm_f_g=m_norm_f_g)
    var = dict(norm1_g=v_norm1_g, w_in=v_w_in, ret_gn_g=v_ret_gn_g, s5_a_re=v_s5_a_re, s5_a_im=v_s5_a_im,
               s5_log_dt=v_s5_log_dt, s5_b_re=v_s5_b_re, s5_b_im=v_s5_b_im, s5_c_re=v_s5_c_re, s5_c_im=v_s5_c_im,
               s5_d=v_s5_d, s5_glu_w=v_s5_glu_w, s5_glu_b=v_s5_glu_b, w_out=v_w_out, norm2_g=v_norm2_g,
               norm_mem_g=v_norm_mem_g, xa_wq=v_xa_wq, xa_wk=v_xa_wk, xa_wv=v_xa_wv, xa_wo=v_xa_wo,
               norm_f_g=v_norm_f_g)
    shapes = {n: w[n].shape for n in _W_NAMES}

    x2d, mem2d, tgt = x[0], mem[0], loss_target[0]
    l = x2d.shape[0]
    ret_c = _pick(l, (256, 128))
    s5_t = _pick(l, (256, 128))
    g1, g2, gm, gf = norm1_g, norm2_g, norm_mem_g, norm_f_g.reshape(1, D_MODEL)

    half = RET_DK // 2
    inv = ROPE_BASE ** (-jnp.arange(half, dtype=F32) / half)
    win_s, *rest = _prologue(w_in[0], [w[n][0] for n in _ROW_NAMES], x2d, g1, positions[0].reshape(l, 1),
                             jnp.tile(inv, 128 // half)[None, :])
    row_shards_b, (h1, cos_t, sin_t) = rest[:len(_ROW_NAMES)], rest[len(_ROW_NAMES):]

    to_gpn = lambda b: jnp.transpose(b, (0, 2, 1)).reshape(S5_G, S5_P * S5_N)
    from_gpn = lambda b: jnp.transpose(b.reshape(S5_G, S5_P, S5_N), (0, 2, 1))
    disc_args = (s5_a_re[0], s5_a_im[0], s5_log_dt[0].reshape(S5_G, 1), to_gpn(s5_b_re[0]), to_gpn(s5_b_im[0]))
    abar_re, abar_im, bb_re_t, bb_im_t = _s5_discretize(*disc_args)
    bbm, ccm = _s5_block_mats(from_gpn(bb_re_t), from_gpn(bb_im_t), s5_c_re[0], s5_c_im[0])
    a_z = _s5_z(abar_re, abar_im)

    proj, *rows_01 = _mm_nn_slots("in_proj", h1, win_s, BF16, side=_gather_job(row_shards_b[:2]))
    full = {n: g.reshape(N_DEV * r, D_MODEL) for n, g, r in zip(_ROW_NAMES[:2], rows_01, ROW_SHARDS[:2])}
    rconsts = _ret_constants(ret_c)
    ret, o_saved, r_prev, q_rot, k_rot = _ret_fwd(proj, cos_t, sin_t, rconsts, ret_gn_g, ret_c)
    mix, xstart, *rows_xa = _s5_fwd(proj, ret, bbm, ccm, s5_d, full["s5_glu_w"], s5_glu_b, a_z, s5_t,
                                    side=_gather_job(row_shards_b[2:]))
    full.update({n: g.reshape(N_DEV * r, D_MODEL) for n, g, r in zip(_ROW_NAMES[2:], rows_xa, ROW_SHARDS[2:])})
    x1, h2 = _mm_nn("out_proj", mix, full["w_out"], F32, residual=x2d, epi=_epi_norm_fwd(g2))
    mn = _rms_fwd("norm_mem_fwd", mem2d, gm)
    qa = _mm_nn("xa_q", h2, full["xa_wq"], BF16)
    ka = _mm_nn("xa_k", mn, full["xa_wk"], BF16)
    va = _mm_nn("xa_v", mn, full["xa_wv"], BF16)
    oa = _attn_fwd(qa, ka, va)
    dx2, dgf, loss_lanes = _mm_nn("xa_o", oa, full["xa_wo"], F32, residual=x1, epi=_epi_loss(gf, tgt))

    doa = _mm_nt("xa_o_dx", dx2, full["xa_wo"], BF16)
    dwo = _mm_tn("xa_o_dw", oa, dx2, BF16)
    dqa, dka, dva = _attn_bwd(qa, ka, va, doa)
    dx1, dg2 = _mm_nt("xa_q_dx", dqa, full["xa_wq"], F32, epi=_epi_norm_bwd(x1, g2, dx2))
    dwq = _mm_tn("xa_q_dw", h2, dqa, BF16)
    dwk = _mm_tn("xa_k_dw", mn, dka, BF16)
    dwv = _mm_tn("xa_v_dw", mn, dva, BF16)
    dmn = _mm_nt("xa_v_dx", dva, full["xa_wv"], F32, residual=_mm_nt("xa_k_dx", dka, full["xa_wk"], F32))
    _, dgm = _rms_bwd("norm_mem_bwd", mem2d, gm, dmn, None)
    dmix = _mm_nt("out_proj_dx", dx1, full["w_out"], BF16)
    dwout = _mm_tn("out_proj_dw", mix, dx1, BF16)
    dret, dgn, *got_a = _ret_bwd(proj, q_rot, k_rot, cos_t, sin_t, rconsts, ret_gn_g, o_saved, r_prev, dmix, ret_c,
                                 side=_scatter_job([dwout, dwq, dwk, dwv, dwo]))
    dproj, y2, dz, dbbm, dccm, dabar, dd, dgb = _s5_bwd(proj, dmix, dret, xstart, bbm, ccm, s5_d, full["s5_glu_w"],
                                                        s5_glu_b, a_z, s5_t)
    dglu = _mm_tn("s5_glu_dw", y2, dz, BF16)
    dwin_s, got_glu = _mm_tn_slots("in_proj_dw", h1, dproj, N_DEV, BF16, side=_scatter_job([dglu]))
    grad_x, dg1, got_win = _mm_nt_slots("in_proj_dx", dproj, win_s, F32, side=_scatter_job([dwin_s]),
                                        epi=_epi_norm_bwd(x2d, g1, dx1))

    dab_re, dab_im = _s5_unz(dabar)
    dbb_re, dbb_im = _s5_block_diag_bb(dbbm)
    dc_re, dc_im = _s5_block_diag_cc(dccm)
    da_re, da_im, dlog_dt, db_re_t, db_im_t = _s5_discretize_bwd(*disc_args, dab_re, dab_im, to_gpn(dbb_re),
                                                                 to_gpn(dbb_im))
    db_re, db_im = from_gpn(db_re_t), from_gpn(db_im_t)
    small_g = dict(norm1_g=dg1, ret_gn_g=dgn, s5_d=dd, s5_glu_b=dgb, norm2_g=dg2, norm_mem_g=dgm, norm_f_g=dgf,
                   s5_a_re=da_re, s5_a_im=da_im, s5_log_dt=dlog_dt, s5_b_re=db_re, s5_b_im=db_im, s5_c_re=dc_re,
                   s5_c_im=dc_im)
    small_pack = _pack_small(small_g, loss_row=loss_lanes)

    res = {}
    got = dict(zip(("w_out", "xa_wq", "xa_wk", "xa_wv", "xa_wo"), got_a), w_in=got_win, s5_glu_w=got_glu)
    for n in ("w_in",) + _ROW_NAMES:
        res[n] = _adamw("adamw_" + n, got[n], w[n][0], mom[n][0], var[n][0])
    small_sum = _allreduce_small(small_pack)
    small_res, loss_sum = _adamw_small(small_sum, _pack_small(w), _pack_small(mom), _pack_small(var))
    loss = (0.5 / D_MODEL) * jnp.sum(loss_sum)
    res.update(small_res)

    outs = [loss, grad_x[None]]
    for part in range(4):
        for n in _W_NAMES:
            outs.append(res[n][part].reshape(shapes[n]))
    return tuple(outs)
```

```python
import jax
import jax.numpy as jnp
from jax import lax
from jax.experimental import pallas as pl
from jax.experimental.pallas import tpu as pltpu

F32 = jnp.float32
BF16 = jnp.bfloat16
MESH = pl.DeviceIdType.MESH

D_MODEL = 1024
RET_HEADS, RET_DK, RET_DV = 8, 64, 128
RET_QK = RET_HEADS * RET_DK
S5_G, S5_N, S5_P = 64, 64, 16
S5_NB = 8
S5_GB = S5_G // S5_NB
S5_BS = S5_GB * S5_N
S5_COLS = 2 * S5_G * S5_N
XA_HEADS, XA_DH = 4, 256
EPS = 1e-6
ROPE_BASE = 10000.0
N_DEV = 8
W_IN_SHARD = 640
ROW_SHARDS = (128, 256, 128, 128, 128, 128)
ROWPACK = sum(ROW_SHARDS)
SMALL_ROWS = 320
ADAM_LR, ADAM_B1, ADAM_B2, ADAM_EPS, ADAM_WD, ADAM_STEP = 0.001, 0.9, 0.999, 1e-08, 0.01, 10

VMEM_LIMIT = 56 * 1024 * 1024


def _cp(*sem):
    return pltpu.CompilerParams(dimension_semantics=tuple(sem), vmem_limit_bytes=VMEM_LIMIT)


def _dot(a, b):
    return jnp.dot(a, b, preferred_element_type=F32)


def _dot_nt(a, b):
    return lax.dot_general(a, b, (((1,), (1,)), ((), ())), preferred_element_type=F32)


def _dot_tn(a, b):
    return lax.dot_general(a, b, (((0,), (0,)), ((), ())), preferred_element_type=F32)


def _sigmoid(x):
    return 1.0 / (1.0 + jnp.exp(-x))


def _silu(x):
    return x * _sigmoid(x)


def _dsilu(x):
    s = _sigmoid(x)
    return s * (1.0 + x * (1.0 - s))


_GELU_C = 0.7978845608028654


def _gelu(x):
    return 0.5 * x * (1.0 + jnp.tanh(_GELU_C * (x + 0.044715 * (x * x * x))))


def _gelu_and_grad(x):
    t = jnp.tanh(_GELU_C * (x + 0.044715 * (x * x * x)))
    half = 0.5 * (1.0 + t)
    return x * half, half + 0.5 * x * (1.0 - t * t) * (_GELU_C * (1.0 + 3.0 * 0.044715 * (x * x)))


def _pick(n, cands):
    for c in cands:
        if n % c == 0:
            return c
    return n


class _Epilogue:
    def __init__(self, rows, vecs, row_out_dtypes, n_sums, fn):
        self.rows, self.vecs, self.row_out_dtypes, self.n_sums, self.fn = list(rows), list(vecs), list(row_out_dtypes), n_sums, fn


def _rms(x):
    rs = lax.rsqrt(jnp.mean(x * x, axis=-1, keepdims=True) + EPS)
    return rs, x * rs


def _rms_dx(dn, xn, rs):
    return rs * (dn - xn * jnp.mean(dn * xn, axis=-1, keepdims=True))


def _epi_norm_fwd(g):
    def fn(r, rows, vecs):
        return r, [_rms(r)[1] * vecs[0]], []

    return _Epilogue([], [g], [BF16], 0, fn)


def _epi_loss(gf, target):
    def fn(r, rows, vecs):
        rs, xn = _rms(r)
        e = xn * vecs[0] - rows[0]
        dy = e * (1.0 / r.shape[-1])
        return (_rms_dx(dy * vecs[0], xn, rs), [],
                [jnp.sum(dy * xn, axis=0, keepdims=True), jnp.sum(e * e, axis=0, keepdims=True)])

    return _Epilogue([target], [gf], [], 2, fn)


def _epi_norm_bwd(x, g, dres):
    def fn(r, rows, vecs):
        rs, xn = _rms(rows[0])
        return _rms_dx(r * vecs[0], xn, rs) + rows[1], [], [jnp.sum(r * xn, axis=0, keepdims=True)]

    return _Epilogue([x, dres], [g], [], 1, fn)


def _mm_core(name, operands, in_specs, out_spec, out_shape, grid, nk, dims, acc_shape, has_res, side=None, epi=None):
    n_in = 3 if has_res else 2
    n_epi_in = len(epi.rows) + len(epi.vecs) if epi else 0
    n_epi_out = len(epi.row_out_dtypes) + epi.n_sums if epi else 0
    n_side_in = len(side.srcs) if side else 0
    n_side_out = side.n if side else 0

    def body(*refs):
        a_ref, b_ref = refs[0], refs[1]
        r_ref = refs[2] if has_res else None
        epi_in = refs[n_in:n_in + n_epi_in]
        side_in = refs[n_in + n_epi_in:n_in + n_epi_in + n_side_in]
        n0 = n_in + n_epi_in + n_side_in
        o_ref = refs[n0]
        epi_out = refs[n0 + 1:n0 + 1 + n_epi_out]
        side_out = refs[n0 + 1 + n_epi_out:n0 + 1 + n_epi_out + n_side_out]
        rest = refs[n0 + 1 + n_epi_out + n_side_out:]
        acc, sems = (rest[0], rest[1:]) if nk > 1 else (None, rest)
        i, j, k = pl.program_id(0), pl.program_id(1), pl.program_id(2)
        if side:
            @pl.when((i == 0) & (j == 0) & (k == 0))
            def _():
                side.start(side_in, side_out, sems)

        def product():
            if len(b_ref.shape) == 3:
                ns = b_ref.shape[2]
                return sum(lax.dot_general(a_ref[:, p * ns:(p + 1) * ns].astype(BF16), b_ref[p].astype(BF16),
                                           (dims, ((), ())), preferred_element_type=F32)
                           for p in range(b_ref.shape[0]))
            return lax.dot_general(a_ref[...].astype(BF16), b_ref[...].astype(BF16), (dims, ((), ())),
                                   preferred_element_type=F32)

        def finish(r):
            if has_res:
                r = r + r_ref[...]
            if epi is None:
                o_ref[...] = r.astype(o_ref.dtype)
                return
            n_rows = len(epi.rows)
            main, row_vals, sums = epi.fn(r, [t[...] for t in epi_in[:n_rows]], [t[...] for t in epi_in[n_rows:]])
            o_ref[...] = main.astype(o_ref.dtype)
            for ref, val in zip(epi_out, row_vals):
                ref[...] = val.astype(ref.dtype)
            for ref, val in zip(epi_out[len(row_vals):], sums):
                @pl.when(i == 0)
                def _(ref=ref):
                    ref[...] = jnp.zeros_like(ref)

                ref[...] += val

        if nk == 1:
            finish(product())
        else:
            @pl.when(k == 0)
            def _():
                acc[...] = jnp.zeros_like(acc)

            acc[...] += product()

            @pl.when(k == nk - 1)
            def _():
                finish(acc[...])

        if side:
            @pl.when((i == grid[0] - 1) & (j == grid[1] - 1) & (k == grid[2] - 1))
            def _():
                side.wait(side_in, side_out, sems)

    acc_scratch = [pltpu.VMEM(acc_shape, F32)] if nk > 1 else []
    in_specs, out_specs, out_shapes, operands = list(in_specs), [out_spec], [out_shape], list(operands)
    if epi:
        assert grid[1] == 1, "an epilogue needs tiles that span whole rows"
        tm, n = out_spec.block_shape
        row_spec = pl.BlockSpec((tm, n), lambda i, j, k: (i, 0))
        vec_spec = pl.BlockSpec((1, n), lambda i, j, k: (0, 0))
        in_specs += [row_spec] * len(epi.rows) + [vec_spec] * len(epi.vecs)
        operands += epi.rows + epi.vecs
        out_specs += [row_spec] * len(epi.row_out_dtypes) + [vec_spec] * epi.n_sums
        out_shapes += [jax.ShapeDtypeStruct(out_shape.shape, d) for d in epi.row_out_dtypes]
        out_shapes += [jax.ShapeDtypeStruct((1, n), F32)] * epi.n_sums
    scratch = acc_scratch
    if side:
        in_specs += side.in_specs
        operands += side.srcs
        out_specs += side.out_specs
        out_shapes += side.landing
        scratch = acc_scratch + side.scratch
    plain = side is None and epi is None
    res = pl.pallas_call(
        body, name=name, grid=grid, in_specs=in_specs, out_specs=out_specs[0] if plain else tuple(out_specs),
        out_shape=out_shapes[0] if plain else tuple(out_shapes), scratch_shapes=scratch,
        compiler_params=_cp("parallel", "parallel", "arbitrary") if plain else _cp("arbitrary", "arbitrary", "arbitrary"),
    )(*operands)
    return res


def _mm_nn(name, a, b, out_dtype, residual=None, epi=None):
    m, kk = a.shape
    n = b.shape[1]
    tm, tn, tk = _pick(m, (1024, 512, 256)), _pick(n, (1024, 512)), _pick(kk, (2048, 1024, 512))
    ops = [a, b]
    specs = [pl.BlockSpec((tm, tk), lambda i, j, k: (i, k)), pl.BlockSpec((tk, tn), lambda i, j, k: (k, j))]
    if residual is not None:
        ops.append(residual)
        specs.append(pl.BlockSpec((tm, tn), lambda i, j, k: (i, j)))
    return _mm_core(name, ops, specs, pl.BlockSpec((tm, tn), lambda i, j, k: (i, j)),
                    jax.ShapeDtypeStruct((m, n), out_dtype), (m // tm, n // tn, kk // tk), kk // tk,
                    ((1,), (0,)), (tm, tn), residual is not None, epi=epi)


def _mm_nt(name, a, b, out_dtype, residual=None, epi=None):
    m, kk = a.shape
    n = b.shape[0]
    tm, tn, tk = _pick(m, (1024, 512, 256)), _pick(n, (2048, 1024, 512)), _pick(kk, (2048, 1024, 512))
    ops = [a, b]
    specs = [pl.BlockSpec((tm, tk), lambda i, j, k: (i, k)), pl.BlockSpec((tn, tk), lambda i, j, k: (j, k))]
    if residual is not None:
        ops.append(residual)
        specs.append(pl.BlockSpec((tm, tn), lambda i, j, k: (i, j)))
    return _mm_core(name, ops, specs, pl.BlockSpec((tm, tn), lambda i, j, k: (i, j)),
                    jax.ShapeDtypeStruct((m, n), out_dtype), (m // tm, n // tn, kk // tk), kk // tk,
                    ((1,), (1,)), (tm, tn), residual is not None, epi=epi)


def _mm_tn(name, a, b, out_dtype):
    kk, m = a.shape
    n = b.shape[1]
    tm, tn, tk = _pick(m, (1024, 512)), _pick(n, (1024, 512)), _pick(kk, (2048, 1024, 512, 256))
    specs = [pl.BlockSpec((tk, tm), lambda i, j, k: (k, i)), pl.BlockSpec((tk, tn), lambda i, j, k: (k, j))]
    return _mm_core(name, [a, b], specs, pl.BlockSpec((tm, tn), lambda i, j, k: (i, j)),
                    jax.ShapeDtypeStruct((m, n), out_dtype), (m // tm, n // tn, kk // tk), kk // tk,
                    ((0,), (0,)), (tm, tn), False)


def _mm_nn_slots(name, a, b_slots, out_dtype, side=None):
    m, kk = a.shape
    s, _, ns = b_slots.shape
    tm, tk = _pick(m, (4096, 2048, 1024, 512, 256)), _pick(kk, (1024, 512))
    specs = [pl.BlockSpec((tm, tk), lambda i, j, k: (i, k)), pl.BlockSpec((None, tk, ns), lambda i, j, k: (j, k, 0))]
    return _mm_core(name, [a, b_slots], specs, pl.BlockSpec((tm, ns), lambda i, j, k: (i, j)),
                    jax.ShapeDtypeStruct((m, s * ns), out_dtype), (m // tm, s, kk // tk), kk // tk,
                    ((1,), (0,)), (tm, ns), False, side)


def _mm_nt_slots(name, a, b_slots, out_dtype, side=None, epi=None):
    m = a.shape[0]
    s, n, ns = b_slots.shape
    tm, tn = _pick(m, (1024, 512, 256)), _pick(n, (1024, 512))
    per = _pick(s, (2, 1))
    specs = [pl.BlockSpec((tm, per * ns), lambda i, j, k: (i, k)),
             pl.BlockSpec((per, tn, ns), lambda i, j, k: (k, j, 0))]
    return _mm_core(name, [a, b_slots], specs, pl.BlockSpec((tm, tn), lambda i, j, k: (i, j)),
                    jax.ShapeDtypeStruct((m, n), out_dtype), (m // tm, n // tn, s // per), s // per,
                    ((1,), (1,)), (tm, tn), False, side, epi)


def _mm_tn_slots(name, a, b, s, out_dtype, side=None):
    kk, m = a.shape
    ns = b.shape[1] // s
    tm, tk = _pick(m, (1024, 512)), _pick(kk, (4096, 2048, 1024, 512, 256))
    specs = [pl.BlockSpec((tk, tm), lambda i, j, k: (k, i)), pl.BlockSpec((tk, ns), lambda i, j, k: (k, j))]
    return _mm_core(name, [a, b], specs, pl.BlockSpec((None, tm, ns), lambda i, j, k: (j, i, 0)),
                    jax.ShapeDtypeStruct((s, m, ns), out_dtype), (m // tm, s, kk // tk), kk // tk,
                    ((0,), (0,)), (tm, ns), False, side)


def _rms_fwd(name, x, g):
    r, d = x.shape
    tr = _pick(r, (1024, 512, 256))

    def body(x_ref, g_ref, o_ref):
        xv = x_ref[...]
        rs = lax.rsqrt(jnp.mean(xv * xv, axis=-1, keepdims=True) + EPS)
        o_ref[...] = (xv * rs * g_ref[...]).astype(o_ref.dtype)

    return pl.pallas_call(
        body, name=name, grid=(r // tr,),
        in_specs=[pl.BlockSpec((tr, d), lambda i: (i, 0)), pl.BlockSpec((1, d), lambda i: (0, 0))],
        out_specs=pl.BlockSpec((tr, d), lambda i: (i, 0)),
        out_shape=jax.ShapeDtypeStruct((r, d), BF16), compiler_params=_cp("parallel"),
    )(x, g)


def _rms_bwd(name, x, g, dh, dres):
    r, d = x.shape
    tr = _pick(r, (512, 256))
    has_res = dres is not None

    def body(*refs):
        if has_res:
            x_ref, g_ref, dh_ref, dr_ref, dx_ref, dg_ref = refs
        else:
            x_ref, g_ref, dh_ref, dx_ref, dg_ref = refs
        i = pl.program_id(0)

        @pl.when(i == 0)
        def _():
            dg_ref[...] = jnp.zeros_like(dg_ref)

        xv = x_ref[...]
        dhv = dh_ref[...].astype(F32)
        rs = lax.rsqrt(jnp.mean(xv * xv, axis=-1, keepdims=True) + EPS)
        xn = xv * rs
        dg_ref[...] += jnp.sum(dhv * xn, axis=0, keepdims=True)
        dn = dhv * g_ref[...]
        dx = rs * (dn - xn * jnp.mean(dn * xn, axis=-1, keepdims=True))
        if has_res:
            dx = dx + dr_ref[...]
        dx_ref[...] = dx

    row = pl.BlockSpec((tr, d), lambda i: (i, 0))
    vec = pl.BlockSpec((1, d), lambda i: (0, 0))
    ops = [x, g, dh] + ([dres] if has_res else [])
    return pl.pallas_call(
        body, name=name, grid=(r // tr,),
        in_specs=[row, vec, row] + ([row] if has_res else []),
        out_specs=(row, vec),
        out_shape=(jax.ShapeDtypeStruct((r, d), F32), jax.ShapeDtypeStruct((1, d), F32)),
        compiler_params=_cp("arbitrary"),
    )(*ops)


def _rot(x, cos_t, sin_t):
    n = x.shape[-1]
    lane = lax.broadcasted_iota(jnp.int32, x.shape, 1)
    partner = jnp.where((lane % RET_DK) < RET_DK // 2, pltpu.roll(x, n - RET_DK // 2, 1), pltpu.roll(x, RET_DK // 2, 1))
    return x * cos_t + partner * sin_t


def _ret_constants(c):
    log_g = jnp.log1p(-jnp.exp2(-5.0 - jnp.arange(RET_HEADS, dtype=F32)))
    j = jnp.arange(c, dtype=F32)
    diff = j[:, None] - j[None, :]
    decay = jnp.where(diff[None] >= 0.0, jnp.exp(log_g[:, None, None] * jnp.maximum(diff, 0.0)[None]), 0.0)
    q_w = jnp.exp(log_g[None, :] * (j + 1.0)[:, None])
    k_w = jnp.exp(log_g[None, :] * (c - 1.0 - j)[:, None])
    cd = jnp.exp(log_g * c)
    rep = lambda t: jnp.repeat(t, RET_DK, axis=1)
    cd_row = jnp.repeat(cd, RET_DV)[None, :]
    return decay, rep(q_w), rep(k_w), cd_row


def _pair_of(h, c):
    lane = lax.broadcasted_iota(jnp.int32, (c, 2 * RET_DK), 1)
    mine = (lane < RET_DK) if h % 2 == 0 else (lane >= RET_DK)
    return slice((h // 2) * 2 * RET_DK, (h // 2 + 1) * 2 * RET_DK), mine


def _keep(x, mine):
    return jnp.where(mine, x, jnp.zeros_like(x))


def _ret_fwd(proj, cos_t, sin_t, consts, gn_g, c):
    l = proj.shape[0]
    nc = l // c
    decay, qw, kw, cd_row = consts

    def body(q_ref, k_ref, v_ref, g_ref, cos_ref, sin_ref, dec_ref, qw_ref, kw_ref, cd_ref, gn_ref,
             ret_ref, o_ref, rp_ref, qb_ref, kb_ref, state):
        @pl.when(pl.program_id(0) == 0)
        def _():
            state[...] = jnp.zeros_like(state)

        cs, sn = cos_ref[...], sin_ref[...]
        qr = _rot(q_ref[...].astype(F32), cs, sn)
        kr = _rot(k_ref[...].astype(F32), cs, sn) * (RET_DK ** -0.5)
        qb, kb = qr.astype(BF16), kr.astype(BF16)
        qb_ref[...] = qb
        kb_ref[...] = kb
        qwb = (qr * qw_ref[...]).astype(BF16)
        kwb = (kr * kw_ref[...]).astype(BF16)
        vb = v_ref[...].astype(BF16)
        for h in range(RET_HEADS):
            ps, mine = _pair_of(h, c)
            vs = slice(h * RET_DV, (h + 1) * RET_DV)
            s = _dot_nt(_keep(qb[:, ps], mine), kb[:, ps]) * dec_ref[h]
            r_prev = state[h]
            rp_ref[0, h] = r_prev
            o = _dot(s.astype(BF16), vb[:, vs]) + _dot(_keep(qwb[:, ps], mine), r_prev.astype(BF16))
            state[h] = cd_ref[:, vs] * r_prev + _dot_tn(_keep(kwb[:, ps], mine), vb[:, vs])
            o_ref[:, vs] = o
            mu = jnp.mean(o, axis=-1, keepdims=True)
            var = jnp.mean(jnp.square(o - mu), axis=-1, keepdims=True)
            on = (o - mu) * lax.rsqrt(var + EPS)
            ret_ref[:, vs] = (on * gn_ref[:, vs] * _silu(g_ref[:, vs].astype(F32))).astype(ret_ref.dtype)

    const2 = lambda shape: pl.BlockSpec(shape, lambda i: (0,) * len(shape))
    return pl.pallas_call(
        body, name="retention_fwd", grid=(nc,),
        in_specs=[pl.BlockSpec((c, RET_QK), lambda i: (i, 0)), pl.BlockSpec((c, RET_QK), lambda i: (i, 1)),
                  pl.BlockSpec((c, D_MODEL), lambda i: (i, 1)), pl.BlockSpec((c, D_MODEL), lambda i: (i, 2)),
                  pl.BlockSpec((c, RET_QK), lambda i: (i, 0)), pl.BlockSpec((c, RET_QK), lambda i: (i, 0)),
                  const2((RET_HEADS, c, c)), const2((c, RET_QK)), const2((c, RET_QK)), const2((1, D_MODEL)),
                  const2((1, D_MODEL))],
        out_specs=(pl.BlockSpec((c, D_MODEL), lambda i: (i, 0)), pl.BlockSpec((c, D_MODEL), lambda i: (i, 0)),
                   pl.BlockSpec((1, RET_HEADS, 2 * RET_DK, RET_DV), lambda i: (i, 0, 0, 0)),
                   pl.BlockSpec((c, RET_QK), lambda i: (i, 0)), pl.BlockSpec((c, RET_QK), lambda i: (i, 0))),
        out_shape=(jax.ShapeDtypeStruct((l, 2 * D_MODEL), BF16), jax.ShapeDtypeStruct((l, D_MODEL), F32),
                   jax.ShapeDtypeStruct((nc, RET_HEADS, 2 * RET_DK, RET_DV), F32),
                   jax.ShapeDtypeStruct((l, RET_QK), BF16), jax.ShapeDtypeStruct((l, RET_QK), BF16)),
        scratch_shapes=[pltpu.VMEM((RET_HEADS, 2 * RET_DK, RET_DV), F32)],
        compiler_params=_cp("arbitrary"),
    )(proj, proj, proj, proj, cos_t, sin_t, decay, qw, kw, cd_row, gn_g)


def _ret_bwd(proj, qb_saved, kb_saved, cos_t, sin_t, consts, gn_g, o_saved, r_prev_saved, dmix, c, side):
    l = proj.shape[0]
    nc = l // c
    decay, qw, kw, cd_row = consts
    n_in = 14

    def body(*refs):
        (q_ref, k_ref, v_ref, g_ref, cos_ref, sin_ref, dec_ref, qw_ref, kw_ref, cd_ref, gn_ref, o_ref, rp_ref,
         dr_ref) = refs[:n_in]
        side_in = refs[n_in:n_in + len(side.srcs)]
        out_ref, dgn_ref = refs[n_in + len(side.srcs):n_in + len(side.srcs) + 2]
        side_out = refs[n_in + len(side.srcs) + 2:n_in + len(side.srcs) + 2 + side.n]
        state, dq_s, dk_s = refs[n_in + len(side.srcs) + 2 + side.n:n_in + len(side.srcs) + 5 + side.n]
        sems = refs[n_in + len(side.srcs) + 5 + side.n:]

        @pl.when(pl.program_id(0) == 0)
        def _():
            side.start(side_in, side_out, sems)
            state[...] = jnp.zeros_like(state)
            dgn_ref[...] = jnp.zeros_like(dgn_ref)

        cs, sn = cos_ref[...], sin_ref[...]
        qb, kb = q_ref[...], k_ref[...]
        qwv, kwv = qw_ref[...], kw_ref[...]
        qwb = (qb.astype(F32) * qwv).astype(BF16)
        kwb = (kb.astype(F32) * kwv).astype(BF16)
        vb = v_ref[...].astype(BF16)
        dq2 = dk2 = None
        for h in range(RET_HEADS):
            ps, mine = _pair_of(h, c)
            vs = slice(h * RET_DV, (h + 1) * RET_DV)
            dec = dec_ref[h]
            qm, km = _keep(qb[:, ps], mine), _keep(kb[:, ps], mine)
            o = o_ref[:, vs]
            mu = jnp.mean(o, axis=-1, keepdims=True)
            var = jnp.mean(jnp.square(o - mu), axis=-1, keepdims=True)
            rstd = lax.rsqrt(var + EPS)
            on = (o - mu) * rstd
            gate = g_ref[:, vs].astype(F32)
            sg = _silu(gate)
            dret = dr_ref[:, vs].astype(F32)
            gn = gn_ref[:, vs]
            dgn_ref[:, vs] += jnp.sum(dret * on * sg, axis=0, keepdims=True)
            out_ref[:, 2 * RET_QK + D_MODEL + h * RET_DV:2 * RET_QK + D_MODEL + (h + 1) * RET_DV] = (
                dret * on * gn * _dsilu(gate)).astype(out_ref.dtype)
            don = dret * gn * sg
            do = rstd * (don - jnp.mean(don, axis=-1, keepdims=True)
                         - on * jnp.mean(don * on, axis=-1, keepdims=True))
            dob = do.astype(BF16)
            sn_h = state[h]
            snb = sn_h.astype(BF16)
            s = _dot_nt(qm, kb[:, ps]) * dec
            dv = _dot_tn(s.astype(BF16), dob) + _dot(_keep(kwb[:, ps], mine), snb)
            out_ref[:, 2 * RET_QK + h * RET_DV:2 * RET_QK + (h + 1) * RET_DV] = dv.astype(out_ref.dtype)
            ds = (_dot_nt(dob, vb[:, vs]) * dec).astype(BF16)
            dq_h = _dot(ds, km) + qwv[:, ps] * _dot_nt(dob, rp_ref[0, h].astype(BF16))
            dk_h = _dot_tn(ds, qm) + kwv[:, ps] * _dot_nt(vb[:, vs], snb)
            state[h] = cd_ref[:, vs] * sn_h + _dot_tn(_keep(qwb[:, ps], mine), dob)
            if h % 2 == 0:
                dq2, dk2 = dq_h, dk_h
            else:
                dq_s[:, ps] = dq2 + dq_h
                dk_s[:, ps] = dk2 + dk_h
        out_ref[:, 0:RET_QK] = _rot(dq_s[...], cs, -sn).astype(out_ref.dtype)
        out_ref[:, RET_QK:2 * RET_QK] = (_rot(dk_s[...], cs, -sn) * (RET_DK ** -0.5)).astype(out_ref.dtype)

        @pl.when(pl.program_id(0) == nc - 1)
        def _():
            side.wait(side_in, side_out, sems)

    rev = lambda i: nc - 1 - i
    const2 = lambda shape: pl.BlockSpec(shape, lambda i: (0,) * len(shape))
    return pl.pallas_call(
        body, name="retention_bwd", grid=(nc,),
        in_specs=[pl.BlockSpec((c, RET_QK), lambda i: (rev(i), 0)), pl.BlockSpec((c, RET_QK), lambda i: (rev(i), 0)),
                  pl.BlockSpec((c, D_MODEL), lambda i: (rev(i), 1)), pl.BlockSpec((c, D_MODEL), lambda i: (rev(i), 2)),
                  pl.BlockSpec((c, RET_QK), lambda i: (rev(i), 0)), pl.BlockSpec((c, RET_QK), lambda i: (rev(i), 0)),
                  const2((RET_HEADS, c, c)), const2((c, RET_QK)), const2((c, RET_QK)), const2((1, D_MODEL)),
                  const2((1, D_MODEL)),
                  pl.BlockSpec((c, D_MODEL), lambda i: (rev(i), 0)),
                  pl.BlockSpec((1, RET_HEADS, 2 * RET_DK, RET_DV), lambda i: (rev(i), 0, 0, 0)),
                  pl.BlockSpec((c, D_MODEL), lambda i: (rev(i), 0))] + side.in_specs,
        out_specs=(pl.BlockSpec((c, 2 * RET_QK + 2 * D_MODEL), lambda i: (rev(i), 0)), const2((1, D_MODEL)),
                   *side.out_specs),
        out_shape=(jax.ShapeDtypeStruct((l, 2 * RET_QK + 4 * D_MODEL), BF16), jax.ShapeDtypeStruct((1, D_MODEL), F32),
                   *side.landing),
        scratch_shapes=[pltpu.VMEM((RET_HEADS, 2 * RET_DK, RET_DV), F32), pltpu.VMEM((c, RET_QK), F32),
                        pltpu.VMEM((c, RET_QK), F32)] + side.scratch,
        compiler_params=_cp("arbitrary"),
    )(qb_saved, kb_saved, proj, proj, cos_t, sin_t, decay, qw, kw, cd_row, gn_g, o_saved, r_prev_saved, dmix,
      *side.srcs)


def _zoh(a_re, a_im, log_dt):
    dt = jnp.exp(log_dt)
    mag = jnp.exp(a_re * dt)
    abar_re = mag * jnp.cos(a_im * dt)
    abar_im = mag * jnp.sin(a_im * dt)
    den = a_re * a_re + a_im * a_im
    nr, ni = abar_re - 1.0, abar_im
    f_re = (nr * a_re + ni * a_im) / den
    f_im = (ni * a_re - nr * a_im) / den
    return dt, abar_re, abar_im, f_re, f_im, den


def _lanes_p(f):
    return jnp.tile(f, (1, S5_P))


def _s5_discretize(a_re, a_im, log_dt, b_re_t, b_im_t):
    def body(ar_ref, ai_ref, ld_ref, br_ref, bi_ref, abr_ref, abi_ref, bbr_ref, bbi_ref):
        _, abar_re, abar_im, f_re, f_im, _ = _zoh(ar_ref[...], ai_ref[...], ld_ref[...])
        abr_ref[...] = abar_re
        abi_ref[...] = abar_im
        fr, fi = _lanes_p(f_re), _lanes_p(f_im)
        bbr_ref[...] = fr * br_ref[...] - fi * bi_ref[...]
        bbi_ref[...] = fr * bi_ref[...] + fi * br_ref[...]

    gn = jax.ShapeDtypeStruct((S5_G, S5_N), F32)
    gpn = jax.ShapeDtypeStruct((S5_G, S5_P * S5_N), F32)
    return pl.pallas_call(body, name="s5_discretize", out_shape=(gn, gn, gpn, gpn))(a_re, a_im, log_dt, b_re_t, b_im_t)


def _s5_discretize_bwd(a_re, a_im, log_dt, b_re_t, b_im_t, dab_re, dab_im, dbb_re_t, dbb_im_t):
    def body(ar_ref, ai_ref, ld_ref, br_ref, bi_ref, gar_ref, gai_ref, gbr_ref, gbi_ref,
             dar_ref, dai_ref, dld_ref, dbr_ref, dbi_ref):
        a_r, a_i = ar_ref[...], ai_ref[...]
        dt, abar_re, abar_im, f_re, f_im, den = _zoh(a_r, a_i, ld_ref[...])
        b_r, b_i, g_br, g_bi = br_ref[...], bi_ref[...], gbr_ref[...], gbi_ref[...]
        fr, fi = _lanes_p(f_re), _lanes_p(f_im)
        dbr_ref[...] = fr * g_br + fi * g_bi
        dbi_ref[...] = fr * g_bi - fi * g_br
        t_r = b_r * g_br + b_i * g_bi
        t_i = b_r * g_bi - b_i * g_br
        gf_r = sum(t_r[:, p * S5_N:(p + 1) * S5_N] for p in range(S5_P))
        gf_i = sum(t_i[:, p * S5_N:(p + 1) * S5_N] for p in range(S5_P))
        inv_r, inv_i = a_r / den, a_i / den
        ga_r = gar_ref[...] + gf_r * inv_r - gf_i * inv_i
        ga_i = gai_ref[...] + gf_r * inv_i + gf_i * inv_r
        q_r = -(f_re * a_r + f_im * a_i) / den
        q_i = -(f_im * a_r - f_re * a_i) / den
        gl_r = q_r * gf_r + q_i * gf_i
        gl_i = q_r * gf_i - q_i * gf_r
        dar_ref[...] = gl_r + dt * (abar_re * ga_r + abar_im * ga_i)
        dai_ref[...] = gl_i + dt * (abar_re * ga_i - abar_im * ga_r)
        la_r = a_r * abar_re - a_i * abar_im
        la_i = a_r * abar_im + a_i * abar_re
        dld_ref[...] = dt * jnp.sum(ga_r * la_r + ga_i * la_i, axis=-1, keepdims=True)

    gn = jax.ShapeDtypeStruct((S5_G, S5_N), F32)
    gpn = jax.ShapeDtypeStruct((S5_G, S5_P * S5_N), F32)
    return pl.pallas_call(
        body, name="s5_discretize_bwd", out_shape=(gn, gn, jax.ShapeDtypeStruct((S5_G, 1), F32), gpn, gpn),
    )(a_re, a_im, log_dt, b_re_t, b_im_t, dab_re, dab_im, dbb_re_t, dbb_im_t)


S5_ZQ = S5_NB // 2


def _s5_z(re, im):
    return jnp.concatenate([re.reshape(S5_ZQ, 8, 128), im.reshape(S5_ZQ, 8, 128)], axis=0)


def _s5_unz(z):
    return z[:S5_ZQ].reshape(S5_G, S5_N), z[S5_ZQ:].reshape(S5_G, S5_N)


def _s5_block_mats(bb_re, bb_im, c_re, c_im):
    eye = jnp.eye(S5_GB, dtype=F32)
    bb = jnp.stack([bb_re, bb_im], axis=0).reshape(2, S5_NB, S5_GB, S5_N, S5_P)
    bbm = jnp.einsum("rbgnp,gh->bgprhn", bb, eye).reshape(S5_NB, S5_GB * S5_P, 2 * S5_BS)
    cc = jnp.stack([c_re, -c_im], axis=0).reshape(2, S5_NB, S5_GB, S5_P, S5_N)
    ccm = jnp.einsum("rbgpn,gh->brhngp", cc, eye).reshape(S5_NB, 2 * S5_BS, S5_GB * S5_P)
    return bbm.astype(BF16), ccm.astype(BF16)


def _s5_block_diag_bb(m):
    t = m.reshape(S5_NB, S5_GB, S5_P, 2, S5_GB, S5_N)
    d = jnp.einsum("bgprgn->rbgnp", t).reshape(2, S5_G, S5_N, S5_P)
    return d[0], d[1]


def _s5_block_diag_cc(m):
    t = m.reshape(S5_NB, 2, S5_GB, S5_N, S5_GB, S5_P)
    d = jnp.einsum("brgngp->rbgpn", t).reshape(2, S5_G, S5_P, S5_N)
    return d[0], -d[1]


SCAN_UNROLL = 8


def _z_store(zr, zi, blk, res, t, off):
    q, h = blk // 2, blk % 2
    for lt in range(4):
        zr[q, pl.ds(off + 4 * h + lt, t, stride=8), :] = res[:, lt * 128:(lt + 1) * 128]
        zi[q, pl.ds(off + 4 * h + lt, t, stride=8), :] = res[:, S5_BS + lt * 128:S5_BS + (lt + 1) * 128]


def _z_load(zr, zi, blk, t, off):
    q, h = blk // 2, blk % 2
    return jnp.concatenate([zr[q, pl.ds(off + 4 * h + lt, t, stride=8), :] for lt in range(4)]
                           + [zi[q, pl.ds(off + 4 * h + lt, t, stride=8), :] for lt in range(4)], axis=1)


def _z_scan_fwd(zr, zi, a_ref, carry_ref, t, off):
    ar = [a_ref[q] for q in range(S5_ZQ)]
    ai = [a_ref[S5_ZQ + q] for q in range(S5_ZQ)]

    def step(it, carry):
        carry = list(carry)
        base = pl.multiple_of(it * (8 * SCAN_UNROLL), 8 * SCAN_UNROLL) + off
        for tt in range(SCAN_UNROLL):
            rows = pl.ds(base + 8 * tt, 8)
            for q in range(S5_ZQ):
                c_r, c_i = carry[q], carry[S5_ZQ + q]
                n_r = ar[q] * c_r - ai[q] * c_i + zr[q, rows, :]
                n_i = ar[q] * c_i + ai[q] * c_r + zi[q, rows, :]
                zr[q, rows, :] = n_r
                zi[q, rows, :] = n_i
                carry[q], carry[S5_ZQ + q] = n_r, n_i
        return tuple(carry)

    out = lax.fori_loop(0, t // SCAN_UNROLL, step, tuple(carry_ref[k] for k in range(2 * S5_ZQ)))
    for k in range(2 * S5_ZQ):
        carry_ref[k] = out[k]


def _z_scan_bwd(lr, li, xr, xi, a_ref, carry_ref, acc_ref, t):
    ar = [a_ref[q] for q in range(S5_ZQ)]
    ai = [a_ref[S5_ZQ + q] for q in range(S5_ZQ)]
    n_it = t // SCAN_UNROLL

    def step(it, state):
        carry, acc = list(state[0]), list(state[1])
        base = pl.multiple_of((n_it - 1 - it) * (8 * SCAN_UNROLL), 8 * SCAN_UNROLL)
        for tt in reversed(range(SCAN_UNROLL)):
            rows = pl.ds(base + 8 * tt, 8)
            for q in range(S5_ZQ):
                c_r, c_i = carry[q], carry[S5_ZQ + q]
                n_r = ar[q] * c_r + ai[q] * c_i + lr[q, rows, :]
                n_i = ar[q] * c_i - ai[q] * c_r + li[q, rows, :]
                lr[q, rows, :] = n_r
                li[q, rows, :] = n_i
                p_r, p_i = xr[q, rows, :], xi[q, rows, :]
                acc[q] = acc[q] + n_r * p_r + n_i * p_i
                acc[S5_ZQ + q] = acc[S5_ZQ + q] + n_i * p_r - n_r * p_i
                carry[q], carry[S5_ZQ + q] = n_r, n_i
        return tuple(carry), tuple(acc)

    k8 = range(2 * S5_ZQ)
    carry, acc = lax.fori_loop(0, n_it, step, (tuple(carry_ref[k] for k in k8), tuple(acc_ref[k] for k in k8)))
    for k in k8:
        carry_ref[k] = carry[k]
        acc_ref[k] = acc[k]


def _s5_fwd(proj, mix, bbm, ccm, d_row, glu_w, glu_b, tabs, t, side):
    l = proj.shape[0]
    nt = l // t
    n_in = 9

    def body(*refs):
        u_ref, gs_ref, bb_ref, cc_ref, d_ref, gw_ref, gb_ref, a_ref, _ = refs[:n_in]
        side_in = refs[n_in:n_in + len(side.srcs)]
        ssm_ref, xst_ref, y1_ref = refs[n_in + len(side.srcs):n_in + len(side.srcs) + 3]
        side_out = refs[n_in + len(side.srcs) + 3:n_in + len(side.srcs) + 3 + side.n]
        zr, zi, carry = refs[n_in + len(side.srcs) + 3 + side.n:n_in + len(side.srcs) + 6 + side.n]
        sems = refs[n_in + len(side.srcs) + 6 + side.n:]

        @pl.when(pl.program_id(0) == 0)
        def _():
            side.start(side_in, side_out, sems)
            carry[...] = jnp.zeros_like(carry)

        xst_ref[0] = carry[...]
        ub = u_ref[...]
        u = ub.astype(F32)
        for blk in range(S5_NB):
            _z_store(zr, zi, blk, _dot(ub[:, blk * 128:(blk + 1) * 128], bb_ref[blk]), t, 0)
        _z_scan_fwd(zr, zi, a_ref, carry, t, 0)
        ys = jnp.concatenate(
            [_dot(_z_load(zr, zi, blk, t, 0).astype(BF16), cc_ref[blk]) for blk in range(S5_NB)], axis=1)
        y1 = ys + d_ref[...] * u
        y1_ref[...] = y1.astype(y1_ref.dtype)
        y2 = _gelu(y1)
        z = _dot(y2.astype(BF16), gw_ref[...]) + gb_ref[...]
        ssm_ref[...] = (y2 * _sigmoid(z) * _silu(gs_ref[...].astype(F32))).astype(ssm_ref.dtype)

        @pl.when(pl.program_id(0) == nt - 1)
        def _():
            side.wait(side_in, side_out, sems)

    const2 = lambda shape: pl.BlockSpec(shape, lambda i: (0,) * len(shape))
    zshape = (2 * S5_ZQ, 8, 128)
    return pl.pallas_call(
        body, name="s5_fwd", grid=(nt,),
        in_specs=[pl.BlockSpec((t, D_MODEL), lambda i: (i, 3)), pl.BlockSpec((t, D_MODEL), lambda i: (i, 4)),
                  const2(bbm.shape), const2(ccm.shape), const2((1, D_MODEL)), const2((D_MODEL, D_MODEL)),
                  const2((1, D_MODEL)), const2(zshape), pl.BlockSpec(memory_space=pl.ANY)] + side.in_specs,
        out_specs=(pl.BlockSpec((t, D_MODEL), lambda i: (i, 1)), pl.BlockSpec((1,) + zshape, lambda i: (i, 0, 0, 0)),
                   pl.BlockSpec((t, D_MODEL), lambda i: (i, 0)), *side.out_specs),
        out_shape=(jax.ShapeDtypeStruct((l, 2 * D_MODEL), BF16), jax.ShapeDtypeStruct((nt,) + zshape, F32),
                   jax.ShapeDtypeStruct((l, D_MODEL), BF16), *side.landing),
        scratch_shapes=[pltpu.VMEM((S5_ZQ, 8 * t, 128), F32), pltpu.VMEM((S5_ZQ, 8 * t, 128), F32),
                        pltpu.VMEM(zshape, F32)] + side.scratch,
        input_output_aliases={8: 0},
        compiler_params=_cp("arbitrary"),
    )(proj, proj, bbm, ccm, d_row, glu_w, glu_b, tabs, mix, *side.srcs)


def _s5_bwd(proj, dmix, dproj, xstart, y1, bbm, ccm, d_row, glu_w, glu_b, tabs, t):
    l = proj.shape[0]
    nt = l // t
    col0 = 2 * RET_QK + 2 * D_MODEL

    def body(u_ref, gs_ref, dm_ref, xst_ref, bb_ref, cc_ref, d_ref, gw_ref, gb_ref, a_ref, _, y1_ref,
             dp_ref, y2_ref, dz_ref, dbb_ref, dcc_ref, da_ref, dd_ref, dgb_ref, xr, xi, lr, li, carry, lcarry,
             dug_s, dug_sem):
        step = pl.program_id(0)
        slot = step % 2
        dug_ref = dug_s.at[slot]

        def put(s, at_step):
            rows = pl.ds(pl.multiple_of((nt - 1 - at_step) * t, t), t)
            return pltpu.make_async_copy(dug_s.at[s], dp_ref.at[rows, pl.ds(col0, 2 * D_MODEL)], dug_sem.at[s])

        @pl.when(step >= 2)
        def _():
            put(slot, step - 2).wait()

        @pl.when(step == 0)
        def _():
            lcarry[...] = jnp.zeros_like(lcarry)
            dbb_ref[...] = jnp.zeros_like(dbb_ref)
            dcc_ref[...] = jnp.zeros_like(dcc_ref)
            da_ref[...] = jnp.zeros_like(da_ref)
            dd_ref[...] = jnp.zeros_like(dd_ref)
            dgb_ref[...] = jnp.zeros_like(dgb_ref)

        carry[...] = xst_ref[0]
        for q in range(S5_ZQ):
            xr[q, 0:8, :] = carry[q]
            xi[q, 0:8, :] = carry[S5_ZQ + q]
        ub = u_ref[...]
        u = ub.astype(F32)
        for blk in range(S5_NB):
            _z_store(xr, xi, blk, _dot(ub[:, blk * 128:(blk + 1) * 128], bb_ref[blk]), t, 8)
        _z_scan_fwd(xr, xi, a_ref, carry, t, 8)
        dv = d_ref[...]
        y2, dgelu = _gelu_and_grad(y1_ref[...].astype(F32))
        y2b = y2.astype(BF16)
        sg = _sigmoid(_dot(y2b, gw_ref[...]) + gb_ref[...])
        gs = gs_ref[...].astype(F32)
        dssm = dm_ref[...].astype(F32)
        dug_ref[:, D_MODEL:] = (dssm * (y2 * sg) * _dsilu(gs)).astype(dug_ref.dtype)
        dy3 = dssm * _silu(gs)
        dz = dy3 * y2 * sg * (1.0 - sg)
        dzb = dz.astype(BF16)
        y2_ref[...] = y2b
        dz_ref[...] = dzb
        dgb_ref[...] += jnp.sum(dz, axis=0, keepdims=True)
        dy1 = (dy3 * sg + _dot_nt(dzb, gw_ref[...])) * dgelu
        dd_ref[...] += jnp.sum(dy1 * u, axis=0, keepdims=True)
        dyb = dy1.astype(BF16)
        for blk in range(S5_NB):
            ch = slice(blk * 128, (blk + 1) * 128)
            _z_store(lr, li, blk, _dot_nt(dyb[:, ch], cc_ref[blk]), t, 0)
            dcc_ref[blk] += _dot_tn(_z_load(xr, xi, blk, t, 8).astype(BF16), dyb[:, ch])
        _z_scan_bwd(lr, li, xr, xi, a_ref, lcarry, da_ref, t)
        du = []
        for blk in range(S5_NB):
            lb = _z_load(lr, li, blk, t, 0).astype(BF16)
            du.append(_dot_nt(lb, bb_ref[blk]))
            dbb_ref[blk] += _dot_tn(ub[:, blk * 128:(blk + 1) * 128], lb)
        dug_ref[:, :D_MODEL] = (jnp.concatenate(du, axis=1) + dy1 * dv).astype(dug_ref.dtype)
        put(slot, step).start()

        @pl.when(step == nt - 1)
        def _():
            put(slot, step).wait()
            if nt > 1:
                put(1 - slot, step - 1).wait()

    rev = lambda i: nt - 1 - i
    const2 = lambda shape: pl.BlockSpec(shape, lambda i: (0,) * len(shape))
    row_out = lambda w: pl.BlockSpec((t, w), lambda i: (rev(i), 0))
    zshape = (2 * S5_ZQ, 8, 128)
    hbm = pl.BlockSpec(memory_space=pl.ANY)
    return pl.pallas_call(
        body, name="s5_bwd", grid=(nt,),
        in_specs=[pl.BlockSpec((t, D_MODEL), lambda i: (rev(i), 3)), pl.BlockSpec((t, D_MODEL), lambda i: (rev(i), 4)),
                  pl.BlockSpec((t, D_MODEL), lambda i: (rev(i), 1)),
                  pl.BlockSpec((1,) + zshape, lambda i: (rev(i), 0, 0, 0)),
                  const2(bbm.shape), const2(ccm.shape), const2((1, D_MODEL)), const2((D_MODEL, D_MODEL)),
                  const2((1, D_MODEL)), const2(zshape), hbm, pl.BlockSpec((t, D_MODEL), lambda i: (rev(i), 0))],
        out_specs=(hbm, row_out(D_MODEL), row_out(D_MODEL), const2(bbm.shape), const2(ccm.shape),
                   const2(zshape), const2((1, D_MODEL)), const2((1, D_MODEL))),
        out_shape=(jax.ShapeDtypeStruct(dproj.shape, BF16), jax.ShapeDtypeStruct((l, D_MODEL), BF16),
                   jax.ShapeDtypeStruct((l, D_MODEL), BF16), jax.ShapeDtypeStruct(bbm.shape, F32),
                   jax.ShapeDtypeStruct(ccm.shape, F32), jax.ShapeDtypeStruct(zshape, F32),
                   jax.ShapeDtypeStruct((1, D_MODEL), F32), jax.ShapeDtypeStruct((1, D_MODEL), F32)),
        scratch_shapes=[pltpu.VMEM((S5_ZQ, 8 * t + 8, 128), F32), pltpu.VMEM((S5_ZQ, 8 * t + 8, 128), F32),
                        pltpu.VMEM((S5_ZQ, 8 * t, 128), F32), pltpu.VMEM((S5_ZQ, 8 * t, 128), F32),
                        pltpu.VMEM(zshape, F32), pltpu.VMEM(zshape, F32),
                        pltpu.VMEM((2, t, 2 * D_MODEL), BF16), pltpu.SemaphoreType.DMA((2,))],
        input_output_aliases={10: 0},
        compiler_params=_cp("arbitrary"),
    )(proj, proj, dmix, xstart, bbm, ccm, d_row, glu_w, glu_b, tabs, dproj, y1)


def _attn_probs(qh, kh):
    s = _dot_nt(qh, kh) * (XA_DH ** -0.5)
    e = jnp.exp(s - jnp.max(s, axis=-1, keepdims=True))
    return e / jnp.sum(e, axis=-1, keepdims=True)


def _attn_fwd(qa, ka, va):
    l = qa.shape[0]
    m = ka.shape[0]
    tl = _pick(l, (2048, 1024, 512, 256))

    def body(q_ref, k_ref, v_ref, o_ref):
        for h in range(XA_HEADS):
            hs = slice(h * XA_DH, (h + 1) * XA_DH)
            p = _attn_probs(q_ref[:, hs], k_ref[:, hs])
            o_ref[:, hs] = _dot(p.astype(BF16), v_ref[:, hs]).astype(o_ref.dtype)

    return pl.pallas_call(
        body, name="xattn_fwd", grid=(l // tl,),
        in_specs=[pl.BlockSpec((tl, D_MODEL), lambda i: (i, 0)), pl.BlockSpec((m, D_MODEL), lambda i: (0, 0)),
                  pl.BlockSpec((m, D_MODEL), lambda i: (0, 0))],
        out_specs=pl.BlockSpec((tl, D_MODEL), lambda i: (i, 0)),
        out_shape=jax.ShapeDtypeStruct((l, D_MODEL), BF16), compiler_params=_cp("parallel"),
    )(qa, ka, va)


def _attn_bwd(qa, ka, va, doa):
    l = qa.shape[0]
    m = ka.shape[0]
    tl = _pick(l, (2048, 1024, 512, 256))

    def body(q_ref, k_ref, v_ref, do_ref, dq_ref, dk_ref, dv_ref):
        @pl.when(pl.program_id(0) == 0)
        def _():
            dk_ref[...] = jnp.zeros_like(dk_ref)
            dv_ref[...] = jnp.zeros_like(dv_ref)

        for h in range(XA_HEADS):
            hs = slice(h * XA_DH, (h + 1) * XA_DH)
            qh, kh, vh, doh = q_ref[:, hs], k_ref[:, hs], v_ref[:, hs], do_ref[:, hs]
            p = _attn_probs(qh, kh)
            dv_ref[:, hs] += _dot_tn(p.astype(BF16), doh)
            dp = _dot_nt(doh, vh)
            ds = (p * (dp - jnp.sum(dp * p, axis=-1, keepdims=True)) * (XA_DH ** -0.5)).astype(BF16)
            dq_ref[:, hs] = _dot(ds, kh).astype(dq_ref.dtype)
            dk_ref[:, hs] += _dot_tn(ds, qh)

    row = pl.BlockSpec((tl, D_MODEL), lambda i: (i, 0))
    mem = pl.BlockSpec((m, D_MODEL), lambda i: (0, 0))
    return pl.pallas_call(
        body, name="xattn_bwd", grid=(l // tl,), in_specs=[row, mem, mem, row], out_specs=(row, mem, mem),
        out_shape=(jax.ShapeDtypeStruct((l, D_MODEL), BF16), jax.ShapeDtypeStruct((m, D_MODEL), F32),
                   jax.ShapeDtypeStruct((m, D_MODEL), F32)),
        compiler_params=_cp("arbitrary"),
    )(qa, ka, va, doa)


def _me_and_peers():
    x, y, c = lax.axis_index("x"), lax.axis_index("y"), lax.axis_index("c")
    flip = lambda v, bit: (1 - v) if bit else v
    peers = []
    for k in range(1, N_DEV):
        px, py, pc = flip(x, (k >> 2) & 1), flip(y, (k >> 1) & 1), flip(c, k & 1)
        peers.append(((px, py, pc), 4 * px + 2 * py + pc))
    return 4 * x + 2 * y + c, peers


class _SideJob:
    def __init__(self, srcs, landing, src_of, dst_of):
        self.srcs = list(srcs)
        self.landing = list(landing)
        self.n = len(self.landing)
        self.src_of, self.dst_of = src_of, dst_of
        hbm = pl.BlockSpec(memory_space=pl.ANY)
        self.in_specs = [hbm] * len(self.srcs)
        self.out_specs = [hbm] * self.n
        self.scratch = [pltpu.SemaphoreType.DMA((self.n * (N_DEV - 1),)), pltpu.SemaphoreType.DMA((self.n * (N_DEV - 1),)),
                        pltpu.SemaphoreType.DMA((self.n,))]

    def _copies(self, src_refs, out_refs, sems):
        send_sems, recv_sems, loc_sems = sems
        me, peers = _me_and_peers()
        local = [pltpu.make_async_copy(self.src_of(a, me, src_refs), self.dst_of(a, me, out_refs), loc_sems.at[a])
                 for a in range(self.n)]
        sends, recvs = [], []
        for k, (peer, peer_idx) in enumerate(peers):
            for a in range(self.n):
                s = self.n * k + a
                sends.append(pltpu.make_async_remote_copy(
                    src_ref=self.src_of(a, peer_idx, src_refs), dst_ref=self.dst_of(a, me, out_refs),
                    send_sem=send_sems.at[s], recv_sem=recv_sems.at[s], device_id=peer, device_id_type=MESH))
                recvs.append(pltpu.make_async_remote_copy(
                    src_ref=self.src_of(a, me, src_refs), dst_ref=self.dst_of(a, peer_idx, out_refs),
                    send_sem=send_sems.at[s], recv_sem=recv_sems.at[s], device_id=peer, device_id_type=MESH))
        return local, sends, recvs

    def start(self, src_refs, out_refs, sems):
        if not self.n:
            return
        local, sends, _ = self._copies(src_refs, out_refs, sems)
        for cp in local + sends:
            cp.start()

    def wait(self, src_refs, out_refs, sems):
        if not self.n:
            return
        local, sends, recvs = self._copies(src_refs, out_refs, sems)
        for cp in recvs:
            cp.wait_recv()
        for cp in sends:
            cp.wait_send()
        for cp in local:
            cp.wait()


def _gather_job(shards):
    return _SideJob(shards, [jax.ShapeDtypeStruct((N_DEV,) + s.shape, s.dtype) for s in shards],
                    src_of=lambda a, j, srcs: srcs[a], dst_of=lambda a, j, outs: outs[a].at[j])


def _scatter_job(grads):
    landing, parts = [], []
    for g in grads:
        if g.ndim == 3:
            landing.append(jax.ShapeDtypeStruct(g.shape, g.dtype))
            parts.append(None)
        else:
            r = g.shape[0] // N_DEV
            landing.append(jax.ShapeDtypeStruct((N_DEV, r, g.shape[1]), g.dtype))
            parts.append(r)

    def src_of(a, j, srcs):
        if parts[a] is None:
            return srcs[a].at[j]
        return srcs[a].at[pl.ds(pl.multiple_of(j * parts[a], 8), parts[a]), :]

    return _SideJob(grads, landing, src_of=src_of, dst_of=lambda a, j, outs: outs[a].at[j])


def _prologue(w_in_shard, row_shards, x, g, pos_col, inv_row):
    n_row = len(row_shards)
    l, d = x.shape
    tr = _pick(l, (1024, 512, 256))
    nt = l // tr
    mid = nt // 2

    def body(*refs):
        win_ref = refs[0]
        row_refs = refs[1:1 + n_row]
        x_ref, g_ref, p_ref, inv_ref = refs[1 + n_row:5 + n_row]
        out_win = refs[5 + n_row]
        row_outs = refs[6 + n_row:6 + 2 * n_row]
        h_ref, cos_ref, sin_ref = refs[6 + 2 * n_row:9 + 2 * n_row]
        win_b, send_sems, recv_sems, local_sem = refs[9 + 2 * n_row:]
        step = pl.program_id(0)
        cx, cy, cc = lax.axis_index("x"), lax.axis_index("y"), lax.axis_index("c")
        me, sibling = (cx, cy, cc), (cx, cy, 1 - cc)
        chips = [(1 - cx, cy), (cx, 1 - cy), (1 - cx, 1 - cy)]
        slot = lambda p: out_win.at[4 * p[0] + 2 * p[1] + p[2]]

        def copy(k, block, to, src=None):
            return pltpu.make_async_remote_copy(
                src_ref=slot(block) if src is None else src, dst_ref=slot(block), send_sem=send_sems.at[k],
                recv_sem=recv_sems.at[k], device_id=to, device_id_type=MESH)

        mine = pltpu.make_async_copy(win_b, slot(me), local_sem)
        first = [copy(0, me, sibling, src=win_b)]
        first += [copy(1 + j, me, (*chip, cc), src=win_b) for j, chip in enumerate(chips)]
        passed = [copy(4 + j, (*chip, cc), sibling) for j, chip in enumerate(chips)]

        @pl.when(step == 0)
        def _():
            win_b[...] = win_ref[...].astype(BF16)
            mine.start()
            for cp in first:
                cp.start()
            for r, o in zip(row_refs, row_outs):
                o[...] = r[...].astype(BF16)

        h_ref[...] = (_rms(x_ref[...])[1] * g_ref[...]).astype(h_ref.dtype)
        ang = p_ref[...].astype(F32) * inv_ref[...]
        lane = lax.broadcasted_iota(jnp.int32, ang.shape, 1)
        cos_ref[...] = jnp.tile(jnp.cos(ang), (1, RET_QK // 128))
        sin_ref[...] = jnp.tile(jnp.where((lane % RET_DK) < RET_DK // 2, -jnp.sin(ang), jnp.sin(ang)),
                                (1, RET_QK // 128))

        @pl.when(step == mid)
        def _():
            for j, chip in enumerate(chips):
                copy(1 + j, (*chip, cc), me).wait_recv()
                passed[j].start()

        @pl.when(step == nt - 1)
        def _():
            copy(0, sibling, me).wait_recv()
            for j, chip in enumerate(chips):
                copy(4 + j, (*chip, 1 - cc), me).wait_recv()
            for cp in first + passed:
                cp.wait_send()
            mine.wait()

    whole = lambda a: pl.BlockSpec(a.shape, lambda i: (0,) * a.ndim)
    rows = lambda w: pl.BlockSpec((tr, w), lambda i: (i, 0))
    return pl.pallas_call(
        body, name="prologue_allgather_w_in", grid=(nt,),
        in_specs=[whole(w_in_shard)] + [whole(r) for r in row_shards] + [rows(d), whole(g), rows(1), whole(inv_row)],
        out_specs=(pl.BlockSpec(memory_space=pl.ANY), *[whole(r) for r in row_shards], rows(d), rows(RET_QK),
                   rows(RET_QK)),
        out_shape=(jax.ShapeDtypeStruct((N_DEV,) + w_in_shard.shape, BF16),
                   *[jax.ShapeDtypeStruct(r.shape, BF16) for r in row_shards],
                   jax.ShapeDtypeStruct((l, d), BF16), jax.ShapeDtypeStruct((l, RET_QK), F32),
                   jax.ShapeDtypeStruct((l, RET_QK), F32)),
        scratch_shapes=[pltpu.VMEM(w_in_shard.shape, BF16), pltpu.SemaphoreType.DMA((N_DEV - 1,)),
                        pltpu.SemaphoreType.DMA((N_DEV - 1,)), pltpu.SemaphoreType.DMA],
        compiler_params=_cp("arbitrary"),
    )(w_in_shard, *row_shards, x, g, pos_col, inv_row)


def _allreduce_small(small):
    rows = SMALL_ROWS // N_DEV

    def body(x_ref, out_ref, land, send1, recv1, send2, recv2):
        me, peers = _me_and_peers()
        block = lambda j: pl.ds(pl.multiple_of(j * rows, 8), rows)

        def phase(src_of, dst_of, send_sems, recv_sems):
            sends = [pltpu.make_async_remote_copy(src_ref=src_of(pidx), dst_ref=dst_of(me), send_sem=send_sems.at[k],
                                                  recv_sem=recv_sems.at[k], device_id=peer, device_id_type=MESH)
                     for k, (peer, pidx) in enumerate(peers)]
            recvs = [pltpu.make_async_remote_copy(src_ref=src_of(me), dst_ref=dst_of(pidx), send_sem=send_sems.at[k],
                                                  recv_sem=recv_sems.at[k], device_id=peer, device_id_type=MESH)
                     for k, (peer, pidx) in enumerate(peers)]
            for cp in sends:
                cp.start()
            for cp in recvs:
                cp.wait_recv()
            for cp in sends:
                cp.wait_send()

        land[me] = x_ref[block(me), :]
        phase(lambda j: x_ref.at[block(j), :], lambda j: land.at[j], send1, recv1)
        total = land[0]
        for j in range(1, N_DEV):
            total = total + land[j]
        out_ref[block(me), :] = total
        phase(lambda j: out_ref.at[block(me), :], lambda j: out_ref.at[block(j), :], send2, recv2)

    vm = pl.BlockSpec(memory_space=pltpu.VMEM)
    return pl.pallas_call(
        body, name="allreduce_small", in_specs=[vm], out_specs=vm, out_shape=jax.ShapeDtypeStruct(small.shape, F32),
        scratch_shapes=[pltpu.VMEM((N_DEV, rows, D_MODEL), F32)] + [pltpu.SemaphoreType.DMA((N_DEV - 1,))] * 4,
    )(small)


def _adamw(name, got, w, m, v):
    r, c = w.shape
    n_slots = got.shape[0]
    tr = _pick(r, (256, 128, 64))

    def body(got_ref, w_ref, m_ref, v_ref, g_ref, d_ref, nm_ref, nv_ref):
        g = got_ref[0].astype(F32)
        for j in range(1, n_slots):
            g = g + got_ref[j].astype(F32)
        nm = ADAM_B1 * m_ref[...] + (1.0 - ADAM_B1) * g
        nv = ADAM_B2 * v_ref[...] + (1.0 - ADAM_B2) * jnp.square(g)
        m_hat = nm / (1.0 - ADAM_B1 ** ADAM_STEP)
        v_hat = nv / (1.0 - ADAM_B2 ** ADAM_STEP)
        g_ref[...] = g
        d_ref[...] = -ADAM_LR * (m_hat / (jnp.sqrt(v_hat) + ADAM_EPS) + ADAM_WD * w_ref[...])
        nm_ref[...] = nm
        nv_ref[...] = nv

    blk = pl.BlockSpec((tr, c), lambda i: (i, 0))
    out = jax.ShapeDtypeStruct((r, c), F32)
    return pl.pallas_call(
        body, name=name, grid=(r // tr,),
        in_specs=[pl.BlockSpec((n_slots, tr, c), lambda i: (0, i, 0)), blk, blk, blk],
        out_specs=(blk, blk, blk, blk), out_shape=(out, out, out, out), compiler_params=_cp("parallel"),
    )(got, w, m, v)


_SMALL_VECS = ("norm1_g", "ret_gn_g", "s5_d", "s5_glu_b", "norm2_g", "norm_mem_g", "norm_f_g")


def _small_layout():
    lay, row = {}, 0
    for n in _SMALL_VECS + ("loss",):
        lay[n] = (row, 1, D_MODEL)
        row += 1
    for n in ("s5_a_re", "s5_a_im"):
        lay[n] = (row, 4, D_MODEL)
        row += 4
    lay["s5_log_dt"] = (row, 1, S5_G)
    row += 8
    for n in ("s5_b_re", "s5_b_im", "s5_c_re", "s5_c_im"):
        lay[n] = (row, 64, D_MODEL)
        row += 64
    assert row <= SMALL_ROWS
    return lay


def _pack_small(t, loss_row=None):
    lay = _small_layout()
    pieces = [t[n].reshape(1, D_MODEL) for n in _SMALL_VECS]
    pieces.append(jnp.zeros((1, D_MODEL), F32) if loss_row is None else loss_row)
    pieces += [t["s5_a_re"].reshape(4, D_MODEL), t["s5_a_im"].reshape(4, D_MODEL)]
    pieces.append(jnp.pad(t["s5_log_dt"].reshape(1, S5_G), ((0, 7), (0, D_MODEL - S5_G))))
    pieces += [t[n].reshape(64, D_MODEL) for n in ("s5_b_re", "s5_b_im", "s5_c_re", "s5_c_im")]
    pieces.append(jnp.zeros((SMALL_ROWS - lay["s5_c_im"][0] - 64, D_MODEL), F32))
    return jnp.concatenate(pieces, axis=0)


def _adamw_small(g_sum, w, m, v):
    lay = _small_layout()
    names = [n for n in lay if n != "loss"]

    def body(g_ref, w_ref, m_ref, v_ref, *outs):
        g = g_ref[...]
        nm = ADAM_B1 * m_ref[...] + (1.0 - ADAM_B1) * g
        nv = ADAM_B2 * v_ref[...] + (1.0 - ADAM_B2) * jnp.square(g)
        m_hat = nm / (1.0 - ADAM_B1 ** ADAM_STEP)
        v_hat = nv / (1.0 - ADAM_B2 ** ADAM_STEP)
        delta = -ADAM_LR * (m_hat / (jnp.sqrt(v_hat) + ADAM_EPS) + ADAM_WD * w_ref[...])
        for i, n in enumerate(names):
            r0, rows, lanes = lay[n]
            for part, val in enumerate((g, delta, nm, nv)):
                outs[4 * i + part][...] = val[r0:r0 + rows, 0:lanes]
        r0 = lay["loss"][0]
        outs[-1][...] = g[r0:r0 + 1, :]

    shapes = []
    for n in names:
        shapes += [jax.ShapeDtypeStruct(lay[n][1:], F32)] * 4
    shapes.append(jax.ShapeDtypeStruct((1, D_MODEL), F32))
    outs = pl.pallas_call(body, name="adamw_small", out_shape=tuple(shapes),
                          compiler_params=pltpu.CompilerParams(vmem_limit_bytes=VMEM_LIMIT))(g_sum, w, m, v)
    return {n: tuple(outs[4 * i:4 * i + 4]) for i, n in enumerate(names)}, outs[-1]


_W_NAMES = ("norm1_g", "w_in", "ret_gn_g", "s5_a_re", "s5_a_im", "s5_log_dt", "s5_b_re", "s5_b_im", "s5_c_re", "s5_c_im",
            "s5_d", "s5_glu_w", "s5_glu_b", "w_out", "norm2_g", "norm_mem_g", "xa_wq", "xa_wk", "xa_wv", "xa_wo",
            "norm_f_g")
_ROW_NAMES = ("s5_glu_w", "w_out", "xa_wq", "xa_wk", "xa_wv", "xa_wo")


def kernel(x, mem, positions, norm1_g, w_in, ret_gn_g, s5_a_re, s5_a_im, s5_log_dt, s5_b_re, s5_b_im, s5_c_re, s5_c_im, s5_d, s5_glu_w, s5_glu_b, w_out, norm2_g, norm_mem_g, xa_wq, xa_wk, xa_wv, xa_wo, norm_f_g, loss_target, m_norm1_g, m_w_in, m_ret_gn_g, m_s5_a_re, m_s5_a_im, m_s5_log_dt, m_s5_b_re, m_s5_b_im, m_s5_c_re, m_s5_c_im, m_s5_d, m_s5_glu_w, m_s5_glu_b, m_w_out, m_norm2_g, m_norm_mem_g, m_xa_wq, m_xa_wk, m_xa_wv, m_xa_wo, m_norm_f_g, v_norm1_g, v_w_in, v_ret_gn_g, v_s5_a_re, v_s5_a_im, v_s5_log_dt, v_s5_b_re, v_s5_b_im, v_s5_c_re, v_s5_c_im, v_s5_d, v_s5_glu_w, v_s5_glu_b, v_w_out, v_norm2_g, v_norm_mem_g, v_xa_wq, v_xa_wk, v_xa_wv, v_xa_wo, v_norm_f_g):
    w = dict(norm1_g=norm1_g, w_in=w_in, ret_gn_g=ret_gn_g, s5_a_re=s5_a_re, s5_a_im=s5_a_im, s5_log_dt=s5_log_dt,
             s5_b_re=s5_b_re, s5_b_im=s5_b_im, s5_c_re=s5_c_re, s5_c_im=s5_c_im, s5_d=s5_d, s5_glu_w=s5_glu_w,
             s5_glu_b=s5_glu_b, w_out=w_out, norm2_g=norm2_g, norm_mem_g=norm_mem_g, xa_wq=xa_wq, xa_wk=xa_wk,
             xa_wv=xa_wv, xa_wo=xa_wo, norm_f_g=norm_f_g)
    mom = dict(norm1_g=m_norm1_g, w_in=m_w_in, ret_gn_g=m_ret_gn_g, s5_a_re=m_s5_a_re, s5_a_im=m_s5_a_im,
               s5_log_dt=m_s5_log_dt, s5_b_re=m_s5_b_re, s5_b_im=m_s5_b_im, s5_c_re=m_s5_c_re, s5_c_im=m_s5_c_im,
               s5_d=m_s5_d, s5_glu_w=m_s5_glu_w, s5_glu_b=m_s5_glu_b, w_out=m_w_out, norm2_g=m_norm2_g,
               norm_mem_g=m_norm_mem_g, xa_wq=m_xa_wq, xa_wk=m_xa_wk, xa_wv=m_xa_wv, xa_wo=m_xa_wo,
               norm_f_g=m_norm_f_g)
    var = dict(norm1_g=v_norm1_g, w_in=v_w_in, ret_gn_g=v_ret_gn_g, s5_a_re=v_s5_a_re, s5_a_im=v_s5_a_im,
               s5_log_dt=v_s5_log_dt, s5_b_re=v_s5_b_re, s5_b_im=v_s5_b_im, s5_c_re=v_s5_c_re, s5_c_im=v_s5_c_im,
               s5_d=v_s5_d, s5_glu_w=v_s5_glu_w, s5_glu_b=v_s5_glu_b, w_out=v_w_out, norm2_g=v_norm2_g,
               norm_mem_g=v_norm_mem_g, xa_wq=v_xa_wq, xa_wk=v_xa_wk, xa_wv=v_xa_wv, xa_wo=v_xa_wo,
               norm_f_g=v_norm_f_g)
    shapes = {n: w[n].shape for n in _W_NAMES}

    x2d, mem2d, tgt = x[0], mem[0], loss_target[0]
    l = x2d.shape[0]
    ret_c = _pick(l, (256, 128))
    s5_t = _pick(l, (256, 128))
    g1, g2, gm, gf = norm1_g, norm2_g, norm_mem_g, norm_f_g.reshape(1, D_MODEL)

    half = RET_DK // 2
    inv = ROPE_BASE ** (-jnp.arange(half, dtype=F32) / half)
    win_s, *rest = _prologue(w_in[0], [w[n][0] for n in _ROW_NAMES], x2d, g1, positions[0].reshape(l, 1),
                             jnp.tile(inv, 128 // half)[None, :])
    row_shards_b, (h1, cos_t, sin_t) = rest[:len(_ROW_NAMES)], rest[len(_ROW_NAMES):]

    to_gpn = lambda b: jnp.transpose(b, (0, 2, 1)).reshape(S5_G, S5_P * S5_N)
    from_gpn = lambda b: jnp.transpose(b.reshape(S5_G, S5_P, S5_N), (0, 2, 1))
    disc_args = (s5_a_re[0], s5_a_im[0], s5_log_dt[0].reshape(S5_G, 1), to_gpn(s5_b_re[0]), to_gpn(s5_b_im[0]))
    abar_re, abar_im, bb_re_t, bb_im_t = _s5_discretize(*disc_args)
    bbm, ccm = _s5_block_mats(from_gpn(bb_re_t), from_gpn(bb_im_t), s5_c_re[0], s5_c_im[0])
    a_z = _s5_z(abar_re, abar_im)

    proj, *rows_01 = _mm_nn_slots("in_proj", h1, win_s, BF16, side=_gather_job(row_shards_b[:2]))
    full = {n: g.reshape(N_DEV * r, D_MODEL) for n, g, r in zip(_ROW_NAMES[:2], rows_01, ROW_SHARDS[:2])}
    rconsts = _ret_constants(ret_c)
    ret, o_saved, r_prev, q_rot, k_rot = _ret_fwd(proj, cos_t, sin_t, rconsts, ret_gn_g, ret_c)
    mix, xstart, y1, *rows_xa = _s5_fwd(proj, ret, bbm, ccm, s5_d, full["s5_glu_w"], s5_glu_b, a_z, s5_t,
                                        side=_gather_job(row_shards_b[2:]))
    full.update({n: g.reshape(N_DEV * r, D_MODEL) for n, g, r in zip(_ROW_NAMES[2:], rows_xa, ROW_SHARDS[2:])})
    x1, h2 = _mm_nn("out_proj", mix, full["w_out"], F32, residual=x2d, epi=_epi_norm_fwd(g2))
    mn = _rms_fwd("norm_mem_fwd", mem2d, gm)
    qa = _mm_nn("xa_q", h2, full["xa_wq"], BF16)
    ka = _mm_nn("xa_k", mn, full["xa_wk"], BF16)
    va = _mm_nn("xa_v", mn, full["xa_wv"], BF16)
    oa = _attn_fwd(qa, ka, va)
    dx2, dgf, loss_lanes = _mm_nn("xa_o", oa, full["xa_wo"], F32, residual=x1, epi=_epi_loss(gf, tgt))

    doa = _mm_nt("xa_o_dx", dx2, full["xa_wo"], BF16)
    dwo = _mm_tn("xa_o_dw", oa, dx2, BF16)
    dqa, dka, dva = _attn_bwd(qa, ka, va, doa)
    dx1, dg2 = _mm_nt("xa_q_dx", dqa, full["xa_wq"], F32, epi=_epi_norm_bwd(x1, g2, dx2))
    dwq = _mm_tn("xa_q_dw", h2, dqa, BF16)
    dwk = _mm_tn("xa_k_dw", mn, dka, BF16)
    dwv = _mm_tn("xa_v_dw", mn, dva, BF16)
    dmn = _mm_nt("xa_v_dx", dva, full["xa_wv"], F32, residual=_mm_nt("xa_k_dx", dka, full["xa_wk"], F32))
    _, dgm = _rms_bwd("norm_mem_bwd", mem2d, gm, dmn, None)
    dmix = _mm_nt("out_proj_dx", dx1, full["w_out"], BF16)
    dwout = _mm_tn("out_proj_dw", mix, dx1, BF16)
    dret, dgn, *got_a = _ret_bwd(proj, q_rot, k_rot, cos_t, sin_t, rconsts, ret_gn_g, o_saved, r_prev, dmix, ret_c,
                                 side=_scatter_job([dwout, dwq, dwk, dwv, dwo]))
    dproj, y2, dz, dbbm, dccm, dabar, dd, dgb = _s5_bwd(proj, dmix, dret, xstart, y1, bbm, ccm, s5_d,
                                                        full["s5_glu_w"], s5_glu_b, a_z, s5_t)
    dglu = _mm_tn("s5_glu_dw", y2, dz, BF16)
    dwin_s, got_glu = _mm_tn_slots("in_proj_dw", h1, dproj, N_DEV, BF16, side=_scatter_job([dglu]))
    grad_x, dg1, got_win = _mm_nt_slots("in_proj_dx", dproj, win_s, F32, side=_scatter_job([dwin_s]),
                                        epi=_epi_norm_bwd(x2d, g1, dx1))

    dab_re, dab_im = _s5_unz(dabar)
    dbb_re, dbb_im = _s5_block_diag_bb(dbbm)
    dc_re, dc_im = _s5_block_diag_cc(dccm)
    da_re, da_im, dlog_dt, db_re_t, db_im_t = _s5_discretize_bwd(*disc_args, dab_re, dab_im, to_gpn(dbb_re),
                                                                 to_gpn(dbb_im))
    db_re, db_im = from_gpn(db_re_t), from_gpn(db_im_t)
    small_g = dict(norm1_g=dg1, ret_gn_g=dgn, s5_d=dd, s5_glu_b=dgb, norm2_g=dg2, norm_mem_g=dgm, norm_f_g=dgf,
                   s5_a_re=da_re, s5_a_im=da_im, s5_log_dt=dlog_dt, s5_b_re=db_re, s5_b_im=db_im, s5_c_re=dc_re,
                   s5_c_im=dc_im)
    small_pack = _pack_small(small_g, loss_row=loss_lanes)

    res = {}
    got = dict(zip(("w_out", "xa_wq", "xa_wk", "xa_wv", "xa_wo"), got_a), w_in=got_win, s5_glu_w=got_glu)
    for n in ("w_in",) + _ROW_NAMES:
        res[n] = _adamw("adamw_" + n, got[n], w[n][0], mom[n][0], var[n][0])
    small_sum = _allreduce_small(small_pack)
    small_res, loss_sum = _adamw_small(small_sum, _pack_small(w), _pack_small(mom), _pack_small(var))
    loss = (0.5 / D_MODEL) * jnp.sum(loss_sum)
    res.update(small_res)

    outs = [loss, grad_x[None]]
    for part in range(4):
        for n in _W_NAMES:
            outs.append(res[n][part].reshape(shapes[n]))
    return tuple(outs)
```

```python
import jax
import jax.numpy as jnp
from jax import lax
from jax.experimental import pallas as pl
from jax.experimental.pallas import tpu as pltpu

F32 = jnp.float32
BF16 = jnp.bfloat16
MESH = pl.DeviceIdType.MESH

D_MODEL = 1024
RET_HEADS, RET_DK, RET_DV = 8, 64, 128
RET_QK = RET_HEADS * RET_DK
S5_G, S5_N, S5_P = 64, 64, 16
S5_NB = 8
S5_GB = S5_G // S5_NB
S5_BS = S5_GB * S5_N
S5_COLS = 2 * S5_G * S5_N
XA_HEADS, XA_DH = 4, 256
EPS = 1e-6
ROPE_BASE = 10000.0
N_DEV = 8
W_IN_SHARD = 640
ROW_SHARDS = (128, 256, 128, 128, 128, 128)
ROWPACK = sum(ROW_SHARDS)
SMALL_ROWS = 320
ADAM_LR, ADAM_B1, ADAM_B2, ADAM_EPS, ADAM_WD, ADAM_STEP = 0.001, 0.9, 0.999, 1e-08, 0.01, 10

VMEM_LIMIT = 56 * 1024 * 1024


def _cp(*sem):
    return pltpu.CompilerParams(dimension_semantics=tuple(sem), vmem_limit_bytes=VMEM_LIMIT)


def _dot(a, b):
    return jnp.dot(a, b, preferred_element_type=F32)


def _dot_nt(a, b):
    return lax.dot_general(a, b, (((1,), (1,)), ((), ())), preferred_element_type=F32)


def _dot_tn(a, b):
    return lax.dot_general(a, b, (((0,), (0,)), ((), ())), preferred_element_type=F32)


def _sigmoid(x):
    return 1.0 / (1.0 + jnp.exp(-x))


def _silu(x):
    return x * _sigmoid(x)


def _dsilu(x):
    s = _sigmoid(x)
    return s * (1.0 + x * (1.0 - s))


_GELU_C = 0.7978845608028654


def _gelu(x):
    return 0.5 * x * (1.0 + jnp.tanh(_GELU_C * (x + 0.044715 * (x * x * x))))


def _gelu_and_grad(x):
    t = jnp.tanh(_GELU_C * (x + 0.044715 * (x * x * x)))
    half = 0.5 * (1.0 + t)
    return x * half, half + 0.5 * x * (1.0 - t * t) * (_GELU_C * (1.0 + 3.0 * 0.044715 * (x * x)))


def _pick(n, cands):
    for c in cands:
        if n % c == 0:
            return c
    return n


class _Epilogue:
    def __init__(self, rows, vecs, row_out_dtypes, n_sums, fn):
        self.rows, self.vecs, self.row_out_dtypes, self.n_sums, self.fn = list(rows), list(vecs), list(row_out_dtypes), n_sums, fn


def _rms(x):
    rs = lax.rsqrt(jnp.mean(x * x, axis=-1, keepdims=True) + EPS)
    return rs, x * rs


def _rms_dx(dn, xn, rs):
    return rs * (dn - xn * jnp.mean(dn * xn, axis=-1, keepdims=True))


def _epi_norm_fwd(g):
    def fn(r, rows, vecs):
        return r, [_rms(r)[1] * vecs[0]], []

    return _Epilogue([], [g], [BF16], 0, fn)


def _epi_loss(gf, target):
    def fn(r, rows, vecs):
        rs, xn = _rms(r)
        e = xn * vecs[0] - rows[0]
        dy = e * (1.0 / r.shape[-1])
        return (_rms_dx(dy * vecs[0], xn, rs), [],
                [jnp.sum(dy * xn, axis=0, keepdims=True), jnp.sum(e * e, axis=0, keepdims=True)])

    return _Epilogue([target], [gf], [], 2, fn)


def _epi_norm_bwd(x, g, dres):
    def fn(r, rows, vecs):
        rs, xn = _rms(rows[0])
        return _rms_dx(r * vecs[0], xn, rs) + rows[1], [], [jnp.sum(r * xn, axis=0, keepdims=True)]

    return _Epilogue([x, dres], [g], [], 1, fn)


def _mm_core(name, operands, in_specs, out_spec, out_shape, grid, nk, dims, acc_shape, has_res, side=None, epi=None):
    n_in = 3 if has_res else 2
    n_epi_in = len(epi.rows) + len(epi.vecs) if epi else 0
    n_epi_out = len(epi.row_out_dtypes) + epi.n_sums if epi else 0
    n_side_in = len(side.srcs) if side else 0
    n_side_out = side.n if side else 0

    def body(*refs):
        a_ref, b_ref = refs[0], refs[1]
        r_ref = refs[2] if has_res else None
        epi_in = refs[n_in:n_in + n_epi_in]
        side_in = refs[n_in + n_epi_in:n_in + n_epi_in + n_side_in]
        n0 = n_in + n_epi_in + n_side_in
        o_ref = refs[n0]
        epi_out = refs[n0 + 1:n0 + 1 + n_epi_out]
        side_out = refs[n0 + 1 + n_epi_out:n0 + 1 + n_epi_out + n_side_out]
        rest = refs[n0 + 1 + n_epi_out + n_side_out:]
        acc, sems = (rest[0], rest[1:]) if nk > 1 else (None, rest)
        i, j, k = pl.program_id(0), pl.program_id(1), pl.program_id(2)
        if side:
            @pl.when((i == 0) & (j == 0) & (k == 0))
            def _():
                side.start(side_in, side_out, sems)

        def product():
            if len(b_ref.shape) == 3:
                ns = b_ref.shape[2]
                return sum(lax.dot_general(a_ref[:, p * ns:(p + 1) * ns].astype(BF16), b_ref[p].astype(BF16),
                                           (dims, ((), ())), preferred_element_type=F32)
                           for p in range(b_ref.shape[0]))
            return lax.dot_general(a_ref[...].astype(BF16), b_ref[...].astype(BF16), (dims, ((), ())),
                                   preferred_element_type=F32)

        def finish(r):
            if has_res:
                r = r + r_ref[...]
            if epi is None:
                o_ref[...] = r.astype(o_ref.dtype)
                return
            n_rows = len(epi.rows)
            main, row_vals, sums = epi.fn(r, [t[...] for t in epi_in[:n_rows]], [t[...] for t in epi_in[n_rows:]])
            o_ref[...] = main.astype(o_ref.dtype)
            for ref, val in zip(epi_out, row_vals):
                ref[...] = val.astype(ref.dtype)
            for ref, val in zip(epi_out[len(row_vals):], sums):
                @pl.when(i == 0)
                def _(ref=ref):
                    ref[...] = jnp.zeros_like(ref)

                ref[...] += val

        if nk == 1:
            finish(product())
        else:
            @pl.when(k == 0)
            def _():
                acc[...] = jnp.zeros_like(acc)

            acc[...] += product()

            @pl.when(k == nk - 1)
            def _():
                finish(acc[...])

        if side:
            @pl.when((i == grid[0] - 1) & (j == grid[1] - 1) & (k == grid[2] - 1))
            def _():
                side.wait(side_in, side_out, sems)

    acc_scratch = [pltpu.VMEM(acc_shape, F32)] if nk > 1 else []
    in_specs, out_specs, out_shapes, operands = list(in_specs), [out_spec], [out_shape], list(operands)
    if epi:
        assert grid[1] == 1, "an epilogue needs tiles that span whole rows"
        tm, n = out_spec.block_shape
        row_spec = pl.BlockSpec((tm, n), lambda i, j, k: (i, 0))
        vec_spec = pl.BlockSpec((1, n), lambda i, j, k: (0, 0))
        in_specs += [row_spec] * len(epi.rows) + [vec_spec] * len(epi.vecs)
        operands += epi.rows + epi.vecs
        out_specs += [row_spec] * len(epi.row_out_dtypes) + [vec_spec] * epi.n_sums
        out_shapes += [jax.ShapeDtypeStruct(out_shape.shape, d) for d in epi.row_out_dtypes]
        out_shapes += [jax.ShapeDtypeStruct((1, n), F32)] * epi.n_sums
    scratch = acc_scratch
    if side:
        in_specs += side.in_specs
        operands += side.srcs
        out_specs += side.out_specs
        out_shapes += side.landing
        scratch = acc_scratch + side.scratch
    plain = side is None and epi is None
    res = pl.pallas_call(
        body, name=name, grid=grid, in_specs=in_specs, out_specs=out_specs[0] if plain else tuple(out_specs),
        out_shape=out_shapes[0] if plain else tuple(out_shapes), scratch_shapes=scratch,
        compiler_params=_cp("parallel", "parallel", "arbitrary") if plain else _cp("arbitrary", "arbitrary", "arbitrary"),
    )(*operands)
    return res


def _mm_nn(name, a, b, out_dtype, residual=None, epi=None):
    m, kk = a.shape
    n = b.shape[1]
    tm, tn, tk = _pick(m, (1024, 512, 256)), _pick(n, (1024, 512)), _pick(kk, (2048, 1024, 512))
    ops = [a, b]
    specs = [pl.BlockSpec((tm, tk), lambda i, j, k: (i, k)), pl.BlockSpec((tk, tn), lambda i, j, k: (k, j))]
    if residual is not None:
        ops.append(residual)
        specs.append(pl.BlockSpec((tm, tn), lambda i, j, k: (i, j)))
    return _mm_core(name, ops, specs, pl.BlockSpec((tm, tn), lambda i, j, k: (i, j)),
                    jax.ShapeDtypeStruct((m, n), out_dtype), (m // tm, n // tn, kk // tk), kk // tk,
                    ((1,), (0,)), (tm, tn), residual is not None, epi=epi)


def _mm_nt(name, a, b, out_dtype, residual=None, epi=None):
    m, kk = a.shape
    n = b.shape[0]
    tm, tn, tk = _pick(m, (1024, 512, 256)), _pick(n, (2048, 1024, 512)), _pick(kk, (2048, 1024, 512))
    ops = [a, b]
    specs = [pl.BlockSpec((tm, tk), lambda i, j, k: (i, k)), pl.BlockSpec((tn, tk), lambda i, j, k: (j, k))]
    if residual is not None:
        ops.append(residual)
        specs.append(pl.BlockSpec((tm, tn), lambda i, j, k: (i, j)))
    return _mm_core(name, ops, specs, pl.BlockSpec((tm, tn), lambda i, j, k: (i, j)),
                    jax.ShapeDtypeStruct((m, n), out_dtype), (m // tm, n // tn, kk // tk), kk // tk,
                    ((1,), (1,)), (tm, tn), residual is not None, epi=epi)


def _mm_tn(name, a, b, out_dtype):
    kk, m = a.shape
    n = b.shape[1]
    tm, tn, tk = _pick(m, (1024, 512)), _pick(n, (1024, 512)), _pick(kk, (2048, 1024, 512, 256))
    specs = [pl.BlockSpec((tk, tm), lambda i, j, k: (k, i)), pl.BlockSpec((tk, tn), lambda i, j, k: (k, j))]
    return _mm_core(name, [a, b], specs, pl.BlockSpec((tm, tn), lambda i, j, k: (i, j)),
                    jax.ShapeDtypeStruct((m, n), out_dtype), (m // tm, n // tn, kk // tk), kk // tk,
                    ((0,), (0,)), (tm, tn), False)


def _mm_nn_slots(name, a, b_slots, out_dtype, side=None):
    m, kk = a.shape
    s, _, ns = b_slots.shape
    tm, tk = _pick(m, (4096, 2048, 1024, 512, 256)), _pick(kk, (1024, 512))
    specs = [pl.BlockSpec((tm, tk), lambda i, j, k: (i, k)), pl.BlockSpec((None, tk, ns), lambda i, j, k: (j, k, 0))]
    return _mm_core(name, [a, b_slots], specs, pl.BlockSpec((tm, ns), lambda i, j, k: (i, j)),
                    jax.ShapeDtypeStruct((m, s * ns), out_dtype), (m // tm, s, kk // tk), kk // tk,
                    ((1,), (0,)), (tm, ns), False, side)


def _mm_nt_slots(name, a, b_slots, out_dtype, side=None, epi=None):
    m = a.shape[0]
    s, n, ns = b_slots.shape
    tm, tn = _pick(m, (1024, 512, 256)), _pick(n, (1024, 512))
    per = _pick(s, (2, 1))
    specs = [pl.BlockSpec((tm, per * ns), lambda i, j, k: (i, k)),
             pl.BlockSpec((per, tn, ns), lambda i, j, k: (k, j, 0))]
    return _mm_core(name, [a, b_slots], specs, pl.BlockSpec((tm, tn), lambda i, j, k: (i, j)),
                    jax.ShapeDtypeStruct((m, n), out_dtype), (m // tm, n // tn, s // per), s // per,
                    ((1,), (1,)), (tm, tn), False, side, epi)


def _mm_tn_slots(name, a, b, s, out_dtype, side=None):
    kk, m = a.shape
    ns = b.shape[1] // s
    tm, tk = _pick(m, (1024, 512)), _pick(kk, (4096, 2048, 1024, 512, 256))
    specs = [pl.BlockSpec((tk, tm), lambda i, j, k: (k, i)), pl.BlockSpec((tk, ns), lambda i, j, k: (k, j))]
    return _mm_core(name, [a, b], specs, pl.BlockSpec((None, tm, ns), lambda i, j, k: (j, i, 0)),
                    jax.ShapeDtypeStruct((s, m, ns), out_dtype), (m // tm, s, kk // tk), kk // tk,
                    ((0,), (0,)), (tm, ns), False, side)


def _rms_fwd(name, x, g):
    r, d = x.shape
    tr = _pick(r, (1024, 512, 256))

    def body(x_ref, g_ref, o_ref):
        xv = x_ref[...]
        rs = lax.rsqrt(jnp.mean(xv * xv, axis=-1, keepdims=True) + EPS)
        o_ref[...] = (xv * rs * g_ref[...]).astype(o_ref.dtype)

    return pl.pallas_call(
        body, name=name, grid=(r // tr,),
        in_specs=[pl.BlockSpec((tr, d), lambda i: (i, 0)), pl.BlockSpec((1, d), lambda i: (0, 0))],
        out_specs=pl.BlockSpec((tr, d), lambda i: (i, 0)),
        out_shape=jax.ShapeDtypeStruct((r, d), BF16), compiler_params=_cp("parallel"),
    )(x, g)


def _rms_bwd(name, x, g, dh, dres):
    r, d = x.shape
    tr = _pick(r, (512, 256))
    has_res = dres is not None

    def body(*refs):
        if has_res:
            x_ref, g_ref, dh_ref, dr_ref, dx_ref, dg_ref = refs
        else:
            x_ref, g_ref, dh_ref, dx_ref, dg_ref = refs
        i = pl.program_id(0)

        @pl.when(i == 0)
        def _():
            dg_ref[...] = jnp.zeros_like(dg_ref)

        xv = x_ref[...]
        dhv = dh_ref[...].astype(F32)
        rs = lax.rsqrt(jnp.mean(xv * xv, axis=-1, keepdims=True) + EPS)
        xn = xv * rs
        dg_ref[...] += jnp.sum(dhv * xn, axis=0, keepdims=True)
        dn = dhv * g_ref[...]
        dx = rs * (dn - xn * jnp.mean(dn * xn, axis=-1, keepdims=True))
        if has_res:
            dx = dx + dr_ref[...]
        dx_ref[...] = dx

    row = pl.BlockSpec((tr, d), lambda i: (i, 0))
    vec = pl.BlockSpec((1, d), lambda i: (0, 0))
    ops = [x, g, dh] + ([dres] if has_res else [])
    return pl.pallas_call(
        body, name=name, grid=(r // tr,),
        in_specs=[row, vec, row] + ([row] if has_res else []),
        out_specs=(row, vec),
        out_shape=(jax.ShapeDtypeStruct((r, d), F32), jax.ShapeDtypeStruct((1, d), F32)),
        compiler_params=_cp("arbitrary"),
    )(*ops)


def _rot(x, cos_t, sin_t):
    n = x.shape[-1]
    lane = lax.broadcasted_iota(jnp.int32, x.shape, 1)
    partner = jnp.where((lane % RET_DK) < RET_DK // 2, pltpu.roll(x, n - RET_DK // 2, 1), pltpu.roll(x, RET_DK // 2, 1))
    return x * cos_t + partner * sin_t


def _ret_constants(c):
    log_g = jnp.log1p(-jnp.exp2(-5.0 - jnp.arange(RET_HEADS, dtype=F32)))
    j = jnp.arange(c, dtype=F32)
    diff = j[:, None] - j[None, :]
    decay = jnp.where(diff[None] >= 0.0, jnp.exp(log_g[:, None, None] * jnp.maximum(diff, 0.0)[None]), 0.0)
    q_w = jnp.exp(log_g[None, :] * (j + 1.0)[:, None])
    k_w = jnp.exp(log_g[None, :] * (c - 1.0 - j)[:, None])
    cd = jnp.exp(log_g * c)
    rep = lambda t: jnp.repeat(t, RET_DK, axis=1)
    cd_row = jnp.repeat(cd, RET_DV)[None, :]
    return decay, rep(q_w), rep(k_w), cd_row


def _pair_of(h, c):
    lane = lax.broadcasted_iota(jnp.int32, (c, 2 * RET_DK), 1)
    mine = (lane < RET_DK) if h % 2 == 0 else (lane >= RET_DK)
    return slice((h // 2) * 2 * RET_DK, (h // 2 + 1) * 2 * RET_DK), mine


def _keep(x, mine):
    return jnp.where(mine, x, jnp.zeros_like(x))


def _ret_fwd(proj, cos_t, sin_t, consts, gn_g, c):
    l = proj.shape[0]
    nc = l // c
    decay, qw, kw, cd_row = consts

    def body(q_ref, k_ref, v_ref, g_ref, cos_ref, sin_ref, dec_ref, qw_ref, kw_ref, cd_ref, gn_ref,
             ret_ref, o_ref, rp_ref, qb_ref, kb_ref, state):
        @pl.when(pl.program_id(0) == 0)
        def _():
            state[...] = jnp.zeros_like(state)

        cs, sn = cos_ref[...], sin_ref[...]
        qr = _rot(q_ref[...].astype(F32), cs, sn)
        kr = _rot(k_ref[...].astype(F32), cs, sn) * (RET_DK ** -0.5)
        qb, kb = qr.astype(BF16), kr.astype(BF16)
        qb_ref[...] = qb
        kb_ref[...] = kb
        qwb = (qr * qw_ref[...]).astype(BF16)
        kwb = (kr * kw_ref[...]).astype(BF16)
        vb = v_ref[...].astype(BF16)
        for h in range(RET_HEADS):
            ps, mine = _pair_of(h, c)
            vs = slice(h * RET_DV, (h + 1) * RET_DV)
            s = _dot_nt(_keep(qb[:, ps], mine), kb[:, ps]) * dec_ref[h]
            r_prev = state[h]
            rp_ref[0, h] = r_prev
            o = _dot(s.astype(BF16), vb[:, vs]) + _dot(_keep(qwb[:, ps], mine), r_prev.astype(BF16))
            state[h] = cd_ref[:, vs] * r_prev + _dot_tn(_keep(kwb[:, ps], mine), vb[:, vs])
            o_ref[:, vs] = o
            mu = jnp.mean(o, axis=-1, keepdims=True)
            var = jnp.mean(jnp.square(o - mu), axis=-1, keepdims=True)
            on = (o - mu) * lax.rsqrt(var + EPS)
            ret_ref[:, vs] = (on * gn_ref[:, vs] * _silu(g_ref[:, vs].astype(F32))).astype(ret_ref.dtype)

    const2 = lambda shape: pl.BlockSpec(shape, lambda i: (0,) * len(shape))
    return pl.pallas_call(
        body, name="retention_fwd", grid=(nc,),
        in_specs=[pl.BlockSpec((c, RET_QK), lambda i: (i, 0)), pl.BlockSpec((c, RET_QK), lambda i: (i, 1)),
                  pl.BlockSpec((c, D_MODEL), lambda i: (i, 1)), pl.BlockSpec((c, D_MODEL), lambda i: (i, 2)),
                  pl.BlockSpec((c, RET_QK), lambda i: (i, 0)), pl.BlockSpec((c, RET_QK), lambda i: (i, 0)),
                  const2((RET_HEADS, c, c)), const2((c, RET_QK)), const2((c, RET_QK)), const2((1, D_MODEL)),
                  const2((1, D_MODEL))],
        out_specs=(pl.BlockSpec((c, D_MODEL), lambda i: (i, 0)), pl.BlockSpec((c, D_MODEL), lambda i: (i, 0)),
                   pl.BlockSpec((1, RET_HEADS, 2 * RET_DK, RET_DV), lambda i: (i, 0, 0, 0)),
                   pl.BlockSpec((c, RET_QK), lambda i: (i, 0)), pl.BlockSpec((c, RET_QK), lambda i: (i, 0))),
        out_shape=(jax.ShapeDtypeStruct((l, 2 * D_MODEL), BF16), jax.ShapeDtypeStruct((l, D_MODEL), F32),
                   jax.ShapeDtypeStruct((nc, RET_HEADS, 2 * RET_DK, RET_DV), F32),
                   jax.ShapeDtypeStruct((l, RET_QK), BF16), jax.ShapeDtypeStruct((l, RET_QK), BF16)),
        scratch_shapes=[pltpu.VMEM((RET_HEADS, 2 * RET_DK, RET_DV), F32)],
        compiler_params=_cp("arbitrary"),
    )(proj, proj, proj, proj, cos_t, sin_t, decay, qw, kw, cd_row, gn_g)


def _ret_bwd(proj, qb_saved, kb_saved, cos_t, sin_t, consts, gn_g, o_saved, r_prev_saved, dmix, c, side):
    l = proj.shape[0]
    nc = l // c
    decay, qw, kw, cd_row = consts
    n_in = 14

    def body(*refs):
        (q_ref, k_ref, v_ref, g_ref, cos_ref, sin_ref, dec_ref, qw_ref, kw_ref, cd_ref, gn_ref, o_ref, rp_ref,
         dr_ref) = refs[:n_in]
        side_in = refs[n_in:n_in + len(side.srcs)]
        out_ref, dgn_ref = refs[n_in + len(side.srcs):n_in + len(side.srcs) + 2]
        side_out = refs[n_in + len(side.srcs) + 2:n_in + len(side.srcs) + 2 + side.n]
        state, dq_s, dk_s = refs[n_in + len(side.srcs) + 2 + side.n:n_in + len(side.srcs) + 5 + side.n]
        sems = refs[n_in + len(side.srcs) + 5 + side.n:]

        @pl.when(pl.program_id(0) == 0)
        def _():
            side.start(side_in, side_out, sems)
            state[...] = jnp.zeros_like(state)
            dgn_ref[...] = jnp.zeros_like(dgn_ref)

        cs, sn = cos_ref[...], sin_ref[...]
        qb, kb = q_ref[...], k_ref[...]
        qwv, kwv = qw_ref[...], kw_ref[...]
        qwb = (qb.astype(F32) * qwv).astype(BF16)
        kwb = (kb.astype(F32) * kwv).astype(BF16)
        vb = v_ref[...].astype(BF16)
        dq2 = dk2 = None
        for h in range(RET_HEADS):
            ps, mine = _pair_of(h, c)
            vs = slice(h * RET_DV, (h + 1) * RET_DV)
            dec = dec_ref[h]
            qm, km = _keep(qb[:, ps], mine), _keep(kb[:, ps], mine)
            o = o_ref[:, vs]
            mu = jnp.mean(o, axis=-1, keepdims=True)
            var = jnp.mean(jnp.square(o - mu), axis=-1, keepdims=True)
            rstd = lax.rsqrt(var + EPS)
            on = (o - mu) * rstd
            gate = g_ref[:, vs].astype(F32)
            sg = _silu(gate)
            dret = dr_ref[:, vs].astype(F32)
            gn = gn_ref[:, vs]
            dgn_ref[:, vs] += jnp.sum(dret * on * sg, axis=0, keepdims=True)
            out_ref[:, 2 * RET_QK + D_MODEL + h * RET_DV:2 * RET_QK + D_MODEL + (h + 1) * RET_DV] = (
                dret * on * gn * _dsilu(gate)).astype(out_ref.dtype)
            don = dret * gn * sg
            do = rstd * (don - jnp.mean(don, axis=-1, keepdims=True)
                         - on * jnp.mean(don * on, axis=-1, keepdims=True))
            dob = do.astype(BF16)
            sn_h = state[h]
            snb = sn_h.astype(BF16)
            s = _dot_nt(qm, kb[:, ps]) * dec
            dv = _dot_tn(s.astype(BF16), dob) + _dot(_keep(kwb[:, ps], mine), snb)
            out_ref[:, 2 * RET_QK + h * RET_DV:2 * RET_QK + (h + 1) * RET_DV] = dv.astype(out_ref.dtype)
            ds = (_dot_nt(dob, vb[:, vs]) * dec).astype(BF16)
            dq_h = _dot(ds, km) + qwv[:, ps] * _dot_nt(dob, rp_ref[0, h].astype(BF16))
            dk_h = _dot_tn(ds, qm) + kwv[:, ps] * _dot_nt(vb[:, vs], snb)
            state[h] = cd_ref[:, vs] * sn_h + _dot_tn(_keep(qwb[:, ps], mine), dob)
            if h % 2 == 0:
                dq2, dk2 = dq_h, dk_h
            else:
                dq_s[:, ps] = dq2 + dq_h
                dk_s[:, ps] = dk2 + dk_h
        out_ref[:, 0:RET_QK] = _rot(dq_s[...], cs, -sn).astype(out_ref.dtype)
        out_ref[:, RET_QK:2 * RET_QK] = (_rot(dk_s[...], cs, -sn) * (RET_DK ** -0.5)).astype(out_ref.dtype)

        @pl.when(pl.program_id(0) == nc - 1)
        def _():
            side.wait(side_in, side_out, sems)

    rev = lambda i: nc - 1 - i
    const2 = lambda shape: pl.BlockSpec(shape, lambda i: (0,) * len(shape))
    return pl.pallas_call(
        body, name="retention_bwd", grid=(nc,),
        in_specs=[pl.BlockSpec((c, RET_QK), lambda i: (rev(i), 0)), pl.BlockSpec((c, RET_QK), lambda i: (rev(i), 0)),
                  pl.BlockSpec((c, D_MODEL), lambda i: (rev(i), 1)), pl.BlockSpec((c, D_MODEL), lambda i: (rev(i), 2)),
                  pl.BlockSpec((c, RET_QK), lambda i: (rev(i), 0)), pl.BlockSpec((c, RET_QK), lambda i: (rev(i), 0)),
                  const2((RET_HEADS, c, c)), const2((c, RET_QK)), const2((c, RET_QK)), const2((1, D_MODEL)),
                  const2((1, D_MODEL)),
                  pl.BlockSpec((c, D_MODEL), lambda i: (rev(i), 0)),
                  pl.BlockSpec((1, RET_HEADS, 2 * RET_DK, RET_DV), lambda i: (rev(i), 0, 0, 0)),
                  pl.BlockSpec((c, D_MODEL), lambda i: (rev(i), 0))] + side.in_specs,
        out_specs=(pl.BlockSpec((c, 2 * RET_QK + 2 * D_MODEL), lambda i: (rev(i), 0)), const2((1, D_MODEL)),
                   *side.out_specs),
        out_shape=(jax.ShapeDtypeStruct((l, 2 * RET_QK + 4 * D_MODEL), BF16), jax.ShapeDtypeStruct((1, D_MODEL), F32),
                   *side.landing),
        scratch_shapes=[pltpu.VMEM((RET_HEADS, 2 * RET_DK, RET_DV), F32), pltpu.VMEM((c, RET_QK), F32),
                        pltpu.VMEM((c, RET_QK), F32)] + side.scratch,
        compiler_params=_cp("arbitrary"),
    )(qb_saved, kb_saved, proj, proj, cos_t, sin_t, decay, qw, kw, cd_row, gn_g, o_saved, r_prev_saved, dmix,
      *side.srcs)


def _zoh(a_re, a_im, log_dt):
    dt = jnp.exp(log_dt)
    mag = jnp.exp(a_re * dt)
    abar_re = mag * jnp.cos(a_im * dt)
    abar_im = mag * jnp.sin(a_im * dt)
    den = a_re * a_re + a_im * a_im
    nr, ni = abar_re - 1.0, abar_im
    f_re = (nr * a_re + ni * a_im) / den
    f_im = (ni * a_re - nr * a_im) / den
    return dt, abar_re, abar_im, f_re, f_im, den


def _lanes_p(f):
    return jnp.tile(f, (1, S5_P))


def _s5_discretize(a_re, a_im, log_dt, b_re_t, b_im_t):
    def body(ar_ref, ai_ref, ld_ref, br_ref, bi_ref, abr_ref, abi_ref, bbr_ref, bbi_ref):
        _, abar_re, abar_im, f_re, f_im, _ = _zoh(ar_ref[...], ai_ref[...], ld_ref[...])
        abr_ref[...] = abar_re
        abi_ref[...] = abar_im
        fr, fi = _lanes_p(f_re), _lanes_p(f_im)
        bbr_ref[...] = fr * br_ref[...] - fi * bi_ref[...]
        bbi_ref[...] = fr * bi_ref[...] + fi * br_ref[...]

    gn = jax.ShapeDtypeStruct((S5_G, S5_N), F32)
    gpn = jax.ShapeDtypeStruct((S5_G, S5_P * S5_N), F32)
    return pl.pallas_call(body, name="s5_discretize", out_shape=(gn, gn, gpn, gpn))(a_re, a_im, log_dt, b_re_t, b_im_t)


def _s5_discretize_bwd(a_re, a_im, log_dt, b_re_t, b_im_t, dab_re, dab_im, dbb_re_t, dbb_im_t):
    def body(ar_ref, ai_ref, ld_ref, br_ref, bi_ref, gar_ref, gai_ref, gbr_ref, gbi_ref,
             dar_ref, dai_ref, dld_ref, dbr_ref, dbi_ref):
        a_r, a_i = ar_ref[...], ai_ref[...]
        dt, abar_re, abar_im, f_re, f_im, den = _zoh(a_r, a_i, ld_ref[...])
        b_r, b_i, g_br, g_bi = br_ref[...], bi_ref[...], gbr_ref[...], gbi_ref[...]
        fr, fi = _lanes_p(f_re), _lanes_p(f_im)
        dbr_ref[...] = fr * g_br + fi * g_bi
        dbi_ref[...] = fr * g_bi - fi * g_br
        t_r = b_r * g_br + b_i * g_bi
        t_i = b_r * g_bi - b_i * g_br
        gf_r = sum(t_r[:, p * S5_N:(p + 1) * S5_N] for p in range(S5_P))
        gf_i = sum(t_i[:, p * S5_N:(p + 1) * S5_N] for p in range(S5_P))
        inv_r, inv_i = a_r / den, a_i / den
        ga_r = gar_ref[...] + gf_r * inv_r - gf_i * inv_i
        ga_i = gai_ref[...] + gf_r * inv_i + gf_i * inv_r
        q_r = -(f_re * a_r + f_im * a_i) / den
        q_i = -(f_im * a_r - f_re * a_i) / den
        gl_r = q_r * gf_r + q_i * gf_i
        gl_i = q_r * gf_i - q_i * gf_r
        dar_ref[...] = gl_r + dt * (abar_re * ga_r + abar_im * ga_i)
        dai_ref[...] = gl_i + dt * (abar_re * ga_i - abar_im * ga_r)
        la_r = a_r * abar_re - a_i * abar_im
        la_i = a_r * abar_im + a_i * abar_re
        dld_ref[...] = dt * jnp.sum(ga_r * la_r + ga_i * la_i, axis=-1, keepdims=True)

    gn = jax.ShapeDtypeStruct((S5_G, S5_N), F32)
    gpn = jax.ShapeDtypeStruct((S5_G, S5_P * S5_N), F32)
    return pl.pallas_call(
        body, name="s5_discretize_bwd", out_shape=(gn, gn, jax.ShapeDtypeStruct((S5_G, 1), F32), gpn, gpn),
    )(a_re, a_im, log_dt, b_re_t, b_im_t, dab_re, dab_im, dbb_re_t, dbb_im_t)


S5_ZQ = S5_NB // 2


def _s5_z(re, im):
    return jnp.concatenate([re.reshape(S5_ZQ, 8, 128), im.reshape(S5_ZQ, 8, 128)], axis=0)


def _s5_unz(z):
    return z[:S5_ZQ].reshape(S5_G, S5_N), z[S5_ZQ:].reshape(S5_G, S5_N)


def _s5_block_mats(bb_re, bb_im, c_re, c_im):
    eye = jnp.eye(S5_GB, dtype=F32)
    bb = jnp.stack([bb_re, bb_im], axis=0).reshape(2, S5_NB, S5_GB, S5_N, S5_P)
    bbm = jnp.einsum("rbgnp,gh->bgprhn", bb, eye).reshape(S5_NB, S5_GB * S5_P, 2 * S5_BS)
    cc = jnp.stack([c_re, -c_im], axis=0).reshape(2, S5_NB, S5_GB, S5_P, S5_N)
    ccm = jnp.einsum("rbgpn,gh->brhngp", cc, eye).reshape(S5_NB, 2 * S5_BS, S5_GB * S5_P)
    return bbm.astype(BF16), ccm.astype(BF16)


def _s5_block_diag_bb(m):
    t = m.reshape(S5_NB, S5_GB, S5_P, 2, S5_GB, S5_N)
    d = jnp.einsum("bgprgn->rbgnp", t).reshape(2, S5_G, S5_N, S5_P)
    return d[0], d[1]


def _s5_block_diag_cc(m):
    t = m.reshape(S5_NB, 2, S5_GB, S5_N, S5_GB, S5_P)
    d = jnp.einsum("brgngp->rbgpn", t).reshape(2, S5_G, S5_P, S5_N)
    return d[0], -d[1]


SCAN_UNROLL = 8


def _z_store(zr, zi, blk, res, t, off):
    q, h = blk // 2, blk % 2
    for lt in range(4):
        zr[q, pl.ds(off + 4 * h + lt, t, stride=8), :] = res[:, lt * 128:(lt + 1) * 128]
        zi[q, pl.ds(off + 4 * h + lt, t, stride=8), :] = res[:, S5_BS + lt * 128:S5_BS + (lt + 1) * 128]


def _z_load(zr, zi, blk, t, off):
    q, h = blk // 2, blk % 2
    return jnp.concatenate([zr[q, pl.ds(off + 4 * h + lt, t, stride=8), :] for lt in range(4)]
                           + [zi[q, pl.ds(off + 4 * h + lt, t, stride=8), :] for lt in range(4)], axis=1)


def _z_scan_fwd(zr, zi, a_ref, carry_ref, t, off):
    ar = [a_ref[q] for q in range(S5_ZQ)]
    ai = [a_ref[S5_ZQ + q] for q in range(S5_ZQ)]

    def step(it, carry):
        carry = list(carry)
        base = pl.multiple_of(it * (8 * SCAN_UNROLL), 8 * SCAN_UNROLL) + off
        for tt in range(SCAN_UNROLL):
            rows = pl.ds(base + 8 * tt, 8)
            for q in range(S5_ZQ):
                c_r, c_i = carry[q], carry[S5_ZQ + q]
                n_r = ar[q] * c_r - ai[q] * c_i + zr[q, rows, :]
                n_i = ar[q] * c_i + ai[q] * c_r + zi[q, rows, :]
                zr[q, rows, :] = n_r
                zi[q, rows, :] = n_i
                carry[q], carry[S5_ZQ + q] = n_r, n_i
        return tuple(carry)

    out = lax.fori_loop(0, t // SCAN_UNROLL, step, tuple(carry_ref[k] for k in range(2 * S5_ZQ)))
    for k in range(2 * S5_ZQ):
        carry_ref[k] = out[k]


def _z_scan_bwd(lr, li, xr, xi, a_ref, carry_ref, acc_ref, t):
    ar = [a_ref[q] for q in range(S5_ZQ)]
    ai = [a_ref[S5_ZQ + q] for q in range(S5_ZQ)]
    n_it = t // SCAN_UNROLL

    def step(it, state):
        carry, acc = list(state[0]), list(state[1])
        base = pl.multiple_of((n_it - 1 - it) * (8 * SCAN_UNROLL), 8 * SCAN_UNROLL)
        for tt in reversed(range(SCAN_UNROLL)):
            rows = pl.ds(base + 8 * tt, 8)
            for q in range(S5_ZQ):
                c_r, c_i = carry[q], carry[S5_ZQ + q]
                n_r = ar[q] * c_r + ai[q] * c_i + lr[q, rows, :]
                n_i = ar[q] * c_i - ai[q] * c_r + li[q, rows, :]
                lr[q, rows, :] = n_r
                li[q, rows, :] = n_i
                p_r, p_i = xr[q, rows, :], xi[q, rows, :]
                acc[q] = acc[q] + n_r * p_r + n_i * p_i
                acc[S5_ZQ + q] = acc[S5_ZQ + q] + n_i * p_r - n_r * p_i
                carry[q], carry[S5_ZQ + q] = n_r, n_i
        return tuple(carry), tuple(acc)

    k8 = range(2 * S5_ZQ)
    carry, acc = lax.fori_loop(0, n_it, step, (tuple(carry_ref[k] for k in k8), tuple(acc_ref[k] for k in k8)))
    for k in k8:
        carry_ref[k] = carry[k]
        acc_ref[k] = acc[k]


def _s5_fwd(proj, mix, bbm, ccm, d_row, glu_w, glu_b, tabs, t, side):
    l = proj.shape[0]
    nt = l // t
    n_in = 9

    def body(*refs):
        u_ref, gs_ref, bb_ref, cc_ref, d_ref, gw_ref, gb_ref, a_ref, _ = refs[:n_in]
        side_in = refs[n_in:n_in + len(side.srcs)]
        ssm_ref, xst_ref, y1_ref, z_ref = refs[n_in + len(side.srcs):n_in + len(side.srcs) + 4]
        side_out = refs[n_in + len(side.srcs) + 4:n_in + len(side.srcs) + 4 + side.n]
        zr, zi, carry = refs[n_in + len(side.srcs) + 4 + side.n:n_in + len(side.srcs) + 7 + side.n]
        sems = refs[n_in + len(side.srcs) + 7 + side.n:]

        @pl.when(pl.program_id(0) == 0)
        def _():
            side.start(side_in, side_out, sems)
            carry[...] = jnp.zeros_like(carry)

        xst_ref[0] = carry[...]
        ub = u_ref[...]
        u = ub.astype(F32)
        for blk in range(S5_NB):
            _z_store(zr, zi, blk, _dot(ub[:, blk * 128:(blk + 1) * 128], bb_ref[blk]), t, 0)
        _z_scan_fwd(zr, zi, a_ref, carry, t, 0)
        ys = jnp.concatenate(
            [_dot(_z_load(zr, zi, blk, t, 0).astype(BF16), cc_ref[blk]) for blk in range(S5_NB)], axis=1)
        y1 = ys + d_ref[...] * u
        y1_ref[...] = y1.astype(y1_ref.dtype)
        y2 = _gelu(y1)
        z = _dot(y2.astype(BF16), gw_ref[...]) + gb_ref[...]
        z_ref[...] = z.astype(z_ref.dtype)
        ssm_ref[...] = (y2 * _sigmoid(z) * _silu(gs_ref[...].astype(F32))).astype(ssm_ref.dtype)

        @pl.when(pl.program_id(0) == nt - 1)
        def _():
            side.wait(side_in, side_out, sems)

    const2 = lambda shape: pl.BlockSpec(shape, lambda i: (0,) * len(shape))
    zshape = (2 * S5_ZQ, 8, 128)
    return pl.pallas_call(
        body, name="s5_fwd", grid=(nt,),
        in_specs=[pl.BlockSpec((t, D_MODEL), lambda i: (i, 3)), pl.BlockSpec((t, D_MODEL), lambda i: (i, 4)),
                  const2(bbm.shape), const2(ccm.shape), const2((1, D_MODEL)), const2((D_MODEL, D_MODEL)),
                  const2((1, D_MODEL)), const2(zshape), pl.BlockSpec(memory_space=pl.ANY)] + side.in_specs,
        out_specs=(pl.BlockSpec((t, D_MODEL), lambda i: (i, 1)), pl.BlockSpec((1,) + zshape, lambda i: (i, 0, 0, 0)),
                   pl.BlockSpec((t, D_MODEL), lambda i: (i, 0)), pl.BlockSpec((t, D_MODEL), lambda i: (i, 0)),
                   *side.out_specs),
        out_shape=(jax.ShapeDtypeStruct((l, 2 * D_MODEL), BF16), jax.ShapeDtypeStruct((nt,) + zshape, F32),
                   jax.ShapeDtypeStruct((l, D_MODEL), BF16), jax.ShapeDtypeStruct((l, D_MODEL), BF16), *side.landing),
        scratch_shapes=[pltpu.VMEM((S5_ZQ, 8 * t, 128), F32), pltpu.VMEM((S5_ZQ, 8 * t, 128), F32),
                        pltpu.VMEM(zshape, F32)] + side.scratch,
        input_output_aliases={8: 0},
        compiler_params=_cp("arbitrary"),
    )(proj, proj, bbm, ccm, d_row, glu_w, glu_b, tabs, mix, *side.srcs)


def _s5_bwd(proj, dmix, dproj, xstart, y1, z, bbm, ccm, d_row, glu_w, glu_b, tabs, t):
    l = proj.shape[0]
    nt = l // t
    col0 = 2 * RET_QK + 2 * D_MODEL

    def body(u_ref, gs_ref, dm_ref, xst_ref, bb_ref, cc_ref, d_ref, gw_ref, gb_ref, a_ref, _, y1_ref, z_ref,
             dp_ref, y2_ref, dz_ref, dbb_ref, dcc_ref, da_ref, dd_ref, dgb_ref, xr, xi, lr, li, carry, lcarry,
             dug_s, dug_sem):
        step = pl.program_id(0)
        slot = step % 2
        dug_ref = dug_s.at[slot]

        def put(s, at_step):
            rows = pl.ds(pl.multiple_of((nt - 1 - at_step) * t, t), t)
            return pltpu.make_async_copy(dug_s.at[s], dp_ref.at[rows, pl.ds(col0, 2 * D_MODEL)], dug_sem.at[s])

        @pl.when(step >= 2)
        def _():
            put(slot, step - 2).wait()

        @pl.when(step == 0)
        def _():
            lcarry[...] = jnp.zeros_like(lcarry)
            dbb_ref[...] = jnp.zeros_like(dbb_ref)
            dcc_ref[...] = jnp.zeros_like(dcc_ref)
            da_ref[...] = jnp.zeros_like(da_ref)
            dd_ref[...] = jnp.zeros_like(dd_ref)
            dgb_ref[...] = jnp.zeros_like(dgb_ref)

        carry[...] = xst_ref[0]
        for q in range(S5_ZQ):
            xr[q, 0:8, :] = carry[q]
            xi[q, 0:8, :] = carry[S5_ZQ + q]
        ub = u_ref[...]
        u = ub.astype(F32)
        for blk in range(S5_NB):
            _z_store(xr, xi, blk, _dot(ub[:, blk * 128:(blk + 1) * 128], bb_ref[blk]), t, 8)
        _z_scan_fwd(xr, xi, a_ref, carry, t, 8)
        dv = d_ref[...]
        y2, dgelu = _gelu_and_grad(y1_ref[...].astype(F32))
        y2b = y2.astype(BF16)
        sg = _sigmoid(z_ref[...].astype(F32))
        gs = gs_ref[...].astype(F32)
        dssm = dm_ref[...].astype(F32)
        dug_ref[:, D_MODEL:] = (dssm * (y2 * sg) * _dsilu(gs)).astype(dug_ref.dtype)
        dy3 = dssm * _silu(gs)
        dz = dy3 * y2 * sg * (1.0 - sg)
        dzb = dz.astype(BF16)
        y2_ref[...] = y2b
        dz_ref[...] = dzb
        dgb_ref[...] += jnp.sum(dz, axis=0, keepdims=True)
        dy1 = (dy3 * sg + _dot_nt(dzb, gw_ref[...])) * dgelu
        dd_ref[...] += jnp.sum(dy1 * u, axis=0, keepdims=True)
        dyb = dy1.astype(BF16)
        for blk in range(S5_NB):
            ch = slice(blk * 128, (blk + 1) * 128)
            _z_store(lr, li, blk, _dot_nt(dyb[:, ch], cc_ref[blk]), t, 0)
            dcc_ref[blk] += _dot_tn(_z_load(xr, xi, blk, t, 8).astype(BF16), dyb[:, ch])
        _z_scan_bwd(lr, li, xr, xi, a_ref, lcarry, da_ref, t)
        du = []
        for blk in range(S5_NB):
            lb = _z_load(lr, li, blk, t, 0).astype(BF16)
            du.append(_dot_nt(lb, bb_ref[blk]))
            dbb_ref[blk] += _dot_tn(ub[:, blk * 128:(blk + 1) * 128], lb)
        dug_ref[:, :D_MODEL] = (jnp.concatenate(du, axis=1) + dy1 * dv).astype(dug_ref.dtype)
        put(slot, step).start()

        @pl.when(step == nt - 1)
        def _():
            put(slot, step).wait()
            if nt > 1:
                put(1 - slot, step - 1).wait()

    rev = lambda i: nt - 1 - i
    const2 = lambda shape: pl.BlockSpec(shape, lambda i: (0,) * len(shape))
    row_out = lambda w: pl.BlockSpec((t, w), lambda i: (rev(i), 0))
    zshape = (2 * S5_ZQ, 8, 128)
    hbm = pl.BlockSpec(memory_space=pl.ANY)
    return pl.pallas_call(
        body, name="s5_bwd", grid=(nt,),
        in_specs=[pl.BlockSpec((t, D_MODEL), lambda i: (rev(i), 3)), pl.BlockSpec((t, D_MODEL), lambda i: (rev(i), 4)),
                  pl.BlockSpec((t, D_MODEL), lambda i: (rev(i), 1)),
                  pl.BlockSpec((1,) + zshape, lambda i: (rev(i), 0, 0, 0)),
                  const2(bbm.shape), const2(ccm.shape), const2((1, D_MODEL)), const2((D_MODEL, D_MODEL)),
                  const2((1, D_MODEL)), const2(zshape), hbm, pl.BlockSpec((t, D_MODEL), lambda i: (rev(i), 0)),
                  pl.BlockSpec((t, D_MODEL), lambda i: (rev(i), 0))],
        out_specs=(hbm, row_out(D_MODEL), row_out(D_MODEL), const2(bbm.shape), const2(ccm.shape),
                   const2(zshape), const2((1, D_MODEL)), const2((1, D_MODEL))),
        out_shape=(jax.ShapeDtypeStruct(dproj.shape, BF16), jax.ShapeDtypeStruct((l, D_MODEL), BF16),
                   jax.ShapeDtypeStruct((l, D_MODEL), BF16), jax.ShapeDtypeStruct(bbm.shape, F32),
                   jax.ShapeDtypeStruct(ccm.shape, F32), jax.ShapeDtypeStruct(zshape, F32),
                   jax.ShapeDtypeStruct((1, D_MODEL), F32), jax.ShapeDtypeStruct((1, D_MODEL), F32)),
        scratch_shapes=[pltpu.VMEM((S5_ZQ, 8 * t + 8, 128), F32), pltpu.VMEM((S5_ZQ, 8 * t + 8, 128), F32),
                        pltpu.VMEM((S5_ZQ, 8 * t, 128), F32), pltpu.VMEM((S5_ZQ, 8 * t, 128), F32),
                        pltpu.VMEM(zshape, F32), pltpu.VMEM(zshape, F32),
                        pltpu.VMEM((2, t, 2 * D_MODEL), BF16), pltpu.SemaphoreType.DMA((2,))],
        input_output_aliases={10: 0},
        compiler_params=_cp("arbitrary"),
    )(proj, proj, dmix, xstart, bbm, ccm, d_row, glu_w, glu_b, tabs, dproj, y1, z)


def _attn_probs(qh, kh):
    s = _dot_nt(qh, kh) * (XA_DH ** -0.5)
    e = jnp.exp(s - jnp.max(s, axis=-1, keepdims=True))
    return e / jnp.sum(e, axis=-1, keepdims=True)


def _attn_fwd(qa, ka, va):
    l = qa.shape[0]
    m = ka.shape[0]
    tl = _pick(l, (2048, 1024, 512, 256))

    def body(q_ref, k_ref, v_ref, o_ref):
        for h in range(XA_HEADS):
            hs = slice(h * XA_DH, (h + 1) * XA_DH)
            p = _attn_probs(q_ref[:, hs], k_ref[:, hs])
            o_ref[:, hs] = _dot(p.astype(BF16), v_ref[:, hs]).astype(o_ref.dtype)

    return pl.pallas_call(
        body, name="xattn_fwd", grid=(l // tl,),
        in_specs=[pl.BlockSpec((tl, D_MODEL), lambda i: (i, 0)), pl.BlockSpec((m, D_MODEL), lambda i: (0, 0)),
                  pl.BlockSpec((m, D_MODEL), lambda i: (0, 0))],
        out_specs=pl.BlockSpec((tl, D_MODEL), lambda i: (i, 0)),
        out_shape=jax.ShapeDtypeStruct((l, D_MODEL), BF16), compiler_params=_cp("parallel"),
    )(qa, ka, va)


def _attn_bwd(qa, ka, va, doa):
    l = qa.shape[0]
    m = ka.shape[0]
    tl = _pick(l, (2048, 1024, 512, 256))

    def body(q_ref, k_ref, v_ref, do_ref, dq_ref, dk_ref, dv_ref):
        @pl.when(pl.program_id(0) == 0)
        def _():
            dk_ref[...] = jnp.zeros_like(dk_ref)
            dv_ref[...] = jnp.zeros_like(dv_ref)

        for h in range(XA_HEADS):
            hs = slice(h * XA_DH, (h + 1) * XA_DH)
            qh, kh, vh, doh = q_ref[:, hs], k_ref[:, hs], v_ref[:, hs], do_ref[:, hs]
            p = _attn_probs(qh, kh)
            dv_ref[:, hs] += _dot_tn(p.astype(BF16), doh)
            dp = _dot_nt(doh, vh)
            ds = (p * (dp - jnp.sum(dp * p, axis=-1, keepdims=True)) * (XA_DH ** -0.5)).astype(BF16)
            dq_ref[:, hs] = _dot(ds, kh).astype(dq_ref.dtype)
            dk_ref[:, hs] += _dot_tn(ds, qh)

    row = pl.BlockSpec((tl, D_MODEL), lambda i: (i, 0))
    mem = pl.BlockSpec((m, D_MODEL), lambda i: (0, 0))
    return pl.pallas_call(
        body, name="xattn_bwd", grid=(l // tl,), in_specs=[row, mem, mem, row], out_specs=(row, mem, mem),
        out_shape=(jax.ShapeDtypeStruct((l, D_MODEL), BF16), jax.ShapeDtypeStruct((m, D_MODEL), F32),
                   jax.ShapeDtypeStruct((m, D_MODEL), F32)),
        compiler_params=_cp("arbitrary"),
    )(qa, ka, va, doa)


def _me_and_peers():
    x, y, c = lax.axis_index("x"), lax.axis_index("y"), lax.axis_index("c")
    flip = lambda v, bit: (1 - v) if bit else v
    peers = []
    for k in range(1, N_DEV):
        px, py, pc = flip(x, (k >> 2) & 1), flip(y, (k >> 1) & 1), flip(c, k & 1)
        peers.append(((px, py, pc), 4 * px + 2 * py + pc))
    return 4 * x + 2 * y + c, peers


class _SideJob:
    def __init__(self, srcs, landing, src_of, dst_of):
        self.srcs = list(srcs)
        self.landing = list(landing)
        self.n = len(self.landing)
        self.src_of, self.dst_of = src_of, dst_of
        hbm = pl.BlockSpec(memory_space=pl.ANY)
        self.in_specs = [hbm] * len(self.srcs)
        self.out_specs = [hbm] * self.n
        self.scratch = [pltpu.SemaphoreType.DMA((self.n * (N_DEV - 1),)), pltpu.SemaphoreType.DMA((self.n * (N_DEV - 1),)),
                        pltpu.SemaphoreType.DMA((self.n,))]

    def _copies(self, src_refs, out_refs, sems):
        send_sems, recv_sems, loc_sems = sems
        me, peers = _me_and_peers()
        local = [pltpu.make_async_copy(self.src_of(a, me, src_refs), self.dst_of(a, me, out_refs), loc_sems.at[a])
                 for a in range(self.n)]
        sends, recvs = [], []
        for k, (peer, peer_idx) in enumerate(peers):
            for a in range(self.n):
                s = self.n * k + a
                sends.append(pltpu.make_async_remote_copy(
                    src_ref=self.src_of(a, peer_idx, src_refs), dst_ref=self.dst_of(a, me, out_refs),
                    send_sem=send_sems.at[s], recv_sem=recv_sems.at[s], device_id=peer, device_id_type=MESH))
                recvs.append(pltpu.make_async_remote_copy(
                    src_ref=self.src_of(a, me, src_refs), dst_ref=self.dst_of(a, peer_idx, out_refs),
                    send_sem=send_sems.at[s], recv_sem=recv_sems.at[s], device_id=peer, device_id_type=MESH))
        return local, sends, recvs

    def start(self, src_refs, out_refs, sems):
        if not self.n:
            return
        local, sends, _ = self._copies(src_refs, out_refs, sems)
        for cp in local + sends:
            cp.start()

    def wait(self, src_refs, out_refs, sems):
        if not self.n:
            return
        local, sends, recvs = self._copies(src_refs, out_refs, sems)
        for cp in recvs:
            cp.wait_recv()
        for cp in sends:
            cp.wait_send()
        for cp in local:
            cp.wait()


def _gather_job(shards):
    return _SideJob(shards, [jax.ShapeDtypeStruct((N_DEV,) + s.shape, s.dtype) for s in shards],
                    src_of=lambda a, j, srcs: srcs[a], dst_of=lambda a, j, outs: outs[a].at[j])


def _scatter_job(grads):
    landing, parts = [], []
    for g in grads:
        if g.ndim == 3:
            landing.append(jax.ShapeDtypeStruct(g.shape, g.dtype))
            parts.append(None)
        else:
            r = g.shape[0] // N_DEV
            landing.append(jax.ShapeDtypeStruct((N_DEV, r, g.shape[1]), g.dtype))
            parts.append(r)

    def src_of(a, j, srcs):
        if parts[a] is None:
            return srcs[a].at[j]
        return srcs[a].at[pl.ds(pl.multiple_of(j * parts[a], 8), parts[a]), :]

    return _SideJob(grads, landing, src_of=src_of, dst_of=lambda a, j, outs: outs[a].at[j])


def _prologue(w_in_shard, row_shards, x, g, pos_col, inv_row):
    n_row = len(row_shards)
    l, d = x.shape
    tr = _pick(l, (1024, 512, 256))
    nt = l // tr
    mid = nt // 2

    def body(*refs):
        win_ref = refs[0]
        row_refs = refs[1:1 + n_row]
        x_ref, g_ref, p_ref, inv_ref = refs[1 + n_row:5 + n_row]
        out_win = refs[5 + n_row]
        row_outs = refs[6 + n_row:6 + 2 * n_row]
        h_ref, cos_ref, sin_ref = refs[6 + 2 * n_row:9 + 2 * n_row]
        win_b, send_sems, recv_sems, local_sem = refs[9 + 2 * n_row:]
        step = pl.program_id(0)
        cx, cy, cc = lax.axis_index("x"), lax.axis_index("y"), lax.axis_index("c")
        me, sibling = (cx, cy, cc), (cx, cy, 1 - cc)
        chips = [(1 - cx, cy), (cx, 1 - cy), (1 - cx, 1 - cy)]
        slot = lambda p: out_win.at[4 * p[0] + 2 * p[1] + p[2]]

        def copy(k, block, to, src=None):
            return pltpu.make_async_remote_copy(
                src_ref=slot(block) if src is None else src, dst_ref=slot(block), send_sem=send_sems.at[k],
                recv_sem=recv_sems.at[k], device_id=to, device_id_type=MESH)

        mine = pltpu.make_async_copy(win_b, slot(me), local_sem)
        first = [copy(0, me, sibling, src=win_b)]
        first += [copy(1 + j, me, (*chip, cc), src=win_b) for j, chip in enumerate(chips)]
        passed = [copy(4 + j, (*chip, cc), sibling) for j, chip in enumerate(chips)]

        @pl.when(step == 0)
        def _():
            win_b[...] = win_ref[...].astype(BF16)
            mine.start()
            for cp in first:
                cp.start()
            for r, o in zip(row_refs, row_outs):
                o[...] = r[...].astype(BF16)

        h_ref[...] = (_rms(x_ref[...])[1] * g_ref[...]).astype(h_ref.dtype)
        ang = p_ref[...].astype(F32) * inv_ref[...]
        lane = lax.broadcasted_iota(jnp.int32, ang.shape, 1)
        cos_ref[...] = jnp.tile(jnp.cos(ang), (1, RET_QK // 128))
        sin_ref[...] = jnp.tile(jnp.where((lane % RET_DK) < RET_DK // 2, -jnp.sin(ang), jnp.sin(ang)),
                                (1, RET_QK // 128))

        @pl.when(step == mid)
        def _():
            for j, chip in enumerate(chips):
                copy(1 + j, (*chip, cc), me).wait_recv()
                passed[j].start()

        @pl.when(step == nt - 1)
        def _():
            copy(0, sibling, me).wait_recv()
            for j, chip in enumerate(chips):
                copy(4 + j, (*chip, 1 - cc), me).wait_recv()
            for cp in first + passed:
                cp.wait_send()
            mine.wait()

    whole = lambda a: pl.BlockSpec(a.shape, lambda i: (0,) * a.ndim)
    rows = lambda w: pl.BlockSpec((tr, w), lambda i: (i, 0))
    return pl.pallas_call(
        body, name="prologue_allgather_w_in", grid=(nt,),
        in_specs=[whole(w_in_shard)] + [whole(r) for r in row_shards] + [rows(d), whole(g), rows(1), whole(inv_row)],
        out_specs=(pl.BlockSpec(memory_space=pl.ANY), *[whole(r) for r in row_shards], rows(d), rows(RET_QK),
                   rows(RET_QK)),
        out_shape=(jax.ShapeDtypeStruct((N_DEV,) + w_in_shard.shape, BF16),
                   *[jax.ShapeDtypeStruct(r.shape, BF16) for r in row_shards],
                   jax.ShapeDtypeStruct((l, d), BF16), jax.ShapeDtypeStruct((l, RET_QK), F32),
                   jax.ShapeDtypeStruct((l, RET_QK), F32)),
        scratch_shapes=[pltpu.VMEM(w_in_shard.shape, BF16), pltpu.SemaphoreType.DMA((N_DEV - 1,)),
                        pltpu.SemaphoreType.DMA((N_DEV - 1,)), pltpu.SemaphoreType.DMA],
        compiler_params=_cp("arbitrary"),
    )(w_in_shard, *row_shards, x, g, pos_col, inv_row)


def _allreduce_small(small):
    rows = SMALL_ROWS // N_DEV

    def body(x_ref, out_ref, land, send1, recv1, send2, recv2):
        me, peers = _me_and_peers()
        block = lambda j: pl.ds(pl.multiple_of(j * rows, 8), rows)

        def phase(src_of, dst_of, send_sems, recv_sems):
            sends = [pltpu.make_async_remote_copy(src_ref=src_of(pidx), dst_ref=dst_of(me), send_sem=send_sems.at[k],
                                                  recv_sem=recv_sems.at[k], device_id=peer, device_id_type=MESH)
                     for k, (peer, pidx) in enumerate(peers)]
            recvs = [pltpu.make_async_remote_copy(src_ref=src_of(me), dst_ref=dst_of(pidx), send_sem=send_sems.at[k],
                                                  recv_sem=recv_sems.at[k], device_id=peer, device_id_type=MESH)
                     for k, (peer, pidx) in enumerate(peers)]
            for cp in sends:
                cp.start()
            for cp in recvs:
                cp.wait_recv()
            for cp in sends:
                cp.wait_send()

        land[me] = x_ref[block(me), :]
        phase(lambda j: x_ref.at[block(j), :], lambda j: land.at[j], send1, recv1)
        total = land[0]
        for j in range(1, N_DEV):
            total = total + land[j]
        out_ref[block(me), :] = total
        phase(lambda j: out_ref.at[block(me), :], lambda j: out_ref.at[block(j), :], send2, recv2)

    vm = pl.BlockSpec(memory_space=pltpu.VMEM)
    return pl.pallas_call(
        body, name="allreduce_small", in_specs=[vm], out_specs=vm, out_shape=jax.ShapeDtypeStruct(small.shape, F32),
        scratch_shapes=[pltpu.VMEM((N_DEV, rows, D_MODEL), F32)] + [pltpu.SemaphoreType.DMA((N_DEV - 1,))] * 4,
    )(small)


def _adamw(name, got, w, m, v):
    r, c = w.shape
    n_slots = got.shape[0]
    tr = _pick(r, (256, 128, 64))

    def body(got_ref, w_ref, m_ref, v_ref, g_ref, d_ref, nm_ref, nv_ref):
        g = got_ref[0].astype(F32)
        for j in range(1, n_slots):
            g = g + got_ref[j].astype(F32)
        nm = ADAM_B1 * m_ref[...] + (1.0 - ADAM_B1) * g
        nv = ADAM_B2 * v_ref[...] + (1.0 - ADAM_B2) * jnp.square(g)
        m_hat = nm / (1.0 - ADAM_B1 ** ADAM_STEP)
        v_hat = nv / (1.0 - ADAM_B2 ** ADAM_STEP)
        g_ref[...] = g
        d_ref[...] = -ADAM_LR * (m_hat / (jnp.sqrt(v_hat) + ADAM_EPS) + ADAM_WD * w_ref[...])
        nm_ref[...] = nm
        nv_ref[...] = nv

    blk = pl.BlockSpec((tr, c), lambda i: (i, 0))
    out = jax.ShapeDtypeStruct((r, c), F32)
    return pl.pallas_call(
        body, name=name, grid=(r // tr,),
        in_specs=[pl.BlockSpec((n_slots, tr, c), lambda i: (0, i, 0)), blk, blk, blk],
        out_specs=(blk, blk, blk, blk), out_shape=(out, out, out, out), compiler_params=_cp("parallel"),
    )(got, w, m, v)


_SMALL_VECS = ("norm1_g", "ret_gn_g", "s5_d", "s5_glu_b", "norm2_g", "norm_mem_g", "norm_f_g")


def _small_layout():
    lay, row = {}, 0
    for n in _SMALL_VECS + ("loss",):
        lay[n] = (row, 1, D_MODEL)
        row += 1
    for n in ("s5_a_re", "s5_a_im"):
        lay[n] = (row, 4, D_MODEL)
        row += 4
    lay["s5_log_dt"] = (row, 1, S5_G)
    row += 8
    for n in ("s5_b_re", "s5_b_im", "s5_c_re", "s5_c_im"):
        lay[n] = (row, 64, D_MODEL)
        row += 64
    assert row <= SMALL_ROWS
    return lay


def _pack_small(t, loss_row=None):
    lay = _small_layout()
    pieces = [t[n].reshape(1, D_MODEL) for n in _SMALL_VECS]
    pieces.append(jnp.zeros((1, D_MODEL), F32) if loss_row is None else loss_row)
    pieces += [t["s5_a_re"].reshape(4, D_MODEL), t["s5_a_im"].reshape(4, D_MODEL)]
    pieces.append(jnp.pad(t["s5_log_dt"].reshape(1, S5_G), ((0, 7), (0, D_MODEL - S5_G))))
    pieces += [t[n].reshape(64, D_MODEL) for n in ("s5_b_re", "s5_b_im", "s5_c_re", "s5_c_im")]
    pieces.append(jnp.zeros((SMALL_ROWS - lay["s5_c_im"][0] - 64, D_MODEL), F32))
    return jnp.concatenate(pieces, axis=0)


def _adamw_small(g_sum, w, m, v):
    lay = _small_layout()
    names = [n for n in lay if n != "loss"]

    def body(g_ref, w_ref, m_ref, v_ref, *outs):
        g = g_ref[...]
        nm = ADAM_B1 * m_ref[...] + (1.0 - ADAM_B1) * g
        nv = ADAM_B2 * v_ref[...] + (1.0 - ADAM_B2) * jnp.square(g)
        m_hat = nm / (1.0 - ADAM_B1 ** ADAM_STEP)
        v_hat = nv / (1.0 - ADAM_B2 ** ADAM_STEP)
        delta = -ADAM_LR * (m_hat / (jnp.sqrt(v_hat) + ADAM_EPS) + ADAM_WD * w_ref[...])
        for i, n in enumerate(names):
            r0, rows, lanes = lay[n]
            for part, val in enumerate((g, delta, nm, nv)):
                outs[4 * i + part][...] = val[r0:r0 + rows, 0:lanes]
        r0 = lay["loss"][0]
        outs[-1][...] = g[r0:r0 + 1, :]

    shapes = []
    for n in names:
        shapes += [jax.ShapeDtypeStruct(lay[n][1:], F32)] * 4
    shapes.append(jax.ShapeDtypeStruct((1, D_MODEL), F32))
    outs = pl.pallas_call(body, name="adamw_small", out_shape=tuple(shapes),
                          compiler_params=pltpu.CompilerParams(vmem_limit_bytes=VMEM_LIMIT))(g_sum, w, m, v)
    return {n: tuple(outs[4 * i:4 * i + 4]) for i, n in enumerate(names)}, outs[-1]


_W_NAMES = ("norm1_g", "w_in", "ret_gn_g", "s5_a_re", "s5_a_im", "s5_log_dt", "s5_b_re", "s5_b_im", "s5_c_re", "s5_c_im",
            "s5_d", "s5_glu_w", "s5_glu_b", "w_out", "norm2_g", "norm_mem_g", "xa_wq", "xa_wk", "xa_wv", "xa_wo",
            "norm_f_g")
_ROW_NAMES = ("s5_glu_w", "w_out", "xa_wq", "xa_wk", "xa_wv", "xa_wo")


def kernel(x, mem, positions, norm1_g, w_in, ret_gn_g, s5_a_re, s5_a_im, s5_log_dt, s5_b_re, s5_b_im, s5_c_re, s5_c_im, s5_d, s5_glu_w, s5_glu_b, w_out, norm2_g, norm_mem_g, xa_wq, xa_wk, xa_wv, xa_wo, norm_f_g, loss_target, m_norm1_g, m_w_in, m_ret_gn_g, m_s5_a_re, m_s5_a_im, m_s5_log_dt, m_s5_b_re, m_s5_b_im, m_s5_c_re, m_s5_c_im, m_s5_d, m_s5_glu_w, m_s5_glu_b, m_w_out, m_norm2_g, m_norm_mem_g, m_xa_wq, m_xa_wk, m_xa_wv, m_xa_wo, m_norm_f_g, v_norm1_g, v_w_in, v_ret_gn_g, v_s5_a_re, v_s5_a_im, v_s5_log_dt, v_s5_b_re, v_s5_b_im, v_s5_c_re, v_s5_c_im, v_s5_d, v_s5_glu_w, v_s5_glu_b, v_w_out, v_norm2_g, v_norm_mem_g, v_xa_wq, v_xa_wk, v_xa_wv, v_xa_wo, v_norm_f_g):
    w = dict(norm1_g=norm1_g, w_in=w_in, ret_gn_g=ret_gn_g, s5_a_re=s5_a_re, s5_a_im=s5_a_im, s5_log_dt=s5_log_dt,
             s5_b_re=s5_b_re, s5_b_im=s5_b_im, s5_c_re=s5_c_re, s5_c_im=s5_c_im, s5_d=s5_d, s5_glu_w=s5_glu_w,
             s5_glu_b=s5_glu_b, w_out=w_out, norm2_g=norm2_g, norm_mem_g=norm_mem_g, xa_wq=xa_wq, xa_wk=xa_wk,
             xa_wv=xa_wv, xa_wo=xa_wo, norm_f_g=norm_f_g)
    mom = dict(norm1_g=m_norm1_g, w_in=m_w_in, ret_gn_g=m_ret_gn_g, s5_a_re=m_s5_a_re, s5_a_im=m_s5_a_im,
               s5_log_dt=m_s5_log_dt, s5_b_re=m_s5_b_re, s5_b_im=m_s5_b_im, s5_c_re=m_s5_c_re, s5_c_im=m_s5_c_im,
               s5_d=m_s5_d, s5_glu_w=m_s5_glu_w, s5_glu_b=m_s5_glu_b, w_out=m_w_out, norm2_g=m_norm2_g,
               norm_mem_g=m_norm_mem_g, xa_wq=m_xa_wq, xa_wk=m_xa_wk, xa_wv=m_xa_wv, xa_wo=m_xa_wo,
               norm_f_g=m_norm_f_g)
    var = dict(norm1_g=v_norm1_g, w_in=v_w_in, ret_gn_g=v_ret_gn_g, s5_a_re=v_s5_a_re, s5_a_im=v_s5_a_im,
               s5_log_dt=v_s5_log_dt, s5_b_re=v_s5_b_re, s5_b_im=v_s5_b_im, s5_c_re=v_s5_c_re, s5_c_im=v_s5_c_im,
               s5_d=v_s5_d, s5_glu_w=v_s5_glu_w, s5_glu_b=v_s5_glu_b, w_out=v_w_out, norm2_g=v_norm2_g,
               norm_mem_g=v_norm_mem_g, xa_wq=v_xa_wq, xa_wk=v_xa_wk, xa_wv=v_xa_wv, xa_wo=v_xa_wo,
               norm_f_g=v_norm_f_g)
    shapes = {n: w[n].shape for n in _W_NAMES}

    x2d, mem2d, tgt = x[0], mem[0], loss_target[0]
    l = x2d.shape[0]
    ret_c = _pick(l, (256, 128))
    s5_t = _pick(l, (256, 128))
    g1, g2, gm, gf = norm1_g, norm2_g, norm_mem_g, norm_f_g.reshape(1, D_MODEL)

    half = RET_DK // 2
    inv = ROPE_BASE ** (-jnp.arange(half, dtype=F32) / half)
    win_s, *rest = _prologue(w_in[0], [w[n][0] for n in _ROW_NAMES], x2d, g1, positions[0].reshape(l, 1),
                             jnp.tile(inv, 128 // half)[None, :])
    row_shards_b, (h1, cos_t, sin_t) = rest[:len(_ROW_NAMES)], rest[len(_ROW_NAMES):]

    to_gpn = lambda b: jnp.transpose(b, (0, 2, 1)).reshape(S5_G, S5_P * S5_N)
    from_gpn = lambda b: jnp.transpose(b.reshape(S5_G, S5_P, S5_N), (0, 2, 1))
    disc_args = (s5_a_re[0], s5_a_im[0], s5_log_dt[0].reshape(S5_G, 1), to_gpn(s5_b_re[0]), to_gpn(s5_b_im[0]))
    abar_re, abar_im, bb_re_t, bb_im_t = _s5_discretize(*disc_args)
    bbm, ccm = _s5_block_mats(from_gpn(bb_re_t), from_gpn(bb_im_t), s5_c_re[0], s5_c_im[0])
    a_z = _s5_z(abar_re, abar_im)

    proj, *rows_01 = _mm_nn_slots("in_proj", h1, win_s, BF16, side=_gather_job(row_shards_b[:2]))
    full = {n: g.reshape(N_DEV * r, D_MODEL) for n, g, r in zip(_ROW_NAMES[:2], rows_01, ROW_SHARDS[:2])}
    rconsts = _ret_constants(ret_c)
    ret, o_saved, r_prev, q_rot, k_rot = _ret_fwd(proj, cos_t, sin_t, rconsts, ret_gn_g, ret_c)
    mix, xstart, y1, z_glu, *rows_xa = _s5_fwd(proj, ret, bbm, ccm, s5_d, full["s5_glu_w"], s5_glu_b, a_z, s5_t,
                                        side=_gather_job(row_shards_b[2:]))
    full.update({n: g.reshape(N_DEV * r, D_MODEL) for n, g, r in zip(_ROW_NAMES[2:], rows_xa, ROW_SHARDS[2:])})
    x1, h2 = _mm_nn("out_proj", mix, full["w_out"], F32, residual=x2d, epi=_epi_norm_fwd(g2))
    mn = _rms_fwd("norm_mem_fwd", mem2d, gm)
    qa = _mm_nn("xa_q", h2, full["xa_wq"], BF16)
    ka = _mm_nn("xa_k", mn, full["xa_wk"], BF16)
    va = _mm_nn("xa_v", mn, full["xa_wv"], BF16)
    oa = _attn_fwd(qa, ka, va)
    dx2, dgf, loss_lanes = _mm_nn("xa_o", oa, full["xa_wo"], F32, residual=x1, epi=_epi_loss(gf, tgt))

    doa = _mm_nt("xa_o_dx", dx2, full["xa_wo"], BF16)
    dwo = _mm_tn("xa_o_dw", oa, dx2, BF16)
    dqa, dka, dva = _attn_bwd(qa, ka, va, doa)
    dx1, dg2 = _mm_nt("xa_q_dx", dqa, full["xa_wq"], F32, epi=_epi_norm_bwd(x1, g2, dx2))
    dwq = _mm_tn("xa_q_dw", h2, dqa, BF16)
    dwk = _mm_tn("xa_k_dw", mn, dka, BF16)
    dwv = _mm_tn("xa_v_dw", mn, dva, BF16)
    dmn = _mm_nt("xa_v_dx", dva, full["xa_wv"], F32, residual=_mm_nt("xa_k_dx", dka, full["xa_wk"], F32))
    _, dgm = _rms_bwd("norm_mem_bwd", mem2d, gm, dmn, None)
    dmix = _mm_nt("out_proj_dx", dx1, full["w_out"], BF16)
    dwout = _mm_tn("out_proj_dw", mix, dx1, BF16)
    dret, dgn, *got_a = _ret_bwd(proj, q_rot, k_rot, cos_t, sin_t, rconsts, ret_gn_g, o_saved, r_prev, dmix, ret_c,
                                 side=_scatter_job([dwout, dwq, dwk, dwv, dwo]))
    dproj, y2, dz, dbbm, dccm, dabar, dd, dgb = _s5_bwd(proj, dmix, dret, xstart, y1, z_glu, bbm, ccm, s5_d,
                                                        full["s5_glu_w"], s5_glu_b, a_z, s5_t)
    dglu = _mm_tn("s5_glu_dw", y2, dz, BF16)
    dwin_s, got_glu = _mm_tn_slots("in_proj_dw", h1, dproj, N_DEV, BF16, side=_scatter_job([dglu]))
    grad_x, dg1, got_win = _mm_nt_slots("in_proj_dx", dproj, win_s, F32, side=_scatter_job([dwin_s]),
                                        epi=_epi_norm_bwd(x2d, g1, dx1))

    dab_re, dab_im = _s5_unz(dabar)
    dbb_re, dbb_im = _s5_block_diag_bb(dbbm)
    dc_re, dc_im = _s5_block_diag_cc(dccm)
    da_re, da_im, dlog_dt, db_re_t, db_im_t = _s5_discretize_bwd(*disc_args, dab_re, dab_im, to_gpn(dbb_re),
                                                                 to_gpn(dbb_im))
    db_re, db_im = from_gpn(db_re_t), from_gpn(db_im_t)
    small_g = dict(norm1_g=dg1, ret_gn_g=dgn, s5_d=dd, s5_glu_b=dgb, norm2_g=dg2, norm_mem_g=dgm, norm_f_g=dgf,
                   s5_a_re=da_re, s5_a_im=da_im, s5_log_dt=dlog_dt, s5_b_re=db_re, s5_b_im=db_im, s5_c_re=dc_re,
                   s5_c_im=dc_im)
    small_pack = _pack_small(small_g, loss_row=loss_lanes)

    res = {}
    got = dict(zip(("w_out", "xa_wq", "xa_wk", "xa_wv", "xa_wo"), got_a), w_in=got_win, s5_glu_w=got_glu)
    for n in ("w_in",) + _ROW_NAMES:
        res[n] = _adamw("adamw_" + n, got[n], w[n][0], mom[n][0], var[n][0])
    small_sum = _allreduce_small(small_pack)
    small_res, loss_sum = _adamw_small(small_sum, _pack_small(w), _pack_small(mom), _pack_small(var))
    loss = (0.5 / D_MODEL) * jnp.sum(loss_sum)
    res.update(small_res)

    outs = [loss, grad_x[None]]
    for part in range(4):
        for n in _W_NAMES:
            outs.append(res[n][part].reshape(shapes[n]))
    return tuple(outs)
```

```python
import jax
import jax.numpy as jnp
from jax import lax
from jax.experimental import pallas as pl
from jax.experimental.pallas import tpu as pltpu

F32 = jnp.float32
BF16 = jnp.bfloat16
MESH = pl.DeviceIdType.MESH

D_MODEL = 1024
RET_HEADS, RET_DK, RET_DV = 8, 64, 128
RET_QK = RET_HEADS * RET_DK
S5_G, S5_N, S5_P = 64, 64, 16
S5_NB = 8
S5_GB = S5_G // S5_NB
S5_BS = S5_GB * S5_N
S5_COLS = 2 * S5_G * S5_N
XA_HEADS, XA_DH = 4, 256
EPS = 1e-6
ROPE_BASE = 10000.0
N_DEV = 8
W_IN_SHARD = 640
ROW_SHARDS = (128, 256, 128, 128, 128, 128)
ROWPACK = sum(ROW_SHARDS)
SMALL_ROWS = 320
ADAM_LR, ADAM_B1, ADAM_B2, ADAM_EPS, ADAM_WD, ADAM_STEP = 0.001, 0.9, 0.999, 1e-08, 0.01, 10

VMEM_LIMIT = 56 * 1024 * 1024


def _cp(*sem):
    return pltpu.CompilerParams(dimension_semantics=tuple(sem), vmem_limit_bytes=VMEM_LIMIT)


def _dot(a, b):
    return jnp.dot(a, b, preferred_element_type=F32)


def _dot_nt(a, b):
    return lax.dot_general(a, b, (((1,), (1,)), ((), ())), preferred_element_type=F32)


def _dot_tn(a, b):
    return lax.dot_general(a, b, (((0,), (0,)), ((), ())), preferred_element_type=F32)


def _sigmoid(x):
    return 1.0 / (1.0 + jnp.exp(-x))


def _silu(x):
    return x * _sigmoid(x)


def _dsilu(x):
    s = _sigmoid(x)
    return s * (1.0 + x * (1.0 - s))


_GELU_C = 0.7978845608028654


def _gelu(x):
    return 0.5 * x * (1.0 + jnp.tanh(_GELU_C * (x + 0.044715 * (x * x * x))))


def _gelu_and_grad(x):
    t = jnp.tanh(_GELU_C * (x + 0.044715 * (x * x * x)))
    half = 0.5 * (1.0 + t)
    return x * half, half + 0.5 * x * (1.0 - t * t) * (_GELU_C * (1.0 + 3.0 * 0.044715 * (x * x)))


def _pick(n, cands):
    for c in cands:
        if n % c == 0:
            return c
    return n


class _Epilogue:
    def __init__(self, rows, vecs, row_out_dtypes, n_sums, fn):
        self.rows, self.vecs, self.row_out_dtypes, self.n_sums, self.fn = list(rows), list(vecs), list(row_out_dtypes), n_sums, fn


def _rms(x):
    rs = lax.rsqrt(jnp.mean(x * x, axis=-1, keepdims=True) + EPS)
    return rs, x * rs


def _rms_dx(dn, xn, rs):
    return rs * (dn - xn * jnp.mean(dn * xn, axis=-1, keepdims=True))


def _epi_norm_fwd(g):
    def fn(r, rows, vecs):
        return r, [_rms(r)[1] * vecs[0]], []

    return _Epilogue([], [g], [BF16], 0, fn)


def _epi_loss(gf, target):
    def fn(r, rows, vecs):
        rs, xn = _rms(r)
        e = xn * vecs[0] - rows[0]
        dy = e * (1.0 / r.shape[-1])
        return (_rms_dx(dy * vecs[0], xn, rs), [],
                [jnp.sum(dy * xn, axis=0, keepdims=True), jnp.sum(e * e, axis=0, keepdims=True)])

    return _Epilogue([target], [gf], [], 2, fn)


def _epi_norm_bwd(x, g, dres):
    def fn(r, rows, vecs):
        rs, xn = _rms(rows[0])
        return _rms_dx(r * vecs[0], xn, rs) + rows[1], [], [jnp.sum(r * xn, axis=0, keepdims=True)]

    return _Epilogue([x, dres], [g], [], 1, fn)


def _mm_core(name, operands, in_specs, out_spec, out_shape, grid, nk, dims, acc_shape, has_res, side=None, epi=None):
    n_in = 3 if has_res else 2
    n_epi_in = len(epi.rows) + len(epi.vecs) if epi else 0
    n_epi_out = len(epi.row_out_dtypes) + epi.n_sums if epi else 0
    n_side_in = len(side.srcs) if side else 0
    n_side_out = side.n if side else 0

    def body(*refs):
        a_ref, b_ref = refs[0], refs[1]
        r_ref = refs[2] if has_res else None
        epi_in = refs[n_in:n_in + n_epi_in]
        side_in = refs[n_in + n_epi_in:n_in + n_epi_in + n_side_in]
        n0 = n_in + n_epi_in + n_side_in
        o_ref = refs[n0]
        epi_out = refs[n0 + 1:n0 + 1 + n_epi_out]
        side_out = refs[n0 + 1 + n_epi_out:n0 + 1 + n_epi_out + n_side_out]
        rest = refs[n0 + 1 + n_epi_out + n_side_out:]
        acc, sems = (rest[0], rest[1:]) if nk > 1 else (None, rest)
        i, j, k = pl.program_id(0), pl.program_id(1), pl.program_id(2)
        if side:
            @pl.when((i == 0) & (j == 0) & (k == 0))
            def _():
                side.start(side_in, side_out, sems)

        def product():
            if len(b_ref.shape) == 3:
                ns = b_ref.shape[2]
                return sum(lax.dot_general(a_ref[:, p * ns:(p + 1) * ns].astype(BF16), b_ref[p].astype(BF16),
                                           (dims, ((), ())), preferred_element_type=F32)
                           for p in range(b_ref.shape[0]))
            return lax.dot_general(a_ref[...].astype(BF16), b_ref[...].astype(BF16), (dims, ((), ())),
                                   preferred_element_type=F32)

        def finish(r):
            if has_res:
                r = r + r_ref[...]
            if epi is None:
                o_ref[...] = r.astype(o_ref.dtype)
                return
            n_rows = len(epi.rows)
            main, row_vals, sums = epi.fn(r, [t[...] for t in epi_in[:n_rows]], [t[...] for t in epi_in[n_rows:]])
            o_ref[...] = main.astype(o_ref.dtype)
            for ref, val in zip(epi_out, row_vals):
                ref[...] = val.astype(ref.dtype)
            for ref, val in zip(epi_out[len(row_vals):], sums):
                @pl.when(i == 0)
                def _(ref=ref):
                    ref[...] = jnp.zeros_like(ref)

                ref[...] += val

        if nk == 1:
            finish(product())
        else:
            @pl.when(k == 0)
            def _():
                acc[...] = jnp.zeros_like(acc)

            acc[...] += product()

            @pl.when(k == nk - 1)
            def _():
                finish(acc[...])

        if side:
            @pl.when((i == grid[0] - 1) & (j == grid[1] - 1) & (k == grid[2] - 1))
            def _():
                side.wait(side_in, side_out, sems)

    acc_scratch = [pltpu.VMEM(acc_shape, F32)] if nk > 1 else []
    in_specs, out_specs, out_shapes, operands = list(in_specs), [out_spec], [out_shape], list(operands)
    if epi:
        assert grid[1] == 1, "an epilogue needs tiles that span whole rows"
        tm, n = out_spec.block_shape
        row_spec = pl.BlockSpec((tm, n), lambda i, j, k: (i, 0))
        vec_spec = pl.BlockSpec((1, n), lambda i, j, k: (0, 0))
        in_specs += [row_spec] * len(epi.rows) + [vec_spec] * len(epi.vecs)
        operands += epi.rows + epi.vecs
        out_specs += [row_spec] * len(epi.row_out_dtypes) + [vec_spec] * epi.n_sums
        out_shapes += [jax.ShapeDtypeStruct(out_shape.shape, d) for d in epi.row_out_dtypes]
        out_shapes += [jax.ShapeDtypeStruct((1, n), F32)] * epi.n_sums
    scratch = acc_scratch
    if side:
        in_specs += side.in_specs
        operands += side.srcs
        out_specs += side.out_specs
        out_shapes += side.landing
        scratch = acc_scratch + side.scratch
    plain = side is None and epi is None
    res = pl.pallas_call(
        body, name=name, grid=grid, in_specs=in_specs, out_specs=out_specs[0] if plain else tuple(out_specs),
        out_shape=out_shapes[0] if plain else tuple(out_shapes), scratch_shapes=scratch,
        compiler_params=_cp("parallel", "parallel", "arbitrary") if plain else _cp("arbitrary", "arbitrary", "arbitrary"),
    )(*operands)
    return res


def _mm_nn(name, a, b, out_dtype, residual=None, epi=None):
    m, kk = a.shape
    n = b.shape[1]
    tm, tn, tk = _pick(m, (1024, 512, 256)), _pick(n, (1024, 512)), _pick(kk, (2048, 1024, 512))
    ops = [a, b]
    specs = [pl.BlockSpec((tm, tk), lambda i, j, k: (i, k)), pl.BlockSpec((tk, tn), lambda i, j, k: (k, j))]
    if residual is not None:
        ops.append(residual)
        specs.append(pl.BlockSpec((tm, tn), lambda i, j, k: (i, j)))
    return _mm_core(name, ops, specs, pl.BlockSpec((tm, tn), lambda i, j, k: (i, j)),
                    jax.ShapeDtypeStruct((m, n), out_dtype), (m // tm, n // tn, kk // tk), kk // tk,
                    ((1,), (0,)), (tm, tn), residual is not None, epi=epi)


def _mm_nt(name, a, b, out_dtype, residual=None, epi=None):
    m, kk = a.shape
    n = b.shape[0]
    tm, tn, tk = _pick(m, (1024, 512, 256)), _pick(n, (2048, 1024, 512)), _pick(kk, (2048, 1024, 512))
    ops = [a, b]
    specs = [pl.BlockSpec((tm, tk), lambda i, j, k: (i, k)), pl.BlockSpec((tn, tk), lambda i, j, k: (j, k))]
    if residual is not None:
        ops.append(residual)
        specs.append(pl.BlockSpec((tm, tn), lambda i, j, k: (i, j)))
    return _mm_core(name, ops, specs, pl.BlockSpec((tm, tn), lambda i, j, k: (i, j)),
                    jax.ShapeDtypeStruct((m, n), out_dtype), (m // tm, n // tn, kk // tk), kk // tk,
                    ((1,), (1,)), (tm, tn), residual is not None, epi=epi)


def _mm_tn(name, a, b, out_dtype):
    kk, m = a.shape
    n = b.shape[1]
    tm, tn, tk = _pick(m, (1024, 512)), _pick(n, (1024, 512)), _pick(kk, (2048, 1024, 512, 256))
    specs = [pl.BlockSpec((tk, tm), lambda i, j, k: (k, i)), pl.BlockSpec((tk, tn), lambda i, j, k: (k, j))]
    return _mm_core(name, [a, b], specs, pl.BlockSpec((tm, tn), lambda i, j, k: (i, j)),
                    jax.ShapeDtypeStruct((m, n), out_dtype), (m // tm, n // tn, kk // tk), kk // tk,
                    ((0,), (0,)), (tm, tn), False)


def _mm_nn_slots(name, a, b_slots, out_dtype, side=None):
    m, kk = a.shape
    s, _, ns = b_slots.shape
    tm, tk = _pick(m, (4096, 2048, 1024, 512, 256)), _pick(kk, (1024, 512))
    specs = [pl.BlockSpec((tm, tk), lambda i, j, k: (i, k)), pl.BlockSpec((None, tk, ns), lambda i, j, k: (j, k, 0))]
    return _mm_core(name, [a, b_slots], specs, pl.BlockSpec((tm, ns), lambda i, j, k: (i, j)),
                    jax.ShapeDtypeStruct((m, s * ns), out_dtype), (m // tm, s, kk // tk), kk // tk,
                    ((1,), (0,)), (tm, ns), False, side)


def _mm_nt_slots(name, a, b_slots, out_dtype, side=None, epi=None):
    m = a.shape[0]
    s, n, ns = b_slots.shape
    tm, tn = _pick(m, (1024, 512, 256)), _pick(n, (1024, 512))
    per = _pick(s, (2, 1))
    specs = [pl.BlockSpec((tm, per * ns), lambda i, j, k: (i, k)),
             pl.BlockSpec((per, tn, ns), lambda i, j, k: (k, j, 0))]
    return _mm_core(name, [a, b_slots], specs, pl.BlockSpec((tm, tn), lambda i, j, k: (i, j)),
                    jax.ShapeDtypeStruct((m, n), out_dtype), (m // tm, n // tn, s // per), s // per,
                    ((1,), (1,)), (tm, tn), False, side, epi)


def _mm_tn_slots(name, a, b, s, out_dtype, side=None):
    kk, m = a.shape
    ns = b.shape[1] // s
    tm, tk = _pick(m, (1024, 512)), _pick(kk, (4096, 2048, 1024, 512, 256))
    specs = [pl.BlockSpec((tk, tm), lambda i, j, k: (k, i)), pl.BlockSpec((tk, ns), lambda i, j, k: (k, j))]
    return _mm_core(name, [a, b], specs, pl.BlockSpec((None, tm, ns), lambda i, j, k: (j, i, 0)),
                    jax.ShapeDtypeStruct((s, m, ns), out_dtype), (m // tm, s, kk // tk), kk // tk,
                    ((0,), (0,)), (tm, ns), False, side)


def _rms_fwd(name, x, g):
    r, d = x.shape
    tr = _pick(r, (1024, 512, 256))

    def body(x_ref, g_ref, o_ref):
        xv = x_ref[...]
        rs = lax.rsqrt(jnp.mean(xv * xv, axis=-1, keepdims=True) + EPS)
        o_ref[...] = (xv * rs * g_ref[...]).astype(o_ref.dtype)

    return pl.pallas_call(
        body, name=name, grid=(r // tr,),
        in_specs=[pl.BlockSpec((tr, d), lambda i: (i, 0)), pl.BlockSpec((1, d), lambda i: (0, 0))],
        out_specs=pl.BlockSpec((tr, d), lambda i: (i, 0)),
        out_shape=jax.ShapeDtypeStruct((r, d), BF16), compiler_params=_cp("parallel"),
    )(x, g)


def _rms_bwd(name, x, g, dh, dres):
    r, d = x.shape
    tr = _pick(r, (512, 256))
    has_res = dres is not None

    def body(*refs):
        if has_res:
            x_ref, g_ref, dh_ref, dr_ref, dx_ref, dg_ref = refs
        else:
            x_ref, g_ref, dh_ref, dx_ref, dg_ref = refs
        i = pl.program_id(0)

        @pl.when(i == 0)
        def _():
            dg_ref[...] = jnp.zeros_like(dg_ref)

        xv = x_ref[...]
        dhv = dh_ref[...].astype(F32)
        rs = lax.rsqrt(jnp.mean(xv * xv, axis=-1, keepdims=True) + EPS)
        xn = xv * rs
        dg_ref[...] += jnp.sum(dhv * xn, axis=0, keepdims=True)
        dn = dhv * g_ref[...]
        dx = rs * (dn - xn * jnp.mean(dn * xn, axis=-1, keepdims=True))
        if has_res:
            dx = dx + dr_ref[...]
        dx_ref[...] = dx

    row = pl.BlockSpec((tr, d), lambda i: (i, 0))
    vec = pl.BlockSpec((1, d), lambda i: (0, 0))
    ops = [x, g, dh] + ([dres] if has_res else [])
    return pl.pallas_call(
        body, name=name, grid=(r // tr,),
        in_specs=[row, vec, row] + ([row] if has_res else []),
        out_specs=(row, vec),
        out_shape=(jax.ShapeDtypeStruct((r, d), F32), jax.ShapeDtypeStruct((1, d), F32)),
        compiler_params=_cp("arbitrary"),
    )(*ops)


def _rot(x, cos_t, sin_t):
    n = x.shape[-1]
    lane = lax.broadcasted_iota(jnp.int32, x.shape, 1)
    partner = jnp.where((lane % RET_DK) < RET_DK // 2, pltpu.roll(x, n - RET_DK // 2, 1), pltpu.roll(x, RET_DK // 2, 1))
    return x * cos_t + partner * sin_t


def _ret_constants(c):
    log_g = jnp.log1p(-jnp.exp2(-5.0 - jnp.arange(RET_HEADS, dtype=F32)))
    j = jnp.arange(c, dtype=F32)
    diff = j[:, None] - j[None, :]
    decay = jnp.where(diff[None] >= 0.0, jnp.exp(log_g[:, None, None] * jnp.maximum(diff, 0.0)[None]), 0.0)
    q_w = jnp.exp(log_g[None, :] * (j + 1.0)[:, None])
    k_w = jnp.exp(log_g[None, :] * (c - 1.0 - j)[:, None])
    cd = jnp.exp(log_g * c)
    rep = lambda t: jnp.repeat(t, RET_DK, axis=1)
    cd_row = jnp.repeat(cd, RET_DV)[None, :]
    return decay, rep(q_w), rep(k_w), cd_row


def _pair_of(h, c):
    lane = lax.broadcasted_iota(jnp.int32, (c, 2 * RET_DK), 1)
    mine = (lane < RET_DK) if h % 2 == 0 else (lane >= RET_DK)
    return slice((h // 2) * 2 * RET_DK, (h // 2 + 1) * 2 * RET_DK), mine


def _keep(x, mine):
    return jnp.where(mine, x, jnp.zeros_like(x))


def _ret_fwd(proj, cos_t, sin_t, consts, gn_g, c):
    l = proj.shape[0]
    nc = l // c
    decay, qw, kw, cd_row = consts

    def body(q_ref, k_ref, v_ref, g_ref, cos_ref, sin_ref, dec_ref, qw_ref, kw_ref, cd_ref, gn_ref,
             ret_ref, o_ref, rp_ref, qb_ref, kb_ref, state):
        @pl.when(pl.program_id(0) == 0)
        def _():
            state[...] = jnp.zeros_like(state)

        cs, sn = cos_ref[...], sin_ref[...]
        qr = _rot(q_ref[...].astype(F32), cs, sn)
        kr = _rot(k_ref[...].astype(F32), cs, sn) * (RET_DK ** -0.5)
        qb, kb = qr.astype(BF16), kr.astype(BF16)
        qb_ref[...] = qb
        kb_ref[...] = kb
        qwb = (qr * qw_ref[...]).astype(BF16)
        kwb = (kr * kw_ref[...]).astype(BF16)
        vb = v_ref[...].astype(BF16)
        for h in range(RET_HEADS):
            ps, mine = _pair_of(h, c)
            vs = slice(h * RET_DV, (h + 1) * RET_DV)
            s = _dot_nt(_keep(qb[:, ps], mine), kb[:, ps]) * dec_ref[h]
            r_prev = state[h]
            rp_ref[0, h] = r_prev
            o = _dot(s.astype(BF16), vb[:, vs]) + _dot(_keep(qwb[:, ps], mine), r_prev.astype(BF16))
            state[h] = cd_ref[:, vs] * r_prev + _dot_tn(_keep(kwb[:, ps], mine), vb[:, vs])
            o_ref[:, vs] = o
            mu = jnp.mean(o, axis=-1, keepdims=True)
            var = jnp.mean(jnp.square(o - mu), axis=-1, keepdims=True)
            on = (o - mu) * lax.rsqrt(var + EPS)
            ret_ref[:, vs] = (on * gn_ref[:, vs] * _silu(g_ref[:, vs].astype(F32))).astype(ret_ref.dtype)

    const2 = lambda shape: pl.BlockSpec(shape, lambda i: (0,) * len(shape))
    return pl.pallas_call(
        body, name="retention_fwd", grid=(nc,),
        in_specs=[pl.BlockSpec((c, RET_QK), lambda i: (i, 0)), pl.BlockSpec((c, RET_QK), lambda i: (i, 1)),
                  pl.BlockSpec((c, D_MODEL), lambda i: (i, 1)), pl.BlockSpec((c, D_MODEL), lambda i: (i, 2)),
                  pl.BlockSpec((c, RET_QK), lambda i: (i, 0)), pl.BlockSpec((c, RET_QK), lambda i: (i, 0)),
                  const2((RET_HEADS, c, c)), const2((c, RET_QK)), const2((c, RET_QK)), const2((1, D_MODEL)),
                  const2((1, D_MODEL))],
        out_specs=(pl.BlockSpec((c, D_MODEL), lambda i: (i, 0)), pl.BlockSpec((c, D_MODEL), lambda i: (i, 0)),
                   pl.BlockSpec((1, RET_HEADS, 2 * RET_DK, RET_DV), lambda i: (i, 0, 0, 0)),
                   pl.BlockSpec((c, RET_QK), lambda i: (i, 0)), pl.BlockSpec((c, RET_QK), lambda i: (i, 0))),
        out_shape=(jax.ShapeDtypeStruct((l, 2 * D_MODEL), BF16), jax.ShapeDtypeStruct((l, D_MODEL), F32),
                   jax.ShapeDtypeStruct((nc, RET_HEADS, 2 * RET_DK, RET_DV), F32),
                   jax.ShapeDtypeStruct((l, RET_QK), BF16), jax.ShapeDtypeStruct((l, RET_QK), BF16)),
        scratch_shapes=[pltpu.VMEM((RET_HEADS, 2 * RET_DK, RET_DV), F32)],
        compiler_params=_cp("arbitrary"),
    )(proj, proj, proj, proj, cos_t, sin_t, decay, qw, kw, cd_row, gn_g)


def _ret_bwd(proj, qb_saved, kb_saved, cos_t, sin_t, consts, gn_g, o_saved, r_prev_saved, dmix, c, side):
    l = proj.shape[0]
    nc = l // c
    decay, qw, kw, cd_row = consts
    n_in = 14

    def body(*refs):
        (q_ref, k_ref, v_ref, g_ref, cos_ref, sin_ref, dec_ref, qw_ref, kw_ref, cd_ref, gn_ref, o_ref, rp_ref,
         dr_ref) = refs[:n_in]
        side_in = refs[n_in:n_in + len(side.srcs)]
        out_ref, dgn_ref = refs[n_in + len(side.srcs):n_in + len(side.srcs) + 2]
        side_out = refs[n_in + len(side.srcs) + 2:n_in + len(side.srcs) + 2 + side.n]
        state, dq_s, dk_s = refs[n_in + len(side.srcs) + 2 + side.n:n_in + len(side.srcs) + 5 + side.n]
        sems = refs[n_in + len(side.srcs) + 5 + side.n:]

        @pl.when(pl.program_id(0) == 0)
        def _():
            side.start(side_in, side_out, sems)
            state[...] = jnp.zeros_like(state)
            dgn_ref[...] = jnp.zeros_like(dgn_ref)

        cs, sn = cos_ref[...], sin_ref[...]
        qb, kb = q_ref[...], k_ref[...]
        qwv, kwv = qw_ref[...], kw_ref[...]
        qwb = (qb.astype(F32) * qwv).astype(BF16)
        kwb = (kb.astype(F32) * kwv).astype(BF16)
        vb = v_ref[...].astype(BF16)
        dq2 = dk2 = None
        for h in range(RET_HEADS):
            ps, mine = _pair_of(h, c)
            vs = slice(h * RET_DV, (h + 1) * RET_DV)
            dec = dec_ref[h]
            qm, km = _keep(qb[:, ps], mine), _keep(kb[:, ps], mine)
            o = o_ref[:, vs]
            mu = jnp.mean(o, axis=-1, keepdims=True)
            var = jnp.mean(jnp.square(o - mu), axis=-1, keepdims=True)
            rstd = lax.rsqrt(var + EPS)
            on = (o - mu) * rstd
            gate = g_ref[:, vs].astype(F32)
            sg = _silu(gate)
            dret = dr_ref[:, vs].astype(F32)
            gn = gn_ref[:, vs]
            dgn_ref[:, vs] += jnp.sum(dret * on * sg, axis=0, keepdims=True)
            out_ref[:, 2 * RET_QK + D_MODEL + h * RET_DV:2 * RET_QK + D_MODEL + (h + 1) * RET_DV] = (
                dret * on * gn * _dsilu(gate)).astype(out_ref.dtype)
            don = dret * gn * sg
            do = rstd * (don - jnp.mean(don, axis=-1, keepdims=True)
                         - on * jnp.mean(don * on, axis=-1, keepdims=True))
            dob = do.astype(BF16)
            sn_h = state[h]
            snb = sn_h.astype(BF16)
            s = _dot_nt(qm, kb[:, ps]) * dec
            dv = _dot_tn(s.astype(BF16), dob) + _dot(_keep(kwb[:, ps], mine), snb)
            out_ref[:, 2 * RET_QK + h * RET_DV:2 * RET_QK + (h + 1) * RET_DV] = dv.astype(out_ref.dtype)
            ds = (_dot_nt(dob, vb[:, vs]) * dec).astype(BF16)
            dq_h = _dot(ds, km) + qwv[:, ps] * _dot_nt(dob, rp_ref[0, h].astype(BF16))
            dk_h = _dot_tn(ds, qm) + kwv[:, ps] * _dot_nt(vb[:, vs], snb)
            state[h] = cd_ref[:, vs] * sn_h + _dot_tn(_keep(qwb[:, ps], mine), dob)
            if h % 2 == 0:
                dq2, dk2 = dq_h, dk_h
            else:
                dq_s[:, ps] = dq2 + dq_h
                dk_s[:, ps] = dk2 + dk_h
        out_ref[:, 0:RET_QK] = _rot(dq_s[...], cs, -sn).astype(out_ref.dtype)
        out_ref[:, RET_QK:2 * RET_QK] = (_rot(dk_s[...], cs, -sn) * (RET_DK ** -0.5)).astype(out_ref.dtype)

        @pl.when(pl.program_id(0) == nc - 1)
        def _():
            side.wait(side_in, side_out, sems)

    rev = lambda i: nc - 1 - i
    const2 = lambda shape: pl.BlockSpec(shape, lambda i: (0,) * len(shape))
    return pl.pallas_call(
        body, name="retention_bwd", grid=(nc,),
        in_specs=[pl.BlockSpec((c, RET_QK), lambda i: (rev(i), 0)), pl.BlockSpec((c, RET_QK), lambda i: (rev(i), 0)),
                  pl.BlockSpec((c, D_MODEL), lambda i: (rev(i), 1)), pl.BlockSpec((c, D_MODEL), lambda i: (rev(i), 2)),
                  pl.BlockSpec((c, RET_QK), lambda i: (rev(i), 0)), pl.BlockSpec((c, RET_QK), lambda i: (rev(i), 0)),
                  const2((RET_HEADS, c, c)), const2((c, RET_QK)), const2((c, RET_QK)), const2((1, D_MODEL)),
                  const2((1, D_MODEL)),
                  pl.BlockSpec((c, D_MODEL), lambda i: (rev(i), 0)),
                  pl.BlockSpec((1, RET_HEADS, 2 * RET_DK, RET_DV), lambda i: (rev(i), 0, 0, 0)),
                  pl.BlockSpec((c, D_MODEL), lambda i: (rev(i), 0))] + side.in_specs,
        out_specs=(pl.BlockSpec((c, 2 * RET_QK + 2 * D_MODEL), lambda i: (rev(i), 0)), const2((1, D_MODEL)),
                   *side.out_specs),
        out_shape=(jax.ShapeDtypeStruct((l, 2 * RET_QK + 4 * D_MODEL), BF16), jax.ShapeDtypeStruct((1, D_MODEL), F32),
                   *side.landing),
        scratch_shapes=[pltpu.VMEM((RET_HEADS, 2 * RET_DK, RET_DV), F32), pltpu.VMEM((c, RET_QK), F32),
                        pltpu.VMEM((c, RET_QK), F32)] + side.scratch,
        compiler_params=_cp("arbitrary"),
    )(qb_saved, kb_saved, proj, proj, cos_t, sin_t, decay, qw, kw, cd_row, gn_g, o_saved, r_prev_saved, dmix,
      *side.srcs)


def _zoh(a_re, a_im, log_dt):
    dt = jnp.exp(log_dt)
    mag = jnp.exp(a_re * dt)
    abar_re = mag * jnp.cos(a_im * dt)
    abar_im = mag * jnp.sin(a_im * dt)
    den = a_re * a_re + a_im * a_im
    nr, ni = abar_re - 1.0, abar_im
    f_re = (nr * a_re + ni * a_im) / den
    f_im = (ni * a_re - nr * a_im) / den
    return dt, abar_re, abar_im, f_re, f_im, den


def _lanes_p(f):
    return jnp.tile(f, (1, S5_P))


def _s5_discretize(a_re, a_im, log_dt, b_re_t, b_im_t):
    def body(ar_ref, ai_ref, ld_ref, br_ref, bi_ref, abr_ref, abi_ref, bbr_ref, bbi_ref):
        _, abar_re, abar_im, f_re, f_im, _ = _zoh(ar_ref[...], ai_ref[...], ld_ref[...])
        abr_ref[...] = abar_re
        abi_ref[...] = abar_im
        fr, fi = _lanes_p(f_re), _lanes_p(f_im)
        bbr_ref[...] = fr * br_ref[...] - fi * bi_ref[...]
        bbi_ref[...] = fr * bi_ref[...] + fi * br_ref[...]

    gn = jax.ShapeDtypeStruct((S5_G, S5_N), F32)
    gpn = jax.ShapeDtypeStruct((S5_G, S5_P * S5_N), F32)
    return pl.pallas_call(body, name="s5_discretize", out_shape=(gn, gn, gpn, gpn))(a_re, a_im, log_dt, b_re_t, b_im_t)


def _s5_discretize_bwd(a_re, a_im, log_dt, b_re_t, b_im_t, dab_re, dab_im, dbb_re_t, dbb_im_t):
    def body(ar_ref, ai_ref, ld_ref, br_ref, bi_ref, gar_ref, gai_ref, gbr_ref, gbi_ref,
             dar_ref, dai_ref, dld_ref, dbr_ref, dbi_ref):
        a_r, a_i = ar_ref[...], ai_ref[...]
        dt, abar_re, abar_im, f_re, f_im, den = _zoh(a_r, a_i, ld_ref[...])
        b_r, b_i, g_br, g_bi = br_ref[...], bi_ref[...], gbr_ref[...], gbi_ref[...]
        fr, fi = _lanes_p(f_re), _lanes_p(f_im)
        dbr_ref[...] = fr * g_br + fi * g_bi
        dbi_ref[...] = fr * g_bi - fi * g_br
        t_r = b_r * g_br + b_i * g_bi
        t_i = b_r * g_bi - b_i * g_br
        gf_r = sum(t_r[:, p * S5_N:(p + 1) * S5_N] for p in range(S5_P))
        gf_i = sum(t_i[:, p * S5_N:(p + 1) * S5_N] for p in range(S5_P))
        inv_r, inv_i = a_r / den, a_i / den
        ga_r = gar_ref[...] + gf_r * inv_r - gf_i * inv_i
        ga_i = gai_ref[...] + gf_r * inv_i + gf_i * inv_r
        q_r = -(f_re * a_r + f_im * a_i) / den
        q_i = -(f_im * a_r - f_re * a_i) / den
        gl_r = q_r * gf_r + q_i * gf_i
        gl_i = q_r * gf_i - q_i * gf_r
        dar_ref[...] = gl_r + dt * (abar_re * ga_r + abar_im * ga_i)
        dai_ref[...] = gl_i + dt * (abar_re * ga_i - abar_im * ga_r)
        la_r = a_r * abar_re - a_i * abar_im
        la_i = a_r * abar_im + a_i * abar_re
        dld_ref[...] = dt * jnp.sum(ga_r * la_r + ga_i * la_i, axis=-1, keepdims=True)

    gn = jax.ShapeDtypeStruct((S5_G, S5_N), F32)
    gpn = jax.ShapeDtypeStruct((S5_G, S5_P * S5_N), F32)
    return pl.pallas_call(
        body, name="s5_discretize_bwd", out_shape=(gn, gn, jax.ShapeDtypeStruct((S5_G, 1), F32), gpn, gpn),
    )(a_re, a_im, log_dt, b_re_t, b_im_t, dab_re, dab_im, dbb_re_t, dbb_im_t)


S5_ZQ = S5_NB // 2


def _s5_z(re, im):
    return jnp.concatenate([re.reshape(S5_ZQ, 8, 128), im.reshape(S5_ZQ, 8, 128)], axis=0)


def _s5_unz(z):
    return z[:S5_ZQ].reshape(S5_G, S5_N), z[S5_ZQ:].reshape(S5_G, S5_N)


def _s5_block_mats(bb_re, bb_im, c_re, c_im):
    eye = jnp.eye(S5_GB, dtype=F32)
    bb = jnp.stack([bb_re, bb_im], axis=0).reshape(2, S5_NB, S5_GB, S5_N, S5_P)
    bbm = jnp.einsum("rbgnp,gh->bgprhn", bb, eye).reshape(S5_NB, S5_GB * S5_P, 2 * S5_BS)
    cc = jnp.stack([c_re, -c_im], axis=0).reshape(2, S5_NB, S5_GB, S5_P, S5_N)
    ccm = jnp.einsum("rbgpn,gh->brhngp", cc, eye).reshape(S5_NB, 2 * S5_BS, S5_GB * S5_P)
    return bbm.astype(BF16), ccm.astype(BF16)


def _s5_block_diag_bb(m):
    t = m.reshape(S5_NB, S5_GB, S5_P, 2, S5_GB, S5_N)
    d = jnp.einsum("bgprgn->rbgnp", t).reshape(2, S5_G, S5_N, S5_P)
    return d[0], d[1]


def _s5_block_diag_cc(m):
    t = m.reshape(S5_NB, 2, S5_GB, S5_N, S5_GB, S5_P)
    d = jnp.einsum("brgngp->rbgpn", t).reshape(2, S5_G, S5_P, S5_N)
    return d[0], -d[1]


SCAN_UNROLL = 8


def _z_store(zr, zi, blk, res, t, off):
    q, h = blk // 2, blk % 2
    for lt in range(4):
        zr[q, pl.ds(off + 4 * h + lt, t, stride=8), :] = res[:, lt * 128:(lt + 1) * 128]
        zi[q, pl.ds(off + 4 * h + lt, t, stride=8), :] = res[:, S5_BS + lt * 128:S5_BS + (lt + 1) * 128]


def _z_load(zr, zi, blk, t, off):
    q, h = blk // 2, blk % 2
    return jnp.concatenate([zr[q, pl.ds(off + 4 * h + lt, t, stride=8), :] for lt in range(4)]
                           + [zi[q, pl.ds(off + 4 * h + lt, t, stride=8), :] for lt in range(4)], axis=1)


def _z_scan_fwd(zr, zi, a_ref, carry_ref, t, off):
    ar = [a_ref[q] for q in range(S5_ZQ)]
    ai = [a_ref[S5_ZQ + q] for q in range(S5_ZQ)]

    def step(it, carry):
        carry = list(carry)
        base = pl.multiple_of(it * (8 * SCAN_UNROLL), 8 * SCAN_UNROLL) + off
        for tt in range(SCAN_UNROLL):
            rows = pl.ds(base + 8 * tt, 8)
            for q in range(S5_ZQ):
                c_r, c_i = carry[q], carry[S5_ZQ + q]
                n_r = ar[q] * c_r - ai[q] * c_i + zr[q, rows, :]
                n_i = ar[q] * c_i + ai[q] * c_r + zi[q, rows, :]
                zr[q, rows, :] = n_r
                zi[q, rows, :] = n_i
                carry[q], carry[S5_ZQ + q] = n_r, n_i
        return tuple(carry)

    out = lax.fori_loop(0, t // SCAN_UNROLL, step, tuple(carry_ref[k] for k in range(2 * S5_ZQ)))
    for k in range(2 * S5_ZQ):
        carry_ref[k] = out[k]


def _z_scan_bwd(lr, li, xr, xi, a_ref, carry_ref, acc_ref, t):
    ar = [a_ref[q] for q in range(S5_ZQ)]
    ai = [a_ref[S5_ZQ + q] for q in range(S5_ZQ)]
    n_it = t // SCAN_UNROLL

    def step(it, state):
        carry, acc = list(state[0]), list(state[1])
        base = pl.multiple_of((n_it - 1 - it) * (8 * SCAN_UNROLL), 8 * SCAN_UNROLL)
        for tt in reversed(range(SCAN_UNROLL)):
            rows = pl.ds(base + 8 * tt, 8)
            for q in range(S5_ZQ):
                c_r, c_i = carry[q], carry[S5_ZQ + q]
                n_r = ar[q] * c_r + ai[q] * c_i + lr[q, rows, :]
                n_i = ar[q] * c_i - ai[q] * c_r + li[q, rows, :]
                lr[q, rows, :] = n_r
                li[q, rows, :] = n_i
                p_r, p_i = xr[q, rows, :], xi[q, rows, :]
                acc[q] = acc[q] + n_r * p_r + n_i * p_i
                acc[S5_ZQ + q] = acc[S5_ZQ + q] + n_i * p_r - n_r * p_i
                carry[q], carry[S5_ZQ + q] = n_r, n_i
        return tuple(carry), tuple(acc)

    k8 = range(2 * S5_ZQ)
    carry, acc = lax.fori_loop(0, n_it, step, (tuple(carry_ref[k] for k in k8), tuple(acc_ref[k] for k in k8)))
    for k in k8:
        carry_ref[k] = carry[k]
        acc_ref[k] = acc[k]


def _s5_fwd(proj, mix, bbm, ccm, d_row, glu_w, glu_b, tabs, t, side):
    l = proj.shape[0]
    nt = l // t
    n_in = 9

    def body(*refs):
        u_ref, gs_ref, bb_ref, cc_ref, d_ref, gw_ref, gb_ref, a_ref, _ = refs[:n_in]
        side_in = refs[n_in:n_in + len(side.srcs)]
        ssm_ref, xst_ref, y1_ref, z_ref = refs[n_in + len(side.srcs):n_in + len(side.srcs) + 4]
        side_out = refs[n_in + len(side.srcs) + 4:n_in + len(side.srcs) + 4 + side.n]
        zr, zi, carry = refs[n_in + len(side.srcs) + 4 + side.n:n_in + len(side.srcs) + 7 + side.n]
        sems = refs[n_in + len(side.srcs) + 7 + side.n:]

        @pl.when(pl.program_id(0) == 0)
        def _():
            side.start(side_in, side_out, sems)
            carry[...] = jnp.zeros_like(carry)

        xst_ref[0] = carry[...]
        ub = u_ref[...]
        u = ub.astype(F32)
        for blk in range(S5_NB):
            _z_store(zr, zi, blk, _dot(ub[:, blk * 128:(blk + 1) * 128], bb_ref[blk]), t, 0)
        _z_scan_fwd(zr, zi, a_ref, carry, t, 0)
        ys = jnp.concatenate(
            [_dot(_z_load(zr, zi, blk, t, 0).astype(BF16), cc_ref[blk]) for blk in range(S5_NB)], axis=1)
        y1 = ys + d_ref[...] * u
        y1_ref[...] = y1.astype(y1_ref.dtype)
        y2 = _gelu(y1)
        z = _dot(y2.astype(BF16), gw_ref[...]) + gb_ref[...]
        z_ref[...] = z.astype(z_ref.dtype)
        ssm_ref[...] = (y2 * _sigmoid(z) * _silu(gs_ref[...].astype(F32))).astype(ssm_ref.dtype)

        @pl.when(pl.program_id(0) == nt - 1)
        def _():
            side.wait(side_in, side_out, sems)

    const2 = lambda shape: pl.BlockSpec(shape, lambda i: (0,) * len(shape))
    zshape = (2 * S5_ZQ, 8, 128)
    return pl.pallas_call(
        body, name="s5_fwd", grid=(nt,),
        in_specs=[pl.BlockSpec((t, D_MODEL), lambda i: (i, 3)), pl.BlockSpec((t, D_MODEL), lambda i: (i, 4)),
                  const2(bbm.shape), const2(ccm.shape), const2((1, D_MODEL)), const2((D_MODEL, D_MODEL)),
                  const2((1, D_MODEL)), const2(zshape), pl.BlockSpec(memory_space=pl.ANY)] + side.in_specs,
        out_specs=(pl.BlockSpec((t, D_MODEL), lambda i: (i, 1)), pl.BlockSpec((1,) + zshape, lambda i: (i, 0, 0, 0)),
                   pl.BlockSpec((t, D_MODEL), lambda i: (i, 0)), pl.BlockSpec((t, D_MODEL), lambda i: (i, 0)),
                   *side.out_specs),
        out_shape=(jax.ShapeDtypeStruct((l, 2 * D_MODEL), BF16), jax.ShapeDtypeStruct((nt,) + zshape, F32),
                   jax.ShapeDtypeStruct((l, D_MODEL), BF16), jax.ShapeDtypeStruct((l, D_MODEL), BF16), *side.landing),
        scratch_shapes=[pltpu.VMEM((S5_ZQ, 8 * t, 128), F32), pltpu.VMEM((S5_ZQ, 8 * t, 128), F32),
                        pltpu.VMEM(zshape, F32)] + side.scratch,
        input_output_aliases={8: 0},
        compiler_params=_cp("arbitrary"),
    )(proj, proj, bbm, ccm, d_row, glu_w, glu_b, tabs, mix, *side.srcs)


def _s5_bwd(proj, dmix, dproj, xstart, y1, z, bbm, ccm, d_row, glu_w, glu_b, tabs, t):
    l = proj.shape[0]
    nt = l // t
    col0 = 2 * RET_QK + 2 * D_MODEL

    def body(u_ref, gs_ref, dm_ref, xst_ref, bb_ref, cc_ref, d_ref, gw_ref, gb_ref, a_ref, _, y1_ref, z_ref,
             dp_ref, y2_ref, dz_ref, dbb_ref, dcc_ref, da_ref, dd_ref, dgb_ref, xr, xi, lr, li, carry, lcarry,
             dug_s, dug_sem):
        step = pl.program_id(0)
        slot = step % 2
        dug_ref = dug_s.at[slot]

        def put(s, at_step):
            rows = pl.ds(pl.multiple_of((nt - 1 - at_step) * t, t), t)
            return pltpu.make_async_copy(dug_s.at[s], dp_ref.at[rows, pl.ds(col0, 2 * D_MODEL)], dug_sem.at[s])

        @pl.when(step >= 2)
        def _():
            put(slot, step - 2).wait()

        @pl.when(step == 0)
        def _():
            lcarry[...] = jnp.zeros_like(lcarry)
            dbb_ref[...] = jnp.zeros_like(dbb_ref)
            dcc_ref[...] = jnp.zeros_like(dcc_ref)
            da_ref[...] = jnp.zeros_like(da_ref)
            dd_ref[...] = jnp.zeros_like(dd_ref)
            dgb_ref[...] = jnp.zeros_like(dgb_ref)

        carry[...] = xst_ref[0]
        for q in range(S5_ZQ):
            xr[q, 0:8, :] = carry[q]
            xi[q, 0:8, :] = carry[S5_ZQ + q]
        ub = u_ref[...]
        u = ub.astype(F32)
        for blk in range(S5_NB):
            _z_store(xr, xi, blk, _dot(ub[:, blk * 128:(blk + 1) * 128], bb_ref[blk]), t, 8)
        _z_scan_fwd(xr, xi, a_ref, carry, t, 8)
        dv = d_ref[...]
        y2, dgelu = _gelu_and_grad(y1_ref[...].astype(F32))
        y2b = y2.astype(BF16)
        sg = _sigmoid(z_ref[...].astype(F32))
        gs = gs_ref[...].astype(F32)
        dssm = dm_ref[...].astype(F32)
        dug_ref[:, D_MODEL:] = (dssm * (y2 * sg) * _dsilu(gs)).astype(dug_ref.dtype)
        dy3 = dssm * _silu(gs)
        dz = dy3 * y2 * sg * (1.0 - sg)
        dzb = dz.astype(BF16)
        y2_ref[...] = y2b
        dz_ref[...] = dzb
        dgb_ref[...] += jnp.sum(dz, axis=0, keepdims=True)
        dy1 = (dy3 * sg + _dot_nt(dzb, gw_ref[...])) * dgelu
        dd_ref[...] += jnp.sum(dy1 * u, axis=0, keepdims=True)
        dyb = dy1.astype(BF16)
        for blk in range(S5_NB):
            ch = slice(blk * 128, (blk + 1) * 128)
            _z_store(lr, li, blk, _dot_nt(dyb[:, ch], cc_ref[blk]), t, 0)
            dcc_ref[blk] += _dot_tn(_z_load(xr, xi, blk, t, 8).astype(BF16), dyb[:, ch])
        _z_scan_bwd(lr, li, xr, xi, a_ref, lcarry, da_ref, t)
        du = []
        for blk in range(S5_NB):
            lb = _z_load(lr, li, blk, t, 0).astype(BF16)
            du.append(_dot_nt(lb, bb_ref[blk]))
            dbb_ref[blk] += _dot_tn(ub[:, blk * 128:(blk + 1) * 128], lb)
        dug_ref[:, :D_MODEL] = (jnp.concatenate(du, axis=1) + dy1 * dv).astype(dug_ref.dtype)
        put(slot, step).start()

        @pl.when(step == nt - 1)
        def _():
            put(slot, step).wait()
            if nt > 1:
                put(1 - slot, step - 1).wait()

    rev = lambda i: nt - 1 - i
    const2 = lambda shape: pl.BlockSpec(shape, lambda i: (0,) * len(shape))
    row_out = lambda w: pl.BlockSpec((t, w), lambda i: (rev(i), 0))
    zshape = (2 * S5_ZQ, 8, 128)
    hbm = pl.BlockSpec(memory_space=pl.ANY)
    return pl.pallas_call(
        body, name="s5_bwd", grid=(nt,),
        in_specs=[pl.BlockSpec((t, D_MODEL), lambda i: (rev(i), 3)), pl.BlockSpec((t, D_MODEL), lambda i: (rev(i), 4)),
                  pl.BlockSpec((t, D_MODEL), lambda i: (rev(i), 1)),
                  pl.BlockSpec((1,) + zshape, lambda i: (rev(i), 0, 0, 0)),
                  const2(bbm.shape), const2(ccm.shape), const2((1, D_MODEL)), const2((D_MODEL, D_MODEL)),
                  const2((1, D_MODEL)), const2(zshape), hbm, pl.BlockSpec((t, D_MODEL), lambda i: (rev(i), 0)),
                  pl.BlockSpec((t, D_MODEL), lambda i: (rev(i), 0))],
        out_specs=(hbm, row_out(D_MODEL), row_out(D_MODEL), const2(bbm.shape), const2(ccm.shape),
                   const2(zshape), const2((1, D_MODEL)), const2((1, D_MODEL))),
        out_shape=(jax.ShapeDtypeStruct(dproj.shape, BF16), jax.ShapeDtypeStruct((l, D_MODEL), BF16),
                   jax.ShapeDtypeStruct((l, D_MODEL), BF16), jax.ShapeDtypeStruct(bbm.shape, F32),
                   jax.ShapeDtypeStruct(ccm.shape, F32), jax.ShapeDtypeStruct(zshape, F32),
                   jax.ShapeDtypeStruct((1, D_MODEL), F32), jax.ShapeDtypeStruct((1, D_MODEL), F32)),
        scratch_shapes=[pltpu.VMEM((S5_ZQ, 8 * t + 8, 128), F32), pltpu.VMEM((S5_ZQ, 8 * t + 8, 128), F32),
                        pltpu.VMEM((S5_ZQ, 8 * t, 128), F32), pltpu.VMEM((S5_ZQ, 8 * t, 128), F32),
                        pltpu.VMEM(zshape, F32), pltpu.VMEM(zshape, F32),
                        pltpu.VMEM((2, t, 2 * D_MODEL), BF16), pltpu.SemaphoreType.DMA((2,))],
        input_output_aliases={10: 0},
        compiler_params=_cp("arbitrary"),
    )(proj, proj, dmix, xstart, bbm, ccm, d_row, glu_w, glu_b, tabs, dproj, y1, z)


def _attn_probs(qh, kh):
    s = _dot_nt(qh, kh) * (XA_DH ** -0.5)
    e = jnp.exp(s - jnp.max(s, axis=-1, keepdims=True))
    return e / jnp.sum(e, axis=-1, keepdims=True)


def _attn_fwd(qa, ka, va):
    l = qa.shape[0]
    m = ka.shape[0]
    tl = _pick(l, (2048, 1024, 512, 256))

    def body(q_ref, k_ref, v_ref, o_ref):
        for h in range(XA_HEADS):
            hs = slice(h * XA_DH, (h + 1) * XA_DH)
            p = _attn_probs(q_ref[:, hs], k_ref[:, hs])
            o_ref[:, hs] = _dot(p.astype(BF16), v_ref[:, hs]).astype(o_ref.dtype)

    return pl.pallas_call(
        body, name="xattn_fwd", grid=(l // tl,),
        in_specs=[pl.BlockSpec((tl, D_MODEL), lambda i: (i, 0)), pl.BlockSpec((m, D_MODEL), lambda i: (0, 0)),
                  pl.BlockSpec((m, D_MODEL), lambda i: (0, 0))],
        out_specs=pl.BlockSpec((tl, D_MODEL), lambda i: (i, 0)),
        out_shape=jax.ShapeDtypeStruct((l, D_MODEL), BF16), compiler_params=_cp("parallel"),
    )(qa, ka, va)


def _attn_bwd(qa, ka, va, doa):
    l = qa.shape[0]
    m = ka.shape[0]
    tl = _pick(l, (2048, 1024, 512, 256))

    def body(q_ref, k_ref, v_ref, do_ref, dq_ref, dk_ref, dv_ref):
        @pl.when(pl.program_id(0) == 0)
        def _():
            dk_ref[...] = jnp.zeros_like(dk_ref)
            dv_ref[...] = jnp.zeros_like(dv_ref)

        for h in range(XA_HEADS):
            hs = slice(h * XA_DH, (h + 1) * XA_DH)
            qh, kh, vh, doh = q_ref[:, hs], k_ref[:, hs], v_ref[:, hs], do_ref[:, hs]
            p = _attn_probs(qh, kh)
            dv_ref[:, hs] += _dot_tn(p.astype(BF16), doh)
            dp = _dot_nt(doh, vh)
            ds = (p * (dp - jnp.sum(dp * p, axis=-1, keepdims=True)) * (XA_DH ** -0.5)).astype(BF16)
            dq_ref[:, hs] = _dot(ds, kh).astype(dq_ref.dtype)
            dk_ref[:, hs] += _dot_tn(ds, qh)

    row = pl.BlockSpec((tl, D_MODEL), lambda i: (i, 0))
    mem = pl.BlockSpec((m, D_MODEL), lambda i: (0, 0))
    return pl.pallas_call(
        body, name="xattn_bwd", grid=(l // tl,), in_specs=[row, mem, mem, row], out_specs=(row, mem, mem),
        out_shape=(jax.ShapeDtypeStruct((l, D_MODEL), BF16), jax.ShapeDtypeStruct((m, D_MODEL), F32),
                   jax.ShapeDtypeStruct((m, D_MODEL), F32)),
        compiler_params=_cp("arbitrary"),
    )(qa, ka, va, doa)


def _me_and_peers():
    x, y, c = lax.axis_index("x"), lax.axis_index("y"), lax.axis_index("c")
    flip = lambda v, bit: (1 - v) if bit else v
    peers = []
    for k in range(1, N_DEV):
        px, py, pc = flip(x, (k >> 2) & 1), flip(y, (k >> 1) & 1), flip(c, k & 1)
        peers.append(((px, py, pc), 4 * px + 2 * py + pc))
    return 4 * x + 2 * y + c, peers


class _SideJob:
    def __init__(self, srcs, landing, src_of, dst_of):
        self.srcs = list(srcs)
        self.landing = list(landing)
        self.n = len(self.landing)
        self.src_of, self.dst_of = src_of, dst_of
        hbm = pl.BlockSpec(memory_space=pl.ANY)
        self.in_specs = [hbm] * len(self.srcs)
        self.out_specs = [hbm] * self.n
        self.scratch = [pltpu.SemaphoreType.DMA((self.n * (N_DEV - 1),)), pltpu.SemaphoreType.DMA((self.n * (N_DEV - 1),)),
                        pltpu.SemaphoreType.DMA((self.n,))]

    def _copies(self, src_refs, out_refs, sems):
        send_sems, recv_sems, loc_sems = sems
        me, peers = _me_and_peers()
        local = [pltpu.make_async_copy(self.src_of(a, me, src_refs), self.dst_of(a, me, out_refs), loc_sems.at[a])
                 for a in range(self.n)]
        sends, recvs = [], []
        for k, (peer, peer_idx) in enumerate(peers):
            for a in range(self.n):
                s = self.n * k + a
                sends.append(pltpu.make_async_remote_copy(
                    src_ref=self.src_of(a, peer_idx, src_refs), dst_ref=self.dst_of(a, me, out_refs),
                    send_sem=send_sems.at[s], recv_sem=recv_sems.at[s], device_id=peer, device_id_type=MESH))
                recvs.append(pltpu.make_async_remote_copy(
                    src_ref=self.src_of(a, me, src_refs), dst_ref=self.dst_of(a, peer_idx, out_refs),
                    send_sem=send_sems.at[s], recv_sem=recv_sems.at[s], device_id=peer, device_id_type=MESH))
        return local, sends, recvs

    def start(self, src_refs, out_refs, sems):
        if not self.n:
            return
        local, sends, _ = self._copies(src_refs, out_refs, sems)
        for cp in local + sends:
            cp.start()

    def wait(self, src_refs, out_refs, sems):
        if not self.n:
            return
        local, sends, recvs = self._copies(src_refs, out_refs, sems)
        for cp in recvs:
            cp.wait_recv()
        for cp in sends:
            cp.wait_send()
        for cp in local:
            cp.wait()


def _gather_job(shards):
    return _SideJob(shards, [jax.ShapeDtypeStruct((N_DEV,) + s.shape, s.dtype) for s in shards],
                    src_of=lambda a, j, srcs: srcs[a], dst_of=lambda a, j, outs: outs[a].at[j])


def _scatter_job(grads):
    landing, parts = [], []
    for g in grads:
        if g.ndim == 3:
            landing.append(jax.ShapeDtypeStruct(g.shape, g.dtype))
            parts.append(None)
        else:
            r = g.shape[0] // N_DEV
            landing.append(jax.ShapeDtypeStruct((N_DEV, r, g.shape[1]), g.dtype))
            parts.append(r)

    def src_of(a, j, srcs):
        if parts[a] is None:
            return srcs[a].at[j]
        return srcs[a].at[pl.ds(pl.multiple_of(j * parts[a], 8), parts[a]), :]

    return _SideJob(grads, landing, src_of=src_of, dst_of=lambda a, j, outs: outs[a].at[j])


def _prologue(w_in_shard, row_shards, x, g, pos_col, inv_row):
    n_row = len(row_shards)
    l, d = x.shape
    tr = _pick(l, (1024, 512, 256))
    nt = l // tr
    mid = nt - 1

    def body(*refs):
        win_ref = refs[0]
        row_refs = refs[1:1 + n_row]
        x_ref, g_ref, p_ref, inv_ref = refs[1 + n_row:5 + n_row]
        out_win = refs[5 + n_row]
        row_outs = refs[6 + n_row:6 + 2 * n_row]
        h_ref, cos_ref, sin_ref = refs[6 + 2 * n_row:9 + 2 * n_row]
        win_b, send_sems, recv_sems, local_sem = refs[9 + 2 * n_row:]
        step = pl.program_id(0)
        cx, cy, cc = lax.axis_index("x"), lax.axis_index("y"), lax.axis_index("c")
        me, sibling = (cx, cy, cc), (cx, cy, 1 - cc)
        chips = [(1 - cx, cy), (cx, 1 - cy), (1 - cx, 1 - cy)]
        slot = lambda p: out_win.at[4 * p[0] + 2 * p[1] + p[2]]

        def copy(k, block, to, src=None):
            return pltpu.make_async_remote_copy(
                src_ref=slot(block) if src is None else src, dst_ref=slot(block), send_sem=send_sems.at[k],
                recv_sem=recv_sems.at[k], device_id=to, device_id_type=MESH)

        mine = pltpu.make_async_copy(win_b, slot(me), local_sem)
        first = [copy(0, me, sibling, src=win_b)]
        first += [copy(1 + j, me, (*chip, cc), src=win_b) for j, chip in enumerate(chips)]
        passed = [copy(4 + j, (*chip, cc), sibling) for j, chip in enumerate(chips)]

        @pl.when(step == 0)
        def _():
            win_b[...] = win_ref[...].astype(BF16)
            mine.start()
            for cp in first:
                cp.start()
            for r, o in zip(row_refs, row_outs):
                o[...] = r[...].astype(BF16)

        h_ref[...] = (_rms(x_ref[...])[1] * g_ref[...]).astype(h_ref.dtype)
        ang = p_ref[...].astype(F32) * inv_ref[...]
        lane = lax.broadcasted_iota(jnp.int32, ang.shape, 1)
        cos_ref[...] = jnp.tile(jnp.cos(ang), (1, RET_QK // 128))
        sin_ref[...] = jnp.tile(jnp.where((lane % RET_DK) < RET_DK // 2, -jnp.sin(ang), jnp.sin(ang)),
                                (1, RET_QK // 128))

        @pl.when(step == mid)
        def _():
            for j, chip in enumerate(chips):
                copy(1 + j, (*chip, cc), me).wait_recv()
                passed[j].start()

        @pl.when(step == nt - 1)
        def _():
            copy(0, sibling, me).wait_recv()
            for j, chip in enumerate(chips):
                copy(4 + j, (*chip, 1 - cc), me).wait_recv()
            for cp in first + passed:
                cp.wait_send()
            mine.wait()

    whole = lambda a: pl.BlockSpec(a.shape, lambda i: (0,) * a.ndim)
    rows = lambda w: pl.BlockSpec((tr, w), lambda i: (i, 0))
    return pl.pallas_call(
        body, name="prologue_allgather_w_in", grid=(nt,),
        in_specs=[whole(w_in_shard)] + [whole(r) for r in row_shards] + [rows(d), whole(g), rows(1), whole(inv_row)],
        out_specs=(pl.BlockSpec(memory_space=pl.ANY), *[whole(r) for r in row_shards], rows(d), rows(RET_QK),
                   rows(RET_QK)),
        out_shape=(jax.ShapeDtypeStruct((N_DEV,) + w_in_shard.shape, BF16),
                   *[jax.ShapeDtypeStruct(r.shape, BF16) for r in row_shards],
                   jax.ShapeDtypeStruct((l, d), BF16), jax.ShapeDtypeStruct((l, RET_QK), F32),
                   jax.ShapeDtypeStruct((l, RET_QK), F32)),
        scratch_shapes=[pltpu.VMEM(w_in_shard.shape, BF16), pltpu.SemaphoreType.DMA((N_DEV - 1,)),
                        pltpu.SemaphoreType.DMA((N_DEV - 1,)), pltpu.SemaphoreType.DMA],
        compiler_params=_cp("arbitrary"),
    )(w_in_shard, *row_shards, x, g, pos_col, inv_row)


def _allreduce_small(small):
    rows = SMALL_ROWS // N_DEV

    def body(x_ref, out_ref, land, send1, recv1, send2, recv2):
        me, peers = _me_and_peers()
        block = lambda j: pl.ds(pl.multiple_of(j * rows, 8), rows)

        def phase(src_of, dst_of, send_sems, recv_sems):
            sends = [pltpu.make_async_remote_copy(src_ref=src_of(pidx), dst_ref=dst_of(me), send_sem=send_sems.at[k],
                                                  recv_sem=recv_sems.at[k], device_id=peer, device_id_type=MESH)
                     for k, (peer, pidx) in enumerate(peers)]
            recvs = [pltpu.make_async_remote_copy(src_ref=src_of(me), dst_ref=dst_of(pidx), send_sem=send_sems.at[k],
                                                  recv_sem=recv_sems.at[k], device_id=peer, device_id_type=MESH)
                     for k, (peer, pidx) in enumerate(peers)]
            for cp in sends:
                cp.start()
            for cp in recvs:
                cp.wait_recv()
            for cp in sends:
                cp.wait_send()

        land[me] = x_ref[block(me), :]
        phase(lambda j: x_ref.at[block(j), :], lambda j: land.at[j], send1, recv1)
        total = land[0]
        for j in range(1, N_DEV):
            total = total + land[j]
        out_ref[block(me), :] = total
        phase(lambda j: out_ref.at[block(me), :], lambda j: out_ref.at[block(j), :], send2, recv2)

    vm = pl.BlockSpec(memory_space=pltpu.VMEM)
    return pl.pallas_call(
        body, name="allreduce_small", in_specs=[vm], out_specs=vm, out_shape=jax.ShapeDtypeStruct(small.shape, F32),
        scratch_shapes=[pltpu.VMEM((N_DEV, rows, D_MODEL), F32)] + [pltpu.SemaphoreType.DMA((N_DEV - 1,))] * 4,
    )(small)


def _adamw(name, got, w, m, v):
    r, c = w.shape
    n_slots = got.shape[0]
    tr = _pick(r, (256, 128, 64))

    def body(got_ref, w_ref, m_ref, v_ref, g_ref, d_ref, nm_ref, nv_ref):
        g = got_ref[0].astype(F32)
        for j in range(1, n_slots):
            g = g + got_ref[j].astype(F32)
        nm = ADAM_B1 * m_ref[...] + (1.0 - ADAM_B1) * g
        nv = ADAM_B2 * v_ref[...] + (1.0 - ADAM_B2) * jnp.square(g)
        m_hat = nm / (1.0 - ADAM_B1 ** ADAM_STEP)
        v_hat = nv / (1.0 - ADAM_B2 ** ADAM_STEP)
        g_ref[...] = g
        d_ref[...] = -ADAM_LR * (m_hat / (jnp.sqrt(v_hat) + ADAM_EPS) + ADAM_WD * w_ref[...])
        nm_ref[...] = nm
        nv_ref[...] = nv

    blk = pl.BlockSpec((tr, c), lambda i: (i, 0))
    out = jax.ShapeDtypeStruct((r, c), F32)
    return pl.pallas_call(
        body, name=name, grid=(r // tr,),
        in_specs=[pl.BlockSpec((n_slots, tr, c), lambda i: (0, i, 0)), blk, blk, blk],
        out_specs=(blk, blk, blk, blk), out_shape=(out, out, out, out), compiler_params=_cp("parallel"),
    )(got, w, m, v)


_SMALL_VECS = ("norm1_g", "ret_gn_g", "s5_d", "s5_glu_b", "norm2_g", "norm_mem_g", "norm_f_g")


def _small_layout():
    lay, row = {}, 0
    for n in _SMALL_VECS + ("loss",):
        lay[n] = (row, 1, D_MODEL)
        row += 1
    for n in ("s5_a_re", "s5_a_im"):
        lay[n] = (row, 4, D_MODEL)
        row += 4
    lay["s5_log_dt"] = (row, 1, S5_G)
    row += 8
    for n in ("s5_b_re", "s5_b_im", "s5_c_re", "s5_c_im"):
        lay[n] = (row, 64, D_MODEL)
        row += 64
    assert row <= SMALL_ROWS
    return lay


def _pack_small(t, loss_row=None):
    lay = _small_layout()
    pieces = [t[n].reshape(1, D_MODEL) for n in _SMALL_VECS]
    pieces.append(jnp.zeros((1, D_MODEL), F32) if loss_row is None else loss_row)
    pieces += [t["s5_a_re"].reshape(4, D_MODEL), t["s5_a_im"].reshape(4, D_MODEL)]
    pieces.append(jnp.pad(t["s5_log_dt"].reshape(1, S5_G), ((0, 7), (0, D_MODEL - S5_G))))
    pieces += [t[n].reshape(64, D_MODEL) for n in ("s5_b_re", "s5_b_im", "s5_c_re", "s5_c_im")]
    pieces.append(jnp.zeros((SMALL_ROWS - lay["s5_c_im"][0] - 64, D_MODEL), F32))
    return jnp.concatenate(pieces, axis=0)


def _adamw_small(g_sum, w, m, v):
    lay = _small_layout()
    names = [n for n in lay if n != "loss"]

    def body(g_ref, w_ref, m_ref, v_ref, *outs):
        g = g_ref[...]
        nm = ADAM_B1 * m_ref[...] + (1.0 - ADAM_B1) * g
        nv = ADAM_B2 * v_ref[...] + (1.0 - ADAM_B2) * jnp.square(g)
        m_hat = nm / (1.0 - ADAM_B1 ** ADAM_STEP)
        v_hat = nv / (1.0 - ADAM_B2 ** ADAM_STEP)
        delta = -ADAM_LR * (m_hat / (jnp.sqrt(v_hat) + ADAM_EPS) + ADAM_WD * w_ref[...])
        for i, n in enumerate(names):
            r0, rows, lanes = lay[n]
            for part, val in enumerate((g, delta, nm, nv)):
                outs[4 * i + part][...] = val[r0:r0 + rows, 0:lanes]
        r0 = lay["loss"][0]
        outs[-1][...] = g[r0:r0 + 1, :]

    shapes = []
    for n in names:
        shapes += [jax.ShapeDtypeStruct(lay[n][1:], F32)] * 4
    shapes.append(jax.ShapeDtypeStruct((1, D_MODEL), F32))
    outs = pl.pallas_call(body, name="adamw_small", out_shape=tuple(shapes),
                          compiler_params=pltpu.CompilerParams(vmem_limit_bytes=VMEM_LIMIT))(g_sum, w, m, v)
    return {n: tuple(outs[4 * i:4 * i + 4]) for i, n in enumerate(names)}, outs[-1]


_W_NAMES = ("norm1_g", "w_in", "ret_gn_g", "s5_a_re", "s5_a_im", "s5_log_dt", "s5_b_re", "s5_b_im", "s5_c_re", "s5_c_im",
            "s5_d", "s5_glu_w", "s5_glu_b", "w_out", "norm2_g", "norm_mem_g", "xa_wq", "xa_wk", "xa_wv", "xa_wo",
            "norm_f_g")
_ROW_NAMES = ("s5_glu_w", "w_out", "xa_wq", "xa_wk", "xa_wv", "xa_wo")


def kernel(x, mem, positions, norm1_g, w_in, ret_gn_g, s5_a_re, s5_a_im, s5_log_dt, s5_b_re, s5_b_im, s5_c_re, s5_c_im, s5_d, s5_glu_w, s5_glu_b, w_out, norm2_g, norm_mem_g, xa_wq, xa_wk, xa_wv, xa_wo, norm_f_g, loss_target, m_norm1_g, m_w_in, m_ret_gn_g, m_s5_a_re, m_s5_a_im, m_s5_log_dt, m_s5_b_re, m_s5_b_im, m_s5_c_re, m_s5_c_im, m_s5_d, m_s5_glu_w, m_s5_glu_b, m_w_out, m_norm2_g, m_norm_mem_g, m_xa_wq, m_xa_wk, m_xa_wv, m_xa_wo, m_norm_f_g, v_norm1_g, v_w_in, v_ret_gn_g, v_s5_a_re, v_s5_a_im, v_s5_log_dt, v_s5_b_re, v_s5_b_im, v_s5_c_re, v_s5_c_im, v_s5_d, v_s5_glu_w, v_s5_glu_b, v_w_out, v_norm2_g, v_norm_mem_g, v_xa_wq, v_xa_wk, v_xa_wv, v_xa_wo, v_norm_f_g):
    w = dict(norm1_g=norm1_g, w_in=w_in, ret_gn_g=ret_gn_g, s5_a_re=s5_a_re, s5_a_im=s5_a_im, s5_log_dt=s5_log_dt,
             s5_b_re=s5_b_re, s5_b_im=s5_b_im, s5_c_re=s5_c_re, s5_c_im=s5_c_im, s5_d=s5_d, s5_glu_w=s5_glu_w,
             s5_glu_b=s5_glu_b, w_out=w_out, norm2_g=norm2_g, norm_mem_g=norm_mem_g, xa_wq=xa_wq, xa_wk=xa_wk,
             xa_wv=xa_wv, xa_wo=xa_wo, norm_f_g=norm_f_g)
    mom = dict(norm1_g=m_norm1_g, w_in=m_w_in, ret_gn_g=m_ret_gn_g, s5_a_re=m_s5_a_re, s5_a_im=m_s5_a_im,
               s5_log_dt=m_s5_log_dt, s5_b_re=m_s5_b_re, s5_b_im=m_s5_b_im, s5_c_re=m_s5_c_re, s5_c_im=m_s5_c_im,
               s5_d=m_s5_d, s5_glu_w=m_s5_glu_w, s5_glu_b=m_s5_glu_b, w_out=m_w_out, norm2_g=m_norm2_g,
               norm_mem_g=m_norm_mem_g, xa_wq=m_xa_wq, xa_wk=m_xa_wk, xa_wv=m_xa_wv, xa_wo=m_xa_wo,
               norm_f_g=m_norm_f_g)
    var = dict(norm1_g=v_norm1_g, w_in=v_w_in, ret_gn_g=v_ret_gn_g, s5_a_re=v_s5_a_re, s5_a_im=v_s5_a_im,
               s5_log_dt=v_s5_log_dt, s5_b_re=v_s5_b_re, s5_b_im=v_s5_b_im, s5_c_re=v_s5_c_re, s5_c_im=v_s5_c_im,
               s5_d=v_s5_d, s5_glu_w=v_s5_glu_w, s5_glu_b=v_s5_glu_b, w_out=v_w_out, norm2_g=v_norm2_g,
               norm_mem_g=v_norm_mem_g, xa_wq=v_xa_wq, xa_wk=v_xa_wk, xa_wv=v_xa_wv, xa_wo=v_xa_wo,
               norm_f_g=v_norm_f_g)
    shapes = {n: w[n].shape for n in _W_NAMES}

    x2d, mem2d, tgt = x[0], mem[0], loss_target[0]
    l = x2d.shape[0]
    ret_c = _pick(l, (256, 128))
    s5_t = _pick(l, (256, 128))
    g1, g2, gm, gf = norm1_g, norm2_g, norm_mem_g, norm_f_g.reshape(1, D_MODEL)

    half = RET_DK // 2
    inv = ROPE_BASE ** (-jnp.arange(half, dtype=F32) / half)
    win_s, *rest = _prologue(w_in[0], [w[n][0] for n in _ROW_NAMES], x2d, g1, positions[0].reshape(l, 1),
                             jnp.tile(inv, 128 // half)[None, :])
    row_shards_b, (h1, cos_t, sin_t) = rest[:len(_ROW_NAMES)], rest[len(_ROW_NAMES):]

    to_gpn = lambda b: jnp.transpose(b, (0, 2, 1)).reshape(S5_G, S5_P * S5_N)
    from_gpn = lambda b: jnp.transpose(b.reshape(S5_G, S5_P, S5_N), (0, 2, 1))
    disc_args = (s5_a_re[0], s5_a_im[0], s5_log_dt[0].reshape(S5_G, 1), to_gpn(s5_b_re[0]), to_gpn(s5_b_im[0]))
    abar_re, abar_im, bb_re_t, bb_im_t = _s5_discretize(*disc_args)
    bbm, ccm = _s5_block_mats(from_gpn(bb_re_t), from_gpn(bb_im_t), s5_c_re[0], s5_c_im[0])
    a_z = _s5_z(abar_re, abar_im)

    proj, *rows_01 = _mm_nn_slots("in_proj", h1, win_s, BF16, side=_gather_job(row_shards_b[:2]))
    full = {n: g.reshape(N_DEV * r, D_MODEL) for n, g, r in zip(_ROW_NAMES[:2], rows_01, ROW_SHARDS[:2])}
    rconsts = _ret_constants(ret_c)
    ret, o_saved, r_prev, q_rot, k_rot = _ret_fwd(proj, cos_t, sin_t, rconsts, ret_gn_g, ret_c)
    mix, xstart, y1, z_glu, *rows_xa = _s5_fwd(proj, ret, bbm, ccm, s5_d, full["s5_glu_w"], s5_glu_b, a_z, s5_t,
                                        side=_gather_job(row_shards_b[2:]))
    full.update({n: g.reshape(N_DEV * r, D_MODEL) for n, g, r in zip(_ROW_NAMES[2:], rows_xa, ROW_SHARDS[2:])})
    x1, h2 = _mm_nn("out_proj", mix, full["w_out"], F32, residual=x2d, epi=_epi_norm_fwd(g2))
    mn = _rms_fwd("norm_mem_fwd", mem2d, gm)
    qa = _mm_nn("xa_q", h2, full["xa_wq"], BF16)
    ka = _mm_nn("xa_k", mn, full["xa_wk"], BF16)
    va = _mm_nn("xa_v", mn, full["xa_wv"], BF16)
    oa = _attn_fwd(qa, ka, va)
    dx2, dgf, loss_lanes = _mm_nn("xa_o", oa, full["xa_wo"], F32, residual=x1, epi=_epi_loss(gf, tgt))

    doa = _mm_nt("xa_o_dx", dx2, full["xa_wo"], BF16)
    dwo = _mm_tn("xa_o_dw", oa, dx2, BF16)
    dqa, dka, dva = _attn_bwd(qa, ka, va, doa)
    dx1, dg2 = _mm_nt("xa_q_dx", dqa, full["xa_wq"], F32, epi=_epi_norm_bwd(x1, g2, dx2))
    dwq = _mm_tn("xa_q_dw", h2, dqa, BF16)
    dwk = _mm_tn("xa_k_dw", mn, dka, BF16)
    dwv = _mm_tn("xa_v_dw", mn, dva, BF16)
    dmn = _mm_nt("xa_v_dx", dva, full["xa_wv"], F32, residual=_mm_nt("xa_k_dx", dka, full["xa_wk"], F32))
    _, dgm = _rms_bwd("norm_mem_bwd", mem2d, gm, dmn, None)
    dmix = _mm_nt("out_proj_dx", dx1, full["w_out"], BF16)
    dwout = _mm_tn("out_proj_dw", mix, dx1, BF16)
    dret, dgn, *got_a = _ret_bwd(proj, q_rot, k_rot, cos_t, sin_t, rconsts, ret_gn_g, o_saved, r_prev, dmix, ret_c,
                                 side=_scatter_job([dwout, dwq, dwk, dwv, dwo]))
    dproj, y2, dz, dbbm, dccm, dabar, dd, dgb = _s5_bwd(proj, dmix, dret, xstart, y1, z_glu, bbm, ccm, s5_d,
                                                        full["s5_glu_w"], s5_glu_b, a_z, s5_t)
    dglu = _mm_tn("s5_glu_dw", y2, dz, BF16)
    dwin_s, got_glu = _mm_tn_slots("in_proj_dw", h1, dproj, N_DEV, BF16, side=_scatter_job([dglu]))
    grad_x, dg1, got_win = _mm_nt_slots("in_proj_dx", dproj, win_s, F32, side=_scatter_job([dwin_s]),
                                        epi=_epi_norm_bwd(x2d, g1, dx1))

    dab_re, dab_im = _s5_unz(dabar)
    dbb_re, dbb_im = _s5_block_diag_bb(dbbm)
    dc_re, dc_im = _s5_block_diag_cc(dccm)
    da_re, da_im, dlog_dt, db_re_t, db_im_t = _s5_discretize_bwd(*disc_args, dab_re, dab_im, to_gpn(dbb_re),
                                                                 to_gpn(dbb_im))
    db_re, db_im = from_gpn(db_re_t), from_gpn(db_im_t)
    small_g = dict(norm1_g=dg1, ret_gn_g=dgn, s5_d=dd, s5_glu_b=dgb, norm2_g=dg2, norm_mem_g=dgm, norm_f_g=dgf,
                   s5_a_re=da_re, s5_a_im=da_im, s5_log_dt=dlog_dt, s5_b_re=db_re, s5_b_im=db_im, s5_c_re=dc_re,
                   s5_c_im=dc_im)
    small_pack = _pack_small(small_g, loss_row=loss_lanes)

    res = {}
    got = dict(zip(("w_out", "xa_wq", "xa_wk", "xa_wv", "xa_wo"), got_a), w_in=got_win, s5_glu_w=got_glu)
    for n in ("w_in",) + _ROW_NAMES:
        res[n] = _adamw("adamw_" + n, got[n], w[n][0], mom[n][0], var[n][0])
    small_sum = _allreduce_small(small_pack)
    small_res, loss_sum = _adamw_small(small_sum, _pack_small(w), _pack_small(mom), _pack_small(var))
    loss = (0.5 / D_MODEL) * jnp.sum(loss_sum)
    res.update(small_res)

    outs = [loss, grad_x[None]]
    for part in range(4):
        for n in _W_NAMES:
            outs.append(res[n][part].reshape(shapes[n]))
    return tuple(outs)
```

```python
import jax
import jax.numpy as jnp
from jax import lax
from jax.experimental import pallas as pl
from jax.experimental.pallas import tpu as pltpu

F32 = jnp.float32
BF16 = jnp.bfloat16
MESH = pl.DeviceIdType.MESH

D_MODEL = 1024
RET_HEADS, RET_DK, RET_DV = 8, 64, 128
RET_QK = RET_HEADS * RET_DK
S5_G, S5_N, S5_P = 64, 64, 16
S5_NB = 8
S5_GB = S5_G // S5_NB
S5_BS = S5_GB * S5_N
S5_COLS = 2 * S5_G * S5_N
XA_HEADS, XA_DH = 4, 256
EPS = 1e-6
ROPE_BASE = 10000.0
N_DEV = 8
W_IN_SHARD = 640
ROW_SHARDS = (128, 256, 128, 128, 128, 128)
ROWPACK = sum(ROW_SHARDS)
SMALL_ROWS = 320
ADAM_LR, ADAM_B1, ADAM_B2, ADAM_EPS, ADAM_WD, ADAM_STEP = 0.001, 0.9, 0.999, 1e-08, 0.01, 10

VMEM_LIMIT = 56 * 1024 * 1024


def _cp(*sem):
    return pltpu.CompilerParams(dimension_semantics=tuple(sem), vmem_limit_bytes=VMEM_LIMIT)


def _dot(a, b):
    return jnp.dot(a, b, preferred_element_type=F32)


def _dot_nt(a, b):
    return lax.dot_general(a, b, (((1,), (1,)), ((), ())), preferred_element_type=F32)


def _dot_tn(a, b):
    return lax.dot_general(a, b, (((0,), (0,)), ((), ())), preferred_element_type=F32)


def _sigmoid(x):
    return 1.0 / (1.0 + jnp.exp(-x))


def _silu(x):
    return x * _sigmoid(x)


def _dsilu(x):
    s = _sigmoid(x)
    return s * (1.0 + x * (1.0 - s))


_GELU_C = 0.7978845608028654


def _gelu(x):
    return 0.5 * x * (1.0 + jnp.tanh(_GELU_C * (x + 0.044715 * (x * x * x))))


def _gelu_and_grad(x):
    t = jnp.tanh(_GELU_C * (x + 0.044715 * (x * x * x)))
    half = 0.5 * (1.0 + t)
    return x * half, half + 0.5 * x * (1.0 - t * t) * (_GELU_C * (1.0 + 3.0 * 0.044715 * (x * x)))


def _pick(n, cands):
    for c in cands:
        if n % c == 0:
            return c
    return n


class _Epilogue:
    def __init__(self, rows, vecs, row_out_dtypes, n_sums, fn):
        self.rows, self.vecs, self.row_out_dtypes, self.n_sums, self.fn = list(rows), list(vecs), list(row_out_dtypes), n_sums, fn


def _rms(x):
    rs = lax.rsqrt(jnp.mean(x * x, axis=-1, keepdims=True) + EPS)
    return rs, x * rs


def _rms_dx(dn, xn, rs):
    return rs * (dn - xn * jnp.mean(dn * xn, axis=-1, keepdims=True))


def _epi_norm_fwd(g):
    def fn(r, rows, vecs):
        return r, [_rms(r)[1] * vecs[0]], []

    return _Epilogue([], [g], [BF16], 0, fn)


def _epi_loss(gf, target):
    def fn(r, rows, vecs):
        rs, xn = _rms(r)
        e = xn * vecs[0] - rows[0]
        dy = e * (1.0 / r.shape[-1])
        return (_rms_dx(dy * vecs[0], xn, rs), [],
                [jnp.sum(dy * xn, axis=0, keepdims=True), jnp.sum(e * e, axis=0, keepdims=True)])

    return _Epilogue([target], [gf], [], 2, fn)


def _epi_norm_bwd(x, g, dres):
    def fn(r, rows, vecs):
        rs, xn = _rms(rows[0])
        return _rms_dx(r * vecs[0], xn, rs) + rows[1], [], [jnp.sum(r * xn, axis=0, keepdims=True)]

    return _Epilogue([x, dres], [g], [], 1, fn)


def _mm_core(name, operands, in_specs, out_spec, out_shape, grid, nk, dims, acc_shape, has_res, side=None, epi=None):
    n_in = 3 if has_res else 2
    n_epi_in = len(epi.rows) + len(epi.vecs) if epi else 0
    n_epi_out = len(epi.row_out_dtypes) + epi.n_sums if epi else 0
    n_side_in = len(side.srcs) if side else 0
    n_side_out = side.n if side else 0

    def body(*refs):
        a_ref, b_ref = refs[0], refs[1]
        r_ref = refs[2] if has_res else None
        epi_in = refs[n_in:n_in + n_epi_in]
        side_in = refs[n_in + n_epi_in:n_in + n_epi_in + n_side_in]
        n0 = n_in + n_epi_in + n_side_in
        o_ref = refs[n0]
        epi_out = refs[n0 + 1:n0 + 1 + n_epi_out]
        side_out = refs[n0 + 1 + n_epi_out:n0 + 1 + n_epi_out + n_side_out]
        rest = refs[n0 + 1 + n_epi_out + n_side_out:]
        acc, sems = (rest[0], rest[1:]) if nk > 1 else (None, rest)
        i, j, k = pl.program_id(0), pl.program_id(1), pl.program_id(2)
        if side:
            @pl.when((i == 0) & (j == 0) & (k == 0))
            def _():
                side.start(side_in, side_out, sems)

        def product():
            if len(b_ref.shape) == 3:
                ns = b_ref.shape[2]
                return sum(lax.dot_general(a_ref[:, p * ns:(p + 1) * ns].astype(BF16), b_ref[p].astype(BF16),
                                           (dims, ((), ())), preferred_element_type=F32)
                           for p in range(b_ref.shape[0]))
            return lax.dot_general(a_ref[...].astype(BF16), b_ref[...].astype(BF16), (dims, ((), ())),
                                   preferred_element_type=F32)

        def finish(r):
            if has_res:
                r = r + r_ref[...]
            if epi is None:
                o_ref[...] = r.astype(o_ref.dtype)
                return
            n_rows = len(epi.rows)
            main, row_vals, sums = epi.fn(r, [t[...] for t in epi_in[:n_rows]], [t[...] for t in epi_in[n_rows:]])
            o_ref[...] = main.astype(o_ref.dtype)
            for ref, val in zip(epi_out, row_vals):
                ref[...] = val.astype(ref.dtype)
            for ref, val in zip(epi_out[len(row_vals):], sums):
                @pl.when(i == 0)
                def _(ref=ref):
                    ref[...] = jnp.zeros_like(ref)

                ref[...] += val

        if nk == 1:
            finish(product())
        else:
            @pl.when(k == 0)
            def _():
                acc[...] = jnp.zeros_like(acc)

            acc[...] += product()

            @pl.when(k == nk - 1)
            def _():
                finish(acc[...])

        if side:
            @pl.when((i == grid[0] - 1) & (j == grid[1] - 1) & (k == grid[2] - 1))
            def _():
                side.wait(side_in, side_out, sems)

    acc_scratch = [pltpu.VMEM(acc_shape, F32)] if nk > 1 else []
    in_specs, out_specs, out_shapes, operands = list(in_specs), [out_spec], [out_shape], list(operands)
    if epi:
        assert grid[1] == 1, "an epilogue needs tiles that span whole rows"
        tm, n = out_spec.block_shape
        row_spec = pl.BlockSpec((tm, n), lambda i, j, k: (i, 0))
        vec_spec = pl.BlockSpec((1, n), lambda i, j, k: (0, 0))
        in_specs += [row_spec] * len(epi.rows) + [vec_spec] * len(epi.vecs)
        operands += epi.rows + epi.vecs
        out_specs += [row_spec] * len(epi.row_out_dtypes) + [vec_spec] * epi.n_sums
        out_shapes += [jax.ShapeDtypeStruct(out_shape.shape, d) for d in epi.row_out_dtypes]
        out_shapes += [jax.ShapeDtypeStruct((1, n), F32)] * epi.n_sums
    scratch = acc_scratch
    if side:
        in_specs += side.in_specs
        operands += side.srcs
        out_specs += side.out_specs
        out_shapes += side.landing
        scratch = acc_scratch + side.scratch
    plain = side is None and epi is None
    res = pl.pallas_call(
        body, name=name, grid=grid, in_specs=in_specs, out_specs=out_specs[0] if plain else tuple(out_specs),
        out_shape=out_shapes[0] if plain else tuple(out_shapes), scratch_shapes=scratch,
        compiler_params=_cp("parallel", "parallel", "arbitrary") if plain else _cp("arbitrary", "arbitrary", "arbitrary"),
    )(*operands)
    return res


def _mm_nn(name, a, b, out_dtype, residual=None, epi=None):
    m, kk = a.shape
    n = b.shape[1]
    tm, tn, tk = _pick(m, (1024, 512, 256)), _pick(n, (1024, 512)), _pick(kk, (2048, 1024, 512))
    ops = [a, b]
    specs = [pl.BlockSpec((tm, tk), lambda i, j, k: (i, k)), pl.BlockSpec((tk, tn), lambda i, j, k: (k, j))]
    if residual is not None:
        ops.append(residual)
        specs.append(pl.BlockSpec((tm, tn), lambda i, j, k: (i, j)))
    return _mm_core(name, ops, specs, pl.BlockSpec((tm, tn), lambda i, j, k: (i, j)),
                    jax.ShapeDtypeStruct((m, n), out_dtype), (m // tm, n // tn, kk // tk), kk // tk,
                    ((1,), (0,)), (tm, tn), residual is not None, epi=epi)


def _mm_nt(name, a, b, out_dtype, residual=None, epi=None):
    m, kk = a.shape
    n = b.shape[0]
    tm, tn, tk = _pick(m, (1024, 512, 256)), _pick(n, (2048, 1024, 512)), _pick(kk, (2048, 1024, 512))
    ops = [a, b]
    specs = [pl.BlockSpec((tm, tk), lambda i, j, k: (i, k)), pl.BlockSpec((tn, tk), lambda i, j, k: (j, k))]
    if residual is not None:
        ops.append(residual)
        specs.append(pl.BlockSpec((tm, tn), lambda i, j, k: (i, j)))
    return _mm_core(name, ops, specs, pl.BlockSpec((tm, tn), lambda i, j, k: (i, j)),
                    jax.ShapeDtypeStruct((m, n), out_dtype), (m // tm, n // tn, kk // tk), kk // tk,
                    ((1,), (1,)), (tm, tn), residual is not None, epi=epi)


def _mm_tn(name, a, b, out_dtype):
    kk, m = a.shape
    n = b.shape[1]
    tm, tn, tk = _pick(m, (1024, 512)), _pick(n, (1024, 512)), _pick(kk, (2048, 1024, 512, 256))
    specs = [pl.BlockSpec((tk, tm), lambda i, j, k: (k, i)), pl.BlockSpec((tk, tn), lambda i, j, k: (k, j))]
    return _mm_core(name, [a, b], specs, pl.BlockSpec((tm, tn), lambda i, j, k: (i, j)),
                    jax.ShapeDtypeStruct((m, n), out_dtype), (m // tm, n // tn, kk // tk), kk // tk,
                    ((0,), (0,)), (tm, tn), False)


def _mm_nn_slots(name, a, b_slots, out_dtype, side=None):
    m, kk = a.shape
    s, _, ns = b_slots.shape
    tm, tk = _pick(m, (4096, 2048, 1024, 512, 256)), _pick(kk, (1024, 512))
    specs = [pl.BlockSpec((tm, tk), lambda i, j, k: (i, k)), pl.BlockSpec((None, tk, ns), lambda i, j, k: (j, k, 0))]
    return _mm_core(name, [a, b_slots], specs, pl.BlockSpec((tm, ns), lambda i, j, k: (i, j)),
                    jax.ShapeDtypeStruct((m, s * ns), out_dtype), (m // tm, s, kk // tk), kk // tk,
                    ((1,), (0,)), (tm, ns), False, side)


def _mm_nt_slots(name, a, b_slots, out_dtype, side=None, epi=None):
    m = a.shape[0]
    s, n, ns = b_slots.shape
    tm, tn = _pick(m, (1024, 512, 256)), _pick(n, (1024, 512))
    per = _pick(s, (2, 1))
    specs = [pl.BlockSpec((tm, per * ns), lambda i, j, k: (i, k)),
             pl.BlockSpec((per, tn, ns), lambda i, j, k: (k, j, 0))]
    return _mm_core(name, [a, b_slots], specs, pl.BlockSpec((tm, tn), lambda i, j, k: (i, j)),
                    jax.ShapeDtypeStruct((m, n), out_dtype), (m // tm, n // tn, s // per), s // per,
                    ((1,), (1,)), (tm, tn), False, side, epi)


def _mm_tn_slots(name, a, b, s, out_dtype, side=None):
    kk, m = a.shape
    ns = b.shape[1] // s
    tm, tk = _pick(m, (1024, 512)), _pick(kk, (4096, 2048, 1024, 512, 256))
    specs = [pl.BlockSpec((tk, tm), lambda i, j, k: (k, i)), pl.BlockSpec((tk, ns), lambda i, j, k: (k, j))]
    return _mm_core(name, [a, b], specs, pl.BlockSpec((None, tm, ns), lambda i, j, k: (j, i, 0)),
                    jax.ShapeDtypeStruct((s, m, ns), out_dtype), (m // tm, s, kk // tk), kk // tk,
                    ((0,), (0,)), (tm, ns), False, side)


def _rms_fwd(name, x, g):
    r, d = x.shape
    tr = _pick(r, (1024, 512, 256))

    def body(x_ref, g_ref, o_ref):
        xv = x_ref[...]
        rs = lax.rsqrt(jnp.mean(xv * xv, axis=-1, keepdims=True) + EPS)
        o_ref[...] = (xv * rs * g_ref[...]).astype(o_ref.dtype)

    return pl.pallas_call(
        body, name=name, grid=(r // tr,),
        in_specs=[pl.BlockSpec((tr, d), lambda i: (i, 0)), pl.BlockSpec((1, d), lambda i: (0, 0))],
        out_specs=pl.BlockSpec((tr, d), lambda i: (i, 0)),
        out_shape=jax.ShapeDtypeStruct((r, d), BF16), compiler_params=_cp("parallel"),
    )(x, g)


def _rms_bwd(name, x, g, dh, dres):
    r, d = x.shape
    tr = _pick(r, (512, 256))
    has_res = dres is not None

    def body(*refs):
        if has_res:
            x_ref, g_ref, dh_ref, dr_ref, dx_ref, dg_ref = refs
        else:
            x_ref, g_ref, dh_ref, dx_ref, dg_ref = refs
        i = pl.program_id(0)

        @pl.when(i == 0)
        def _():
            dg_ref[...] = jnp.zeros_like(dg_ref)

        xv = x_ref[...]
        dhv = dh_ref[...].astype(F32)
        rs = lax.rsqrt(jnp.mean(xv * xv, axis=-1, keepdims=True) + EPS)
        xn = xv * rs
        dg_ref[...] += jnp.sum(dhv * xn, axis=0, keepdims=True)
        dn = dhv * g_ref[...]
        dx = rs * (dn - xn * jnp.mean(dn * xn, axis=-1, keepdims=True))
        if has_res:
            dx = dx + dr_ref[...]
        dx_ref[...] = dx

    row = pl.BlockSpec((tr, d), lambda i: (i, 0))
    vec = pl.BlockSpec((1, d), lambda i: (0, 0))
    ops = [x, g, dh] + ([dres] if has_res else [])
    return pl.pallas_call(
        body, name=name, grid=(r // tr,),
        in_specs=[row, vec, row] + ([row] if has_res else []),
        out_specs=(row, vec),
        out_shape=(jax.ShapeDtypeStruct((r, d), F32), jax.ShapeDtypeStruct((1, d), F32)),
        compiler_params=_cp("arbitrary"),
    )(*ops)


def _rot(x, cos_t, sin_t):
    n = x.shape[-1]
    lane = lax.broadcasted_iota(jnp.int32, x.shape, 1)
    partner = jnp.where((lane % RET_DK) < RET_DK // 2, pltpu.roll(x, n - RET_DK // 2, 1), pltpu.roll(x, RET_DK // 2, 1))
    return x * cos_t + partner * sin_t


def _ret_constants(c):
    log_g = jnp.log1p(-jnp.exp2(-5.0 - jnp.arange(RET_HEADS, dtype=F32)))
    j = jnp.arange(c, dtype=F32)
    diff = j[:, None] - j[None, :]
    decay = jnp.where(diff[None] >= 0.0, jnp.exp(log_g[:, None, None] * jnp.maximum(diff, 0.0)[None]), 0.0)
    q_w = jnp.exp(log_g[None, :] * (j + 1.0)[:, None])
    k_w = jnp.exp(log_g[None, :] * (c - 1.0 - j)[:, None])
    cd = jnp.exp(log_g * c)
    rep = lambda t: jnp.repeat(t, RET_DK, axis=1)
    cd_row = jnp.repeat(cd, RET_DV)[None, :]
    return decay, rep(q_w), rep(k_w), cd_row


def _pair_of(h, c):
    lane = lax.broadcasted_iota(jnp.int32, (c, 2 * RET_DK), 1)
    mine = (lane < RET_DK) if h % 2 == 0 else (lane >= RET_DK)
    return slice((h // 2) * 2 * RET_DK, (h // 2 + 1) * 2 * RET_DK), mine


def _keep(x, mine):
    return jnp.where(mine, x, jnp.zeros_like(x))


def _ret_fwd(proj, cos_t, sin_t, consts, gn_g, c):
    l = proj.shape[0]
    nc = l // c
    decay, qw, kw, cd_row = consts

    def body(q_ref, k_ref, v_ref, g_ref, cos_ref, sin_ref, dec_ref, qw_ref, kw_ref, cd_ref, gn_ref,
             ret_ref, o_ref, rp_ref, qb_ref, kb_ref, state):
        @pl.when(pl.program_id(0) == 0)
        def _():
            state[...] = jnp.zeros_like(state)

        cs, sn = cos_ref[...], sin_ref[...]
        qr = _rot(q_ref[...].astype(F32), cs, sn)
        kr = _rot(k_ref[...].astype(F32), cs, sn) * (RET_DK ** -0.5)
        qb, kb = qr.astype(BF16), kr.astype(BF16)
        qb_ref[...] = qb
        kb_ref[...] = kb
        qwb = (qr * qw_ref[...]).astype(BF16)
        kwb = (kr * kw_ref[...]).astype(BF16)
        vb = v_ref[...].astype(BF16)
        for h in range(RET_HEADS):
            ps, mine = _pair_of(h, c)
            vs = slice(h * RET_DV, (h + 1) * RET_DV)
            s = _dot_nt(_keep(qb[:, ps], mine), kb[:, ps]) * dec_ref[h]
            r_prev = state[h]
            rp_ref[0, h] = r_prev
            o = _dot(s.astype(BF16), vb[:, vs]) + _dot(_keep(qwb[:, ps], mine), r_prev.astype(BF16))
            state[h] = cd_ref[:, vs] * r_prev + _dot_tn(_keep(kwb[:, ps], mine), vb[:, vs])
            o_ref[:, vs] = o
            mu = jnp.mean(o, axis=-1, keepdims=True)
            var = jnp.mean(jnp.square(o - mu), axis=-1, keepdims=True)
            on = (o - mu) * lax.rsqrt(var + EPS)
            ret_ref[:, vs] = (on * gn_ref[:, vs] * _silu(g_ref[:, vs].astype(F32))).astype(ret_ref.dtype)

    const2 = lambda shape: pl.BlockSpec(shape, lambda i: (0,) * len(shape))
    return pl.pallas_call(
        body, name="retention_fwd", grid=(nc,),
        in_specs=[pl.BlockSpec((c, RET_QK), lambda i: (i, 0)), pl.BlockSpec((c, RET_QK), lambda i: (i, 1)),
                  pl.BlockSpec((c, D_MODEL), lambda i: (i, 1)), pl.BlockSpec((c, D_MODEL), lambda i: (i, 2)),
                  pl.BlockSpec((c, RET_QK), lambda i: (i, 0)), pl.BlockSpec((c, RET_QK), lambda i: (i, 0)),
                  const2((RET_HEADS, c, c)), const2((c, RET_QK)), const2((c, RET_QK)), const2((1, D_MODEL)),
                  const2((1, D_MODEL))],
        out_specs=(pl.BlockSpec((c, D_MODEL), lambda i: (i, 0)), pl.BlockSpec((c, D_MODEL), lambda i: (i, 0)),
                   pl.BlockSpec((1, RET_HEADS, 2 * RET_DK, RET_DV), lambda i: (i, 0, 0, 0)),
                   pl.BlockSpec((c, RET_QK), lambda i: (i, 0)), pl.BlockSpec((c, RET_QK), lambda i: (i, 0))),
        out_shape=(jax.ShapeDtypeStruct((l, 2 * D_MODEL), BF16), jax.ShapeDtypeStruct((l, D_MODEL), F32),
                   jax.ShapeDtypeStruct((nc, RET_HEADS, 2 * RET_DK, RET_DV), F32),
                   jax.ShapeDtypeStruct((l, RET_QK), BF16), jax.ShapeDtypeStruct((l, RET_QK), BF16)),
        scratch_shapes=[pltpu.VMEM((RET_HEADS, 2 * RET_DK, RET_DV), F32)],
        compiler_params=_cp("arbitrary"),
    )(proj, proj, proj, proj, cos_t, sin_t, decay, qw, kw, cd_row, gn_g)


def _ret_bwd(proj, qb_saved, kb_saved, cos_t, sin_t, consts, gn_g, o_saved, r_prev_saved, dmix, c, side):
    l = proj.shape[0]
    nc = l // c
    decay, qw, kw, cd_row = consts
    n_in = 14

    def body(*refs):
        (q_ref, k_ref, v_ref, g_ref, cos_ref, sin_ref, dec_ref, qw_ref, kw_ref, cd_ref, gn_ref, o_ref, rp_ref,
         dr_ref) = refs[:n_in]
        side_in = refs[n_in:n_in + len(side.srcs)]
        out_ref, dgn_ref = refs[n_in + len(side.srcs):n_in + len(side.srcs) + 2]
        side_out = refs[n_in + len(side.srcs) + 2:n_in + len(side.srcs) + 2 + side.n]
        state, dq_s, dk_s = refs[n_in + len(side.srcs) + 2 + side.n:n_in + len(side.srcs) + 5 + side.n]
        sems = refs[n_in + len(side.srcs) + 5 + side.n:]

        @pl.when(pl.program_id(0) == 0)
        def _():
            side.start(side_in, side_out, sems)
            state[...] = jnp.zeros_like(state)
            dgn_ref[...] = jnp.zeros_like(dgn_ref)

        cs, sn = cos_ref[...], sin_ref[...]
        qb, kb = q_ref[...], k_ref[...]
        qwv, kwv = qw_ref[...], kw_ref[...]
        qwb = (qb.astype(F32) * qwv).astype(BF16)
        kwb = (kb.astype(F32) * kwv).astype(BF16)
        vb = v_ref[...].astype(BF16)
        dq2 = dk2 = None
        for h in range(RET_HEADS):
            ps, mine = _pair_of(h, c)
            vs = slice(h * RET_DV, (h + 1) * RET_DV)
            dec = dec_ref[h]
            qm, km = _keep(qb[:, ps], mine), _keep(kb[:, ps], mine)
            o = o_ref[:, vs]
            mu = jnp.mean(o, axis=-1, keepdims=True)
            var = jnp.mean(jnp.square(o - mu), axis=-1, keepdims=True)
            rstd = lax.rsqrt(var + EPS)
            on = (o - mu) * rstd
            gate = g_ref[:, vs].astype(F32)
            sg = _silu(gate)
            dret = dr_ref[:, vs].astype(F32)
            gn = gn_ref[:, vs]
            dgn_ref[:, vs] += jnp.sum(dret * on * sg, axis=0, keepdims=True)
            out_ref[:, 2 * RET_QK + D_MODEL + h * RET_DV:2 * RET_QK + D_MODEL + (h + 1) * RET_DV] = (
                dret * on * gn * _dsilu(gate)).astype(out_ref.dtype)
            don = dret * gn * sg
            do = rstd * (don - jnp.mean(don, axis=-1, keepdims=True)
                         - on * jnp.mean(don * on, axis=-1, keepdims=True))
            dob = do.astype(BF16)
            sn_h = state[h]
            snb = sn_h.astype(BF16)
            s = _dot_nt(qm, kb[:, ps]) * dec
            dv = _dot_tn(s.astype(BF16), dob) + _dot(_keep(kwb[:, ps], mine), snb)
            out_ref[:, 2 * RET_QK + h * RET_DV:2 * RET_QK + (h + 1) * RET_DV] = dv.astype(out_ref.dtype)
            ds = (_dot_nt(dob, vb[:, vs]) * dec).astype(BF16)
            dq_h = _dot(ds, km) + qwv[:, ps] * _dot_nt(dob, rp_ref[0, h].astype(BF16))
            dk_h = _dot_tn(ds, qm) + kwv[:, ps] * _dot_nt(vb[:, vs], snb)
            state[h] = cd_ref[:, vs] * sn_h + _dot_tn(_keep(qwb[:, ps], mine), dob)
            if h % 2 == 0:
                dq2, dk2 = dq_h, dk_h
            else:
                dq_s[:, ps] = dq2 + dq_h
                dk_s[:, ps] = dk2 + dk_h
        out_ref[:, 0:RET_QK] = _rot(dq_s[...], cs, -sn).astype(out_ref.dtype)
        out_ref[:, RET_QK:2 * RET_QK] = (_rot(dk_s[...], cs, -sn) * (RET_DK ** -0.5)).astype(out_ref.dtype)

        @pl.when(pl.program_id(0) == nc - 1)
        def _():
            side.wait(side_in, side_out, sems)

    rev = lambda i: nc - 1 - i
    const2 = lambda shape: pl.BlockSpec(shape, lambda i: (0,) * len(shape))
    return pl.pallas_call(
        body, name="retention_bwd", grid=(nc,),
        in_specs=[pl.BlockSpec((c, RET_QK), lambda i: (rev(i), 0)), pl.BlockSpec((c, RET_QK), lambda i: (rev(i), 0)),
                  pl.BlockSpec((c, D_MODEL), lambda i: (rev(i), 1)), pl.BlockSpec((c, D_MODEL), lambda i: (rev(i), 2)),
                  pl.BlockSpec((c, RET_QK), lambda i: (rev(i), 0)), pl.BlockSpec((c, RET_QK), lambda i: (rev(i), 0)),
                  const2((RET_HEADS, c, c)), const2((c, RET_QK)), const2((c, RET_QK)), const2((1, D_MODEL)),
                  const2((1, D_MODEL)),
                  pl.BlockSpec((c, D_MODEL), lambda i: (rev(i), 0)),
                  pl.BlockSpec((1, RET_HEADS, 2 * RET_DK, RET_DV), lambda i: (rev(i), 0, 0, 0)),
                  pl.BlockSpec((c, D_MODEL), lambda i: (rev(i), 0))] + side.in_specs,
        out_specs=(pl.BlockSpec((c, 2 * RET_QK + 2 * D_MODEL), lambda i: (rev(i), 0)), const2((1, D_MODEL)),
                   *side.out_specs),
        out_shape=(jax.ShapeDtypeStruct((l, 2 * RET_QK + 4 * D_MODEL), BF16), jax.ShapeDtypeStruct((1, D_MODEL), F32),
                   *side.landing),
        scratch_shapes=[pltpu.VMEM((RET_HEADS, 2 * RET_DK, RET_DV), F32), pltpu.VMEM((c, RET_QK), F32),
                        pltpu.VMEM((c, RET_QK), F32)] + side.scratch,
        compiler_params=_cp("arbitrary"),
    )(qb_saved, kb_saved, proj, proj, cos_t, sin_t, decay, qw, kw, cd_row, gn_g, o_saved, r_prev_saved, dmix,
      *side.srcs)


def _zoh(a_re, a_im, log_dt):
    dt = jnp.exp(log_dt)
    mag = jnp.exp(a_re * dt)
    abar_re = mag * jnp.cos(a_im * dt)
    abar_im = mag * jnp.sin(a_im * dt)
    den = a_re * a_re + a_im * a_im
    nr, ni = abar_re - 1.0, abar_im
    f_re = (nr * a_re + ni * a_im) / den
    f_im = (ni * a_re - nr * a_im) / den
    return dt, abar_re, abar_im, f_re, f_im, den


def _lanes_p(f):
    return jnp.tile(f, (1, S5_P))


def _s5_discretize(a_re, a_im, log_dt, b_re_t, b_im_t):
    def body(ar_ref, ai_ref, ld_ref, br_ref, bi_ref, abr_ref, abi_ref, bbr_ref, bbi_ref):
        _, abar_re, abar_im, f_re, f_im, _ = _zoh(ar_ref[...], ai_ref[...], ld_ref[...])
        abr_ref[...] = abar_re
        abi_ref[...] = abar_im
        fr, fi = _lanes_p(f_re), _lanes_p(f_im)
        bbr_ref[...] = fr * br_ref[...] - fi * bi_ref[...]
        bbi_ref[...] = fr * bi_ref[...] + fi * br_ref[...]

    gn = jax.ShapeDtypeStruct((S5_G, S5_N), F32)
    gpn = jax.ShapeDtypeStruct((S5_G, S5_P * S5_N), F32)
    return pl.pallas_call(body, name="s5_discretize", out_shape=(gn, gn, gpn, gpn))(a_re, a_im, log_dt, b_re_t, b_im_t)


def _s5_discretize_bwd(a_re, a_im, log_dt, b_re_t, b_im_t, dab_re, dab_im, dbb_re_t, dbb_im_t):
    def body(ar_ref, ai_ref, ld_ref, br_ref, bi_ref, gar_ref, gai_ref, gbr_ref, gbi_ref,
             dar_ref, dai_ref, dld_ref, dbr_ref, dbi_ref):
        a_r, a_i = ar_ref[...], ai_ref[...]
        dt, abar_re, abar_im, f_re, f_im, den = _zoh(a_r, a_i, ld_ref[...])
        b_r, b_i, g_br, g_bi = br_ref[...], bi_ref[...], gbr_ref[...], gbi_ref[...]
        fr, fi = _lanes_p(f_re), _lanes_p(f_im)
        dbr_ref[...] = fr * g_br + fi * g_bi
        dbi_ref[...] = fr * g_bi - fi * g_br
        t_r = b_r * g_br + b_i * g_bi
        t_i = b_r * g_bi - b_i * g_br
        gf_r = sum(t_r[:, p * S5_N:(p + 1) * S5_N] for p in range(S5_P))
        gf_i = sum(t_i[:, p * S5_N:(p + 1) * S5_N] for p in range(S5_P))
        inv_r, inv_i = a_r / den, a_i / den
        ga_r = gar_ref[...] + gf_r * inv_r - gf_i * inv_i
        ga_i = gai_ref[...] + gf_r * inv_i + gf_i * inv_r
        q_r = -(f_re * a_r + f_im * a_i) / den
        q_i = -(f_im * a_r - f_re * a_i) / den
        gl_r = q_r * gf_r + q_i * gf_i
        gl_i = q_r * gf_i - q_i * gf_r
        dar_ref[...] = gl_r + dt * (abar_re * ga_r + abar_im * ga_i)
        dai_ref[...] = gl_i + dt * (abar_re * ga_i - abar_im * ga_r)
        la_r = a_r * abar_re - a_i * abar_im
        la_i = a_r * abar_im + a_i * abar_re
        dld_ref[...] = dt * jnp.sum(ga_r * la_r + ga_i * la_i, axis=-1, keepdims=True)

    gn = jax.ShapeDtypeStruct((S5_G, S5_N), F32)
    gpn = jax.ShapeDtypeStruct((S5_G, S5_P * S5_N), F32)
    return pl.pallas_call(
        body, name="s5_discretize_bwd", out_shape=(gn, gn, jax.ShapeDtypeStruct((S5_G, 1), F32), gpn, gpn),
    )(a_re, a_im, log_dt, b_re_t, b_im_t, dab_re, dab_im, dbb_re_t, dbb_im_t)


S5_ZQ = S5_NB // 2


def _s5_z(re, im):
    return jnp.concatenate([re.reshape(S5_ZQ, 8, 128), im.reshape(S5_ZQ, 8, 128)], axis=0)


def _s5_unz(z):
    return z[:S5_ZQ].reshape(S5_G, S5_N), z[S5_ZQ:].reshape(S5_G, S5_N)


def _s5_block_mats(bb_re, bb_im, c_re, c_im):
    eye = jnp.eye(S5_GB, dtype=F32)
    bb = jnp.stack([bb_re, bb_im], axis=0).reshape(2, S5_NB, S5_GB, S5_N, S5_P)
    bbm = jnp.einsum("rbgnp,gh->bgprhn", bb, eye).reshape(S5_NB, S5_GB * S5_P, 2 * S5_BS)
    cc = jnp.stack([c_re, -c_im], axis=0).reshape(2, S5_NB, S5_GB, S5_P, S5_N)
    ccm = jnp.einsum("rbgpn,gh->brhngp", cc, eye).reshape(S5_NB, 2 * S5_BS, S5_GB * S5_P)
    return bbm.astype(BF16), ccm.astype(BF16)


def _s5_block_diag_bb(m):
    t = m.reshape(S5_NB, S5_GB, S5_P, 2, S5_GB, S5_N)
    d = jnp.einsum("bgprgn->rbgnp", t).reshape(2, S5_G, S5_N, S5_P)
    return d[0], d[1]


def _s5_block_diag_cc(m):
    t = m.reshape(S5_NB, 2, S5_GB, S5_N, S5_GB, S5_P)
    d = jnp.einsum("brgngp->rbgpn", t).reshape(2, S5_G, S5_P, S5_N)
    return d[0], -d[1]


SCAN_UNROLL = 8


def _z_store(zr, zi, blk, res, t, off):
    q, h = blk // 2, blk % 2
    for lt in range(4):
        zr[q, pl.ds(off + 4 * h + lt, t, stride=8), :] = res[:, lt * 128:(lt + 1) * 128]
        zi[q, pl.ds(off + 4 * h + lt, t, stride=8), :] = res[:, S5_BS + lt * 128:S5_BS + (lt + 1) * 128]


def _z_load(zr, zi, blk, t, off):
    q, h = blk // 2, blk % 2
    return jnp.concatenate([zr[q, pl.ds(off + 4 * h + lt, t, stride=8), :] for lt in range(4)]
                           + [zi[q, pl.ds(off + 4 * h + lt, t, stride=8), :] for lt in range(4)], axis=1)


def _z_scan_fwd(zr, zi, a_ref, carry_ref, t, off):
    ar = [a_ref[q] for q in range(S5_ZQ)]
    ai = [a_ref[S5_ZQ + q] for q in range(S5_ZQ)]

    def step(it, carry):
        carry = list(carry)
        base = pl.multiple_of(it * (8 * SCAN_UNROLL), 8 * SCAN_UNROLL) + off
        for tt in range(SCAN_UNROLL):
            rows = pl.ds(base + 8 * tt, 8)
            for q in range(S5_ZQ):
                c_r, c_i = carry[q], carry[S5_ZQ + q]
                n_r = ar[q] * c_r - ai[q] * c_i + zr[q, rows, :]
                n_i = ar[q] * c_i + ai[q] * c_r + zi[q, rows, :]
                zr[q, rows, :] = n_r
                zi[q, rows, :] = n_i
                carry[q], carry[S5_ZQ + q] = n_r, n_i
        return tuple(carry)

    out = lax.fori_loop(0, t // SCAN_UNROLL, step, tuple(carry_ref[k] for k in range(2 * S5_ZQ)))
    for k in range(2 * S5_ZQ):
        carry_ref[k] = out[k]


def _z_scan_bwd(lr, li, xr, xi, a_ref, carry_ref, acc_ref, t):
    ar = [a_ref[q] for q in range(S5_ZQ)]
    ai = [a_ref[S5_ZQ + q] for q in range(S5_ZQ)]
    n_it = t // SCAN_UNROLL

    def step(it, state):
        carry, acc = list(state[0]), list(state[1])
        base = pl.multiple_of((n_it - 1 - it) * (8 * SCAN_UNROLL), 8 * SCAN_UNROLL)
        for tt in reversed(range(SCAN_UNROLL)):
            rows = pl.ds(base + 8 * tt, 8)
            for q in range(S5_ZQ):
                c_r, c_i = carry[q], carry[S5_ZQ + q]
                n_r = ar[q] * c_r + ai[q] * c_i + lr[q, rows, :]
                n_i = ar[q] * c_i - ai[q] * c_r + li[q, rows, :]
                lr[q, rows, :] = n_r
                li[q, rows, :] = n_i
                p_r, p_i = xr[q, rows, :], xi[q, rows, :]
                acc[q] = acc[q] + n_r * p_r + n_i * p_i
                acc[S5_ZQ + q] = acc[S5_ZQ + q] + n_i * p_r - n_r * p_i
                carry[q], carry[S5_ZQ + q] = n_r, n_i
        return tuple(carry), tuple(acc)

    k8 = range(2 * S5_ZQ)
    carry, acc = lax.fori_loop(0, n_it, step, (tuple(carry_ref[k] for k in k8), tuple(acc_ref[k] for k in k8)))
    for k in k8:
        carry_ref[k] = carry[k]
        acc_ref[k] = acc[k]


def _s5_fwd(proj, mix, bbm, ccm, d_row, glu_w, glu_b, tabs, t, side):
    l = proj.shape[0]
    nt = l // t
    n_in = 9

    def body(*refs):
        u_ref, gs_ref, bb_ref, cc_ref, d_ref, gw_ref, gb_ref, a_ref, _ = refs[:n_in]
        side_in = refs[n_in:n_in + len(side.srcs)]
        ssm_ref, xst_ref, y1_ref, z_ref = refs[n_in + len(side.srcs):n_in + len(side.srcs) + 4]
        side_out = refs[n_in + len(side.srcs) + 4:n_in + len(side.srcs) + 4 + side.n]
        zr, zi, carry = refs[n_in + len(side.srcs) + 4 + side.n:n_in + len(side.srcs) + 7 + side.n]
        sems = refs[n_in + len(side.srcs) + 7 + side.n:]

        @pl.when(pl.program_id(0) == 0)
        def _():
            side.start(side_in, side_out, sems)
            carry[...] = jnp.zeros_like(carry)

        xst_ref[0] = carry[...]
        ub = u_ref[...]
        u = ub.astype(F32)
        for blk in range(S5_NB):
            _z_store(zr, zi, blk, _dot(ub[:, blk * 128:(blk + 1) * 128], bb_ref[blk]), t, 0)
        _z_scan_fwd(zr, zi, a_ref, carry, t, 0)
        ys = jnp.concatenate(
            [_dot(_z_load(zr, zi, blk, t, 0).astype(BF16), cc_ref[blk]) for blk in range(S5_NB)], axis=1)
        y1 = ys + d_ref[...] * u
        y1_ref[...] = y1.astype(y1_ref.dtype)
        y2 = _gelu(y1)
        z = _dot(y2.astype(BF16), gw_ref[...]) + gb_ref[...]
        z_ref[...] = z.astype(z_ref.dtype)
        ssm_ref[...] = (y2 * _sigmoid(z) * _silu(gs_ref[...].astype(F32))).astype(ssm_ref.dtype)

        @pl.when(pl.program_id(0) == nt - 1)
        def _():
            side.wait(side_in, side_out, sems)

    const2 = lambda shape: pl.BlockSpec(shape, lambda i: (0,) * len(shape))
    zshape = (2 * S5_ZQ, 8, 128)
    return pl.pallas_call(
        body, name="s5_fwd", grid=(nt,),
        in_specs=[pl.BlockSpec((t, D_MODEL), lambda i: (i, 3)), pl.BlockSpec((t, D_MODEL), lambda i: (i, 4)),
                  const2(bbm.shape), const2(ccm.shape), const2((1, D_MODEL)), const2((D_MODEL, D_MODEL)),
                  const2((1, D_MODEL)), const2(zshape), pl.BlockSpec(memory_space=pl.ANY)] + side.in_specs,
        out_specs=(pl.BlockSpec((t, D_MODEL), lambda i: (i, 1)), pl.BlockSpec((1,) + zshape, lambda i: (i, 0, 0, 0)),
                   pl.BlockSpec((t, D_MODEL), lambda i: (i, 0)), pl.BlockSpec((t, D_MODEL), lambda i: (i, 0)),
                   *side.out_specs),
        out_shape=(jax.ShapeDtypeStruct((l, 2 * D_MODEL), BF16), jax.ShapeDtypeStruct((nt,) + zshape, F32),
                   jax.ShapeDtypeStruct((l, D_MODEL), BF16), jax.ShapeDtypeStruct((l, D_MODEL), BF16), *side.landing),
        scratch_shapes=[pltpu.VMEM((S5_ZQ, 8 * t, 128), F32), pltpu.VMEM((S5_ZQ, 8 * t, 128), F32),
                        pltpu.VMEM(zshape, F32)] + side.scratch,
        input_output_aliases={8: 0},
        compiler_params=_cp("arbitrary"),
    )(proj, proj, bbm, ccm, d_row, glu_w, glu_b, tabs, mix, *side.srcs)


def _s5_bwd(proj, dmix, dproj, xstart, y1, z, bbm, ccm, d_row, glu_w, tabs, t):
    l = proj.shape[0]
    nt = l // t
    col0 = 2 * RET_QK + 2 * D_MODEL

    def body(u_ref, gs_ref, dm_ref, xst_ref, bb_ref, cc_ref, d_ref, gw_ref, a_ref, _, y1_ref, z_ref,
             dp_ref, y2_ref, dz_ref, dbb_ref, dcc_ref, da_ref, dd_ref, dgb_ref, xr, xi, lr, li, carry, lcarry,
             dug_s, dug_sem):
        step = pl.program_id(0)
        slot = step % 2
        dug_ref = dug_s.at[slot]

        def put(s, at_step):
            rows = pl.ds(pl.multiple_of((nt - 1 - at_step) * t, t), t)
            return pltpu.make_async_copy(dug_s.at[s], dp_ref.at[rows, pl.ds(col0, 2 * D_MODEL)], dug_sem.at[s])

        @pl.when(step >= 2)
        def _():
            put(slot, step - 2).wait()

        @pl.when(step == 0)
        def _():
            lcarry[...] = jnp.zeros_like(lcarry)
            dbb_ref[...] = jnp.zeros_like(dbb_ref)
            dcc_ref[...] = jnp.zeros_like(dcc_ref)
            da_ref[...] = jnp.zeros_like(da_ref)
            dd_ref[...] = jnp.zeros_like(dd_ref)
            dgb_ref[...] = jnp.zeros_like(dgb_ref)

        carry[...] = xst_ref[0]
        for q in range(S5_ZQ):
            xr[q, 0:8, :] = carry[q]
            xi[q, 0:8, :] = carry[S5_ZQ + q]
        ub = u_ref[...]
        u = ub.astype(F32)
        for blk in range(S5_NB):
            _z_store(xr, xi, blk, _dot(ub[:, blk * 128:(blk + 1) * 128], bb_ref[blk]), t, 8)
        _z_scan_fwd(xr, xi, a_ref, carry, t, 8)
        dv = d_ref[...]
        y2, dgelu = _gelu_and_grad(y1_ref[...].astype(F32))
        y2b = y2.astype(BF16)
        sg = _sigmoid(z_ref[...].astype(F32))
        gs = gs_ref[...].astype(F32)
        dssm = dm_ref[...].astype(F32)
        dug_ref[:, D_MODEL:] = (dssm * (y2 * sg) * _dsilu(gs)).astype(dug_ref.dtype)
        dy3 = dssm * _silu(gs)
        dz = dy3 * y2 * sg * (1.0 - sg)
        dzb = dz.astype(BF16)
        y2_ref[...] = y2b
        dz_ref[...] = dzb
        dgb_ref[...] += jnp.sum(dz, axis=0, keepdims=True)
        dy1 = (dy3 * sg + _dot_nt(dzb, gw_ref[...])) * dgelu
        dd_ref[...] += jnp.sum(dy1 * u, axis=0, keepdims=True)
        dyb = dy1.astype(BF16)
        for blk in range(S5_NB):
            ch = slice(blk * 128, (blk + 1) * 128)
            _z_store(lr, li, blk, _dot_nt(dyb[:, ch], cc_ref[blk]), t, 0)
            dcc_ref[blk] += _dot_tn(_z_load(xr, xi, blk, t, 8).astype(BF16), dyb[:, ch])
        _z_scan_bwd(lr, li, xr, xi, a_ref, lcarry, da_ref, t)
        du = []
        for blk in range(S5_NB):
            lb = _z_load(lr, li, blk, t, 0).astype(BF16)
            du.append(_dot_nt(lb, bb_ref[blk]))
            dbb_ref[blk] += _dot_tn(ub[:, blk * 128:(blk + 1) * 128], lb)
        dug_ref[:, :D_MODEL] = (jnp.concatenate(du, axis=1) + dy1 * dv).astype(dug_ref.dtype)
        put(slot, step).start()

        @pl.when(step == nt - 1)
        def _():
            put(slot, step).wait()
            if nt > 1:
                put(1 - slot, step - 1).wait()

    rev = lambda i: nt - 1 - i
    const2 = lambda shape: pl.BlockSpec(shape, lambda i: (0,) * len(shape))
    row_out = lambda w: pl.BlockSpec((t, w), lambda i: (rev(i), 0))
    zshape = (2 * S5_ZQ, 8, 128)
    hbm = pl.BlockSpec(memory_space=pl.ANY)
    return pl.pallas_call(
        body, name="s5_bwd", grid=(nt,),
        in_specs=[pl.BlockSpec((t, D_MODEL), lambda i: (rev(i), 3)), pl.BlockSpec((t, D_MODEL), lambda i: (rev(i), 4)),
                  pl.BlockSpec((t, D_MODEL), lambda i: (rev(i), 1)),
                  pl.BlockSpec((1,) + zshape, lambda i: (rev(i), 0, 0, 0)),
                  const2(bbm.shape), const2(ccm.shape), const2((1, D_MODEL)), const2((D_MODEL, D_MODEL)),
                  const2(zshape), hbm, pl.BlockSpec((t, D_MODEL), lambda i: (rev(i), 0)),
                  pl.BlockSpec((t, D_MODEL), lambda i: (rev(i), 0))],
        out_specs=(hbm, row_out(D_MODEL), row_out(D_MODEL), const2(bbm.shape), const2(ccm.shape),
                   const2(zshape), const2((1, D_MODEL)), const2((1, D_MODEL))),
        out_shape=(jax.ShapeDtypeStruct(dproj.shape, BF16), jax.ShapeDtypeStruct((l, D_MODEL), BF16),
                   jax.ShapeDtypeStruct((l, D_MODEL), BF16), jax.ShapeDtypeStruct(bbm.shape, F32),
                   jax.ShapeDtypeStruct(ccm.shape, F32), jax.ShapeDtypeStruct(zshape, F32),
                   jax.ShapeDtypeStruct((1, D_MODEL), F32), jax.ShapeDtypeStruct((1, D_MODEL), F32)),
        scratch_shapes=[pltpu.VMEM((S5_ZQ, 8 * t + 8, 128), F32), pltpu.VMEM((S5_ZQ, 8 * t + 8, 128), F32),
                        pltpu.VMEM((S5_ZQ, 8 * t, 128), F32), pltpu.VMEM((S5_ZQ, 8 * t, 128), F32),
                        pltpu.VMEM(zshape, F32), pltpu.VMEM(zshape, F32),
                        pltpu.VMEM((2, t, 2 * D_MODEL), BF16), pltpu.SemaphoreType.DMA((2,))],
        input_output_aliases={9: 0},
        compiler_params=_cp("arbitrary"),
    )(proj, proj, dmix, xstart, bbm, ccm, d_row, glu_w, tabs, dproj, y1, z)


def _attn_probs(qh, kh):
    s = _dot_nt(qh, kh) * (XA_DH ** -0.5)
    e = jnp.exp(s - jnp.max(s, axis=-1, keepdims=True))
    return e / jnp.sum(e, axis=-1, keepdims=True)


def _attn_fwd(qa, ka, va):
    l = qa.shape[0]
    m = ka.shape[0]
    tl = _pick(l, (2048, 1024, 512, 256))

    def body(q_ref, k_ref, v_ref, o_ref):
        for h in range(XA_HEADS):
            hs = slice(h * XA_DH, (h + 1) * XA_DH)
            p = _attn_probs(q_ref[:, hs], k_ref[:, hs])
            o_ref[:, hs] = _dot(p.astype(BF16), v_ref[:, hs]).astype(o_ref.dtype)

    return pl.pallas_call(
        body, name="xattn_fwd", grid=(l // tl,),
        in_specs=[pl.BlockSpec((tl, D_MODEL), lambda i: (i, 0)), pl.BlockSpec((m, D_MODEL), lambda i: (0, 0)),
                  pl.BlockSpec((m, D_MODEL), lambda i: (0, 0))],
        out_specs=pl.BlockSpec((tl, D_MODEL), lambda i: (i, 0)),
        out_shape=jax.ShapeDtypeStruct((l, D_MODEL), BF16), compiler_params=_cp("parallel"),
    )(qa, ka, va)


def _attn_bwd(qa, ka, va, doa):
    l = qa.shape[0]
    m = ka.shape[0]
    tl = _pick(l, (2048, 1024, 512, 256))

    def body(q_ref, k_ref, v_ref, do_ref, dq_ref, dk_ref, dv_ref):
        @pl.when(pl.program_id(0) == 0)
        def _():
            dk_ref[...] = jnp.zeros_like(dk_ref)
            dv_ref[...] = jnp.zeros_like(dv_ref)

        for h in range(XA_HEADS):
            hs = slice(h * XA_DH, (h + 1) * XA_DH)
            qh, kh, vh, doh = q_ref[:, hs], k_ref[:, hs], v_ref[:, hs], do_ref[:, hs]
            p = _attn_probs(qh, kh)
            dv_ref[:, hs] += _dot_tn(p.astype(BF16), doh)
            dp = _dot_nt(doh, vh)
            ds = (p * (dp - jnp.sum(dp * p, axis=-1, keepdims=True)) * (XA_DH ** -0.5)).astype(BF16)
            dq_ref[:, hs] = _dot(ds, kh).astype(dq_ref.dtype)
            dk_ref[:, hs] += _dot_tn(ds, qh)

    row = pl.BlockSpec((tl, D_MODEL), lambda i: (i, 0))
    mem = pl.BlockSpec((m, D_MODEL), lambda i: (0, 0))
    return pl.pallas_call(
        body, name="xattn_bwd", grid=(l // tl,), in_specs=[row, mem, mem, row], out_specs=(row, mem, mem),
        out_shape=(jax.ShapeDtypeStruct((l, D_MODEL), BF16), jax.ShapeDtypeStruct((m, D_MODEL), F32),
                   jax.ShapeDtypeStruct((m, D_MODEL), F32)),
        compiler_params=_cp("arbitrary"),
    )(qa, ka, va, doa)


def _me_and_peers():
    x, y, c = lax.axis_index("x"), lax.axis_index("y"), lax.axis_index("c")
    flip = lambda v, bit: (1 - v) if bit else v
    peers = []
    for k in range(1, N_DEV):
        px, py, pc = flip(x, (k >> 2) & 1), flip(y, (k >> 1) & 1), flip(c, k & 1)
        peers.append(((px, py, pc), 4 * px + 2 * py + pc))
    return 4 * x + 2 * y + c, peers


class _SideJob:
    def __init__(self, srcs, landing, src_of, dst_of):
        self.srcs = list(srcs)
        self.landing = list(landing)
        self.n = len(self.landing)
        self.src_of, self.dst_of = src_of, dst_of
        hbm = pl.BlockSpec(memory_space=pl.ANY)
        self.in_specs = [hbm] * len(self.srcs)
        self.out_specs = [hbm] * self.n
        self.scratch = [pltpu.SemaphoreType.DMA((self.n * (N_DEV - 1),)), pltpu.SemaphoreType.DMA((self.n * (N_DEV - 1),)),
                        pltpu.SemaphoreType.DMA((self.n,))]

    def _copies(self, src_refs, out_refs, sems):
        send_sems, recv_sems, loc_sems = sems
        me, peers = _me_and_peers()
        local = [pltpu.make_async_copy(self.src_of(a, me, src_refs), self.dst_of(a, me, out_refs), loc_sems.at[a])
                 for a in range(self.n)]
        sends, recvs = [], []
        for k, (peer, peer_idx) in enumerate(peers):
            for a in range(self.n):
                s = self.n * k + a
                sends.append(pltpu.make_async_remote_copy(
                    src_ref=self.src_of(a, peer_idx, src_refs), dst_ref=self.dst_of(a, me, out_refs),
                    send_sem=send_sems.at[s], recv_sem=recv_sems.at[s], device_id=peer, device_id_type=MESH))
                recvs.append(pltpu.make_async_remote_copy(
                    src_ref=self.src_of(a, me, src_refs), dst_ref=self.dst_of(a, peer_idx, out_refs),
                    send_sem=send_sems.at[s], recv_sem=recv_sems.at[s], device_id=peer, device_id_type=MESH))
        return local, sends, recvs

    def start(self, src_refs, out_refs, sems):
        if not self.n:
            return
        local, sends, _ = self._copies(src_refs, out_refs, sems)
        for cp in local + sends:
            cp.start()

    def wait(self, src_refs, out_refs, sems):
        if not self.n:
            return
        local, sends, recvs = self._copies(src_refs, out_refs, sems)
        for cp in recvs:
            cp.wait_recv()
        for cp in sends:
            cp.wait_send()
        for cp in local:
            cp.wait()


def _gather_job(shards):
    return _SideJob(shards, [jax.ShapeDtypeStruct((N_DEV,) + s.shape, s.dtype) for s in shards],
                    src_of=lambda a, j, srcs: srcs[a], dst_of=lambda a, j, outs: outs[a].at[j])


def _scatter_job(grads):
    landing, parts = [], []
    for g in grads:
        if g.ndim == 3:
            landing.append(jax.ShapeDtypeStruct(g.shape, g.dtype))
            parts.append(None)
        else:
            r = g.shape[0] // N_DEV
            landing.append(jax.ShapeDtypeStruct((N_DEV, r, g.shape[1]), g.dtype))
            parts.append(r)

    def src_of(a, j, srcs):
        if parts[a] is None:
            return srcs[a].at[j]
        return srcs[a].at[pl.ds(pl.multiple_of(j * parts[a], 8), parts[a]), :]

    return _SideJob(grads, landing, src_of=src_of, dst_of=lambda a, j, outs: outs[a].at[j])


def _prologue(w_in_shard, row_shards, x, g, pos_col, inv_row):
    n_row = len(row_shards)
    l, d = x.shape
    tr = _pick(l, (1024, 512, 256))
    nt = l // tr
    mid = nt - 1

    def body(*refs):
        win_ref = refs[0]
        row_refs = refs[1:1 + n_row]
        x_ref, g_ref, p_ref, inv_ref = refs[1 + n_row:5 + n_row]
        out_win = refs[5 + n_row]
        row_outs = refs[6 + n_row:6 + 2 * n_row]
        h_ref, cos_ref, sin_ref = refs[6 + 2 * n_row:9 + 2 * n_row]
        win_b, send_sems, recv_sems, local_sem = refs[9 + 2 * n_row:]
        step = pl.program_id(0)
        cx, cy, cc = lax.axis_index("x"), lax.axis_index("y"), lax.axis_index("c")
        me, sibling = (cx, cy, cc), (cx, cy, 1 - cc)
        chips = [(1 - cx, cy), (cx, 1 - cy), (1 - cx, 1 - cy)]
        slot = lambda p: out_win.at[4 * p[0] + 2 * p[1] + p[2]]

        def copy(k, block, to, src=None):
            return pltpu.make_async_remote_copy(
                src_ref=slot(block) if src is None else src, dst_ref=slot(block), send_sem=send_sems.at[k],
                recv_sem=recv_sems.at[k], device_id=to, device_id_type=MESH)

        mine = pltpu.make_async_copy(win_b, slot(me), local_sem)
        first = [copy(0, me, sibling, src=win_b)]
        first += [copy(1 + j, me, (*chip, cc), src=win_b) for j, chip in enumerate(chips)]
        passed = [copy(4 + j, (*chip, cc), sibling) for j, chip in enumerate(chips)]

        @pl.when(step == 0)
        def _():
            win_b[...] = win_ref[...].astype(BF16)
            mine.start()
            for cp in first:
                cp.start()
            for r, o in zip(row_refs, row_outs):
                o[...] = r[...].astype(BF16)

        h_ref[...] = (_rms(x_ref[...])[1] * g_ref[...]).astype(h_ref.dtype)
        ang = p_ref[...].astype(F32) * inv_ref[...]
        lane = lax.broadcasted_iota(jnp.int32, ang.shape, 1)
        cos_ref[...] = jnp.tile(jnp.cos(ang), (1, RET_QK // 128))
        sin_ref[...] = jnp.tile(jnp.where((lane % RET_DK) < RET_DK // 2, -jnp.sin(ang), jnp.sin(ang)),
                                (1, RET_QK // 128))

        @pl.when(step == mid)
        def _():
            for j, chip in enumerate(chips):
                copy(1 + j, (*chip, cc), me).wait_recv()
                passed[j].start()

        @pl.when(step == nt - 1)
        def _():
            copy(0, sibling, me).wait_recv()
            for j, chip in enumerate(chips):
                copy(4 + j, (*chip, 1 - cc), me).wait_recv()
            for cp in first + passed:
                cp.wait_send()
            mine.wait()

    whole = lambda a: pl.BlockSpec(a.shape, lambda i: (0,) * a.ndim)
    rows = lambda w: pl.BlockSpec((tr, w), lambda i: (i, 0))
    return pl.pallas_call(
        body, name="prologue_allgather_w_in", grid=(nt,),
        in_specs=[whole(w_in_shard)] + [whole(r) for r in row_shards] + [rows(d), whole(g), rows(1), whole(inv_row)],
        out_specs=(pl.BlockSpec(memory_space=pl.ANY), *[whole(r) for r in row_shards], rows(d), rows(RET_QK),
                   rows(RET_QK)),
        out_shape=(jax.ShapeDtypeStruct((N_DEV,) + w_in_shard.shape, BF16),
                   *[jax.ShapeDtypeStruct(r.shape, BF16) for r in row_shards],
                   jax.ShapeDtypeStruct((l, d), BF16), jax.ShapeDtypeStruct((l, RET_QK), F32),
                   jax.ShapeDtypeStruct((l, RET_QK), F32)),
        scratch_shapes=[pltpu.VMEM(w_in_shard.shape, BF16), pltpu.SemaphoreType.DMA((N_DEV - 1,)),
                        pltpu.SemaphoreType.DMA((N_DEV - 1,)), pltpu.SemaphoreType.DMA],
        compiler_params=_cp("arbitrary"),
    )(w_in_shard, *row_shards, x, g, pos_col, inv_row)


def _allreduce_small(small):
    rows = SMALL_ROWS // N_DEV

    def body(x_ref, out_ref, land, send1, recv1, send2, recv2):
        me, peers = _me_and_peers()
        block = lambda j: pl.ds(pl.multiple_of(j * rows, 8), rows)

        def phase(src_of, dst_of, send_sems, recv_sems):
            sends = [pltpu.make_async_remote_copy(src_ref=src_of(pidx), dst_ref=dst_of(me), send_sem=send_sems.at[k],
                                                  recv_sem=recv_sems.at[k], device_id=peer, device_id_type=MESH)
                     for k, (peer, pidx) in enumerate(peers)]
            recvs = [pltpu.make_async_remote_copy(src_ref=src_of(me), dst_ref=dst_of(pidx), send_sem=send_sems.at[k],
                                                  recv_sem=recv_sems.at[k], device_id=peer, device_id_type=MESH)
                     for k, (peer, pidx) in enumerate(peers)]
            for cp in sends:
                cp.start()
            for cp in recvs:
                cp.wait_recv()
            for cp in sends:
                cp.wait_send()

        land[me] = x_ref[block(me), :]
        phase(lambda j: x_ref.at[block(j), :], lambda j: land.at[j], send1, recv1)
        total = land[0]
        for j in range(1, N_DEV):
            total = total + land[j]
        out_ref[block(me), :] = total
        phase(lambda j: out_ref.at[block(me), :], lambda j: out_ref.at[block(j), :], send2, recv2)

    vm = pl.BlockSpec(memory_space=pltpu.VMEM)
    return pl.pallas_call(
        body, name="allreduce_small", in_specs=[vm], out_specs=vm, out_shape=jax.ShapeDtypeStruct(small.shape, F32),
        scratch_shapes=[pltpu.VMEM((N_DEV, rows, D_MODEL), F32)] + [pltpu.SemaphoreType.DMA((N_DEV - 1,))] * 4,
    )(small)


def _adamw(name, got, w, m, v):
    r, c = w.shape
    n_slots = got.shape[0]
    tr = _pick(r, (256, 128, 64))

    def body(got_ref, w_ref, m_ref, v_ref, g_ref, d_ref, nm_ref, nv_ref):
        g = got_ref[0].astype(F32)
        for j in range(1, n_slots):
            g = g + got_ref[j].astype(F32)
        nm = ADAM_B1 * m_ref[...] + (1.0 - ADAM_B1) * g
        nv = ADAM_B2 * v_ref[...] + (1.0 - ADAM_B2) * jnp.square(g)
        m_hat = nm / (1.0 - ADAM_B1 ** ADAM_STEP)
        v_hat = nv / (1.0 - ADAM_B2 ** ADAM_STEP)
        g_ref[...] = g
        d_ref[...] = -ADAM_LR * (m_hat / (jnp.sqrt(v_hat) + ADAM_EPS) + ADAM_WD * w_ref[...])
        nm_ref[...] = nm
        nv_ref[...] = nv

    blk = pl.BlockSpec((tr, c), lambda i: (i, 0))
    out = jax.ShapeDtypeStruct((r, c), F32)
    return pl.pallas_call(
        body, name=name, grid=(r // tr,),
        in_specs=[pl.BlockSpec((n_slots, tr, c), lambda i: (0, i, 0)), blk, blk, blk],
        out_specs=(blk, blk, blk, blk), out_shape=(out, out, out, out), compiler_params=_cp("parallel"),
    )(got, w, m, v)


_SMALL_VECS = ("norm1_g", "ret_gn_g", "s5_d", "s5_glu_b", "norm2_g", "norm_mem_g", "norm_f_g")


def _small_layout():
    lay, row = {}, 0
    for n in _SMALL_VECS + ("loss",):
        lay[n] = (row, 1, D_MODEL)
        row += 1
    for n in ("s5_a_re", "s5_a_im"):
        lay[n] = (row, 4, D_MODEL)
        row += 4
    lay["s5_log_dt"] = (row, 1, S5_G)
    row += 8
    for n in ("s5_b_re", "s5_b_im", "s5_c_re", "s5_c_im"):
        lay[n] = (row, 64, D_MODEL)
        row += 64
    assert row <= SMALL_ROWS
    return lay


def _pack_small(t, loss_row=None):
    lay = _small_layout()
    pieces = [t[n].reshape(1, D_MODEL) for n in _SMALL_VECS]
    pieces.append(jnp.zeros((1, D_MODEL), F32) if loss_row is None else loss_row)
    pieces += [t["s5_a_re"].reshape(4, D_MODEL), t["s5_a_im"].reshape(4, D_MODEL)]
    pieces.append(jnp.pad(t["s5_log_dt"].reshape(1, S5_G), ((0, 7), (0, D_MODEL - S5_G))))
    pieces += [t[n].reshape(64, D_MODEL) for n in ("s5_b_re", "s5_b_im", "s5_c_re", "s5_c_im")]
    pieces.append(jnp.zeros((SMALL_ROWS - lay["s5_c_im"][0] - 64, D_MODEL), F32))
    return jnp.concatenate(pieces, axis=0)


def _adamw_small(g_sum, w, m, v):
    lay = _small_layout()
    names = [n for n in lay if n != "loss"]

    def body(g_ref, w_ref, m_ref, v_ref, *outs):
        g = g_ref[...]
        nm = ADAM_B1 * m_ref[...] + (1.0 - ADAM_B1) * g
        nv = ADAM_B2 * v_ref[...] + (1.0 - ADAM_B2) * jnp.square(g)
        m_hat = nm / (1.0 - ADAM_B1 ** ADAM_STEP)
        v_hat = nv / (1.0 - ADAM_B2 ** ADAM_STEP)
        delta = -ADAM_LR * (m_hat / (jnp.sqrt(v_hat) + ADAM_EPS) + ADAM_WD * w_ref[...])
        for i, n in enumerate(names):
            r0, rows, lanes = lay[n]
            for part, val in enumerate((g, delta, nm, nv)):
                outs[4 * i + part][...] = val[r0:r0 + rows, 0:lanes]
        r0 = lay["loss"][0]
        outs[-1][...] = g[r0:r0 + 1, :]

    shapes = []
    for n in names:
        shapes += [jax.ShapeDtypeStruct(lay[n][1:], F32)] * 4
    shapes.append(jax.ShapeDtypeStruct((1, D_MODEL), F32))
    outs = pl.pallas_call(body, name="adamw_small", out_shape=tuple(shapes),
                          compiler_params=pltpu.CompilerParams(vmem_limit_bytes=VMEM_LIMIT))(g_sum, w, m, v)
    return {n: tuple(outs[4 * i:4 * i + 4]) for i, n in enumerate(names)}, outs[-1]


_W_NAMES = ("norm1_g", "w_in", "ret_gn_g", "s5_a_re", "s5_a_im", "s5_log_dt", "s5_b_re", "s5_b_im", "s5_c_re", "s5_c_im",
            "s5_d", "s5_glu_w", "s5_glu_b", "w_out", "norm2_g", "norm_mem_g", "xa_wq", "xa_wk", "xa_wv", "xa_wo",
            "norm_f_g")
_ROW_NAMES = ("s5_glu_w", "w_out", "xa_wq", "xa_wk", "xa_wv", "xa_wo")


def kernel(x, mem, positions, norm1_g, w_in, ret_gn_g, s5_a_re, s5_a_im, s5_log_dt, s5_b_re, s5_b_im, s5_c_re, s5_c_im, s5_d, s5_glu_w, s5_glu_b, w_out, norm2_g, norm_mem_g, xa_wq, xa_wk, xa_wv, xa_wo, norm_f_g, loss_target, m_norm1_g, m_w_in, m_ret_gn_g, m_s5_a_re, m_s5_a_im, m_s5_log_dt, m_s5_b_re, m_s5_b_im, m_s5_c_re, m_s5_c_im, m_s5_d, m_s5_glu_w, m_s5_glu_b, m_w_out, m_norm2_g, m_norm_mem_g, m_xa_wq, m_xa_wk, m_xa_wv, m_xa_wo, m_norm_f_g, v_norm1_g, v_w_in, v_ret_gn_g, v_s5_a_re, v_s5_a_im, v_s5_log_dt, v_s5_b_re, v_s5_b_im, v_s5_c_re, v_s5_c_im, v_s5_d, v_s5_glu_w, v_s5_glu_b, v_w_out, v_norm2_g, v_norm_mem_g, v_xa_wq, v_xa_wk, v_xa_wv, v_xa_wo, v_norm_f_g):
    w = dict(norm1_g=norm1_g, w_in=w_in, ret_gn_g=ret_gn_g, s5_a_re=s5_a_re, s5_a_im=s5_a_im, s5_log_dt=s5_log_dt,
             s5_b_re=s5_b_re, s5_b_im=s5_b_im, s5_c_re=s5_c_re, s5_c_im=s5_c_im, s5_d=s5_d, s5_glu_w=s5_glu_w,
             s5_glu_b=s5_glu_b, w_out=w_out, norm2_g=norm2_g, norm_mem_g=norm_mem_g, xa_wq=xa_wq, xa_wk=xa_wk,
             xa_wv=xa_wv, xa_wo=xa_wo, norm_f_g=norm_f_g)
    mom = dict(norm1_g=m_norm1_g, w_in=m_w_in, ret_gn_g=m_ret_gn_g, s5_a_re=m_s5_a_re, s5_a_im=m_s5_a_im,
               s5_log_dt=m_s5_log_dt, s5_b_re=m_s5_b_re, s5_b_im=m_s5_b_im, s5_c_re=m_s5_c_re, s5_c_im=m_s5_c_im,
               s5_d=m_s5_d, s5_glu_w=m_s5_glu_w, s5_glu_b=m_s5_glu_b, w_out=m_w_out, norm2_g=m_norm2_g,
               norm_mem_g=m_norm_mem_g, xa_wq=m_xa_wq, xa_wk=m_xa_wk, xa_wv=m_xa_wv, xa_wo=m_xa_wo,
               norm_f_g=m_norm_f_g)
    var = dict(norm1_g=v_norm1_g, w_in=v_w_in, ret_gn_g=v_ret_gn_g, s5_a_re=v_s5_a_re, s5_a_im=v_s5_a_im,
               s5_log_dt=v_s5_log_dt, s5_b_re=v_s5_b_re, s5_b_im=v_s5_b_im, s5_c_re=v_s5_c_re, s5_c_im=v_s5_c_im,
               s5_d=v_s5_d, s5_glu_w=v_s5_glu_w, s5_glu_b=v_s5_glu_b, w_out=v_w_out, norm2_g=v_norm2_g,
               norm_mem_g=v_norm_mem_g, xa_wq=v_xa_wq, xa_wk=v_xa_wk, xa_wv=v_xa_wv, xa_wo=v_xa_wo,
               norm_f_g=v_norm_f_g)
    shapes = {n: w[n].shape for n in _W_NAMES}

    x2d, mem2d, tgt = x[0], mem[0], loss_target[0]
    l = x2d.shape[0]
    ret_c = _pick(l, (256, 128))
    s5_t = _pick(l, (256, 128))
    g1, g2, gm, gf = norm1_g, norm2_g, norm_mem_g, norm_f_g.reshape(1, D_MODEL)

    half = RET_DK // 2
    inv = ROPE_BASE ** (-jnp.arange(half, dtype=F32) / half)
    win_s, *rest = _prologue(w_in[0], [w[n][0] for n in _ROW_NAMES], x2d, g1, positions[0].reshape(l, 1),
                             jnp.tile(inv, 128 // half)[None, :])
    row_shards_b, (h1, cos_t, sin_t) = rest[:len(_ROW_NAMES)], rest[len(_ROW_NAMES):]

    to_gpn = lambda b: jnp.transpose(b, (0, 2, 1)).reshape(S5_G, S5_P * S5_N)
    from_gpn = lambda b: jnp.transpose(b.reshape(S5_G, S5_P, S5_N), (0, 2, 1))
    disc_args = (s5_a_re[0], s5_a_im[0], s5_log_dt[0].reshape(S5_G, 1), to_gpn(s5_b_re[0]), to_gpn(s5_b_im[0]))
    abar_re, abar_im, bb_re_t, bb_im_t = _s5_discretize(*disc_args)
    bbm, ccm = _s5_block_mats(from_gpn(bb_re_t), from_gpn(bb_im_t), s5_c_re[0], s5_c_im[0])
    a_z = _s5_z(abar_re, abar_im)

    proj, *rows_01 = _mm_nn_slots("in_proj", h1, win_s, BF16, side=_gather_job(row_shards_b[:2]))
    full = {n: g.reshape(N_DEV * r, D_MODEL) for n, g, r in zip(_ROW_NAMES[:2], rows_01, ROW_SHARDS[:2])}
    rconsts = _ret_constants(ret_c)
    ret, o_saved, r_prev, q_rot, k_rot = _ret_fwd(proj, cos_t, sin_t, rconsts, ret_gn_g, ret_c)
    mix, xstart, y1, z_glu, *rows_xa = _s5_fwd(proj, ret, bbm, ccm, s5_d, full["s5_glu_w"], s5_glu_b, a_z, s5_t,
                                        side=_gather_job(row_shards_b[2:]))
    full.update({n: g.reshape(N_DEV * r, D_MODEL) for n, g, r in zip(_ROW_NAMES[2:], rows_xa, ROW_SHARDS[2:])})
    x1, h2 = _mm_nn("out_proj", mix, full["w_out"], F32, residual=x2d, epi=_epi_norm_fwd(g2))
    mn = _rms_fwd("norm_mem_fwd", mem2d, gm)
    qa = _mm_nn("xa_q", h2, full["xa_wq"], BF16)
    ka = _mm_nn("xa_k", mn, full["xa_wk"], BF16)
    va = _mm_nn("xa_v", mn, full["xa_wv"], BF16)
    oa = _attn_fwd(qa, ka, va)
    dx2, dgf, loss_lanes = _mm_nn("xa_o", oa, full["xa_wo"], F32, residual=x1, epi=_epi_loss(gf, tgt))

    doa = _mm_nt("xa_o_dx", dx2, full["xa_wo"], BF16)
    dwo = _mm_tn("xa_o_dw", oa, dx2, BF16)
    dqa, dka, dva = _attn_bwd(qa, ka, va, doa)
    dx1, dg2 = _mm_nt("xa_q_dx", dqa, full["xa_wq"], F32, epi=_epi_norm_bwd(x1, g2, dx2))
    dwq = _mm_tn("xa_q_dw", h2, dqa, BF16)
    dwk = _mm_tn("xa_k_dw", mn, dka, BF16)
    dwv = _mm_tn("xa_v_dw", mn, dva, BF16)
    dmn = _mm_nt("xa_v_dx", dva, full["xa_wv"], F32, residual=_mm_nt("xa_k_dx", dka, full["xa_wk"], F32))
    _, dgm = _rms_bwd("norm_mem_bwd", mem2d, gm, dmn, None)
    dmix = _mm_nt("out_proj_dx", dx1, full["w_out"], BF16)
    dwout = _mm_tn("out_proj_dw", mix, dx1, BF16)
    dret, dgn, *got_a = _ret_bwd(proj, q_rot, k_rot, cos_t, sin_t, rconsts, ret_gn_g, o_saved, r_prev, dmix, ret_c,
                                 side=_scatter_job([dwout, dwq, dwk, dwv, dwo]))
    dproj, y2, dz, dbbm, dccm, dabar, dd, dgb = _s5_bwd(proj, dmix, dret, xstart, y1, z_glu, bbm, ccm, s5_d,
                                                        full["s5_glu_w"], a_z, s5_t)
    dglu = _mm_tn("s5_glu_dw", y2, dz, BF16)
    dwin_s, got_glu = _mm_tn_slots("in_proj_dw", h1, dproj, N_DEV, BF16, side=_scatter_job([dglu]))
    grad_x, dg1, got_win = _mm_nt_slots("in_proj_dx", dproj, win_s, F32, side=_scatter_job([dwin_s]),
                                        epi=_epi_norm_bwd(x2d, g1, dx1))

    dab_re, dab_im = _s5_unz(dabar)
    dbb_re, dbb_im = _s5_block_diag_bb(dbbm)
    dc_re, dc_im = _s5_block_diag_cc(dccm)
    da_re, da_im, dlog_dt, db_re_t, db_im_t = _s5_discretize_bwd(*disc_args, dab_re, dab_im, to_gpn(dbb_re),
                                                                 to_gpn(dbb_im))
    db_re, db_im = from_gpn(db_re_t), from_gpn(db_im_t)
    small_g = dict(norm1_g=dg1, ret_gn_g=dgn, s5_d=dd, s5_glu_b=dgb, norm2_g=dg2, norm_mem_g=dgm, norm_f_g=dgf,
                   s5_a_re=da_re, s5_a_im=da_im, s5_log_dt=dlog_dt, s5_b_re=db_re, s5_b_im=db_im, s5_c_re=dc_re,
                   s5_c_im=dc_im)
    small_pack = _pack_small(small_g, loss_row=loss_lanes)

    res = {}
    got = dict(zip(("w_out", "xa_wq", "xa_wk", "xa_wv", "xa_wo"), got_a), w_in=got_win, s5_glu_w=got_glu)
    for n in ("w_in",) + _ROW_NAMES:
        res[n] = _adamw("adamw_" + n, got[n], w[n][0], mom[n][0], var[n][0])
    small_sum = _allreduce_small(small_pack)
    small_res, loss_sum = _adamw_small(small_sum, _pack_small(w), _pack_small(mom), _pack_small(var))
    loss = (0.5 / D_MODEL) * jnp.sum(loss_sum)
    res.update(small_res)

    outs = [loss, grad_x[None]]
    for part in range(4):
        for n in _W_NAMES:
            outs.append(res[n][part].reshape(shapes[n]))
    return tuple(outs)
```

```python
import jax
import jax.numpy as jnp
from jax import lax
from jax.experimental import pallas as pl
from jax.experimental.pallas import tpu as pltpu

F32 = jnp.float32
BF16 = jnp.bfloat16
MESH = pl.DeviceIdType.MESH

D_MODEL = 1024
RET_HEADS, RET_DK, RET_DV = 8, 64, 128
RET_QK = RET_HEADS * RET_DK
S5_G, S5_N, S5_P = 64, 64, 16
S5_NB = 8
S5_GB = S5_G // S5_NB
S5_BS = S5_GB * S5_N
S5_COLS = 2 * S5_G * S5_N
XA_HEADS, XA_DH = 4, 256
EPS = 1e-6
ROPE_BASE = 10000.0
N_DEV = 8
W_IN_SHARD = 640
ROW_SHARDS = (128, 256, 128, 128, 128, 128)
ROWPACK = sum(ROW_SHARDS)
SMALL_ROWS = 320
ADAM_LR, ADAM_B1, ADAM_B2, ADAM_EPS, ADAM_WD, ADAM_STEP = 0.001, 0.9, 0.999, 1e-08, 0.01, 10

VMEM_LIMIT = 56 * 1024 * 1024


def _cp(*sem):
    return pltpu.CompilerParams(dimension_semantics=tuple(sem), vmem_limit_bytes=VMEM_LIMIT)


def _dot(a, b):
    return jnp.dot(a, b, preferred_element_type=F32)


def _dot_nt(a, b):
    return lax.dot_general(a, b, (((1,), (1,)), ((), ())), preferred_element_type=F32)


def _dot_tn(a, b):
    return lax.dot_general(a, b, (((0,), (0,)), ((), ())), preferred_element_type=F32)


def _sigmoid(x):
    return 1.0 / (1.0 + jnp.exp(-x))


def _silu(x):
    return x * _sigmoid(x)


def _dsilu(x):
    s = _sigmoid(x)
    return s * (1.0 + x * (1.0 - s))


_GELU_C = 0.7978845608028654


def _gelu(x):
    return 0.5 * x * (1.0 + jnp.tanh(_GELU_C * (x + 0.044715 * (x * x * x))))


def _gelu_and_grad(x):
    t = jnp.tanh(_GELU_C * (x + 0.044715 * (x * x * x)))
    half = 0.5 * (1.0 + t)
    return x * half, half + 0.5 * x * (1.0 - t * t) * (_GELU_C * (1.0 + 3.0 * 0.044715 * (x * x)))


def _pick(n, cands):
    for c in cands:
        if n % c == 0:
            return c
    return n


class _Epilogue:
    def __init__(self, rows, vecs, row_out_dtypes, n_sums, fn):
        self.rows, self.vecs, self.row_out_dtypes, self.n_sums, self.fn = list(rows), list(vecs), list(row_out_dtypes), n_sums, fn


def _rms(x):
    rs = lax.rsqrt(jnp.mean(x * x, axis=-1, keepdims=True) + EPS)
    return rs, x * rs


def _rms_dx(dn, xn, rs):
    return rs * (dn - xn * jnp.mean(dn * xn, axis=-1, keepdims=True))


def _epi_norm_fwd(g):
    def fn(r, rows, vecs):
        return r, [_rms(r)[1] * vecs[0]], []

    return _Epilogue([], [g], [BF16], 0, fn)


def _epi_loss(gf, target):
    def fn(r, rows, vecs):
        rs, xn = _rms(r)
        e = xn * vecs[0] - rows[0]
        dy = e * (1.0 / r.shape[-1])
        return (_rms_dx(dy * vecs[0], xn, rs), [],
                [jnp.sum(dy * xn, axis=0, keepdims=True), jnp.sum(e * e, axis=0, keepdims=True)])

    return _Epilogue([target], [gf], [], 2, fn)


def _epi_norm_bwd(x, g, dres):
    def fn(r, rows, vecs):
        rs, xn = _rms(rows[0])
        return _rms_dx(r * vecs[0], xn, rs) + rows[1], [], [jnp.sum(r * xn, axis=0, keepdims=True)]

    return _Epilogue([x, dres], [g], [], 1, fn)


def _mm_core(name, operands, in_specs, out_spec, out_shape, grid, nk, dims, acc_shape, has_res, side=None, epi=None):
    n_in = 3 if has_res else 2
    n_epi_in = len(epi.rows) + len(epi.vecs) if epi else 0
    n_epi_out = len(epi.row_out_dtypes) + epi.n_sums if epi else 0
    n_side_in = len(side.srcs) if side else 0
    n_side_out = side.n if side else 0

    def body(*refs):
        a_ref, b_ref = refs[0], refs[1]
        r_ref = refs[2] if has_res else None
        epi_in = refs[n_in:n_in + n_epi_in]
        side_in = refs[n_in + n_epi_in:n_in + n_epi_in + n_side_in]
        n0 = n_in + n_epi_in + n_side_in
        o_ref = refs[n0]
        epi_out = refs[n0 + 1:n0 + 1 + n_epi_out]
        side_out = refs[n0 + 1 + n_epi_out:n0 + 1 + n_epi_out + n_side_out]
        rest = refs[n0 + 1 + n_epi_out + n_side_out:]
        acc, sems = (rest[0], rest[1:]) if nk > 1 else (None, rest)
        i, j, k = pl.program_id(0), pl.program_id(1), pl.program_id(2)
        if side:
            @pl.when((i == 0) & (j == 0) & (k == 0))
            def _():
                side.start(side_in, side_out, sems)

        def product():
            if len(b_ref.shape) == 3:
                ns = b_ref.shape[2]
                return sum(lax.dot_general(a_ref[:, p * ns:(p + 1) * ns].astype(BF16), b_ref[p].astype(BF16),
                                           (dims, ((), ())), preferred_element_type=F32)
                           for p in range(b_ref.shape[0]))
            return lax.dot_general(a_ref[...].astype(BF16), b_ref[...].astype(BF16), (dims, ((), ())),
                                   preferred_element_type=F32)

        def finish(r):
            if has_res:
                r = r + r_ref[...]
            if epi is None:
                o_ref[...] = r.astype(o_ref.dtype)
                return
            n_rows = len(epi.rows)
            main, row_vals, sums = epi.fn(r, [t[...] for t in epi_in[:n_rows]], [t[...] for t in epi_in[n_rows:]])
            o_ref[...] = main.astype(o_ref.dtype)
            for ref, val in zip(epi_out, row_vals):
                ref[...] = val.astype(ref.dtype)
            for ref, val in zip(epi_out[len(row_vals):], sums):
                @pl.when(i == 0)
                def _(ref=ref):
                    ref[...] = jnp.zeros_like(ref)

                ref[...] += val

        if nk == 1:
            finish(product())
        else:
            @pl.when(k == 0)
            def _():
                acc[...] = jnp.zeros_like(acc)

            acc[...] += product()

            @pl.when(k == nk - 1)
            def _():
                finish(acc[...])

        if side:
            @pl.when((i == grid[0] - 1) & (j == grid[1] - 1) & (k == grid[2] - 1))
            def _():
                side.wait(side_in, side_out, sems)

    acc_scratch = [pltpu.VMEM(acc_shape, F32)] if nk > 1 else []
    in_specs, out_specs, out_shapes, operands = list(in_specs), [out_spec], [out_shape], list(operands)
    if epi:
        assert grid[1] == 1, "an epilogue needs tiles that span whole rows"
        tm, n = out_spec.block_shape
        row_spec = pl.BlockSpec((tm, n), lambda i, j, k: (i, 0))
        vec_spec = pl.BlockSpec((1, n), lambda i, j, k: (0, 0))
        in_specs += [row_spec] * len(epi.rows) + [vec_spec] * len(epi.vecs)
        operands += epi.rows + epi.vecs
        out_specs += [row_spec] * len(epi.row_out_dtypes) + [vec_spec] * epi.n_sums
        out_shapes += [jax.ShapeDtypeStruct(out_shape.shape, d) for d in epi.row_out_dtypes]
        out_shapes += [jax.ShapeDtypeStruct((1, n), F32)] * epi.n_sums
    scratch = acc_scratch
    if side:
        in_specs += side.in_specs
        operands += side.srcs
        out_specs += side.out_specs
        out_shapes += side.landing
        scratch = acc_scratch + side.scratch
    plain = side is None and epi is None
    res = pl.pallas_call(
        body, name=name, grid=grid, in_specs=in_specs, out_specs=out_specs[0] if plain else tuple(out_specs),
        out_shape=out_shapes[0] if plain else tuple(out_shapes), scratch_shapes=scratch,
        compiler_params=_cp("parallel", "parallel", "arbitrary") if plain else _cp("arbitrary", "arbitrary", "arbitrary"),
    )(*operands)
    return res


def _mm_nn(name, a, b, out_dtype, residual=None, epi=None):
    m, kk = a.shape
    n = b.shape[1]
    tm, tn, tk = _pick(m, (1024, 512, 256)), _pick(n, (1024, 512)), _pick(kk, (2048, 1024, 512))
    ops = [a, b]
    specs = [pl.BlockSpec((tm, tk), lambda i, j, k: (i, k)), pl.BlockSpec((tk, tn), lambda i, j, k: (k, j))]
    if residual is not None:
        ops.append(residual)
        specs.append(pl.BlockSpec((tm, tn), lambda i, j, k: (i, j)))
    return _mm_core(name, ops, specs, pl.BlockSpec((tm, tn), lambda i, j, k: (i, j)),
                    jax.ShapeDtypeStruct((m, n), out_dtype), (m // tm, n // tn, kk // tk), kk // tk,
                    ((1,), (0,)), (tm, tn), residual is not None, epi=epi)


def _mm_nt(name, a, b, out_dtype, residual=None, epi=None):
    m, kk = a.shape
    n = b.shape[0]
    tm, tn, tk = _pick(m, (1024, 512, 256)), _pick(n, (2048, 1024, 512)), _pick(kk, (2048, 1024, 512))
    ops = [a, b]
    specs = [pl.BlockSpec((tm, tk), lambda i, j, k: (i, k)), pl.BlockSpec((tn, tk), lambda i, j, k: (j, k))]
    if residual is not None:
        ops.append(residual)
        specs.append(pl.BlockSpec((tm, tn), lambda i, j, k: (i, j)))
    return _mm_core(name, ops, specs, pl.BlockSpec((tm, tn), lambda i, j, k: (i, j)),
                    jax.ShapeDtypeStruct((m, n), out_dtype), (m // tm, n // tn, kk // tk), kk // tk,
                    ((1,), (1,)), (tm, tn), residual is not None, epi=epi)


def _mm_tn(name, a, b, out_dtype):
    kk, m = a.shape
    n = b.shape[1]
    tm, tn, tk = _pick(m, (2048, 1024, 512)), _pick(n, (1024, 512)), _pick(kk, (2048, 1024, 512, 256))
    specs = [pl.BlockSpec((tk, tm), lambda i, j, k: (k, i)), pl.BlockSpec((tk, tn), lambda i, j, k: (k, j))]
    return _mm_core(name, [a, b], specs, pl.BlockSpec((tm, tn), lambda i, j, k: (i, j)),
                    jax.ShapeDtypeStruct((m, n), out_dtype), (m // tm, n // tn, kk // tk), kk // tk,
                    ((0,), (0,)), (tm, tn), False)


def _mm_nn_slots(name, a, b_slots, out_dtype, side=None):
    m, kk = a.shape
    s, _, ns = b_slots.shape
    tm, tk = _pick(m, (4096, 2048, 1024, 512, 256)), _pick(kk, (1024, 512))
    specs = [pl.BlockSpec((tm, tk), lambda i, j, k: (i, k)), pl.BlockSpec((None, tk, ns), lambda i, j, k: (j, k, 0))]
    return _mm_core(name, [a, b_slots], specs, pl.BlockSpec((tm, ns), lambda i, j, k: (i, j)),
                    jax.ShapeDtypeStruct((m, s * ns), out_dtype), (m // tm, s, kk // tk), kk // tk,
                    ((1,), (0,)), (tm, ns), False, side)


def _mm_nt_slots(name, a, b_slots, out_dtype, side=None, epi=None):
    m = a.shape[0]
    s, n, ns = b_slots.shape
    tm, tn = _pick(m, (1024, 512, 256)), _pick(n, (1024, 512))
    per = _pick(s, (2, 1))
    specs = [pl.BlockSpec((tm, per * ns), lambda i, j, k: (i, k)),
             pl.BlockSpec((per, tn, ns), lambda i, j, k: (k, j, 0))]
    return _mm_core(name, [a, b_slots], specs, pl.BlockSpec((tm, tn), lambda i, j, k: (i, j)),
                    jax.ShapeDtypeStruct((m, n), out_dtype), (m // tm, n // tn, s // per), s // per,
                    ((1,), (1,)), (tm, tn), False, side, epi)


def _mm_tn_slots(name, a, b, s, out_dtype, side=None):
    kk, m = a.shape
    ns = b.shape[1] // s
    tm, tk = _pick(m, (1024, 512)), _pick(kk, (4096, 2048, 1024, 512, 256))
    specs = [pl.BlockSpec((tk, tm), lambda i, j, k: (k, i)), pl.BlockSpec((tk, ns), lambda i, j, k: (k, j))]
    return _mm_core(name, [a, b], specs, pl.BlockSpec((None, tm, ns), lambda i, j, k: (j, i, 0)),
                    jax.ShapeDtypeStruct((s, m, ns), out_dtype), (m // tm, s, kk // tk), kk // tk,
                    ((0,), (0,)), (tm, ns), False, side)


def _rms_fwd(name, x, g):
    r, d = x.shape
    tr = _pick(r, (1024, 512, 256))

    def body(x_ref, g_ref, o_ref):
        xv = x_ref[...]
        rs = lax.rsqrt(jnp.mean(xv * xv, axis=-1, keepdims=True) + EPS)
        o_ref[...] = (xv * rs * g_ref[...]).astype(o_ref.dtype)

    return pl.pallas_call(
        body, name=name, grid=(r // tr,),
        in_specs=[pl.BlockSpec((tr, d), lambda i: (i, 0)), pl.BlockSpec((1, d), lambda i: (0, 0))],
        out_specs=pl.BlockSpec((tr, d), lambda i: (i, 0)),
        out_shape=jax.ShapeDtypeStruct((r, d), BF16), compiler_params=_cp("parallel"),
    )(x, g)


def _rms_bwd(name, x, g, dh, dres):
    r, d = x.shape
    tr = _pick(r, (512, 256))
    has_res = dres is not None

    def body(*refs):
        if has_res:
            x_ref, g_ref, dh_ref, dr_ref, dx_ref, dg_ref = refs
        else:
            x_ref, g_ref, dh_ref, dx_ref, dg_ref = refs
        i = pl.program_id(0)

        @pl.when(i == 0)
        def _():
            dg_ref[...] = jnp.zeros_like(dg_ref)

        xv = x_ref[...]
        dhv = dh_ref[...].astype(F32)
        rs = lax.rsqrt(jnp.mean(xv * xv, axis=-1, keepdims=True) + EPS)
        xn = xv * rs
        dg_ref[...] += jnp.sum(dhv * xn, axis=0, keepdims=True)
        dn = dhv * g_ref[...]
        dx = rs * (dn - xn * jnp.mean(dn * xn, axis=-1, keepdims=True))
        if has_res:
            dx = dx + dr_ref[...]
        dx_ref[...] = dx

    row = pl.BlockSpec((tr, d), lambda i: (i, 0))
    vec = pl.BlockSpec((1, d), lambda i: (0, 0))
    ops = [x, g, dh] + ([dres] if has_res else [])
    return pl.pallas_call(
        body, name=name, grid=(r // tr,),
        in_specs=[row, vec, row] + ([row] if has_res else []),
        out_specs=(row, vec),
        out_shape=(jax.ShapeDtypeStruct((r, d), F32), jax.ShapeDtypeStruct((1, d), F32)),
        compiler_params=_cp("arbitrary"),
    )(*ops)


def _rot(x, cos_t, sin_t):
    n = x.shape[-1]
    lane = lax.broadcasted_iota(jnp.int32, x.shape, 1)
    partner = jnp.where((lane % RET_DK) < RET_DK // 2, pltpu.roll(x, n - RET_DK // 2, 1), pltpu.roll(x, RET_DK // 2, 1))
    return x * cos_t + partner * sin_t


def _ret_constants(c):
    log_g = jnp.log1p(-jnp.exp2(-5.0 - jnp.arange(RET_HEADS, dtype=F32)))
    j = jnp.arange(c, dtype=F32)
    diff = j[:, None] - j[None, :]
    decay = jnp.where(diff[None] >= 0.0, jnp.exp(log_g[:, None, None] * jnp.maximum(diff, 0.0)[None]), 0.0)
    q_w = jnp.exp(log_g[None, :] * (j + 1.0)[:, None])
    k_w = jnp.exp(log_g[None, :] * (c - 1.0 - j)[:, None])
    cd = jnp.exp(log_g * c)
    rep = lambda t: jnp.repeat(t, RET_DK, axis=1)
    cd_row = jnp.repeat(cd, RET_DV)[None, :]
    return decay, rep(q_w), rep(k_w), cd_row


def _pair_of(h, c):
    lane = lax.broadcasted_iota(jnp.int32, (c, 2 * RET_DK), 1)
    mine = (lane < RET_DK) if h % 2 == 0 else (lane >= RET_DK)
    return slice((h // 2) * 2 * RET_DK, (h // 2 + 1) * 2 * RET_DK), mine


def _keep(x, mine):
    return jnp.where(mine, x, jnp.zeros_like(x))


def _ret_fwd(proj, cos_t, sin_t, consts, gn_g, c):
    l = proj.shape[0]
    nc = l // c
    decay, qw, kw, cd_row = consts

    def body(q_ref, k_ref, v_ref, g_ref, cos_ref, sin_ref, dec_ref, qw_ref, kw_ref, cd_ref, gn_ref,
             ret_ref, o_ref, rp_ref, qb_ref, kb_ref, state):
        @pl.when(pl.program_id(0) == 0)
        def _():
            state[...] = jnp.zeros_like(state)

        cs, sn = cos_ref[...], sin_ref[...]
        qr = _rot(q_ref[...].astype(F32), cs, sn)
        kr = _rot(k_ref[...].astype(F32), cs, sn) * (RET_DK ** -0.5)
        qb, kb = qr.astype(BF16), kr.astype(BF16)
        qb_ref[...] = qb
        kb_ref[...] = kb
        qwb = (qr * qw_ref[...]).astype(BF16)
        kwb = (kr * kw_ref[...]).astype(BF16)
        vb = v_ref[...].astype(BF16)
        for h in range(RET_HEADS):
            ps, mine = _pair_of(h, c)
            vs = slice(h * RET_DV, (h + 1) * RET_DV)
            s = _dot_nt(_keep(qb[:, ps], mine), kb[:, ps]) * dec_ref[h]
            r_prev = state[h]
            rp_ref[0, h] = r_prev
            o = _dot(s.astype(BF16), vb[:, vs]) + _dot(_keep(qwb[:, ps], mine), r_prev.astype(BF16))
            state[h] = cd_ref[:, vs] * r_prev + _dot_tn(_keep(kwb[:, ps], mine), vb[:, vs])
            o_ref[:, vs] = o
            mu = jnp.mean(o, axis=-1, keepdims=True)
            var = jnp.mean(jnp.square(o - mu), axis=-1, keepdims=True)
            on = (o - mu) * lax.rsqrt(var + EPS)
            ret_ref[:, vs] = (on * gn_ref[:, vs] * _silu(g_ref[:, vs].astype(F32))).astype(ret_ref.dtype)

    const2 = lambda shape: pl.BlockSpec(shape, lambda i: (0,) * len(shape))
    return pl.pallas_call(
        body, name="retention_fwd", grid=(nc,),
        in_specs=[pl.BlockSpec((c, RET_QK), lambda i: (i, 0)), pl.BlockSpec((c, RET_QK), lambda i: (i, 1)),
                  pl.BlockSpec((c, D_MODEL), lambda i: (i, 1)), pl.BlockSpec((c, D_MODEL), lambda i: (i, 2)),
                  pl.BlockSpec((c, RET_QK), lambda i: (i, 0)), pl.BlockSpec((c, RET_QK), lambda i: (i, 0)),
                  const2((RET_HEADS, c, c)), const2((c, RET_QK)), const2((c, RET_QK)), const2((1, D_MODEL)),
                  const2((1, D_MODEL))],
        out_specs=(pl.BlockSpec((c, D_MODEL), lambda i: (i, 0)), pl.BlockSpec((c, D_MODEL), lambda i: (i, 0)),
                   pl.BlockSpec((1, RET_HEADS, 2 * RET_DK, RET_DV), lambda i: (i, 0, 0, 0)),
                   pl.BlockSpec((c, RET_QK), lambda i: (i, 0)), pl.BlockSpec((c, RET_QK), lambda i: (i, 0))),
        out_shape=(jax.ShapeDtypeStruct((l, 2 * D_MODEL), BF16), jax.ShapeDtypeStruct((l, D_MODEL), F32),
                   jax.ShapeDtypeStruct((nc, RET_HEADS, 2 * RET_DK, RET_DV), F32),
                   jax.ShapeDtypeStruct((l, RET_QK), BF16), jax.ShapeDtypeStruct((l, RET_QK), BF16)),
        scratch_shapes=[pltpu.VMEM((RET_HEADS, 2 * RET_DK, RET_DV), F32)],
        compiler_params=_cp("arbitrary"),
    )(proj, proj, proj, proj, cos_t, sin_t, decay, qw, kw, cd_row, gn_g)


def _ret_bwd(proj, qb_saved, kb_saved, cos_t, sin_t, consts, gn_g, o_saved, r_prev_saved, dmix, c, side):
    l = proj.shape[0]
    nc = l // c
    decay, qw, kw, cd_row = consts
    n_in = 14

    def body(*refs):
        (q_ref, k_ref, v_ref, g_ref, cos_ref, sin_ref, dec_ref, qw_ref, kw_ref, cd_ref, gn_ref, o_ref, rp_ref,
         dr_ref) = refs[:n_in]
        side_in = refs[n_in:n_in + len(side.srcs)]
        out_ref, dgn_ref = refs[n_in + len(side.srcs):n_in + len(side.srcs) + 2]
        side_out = refs[n_in + len(side.srcs) + 2:n_in + len(side.srcs) + 2 + side.n]
        state, dq_s, dk_s = refs[n_in + len(side.srcs) + 2 + side.n:n_in + len(side.srcs) + 5 + side.n]
        sems = refs[n_in + len(side.srcs) + 5 + side.n:]

        @pl.when(pl.program_id(0) == 0)
        def _():
            side.start(side_in, side_out, sems)
            state[...] = jnp.zeros_like(state)
            dgn_ref[...] = jnp.zeros_like(dgn_ref)

        cs, sn = cos_ref[...], sin_ref[...]
        qb, kb = q_ref[...], k_ref[...]
        qwv, kwv = qw_ref[...], kw_ref[...]
        qwb = (qb.astype(F32) * qwv).astype(BF16)
        kwb = (kb.astype(F32) * kwv).astype(BF16)
        vb = v_ref[...].astype(BF16)
        dq2 = dk2 = None
        for h in range(RET_HEADS):
            ps, mine = _pair_of(h, c)
            vs = slice(h * RET_DV, (h + 1) * RET_DV)
            dec = dec_ref[h]
            qm, km = _keep(qb[:, ps], mine), _keep(kb[:, ps], mine)
            o = o_ref[:, vs]
            mu = jnp.mean(o, axis=-1, keepdims=True)
            var = jnp.mean(jnp.square(o - mu), axis=-1, keepdims=True)
            rstd = lax.rsqrt(var + EPS)
            on = (o - mu) * rstd
            gate = g_ref[:, vs].astype(F32)
            sg = _silu(gate)
            dret = dr_ref[:, vs].astype(F32)
            gn = gn_ref[:, vs]
            dgn_ref[:, vs] += jnp.sum(dret * on * sg, axis=0, keepdims=True)
            out_ref[:, 2 * RET_QK + D_MODEL + h * RET_DV:2 * RET_QK + D_MODEL + (h + 1) * RET_DV] = (
                dret * on * gn * _dsilu(gate)).astype(out_ref.dtype)
            don = dret * gn * sg
            do = rstd * (don - jnp.mean(don, axis=-1, keepdims=True)
                         - on * jnp.mean(don * on, axis=-1, keepdims=True))
            dob = do.astype(BF16)
            sn_h = state[h]
            snb = sn_h.astype(BF16)
            s = _dot_nt(qm, kb[:, ps]) * dec
            dv = _dot_tn(s.astype(BF16), dob) + _dot(_keep(kwb[:, ps], mine), snb)
            out_ref[:, 2 * RET_QK + h * RET_DV:2 * RET_QK + (h + 1) * RET_DV] = dv.astype(out_ref.dtype)
            ds = (_dot_nt(dob, vb[:, vs]) * dec).astype(BF16)
            dq_h = _dot(ds, km) + qwv[:, ps] * _dot_nt(dob, rp_ref[0, h].astype(BF16))
            dk_h = _dot_tn(ds, qm) + kwv[:, ps] * _dot_nt(vb[:, vs], snb)
            state[h] = cd_ref[:, vs] * sn_h + _dot_tn(_keep(qwb[:, ps], mine), dob)
            if h % 2 == 0:
                dq2, dk2 = dq_h, dk_h
            else:
                dq_s[:, ps] = dq2 + dq_h
                dk_s[:, ps] = dk2 + dk_h
        out_ref[:, 0:RET_QK] = _rot(dq_s[...], cs, -sn).astype(out_ref.dtype)
        out_ref[:, RET_QK:2 * RET_QK] = (_rot(dk_s[...], cs, -sn) * (RET_DK ** -0.5)).astype(out_ref.dtype)

        @pl.when(pl.program_id(0) == nc - 1)
        def _():
            side.wait(side_in, side_out, sems)

    rev = lambda i: nc - 1 - i
    const2 = lambda shape: pl.BlockSpec(shape, lambda i: (0,) * len(shape))
    return pl.pallas_call(
        body, name="retention_bwd", grid=(nc,),
        in_specs=[pl.BlockSpec((c, RET_QK), lambda i: (rev(i), 0)), pl.BlockSpec((c, RET_QK), lambda i: (rev(i), 0)),
                  pl.BlockSpec((c, D_MODEL), lambda i: (rev(i), 1)), pl.BlockSpec((c, D_MODEL), lambda i: (rev(i), 2)),
                  pl.BlockSpec((c, RET_QK), lambda i: (rev(i), 0)), pl.BlockSpec((c, RET_QK), lambda i: (rev(i), 0)),
                  const2((RET_HEADS, c, c)), const2((c, RET_QK)), const2((c, RET_QK)), const2((1, D_MODEL)),
                  const2((1, D_MODEL)),
                  pl.BlockSpec((c, D_MODEL), lambda i: (rev(i), 0)),
                  pl.BlockSpec((1, RET_HEADS, 2 * RET_DK, RET_DV), lambda i: (rev(i), 0, 0, 0)),
                  pl.BlockSpec((c, D_MODEL), lambda i: (rev(i), 0))] + side.in_specs,
        out_specs=(pl.BlockSpec((c, 2 * RET_QK + 2 * D_MODEL), lambda i: (rev(i), 0)), const2((1, D_MODEL)),
                   *side.out_specs),
        out_shape=(jax.ShapeDtypeStruct((l, 2 * RET_QK + 4 * D_MODEL), BF16), jax.ShapeDtypeStruct((1, D_MODEL), F32),
                   *side.landing),
        scratch_shapes=[pltpu.VMEM((RET_HEADS, 2 * RET_DK, RET_DV), F32), pltpu.VMEM((c, RET_QK), F32),
                        pltpu.VMEM((c, RET_QK), F32)] + side.scratch,
        compiler_params=_cp("arbitrary"),
    )(qb_saved, kb_saved, proj, proj, cos_t, sin_t, decay, qw, kw, cd_row, gn_g, o_saved, r_prev_saved, dmix,
      *side.srcs)


def _zoh(a_re, a_im, log_dt):
    dt = jnp.exp(log_dt)
    mag = jnp.exp(a_re * dt)
    abar_re = mag * jnp.cos(a_im * dt)
    abar_im = mag * jnp.sin(a_im * dt)
    den = a_re * a_re + a_im * a_im
    nr, ni = abar_re - 1.0, abar_im
    f_re = (nr * a_re + ni * a_im) / den
    f_im = (ni * a_re - nr * a_im) / den
    return dt, abar_re, abar_im, f_re, f_im, den


def _lanes_p(f):
    return jnp.tile(f, (1, S5_P))


def _s5_discretize(a_re, a_im, log_dt, b_re_t, b_im_t):
    def body(ar_ref, ai_ref, ld_ref, br_ref, bi_ref, abr_ref, abi_ref, bbr_ref, bbi_ref):
        _, abar_re, abar_im, f_re, f_im, _ = _zoh(ar_ref[...], ai_ref[...], ld_ref[...])
        abr_ref[...] = abar_re
        abi_ref[...] = abar_im
        fr, fi = _lanes_p(f_re), _lanes_p(f_im)
        bbr_ref[...] = fr * br_ref[...] - fi * bi_ref[...]
        bbi_ref[...] = fr * bi_ref[...] + fi * br_ref[...]

    gn = jax.ShapeDtypeStruct((S5_G, S5_N), F32)
    gpn = jax.ShapeDtypeStruct((S5_G, S5_P * S5_N), F32)
    return pl.pallas_call(body, name="s5_discretize", out_shape=(gn, gn, gpn, gpn))(a_re, a_im, log_dt, b_re_t, b_im_t)


def _s5_discretize_bwd(a_re, a_im, log_dt, b_re_t, b_im_t, dab_re, dab_im, dbb_re_t, dbb_im_t):
    def body(ar_ref, ai_ref, ld_ref, br_ref, bi_ref, gar_ref, gai_ref, gbr_ref, gbi_ref,
             dar_ref, dai_ref, dld_ref, dbr_ref, dbi_ref):
        a_r, a_i = ar_ref[...], ai_ref[...]
        dt, abar_re, abar_im, f_re, f_im, den = _zoh(a_r, a_i, ld_ref[...])
        b_r, b_i, g_br, g_bi = br_ref[...], bi_ref[...], gbr_ref[...], gbi_ref[...]
        fr, fi = _lanes_p(f_re), _lanes_p(f_im)
        dbr_ref[...] = fr * g_br + fi * g_bi
        dbi_ref[...] = fr * g_bi - fi * g_br
        t_r = b_r * g_br + b_i * g_bi
        t_i = b_r * g_bi - b_i * g_br
        gf_r = sum(t_r[:, p * S5_N:(p + 1) * S5_N] for p in range(S5_P))
        gf_i = sum(t_i[:, p * S5_N:(p + 1) * S5_N] for p in range(S5_P))
        inv_r, inv_i = a_r / den, a_i / den
        ga_r = gar_ref[...] + gf_r * inv_r - gf_i * inv_i
        ga_i = gai_ref[...] + gf_r * inv_i + gf_i * inv_r
        q_r = -(f_re * a_r + f_im * a_i) / den
        q_i = -(f_im * a_r - f_re * a_i) / den
        gl_r = q_r * gf_r + q_i * gf_i
        gl_i = q_r * gf_i - q_i * gf_r
        dar_ref[...] = gl_r + dt * (abar_re * ga_r + abar_im * ga_i)
        dai_ref[...] = gl_i + dt * (abar_re * ga_i - abar_im * ga_r)
        la_r = a_r * abar_re - a_i * abar_im
        la_i = a_r * abar_im + a_i * abar_re
        dld_ref[...] = dt * jnp.sum(ga_r * la_r + ga_i * la_i, axis=-1, keepdims=True)

    gn = jax.ShapeDtypeStruct((S5_G, S5_N), F32)
    gpn = jax.ShapeDtypeStruct((S5_G, S5_P * S5_N), F32)
    return pl.pallas_call(
        body, name="s5_discretize_bwd", out_shape=(gn, gn, jax.ShapeDtypeStruct((S5_G, 1), F32), gpn, gpn),
    )(a_re, a_im, log_dt, b_re_t, b_im_t, dab_re, dab_im, dbb_re_t, dbb_im_t)


S5_ZQ = S5_NB // 2


def _s5_z(re, im):
    return jnp.concatenate([re.reshape(S5_ZQ, 8, 128), im.reshape(S5_ZQ, 8, 128)], axis=0)


def _s5_unz(z):
    return z[:S5_ZQ].reshape(S5_G, S5_N), z[S5_ZQ:].reshape(S5_G, S5_N)


def _s5_block_mats(bb_re, bb_im, c_re, c_im):
    eye = jnp.eye(S5_GB, dtype=F32)
    bb = jnp.stack([bb_re, bb_im], axis=0).reshape(2, S5_NB, S5_GB, S5_N, S5_P)
    bbm = jnp.einsum("rbgnp,gh->bgprhn", bb, eye).reshape(S5_NB, S5_GB * S5_P, 2 * S5_BS)
    cc = jnp.stack([c_re, -c_im], axis=0).reshape(2, S5_NB, S5_GB, S5_P, S5_N)
    ccm = jnp.einsum("rbgpn,gh->brhngp", cc, eye).reshape(S5_NB, 2 * S5_BS, S5_GB * S5_P)
    return bbm.astype(BF16), ccm.astype(BF16)


def _s5_block_diag_bb(m):
    t = m.reshape(S5_NB, S5_GB, S5_P, 2, S5_GB, S5_N)
    d = jnp.einsum("bgprgn->rbgnp", t).reshape(2, S5_G, S5_N, S5_P)
    return d[0], d[1]


def _s5_block_diag_cc(m):
    t = m.reshape(S5_NB, 2, S5_GB, S5_N, S5_GB, S5_P)
    d = jnp.einsum("brgngp->rbgpn", t).reshape(2, S5_G, S5_P, S5_N)
    return d[0], -d[1]


SCAN_UNROLL = 8


def _z_store(zr, zi, blk, res, t, off):
    q, h = blk // 2, blk % 2
    for lt in range(4):
        zr[q, pl.ds(off + 4 * h + lt, t, stride=8), :] = res[:, lt * 128:(lt + 1) * 128]
        zi[q, pl.ds(off + 4 * h + lt, t, stride=8), :] = res[:, S5_BS + lt * 128:S5_BS + (lt + 1) * 128]


def _z_load(zr, zi, blk, t, off):
    q, h = blk // 2, blk % 2
    return jnp.concatenate([zr[q, pl.ds(off + 4 * h + lt, t, stride=8), :] for lt in range(4)]
                           + [zi[q, pl.ds(off + 4 * h + lt, t, stride=8), :] for lt in range(4)], axis=1)


def _z_scan_fwd(zr, zi, a_ref, carry_ref, t, off):
    ar = [a_ref[q] for q in range(S5_ZQ)]
    ai = [a_ref[S5_ZQ + q] for q in range(S5_ZQ)]

    def step(it, carry):
        carry = list(carry)
        base = pl.multiple_of(it * (8 * SCAN_UNROLL), 8 * SCAN_UNROLL) + off
        for tt in range(SCAN_UNROLL):
            rows = pl.ds(base + 8 * tt, 8)
            for q in range(S5_ZQ):
                c_r, c_i = carry[q], carry[S5_ZQ + q]
                n_r = ar[q] * c_r - ai[q] * c_i + zr[q, rows, :]
                n_i = ar[q] * c_i + ai[q] * c_r + zi[q, rows, :]
                zr[q, rows, :] = n_r
                zi[q, rows, :] = n_i
                carry[q], carry[S5_ZQ + q] = n_r, n_i
        return tuple(carry)

    out = lax.fori_loop(0, t // SCAN_UNROLL, step, tuple(carry_ref[k] for k in range(2 * S5_ZQ)))
    for k in range(2 * S5_ZQ):
        carry_ref[k] = out[k]


def _z_scan_bwd(lr, li, xr, xi, a_ref, carry_ref, acc_ref, t):
    ar = [a_ref[q] for q in range(S5_ZQ)]
    ai = [a_ref[S5_ZQ + q] for q in range(S5_ZQ)]
    n_it = t // SCAN_UNROLL

    def step(it, state):
        carry, acc = list(state[0]), list(state[1])
        base = pl.multiple_of((n_it - 1 - it) * (8 * SCAN_UNROLL), 8 * SCAN_UNROLL)
        for tt in reversed(range(SCAN_UNROLL)):
            rows = pl.ds(base + 8 * tt, 8)
            for q in range(S5_ZQ):
                c_r, c_i = carry[q], carry[S5_ZQ + q]
                n_r = ar[q] * c_r + ai[q] * c_i + lr[q, rows, :]
                n_i = ar[q] * c_i - ai[q] * c_r + li[q, rows, :]
                lr[q, rows, :] = n_r
                li[q, rows, :] = n_i
                p_r, p_i = xr[q, rows, :], xi[q, rows, :]
                acc[q] = acc[q] + n_r * p_r + n_i * p_i
                acc[S5_ZQ + q] = acc[S5_ZQ + q] + n_i * p_r - n_r * p_i
                carry[q], carry[S5_ZQ + q] = n_r, n_i
        return tuple(carry), tuple(acc)

    k8 = range(2 * S5_ZQ)
    carry, acc = lax.fori_loop(0, n_it, step, (tuple(carry_ref[k] for k in k8), tuple(acc_ref[k] for k in k8)))
    for k in k8:
        carry_ref[k] = carry[k]
        acc_ref[k] = acc[k]


def _s5_fwd(proj, mix, bbm, ccm, d_row, glu_w, glu_b, tabs, t, side):
    l = proj.shape[0]
    nt = l // t
    n_in = 9

    def body(*refs):
        u_ref, gs_ref, bb_ref, cc_ref, d_ref, gw_ref, gb_ref, a_ref, _ = refs[:n_in]
        side_in = refs[n_in:n_in + len(side.srcs)]
        ssm_ref, xst_ref, y1_ref, z_ref = refs[n_in + len(side.srcs):n_in + len(side.srcs) + 4]
        side_out = refs[n_in + len(side.srcs) + 4:n_in + len(side.srcs) + 4 + side.n]
        zr, zi, carry = refs[n_in + len(side.srcs) + 4 + side.n:n_in + len(side.srcs) + 7 + side.n]
        sems = refs[n_in + len(side.srcs) + 7 + side.n:]

        @pl.when(pl.program_id(0) == 0)
        def _():
            side.start(side_in, side_out, sems)
            carry[...] = jnp.zeros_like(carry)

        xst_ref[0] = carry[...]
        ub = u_ref[...]
        u = ub.astype(F32)
        for blk in range(S5_NB):
            _z_store(zr, zi, blk, _dot(ub[:, blk * 128:(blk + 1) * 128], bb_ref[blk]), t, 0)
        _z_scan_fwd(zr, zi, a_ref, carry, t, 0)
        ys = jnp.concatenate(
            [_dot(_z_load(zr, zi, blk, t, 0).astype(BF16), cc_ref[blk]) for blk in range(S5_NB)], axis=1)
        y1 = ys + d_ref[...] * u
        y1_ref[...] = y1.astype(y1_ref.dtype)
        y2 = _gelu(y1)
        z = _dot(y2.astype(BF16), gw_ref[...]) + gb_ref[...]
        z_ref[...] = z.astype(z_ref.dtype)
        ssm_ref[...] = (y2 * _sigmoid(z) * _silu(gs_ref[...].astype(F32))).astype(ssm_ref.dtype)

        @pl.when(pl.program_id(0) == nt - 1)
        def _():
            side.wait(side_in, side_out, sems)

    const2 = lambda shape: pl.BlockSpec(shape, lambda i: (0,) * len(shape))
    zshape = (2 * S5_ZQ, 8, 128)
    return pl.pallas_call(
        body, name="s5_fwd", grid=(nt,),
        in_specs=[pl.BlockSpec((t, D_MODEL), lambda i: (i, 3)), pl.BlockSpec((t, D_MODEL), lambda i: (i, 4)),
                  const2(bbm.shape), const2(ccm.shape), const2((1, D_MODEL)), const2((D_MODEL, D_MODEL)),
                  const2((1, D_MODEL)), const2(zshape), pl.BlockSpec(memory_space=pl.ANY)] + side.in_specs,
        out_specs=(pl.BlockSpec((t, D_MODEL), lambda i: (i, 1)), pl.BlockSpec((1,) + zshape, lambda i: (i, 0, 0, 0)),
                   pl.BlockSpec((t, D_MODEL), lambda i: (i, 0)), pl.BlockSpec((t, D_MODEL), lambda i: (i, 0)),
                   *side.out_specs),
        out_shape=(jax.ShapeDtypeStruct((l, 2 * D_MODEL), BF16), jax.ShapeDtypeStruct((nt,) + zshape, F32),
                   jax.ShapeDtypeStruct((l, D_MODEL), BF16), jax.ShapeDtypeStruct((l, D_MODEL), BF16), *side.landing),
        scratch_shapes=[pltpu.VMEM((S5_ZQ, 8 * t, 128), F32), pltpu.VMEM((S5_ZQ, 8 * t, 128), F32),
                        pltpu.VMEM(zshape, F32)] + side.scratch,
        input_output_aliases={8: 0},
        compiler_params=_cp("arbitrary"),
    )(proj, proj, bbm, ccm, d_row, glu_w, glu_b, tabs, mix, *side.srcs)


def _s5_bwd(proj, dmix, dproj, xstart, y1, z, bbm, ccm, d_row, glu_w, tabs, t):
    l = proj.shape[0]
    nt = l // t
    col0 = 2 * RET_QK + 2 * D_MODEL

    def body(u_ref, gs_ref, dm_ref, xst_ref, bb_ref, cc_ref, d_ref, gw_ref, a_ref, _, y1_ref, z_ref,
             dp_ref, y2_ref, dz_ref, dbb_ref, dcc_ref, da_ref, dd_ref, dgb_ref, xr, xi, lr, li, carry, lcarry,
             dug_s, dug_sem):
        step = pl.program_id(0)
        slot = step % 2
        dug_ref = dug_s.at[slot]

        def put(s, at_step):
            rows = pl.ds(pl.multiple_of((nt - 1 - at_step) * t, t), t)
            return pltpu.make_async_copy(dug_s.at[s], dp_ref.at[rows, pl.ds(col0, 2 * D_MODEL)], dug_sem.at[s])

        @pl.when(step >= 2)
        def _():
            put(slot, step - 2).wait()

        @pl.when(step == 0)
        def _():
            lcarry[...] = jnp.zeros_like(lcarry)
            dbb_ref[...] = jnp.zeros_like(dbb_ref)
            dcc_ref[...] = jnp.zeros_like(dcc_ref)
            da_ref[...] = jnp.zeros_like(da_ref)
            dd_ref[...] = jnp.zeros_like(dd_ref)
            dgb_ref[...] = jnp.zeros_like(dgb_ref)

        carry[...] = xst_ref[0]
        for q in range(S5_ZQ):
            xr[q, 0:8, :] = carry[q]
            xi[q, 0:8, :] = carry[S5_ZQ + q]
        ub = u_ref[...]
        u = ub.astype(F32)
        for blk in range(S5_NB):
            _z_store(xr, xi, blk, _dot(ub[:, blk * 128:(blk + 1) * 128], bb_ref[blk]), t, 8)
        _z_scan_fwd(xr, xi, a_ref, carry, t, 8)
        dv = d_ref[...]
        y2, dgelu = _gelu_and_grad(y1_ref[...].astype(F32))
        y2b = y2.astype(BF16)
        sg = _sigmoid(z_ref[...].astype(F32))
        gs = gs_ref[...].astype(F32)
        dssm = dm_ref[...].astype(F32)
        dug_ref[:, D_MODEL:] = (dssm * (y2 * sg) * _dsilu(gs)).astype(dug_ref.dtype)
        dy3 = dssm * _silu(gs)
        dz = dy3 * y2 * sg * (1.0 - sg)
        dzb = dz.astype(BF16)
        y2_ref[...] = y2b
        dz_ref[...] = dzb
        dgb_ref[...] += jnp.sum(dz, axis=0, keepdims=True)
        dy1 = (dy3 * sg + _dot_nt(dzb, gw_ref[...])) * dgelu
        dd_ref[...] += jnp.sum(dy1 * u, axis=0, keepdims=True)
        dyb = dy1.astype(BF16)
        for blk in range(S5_NB):
            ch = slice(blk * 128, (blk + 1) * 128)
            _z_store(lr, li, blk, _dot_nt(dyb[:, ch], cc_ref[blk]), t, 0)
            dcc_ref[blk] += _dot_tn(_z_load(xr, xi, blk, t, 8).astype(BF16), dyb[:, ch])
        _z_scan_bwd(lr, li, xr, xi, a_ref, lcarry, da_ref, t)
        du = []
        for blk in range(S5_NB):
            lb = _z_load(lr, li, blk, t, 0).astype(BF16)
            du.append(_dot_nt(lb, bb_ref[blk]))
            dbb_ref[blk] += _dot_tn(ub[:, blk * 128:(blk + 1) * 128], lb)
        dug_ref[:, :D_MODEL] = (jnp.concatenate(du, axis=1) + dy1 * dv).astype(dug_ref.dtype)
        put(slot, step).start()

        @pl.when(step == nt - 1)
        def _():
            put(slot, step).wait()
            if nt > 1:
                put(1 - slot, step - 1).wait()

    rev = lambda i: nt - 1 - i
    const2 = lambda shape: pl.BlockSpec(shape, lambda i: (0,) * len(shape))
    row_out = lambda w: pl.BlockSpec((t, w), lambda i: (rev(i), 0))
    zshape = (2 * S5_ZQ, 8, 128)
    hbm = pl.BlockSpec(memory_space=pl.ANY)
    return pl.pallas_call(
        body, name="s5_bwd", grid=(nt,),
        in_specs=[pl.BlockSpec((t, D_MODEL), lambda i: (rev(i), 3)), pl.BlockSpec((t, D_MODEL), lambda i: (rev(i), 4)),
                  pl.BlockSpec((t, D_MODEL), lambda i: (rev(i), 1)),
                  pl.BlockSpec((1,) + zshape, lambda i: (rev(i), 0, 0, 0)),
                  const2(bbm.shape), const2(ccm.shape), const2((1, D_MODEL)), const2((D_MODEL, D_MODEL)),
                  const2(zshape), hbm, pl.BlockSpec((t, D_MODEL), lambda i: (rev(i), 0)),
                  pl.BlockSpec((t, D_MODEL), lambda i: (rev(i), 0))],
        out_specs=(hbm, row_out(D_MODEL), row_out(D_MODEL), const2(bbm.shape), const2(ccm.shape),
                   const2(zshape), const2((1, D_MODEL)), const2((1, D_MODEL))),
        out_shape=(jax.ShapeDtypeStruct(dproj.shape, BF16), jax.ShapeDtypeStruct((l, D_MODEL), BF16),
                   jax.ShapeDtypeStruct((l, D_MODEL), BF16), jax.ShapeDtypeStruct(bbm.shape, F32),
                   jax.ShapeDtypeStruct(ccm.shape, F32), jax.ShapeDtypeStruct(zshape, F32),
                   jax.ShapeDtypeStruct((1, D_MODEL), F32), jax.ShapeDtypeStruct((1, D_MODEL), F32)),
        scratch_shapes=[pltpu.VMEM((S5_ZQ, 8 * t + 8, 128), F32), pltpu.VMEM((S5_ZQ, 8 * t + 8, 128), F32),
                        pltpu.VMEM((S5_ZQ, 8 * t, 128), F32), pltpu.VMEM((S5_ZQ, 8 * t, 128), F32),
                        pltpu.VMEM(zshape, F32), pltpu.VMEM(zshape, F32),
                        pltpu.VMEM((2, t, 2 * D_MODEL), BF16), pltpu.SemaphoreType.DMA((2,))],
        input_output_aliases={9: 0},
        compiler_params=_cp("arbitrary"),
    )(proj, proj, dmix, xstart, bbm, ccm, d_row, glu_w, tabs, dproj, y1, z)


def _attn_probs(qh, kh):
    s = _dot_nt(qh, kh) * (XA_DH ** -0.5)
    e = jnp.exp(s - jnp.max(s, axis=-1, keepdims=True))
    return e / jnp.sum(e, axis=-1, keepdims=True)


def _attn_fwd(qa, ka, va):
    l = qa.shape[0]
    m = ka.shape[0]
    tl = _pick(l, (2048, 1024, 512, 256))

    def body(q_ref, k_ref, v_ref, o_ref):
        for h in range(XA_HEADS):
            hs = slice(h * XA_DH, (h + 1) * XA_DH)
            p = _attn_probs(q_ref[:, hs], k_ref[:, hs])
            o_ref[:, hs] = _dot(p.astype(BF16), v_ref[:, hs]).astype(o_ref.dtype)

    return pl.pallas_call(
        body, name="xattn_fwd", grid=(l // tl,),
        in_specs=[pl.BlockSpec((tl, D_MODEL), lambda i: (i, 0)), pl.BlockSpec((m, D_MODEL), lambda i: (0, 0)),
                  pl.BlockSpec((m, D_MODEL), lambda i: (0, 0))],
        out_specs=pl.BlockSpec((tl, D_MODEL), lambda i: (i, 0)),
        out_shape=jax.ShapeDtypeStruct((l, D_MODEL), BF16), compiler_params=_cp("parallel"),
    )(qa, ka, va)


def _attn_bwd(qa, ka, va, doa):
    l = qa.shape[0]
    m = ka.shape[0]
    tl = _pick(l, (2048, 1024, 512, 256))

    def body(q_ref, k_ref, v_ref, do_ref, dq_ref, dk_ref, dv_ref):
        @pl.when(pl.program_id(0) == 0)
        def _():
            dk_ref[...] = jnp.zeros_like(dk_ref)
            dv_ref[...] = jnp.zeros_like(dv_ref)

        for h in range(XA_HEADS):
            hs = slice(h * XA_DH, (h + 1) * XA_DH)
            qh, kh, vh, doh = q_ref[:, hs], k_ref[:, hs], v_ref[:, hs], do_ref[:, hs]
            p = _attn_probs(qh, kh)
            dv_ref[:, hs] += _dot_tn(p.astype(BF16), doh)
            dp = _dot_nt(doh, vh)
            ds = (p * (dp - jnp.sum(dp * p, axis=-1, keepdims=True)) * (XA_DH ** -0.5)).astype(BF16)
            dq_ref[:, hs] = _dot(ds, kh).astype(dq_ref.dtype)
            dk_ref[:, hs] += _dot_tn(ds, qh)

    row = pl.BlockSpec((tl, D_MODEL), lambda i: (i, 0))
    mem = pl.BlockSpec((m, D_MODEL), lambda i: (0, 0))
    return pl.pallas_call(
        body, name="xattn_bwd", grid=(l // tl,), in_specs=[row, mem, mem, row], out_specs=(row, mem, mem),
        out_shape=(jax.ShapeDtypeStruct((l, D_MODEL), BF16), jax.ShapeDtypeStruct((m, D_MODEL), F32),
                   jax.ShapeDtypeStruct((m, D_MODEL), F32)),
        compiler_params=_cp("arbitrary"),
    )(qa, ka, va, doa)


def _me_and_peers():
    x, y, c = lax.axis_index("x"), lax.axis_index("y"), lax.axis_index("c")
    flip = lambda v, bit: (1 - v) if bit else v
    peers = []
    for k in range(1, N_DEV):
        px, py, pc = flip(x, (k >> 2) & 1), flip(y, (k >> 1) & 1), flip(c, k & 1)
        peers.append(((px, py, pc), 4 * px + 2 * py + pc))
    return 4 * x + 2 * y + c, peers


class _SideJob:
    def __init__(self, srcs, landing, src_of, dst_of):
        self.srcs = list(srcs)
        self.landing = list(landing)
        self.n = len(self.landing)
        self.src_of, self.dst_of = src_of, dst_of
        hbm = pl.BlockSpec(memory_space=pl.ANY)
        self.in_specs = [hbm] * len(self.srcs)
        self.out_specs = [hbm] * self.n
        self.scratch = [pltpu.SemaphoreType.DMA((self.n * (N_DEV - 1),)), pltpu.SemaphoreType.DMA((self.n * (N_DEV - 1),)),
                        pltpu.SemaphoreType.DMA((self.n,))]

    def _copies(self, src_refs, out_refs, sems):
        send_sems, recv_sems, loc_sems = sems
        me, peers = _me_and_peers()
        local = [pltpu.make_async_copy(self.src_of(a, me, src_refs), self.dst_of(a, me, out_refs), loc_sems.at[a])
                 for a in range(self.n)]
        sends, recvs = [], []
        for k, (peer, peer_idx) in enumerate(peers):
            for a in range(self.n):
                s = self.n * k + a
                sends.append(pltpu.make_async_remote_copy(
                    src_ref=self.src_of(a, peer_idx, src_refs), dst_ref=self.dst_of(a, me, out_refs),
                    send_sem=send_sems.at[s], recv_sem=recv_sems.at[s], device_id=peer, device_id_type=MESH))
                recvs.append(pltpu.make_async_remote_copy(
                    src_ref=self.src_of(a, me, src_refs), dst_ref=self.dst_of(a, peer_idx, out_refs),
                    send_sem=send_sems.at[s], recv_sem=recv_sems.at[s], device_id=peer, device_id_type=MESH))
        return local, sends, recvs

    def start(self, src_refs, out_refs, sems):
        if not self.n:
            return
        local, sends, _ = self._copies(src_refs, out_refs, sems)
        for cp in local + sends:
            cp.start()

    def wait(self, src_refs, out_refs, sems):
        if not self.n:
            return
        local, sends, recvs = self._copies(src_refs, out_refs, sems)
        for cp in recvs:
            cp.wait_recv()
        for cp in sends:
            cp.wait_send()
        for cp in local:
            cp.wait()


def _gather_job(shards):
    return _SideJob(shards, [jax.ShapeDtypeStruct((N_DEV,) + s.shape, s.dtype) for s in shards],
                    src_of=lambda a, j, srcs: srcs[a], dst_of=lambda a, j, outs: outs[a].at[j])


def _scatter_job(grads):
    landing, parts = [], []
    for g in grads:
        if g.ndim == 3:
            landing.append(jax.ShapeDtypeStruct(g.shape, g.dtype))
            parts.append(None)
        else:
            r = g.shape[0] // N_DEV
            landing.append(jax.ShapeDtypeStruct((N_DEV, r, g.shape[1]), g.dtype))
            parts.append(r)

    def src_of(a, j, srcs):
        if parts[a] is None:
            return srcs[a].at[j]
        return srcs[a].at[pl.ds(pl.multiple_of(j * parts[a], 8), parts[a]), :]

    return _SideJob(grads, landing, src_of=src_of, dst_of=lambda a, j, outs: outs[a].at[j])


def _prologue(w_in_shard, row_shards, x, g, pos_col, inv_row):
    n_row = len(row_shards)
    l, d = x.shape
    tr = _pick(l, (1024, 512, 256))
    nt = l // tr
    mid = nt - 1

    def body(*refs):
        win_ref = refs[0]
        row_refs = refs[1:1 + n_row]
        x_ref, g_ref, p_ref, inv_ref = refs[1 + n_row:5 + n_row]
        out_win = refs[5 + n_row]
        row_outs = refs[6 + n_row:6 + 2 * n_row]
        h_ref, cos_ref, sin_ref = refs[6 + 2 * n_row:9 + 2 * n_row]
        win_b, send_sems, recv_sems, local_sem = refs[9 + 2 * n_row:]
        step = pl.program_id(0)
        cx, cy, cc = lax.axis_index("x"), lax.axis_index("y"), lax.axis_index("c")
        me, sibling = (cx, cy, cc), (cx, cy, 1 - cc)
        chips = [(1 - cx, cy), (cx, 1 - cy), (1 - cx, 1 - cy)]
        slot = lambda p: out_win.at[4 * p[0] + 2 * p[1] + p[2]]

        def copy(k, block, to, src=None):
            return pltpu.make_async_remote_copy(
                src_ref=slot(block) if src is None else src, dst_ref=slot(block), send_sem=send_sems.at[k],
                recv_sem=recv_sems.at[k], device_id=to, device_id_type=MESH)

        mine = pltpu.make_async_copy(win_b, slot(me), local_sem)
        first = [copy(0, me, sibling, src=win_b)]
        first += [copy(1 + j, me, (*chip, cc), src=win_b) for j, chip in enumerate(chips)]
        passed = [copy(4 + j, (*chip, cc), sibling) for j, chip in enumerate(chips)]

        @pl.when(step == 0)
        def _():
            win_b[...] = win_ref[...].astype(BF16)
            mine.start()
            for cp in first:
                cp.start()
            for r, o in zip(row_refs, row_outs):
                o[...] = r[...].astype(BF16)

        h_ref[...] = (_rms(x_ref[...])[1] * g_ref[...]).astype(h_ref.dtype)
        ang = p_ref[...].astype(F32) * inv_ref[...]
        lane = lax.broadcasted_iota(jnp.int32, ang.shape, 1)
        cos_ref[...] = jnp.tile(jnp.cos(ang), (1, RET_QK // 128))
        sin_ref[...] = jnp.tile(jnp.where((lane % RET_DK) < RET_DK // 2, -jnp.sin(ang), jnp.sin(ang)),
                                (1, RET_QK // 128))

        @pl.when(step == mid)
        def _():
            for j, chip in enumerate(chips):
                copy(1 + j, (*chip, cc), me).wait_recv()
                passed[j].start()

        @pl.when(step == nt - 1)
        def _():
            copy(0, sibling, me).wait_recv()
            for j, chip in enumerate(chips):
                copy(4 + j, (*chip, 1 - cc), me).wait_recv()
            for cp in first + passed:
                cp.wait_send()
            mine.wait()

    whole = lambda a: pl.BlockSpec(a.shape, lambda i: (0,) * a.ndim)
    rows = lambda w: pl.BlockSpec((tr, w), lambda i: (i, 0))
    return pl.pallas_call(
        body, name="prologue_allgather_w_in", grid=(nt,),
        in_specs=[whole(w_in_shard)] + [whole(r) for r in row_shards] + [rows(d), whole(g), rows(1), whole(inv_row)],
        out_specs=(pl.BlockSpec(memory_space=pl.ANY), *[whole(r) for r in row_shards], rows(d), rows(RET_QK),
                   rows(RET_QK)),
        out_shape=(jax.ShapeDtypeStruct((N_DEV,) + w_in_shard.shape, BF16),
                   *[jax.ShapeDtypeStruct(r.shape, BF16) for r in row_shards],
                   jax.ShapeDtypeStruct((l, d), BF16), jax.ShapeDtypeStruct((l, RET_QK), F32),
                   jax.ShapeDtypeStruct((l, RET_QK), F32)),
        scratch_shapes=[pltpu.VMEM(w_in_shard.shape, BF16), pltpu.SemaphoreType.DMA((N_DEV - 1,)),
                        pltpu.SemaphoreType.DMA((N_DEV - 1,)), pltpu.SemaphoreType.DMA],
        compiler_params=_cp("arbitrary"),
    )(w_in_shard, *row_shards, x, g, pos_col, inv_row)


def _allreduce_small(small):
    rows = SMALL_ROWS // N_DEV

    def body(x_ref, out_ref, land, send1, recv1, send2, recv2):
        me, peers = _me_and_peers()
        block = lambda j: pl.ds(pl.multiple_of(j * rows, 8), rows)

        def phase(src_of, dst_of, send_sems, recv_sems):
            sends = [pltpu.make_async_remote_copy(src_ref=src_of(pidx), dst_ref=dst_of(me), send_sem=send_sems.at[k],
                                                  recv_sem=recv_sems.at[k], device_id=peer, device_id_type=MESH)
                     for k, (peer, pidx) in enumerate(peers)]
            recvs = [pltpu.make_async_remote_copy(src_ref=src_of(me), dst_ref=dst_of(pidx), send_sem=send_sems.at[k],
                                                  recv_sem=recv_sems.at[k], device_id=peer, device_id_type=MESH)
                     for k, (peer, pidx) in enumerate(peers)]
            for cp in sends:
                cp.start()
            for cp in recvs:
                cp.wait_recv()
            for cp in sends:
                cp.wait_send()

        land[me] = x_ref[block(me), :]
        phase(lambda j: x_ref.at[block(j), :], lambda j: land.at[j], send1, recv1)
        total = land[0]
        for j in range(1, N_DEV):
            total = total + land[j]
        out_ref[block(me), :] = total
        phase(lambda j: out_ref.at[block(me), :], lambda j: out_ref.at[block(j), :], send2, recv2)

    vm = pl.BlockSpec(memory_space=pltpu.VMEM)
    return pl.pallas_call(
        body, name="allreduce_small", in_specs=[vm], out_specs=vm, out_shape=jax.ShapeDtypeStruct(small.shape, F32),
        scratch_shapes=[pltpu.VMEM((N_DEV, rows, D_MODEL), F32)] + [pltpu.SemaphoreType.DMA((N_DEV - 1,))] * 4,
    )(small)


def _adamw(name, got, w, m, v):
    r, c = w.shape
    n_slots = got.shape[0]
    tr = _pick(r, (256, 128, 64))

    def body(got_ref, w_ref, m_ref, v_ref, g_ref, d_ref, nm_ref, nv_ref):
        g = got_ref[0].astype(F32)
        for j in range(1, n_slots):
            g = g + got_ref[j].astype(F32)
        nm = ADAM_B1 * m_ref[...] + (1.0 - ADAM_B1) * g
        nv = ADAM_B2 * v_ref[...] + (1.0 - ADAM_B2) * jnp.square(g)
        m_hat = nm / (1.0 - ADAM_B1 ** ADAM_STEP)
        v_hat = nv / (1.0 - ADAM_B2 ** ADAM_STEP)
        g_ref[...] = g
        d_ref[...] = -ADAM_LR * (m_hat / (jnp.sqrt(v_hat) + ADAM_EPS) + ADAM_WD * w_ref[...])
        nm_ref[...] = nm
        nv_ref[...] = nv

    blk = pl.BlockSpec((tr, c), lambda i: (i, 0))
    out = jax.ShapeDtypeStruct((r, c), F32)
    return pl.pallas_call(
        body, name=name, grid=(r // tr,),
        in_specs=[pl.BlockSpec((n_slots, tr, c), lambda i: (0, i, 0)), blk, blk, blk],
        out_specs=(blk, blk, blk, blk), out_shape=(out, out, out, out), compiler_params=_cp("parallel"),
    )(got, w, m, v)


_SMALL_VECS = ("norm1_g", "ret_gn_g", "s5_d", "s5_glu_b", "norm2_g", "norm_mem_g", "norm_f_g")


def _small_layout():
    lay, row = {}, 0
    for n in _SMALL_VECS + ("loss",):
        lay[n] = (row, 1, D_MODEL)
        row += 1
    for n in ("s5_a_re", "s5_a_im"):
        lay[n] = (row, 4, D_MODEL)
        row += 4
    lay["s5_log_dt"] = (row, 1, S5_G)
    row += 8
    for n in ("s5_b_re", "s5_b_im", "s5_c_re", "s5_c_im"):
        lay[n] = (row, 64, D_MODEL)
        row += 64
    assert row <= SMALL_ROWS
    return lay


def _pack_small(t, loss_row=None):
    lay = _small_layout()
    pieces = [t[n].reshape(1, D_MODEL) for n in _SMALL_VECS]
    pieces.append(jnp.zeros((1, D_MODEL), F32) if loss_row is None else loss_row)
    pieces += [t["s5_a_re"].reshape(4, D_MODEL), t["s5_a_im"].reshape(4, D_MODEL)]
    pieces.append(jnp.pad(t["s5_log_dt"].reshape(1, S5_G), ((0, 7), (0, D_MODEL - S5_G))))
    pieces += [t[n].reshape(64, D_MODEL) for n in ("s5_b_re", "s5_b_im", "s5_c_re", "s5_c_im")]
    pieces.append(jnp.zeros((SMALL_ROWS - lay["s5_c_im"][0] - 64, D_MODEL), F32))
    return jnp.concatenate(pieces, axis=0)


def _adamw_small(g_sum, w, m, v):
    lay = _small_layout()
    names = [n for n in lay if n != "loss"]

    def body(g_ref, w_ref, m_ref, v_ref, *outs):
        g = g_ref[...]
        nm = ADAM_B1 * m_ref[...] + (1.0 - ADAM_B1) * g
        nv = ADAM_B2 * v_ref[...] + (1.0 - ADAM_B2) * jnp.square(g)
        m_hat = nm / (1.0 - ADAM_B1 ** ADAM_STEP)
        v_hat = nv / (1.0 - ADAM_B2 ** ADAM_STEP)
        delta = -ADAM_LR * (m_hat / (jnp.sqrt(v_hat) + ADAM_EPS) + ADAM_WD * w_ref[...])
        for i, n in enumerate(names):
            r0, rows, lanes = lay[n]
            for part, val in enumerate((g, delta, nm, nv)):
                outs[4 * i + part][...] = val[r0:r0 + rows, 0:lanes]
        r0 = lay["loss"][0]
        outs[-1][...] = g[r0:r0 + 1, :]

    shapes = []
    for n in names:
        shapes += [jax.ShapeDtypeStruct(lay[n][1:], F32)] * 4
    shapes.append(jax.ShapeDtypeStruct((1, D_MODEL), F32))
    outs = pl.pallas_call(body, name="adamw_small", out_shape=tuple(shapes),
                          compiler_params=pltpu.CompilerParams(vmem_limit_bytes=VMEM_LIMIT))(g_sum, w, m, v)
    return {n: tuple(outs[4 * i:4 * i + 4]) for i, n in enumerate(names)}, outs[-1]


_W_NAMES = ("norm1_g", "w_in", "ret_gn_g", "s5_a_re", "s5_a_im", "s5_log_dt", "s5_b_re", "s5_b_im", "s5_c_re", "s5_c_im",
            "s5_d", "s5_glu_w", "s5_glu_b", "w_out", "norm2_g", "norm_mem_g", "xa_wq", "xa_wk", "xa_wv", "xa_wo",
            "norm_f_g")
_ROW_NAMES = ("s5_glu_w", "w_out", "xa_wq", "xa_wk", "xa_wv", "xa_wo")


def kernel(x, mem, positions, norm1_g, w_in, ret_gn_g, s5_a_re, s5_a_im, s5_log_dt, s5_b_re, s5_b_im, s5_c_re, s5_c_im, s5_d, s5_glu_w, s5_glu_b, w_out, norm2_g, norm_mem_g, xa_wq, xa_wk, xa_wv, xa_wo, norm_f_g, loss_target, m_norm1_g, m_w_in, m_ret_gn_g, m_s5_a_re, m_s5_a_im, m_s5_log_dt, m_s5_b_re, m_s5_b_im, m_s5_c_re, m_s5_c_im, m_s5_d, m_s5_glu_w, m_s5_glu_b, m_w_out, m_norm2_g, m_norm_mem_g, m_xa_wq, m_xa_wk, m_xa_wv, m_xa_wo, m_norm_f_g, v_norm1_g, v_w_in, v_ret_gn_g, v_s5_a_re, v_s5_a_im, v_s5_log_dt, v_s5_b_re, v_s5_b_im, v_s5_c_re, v_s5_c_im, v_s5_d, v_s5_glu_w, v_s5_glu_b, v_w_out, v_norm2_g, v_norm_mem_g, v_xa_wq, v_xa_wk, v_xa_wv, v_xa_wo, v_norm_f_g):
    w = dict(norm1_g=norm1_g, w_in=w_in, ret_gn_g=ret_gn_g, s5_a_re=s5_a_re, s5_a_im=s5_a_im, s5_log_dt=s5_log_dt,
             s5_b_re=s5_b_re, s5_b_im=s5_b_im, s5_c_re=s5_c_re, s5_c_im=s5_c_im, s5_d=s5_d, s5_glu_w=s5_glu_w,
             s5_glu_b=s5_glu_b, w_out=w_out, norm2_g=norm2_g, norm_mem_g=norm_mem_g, xa_wq=xa_wq, xa_wk=xa_wk,
             xa_wv=xa_wv, xa_wo=xa_wo, norm_f_g=norm_f_g)
    mom = dict(norm1_g=m_norm1_g, w_in=m_w_in, ret_gn_g=m_ret_gn_g, s5_a_re=m_s5_a_re, s5_a_im=m_s5_a_im,
               s5_log_dt=m_s5_log_dt, s5_b_re=m_s5_b_re, s5_b_im=m_s5_b_im, s5_c_re=m_s5_c_re, s5_c_im=m_s5_c_im,
               s5_d=m_s5_d, s5_glu_w=m_s5_glu_w, s5_glu_b=m_s5_glu_b, w_out=m_w_out, norm2_g=m_norm2_g,
               norm_mem_g=m_norm_mem_g, xa_wq=m_xa_wq, xa_wk=m_xa_wk, xa_wv=m_xa_wv, xa_wo=m_xa_wo,
               norm_f_g=m_norm_f_g)
    var = dict(norm1_g=v_norm1_g, w_in=v_w_in, ret_gn_g=v_ret_gn_g, s5_a_re=v_s5_a_re, s5_a_im=v_s5_a_im,
               s5_log_dt=v_s5_log_dt, s5_b_re=v_s5_b_re, s5_b_im=v_s5_b_im, s5_c_re=v_s5_c_re, s5_c_im=v_s5_c_im,
               s5_d=v_s5_d, s5_glu_w=v_s5_glu_w, s5_glu_b=v_s5_glu_b, w_out=v_w_out, norm2_g=v_norm2_g,
               norm_mem_g=v_norm_mem_g, xa_wq=v_xa_wq, xa_wk=v_xa_wk, xa_wv=v_xa_wv, xa_wo=v_xa_wo,
               norm_f_g=v_norm_f_g)
    shapes = {n: w[n].shape for n in _W_NAMES}

    x2d, mem2d, tgt = x[0], mem[0], loss_target[0]
    l = x2d.shape[0]
    ret_c = _pick(l, (256, 128))
    s5_t = _pick(l, (256, 128))
    g1, g2, gm, gf = norm1_g, norm2_g, norm_mem_g, norm_f_g.reshape(1, D_MODEL)

    half = RET_DK // 2
    inv = ROPE_BASE ** (-jnp.arange(half, dtype=F32) / half)
    win_s, *rest = _prologue(w_in[0], [w[n][0] for n in _ROW_NAMES], x2d, g1, positions[0].reshape(l, 1),
                             jnp.tile(inv, 128 // half)[None, :])
    row_shards_b, (h1, cos_t, sin_t) = rest[:len(_ROW_NAMES)], rest[len(_ROW_NAMES):]

    to_gpn = lambda b: jnp.transpose(b, (0, 2, 1)).reshape(S5_G, S5_P * S5_N)
    from_gpn = lambda b: jnp.transpose(b.reshape(S5_G, S5_P, S5_N), (0, 2, 1))
    disc_args = (s5_a_re[0], s5_a_im[0], s5_log_dt[0].reshape(S5_G, 1), to_gpn(s5_b_re[0]), to_gpn(s5_b_im[0]))
    abar_re, abar_im, bb_re_t, bb_im_t = _s5_discretize(*disc_args)
    bbm, ccm = _s5_block_mats(from_gpn(bb_re_t), from_gpn(bb_im_t), s5_c_re[0], s5_c_im[0])
    a_z = _s5_z(abar_re, abar_im)

    proj, *rows_01 = _mm_nn_slots("in_proj", h1, win_s, BF16, side=_gather_job(row_shards_b[:2]))
    full = {n: g.reshape(N_DEV * r, D_MODEL) for n, g, r in zip(_ROW_NAMES[:2], rows_01, ROW_SHARDS[:2])}
    rconsts = _ret_constants(ret_c)
    ret, o_saved, r_prev, q_rot, k_rot = _ret_fwd(proj, cos_t, sin_t, rconsts, ret_gn_g, ret_c)
    mix, xstart, y1, z_glu, *rows_xa = _s5_fwd(proj, ret, bbm, ccm, s5_d, full["s5_glu_w"], s5_glu_b, a_z, s5_t,
                                        side=_gather_job(row_shards_b[2:]))
    full.update({n: g.reshape(N_DEV * r, D_MODEL) for n, g, r in zip(_ROW_NAMES[2:], rows_xa, ROW_SHARDS[2:])})
    x1, h2 = _mm_nn("out_proj", mix, full["w_out"], F32, residual=x2d, epi=_epi_norm_fwd(g2))
    mn = _rms_fwd("norm_mem_fwd", mem2d, gm)
    qa = _mm_nn("xa_q", h2, full["xa_wq"], BF16)
    ka = _mm_nn("xa_k", mn, full["xa_wk"], BF16)
    va = _mm_nn("xa_v", mn, full["xa_wv"], BF16)
    oa = _attn_fwd(qa, ka, va)
    dx2, dgf, loss_lanes = _mm_nn("xa_o", oa, full["xa_wo"], F32, residual=x1, epi=_epi_loss(gf, tgt))

    doa = _mm_nt("xa_o_dx", dx2, full["xa_wo"], BF16)
    dwo = _mm_tn("xa_o_dw", oa, dx2, BF16)
    dqa, dka, dva = _attn_bwd(qa, ka, va, doa)
    dx1, dg2 = _mm_nt("xa_q_dx", dqa, full["xa_wq"], F32, epi=_epi_norm_bwd(x1, g2, dx2))
    dwq = _mm_tn("xa_q_dw", h2, dqa, BF16)
    dwk = _mm_tn("xa_k_dw", mn, dka, BF16)
    dwv = _mm_tn("xa_v_dw", mn, dva, BF16)
    dmn = _mm_nt("xa_v_dx", dva, full["xa_wv"], F32, residual=_mm_nt("xa_k_dx", dka, full["xa_wk"], F32))
    _, dgm = _rms_bwd("norm_mem_bwd", mem2d, gm, dmn, None)
    dmix = _mm_nt("out_proj_dx", dx1, full["w_out"], BF16)
    dwout = _mm_tn("out_proj_dw", mix, dx1, BF16)
    dret, dgn, *got_a = _ret_bwd(proj, q_rot, k_rot, cos_t, sin_t, rconsts, ret_gn_g, o_saved, r_prev, dmix, ret_c,
                                 side=_scatter_job([dwout, dwq, dwk, dwv, dwo]))
    dproj, y2, dz, dbbm, dccm, dabar, dd, dgb = _s5_bwd(proj, dmix, dret, xstart, y1, z_glu, bbm, ccm, s5_d,
                                                        full["s5_glu_w"], a_z, s5_t)
    dglu = _mm_tn("s5_glu_dw", y2, dz, BF16)
    dwin_s, got_glu = _mm_tn_slots("in_proj_dw", h1, dproj, N_DEV, BF16, side=_scatter_job([dglu]))
    grad_x, dg1, got_win = _mm_nt_slots("in_proj_dx", dproj, win_s, F32, side=_scatter_job([dwin_s]),
                                        epi=_epi_norm_bwd(x2d, g1, dx1))

    dab_re, dab_im = _s5_unz(dabar)
    dbb_re, dbb_im = _s5_block_diag_bb(dbbm)
    dc_re, dc_im = _s5_block_diag_cc(dccm)
    da_re, da_im, dlog_dt, db_re_t, db_im_t = _s5_discretize_bwd(*disc_args, dab_re, dab_im, to_gpn(dbb_re),
                                                                 to_gpn(dbb_im))
    db_re, db_im = from_gpn(db_re_t), from_gpn(db_im_t)
    small_g = dict(norm1_g=dg1, ret_gn_g=dgn, s5_d=dd, s5_glu_b=dgb, norm2_g=dg2, norm_mem_g=dgm, norm_f_g=dgf,
                   s5_a_re=da_re, s5_a_im=da_im, s5_log_dt=dlog_dt, s5_b_re=db_re, s5_b_im=db_im, s5_c_re=dc_re,
                   s5_c_im=dc_im)
    small_pack = _pack_small(small_g, loss_row=loss_lanes)

    res = {}
    got = dict(zip(("w_out", "xa_wq", "xa_wk", "xa_wv", "xa_wo"), got_a), w_in=got_win, s5_glu_w=got_glu)
    for n in ("w_in",) + _ROW_NAMES:
        res[n] = _adamw("adamw_" + n, got[n], w[n][0], mom[n][0], var[n][0])
    small_sum = _allreduce_small(small_pack)
    small_res, loss_sum = _adamw_small(small_sum, _pack_small(w), _pack_small(mom), _pack_small(var))
    loss = (0.5 / D_MODEL) * jnp.sum(loss_sum)
    res.update(small_res)

    outs = [loss, grad_x[None]]
    for part in range(4):
        for n in _W_NAMES:
            outs.append(res[n][part].reshape(shapes[n]))
    return tuple(outs)
```
